```python
import math
import jax, jax.numpy as jnp
from jax import lax
import numpy as np

D_MODEL = 1024
BATCH = 32
SEQ = 256
DEPTH = 4
DEC_BATCH = 4
DEC_SEQ = 2048
PAST_LEN = 256

GRID_W = 64
N_MIXERS = 3
LAYER_MIXER = tuple(i % N_MIXERS for i in range(DEPTH))
N_MLA_LAYERS = LAYER_MIXER.count(0)
N_RET_LAYERS = LAYER_MIXER.count(1)
N_FNET_LAYERS = LAYER_MIXER.count(2)

MLA_HEADS = 8
MLA_NOPE = 128
MLA_ROPE = 64
MLA_V = 128
MLA_Q_LORA = 384
MLA_KV_LORA = 256
ROPE_BASE = 10000.0
Q_BLOCK = 128
RET_HEADS = 4
RET_DK = D_MODEL // RET_HEADS
RET_DV = 2 * D_MODEL // RET_HEADS
RET_CHUNK = 128
FNET_GROUPS = 4
FNET_GROUP_DIM = D_MODEL // FNET_GROUPS
N_EXPERTS = 16
N_EXPERT_GROUPS = 4
EXPERTS_PER_GROUP = N_EXPERTS // N_EXPERT_GROUPS
TOP_K = 2
D_EXPERT = 512
MOE_BLOCK = 128
NORM_EPS = 1e-6

kernel_name = 'hybrid_mla_retention_fnet_moe_diffusion_step'


def rmsnorm(x, g=None):
    xf = x.astype(jnp.float32)
    y = xf * lax.rsqrt(jnp.mean(xf * xf, axis=-1, keepdims=True) + NORM_EPS)
    if g is not None:
        y = y * g.astype(jnp.float32)
    return y.astype(x.dtype)


def modulation(cond, w, b):
    m = jax.nn.silu(cond) @ w + b
    return jnp.split(m, 6, axis=-1)


def modulate(x, g, shift, scale):
    return rmsnorm(x, g) * (1 + scale[:, None, :]) + shift[:, None, :]


def rotate(x, cos, sin):
    half = x.shape[-1] // 2
    x1, x2 = x[..., :half], x[..., half:]
    return jnp.concatenate([x1 * cos - x2 * sin, x1 * sin + x2 * cos], axis=-1)


def axial_angles(n_tokens):
    n_rows = n_tokens // GRID_W
    row = jnp.repeat(jnp.arange(n_rows), GRID_W).astype(jnp.float32)
    col = jnp.tile(jnp.arange(GRID_W), n_rows).astype(jnp.float32)
    axis_dim = MLA_ROPE // 2
    inv = ROPE_BASE ** (-jnp.arange(0, axis_dim, 2, dtype=jnp.float32) / axis_dim)
    return row[:, None] * inv[None, :], col[:, None] * inv[None, :]


def apply_axial_rope(x, ang_r, ang_c):
    axis_dim = MLA_ROPE // 2
    shape = ang_r.shape[:1] + (1,) * (x.ndim - 3) + ang_r.shape[1:]
    def rot(part, ang):
        ang = ang.reshape(shape)
        return rotate(part, jnp.cos(ang).astype(x.dtype), jnp.sin(ang).astype(x.dtype))
    return jnp.concatenate([rot(x[..., :axis_dim], ang_r), rot(x[..., axis_dim:], ang_c)], axis=-1)


def block_attention(q, k, v):
    b, tq, h, dq = q.shape
    nb = tq // Q_BLOCK
    qb = q.reshape(b, nb, Q_BLOCK, h, dq).transpose(1, 0, 2, 3, 4)
    scale = dq ** -0.5
    def one(qblk):
        s = jnp.einsum('bqhd,bkhd->bhqk', qblk, k, preferred_element_type=jnp.float32) * scale
        p = jax.nn.softmax(s, axis=-1).astype(v.dtype)
        return jnp.einsum('bhqk,bkhd->bqhd', p, v)
    o = lax.map(one, qb)
    return o.transpose(1, 0, 2, 3, 4).reshape(b, tq, h, v.shape[-1])


def mla_project(h, w_in, q_norm_g, kv_norm_g, w_uq):
    b, t, _ = h.shape
    z = h @ w_in
    cq, ckv, kpe = jnp.split(z, [MLA_Q_LORA, MLA_Q_LORA + MLA_KV_LORA], axis=-1)
    q = (rmsnorm(cq, q_norm_g) @ w_uq).reshape(b, t, MLA_HEADS, MLA_NOPE + MLA_ROPE)
    return q, rmsnorm(ckv, kv_norm_g), kpe


def mla_keys_values(ckv, kpe, w_ukv):
    b, t, _ = ckv.shape
    kv = (ckv @ w_ukv).reshape(b, t, MLA_HEADS, MLA_NOPE + MLA_V)
    k_pe = jnp.broadcast_to(kpe[:, :, None, :], (b, t, MLA_HEADS, MLA_ROPE))
    return jnp.concatenate([kv[..., :MLA_NOPE], k_pe], axis=-1), kv[..., MLA_NOPE:]


def mla_context(h, w_in, q_norm_g, kv_norm_g, w_uq, w_ukv, w_o):
    b, t, _ = h.shape
    q, ckv, kpe = mla_project(h, w_in, q_norm_g, kv_norm_g, w_uq)
    k, v = mla_keys_values(ckv, kpe, w_ukv)
    o = block_attention(q, k, v).reshape(b, t, MLA_HEADS * MLA_V)
    return o @ w_o, jnp.concatenate([ckv, kpe], axis=-1)


def mla_latent(h, ctx_cache, w_in, q_norm_g, kv_norm_g, w_uq, w_ukv, w_o):
    b, t, _ = h.shape
    q, ckv, kpe = mla_project(h, w_in, q_norm_g, kv_norm_g, w_uq)
    ang_r, ang_c = axial_angles(t)
    q = jnp.concatenate([q[..., :MLA_NOPE], apply_axial_rope(q[..., MLA_NOPE:], ang_r, ang_c)], axis=-1)
    kpe = apply_axial_rope(kpe, ang_r, ang_c)
    k_lat, v_lat = mla_keys_values(ckv, kpe, w_ukv)
    k_ctx, v_ctx = mla_keys_values(ctx_cache[..., :MLA_KV_LORA], ctx_cache[..., MLA_KV_LORA:], w_ukv)
    k = jnp.concatenate([k_lat, k_ctx], axis=1)
    v = jnp.concatenate([v_lat, v_ctx], axis=1)
    o = block_attention(q, k, v).reshape(b, t, MLA_HEADS * MLA_V)
    return o @ w_o


def retention_scan(q, k, v, log_g, s0):
    b, h, t, _ = q.shape
    nc = t // RET_CHUNK
    def chunks(a):
        return a.reshape(b, h, nc, RET_CHUNK, a.shape[-1]).transpose(2, 0, 1, 3, 4)
    idx = jnp.arange(RET_CHUNK, dtype=jnp.float32)
    rel = idx[:, None] - idx[None, :]
    lower = rel >= 0
    decay_in = jnp.where(lower, jnp.exp(jnp.where(lower, rel, 0.0)[None] * log_g[:, None, None]), 0.0)
    decay_q = jnp.exp((idx + 1)[None, :] * log_g[:, None])[..., None]
    decay_k = jnp.exp((RET_CHUNK - 1 - idx)[None, :] * log_g[:, None])[..., None]
    decay_chunk = jnp.exp(RET_CHUNK * log_g)[:, None, None]
    def step(s, qkv):
        qc, kc, vc = qkv
        scores = jnp.einsum('bhid,bhjd->bhij', qc, kc) * decay_in
        out = jnp.einsum('bhij,bhje->bhie', scores, vc) + jnp.einsum('bhid,bhde->bhie', qc * decay_q, s)
        s = decay_chunk * s + jnp.einsum('bhjd,bhje->bhde', kc * decay_k, vc)
        return s, out
    s_fin, out = lax.scan(step, s0.astype(jnp.float32), (chunks(q), chunks(k), chunks(v)))
    return out.transpose(1, 2, 0, 3, 4).reshape(b, h, t, -1), s_fin


def retention_mixer(h, s0_f, s0_b, w_in, logit_f, logit_b, w_o, positional):
    b, t, _ = h.shape
    qk = RET_HEADS * RET_DK
    vw = RET_HEADS * RET_DV
    z = h @ w_in
    q, k, v, g_f, g_b = jnp.split(z, [qk, 2 * qk, 2 * qk + vw, 2 * qk + 2 * vw], axis=-1)
    def heads(a, d):
        return a.reshape(b, t, RET_HEADS, d).transpose(0, 2, 1, 3).astype(jnp.float32)
    q = heads(q, RET_DK)
    k = heads(k, RET_DK) * (RET_DK ** -0.5)
    v = heads(v, RET_DV)
    if positional:
        inv = ROPE_BASE ** (-jnp.linspace(0.0, 1.0, RET_DK // 2))
        ang = jnp.arange(t, dtype=jnp.float32)[:, None] * inv[None, :]
        cos, sin = jnp.cos(ang), jnp.sin(ang)
        q = rotate(q, cos, sin)
        k = rotate(k, cos, sin)
    o_f, s_f = retention_scan(q, k, v, jax.nn.log_sigmoid(logit_f.astype(jnp.float32)), s0_f)
    o_b, s_b = retention_scan(q[:, :, ::-1], k[:, :, ::-1], v[:, :, ::-1],
                              jax.nn.log_sigmoid(logit_b.astype(jnp.float32)), s0_b)
    o_b = o_b[:, :, ::-1]
    def gated(o, g):
        o = rmsnorm(o).transpose(0, 2, 1, 3).reshape(b, t, vw)
        return jax.nn.silu(g.astype(jnp.float32)) * o
    y = (gated(o_f, g_f) + gated(o_b, g_b)).astype(h.dtype) @ w_o
    return y, s_f, s_b


def fourier_mixer(h, w):
    b, t, _ = h.shape
    hg = h.astype(jnp.float32).reshape(b, t, FNET_GROUPS, FNET_GROUP_DIM)
    f = jnp.fft.fft2(hg, axes=(1, 3), norm='ortho').real
    return f.reshape(b, t, D_MODEL).astype(h.dtype) @ w


def moe_ffn(h, router_w, router_b, w_gate, w_up, w_down):
    b, t, d = h.shape
    x = h.reshape(b * t, d)
    n = x.shape[0]
    scores = jax.nn.sigmoid(jnp.dot(x, router_w, preferred_element_type=jnp.float32))
    grouped = (scores + router_b.astype(jnp.float32)).reshape(n, N_EXPERT_GROUPS, EXPERTS_PER_GROUP)
    group_score = jnp.sum(lax.top_k(grouped, TOP_K)[0], axis=-1)
    best = jnp.argmax(group_score, axis=-1)
    keep = jnp.arange(N_EXPERT_GROUPS)[None, :] == best[:, None]
    masked = jnp.where(keep[:, :, None], grouped, -jnp.inf).reshape(n, N_EXPERTS)
    _, idx = lax.top_k(masked, TOP_K)
    wts = jnp.take_along_axis(scores, idx, axis=-1)
    wts = wts / jnp.sum(wts, axis=-1, keepdims=True)
    a = n * TOP_K
    flat_e = idx.reshape(a)
    order = jnp.argsort(flat_e)
    e_sorted = flat_e[order]
    tok_sorted = (jnp.arange(a, dtype=jnp.int32) // TOP_K)[order]
    w_sorted = wts.reshape(a)[order]
    counts = jnp.bincount(flat_e, length=N_EXPERTS)
    padded = (counts + MOE_BLOCK - 1) // MOE_BLOCK * MOE_BLOCK
    pad_end = jnp.cumsum(padded)
    pad_start = pad_end - padded
    start = jnp.cumsum(counts) - counts
    dest = pad_start[e_sorted] + jnp.arange(a, dtype=jnp.int32) - start[e_sorted]
    n_blocks = -(-a // MOE_BLOCK) + N_EXPERTS
    rows = n_blocks * MOE_BLOCK
    row_tok = jnp.zeros((rows,), jnp.int32).at[dest].set(tok_sorted)
    row_w = jnp.zeros((rows,), jnp.float32).at[dest].set(w_sorted)
    block_e = jnp.minimum(jnp.searchsorted(pad_end, jnp.arange(n_blocks, dtype=jnp.int32) * MOE_BLOCK,
                                           side='right'), N_EXPERTS - 1)
    def expert_block(args):
        e, tok, wt = args
        xb = x[tok]
        hid = jax.nn.silu(xb @ w_gate[e]) * (xb @ w_up[e])
        return (hid @ w_down[e]) * wt[:, None].astype(x.dtype)
    yb = lax.map(expert_block, (block_e, row_tok.reshape(n_blocks, MOE_BLOCK), row_w.reshape(n_blocks, MOE_BLOCK)))
    y = jnp.zeros_like(x).at[row_tok].add(yb.reshape(rows, d))
    return y.reshape(b, t, d)


def setup_inputs(seed: int = 0) -> dict:
    key = jax.random.key(seed)
    ks = iter(jax.random.split(key, 40))
    def nrm(shape, scale=1.0):
        return jax.random.normal(next(ks), shape, jnp.float32) * scale
    D = D_MODEL
    base = 1.0 - jnp.exp(jnp.linspace(math.log(1.0 / 32), math.log(1.0 / 512), RET_HEADS))
    base_logit = jnp.log(base) - jnp.log1p(-base)
    return {
        'x_prompt': nrm((BATCH, SEQ, D)),
        'x_sample': nrm((DEC_BATCH, DEC_SEQ, D)),
        'cache_mla': nrm((DEC_BATCH, N_MLA_LAYERS, PAST_LEN, MLA_KV_LORA + MLA_ROPE)),
        'state_ret': nrm((DEC_BATCH, N_RET_LAYERS, 2, RET_HEADS, RET_DK, RET_DV), 0.5),
        'c': nrm((DEC_BATCH, D)),
        'c_ctx': nrm((D,)),
        'norm1_g': 1.0 + nrm((DEPTH, D), 0.02),
        'norm2_g': 1.0 + nrm((DEPTH, D), 0.02),
        'ada_w': nrm((DEPTH, D, 6 * D), 0.5 * D ** -0.5),
        'ada_b': nrm((DEPTH, 6 * D), 0.01),
        'final_norm_g': 1.0 + nrm((D,), 0.02),
        'mla_w_in': nrm((N_MLA_LAYERS, D, MLA_Q_LORA + MLA_KV_LORA + MLA_ROPE), D ** -0.5),
        'mla_q_norm_g': 1.0 + nrm((N_MLA_LAYERS, MLA_Q_LORA), 0.02),
        'mla_kv_norm_g': 1.0 + nrm((N_MLA_LAYERS, MLA_KV_LORA), 0.02),
        'mla_w_uq': nrm((N_MLA_LAYERS, MLA_Q_LORA, MLA_HEADS * (MLA_NOPE + MLA_ROPE)), MLA_Q_LORA ** -0.5),
        'mla_w_ukv': nrm((N_MLA_LAYERS, MLA_KV_LORA, MLA_HEADS * (MLA_NOPE + MLA_V)), MLA_KV_LORA ** -0.5),
        'mla_w_o': nrm((N_MLA_LAYERS, MLA_HEADS * MLA_V, D), (MLA_HEADS * MLA_V) ** -0.5),
        'ret_w_in': nrm((N_RET_LAYERS, D, 2 * RET_HEADS * RET_DK + 3 * RET_HEADS * RET_DV), D ** -0.5),
        'ret_decay_f': base_logit[None, :] + nrm((N_RET_LAYERS, RET_HEADS), 0.1),
        'ret_decay_b': base_logit[None, :] + nrm((N_RET_LAYERS, RET_HEADS), 0.1),
        'ret_w_o': nrm((N_RET_LAYERS, RET_HEADS * RET_DV, D), (RET_HEADS * RET_DV) ** -0.5),
        'fnet_w': nrm((N_FNET_LAYERS, D, D), D ** -0.5),
        'router_w': nrm((D, N_EXPERTS), D ** -0.5),
        'router_b': nrm((N_EXPERTS,), 0.01),
        'moe_w_gate': nrm((DEPTH, N_EXPERTS, D, D_EXPERT), D ** -0.5),
        'moe_w_up': nrm((DEPTH, N_EXPERTS, D, D_EXPERT), D ** -0.5),
        'moe_w_down': nrm((DEPTH, N_EXPERTS, D_EXPERT, D), D_EXPERT ** -0.5),
    }


def reference(x_prompt, x_sample, cache_mla, state_ret, c, c_ctx, norm1_g, norm2_g, ada_w, ada_b,
              final_norm_g, mla_w_in, mla_q_norm_g, mla_kv_norm_g, mla_w_uq, mla_w_ukv, mla_w_o,
              ret_w_in, ret_decay_f, ret_decay_b, ret_w_o, fnet_w, router_w, router_b,
              moe_w_gate, moe_w_up, moe_w_down):
    xc, xl = x_prompt, x_sample
    cond_ctx = c_ctx[None, :]
    ctx_mla_states, ctx_ret_states = [], []
    counters = [0, 0, 0]
    for layer in range(DEPTH):
        kind = LAYER_MIXER[layer]
        j = counters[kind]
        counters[kind] += 1
        mod_c = modulation(cond_ctx, ada_w[layer], ada_b[layer])
        mod_l = modulation(c, ada_w[layer], ada_b[layer])
        hc = modulate(xc, norm1_g[layer], mod_c[0], mod_c[1])
        hl = modulate(xl, norm1_g[layer], mod_l[0], mod_l[1])
        if kind == 0:
            oc, ctx_kv = mla_context(hc, mla_w_in[j], mla_q_norm_g[j], mla_kv_norm_g[j],
                                     mla_w_uq[j], mla_w_ukv[j], mla_w_o[j])
            ol = mla_latent(hl, cache_mla[:, j], mla_w_in[j], mla_q_norm_g[j], mla_kv_norm_g[j],
                            mla_w_uq[j], mla_w_ukv[j], mla_w_o[j])
            ctx_mla_states.append(ctx_kv)
        elif kind == 1:
            zero = jnp.zeros((xc.shape[0], RET_HEADS, RET_DK, RET_DV), jnp.float32)
            oc, s_f, s_b = retention_mixer(hc, zero, zero, ret_w_in[j], ret_decay_f[j], ret_decay_b[j],
                                           ret_w_o[j], False)
            ol, _, _ = retention_mixer(hl, state_ret[:, j, 0], state_ret[:, j, 1], ret_w_in[j],
                                       ret_decay_f[j], ret_decay_b[j], ret_w_o[j], True)
            ctx_ret_states.append(jnp.stack([s_f, s_b], axis=1).astype(xc.dtype))
        else:
            oc = fourier_mixer(hc, fnet_w[j])
            ol = fourier_mixer(hl, fnet_w[j])
        xc = xc + mod_c[2][:, None, :] * oc
        xl = xl + mod_l[2][:, None, :] * ol
        hc = modulate(xc, norm2_g[layer], mod_c[3], mod_c[4])
        hl = modulate(xl, norm2_g[layer], mod_l[3], mod_l[4])
        xc = xc + mod_c[5][:, None, :] * moe_ffn(hc, router_w, router_b, moe_w_gate[layer],
                                                 moe_w_up[layer], moe_w_down[layer])
        xl = xl + mod_l[5][:, None, :] * moe_ffn(hl, router_w, router_b, moe_w_gate[layer],
                                                 moe_w_up[layer], moe_w_down[layer])
    y_prompt = rmsnorm(xc, final_norm_g)
    y_sample = rmsnorm(xl, final_norm_g)
    new_cache_mla = jnp.stack(ctx_mla_states, axis=1)
    new_state_ret = jnp.stack(ctx_ret_states, axis=1)
    return (y_prompt, y_sample, new_cache_mla, new_state_ret)
```

```python
import functools
import math

import jax
import jax.numpy as jnp
import numpy as np
from jax import lax
from jax.experimental import pallas as pl
from jax.experimental.pallas import tpu as pltpu

F32 = jnp.float32
BF16 = jnp.bfloat16

GRID_W = 64
MLA_HEADS = 8
MLA_NOPE = 128
MLA_ROPE = 64
MLA_V = 128
MLA_Q_LORA = 384
MLA_KV_LORA = 256
ROPE_BASE = 10000.0
RET_HEADS = 4
RET_CHUNK = 128
FNET_GROUPS = 4
N_EXPERTS = 16
N_EXPERT_GROUPS = 4
EXPERTS_PER_GROUP = 4
D_EXPERT = 512
NORM_EPS = 1e-6

LANE = 128
ROW_TILE = 256
MOE_ROWS = 256
VMEM_LIMIT = 56 * 1024 * 1024


def _cparams(*sem):
    return pltpu.CompilerParams(dimension_semantics=sem, vmem_limit_bytes=VMEM_LIMIT)


def _sigmoid(x):
    return 1.0 / (1.0 + jnp.exp(-x))


def _rms(x):
    return x * lax.rsqrt(jnp.mean(x * x, axis=-1, keepdims=True) + NORM_EPS)


def _modulate(x, g, shift, scale):
    return (_rms(x) * g) * (1.0 + scale) + shift


def _dot(a, b):
    return jnp.dot(a, b, preferred_element_type=F32)


def _dot_nt(a, b):
    return lax.dot_general(a, b, (((1,), (1,)), ((), ())), preferred_element_type=F32)


def _mod_kernel(c_ref, w_ref, b_ref, o_ref):
    c = c_ref[...]
    s = (c * _sigmoid(c)).astype(BF16)
    o_ref[...] = _dot(s, w_ref[...].astype(BF16)) + b_ref[...]


def _modulation_all(cond8, ada_w, ada_b):
    depth, d, d6 = ada_w.shape
    tn = d6 // 4
    return pl.pallas_call(
        _mod_kernel,
        grid=(depth, d6 // tn),
        in_specs=[
            pl.BlockSpec((8, d), lambda l, n: (0, 0)),
            pl.BlockSpec((None, d, tn), lambda l, n: (l, 0, n)),
            pl.BlockSpec((None, 1, tn), lambda l, n: (l, 0, n)),
        ],
        out_specs=pl.BlockSpec((None, 8, tn), lambda l, n: (l, 0, n)),
        out_shape=jax.ShapeDtypeStruct((depth, 8, d6), F32),
        compiler_params=_cparams("parallel", "parallel"),
        name="modulation",
    )(cond8, ada_w, ada_b.reshape(depth, 1, d6))


class _Seg:
    def __init__(self, row0, batch, seq, mod0, per_batch_mod):
        self.row0, self.batch, self.seq = row0, batch, seq
        self.mod0, self.per_batch_mod = mod0, per_batch_mod
        self.tiles = batch * seq // ROW_TILE
        self.tile0 = row0 // ROW_TILE
        self.tiles_per_seq = seq // ROW_TILE

    def mod_row(self, tile):
        if self.per_batch_mod:
            return self.mod0 + tile // self.tiles_per_seq
        return self.mod0


def _mod_spec(layer, seg, d, tile_of=lambda *a: a[0]):
    return pl.BlockSpec((None, None, 6, 1, d), lambda *a: (layer, seg.mod_row(tile_of(*a)), 0, 0, 0))


def _rope_partner(x):
    lane = lax.broadcasted_iota(jnp.int32, x.shape, 1)
    first = (lane % 32) < 16
    return jnp.where(first, pltpu.roll(x, LANE - 16, 1), pltpu.roll(x, 16, 1))


def _mla_proj_kernel(*refs, rope, scale):
    if rope:
        (x_ref, m_ref, g_ref, win_ref, qg_ref, kvg_ref, wq_ref, wkn_ref, wv_ref, cos_ref, sin_ref,
         q_ref, k_ref, v_ref) = refs
    else:
        (x_ref, m_ref, g_ref, win_ref, qg_ref, kvg_ref, wq_ref, wkn_ref, wv_ref,
         q_ref, k_ref, v_ref, cache_ref) = refs
    h = _modulate(x_ref[...], g_ref[...], m_ref[0], m_ref[1]).astype(BF16)
    z = _dot(h, win_ref[...])
    cq = z[:, :MLA_Q_LORA]
    ckv = z[:, MLA_Q_LORA:MLA_Q_LORA + MLA_KV_LORA]
    kpe = z[:, MLA_Q_LORA + MLA_KV_LORA:]
    cqn = (_rms(cq) * qg_ref[...]).astype(BF16)
    ckvn = _rms(ckv) * kvg_ref[...]
    ckvb = ckvn.astype(BF16)
    q = _dot(cqn, wq_ref[...]) * scale
    kn = _dot(ckvb, wkn_ref[...])
    v_ref[...] = _dot(ckvb, wv_ref[...]).astype(BF16)
    if rope:
        cos, sin = cos_ref[...], sin_ref[...]
        kpe = kpe * cos + _rope_partner(kpe) * sin
    else:
        cache_ref[:, :MLA_KV_LORA] = ckvn
        cache_ref[:, MLA_KV_LORA:] = kpe[:, :MLA_ROPE]
    kpe_b = kpe.astype(BF16)
    for hd in range(MLA_HEADS):
        lo = hd * 2 * LANE
        q_ref[:, lo:lo + LANE] = q[:, lo:lo + LANE].astype(BF16)
        qr = q[:, lo + LANE:lo + 2 * LANE]
        if rope:
            qr = qr * cos + _rope_partner(qr) * sin
        q_ref[:, lo + LANE:lo + 2 * LANE] = qr.astype(BF16)
        k_ref[:, lo:lo + LANE] = kn[:, hd * LANE:(hd + 1) * LANE].astype(BF16)
        k_ref[:, lo + LANE:lo + 2 * LANE] = kpe_b


def _mla_proj(x, mods, layer, seg, g1, w, rope_tabs):
    n, d = x.shape
    rope = rope_tabs is not None
    rows = seg.batch * seg.seq
    hq = MLA_HEADS * 2 * LANE
    hv = MLA_HEADS * MLA_V
    const = lambda i: (0, 0)
    in_specs = [
        pl.BlockSpec((ROW_TILE, d), lambda i: (seg.tile0 + i, 0)),
        _mod_spec(layer, seg, d),
        pl.BlockSpec((1, d), const),
        pl.BlockSpec(w["w_in"].shape, const),
        pl.BlockSpec((1, MLA_Q_LORA), const),
        pl.BlockSpec((1, MLA_KV_LORA), const),
        pl.BlockSpec(w["w_q"].shape, const),
        pl.BlockSpec(w["w_kn"].shape, const),
        pl.BlockSpec(w["w_v"].shape, const),
    ]
    args = [x, mods, g1, w["w_in"], w["q_g"], w["kv_g"], w["w_q"], w["w_kn"], w["w_v"]]
    out_specs = [
        pl.BlockSpec((ROW_TILE, hq), lambda i: (i, 0)),
        pl.BlockSpec((ROW_TILE, hq), lambda i: (i, 0)),
        pl.BlockSpec((ROW_TILE, hv), lambda i: (i, 0)),
    ]
    out_shape = [
        jax.ShapeDtypeStruct((rows, hq), BF16),
        jax.ShapeDtypeStruct((rows, hq), BF16),
        jax.ShapeDtypeStruct((rows, hv), BF16),
    ]
    if rope:
        tab = pl.BlockSpec((ROW_TILE, LANE), lambda i: (i % seg.tiles_per_seq, 0))
        in_specs += [tab, tab]
        args += list(rope_tabs)
    else:
        cw = MLA_KV_LORA + MLA_ROPE
        out_specs.append(pl.BlockSpec((ROW_TILE, cw), lambda i: (i, 0)))
        out_shape.append(jax.ShapeDtypeStruct((rows, cw), F32))
    return pl.pallas_call(
        functools.partial(_mla_proj_kernel, rope=rope, scale=(MLA_NOPE + MLA_ROPE) ** -0.5),
        grid=(seg.tiles,),
        in_specs=in_specs,
        out_specs=out_specs,
        out_shape=out_shape,
        compiler_params=_cparams("parallel"),
        name="mla_proj_lat" if rope else "mla_proj_ctx",
    )(*args)


def _cache_kv_kernel(c_ref, wkn_ref, wv_ref, k_ref, v_ref):
    c = c_ref[...]
    ckv = c[:, :MLA_KV_LORA].astype(BF16)
    kpe_b = c[:, MLA_KV_LORA:].astype(BF16)
    kn = _dot(ckv, wkn_ref[...])
    v_ref[...] = _dot(ckv, wv_ref[...]).astype(BF16)
    for hd in range(MLA_HEADS):
        lo = hd * 2 * LANE
        k_ref[:, lo:lo + LANE] = kn[:, hd * LANE:(hd + 1) * LANE].astype(BF16)
        k_ref[:, lo + LANE:lo + 2 * LANE] = kpe_b


def _cache_kv(cache_pad, w):
    rows, cw = cache_pad.shape
    hq = MLA_HEADS * 2 * LANE
    hv = MLA_HEADS * MLA_V
    const = lambda i: (0, 0)
    return pl.pallas_call(
        _cache_kv_kernel,
        grid=(rows // ROW_TILE,),
        in_specs=[
            pl.BlockSpec((ROW_TILE, cw), lambda i: (i, 0)),
            pl.BlockSpec(w["w_kn"].shape, const),
            pl.BlockSpec(w["w_v"].shape, const),
        ],
        out_specs=[pl.BlockSpec((ROW_TILE, hq), lambda i: (i, 0)),
                   pl.BlockSpec((ROW_TILE, hv), lambda i: (i, 0))],
        out_shape=[jax.ShapeDtypeStruct((rows, hq), BF16), jax.ShapeDtypeStruct((rows, hv), BF16)],
        compiler_params=_cparams("parallel"),
        name="mla_cache_kv",
    )(cache_pad, w["w_kn"], w["w_v"])


def _attn_kernel(*refs, n_parts):
    q_ref = refs[0]
    kv_refs = refs[1:1 + 2 * n_parts]
    wo_ref, x_ref, m_ref, o_ref, acc_ref = refs[1 + 2 * n_parts:]
    for hd in range(MLA_HEADS):
        qh = q_ref[:, hd * 2 * LANE:(hd + 1) * 2 * LANE]
        scores = [_dot_nt(qh, kv_refs[2 * p][:, hd * 2 * LANE:(hd + 1) * 2 * LANE]) for p in range(n_parts)]
        mx = scores[0].max(axis=-1, keepdims=True)
        for s in scores[1:]:
            mx = jnp.maximum(mx, s.max(axis=-1, keepdims=True))
        den = None
        out = None
        for p, s in enumerate(scores):
            e = jnp.exp(s - mx)
            den = e.sum(axis=-1, keepdims=True) if den is None else den + e.sum(axis=-1, keepdims=True)
            pv = _dot(e.astype(BF16), kv_refs[2 * p + 1][:, hd * MLA_V:(hd + 1) * MLA_V])
            out = pv if out is None else out + pv
        acc_ref[:, hd * MLA_V:(hd + 1) * MLA_V] = (out / den).astype(BF16)
    y = _dot(acc_ref[...], wo_ref[...])
    o_ref[...] = x_ref[...] + m_ref[2] * y


def _attention(x, mods, layer, seg, q, kv_parts, w_o):
    n, d = x.shape
    hq = MLA_HEADS * 2 * LANE
    hv = MLA_HEADS * MLA_V
    tps = seg.tiles_per_seq
    in_specs = [pl.BlockSpec((ROW_TILE, hq), lambda b, i: (b * tps + i, 0))]
    args = [q]
    for k, v, rows in kv_parts:
        in_specs += [pl.BlockSpec((rows, hq), lambda b, i: (b, 0)), pl.BlockSpec((rows, hv), lambda b, i: (b, 0))]
        args += [k, v]
    x_spec = pl.BlockSpec((ROW_TILE, d), lambda b, i: (seg.tile0 + b * tps + i, 0))
    in_specs += [
        pl.BlockSpec(w_o.shape, lambda b, i: (0, 0)),
        x_spec,
        _mod_spec(layer, seg, d, tile_of=lambda b, i: b * tps + i),
    ]
    args += [w_o, x, mods]
    return pl.pallas_call(
        functools.partial(_attn_kernel, n_parts=len(kv_parts)),
        grid=(seg.batch, tps),
        in_specs=in_specs,
        out_specs=x_spec,
        out_shape=jax.ShapeDtypeStruct((n, d), F32),
        scratch_shapes=[pltpu.VMEM((ROW_TILE, hv), BF16)],
        input_output_aliases={len(args) - 2: 0},
        compiler_params=_cparams("parallel", "arbitrary"),
        name="mla_attention",
    )(*args)


RET_COL = 1024


def _ret_proj_kernel(*refs, rotary, dk):
    if rotary:
        x_ref, m_ref, g_ref, w_ref, cos_ref, sin_ref, z_ref, h_ref = refs
    else:
        x_ref, m_ref, g_ref, w_ref, z_ref, h_ref = refs
    j = pl.program_id(1)

    @pl.when(j == 0)
    def _():
        h_ref[...] = _modulate(x_ref[...], g_ref[...], m_ref[0], m_ref[1]).astype(BF16)

    acc = _dot(h_ref[...], w_ref[...])
    qk_blocks = 2 * RET_HEADS * dk // RET_COL
    gate0 = (2 * RET_HEADS * dk + 2 * RET_HEADS * dk) // RET_COL

    if rotary:
        @pl.when(j < qk_blocks)
        def _():
            cos, sin = cos_ref[...], sin_ref[...]
            half = dk // 2
            for hd in range(RET_COL // dk):
                x1 = acc[:, hd * dk:hd * dk + half]
                x2 = acc[:, hd * dk + half:(hd + 1) * dk]
                z_ref[:, hd * dk:hd * dk + half] = (x1 * cos - x2 * sin).astype(BF16)
                z_ref[:, hd * dk + half:(hd + 1) * dk] = (x1 * sin + x2 * cos).astype(BF16)

        @pl.when(jnp.logical_and(j >= qk_blocks, j < gate0))
        def _():
            z_ref[...] = acc.astype(BF16)
    else:
        @pl.when(j < gate0)
        def _():
            z_ref[...] = acc.astype(BF16)

    @pl.when(j >= gate0)
    def _():
        z_ref[...] = (acc * _sigmoid(acc)).astype(BF16)


def _ret_proj(x, mods, layer, seg, g1, w_in, rot_tabs, dk):
    n, d = x.shape
    rotary = rot_tabs is not None
    rows = seg.batch * seg.seq
    tm = 512
    ncol = w_in.shape[1]
    tps = seg.seq // tm if seg.seq >= tm else 1
    in_specs = [
        pl.BlockSpec((tm, d), lambda i, j: (seg.row0 // tm + i, 0)),
        pl.BlockSpec((None, None, 6, 1, d),
                     lambda i, j: (layer, seg.mod0 + (i * tm // seg.seq if seg.per_batch_mod else 0), 0, 0, 0)),
        pl.BlockSpec((1, d), lambda i, j: (0, 0)),
        pl.BlockSpec((d, RET_COL), lambda i, j: (0, j)),
    ]
    args = [x, mods, g1, w_in]
    if rotary:
        tab = pl.BlockSpec((tm, dk // 2), lambda i, j: (i % tps, 0))
        in_specs += [tab, tab]
        args += list(rot_tabs)
    return pl.pallas_call(
        functools.partial(_ret_proj_kernel, rotary=rotary, dk=dk),
        grid=(rows // tm, ncol // RET_COL),
        in_specs=in_specs,
        out_specs=pl.BlockSpec((tm, RET_COL), lambda i, j: (i, j)),
        out_shape=jax.ShapeDtypeStruct((rows, ncol), BF16),
        scratch_shapes=[pltpu.VMEM((tm, d), BF16)],
        compiler_params=_cparams("parallel", "arbitrary"),
        name="ret_proj_lat" if rotary else "ret_proj_ctx",
    )(*args)


def _log_sigmoid(x):
    return jnp.minimum(x, 0.0) - jnp.log(1.0 + jnp.exp(-jnp.abs(x)))


def _ret_scan_kernel(*refs, has_s0, emit_state, n_chunks):
    refs = list(refs)
    lf_ref, lb_ref, q_ref, k_ref, v_ref, gf_ref, gb_ref = refs[:7]
    pos = 7
    s0_ref = None
    if has_s0:
        s0_ref = refs[pos]
        pos += 1
    y_ref = refs[pos]
    pos += 1
    sout_ref = None
    if emit_state:
        sout_ref = refs[pos]
        pos += 1
    s_ref, yf_ref = refs[pos:]
    c = RET_CHUNK
    ii = lax.broadcasted_iota(jnp.int32, (c, c), 0).astype(F32)
    jj = lax.broadcasted_iota(jnp.int32, (c, c), 1).astype(F32)
    idx = lax.broadcasted_iota(jnp.int32, (c, 1), 0).astype(F32)

    for direction in range(2):
        fwd = direction == 0
        lg = _log_sigmoid((lf_ref if fwd else lb_ref)[...])
        rel = (ii - jj) if fwd else (jj - ii)
        keep = rel >= 0
        decay_in = jnp.where(keep, jnp.exp(jnp.where(keep, rel, 0.0) * lg), 0.0)
        decay_q = jnp.exp(((idx + 1.0) if fwd else (c - idx)) * lg)
        decay_k = jnp.exp(((c - 1.0 - idx) if fwd else idx) * lg)
        decay_c = jnp.exp(c * lg)
        g_ref = gf_ref if fwd else gb_ref
        if has_s0:
            s_ref[...] = s0_ref[direction]
        else:
            s_ref[...] = jnp.zeros_like(s_ref)

        def step(ci, carry, fwd=fwd, decay_in=decay_in, decay_q=decay_q, decay_k=decay_k, decay_c=decay_c,
                 g_ref=g_ref):
            cc = ci if fwd else n_chunks - 1 - ci
            r0 = pl.multiple_of(cc * c, c)
            qc = q_ref[pl.ds(r0, c), :]
            kc = k_ref[pl.ds(r0, c), :]
            vc = v_ref[pl.ds(r0, c), :]
            sc = _dot_nt(qc, kc) * decay_in
            s = s_ref[...]
            out = _dot(sc.astype(BF16), vc) + _dot((qc.astype(F32) * decay_q).astype(BF16), s.astype(BF16))
            kd_t = (kc.astype(F32) * decay_k).T.astype(BF16)
            s_ref[...] = decay_c * s + _dot(kd_t, vc)
            o = _rms(out) * g_ref[pl.ds(r0, c), :].astype(F32)
            if fwd:
                yf_ref[pl.ds(r0, c), :] = o
            else:
                y_ref[pl.ds(r0, c), :] = (yf_ref[pl.ds(r0, c), :] + o).astype(BF16)
            return carry

        lax.fori_loop(0, n_chunks, step, 0)
        if emit_state:
            sout_ref[direction] = s_ref[...]


def _ret_scan(z, seg, logit_f, logit_b, s0, emit_state, dk, dv):
    rows = seg.batch * seg.seq
    t = seg.seq
    hh = RET_HEADS
    qb, vb = hh * dk // dk, hh * dk * 2 // dv
    in_specs = [
        pl.BlockSpec((None, 1, 1), lambda b, h: (h, 0, 0)),
        pl.BlockSpec((None, 1, 1), lambda b, h: (h, 0, 0)),
        pl.BlockSpec((t, dk), lambda b, h: (b, h)),
        pl.BlockSpec((t, dk), lambda b, h: (b, qb + h)),
        pl.BlockSpec((t, dv), lambda b, h: (b, vb + h)),
        pl.BlockSpec((t, dv), lambda b, h: (b, vb + hh + h)),
        pl.BlockSpec((t, dv), lambda b, h: (b, vb + 2 * hh + h)),
    ]
    args = [logit_f.reshape(hh, 1, 1), logit_b.reshape(hh, 1, 1), z, z, z, z, z]
    state_spec = pl.BlockSpec((None, 2, None, dk, dv), lambda b, h: (b, 0, h, 0, 0))
    if s0 is not None:
        in_specs.append(state_spec)
        args.append(s0)
    out_specs = [pl.BlockSpec((t, dv), lambda b, h: (b, h))]
    out_shape = [jax.ShapeDtypeStruct((rows, hh * dv), BF16)]
    if emit_state:
        out_specs.append(state_spec)
        out_shape.append(jax.ShapeDtypeStruct((seg.batch, 2, hh, dk, dv), F32))
    res = pl.pallas_call(
        functools.partial(_ret_scan_kernel, has_s0=s0 is not None, emit_state=emit_state,
                          n_chunks=t // RET_CHUNK),
        grid=(seg.batch, hh),
        in_specs=in_specs,
        out_specs=out_specs,
        out_shape=out_shape,
        scratch_shapes=[pltpu.VMEM((dk, dv), F32), pltpu.VMEM((t, dv), F32)],
        compiler_params=_cparams("parallel", "parallel"),
        name="ret_scan",
    )(*args)
    return res if emit_state else (res[0], None)


def _mm_res_kernel(a_ref, w_ref, x_ref, m_ref, o_ref):
    o_ref[...] = x_ref[...] + m_ref[2] * _dot(a_ref[...], w_ref[...])


def _matmul_residual(x, mods, layer, seg, a, w):
    n, d = x.shape
    x_spec = pl.BlockSpec((ROW_TILE, d), lambda i: (seg.tile0 + i, 0))
    return pl.pallas_call(
        _mm_res_kernel,
        grid=(seg.tiles,),
        in_specs=[
            pl.BlockSpec((ROW_TILE, a.shape[1]), lambda i: (i, 0)),
            pl.BlockSpec(w.shape, lambda i: (0, 0)),
            x_spec,
            _mod_spec(layer, seg, d),
        ],
        out_specs=x_spec,
        out_shape=jax.ShapeDtypeStruct((n, d), F32),
        input_output_aliases={2: 0},
        compiler_params=_cparams("parallel"),
        name="matmul_residual",
    )(a, w, x, mods)


def _fnet_a_kernel(x_ref, m_ref, g_ref, cs_ref, ac_ref, as_ref, *, gd):
    h = _modulate(x_ref[...], g_ref[...], m_ref[0], m_ref[1]).astype(BF16)
    cs = cs_ref[...]
    for g in range(FNET_GROUPS):
        a = _dot(h[:, g * gd:(g + 1) * gd], cs)
        ac_ref[:, g * gd:(g + 1) * gd] = a[:, :gd].astype(BF16)
        as_ref[:, g * gd:(g + 1) * gd] = a[:, gd:].astype(BF16)


def _fnet_a(x, mods, layer, seg, g1, cs):
    n, d = x.shape
    rows = seg.batch * seg.seq
    out = pl.BlockSpec((ROW_TILE, d), lambda i: (i, 0))
    return pl.pallas_call(
        functools.partial(_fnet_a_kernel, gd=d // FNET_GROUPS),
        grid=(seg.tiles,),
        in_specs=[
            pl.BlockSpec((ROW_TILE, d), lambda i: (seg.tile0 + i, 0)),
            _mod_spec(layer, seg, d),
            pl.BlockSpec((1, d), lambda i: (0, 0)),
            pl.BlockSpec(cs.shape, lambda i: (0, 0)),
        ],
        out_specs=[out, out],
        out_shape=[jax.ShapeDtypeStruct((rows, d), BF16)] * 2,
        compiler_params=_cparams("parallel"),
        name="fnet_channel_dft",
    )(x, mods, g1, cs)


def _fnet_b_kernel(ct_ref, st_ref, ac_ref, as_ref, w_ref, x_ref, m_ref, o_ref, *, norm):
    f = (_dot(ct_ref[...], ac_ref[...]) - _dot(st_ref[...], as_ref[...])) * norm
    o_ref[...] = x_ref[...] + m_ref[2] * _dot(f.astype(BF16), w_ref[...])


def _fnet_b(x, mods, layer, seg, ac, as_, ct, st, w, norm):
    n, d = x.shape
    t = seg.seq
    tps = seg.tiles_per_seq
    x_spec = pl.BlockSpec((ROW_TILE, d), lambda b, i: (seg.tile0 + b * tps + i, 0))
    tab = pl.BlockSpec((ROW_TILE, t), lambda b, i: (i, 0))
    seq = pl.BlockSpec((t, d), lambda b, i: (b, 0))
    return pl.pallas_call(
        functools.partial(_fnet_b_kernel, norm=norm),
        grid=(seg.batch, tps),
        in_specs=[tab, tab, seq, seq, pl.BlockSpec(w.shape, lambda b, i: (0, 0)), x_spec,
                  _mod_spec(layer, seg, d, tile_of=lambda b, i: b * tps + i)],
        out_specs=x_spec,
        out_shape=jax.ShapeDtypeStruct((n, d), F32),
        input_output_aliases={5: 0},
        compiler_params=_cparams("parallel", "arbitrary"),
        name="fnet_position_dft",
    )(ct, st, ac, as_, w, x, mods)


def _router_kernel(x_ref, m_ref, g_ref, rwt_ref, rb_ref, h_ref, idx_ref, wts_ref):
    h = _modulate(x_ref[...], g_ref[...], m_ref[3], m_ref[4])
    h_ref[...] = h.astype(BF16)
    logits = lax.dot_general(rwt_ref[...], h, (((1,), (1,)), ((), ())), preferred_element_type=F32,
                             precision=lax.Precision.HIGHEST)
    sc = _sigmoid(logits)
    gr = sc + rb_ref[...]
    gp = EXPERTS_PER_GROUP
    row = lambda a, e: a[e:e + 1, :]
    best_g = None
    for g in range(N_EXPERT_GROUPS):
        vals = [row(gr, g * gp + i) for i in range(gp)]
        gs = None
        for i in range(gp):
            for j in range(i + 1, gp):
                pair = vals[i] + vals[j]
                gs = pair if gs is None else jnp.maximum(gs, pair)
        if best_g is None:
            best_g, best_v = jnp.zeros(gs.shape, jnp.int32), gs
        else:
            better = gs > best_v
            best_g = jnp.where(better, g, best_g)
            best_v = jnp.where(better, gs, best_v)
    sel, raw = [], []
    for i in range(gp):
        s_i, r_i = row(gr, i), row(sc, i)
        for g in range(1, N_EXPERT_GROUPS):
            s_i = jnp.where(best_g == g, row(gr, g * gp + i), s_i)
            r_i = jnp.where(best_g == g, row(sc, g * gp + i), r_i)
        sel.append(s_i)
        raw.append(r_i)

    def argmax_first(vals, raws):
        bi, bv, br = jnp.zeros(vals[0].shape, jnp.int32), vals[0], raws[0]
        for i in range(1, len(vals)):
            better = vals[i] > bv
            bi = jnp.where(better, i, bi)
            bv = jnp.where(better, vals[i], bv)
            br = jnp.where(better, raws[i], br)
        return bi, br

    i1, w1 = argmax_first(sel, raw)
    masked = [jnp.where(i1 == i, -jnp.inf, sel[i]) for i in range(gp)]
    i2, w2 = argmax_first(masked, raw)
    tot = w1 + w2
    idx_ref[0:1, :] = best_g * gp + i1
    idx_ref[1:2, :] = best_g * gp + i2
    wts_ref[0:1, :] = w1 / tot
    wts_ref[1:2, :] = w2 / tot


def _router(x, mods, layer, segs, g2, rw_t, rb):
    n, d = x.shape
    ctx, lat = segs

    def mod_row(i):
        return jnp.where(i < ctx.tiles, ctx.mod0, lat.mod0 + (i - ctx.tiles) // lat.tiles_per_seq)

    return pl.pallas_call(
        _router_kernel,
        grid=(n // ROW_TILE,),
        in_specs=[
            pl.BlockSpec((ROW_TILE, d), lambda i: (i, 0)),
            pl.BlockSpec((None, None, 6, 1, d), lambda i: (layer, mod_row(i), 0, 0, 0)),
            pl.BlockSpec((1, d), lambda i: (0, 0)),
            pl.BlockSpec(rw_t.shape, lambda i: (0, 0)),
            pl.BlockSpec(rb.shape, lambda i: (0, 0)),
        ],
        out_specs=[
            pl.BlockSpec((ROW_TILE, d), lambda i: (i, 0)),
            pl.BlockSpec((2, ROW_TILE), lambda i: (0, i)),
            pl.BlockSpec((2, ROW_TILE), lambda i: (0, i)),
        ],
        out_shape=[
            jax.ShapeDtypeStruct((n, d), BF16),
            jax.ShapeDtypeStruct((2, n), jnp.int32),
            jax.ShapeDtypeStruct((2, n), F32),
        ],
        compiler_params=_cparams("parallel"),
        name="moe_router",
    )(x, mods, g2, rw_t, rb)


def _expert_kernel(be_ref, nb_ref, xs_ref, wg_ref, wu_ref, wd_ref, rw_ref, y_ref, wg_b, wu_b, wd_b):
    i = pl.program_id(0)
    prev = be_ref[jnp.maximum(i - 1, 0)]

    @pl.when(jnp.logical_or(i == 0, be_ref[i] != prev))
    def _():
        wg_b[...] = wg_ref[...].astype(BF16)
        wu_b[...] = wu_ref[...].astype(BF16)
        wd_b[...] = wd_ref[...].astype(BF16)

    @pl.when(i < nb_ref[0])
    def _():
        xb = xs_ref[...]
        gate = _dot(xb, wg_b[...])
        hid = (gate * _sigmoid(gate)) * _dot(xb, wu_b[...])
        y = _dot(hid.astype(BF16), wd_b[...]) * rw_ref[...]
        y_ref[...] = y.astype(BF16)

    @pl.when(i >= nb_ref[0])
    def _():
        y_ref[...] = jnp.zeros_like(y_ref)


def _experts(xs, row_w, block_e, n_used, w_gate, w_up, w_down, layer):
    rows, d = xs.shape
    de = w_gate.shape[-1]
    n_blocks = rows // MOE_ROWS
    grid_spec = pltpu.PrefetchScalarGridSpec(
        num_scalar_prefetch=2,
        grid=(n_blocks,),
        in_specs=[
            pl.BlockSpec((MOE_ROWS, d), lambda i, be, nb: (i, 0)),
            pl.BlockSpec((None, None, d, de), lambda i, be, nb: (layer, be[i], 0, 0)),
            pl.BlockSpec((None, None, d, de), lambda i, be, nb: (layer, be[i], 0, 0)),
            pl.BlockSpec((None, None, de, d), lambda i, be, nb: (layer, be[i], 0, 0)),
            pl.BlockSpec((MOE_ROWS, 1), lambda i, be, nb: (i, 0)),
        ],
        out_specs=pl.BlockSpec((MOE_ROWS, d), lambda i, be, nb: (i, 0)),
        scratch_shapes=[pltpu.VMEM((d, de), BF16), pltpu.VMEM((d, de), BF16), pltpu.VMEM((de, d), BF16)],
    )
    return pl.pallas_call(
        _expert_kernel,
        grid_spec=grid_spec,
        out_shape=jax.ShapeDtypeStruct((rows, d), BF16),
        compiler_params=_cparams("arbitrary"),
        name="moe_experts",
    )(block_e, n_used, xs, w_gate, w_up, w_down, row_w)


def _combine_kernel(x_ref, m_ref, y0_ref, y1_ref, *rest, final):
    if final:
        fg_ref, o_ref = rest
    else:
        (o_ref,) = rest
    y = y0_ref[...].astype(F32) + y1_ref[...].astype(F32)
    x = x_ref[...] + m_ref[5] * y
    if final:
        x = _rms(x) * fg_ref[...]
    o_ref[...] = x


def _combine(x, mods, layer, seg, y0, y1, final_g):
    n, d = x.shape
    final = final_g is not None
    rows = seg.batch * seg.seq
    x_spec = pl.BlockSpec((ROW_TILE, d), lambda i: (seg.tile0 + i, 0))
    in_specs = [x_spec, _mod_spec(layer, seg, d), x_spec, x_spec]
    args = [x, mods, y0, y1]
    if final:
        in_specs.append(pl.BlockSpec((1, d), lambda i: (0, 0)))
        args.append(final_g)
        out_spec = pl.BlockSpec((ROW_TILE, d), lambda i: (i, 0))
        out_shape = jax.ShapeDtypeStruct((rows, d), F32)
        aliases = {}
    else:
        out_spec, out_shape, aliases = x_spec, jax.ShapeDtypeStruct((n, d), F32), {0: 0}
    return pl.pallas_call(
        functools.partial(_combine_kernel, final=final),
        grid=(seg.tiles,),
        in_specs=in_specs,
        out_specs=out_spec,
        out_shape=out_shape,
        input_output_aliases=aliases,
        compiler_params=_cparams("parallel"),
        name="moe_combine_final" if final else "moe_combine",
    )(*args)


def _dispatch_plan(idx, wts):
    n = idx.shape[1]
    a = 2 * n
    flat_e = idx.reshape(a)
    onehot = (flat_e[:, None] == jnp.arange(N_EXPERTS, dtype=jnp.int32)[None, :]).astype(jnp.int32)
    csum = jnp.cumsum(onehot, axis=0)
    counts = csum[-1]
    rank = jnp.sum(csum * onehot, axis=1) - 1
    padded = (counts + MOE_ROWS - 1) // MOE_ROWS * MOE_ROWS
    pad_end = jnp.cumsum(padded)
    pad_start = pad_end - padded
    pos = pad_start[flat_e] + rank
    n_blocks = a // MOE_ROWS + N_EXPERTS
    rows = n_blocks * MOE_ROWS
    tok = jnp.arange(a, dtype=jnp.int32) % n
    row_tok = jnp.zeros((rows,), jnp.int32).at[pos].set(tok)
    row_w = jnp.zeros((rows,), F32).at[pos].set(wts.reshape(a))
    starts = jnp.arange(n_blocks, dtype=jnp.int32) * MOE_ROWS
    block_e = jnp.minimum(jnp.searchsorted(pad_end, starts, side="right"), N_EXPERTS - 1).astype(jnp.int32)
    n_used = (pad_end[-1] // MOE_ROWS).astype(jnp.int32).reshape(1)
    return pos.reshape(2, n), row_tok, row_w.reshape(rows, 1), block_e, n_used


def _moe(x, mods, layer, segs, g2, rw_t, rb, w_gate, w_up, w_down, final_g):
    h2, idx, wts = _router(x, mods, layer, segs, g2, rw_t, rb)
    pos, row_tok, row_w, block_e, n_used = _dispatch_plan(idx, wts)
    xs = jnp.take(h2, row_tok, axis=0)
    ys = _experts(xs, row_w, block_e, n_used, w_gate, w_up, w_down, layer)
    y0 = jnp.take(ys, pos[0], axis=0)
    y1 = jnp.take(ys, pos[1], axis=0)
    if final_g is None:
        for seg in segs:
            x = _combine(x, mods, layer, seg, y0, y1, None)
        return x
    return tuple(_combine(x, mods, layer, seg, y0, y1, final_g) for seg in segs)


def _mla_rope_tables(t):
    axis_dim = MLA_ROPE // 2
    row = jnp.repeat(jnp.arange(t // GRID_W), GRID_W).astype(F32)
    col = jnp.tile(jnp.arange(GRID_W), t // GRID_W).astype(F32)
    inv = ROPE_BASE ** (-jnp.arange(0, axis_dim, 2, dtype=F32) / axis_dim)
    ar, ac = row[:, None] * inv[None, :], col[:, None] * inv[None, :]
    ones = jnp.ones((t, LANE - MLA_ROPE), F32)
    cos = jnp.concatenate([jnp.cos(ar), jnp.cos(ar), jnp.cos(ac), jnp.cos(ac), ones], axis=-1)
    sin = jnp.concatenate([-jnp.sin(ar), jnp.sin(ar), -jnp.sin(ac), jnp.sin(ac), 0.0 * ones], axis=-1)
    return cos, sin


def _ret_rot_tables(t, dk):
    inv = ROPE_BASE ** (-jnp.linspace(0.0, 1.0, dk // 2))
    ang = jnp.arange(t, dtype=F32)[:, None] * inv[None, :]
    return jnp.cos(ang), jnp.sin(ang)


def _dft_tables(n):
    k = np.arange(n, dtype=np.int64)
    ang = jnp.asarray((np.outer(k, k) % n).astype(np.float32)) * (2.0 * math.pi / n)
    return jnp.cos(ang), jnp.sin(ang)


def _mla_weights(w_in, q_g, kv_g, w_uq, w_ukv, w_o):
    d = w_in.shape[0]
    hd = MLA_NOPE + MLA_ROPE
    w_in_p = jnp.concatenate([w_in, jnp.zeros((d, LANE - MLA_ROPE), w_in.dtype)], axis=1)
    uq = w_uq.reshape(MLA_Q_LORA, MLA_HEADS, hd)
    uq = jnp.concatenate([uq, jnp.zeros((MLA_Q_LORA, MLA_HEADS, 2 * LANE - hd), uq.dtype)], axis=-1)
    ukv = w_ukv.reshape(MLA_KV_LORA, MLA_HEADS, MLA_NOPE + MLA_V)
    return {
        "w_in": w_in_p.astype(BF16),
        "q_g": q_g.reshape(1, -1),
        "kv_g": kv_g.reshape(1, -1),
        "w_q": uq.reshape(MLA_Q_LORA, MLA_HEADS * 2 * LANE).astype(BF16),
        "w_kn": ukv[..., :MLA_NOPE].reshape(MLA_KV_LORA, MLA_HEADS * MLA_NOPE).astype(BF16),
        "w_v": ukv[..., MLA_NOPE:].reshape(MLA_KV_LORA, MLA_HEADS * MLA_V).astype(BF16),
        "w_o": w_o.astype(BF16),
    }


def kernel(x_prompt, x_sample, cache_mla, state_ret, c, c_ctx, norm1_g, norm2_g, ada_w, ada_b, final_norm_g,
           mla_w_in, mla_q_norm_g, mla_kv_norm_g, mla_w_uq, mla_w_ukv, mla_w_o, ret_w_in, ret_decay_f,
           ret_decay_b, ret_w_o, fnet_w, router_w, router_b, moe_w_gate, moe_w_up, moe_w_down):
    b_ctx, t_ctx, d = x_prompt.shape
    b_lat, t_lat, _ = x_sample.shape
    depth = ada_w.shape[0]
    assert t_ctx == ROW_TILE and t_lat % ROW_TILE == 0 and b_lat + 1 <= 8
    n_ctx = b_ctx * t_ctx
    ctx = _Seg(0, b_ctx, t_ctx, 0, False)
    lat = _Seg(n_ctx, b_lat, t_lat, 1, True)
    segs = (ctx, lat)

    x = jnp.concatenate([x_prompt.reshape(n_ctx, d), x_sample.reshape(b_lat * t_lat, d)], axis=0)
    cond8 = jnp.concatenate([c_ctx[None, :], c, jnp.zeros((8 - 1 - b_lat, d), F32)], axis=0)
    mods = _modulation_all(cond8, ada_w, ada_b).reshape(depth, 8, 6, 1, d)

    rw_t = router_w.T
    rb = router_b.reshape(N_EXPERTS, 1).astype(F32)
    final_g = final_norm_g.reshape(1, d)
    dk = ret_w_in.shape[2] // (8 * RET_HEADS)
    dv = 2 * dk

    caches, states = [], []
    counters = [0, 0, 0]
    for layer in range(depth):
        kind = layer % 3
        j = counters[kind]
        counters[kind] += 1
        g1 = norm1_g[layer].reshape(1, d)
        g2 = norm2_g[layer].reshape(1, d)
        if kind == 0:
            w = _mla_weights(mla_w_in[j], mla_q_norm_g[j], mla_kv_norm_g[j], mla_w_uq[j], mla_w_ukv[j],
                             mla_w_o[j])
            qc, kc, vc, cache = _mla_proj(x, mods, layer, ctx, g1, w, None)
            ql, kl, vl = _mla_proj(x, mods, layer, lat, g1, w, _mla_rope_tables(t_lat))
            past = cache_mla.shape[2]
            cpad = jnp.pad(cache_mla[:, j].reshape(b_lat * past, -1), ((0, 0), (0, LANE - MLA_ROPE)))
            kp, vp = _cache_kv(cpad, w)
            x = _attention(x, mods, layer, ctx, qc, [(kc, vc, t_ctx)], w["w_o"])
            x = _attention(x, mods, layer, lat, ql, [(kl, vl, t_lat), (kp, vp, past)], w["w_o"])
            caches.append(cache.reshape(b_ctx, t_ctx, -1))
        elif kind == 1:
            w_in = ret_w_in[j]
            qk = RET_HEADS * dk
            k_scale = jnp.concatenate([jnp.ones((qk,), F32), jnp.full((qk,), dk ** -0.5, F32),
                                       jnp.ones((w_in.shape[1] - 2 * qk,), F32)])
            w_in_b = (w_in * k_scale[None, :]).astype(BF16)
            w_o_b = ret_w_o[j].astype(BF16)
            zc = _ret_proj(x, mods, layer, ctx, g1, w_in_b, None, dk)
            zl = _ret_proj(x, mods, layer, lat, g1, w_in_b, _ret_rot_tables(t_lat, dk), dk)
            yc, s_ctx = _ret_scan(zc, ctx, ret_decay_f[j], ret_decay_b[j], None, True, dk, dv)
            yl, _ = _ret_scan(zl, lat, ret_decay_f[j], ret_decay_b[j], state_ret[:, j], False, dk, dv)
            x = _matmul_residual(x, mods, layer, ctx, yc, w_o_b)
            x = _matmul_residual(x, mods, layer, lat, yl, w_o_b)
            states.append(s_ctx)
        else:
            gd = d // FNET_GROUPS
            cc, sc = _dft_tables(gd)
            cs = jnp.concatenate([cc, sc], axis=1).astype(BF16)
            w_b = fnet_w[j].astype(BF16)
            for seg in segs:
                ct, st = _dft_tables(seg.seq)
                ac, as_ = _fnet_a(x, mods, layer, seg, g1, cs)
                x = _fnet_b(x, mods, layer, seg, ac, as_, ct.astype(BF16), st.astype(BF16), w_b,
                            (seg.seq * gd) ** -0.5)
        last = layer == depth - 1
        x = _moe(x, mods, layer, segs, g2, rw_t, rb, moe_w_gate, moe_w_up, moe_w_down,
                 final_g if last else None)

    y_prompt, y_sample = x
    new_cache = jnp.stack(caches, axis=1)
    new_state = jnp.stack(states, axis=1)
    return (y_prompt.reshape(b_ctx, t_ctx, d), y_sample.reshape(b_lat, t_lat, d), new_cache, new_state)
```

```python
import functools
import math

import jax
import jax.numpy as jnp
import numpy as np
from jax import lax
from jax.experimental import pallas as pl
from jax.experimental.pallas import tpu as pltpu
from jax.experimental.pallas import tpu_sc as plsc

F32 = jnp.float32
BF16 = jnp.bfloat16

GRID_W = 64
MLA_HEADS = 8
MLA_NOPE = 128
MLA_ROPE = 64
MLA_V = 128
MLA_Q_LORA = 384
MLA_KV_LORA = 256
ROPE_BASE = 10000.0
RET_HEADS = 4
RET_CHUNK = 128
FNET_GROUPS = 4
N_EXPERTS = 16
N_EXPERT_GROUPS = 4
EXPERTS_PER_GROUP = 4
D_EXPERT = 512
NORM_EPS = 1e-6

LANE = 128
ROW_TILE = 256
MOE_ROWS = 256
VMEM_LIMIT = 56 * 1024 * 1024
SC_CORES = 2
SC_SUBCORES = 16
SC_CHUNK = 128


def _cparams(*sem):
    return pltpu.CompilerParams(dimension_semantics=sem, vmem_limit_bytes=VMEM_LIMIT)


def _sigmoid(x):
    return 1.0 / (1.0 + jnp.exp(-x))


def _rms(x):
    return x * lax.rsqrt(jnp.mean(x * x, axis=-1, keepdims=True) + NORM_EPS)


def _modulate(x, g, shift, scale):
    return (_rms(x) * g) * (1.0 + scale) + shift


def _dot(a, b):
    return jnp.dot(a, b, preferred_element_type=F32)


def _dot_nt(a, b):
    return lax.dot_general(a, b, (((1,), (1,)), ((), ())), preferred_element_type=F32)


def _mod_kernel(c_ref, w_ref, b_ref, o_ref):
    c = c_ref[...]
    s = (c * _sigmoid(c)).astype(BF16)
    o_ref[...] = _dot(s, w_ref[...].astype(BF16)) + b_ref[...]


def _modulation_all(cond8, ada_w, ada_b):
    depth, d, d6 = ada_w.shape
    tn = d6 // 4
    return pl.pallas_call(
        _mod_kernel,
        grid=(depth, d6 // tn),
        in_specs=[
            pl.BlockSpec((8, d), lambda l, n: (0, 0)),
            pl.BlockSpec((None, d, tn), lambda l, n: (l, 0, n)),
            pl.BlockSpec((None, 1, tn), lambda l, n: (l, 0, n)),
        ],
        out_specs=pl.BlockSpec((None, 8, tn), lambda l, n: (l, 0, n)),
        out_shape=jax.ShapeDtypeStruct((depth, 8, d6), F32),
        compiler_params=_cparams("parallel", "parallel"),
        name="modulation",
    )(cond8, ada_w, ada_b.reshape(depth, 1, d6))


class _Seg:
    def __init__(self, row0, batch, seq, mod0, per_batch_mod):
        self.row0, self.batch, self.seq = row0, batch, seq
        self.mod0, self.per_batch_mod = mod0, per_batch_mod
        self.tiles = batch * seq // ROW_TILE
        self.tile0 = row0 // ROW_TILE
        self.tiles_per_seq = seq // ROW_TILE

    def mod_row(self, tile):
        if self.per_batch_mod:
            return self.mod0 + tile // self.tiles_per_seq
        return self.mod0


def _mod_spec(layer, seg, d, tile_of=lambda *a: a[0]):
    return pl.BlockSpec((None, None, 6, 1, d), lambda *a: (layer, seg.mod_row(tile_of(*a)), 0, 0, 0))


def _rope_partner(x):
    lane = lax.broadcasted_iota(jnp.int32, x.shape, 1)
    first = (lane % 32) < 16
    return jnp.where(first, pltpu.roll(x, LANE - 16, 1), pltpu.roll(x, 16, 1))


def _mla_proj_kernel(*refs, rope, scale):
    if rope:
        (x_ref, m_ref, g_ref, win_ref, qg_ref, kvg_ref, wq_ref, wkn_ref, wv_ref, cos_ref, sin_ref,
         q_ref, k_ref, v_ref) = refs
    else:
        (x_ref, m_ref, g_ref, win_ref, qg_ref, kvg_ref, wq_ref, wkn_ref, wv_ref,
         q_ref, k_ref, v_ref, cache_ref) = refs
    h = _modulate(x_ref[...], g_ref[...], m_ref[0], m_ref[1]).astype(BF16)
    z = _dot(h, win_ref[...])
    cq = z[:, :MLA_Q_LORA]
    ckv = z[:, MLA_Q_LORA:MLA_Q_LORA + MLA_KV_LORA]
    kpe = z[:, MLA_Q_LORA + MLA_KV_LORA:]
    cqn = (_rms(cq) * qg_ref[...]).astype(BF16)
    ckvn = _rms(ckv) * kvg_ref[...]
    ckvb = ckvn.astype(BF16)
    q = _dot(cqn, wq_ref[...]) * scale
    kn = _dot(ckvb, wkn_ref[...])
    v_ref[...] = _dot(ckvb, wv_ref[...]).astype(BF16)
    if rope:
        cos, sin = cos_ref[...], sin_ref[...]
        kpe = kpe * cos + _rope_partner(kpe) * sin
    else:
        cache_ref[:, :MLA_KV_LORA] = ckvn
        cache_ref[:, MLA_KV_LORA:] = kpe[:, :MLA_ROPE]
    kpe_b = kpe.astype(BF16)
    for hd in range(MLA_HEADS):
        lo = hd * 2 * LANE
        q_ref[:, lo:lo + LANE] = q[:, lo:lo + LANE].astype(BF16)
        qr = q[:, lo + LANE:lo + 2 * LANE]
        if rope:
            qr = qr * cos + _rope_partner(qr) * sin
        q_ref[:, lo + LANE:lo + 2 * LANE] = qr.astype(BF16)
        k_ref[:, lo:lo + LANE] = kn[:, hd * LANE:(hd + 1) * LANE].astype(BF16)
        k_ref[:, lo + LANE:lo + 2 * LANE] = kpe_b


def _mla_proj(x, mods, layer, seg, g1, w, rope_tabs):
    n, d = x.shape
    rope = rope_tabs is not None
    rows = seg.batch * seg.seq
    hq = MLA_HEADS * 2 * LANE
    hv = MLA_HEADS * MLA_V
    const = lambda i: (0, 0)
    in_specs = [
        pl.BlockSpec((ROW_TILE, d), lambda i: (seg.tile0 + i, 0)),
        _mod_spec(layer, seg, d),
        pl.BlockSpec((1, d), const),
        pl.BlockSpec(w["w_in"].shape, const),
        pl.BlockSpec((1, MLA_Q_LORA), const),
        pl.BlockSpec((1, MLA_KV_LORA), const),
        pl.BlockSpec(w["w_q"].shape, const),
        pl.BlockSpec(w["w_kn"].shape, const),
        pl.BlockSpec(w["w_v"].shape, const),
    ]
    args = [x, mods, g1, w["w_in"], w["q_g"], w["kv_g"], w["w_q"], w["w_kn"], w["w_v"]]
    out_specs = [
        pl.BlockSpec((ROW_TILE, hq), lambda i: (i, 0)),
        pl.BlockSpec((ROW_TILE, hq), lambda i: (i, 0)),
        pl.BlockSpec((ROW_TILE, hv), lambda i: (i, 0)),
    ]
    out_shape = [
        jax.ShapeDtypeStruct((rows, hq), BF16),
        jax.ShapeDtypeStruct((rows, hq), BF16),
        jax.ShapeDtypeStruct((rows, hv), BF16),
    ]
    if rope:
        tab = pl.BlockSpec((ROW_TILE, LANE), lambda i: (i % seg.tiles_per_seq, 0))
        in_specs += [tab, tab]
        args += list(rope_tabs)
    else:
        cw = MLA_KV_LORA + MLA_ROPE
        out_specs.append(pl.BlockSpec((ROW_TILE, cw), lambda i: (i, 0)))
        out_shape.append(jax.ShapeDtypeStruct((rows, cw), F32))
    return pl.pallas_call(
        functools.partial(_mla_proj_kernel, rope=rope, scale=(MLA_NOPE + MLA_ROPE) ** -0.5),
        grid=(seg.tiles,),
        in_specs=in_specs,
        out_specs=out_specs,
        out_shape=out_shape,
        compiler_params=_cparams("parallel"),
        name="mla_proj_lat" if rope else "mla_proj_ctx",
    )(*args)


def _cache_kv_kernel(c_ref, wkn_ref, wv_ref, k_ref, v_ref):
    c = c_ref[...]
    ckv = c[:, :MLA_KV_LORA].astype(BF16)
    kpe_b = c[:, MLA_KV_LORA:].astype(BF16)
    kn = _dot(ckv, wkn_ref[...])
    v_ref[...] = _dot(ckv, wv_ref[...]).astype(BF16)
    for hd in range(MLA_HEADS):
        lo = hd * 2 * LANE
        k_ref[:, lo:lo + LANE] = kn[:, hd * LANE:(hd + 1) * LANE].astype(BF16)
        k_ref[:, lo + LANE:lo + 2 * LANE] = kpe_b


def _cache_kv(cache_pad, w):
    rows, cw = cache_pad.shape
    hq = MLA_HEADS * 2 * LANE
    hv = MLA_HEADS * MLA_V
    const = lambda i: (0, 0)
    return pl.pallas_call(
        _cache_kv_kernel,
        grid=(rows // ROW_TILE,),
        in_specs=[
            pl.BlockSpec((ROW_TILE, cw), lambda i: (i, 0)),
            pl.BlockSpec(w["w_kn"].shape, const),
            pl.BlockSpec(w["w_v"].shape, const),
        ],
        out_specs=[pl.BlockSpec((ROW_TILE, hq), lambda i: (i, 0)),
                   pl.BlockSpec((ROW_TILE, hv), lambda i: (i, 0))],
        out_shape=[jax.ShapeDtypeStruct((rows, hq), BF16), jax.ShapeDtypeStruct((rows, hv), BF16)],
        compiler_params=_cparams("parallel"),
        name="mla_cache_kv",
    )(cache_pad, w["w_kn"], w["w_v"])


def _attn_kernel(*refs, n_parts):
    q_ref = refs[0]
    kv_refs = refs[1:1 + 2 * n_parts]
    wo_ref, x_ref, m_ref, o_ref, acc_ref = refs[1 + 2 * n_parts:]
    for hd in range(MLA_HEADS):
        qh = q_ref[:, hd * 2 * LANE:(hd + 1) * 2 * LANE]
        scores = [_dot_nt(qh, kv_refs[2 * p][:, hd * 2 * LANE:(hd + 1) * 2 * LANE]) for p in range(n_parts)]
        mx = scores[0].max(axis=-1, keepdims=True)
        for s in scores[1:]:
            mx = jnp.maximum(mx, s.max(axis=-1, keepdims=True))
        den = None
        out = None
        for p, s in enumerate(scores):
            e = jnp.exp(s - mx)
            den = e.sum(axis=-1, keepdims=True) if den is None else den + e.sum(axis=-1, keepdims=True)
            pv = _dot(e.astype(BF16), kv_refs[2 * p + 1][:, hd * MLA_V:(hd + 1) * MLA_V])
            out = pv if out is None else out + pv
        acc_ref[:, hd * MLA_V:(hd + 1) * MLA_V] = (out / den).astype(BF16)
    y = _dot(acc_ref[...], wo_ref[...])
    o_ref[...] = x_ref[...] + m_ref[2] * y


def _attention(x, mods, layer, seg, q, kv_parts, w_o):
    n, d = x.shape
    hq = MLA_HEADS * 2 * LANE
    hv = MLA_HEADS * MLA_V
    tps = seg.tiles_per_seq
    in_specs = [pl.BlockSpec((ROW_TILE, hq), lambda b, i: (b * tps + i, 0))]
    args = [q]
    for k, v, rows in kv_parts:
        in_specs += [pl.BlockSpec((rows, hq), lambda b, i: (b, 0)), pl.BlockSpec((rows, hv), lambda b, i: (b, 0))]
        args += [k, v]
    x_spec = pl.BlockSpec((ROW_TILE, d), lambda b, i: (seg.tile0 + b * tps + i, 0))
    in_specs += [
        pl.BlockSpec(w_o.shape, lambda b, i: (0, 0)),
        x_spec,
        _mod_spec(layer, seg, d, tile_of=lambda b, i: b * tps + i),
    ]
    args += [w_o, x, mods]
    return pl.pallas_call(
        functools.partial(_attn_kernel, n_parts=len(kv_parts)),
        grid=(seg.batch, tps),
        in_specs=in_specs,
        out_specs=x_spec,
        out_shape=jax.ShapeDtypeStruct((n, d), F32),
        scratch_shapes=[pltpu.VMEM((ROW_TILE, hv), BF16)],
        input_output_aliases={len(args) - 2: 0},
        compiler_params=_cparams("parallel", "arbitrary"),
        name="mla_attention",
    )(*args)


RET_COL = 1024


def _ret_proj_kernel(*refs, rotary, dk):
    if rotary:
        x_ref, m_ref, g_ref, w_ref, cos_ref, sin_ref, z_ref, h_ref = refs
    else:
        x_ref, m_ref, g_ref, w_ref, z_ref, h_ref = refs
    j = pl.program_id(1)

    @pl.when(j == 0)
    def _():
        h_ref[...] = _modulate(x_ref[...], g_ref[...], m_ref[0], m_ref[1]).astype(BF16)

    acc = _dot(h_ref[...], w_ref[...])
    qk_blocks = 2 * RET_HEADS * dk // RET_COL
    gate0 = (2 * RET_HEADS * dk + 2 * RET_HEADS * dk) // RET_COL

    if rotary:
        @pl.when(j < qk_blocks)
        def _():
            cos, sin = cos_ref[...], sin_ref[...]
            half = dk // 2
            for hd in range(RET_COL // dk):
                x1 = acc[:, hd * dk:hd * dk + half]
                x2 = acc[:, hd * dk + half:(hd + 1) * dk]
                z_ref[:, hd * dk:hd * dk + half] = (x1 * cos - x2 * sin).astype(BF16)
                z_ref[:, hd * dk + half:(hd + 1) * dk] = (x1 * sin + x2 * cos).astype(BF16)

        @pl.when(jnp.logical_and(j >= qk_blocks, j < gate0))
        def _():
            z_ref[...] = acc.astype(BF16)
    else:
        @pl.when(j < gate0)
        def _():
            z_ref[...] = acc.astype(BF16)

    @pl.when(j >= gate0)
    def _():
        z_ref[...] = (acc * _sigmoid(acc)).astype(BF16)


def _ret_proj(x, mods, layer, seg, g1, w_in, rot_tabs, dk):
    n, d = x.shape
    rotary = rot_tabs is not None
    rows = seg.batch * seg.seq
    tm = 512
    ncol = w_in.shape[1]
    tps = seg.seq // tm if seg.seq >= tm else 1
    in_specs = [
        pl.BlockSpec((tm, d), lambda i, j: (seg.row0 // tm + i, 0)),
        pl.BlockSpec((None, None, 6, 1, d),
                     lambda i, j: (layer, seg.mod0 + (i * tm // seg.seq if seg.per_batch_mod else 0), 0, 0, 0)),
        pl.BlockSpec((1, d), lambda i, j: (0, 0)),
        pl.BlockSpec((d, RET_COL), lambda i, j: (0, j)),
    ]
    args = [x, mods, g1, w_in]
    if rotary:
        tab = pl.BlockSpec((tm, dk // 2), lambda i, j: (i % tps, 0))
        in_specs += [tab, tab]
        args += list(rot_tabs)
    return pl.pallas_call(
        functools.partial(_ret_proj_kernel, rotary=rotary, dk=dk),
        grid=(rows // tm, ncol // RET_COL),
        in_specs=in_specs,
        out_specs=pl.BlockSpec((tm, RET_COL), lambda i, j: (i, j)),
        out_shape=jax.ShapeDtypeStruct((rows, ncol), BF16),
        scratch_shapes=[pltpu.VMEM((tm, d), BF16)],
        compiler_params=_cparams("parallel", "arbitrary"),
        name="ret_proj_lat" if rotary else "ret_proj_ctx",
    )(*args)


def _log_sigmoid(x):
    return jnp.minimum(x, 0.0) - jnp.log(1.0 + jnp.exp(-jnp.abs(x)))


def _ret_scan_kernel(*refs, has_s0, emit_state, n_chunks):
    refs = list(refs)
    lf_ref, lb_ref, q_ref, k_ref, v_ref, gf_ref, gb_ref = refs[:7]
    pos = 7
    s0_ref = None
    if has_s0:
        s0_ref = refs[pos]
        pos += 1
    y_ref = refs[pos]
    pos += 1
    sout_ref = None
    if emit_state:
        sout_ref = refs[pos]
        pos += 1
    s_ref, yf_ref = refs[pos:]
    c = RET_CHUNK
    ii = lax.broadcasted_iota(jnp.int32, (c, c), 0).astype(F32)
    jj = lax.broadcasted_iota(jnp.int32, (c, c), 1).astype(F32)
    idx = lax.broadcasted_iota(jnp.int32, (c, 1), 0).astype(F32)

    for direction in range(2):
        fwd = direction == 0
        lg = _log_sigmoid((lf_ref if fwd else lb_ref)[...])
        rel = (ii - jj) if fwd else (jj - ii)
        keep = rel >= 0
        decay_in = jnp.where(keep, jnp.exp(jnp.where(keep, rel, 0.0) * lg), 0.0)
        decay_q = jnp.exp(((idx + 1.0) if fwd else (c - idx)) * lg)
        decay_k = jnp.exp(((c - 1.0 - idx) if fwd else idx) * lg)
        decay_c = jnp.exp(c * lg)
        g_ref = gf_ref if fwd else gb_ref
        if has_s0:
            s_ref[...] = s0_ref[direction]
        else:
            s_ref[...] = jnp.zeros_like(s_ref)

        def step(ci, carry, fwd=fwd, decay_in=decay_in, decay_q=decay_q, decay_k=decay_k, decay_c=decay_c,
                 g_ref=g_ref):
            cc = ci if fwd else n_chunks - 1 - ci
            r0 = pl.multiple_of(cc * c, c)
            qc = q_ref[pl.ds(r0, c), :]
            kc = k_ref[pl.ds(r0, c), :]
            vc = v_ref[pl.ds(r0, c), :]
            sc = _dot_nt(qc, kc) * decay_in
            s = s_ref[...]
            out = _dot(sc.astype(BF16), vc) + _dot((qc.astype(F32) * decay_q).astype(BF16), s.astype(BF16))
            kd_t = (kc.astype(F32) * decay_k).T.astype(BF16)
            s_ref[...] = decay_c * s + _dot(kd_t, vc)
            o = _rms(out) * g_ref[pl.ds(r0, c), :].astype(F32)
            if fwd:
                yf_ref[pl.ds(r0, c), :] = o
            else:
                y_ref[pl.ds(r0, c), :] = (yf_ref[pl.ds(r0, c), :] + o).astype(BF16)
            return carry

        lax.fori_loop(0, n_chunks, step, 0)
        if emit_state:
            sout_ref[direction] = s_ref[...]


def _ret_scan(z, seg, logit_f, logit_b, s0, emit_state, dk, dv):
    rows = seg.batch * seg.seq
    t = seg.seq
    hh = RET_HEADS
    qb, vb = hh * dk // dk, hh * dk * 2 // dv
    in_specs = [
        pl.BlockSpec((None, 1, 1), lambda b, h: (h, 0, 0)),
        pl.BlockSpec((None, 1, 1), lambda b, h: (h, 0, 0)),
        pl.BlockSpec((t, dk), lambda b, h: (b, h)),
        pl.BlockSpec((t, dk), lambda b, h: (b, qb + h)),
        pl.BlockSpec((t, dv), lambda b, h: (b, vb + h)),
        pl.BlockSpec((t, dv), lambda b, h: (b, vb + hh + h)),
        pl.BlockSpec((t, dv), lambda b, h: (b, vb + 2 * hh + h)),
    ]
    args = [logit_f.reshape(hh, 1, 1), logit_b.reshape(hh, 1, 1), z, z, z, z, z]
    state_spec = pl.BlockSpec((None, 2, None, dk, dv), lambda b, h: (b, 0, h, 0, 0))
    if s0 is not None:
        in_specs.append(state_spec)
        args.append(s0)
    out_specs = [pl.BlockSpec((t, dv), lambda b, h: (b, h))]
    out_shape = [jax.ShapeDtypeStruct((rows, hh * dv), BF16)]
    if emit_state:
        out_specs.append(state_spec)
        out_shape.append(jax.ShapeDtypeStruct((seg.batch, 2, hh, dk, dv), F32))
    res = pl.pallas_call(
        functools.partial(_ret_scan_kernel, has_s0=s0 is not None, emit_state=emit_state,
                          n_chunks=t // RET_CHUNK),
        grid=(seg.batch, hh),
        in_specs=in_specs,
        out_specs=out_specs,
        out_shape=out_shape,
        scratch_shapes=[pltpu.VMEM((dk, dv), F32), pltpu.VMEM((t, dv), F32)],
        compiler_params=_cparams("parallel", "parallel"),
        name="ret_scan",
    )(*args)
    return res if emit_state else (res[0], None)


def _mm_res_kernel(a_ref, w_ref, x_ref, m_ref, o_ref):
    o_ref[...] = x_ref[...] + m_ref[2] * _dot(a_ref[...], w_ref[...])


def _matmul_residual(x, mods, layer, seg, a, w):
    n, d = x.shape
    x_spec = pl.BlockSpec((ROW_TILE, d), lambda i: (seg.tile0 + i, 0))
    return pl.pallas_call(
        _mm_res_kernel,
        grid=(seg.tiles,),
        in_specs=[
            pl.BlockSpec((ROW_TILE, a.shape[1]), lambda i: (i, 0)),
            pl.BlockSpec(w.shape, lambda i: (0, 0)),
            x_spec,
            _mod_spec(layer, seg, d),
        ],
        out_specs=x_spec,
        out_shape=jax.ShapeDtypeStruct((n, d), F32),
        input_output_aliases={2: 0},
        compiler_params=_cparams("parallel"),
        name="matmul_residual",
    )(a, w, x, mods)


def _fnet_a_kernel(x_ref, m_ref, g_ref, cs_ref, ac_ref, as_ref, *, gd):
    h = _modulate(x_ref[...], g_ref[...], m_ref[0], m_ref[1]).astype(BF16)
    cs = cs_ref[...]
    for g in range(FNET_GROUPS):
        a = _dot(h[:, g * gd:(g + 1) * gd], cs)
        ac_ref[:, g * gd:(g + 1) * gd] = a[:, :gd].astype(BF16)
        as_ref[:, g * gd:(g + 1) * gd] = a[:, gd:].astype(BF16)


def _fnet_a(x, mods, layer, seg, g1, cs):
    n, d = x.shape
    rows = seg.batch * seg.seq
    out = pl.BlockSpec((ROW_TILE, d), lambda i: (i, 0))
    return pl.pallas_call(
        functools.partial(_fnet_a_kernel, gd=d // FNET_GROUPS),
        grid=(seg.tiles,),
        in_specs=[
            pl.BlockSpec((ROW_TILE, d), lambda i: (seg.tile0 + i, 0)),
            _mod_spec(layer, seg, d),
            pl.BlockSpec((1, d), lambda i: (0, 0)),
            pl.BlockSpec(cs.shape, lambda i: (0, 0)),
        ],
        out_specs=[out, out],
        out_shape=[jax.ShapeDtypeStruct((rows, d), BF16)] * 2,
        compiler_params=_cparams("parallel"),
        name="fnet_channel_dft",
    )(x, mods, g1, cs)


def _fnet_b_kernel(ct_ref, st_ref, ac_ref, as_ref, w_ref, x_ref, m_ref, o_ref, *, norm):
    f = (_dot(ct_ref[...], ac_ref[...]) - _dot(st_ref[...], as_ref[...])) * norm
    o_ref[...] = x_ref[...] + m_ref[2] * _dot(f.astype(BF16), w_ref[...])


def _fnet_b(x, mods, layer, seg, ac, as_, ct, st, w, norm):
    n, d = x.shape
    t = seg.seq
    tps = seg.tiles_per_seq
    x_spec = pl.BlockSpec((ROW_TILE, d), lambda b, i: (seg.tile0 + b * tps + i, 0))
    tab = pl.BlockSpec((ROW_TILE, t), lambda b, i: (i, 0))
    seq = pl.BlockSpec((t, d), lambda b, i: (b, 0))
    return pl.pallas_call(
        functools.partial(_fnet_b_kernel, norm=norm),
        grid=(seg.batch, tps),
        in_specs=[tab, tab, seq, seq, pl.BlockSpec(w.shape, lambda b, i: (0, 0)), x_spec,
                  _mod_spec(layer, seg, d, tile_of=lambda b, i: b * tps + i)],
        out_specs=x_spec,
        out_shape=jax.ShapeDtypeStruct((n, d), F32),
        input_output_aliases={5: 0},
        compiler_params=_cparams("parallel", "arbitrary"),
        name="fnet_position_dft",
    )(ct, st, ac, as_, w, x, mods)


def _pack_halves(a):
    w = a.shape[1] // 2
    bits = lambda v: lax.bitcast_convert_type(v.astype(BF16).astype(F32), jnp.uint32)
    return (bits(a[:, :w]) >> 16) | (bits(a[:, w:]) & jnp.uint32(0xFFFF0000))


def _unpack_halves(p):
    lo = lax.bitcast_convert_type(p << 16, F32)
    hi = lax.bitcast_convert_type(p & jnp.uint32(0xFFFF0000), F32)
    return lo, hi


def _router_kernel(x_ref, m_ref, g_ref, rwt_ref, rb_ref, h_ref, idx_ref, rank_ref, wcol_ref, cnt_ref, run_ref):
    step = pl.program_id(0)

    @pl.when(step == 0)
    def _():
        run_ref[...] = jnp.zeros_like(run_ref)

    h = _modulate(x_ref[...], g_ref[...], m_ref[3], m_ref[4])
    h_ref[...] = _pack_halves(h)
    logits = lax.dot_general(rwt_ref[...], h, (((1,), (1,)), ((), ())), preferred_element_type=F32,
                             precision=lax.Precision.HIGHEST)
    sc = _sigmoid(logits)
    gr = sc + rb_ref[...]
    gp = EXPERTS_PER_GROUP
    row = lambda a, e: a[e:e + 1, :]
    best_g = None
    for g in range(N_EXPERT_GROUPS):
        vals = [row(gr, g * gp + i) for i in range(gp)]
        gs = None
        for i in range(gp):
            for j in range(i + 1, gp):
                pair = vals[i] + vals[j]
                gs = pair if gs is None else jnp.maximum(gs, pair)
        if best_g is None:
            best_g, best_v = jnp.zeros(gs.shape, jnp.int32), gs
        else:
            better = gs > best_v
            best_g = jnp.where(better, g, best_g)
            best_v = jnp.where(better, gs, best_v)
    sel, raw = [], []
    for i in range(gp):
        s_i, r_i = row(gr, i), row(sc, i)
        for g in range(1, N_EXPERT_GROUPS):
            s_i = jnp.where(best_g == g, row(gr, g * gp + i), s_i)
            r_i = jnp.where(best_g == g, row(sc, g * gp + i), r_i)
        sel.append(s_i)
        raw.append(r_i)

    def argmax_first(vals, raws):
        bi, bv, br = jnp.zeros(vals[0].shape, jnp.int32), vals[0], raws[0]
        for i in range(1, len(vals)):
            better = vals[i] > bv
            bi = jnp.where(better, i, bi)
            bv = jnp.where(better, vals[i], bv)
            br = jnp.where(better, raws[i], br)
        return bi, br

    i1, w1 = argmax_first(sel, raw)
    masked = [jnp.where(i1 == i, -jnp.inf, sel[i]) for i in range(gp)]
    i2, w2 = argmax_first(masked, raw)
    tot = w1 + w2
    e1 = best_g * gp + i1
    e2 = best_g * gp + i2
    idx_ref[0:1, :] = e1
    idx_ref[1:2, :] = e2
    t = e1.shape[1]
    sub = lax.broadcasted_iota(jnp.int32, (8, t), 0)
    w8 = jnp.where(sub == 0, w1 / tot, jnp.where(sub == 1, w2 / tot, 0.0))
    wcol_ref[...] = w8.T
    eio = lax.broadcasted_iota(jnp.int32, (N_EXPERTS, t), 0)
    oh1, oh2 = eio == e1, eio == e2
    oh = jnp.where(oh1, 1.0, jnp.where(oh2, 1.0, 0.0))
    earlier = lax.broadcasted_iota(jnp.int32, (t, t), 0) < lax.broadcasted_iota(jnp.int32, (t, t), 1)
    local = _dot(oh.astype(BF16), jnp.where(earlier, 1.0, 0.0).astype(BF16))
    rank = local + run_ref[:, 0:1]
    rank_ref[0:1, :] = jnp.sum(jnp.where(oh1, rank, 0.0), axis=0, keepdims=True).astype(jnp.int32)
    rank_ref[1:2, :] = jnp.sum(jnp.where(oh2, rank, 0.0), axis=0, keepdims=True).astype(jnp.int32)
    run_ref[...] = run_ref[...] + jnp.sum(oh, axis=1, keepdims=True)
    cnt_ref[...] = run_ref[...]


def _router(x, mods, layer, segs, g2, rw_t, rb):
    n, d = x.shape
    ctx, lat = segs

    def mod_row(i):
        return jnp.where(i < ctx.tiles, ctx.mod0, lat.mod0 + (i - ctx.tiles) // lat.tiles_per_seq)

    return pl.pallas_call(
        _router_kernel,
        grid=(n // ROW_TILE,),
        in_specs=[
            pl.BlockSpec((ROW_TILE, d), lambda i: (i, 0)),
            pl.BlockSpec((None, None, 6, 1, d), lambda i: (layer, mod_row(i), 0, 0, 0)),
            pl.BlockSpec((1, d), lambda i: (0, 0)),
            pl.BlockSpec(rw_t.shape, lambda i: (0, 0)),
            pl.BlockSpec(rb.shape, lambda i: (0, 0)),
        ],
        out_specs=[
            pl.BlockSpec((ROW_TILE, d // 2), lambda i: (i, 0)),
            pl.BlockSpec((2, ROW_TILE), lambda i: (0, i)),
            pl.BlockSpec((2, ROW_TILE), lambda i: (0, i)),
            pl.BlockSpec((ROW_TILE, 8), lambda i: (i, 0)),
            pl.BlockSpec((N_EXPERTS, LANE), lambda i: (0, 0)),
        ],
        out_shape=[
            jax.ShapeDtypeStruct((n, d // 2), jnp.uint32),
            jax.ShapeDtypeStruct((2, n), jnp.int32),
            jax.ShapeDtypeStruct((2, n), jnp.int32),
            jax.ShapeDtypeStruct((n, 8), F32),
            jax.ShapeDtypeStruct((N_EXPERTS, LANE), F32),
        ],
        scratch_shapes=[pltpu.VMEM((N_EXPERTS, LANE), F32)],
        compiler_params=_cparams("arbitrary"),
        name="moe_router",
    )(x, mods, g2, rw_t, rb)


def _expert_kernel(be_ref, nb_ref, xs_ref, wg_ref, wu_ref, wd_ref, y_ref, wg_b, wu_b, wd_b):
    i = pl.program_id(0)
    prev = be_ref[jnp.maximum(i - 1, 0)]

    @pl.when(jnp.logical_or(i == 0, be_ref[i] != prev))
    def _():
        wg_b[...] = wg_ref[...].astype(BF16)
        wu_b[...] = wu_ref[...].astype(BF16)
        wd_b[...] = wd_ref[...].astype(BF16)

    @pl.when(i < nb_ref[0])
    def _():
        lo, hi = _unpack_halves(xs_ref[...])
        lo, hi = lo.astype(BF16), hi.astype(BF16)
        half = lo.shape[1]
        gate = _dot(lo, wg_b[:half, :]) + _dot(hi, wg_b[half:, :])
        up = _dot(lo, wu_b[:half, :]) + _dot(hi, wu_b[half:, :])
        hid = (gate * _sigmoid(gate)) * up
        y_ref[...] = _pack_halves(_dot(hid.astype(BF16), wd_b[...]))

    @pl.when(i >= nb_ref[0])
    def _():
        y_ref[...] = jnp.zeros_like(y_ref)


def _experts(xs, block_e, n_used, w_gate, w_up, w_down, layer):
    rows, half = xs.shape
    d = 2 * half
    de = w_gate.shape[-1]
    n_blocks = rows // MOE_ROWS
    grid_spec = pltpu.PrefetchScalarGridSpec(
        num_scalar_prefetch=2,
        grid=(n_blocks,),
        in_specs=[
            pl.BlockSpec((MOE_ROWS, half), lambda i, be, nb: (i, 0)),
            pl.BlockSpec((None, None, d, de), lambda i, be, nb: (layer, be[i], 0, 0)),
            pl.BlockSpec((None, None, d, de), lambda i, be, nb: (layer, be[i], 0, 0)),
            pl.BlockSpec((None, None, de, d), lambda i, be, nb: (layer, be[i], 0, 0)),
        ],
        out_specs=pl.BlockSpec((MOE_ROWS, half), lambda i, be, nb: (i, 0)),
        scratch_shapes=[pltpu.VMEM((d, de), BF16), pltpu.VMEM((d, de), BF16), pltpu.VMEM((de, d), BF16)],
    )
    return pl.pallas_call(
        _expert_kernel,
        grid_spec=grid_spec,
        out_shape=jax.ShapeDtypeStruct((rows, half), jnp.uint32),
        compiler_params=_cparams("arbitrary"),
        name="moe_experts",
    )(block_e, n_used, xs, w_gate, w_up, w_down)


def _combine_kernel(x_ref, m_ref, y_ref, w_ref, *rest, final):
    if final:
        fg_ref, o_ref = rest
    else:
        (o_ref,) = rest
    w = w_ref[...]
    lo0, hi0 = _unpack_halves(y_ref[0])
    lo1, hi1 = _unpack_halves(y_ref[1])
    w0, w1 = w[:, 0:1], w[:, 1:2]
    y = jnp.concatenate([w0 * lo0 + w1 * lo1, w0 * hi0 + w1 * hi1], axis=-1)
    x = x_ref[...] + m_ref[5] * y
    if final:
        x = _rms(x) * fg_ref[...]
    o_ref[...] = x


def _combine(x, mods, layer, seg, yg, wcol, final_g):
    n, d = x.shape
    final = final_g is not None
    rows = seg.batch * seg.seq
    x_spec = pl.BlockSpec((ROW_TILE, d), lambda i: (seg.tile0 + i, 0))
    in_specs = [x_spec, _mod_spec(layer, seg, d),
                pl.BlockSpec((2, ROW_TILE, d // 2), lambda i: (0, seg.tile0 + i, 0)),
                pl.BlockSpec((ROW_TILE, 8), lambda i: (seg.tile0 + i, 0))]
    args = [x, mods, yg, wcol]
    if final:
        in_specs.append(pl.BlockSpec((1, d), lambda i: (0, 0)))
        args.append(final_g)
        out_spec = pl.BlockSpec((ROW_TILE, d), lambda i: (i, 0))
        out_shape = jax.ShapeDtypeStruct((rows, d), F32)
        aliases = {}
    else:
        out_spec, out_shape, aliases = x_spec, jax.ShapeDtypeStruct((n, d), F32), {0: 0}
    return pl.pallas_call(
        functools.partial(_combine_kernel, final=final),
        grid=(seg.tiles,),
        in_specs=in_specs,
        out_specs=out_spec,
        out_shape=out_shape,
        input_output_aliases=aliases,
        compiler_params=_cparams("parallel"),
        name="moe_combine_final" if final else "moe_combine",
    )(*args)


def _sc_mesh():
    return plsc.VectorSubcoreMesh(core_axis_name="c", subcore_axis_name="s")


def _sc_worker_split(n):
    workers = SC_CORES * SC_SUBCORES
    per = n // workers
    assert per * workers == n and per % SC_CHUNK == 0
    return workers, per, per // SC_CHUNK


def _sc_dispatch(h, pos, rows):
    n, w = h.shape
    workers, per, chunks = _sc_worker_split(n)

    @functools.partial(
        pl.kernel, out_type=jax.ShapeDtypeStruct((rows, w), h.dtype), mesh=_sc_mesh(),
        scratch_types=[pltpu.VMEM((2, chunks, SC_CHUNK), jnp.int32), pltpu.VMEM((SC_CHUNK, w), h.dtype)],
        name="moe_dispatch_scatter")
    def scatter_rows(h_hbm, pos_hbm, xs_hbm, idx_v, rows_v):
        wid = lax.axis_index("s") * SC_CORES + lax.axis_index("c")
        pltpu.sync_copy(pos_hbm.at[0, wid], idx_v.at[0])
        pltpu.sync_copy(pos_hbm.at[1, wid], idx_v.at[1])

        @pl.loop(0, chunks)
        def _(c):
            pltpu.sync_copy(h_hbm.at[pl.ds(wid * per + c * SC_CHUNK, SC_CHUNK)], rows_v)
            pltpu.sync_copy(rows_v, xs_hbm.at[idx_v.at[0, c]])
            pltpu.sync_copy(rows_v, xs_hbm.at[idx_v.at[1, c]])

    return scatter_rows(h, pos.reshape(2, workers, chunks, SC_CHUNK))


def _sc_gather2(ys, pos):
    _, w = ys.shape
    n = pos.shape[1]
    workers, per, chunks = _sc_worker_split(n)

    @functools.partial(
        pl.kernel, out_type=jax.ShapeDtypeStruct((2, n, w), ys.dtype), mesh=_sc_mesh(),
        scratch_types=[pltpu.VMEM((2, chunks, SC_CHUNK), jnp.int32), pltpu.VMEM((SC_CHUNK, w), ys.dtype),
                       pltpu.SemaphoreType.DMA],
        name="moe_combine_gather")
    def gather_rows(ys_hbm, pos_hbm, out_hbm, idx_v, rows_v, sem):
        wid = lax.axis_index("s") * SC_CORES + lax.axis_index("c")
        pltpu.sync_copy(pos_hbm.at[0, wid], idx_v.at[0])
        pltpu.sync_copy(pos_hbm.at[1, wid], idx_v.at[1])

        @pl.loop(0, chunks)
        def _(c):
            for k in range(2):
                pltpu.async_copy(ys_hbm.at[idx_v.at[k, c]], rows_v, sem).wait()
                pltpu.sync_copy(rows_v, out_hbm.at[k, pl.ds(wid * per + c * SC_CHUNK, SC_CHUNK)])

    return gather_rows(ys, pos.reshape(2, workers, chunks, SC_CHUNK))


def _dispatch_plan(idx, rank, counts):
    n = idx.shape[1]
    padded = (counts + MOE_ROWS - 1) // MOE_ROWS * MOE_ROWS
    pad_end = jnp.cumsum(padded)
    pad_start = pad_end - padded
    experts = jnp.arange(N_EXPERTS, dtype=jnp.int32)
    start_of = jnp.sum(jnp.where(idx[..., None] == experts, pad_start, 0), axis=-1)
    pos = start_of + rank
    n_blocks = 2 * n // MOE_ROWS + N_EXPERTS
    starts = jnp.arange(n_blocks, dtype=jnp.int32) * MOE_ROWS
    block_e = jnp.minimum(jnp.sum(starts[:, None] >= pad_end[None, :], axis=1), N_EXPERTS - 1).astype(jnp.int32)
    n_used = (pad_end[-1] // MOE_ROWS).astype(jnp.int32).reshape(1)
    return pos, block_e, n_used, n_blocks * MOE_ROWS


def _moe(x, mods, layer, segs, g2, rw_t, rb, w_gate, w_up, w_down, final_g):
    h2p, idx, rank, wcol, cnt = _router(x, mods, layer, segs, g2, rw_t, rb)
    pos, block_e, n_used, rows = _dispatch_plan(idx, rank, cnt[:, 0].astype(jnp.int32))
    xs = _sc_dispatch(h2p, pos, rows)
    ys = _experts(xs, block_e, n_used, w_gate, w_up, w_down, layer)
    yg = _sc_gather2(ys, pos)
    if final_g is None:
        for seg in segs:
            x = _combine(x, mods, layer, seg, yg, wcol, None)
        return x
    return tuple(_combine(x, mods, layer, seg, yg, wcol, final_g) for seg in segs)


def _mla_rope_tables(t):
    axis_dim = MLA_ROPE // 2
    row = jnp.repeat(jnp.arange(t // GRID_W), GRID_W).astype(F32)
    col = jnp.tile(jnp.arange(GRID_W), t // GRID_W).astype(F32)
    inv = ROPE_BASE ** (-jnp.arange(0, axis_dim, 2, dtype=F32) / axis_dim)
    ar, ac = row[:, None] * inv[None, :], col[:, None] * inv[None, :]
    ones = jnp.ones((t, LANE - MLA_ROPE), F32)
    cos = jnp.concatenate([jnp.cos(ar), jnp.cos(ar), jnp.cos(ac), jnp.cos(ac), ones], axis=-1)
    sin = jnp.concatenate([-jnp.sin(ar), jnp.sin(ar), -jnp.sin(ac), jnp.sin(ac), 0.0 * ones], axis=-1)
    return cos, sin


def _ret_rot_tables(t, dk):
    inv = ROPE_BASE ** (-jnp.linspace(0.0, 1.0, dk // 2))
    ang = jnp.arange(t, dtype=F32)[:, None] * inv[None, :]
    return jnp.cos(ang), jnp.sin(ang)


def _dft_tables(n):
    k = np.arange(n, dtype=np.int64)
    ang = jnp.asarray((np.outer(k, k) % n).astype(np.float32)) * (2.0 * math.pi / n)
    return jnp.cos(ang), jnp.sin(ang)


def _mla_weights(w_in, q_g, kv_g, w_uq, w_ukv, w_o):
    d = w_in.shape[0]
    hd = MLA_NOPE + MLA_ROPE
    w_in_p = jnp.concatenate([w_in, jnp.zeros((d, LANE - MLA_ROPE), w_in.dtype)], axis=1)
    uq = w_uq.reshape(MLA_Q_LORA, MLA_HEADS, hd)
    uq = jnp.concatenate([uq, jnp.zeros((MLA_Q_LORA, MLA_HEADS, 2 * LANE - hd), uq.dtype)], axis=-1)
    ukv = w_ukv.reshape(MLA_KV_LORA, MLA_HEADS, MLA_NOPE + MLA_V)
    return {
        "w_in": w_in_p.astype(BF16),
        "q_g": q_g.reshape(1, -1),
        "kv_g": kv_g.reshape(1, -1),
        "w_q": uq.reshape(MLA_Q_LORA, MLA_HEADS * 2 * LANE).astype(BF16),
        "w_kn": ukv[..., :MLA_NOPE].reshape(MLA_KV_LORA, MLA_HEADS * MLA_NOPE).astype(BF16),
        "w_v": ukv[..., MLA_NOPE:].reshape(MLA_KV_LORA, MLA_HEADS * MLA_V).astype(BF16),
        "w_o": w_o.astype(BF16),
    }


def kernel(x_prompt, x_sample, cache_mla, state_ret, c, c_ctx, norm1_g, norm2_g, ada_w, ada_b, final_norm_g,
           mla_w_in, mla_q_norm_g, mla_kv_norm_g, mla_w_uq, mla_w_ukv, mla_w_o, ret_w_in, ret_decay_f,
           ret_decay_b, ret_w_o, fnet_w, router_w, router_b, moe_w_gate, moe_w_up, moe_w_down):
    b_ctx, t_ctx, d = x_prompt.shape
    b_lat, t_lat, _ = x_sample.shape
    depth = ada_w.shape[0]
    assert t_ctx == ROW_TILE and t_lat % ROW_TILE == 0 and b_lat + 1 <= 8
    n_ctx = b_ctx * t_ctx
    ctx = _Seg(0, b_ctx, t_ctx, 0, False)
    lat = _Seg(n_ctx, b_lat, t_lat, 1, True)
    segs = (ctx, lat)

    x = jnp.concatenate([x_prompt.reshape(n_ctx, d), x_sample.reshape(b_lat * t_lat, d)], axis=0)
    cond8 = jnp.concatenate([c_ctx[None, :], c, jnp.zeros((8 - 1 - b_lat, d), F32)], axis=0)
    mods = _modulation_all(cond8, ada_w, ada_b).reshape(depth, 8, 6, 1, d)

    rw_t = router_w.T
    rb = router_b.reshape(N_EXPERTS, 1).astype(F32)
    final_g = final_norm_g.reshape(1, d)
    dk = ret_w_in.shape[2] // (8 * RET_HEADS)
    dv = 2 * dk

    caches, states = [], []
    counters = [0, 0, 0]
    for layer in range(depth):
        kind = layer % 3
        j = counters[kind]
        counters[kind] += 1
        g1 = norm1_g[layer].reshape(1, d)
        g2 = norm2_g[layer].reshape(1, d)
        if kind == 0:
            w = _mla_weights(mla_w_in[j], mla_q_norm_g[j], mla_kv_norm_g[j], mla_w_uq[j], mla_w_ukv[j],
                             mla_w_o[j])
            qc, kc, vc, cache = _mla_proj(x, mods, layer, ctx, g1, w, None)
            ql, kl, vl = _mla_proj(x, mods, layer, lat, g1, w, _mla_rope_tables(t_lat))
            past = cache_mla.shape[2]
            cpad = jnp.pad(cache_mla[:, j].reshape(b_lat * past, -1), ((0, 0), (0, LANE - MLA_ROPE)))
            kp, vp = _cache_kv(cpad, w)
            x = _attention(x, mods, layer, ctx, qc, [(kc, vc, t_ctx)], w["w_o"])
            x = _attention(x, mods, layer, lat, ql, [(kl, vl, t_lat), (kp, vp, past)], w["w_o"])
            caches.append(cache.reshape(b_ctx, t_ctx, -1))
        elif kind == 1:
            w_in = ret_w_in[j]
            qk = RET_HEADS * dk
            k_scale = jnp.concatenate([jnp.ones((qk,), F32), jnp.full((qk,), dk ** -0.5, F32),
                                       jnp.ones((w_in.shape[1] - 2 * qk,), F32)])
            w_in_b = (w_in * k_scale[None, :]).astype(BF16)
            w_o_b = ret_w_o[j].astype(BF16)
            zc = _ret_proj(x, mods, layer, ctx, g1, w_in_b, None, dk)
            zl = _ret_proj(x, mods, layer, lat, g1, w_in_b, _ret_rot_tables(t_lat, dk), dk)
            yc, s_ctx = _ret_scan(zc, ctx, ret_decay_f[j], ret_decay_b[j], None, True, dk, dv)
            yl, _ = _ret_scan(zl, lat, ret_decay_f[j], ret_decay_b[j], state_ret[:, j], False, dk, dv)
            x = _matmul_residual(x, mods, layer, ctx, yc, w_o_b)
            x = _matmul_residual(x, mods, layer, lat, yl, w_o_b)
            states.append(s_ctx)
        else:
            gd = d // FNET_GROUPS
            cc, sc = _dft_tables(gd)
            cs = jnp.concatenate([cc, sc], axis=1).astype(BF16)
            w_b = fnet_w[j].astype(BF16)
            for seg in segs:
                ct, st = _dft_tables(seg.seq)
                ac, as_ = _fnet_a(x, mods, layer, seg, g1, cs)
                x = _fnet_b(x, mods, layer, seg, ac, as_, ct.astype(BF16), st.astype(BF16), w_b,
                            (seg.seq * gd) ** -0.5)
        last = layer == depth - 1
        x = _moe(x, mods, layer, segs, g2, rw_t, rb, moe_w_gate, moe_w_up, moe_w_down,
                 final_g if last else None)

    y_prompt, y_sample = x
    new_cache = jnp.stack(caches, axis=1)
    new_state = jnp.stack(states, axis=1)
    return (y_prompt.reshape(b_ctx, t_ctx, d), y_sample.reshape(b_lat, t_lat, d), new_cache, new_state)
```

```python
import functools
import math

import jax
import jax.numpy as jnp
import numpy as np
from jax import lax
from jax.experimental import pallas as pl
from jax.experimental.pallas import tpu as pltpu
from jax.experimental.pallas import tpu_sc as plsc

F32 = jnp.float32
BF16 = jnp.bfloat16

GRID_W = 64
MLA_HEADS = 8
MLA_NOPE = 128
MLA_ROPE = 64
MLA_V = 128
MLA_Q_LORA = 384
MLA_KV_LORA = 256
ROPE_BASE = 10000.0
RET_HEADS = 4
RET_CHUNK = 256
FNET_GROUPS = 4
N_EXPERTS = 16
N_EXPERT_GROUPS = 4
EXPERTS_PER_GROUP = 4
D_EXPERT = 512
NORM_EPS = 1e-6

LANE = 128
ROW_TILE = 256
WIDE_TILE = 512
MOE_ROWS = 512
VMEM_LIMIT = 56 * 1024 * 1024
SC_CORES = 2
SC_SUBCORES = 16
SC_CHUNK = 128


def _cparams(*sem):
    return pltpu.CompilerParams(dimension_semantics=sem, vmem_limit_bytes=VMEM_LIMIT)


def _sigmoid(x):
    return 1.0 / (1.0 + jnp.exp(-x))


def _rms(x):
    return x * lax.rsqrt(jnp.mean(x * x, axis=-1, keepdims=True) + NORM_EPS)


def _modulate(x, g, shift, scale):
    return (_rms(x) * g) * (1.0 + scale) + shift


def _dot(a, b):
    return jnp.dot(a, b, preferred_element_type=F32)


def _dot_nt(a, b):
    return lax.dot_general(a, b, (((1,), (1,)), ((), ())), preferred_element_type=F32)


def _mod_kernel(c_ref, w_ref, b_ref, o_ref):
    c = c_ref[...]
    s = (c * _sigmoid(c)).astype(BF16)
    o_ref[...] = _dot(s, w_ref[...].astype(BF16)) + b_ref[...]


def _modulation_all(cond8, ada_w, ada_b):
    depth, d, d6 = ada_w.shape
    tn = d6 // 4
    return pl.pallas_call(
        _mod_kernel,
        grid=(depth, d6 // tn),
        in_specs=[
            pl.BlockSpec((8, d), lambda l, n: (0, 0)),
            pl.BlockSpec((None, d, tn), lambda l, n: (l, 0, n)),
            pl.BlockSpec((None, 1, tn), lambda l, n: (l, 0, n)),
        ],
        out_specs=pl.BlockSpec((None, 8, tn), lambda l, n: (l, 0, n)),
        out_shape=jax.ShapeDtypeStruct((depth, 8, d6), F32),
        compiler_params=_cparams("parallel", "parallel"),
        name="modulation",
    )(cond8, ada_w, ada_b.reshape(depth, 1, d6))


class _Seg:
    def __init__(self, row0, batch, seq, mod0, per_batch_mod):
        self.row0, self.batch, self.seq = row0, batch, seq
        self.mod0, self.per_batch_mod = mod0, per_batch_mod
        self.tiles = batch * seq // ROW_TILE
        self.tile0 = row0 // ROW_TILE
        self.tiles_per_seq = seq // ROW_TILE

    def mod_row(self, tile):
        if self.per_batch_mod:
            return self.mod0 + tile // self.tiles_per_seq
        return self.mod0


def _mod_spec(layer, seg, d, tile_of=lambda *a: a[0]):
    return pl.BlockSpec((None, None, 6, 1, d), lambda *a: (layer, seg.mod_row(tile_of(*a)), 0, 0, 0))


def _rope_partner(x):
    lane = lax.broadcasted_iota(jnp.int32, x.shape, 1)
    first = (lane % 32) < 16
    return jnp.where(first, pltpu.roll(x, LANE - 16, 1), pltpu.roll(x, 16, 1))


def _mla_proj_kernel(*refs, rope, scale):
    if rope:
        (x_ref, m_ref, g_ref, win_ref, qg_ref, kvg_ref, wq_ref, wkn_ref, wv_ref, cos_ref, sin_ref,
         q_ref, k_ref, v_ref) = refs
    else:
        (x_ref, m_ref, g_ref, win_ref, qg_ref, kvg_ref, wq_ref, wkn_ref, wv_ref,
         q_ref, k_ref, v_ref, cache_ref) = refs
    h = _modulate(x_ref[...], g_ref[...], m_ref[0], m_ref[1]).astype(BF16)
    z = _dot(h, win_ref[...])
    cq = z[:, :MLA_Q_LORA]
    ckv = z[:, MLA_Q_LORA:MLA_Q_LORA + MLA_KV_LORA]
    kpe = z[:, MLA_Q_LORA + MLA_KV_LORA:]
    cqn = (_rms(cq) * qg_ref[...]).astype(BF16)
    ckvn = _rms(ckv) * kvg_ref[...]
    ckvb = ckvn.astype(BF16)
    q = _dot(cqn, wq_ref[...]) * scale
    kn = _dot(ckvb, wkn_ref[...])
    v_ref[...] = _dot(ckvb, wv_ref[...]).astype(BF16)
    if rope:
        cos, sin = cos_ref[...], sin_ref[...]
        kpe = kpe * cos + _rope_partner(kpe) * sin
    else:
        cache_ref[:, :MLA_KV_LORA] = ckvn
        cache_ref[:, MLA_KV_LORA:] = kpe[:, :MLA_ROPE]
    kpe_b = kpe.astype(BF16)
    for hd in range(MLA_HEADS):
        lo = hd * 2 * LANE
        q_ref[:, lo:lo + LANE] = q[:, lo:lo + LANE].astype(BF16)
        qr = q[:, lo + LANE:lo + 2 * LANE]
        if rope:
            qr = qr * cos + _rope_partner(qr) * sin
        q_ref[:, lo + LANE:lo + 2 * LANE] = qr.astype(BF16)
        k_ref[:, lo:lo + LANE] = kn[:, hd * LANE:(hd + 1) * LANE].astype(BF16)
        k_ref[:, lo + LANE:lo + 2 * LANE] = kpe_b


def _mla_proj(x, mods, layer, seg, g1, w, rope_tabs):
    n, d = x.shape
    rope = rope_tabs is not None
    rows = seg.batch * seg.seq
    hq = MLA_HEADS * 2 * LANE
    hv = MLA_HEADS * MLA_V
    const = lambda i: (0, 0)
    in_specs = [
        pl.BlockSpec((ROW_TILE, d), lambda i: (seg.tile0 + i, 0)),
        _mod_spec(layer, seg, d),
        pl.BlockSpec((1, d), const),
        pl.BlockSpec(w["w_in"].shape, const),
        pl.BlockSpec((1, MLA_Q_LORA), const),
        pl.BlockSpec((1, MLA_KV_LORA), const),
        pl.BlockSpec(w["w_q"].shape, const),
        pl.BlockSpec(w["w_kn"].shape, const),
        pl.BlockSpec(w["w_v"].shape, const),
    ]
    args = [x, mods, g1, w["w_in"], w["q_g"], w["kv_g"], w["w_q"], w["w_kn"], w["w_v"]]
    out_specs = [
        pl.BlockSpec((ROW_TILE, hq), lambda i: (i, 0)),
        pl.BlockSpec((ROW_TILE, hq), lambda i: (i, 0)),
        pl.BlockSpec((ROW_TILE, hv), lambda i: (i, 0)),
    ]
    out_shape = [
        jax.ShapeDtypeStruct((rows, hq), BF16),
        jax.ShapeDtypeStruct((rows, hq), BF16),
        jax.ShapeDtypeStruct((rows, hv), BF16),
    ]
    if rope:
        tab = pl.BlockSpec((ROW_TILE, LANE), lambda i: (i % seg.tiles_per_seq, 0))
        in_specs += [tab, tab]
        args += list(rope_tabs)
    else:
        cw = MLA_KV_LORA + MLA_ROPE
        out_specs.append(pl.BlockSpec((ROW_TILE, cw), lambda i: (i, 0)))
        out_shape.append(jax.ShapeDtypeStruct((rows, cw), F32))
    return pl.pallas_call(
        functools.partial(_mla_proj_kernel, rope=rope, scale=(MLA_NOPE + MLA_ROPE) ** -0.5),
        grid=(seg.tiles,),
        in_specs=in_specs,
        out_specs=out_specs,
        out_shape=out_shape,
        compiler_params=_cparams("parallel"),
        name="mla_proj_lat" if rope else "mla_proj_ctx",
    )(*args)


def _cache_kv_kernel(c_ref, wkn_ref, wv_ref, k_ref, v_ref):
    c = c_ref[...]
    ckv = c[:, :MLA_KV_LORA].astype(BF16)
    kpe_b = c[:, MLA_KV_LORA:].astype(BF16)
    kn = _dot(ckv, wkn_ref[...])
    v_ref[...] = _dot(ckv, wv_ref[...]).astype(BF16)
    for hd in range(MLA_HEADS):
        lo = hd * 2 * LANE
        k_ref[:, lo:lo + LANE] = kn[:, hd * LANE:(hd + 1) * LANE].astype(BF16)
        k_ref[:, lo + LANE:lo + 2 * LANE] = kpe_b


def _cache_kv(cache_pad, w):
    rows, cw = cache_pad.shape
    hq = MLA_HEADS * 2 * LANE
    hv = MLA_HEADS * MLA_V
    const = lambda i: (0, 0)
    return pl.pallas_call(
        _cache_kv_kernel,
        grid=(rows // ROW_TILE,),
        in_specs=[
            pl.BlockSpec((ROW_TILE, cw), lambda i: (i, 0)),
            pl.BlockSpec(w["w_kn"].shape, const),
            pl.BlockSpec(w["w_v"].shape, const),
        ],
        out_specs=[pl.BlockSpec((ROW_TILE, hq), lambda i: (i, 0)),
                   pl.BlockSpec((ROW_TILE, hv), lambda i: (i, 0))],
        out_shape=[jax.ShapeDtypeStruct((rows, hq), BF16), jax.ShapeDtypeStruct((rows, hv), BF16)],
        compiler_params=_cparams("parallel"),
        name="mla_cache_kv",
    )(cache_pad, w["w_kn"], w["w_v"])


def _attn_kernel(*refs, n_parts):
    q_ref = refs[0]
    kv_refs = refs[1:1 + 2 * n_parts]
    wo_ref, x_ref, m_ref, o_ref, acc_ref = refs[1 + 2 * n_parts:]
    for hd in range(MLA_HEADS):
        qh = q_ref[:, hd * 2 * LANE:(hd + 1) * 2 * LANE]
        scores = [_dot_nt(qh, kv_refs[2 * p][:, hd * 2 * LANE:(hd + 1) * 2 * LANE]) for p in range(n_parts)]
        mx = scores[0].max(axis=-1, keepdims=True)
        for s in scores[1:]:
            mx = jnp.maximum(mx, s.max(axis=-1, keepdims=True))
        den = None
        out = None
        for p, s in enumerate(scores):
            e = jnp.exp(s - mx)
            den = e.sum(axis=-1, keepdims=True) if den is None else den + e.sum(axis=-1, keepdims=True)
            pv = _dot(e.astype(BF16), kv_refs[2 * p + 1][:, hd * MLA_V:(hd + 1) * MLA_V])
            out = pv if out is None else out + pv
        acc_ref[:, hd * MLA_V:(hd + 1) * MLA_V] = (out / den).astype(BF16)
    y = _dot(acc_ref[...], wo_ref[...])
    o_ref[...] = x_ref[...] + m_ref[2] * y


def _attention(x, mods, layer, seg, q, kv_parts, w_o):
    n, d = x.shape
    hq = MLA_HEADS * 2 * LANE
    hv = MLA_HEADS * MLA_V
    tps = seg.tiles_per_seq
    in_specs = [pl.BlockSpec((ROW_TILE, hq), lambda b, i: (b * tps + i, 0))]
    args = [q]
    for k, v, rows in kv_parts:
        in_specs += [pl.BlockSpec((rows, hq), lambda b, i: (b, 0)), pl.BlockSpec((rows, hv), lambda b, i: (b, 0))]
        args += [k, v]
    x_spec = pl.BlockSpec((ROW_TILE, d), lambda b, i: (seg.tile0 + b * tps + i, 0))
    in_specs += [
        pl.BlockSpec(w_o.shape, lambda b, i: (0, 0)),
        x_spec,
        _mod_spec(layer, seg, d, tile_of=lambda b, i: b * tps + i),
    ]
    args += [w_o, x, mods]
    return pl.pallas_call(
        functools.partial(_attn_kernel, n_parts=len(kv_parts)),
        grid=(seg.batch, tps),
        in_specs=in_specs,
        out_specs=x_spec,
        out_shape=jax.ShapeDtypeStruct((n, d), F32),
        scratch_shapes=[pltpu.VMEM((ROW_TILE, hv), BF16)],
        input_output_aliases={len(args) - 2: 0},
        compiler_params=_cparams("parallel", "arbitrary"),
        name="mla_attention",
    )(*args)


RET_COL = 1024


RET_ROWS = 512


def _ret_proj_kernel(*refs, kind, dk):
    if kind == "rotary":
        x_ref, m_ref, g_ref, w_ref, cos_ref, sin_ref, z_ref, h_ref = refs
    else:
        x_ref, m_ref, g_ref, w_ref, z_ref, h_ref = refs

    @pl.when(pl.program_id(1) == 0)
    def _():
        h_ref[...] = _modulate(x_ref[...], g_ref[...], m_ref[0], m_ref[1]).astype(BF16)

    acc = _dot(h_ref[...], w_ref[...])
    if kind == "rotary":
        cos, sin = cos_ref[...], sin_ref[...]
        half = dk // 2
        for hd in range(RET_COL // dk):
            x1 = acc[:, hd * dk:hd * dk + half]
            x2 = acc[:, hd * dk + half:(hd + 1) * dk]
            z_ref[:, hd * dk:hd * dk + half] = (x1 * cos - x2 * sin).astype(BF16)
            z_ref[:, hd * dk + half:(hd + 1) * dk] = (x1 * sin + x2 * cos).astype(BF16)
    elif kind == "silu":
        z_ref[...] = (acc * _sigmoid(acc)).astype(BF16)
    else:
        z_ref[...] = acc.astype(BF16)


def _ret_proj(x, mods, layer, seg, g1, w_in, kind, rot_tabs, dk):
    n, d = x.shape
    rotary = kind == "rotary"
    rows = seg.batch * seg.seq
    tm = RET_ROWS
    ncol = w_in.shape[1]
    tps = max(seg.seq // tm, 1)
    in_specs = [
        pl.BlockSpec((tm, d), lambda i, j: (seg.row0 // tm + i, 0)),
        pl.BlockSpec((None, None, 6, 1, d),
                     lambda i, j: (layer, seg.mod0 + (i * tm // seg.seq if seg.per_batch_mod else 0), 0, 0, 0)),
        pl.BlockSpec((1, d), lambda i, j: (0, 0)),
        pl.BlockSpec((d, RET_COL), lambda i, j: (0, j)),
    ]
    args = [x, mods, g1, w_in]
    if rotary:
        tab = pl.BlockSpec((tm, dk // 2), lambda i, j: (i % tps, 0))
        in_specs += [tab, tab]
        args += list(rot_tabs)
    return pl.pallas_call(
        functools.partial(_ret_proj_kernel, kind=kind, dk=dk),
        grid=(rows // tm, ncol // RET_COL),
        in_specs=in_specs,
        out_specs=pl.BlockSpec((tm, RET_COL), lambda i, j: (i, j)),
        out_shape=jax.ShapeDtypeStruct((rows, ncol), BF16),
        scratch_shapes=[pltpu.VMEM((tm, d), BF16)],
        compiler_params=_cparams("parallel", "arbitrary"),
        name="ret_proj_" + kind,
    )(*args)


def _log_sigmoid(x):
    return jnp.minimum(x, 0.0) - jnp.log(1.0 + jnp.exp(-jnp.abs(x)))


def _ret_scan_kernel(*refs, has_s0, emit_state, n_chunks):
    refs = list(refs)
    lf_ref, lb_ref, q_ref, k_ref, v_ref, gf_ref, gb_ref = refs[:7]
    pos = 7
    s0_ref = None
    if has_s0:
        s0_ref = refs[pos]
        pos += 1
    y_ref = refs[pos]
    pos += 1
    sout_ref = None
    if emit_state:
        sout_ref = refs[pos]
        pos += 1
    s_ref, yf_ref = refs[pos:]
    c = RET_CHUNK
    ii = lax.broadcasted_iota(jnp.int32, (c, c), 0).astype(F32)
    jj = lax.broadcasted_iota(jnp.int32, (c, c), 1).astype(F32)
    idx = lax.broadcasted_iota(jnp.int32, (c, 1), 0).astype(F32)

    for direction in range(2):
        fwd = direction == 0
        lg = _log_sigmoid((lf_ref if fwd else lb_ref)[...])
        rel = (ii - jj) if fwd else (jj - ii)
        keep = rel >= 0
        decay_in = jnp.where(keep, jnp.exp(jnp.where(keep, rel, 0.0) * lg), 0.0)
        decay_q = jnp.exp(((idx + 1.0) if fwd else (c - idx)) * lg)
        decay_k = jnp.exp(((c - 1.0 - idx) if fwd else idx) * lg)
        decay_c = jnp.exp(c * lg)
        g_ref = gf_ref if fwd else gb_ref

        def chunk(cc, state, fwd=fwd, decay_in=decay_in, decay_q=decay_q, decay_k=decay_k, decay_c=decay_c,
                  g_ref=g_ref):
            r0 = cc * c if isinstance(cc, int) else pl.multiple_of(cc * c, c)
            qc = q_ref[pl.ds(r0, c), :]
            kc = k_ref[pl.ds(r0, c), :]
            vc = v_ref[pl.ds(r0, c), :]
            sc = _dot_nt(qc, kc) * decay_in
            out = _dot(sc.astype(BF16), vc)
            kd_t = (kc.astype(F32) * decay_k).T.astype(BF16)
            new_s = _dot(kd_t, vc)
            if state is not None:
                out = out + _dot((qc.astype(F32) * decay_q).astype(BF16), state.astype(BF16))
                new_s = decay_c * state + new_s
            s_ref[...] = new_s
            o = _rms(out) * g_ref[pl.ds(r0, c), :].astype(F32)
            if fwd:
                yf_ref[pl.ds(r0, c), :] = o
            else:
                y_ref[pl.ds(r0, c), :] = (yf_ref[pl.ds(r0, c), :] + o).astype(BF16)

        chunk(0 if fwd else n_chunks - 1, s0_ref[direction] if has_s0 else None)

        def step(ci, carry, fwd=fwd, chunk=chunk):
            chunk(ci if fwd else n_chunks - 1 - ci, s_ref[...])
            return carry

        lax.fori_loop(1, n_chunks, step, 0)
        if emit_state:
            sout_ref[direction] = s_ref[...]


def _ret_scan(qk, v, g, seg, logit_f, logit_b, s0, emit_state, dk, dv):
    rows = seg.batch * seg.seq
    t = seg.seq
    hh = RET_HEADS
    in_specs = [
        pl.BlockSpec((None, 1, 1), lambda b, h: (h, 0, 0)),
        pl.BlockSpec((None, 1, 1), lambda b, h: (h, 0, 0)),
        pl.BlockSpec((t, dk), lambda b, h: (b, h)),
        pl.BlockSpec((t, dk), lambda b, h: (b, hh + h)),
        pl.BlockSpec((t, dv), lambda b, h: (b, h)),
        pl.BlockSpec((t, dv), lambda b, h: (b, h)),
        pl.BlockSpec((t, dv), lambda b, h: (b, hh + h)),
    ]
    args = [logit_f.reshape(hh, 1, 1), logit_b.reshape(hh, 1, 1), qk, qk, v, g, g]
    state_spec = pl.BlockSpec((None, 2, None, dk, dv), lambda b, h: (b, 0, h, 0, 0))
    if s0 is not None:
        in_specs.append(state_spec)
        args.append(s0)
    out_specs = [pl.BlockSpec((t, dv), lambda b, h: (b, h))]
    out_shape = [jax.ShapeDtypeStruct((rows, hh * dv), BF16)]
    if emit_state:
        out_specs.append(state_spec)
        out_shape.append(jax.ShapeDtypeStruct((seg.batch, 2, hh, dk, dv), F32))
    res = pl.pallas_call(
        functools.partial(_ret_scan_kernel, has_s0=s0 is not None, emit_state=emit_state,
                          n_chunks=t // RET_CHUNK),
        grid=(seg.batch, hh),
        in_specs=in_specs,
        out_specs=out_specs,
        out_shape=out_shape,
        scratch_shapes=[pltpu.VMEM((dk, dv), F32), pltpu.VMEM((t, dv), F32)],
        compiler_params=_cparams("parallel", "parallel"),
        name="ret_scan",
    )(*args)
    return res if emit_state else (res[0], None)


def _mm_res_kernel(a_ref, w_ref, x_ref, m_ref, o_ref):
    o_ref[...] = x_ref[...] + m_ref[2] * _dot(a_ref[...], w_ref[...])


def _matmul_residual(x, mods, layer, seg, a, w):
    n, d = x.shape
    x_spec = pl.BlockSpec((ROW_TILE, d), lambda i: (seg.tile0 + i, 0))
    return pl.pallas_call(
        _mm_res_kernel,
        grid=(seg.tiles,),
        in_specs=[
            pl.BlockSpec((ROW_TILE, a.shape[1]), lambda i: (i, 0)),
            pl.BlockSpec(w.shape, lambda i: (0, 0)),
            x_spec,
            _mod_spec(layer, seg, d),
        ],
        out_specs=x_spec,
        out_shape=jax.ShapeDtypeStruct((n, d), F32),
        input_output_aliases={2: 0},
        compiler_params=_cparams("parallel"),
        name="matmul_residual",
    )(a, w, x, mods)


def _fnet_a_kernel(x_ref, m_ref, g_ref, cs_ref, ac_ref, as_ref, *, gd):
    h = _modulate(x_ref[...], g_ref[...], m_ref[0], m_ref[1]).astype(BF16)
    cs = cs_ref[...]
    for g in range(FNET_GROUPS):
        a = _dot(h[:, g * gd:(g + 1) * gd], cs)
        ac_ref[:, g * gd:(g + 1) * gd] = a[:, :gd].astype(BF16)
        as_ref[:, g * gd:(g + 1) * gd] = a[:, gd:].astype(BF16)


def _fnet_a(x, mods, layer, seg, g1, cs):
    n, d = x.shape
    rows = seg.batch * seg.seq
    out = pl.BlockSpec((ROW_TILE, d), lambda i: (i, 0))
    return pl.pallas_call(
        functools.partial(_fnet_a_kernel, gd=d // FNET_GROUPS),
        grid=(seg.tiles,),
        in_specs=[
            pl.BlockSpec((ROW_TILE, d), lambda i: (seg.tile0 + i, 0)),
            _mod_spec(layer, seg, d),
            pl.BlockSpec((1, d), lambda i: (0, 0)),
            pl.BlockSpec(cs.shape, lambda i: (0, 0)),
        ],
        out_specs=[out, out],
        out_shape=[jax.ShapeDtypeStruct((rows, d), BF16)] * 2,
        compiler_params=_cparams("parallel"),
        name="fnet_channel_dft",
    )(x, mods, g1, cs)


def _fnet_b_kernel(ct_ref, st_ref, ac_ref, as_ref, w_ref, x_ref, m_ref, o_ref, *, norm):
    f = (_dot(ct_ref[...], ac_ref[...]) - _dot(st_ref[...], as_ref[...])) * norm
    o_ref[...] = x_ref[...] + m_ref[2] * _dot(f.astype(BF16), w_ref[...])


def _fnet_b(x, mods, layer, seg, ac, as_, ct, st, w, norm):
    n, d = x.shape
    t = seg.seq
    tps = seg.tiles_per_seq
    x_spec = pl.BlockSpec((ROW_TILE, d), lambda b, i: (seg.tile0 + b * tps + i, 0))
    tab = pl.BlockSpec((ROW_TILE, t), lambda b, i: (i, 0))
    seq = pl.BlockSpec((t, d), lambda b, i: (b, 0))
    return pl.pallas_call(
        functools.partial(_fnet_b_kernel, norm=norm),
        grid=(seg.batch, tps),
        in_specs=[tab, tab, seq, seq, pl.BlockSpec(w.shape, lambda b, i: (0, 0)), x_spec,
                  _mod_spec(layer, seg, d, tile_of=lambda b, i: b * tps + i)],
        out_specs=x_spec,
        out_shape=jax.ShapeDtypeStruct((n, d), F32),
        input_output_aliases={5: 0},
        compiler_params=_cparams("parallel", "arbitrary"),
        name="fnet_position_dft",
    )(ct, st, ac, as_, w, x, mods)


def _pack_halves(a):
    w = a.shape[1] // 2
    bits = lambda v: lax.bitcast_convert_type(v.astype(BF16).astype(F32), jnp.uint32)
    return (bits(a[:, :w]) >> 16) | (bits(a[:, w:]) & jnp.uint32(0xFFFF0000))


def _unpack_halves(p):
    lo = lax.bitcast_convert_type(p << 16, F32)
    hi = lax.bitcast_convert_type(p & jnp.uint32(0xFFFF0000), F32)
    return lo, hi


def _router_kernel(x_ref, m_ref, g_ref, rwhi_ref, rwlo_ref, rb_ref, h_ref, idx_ref, rank_ref, wcol_ref, cnt_ref,
                   run_ref, tri_ref):
    step = pl.program_id(0)

    @pl.when(step == 0)
    def _():
        run_ref[...] = jnp.zeros_like(run_ref)
        tt = tri_ref.shape[0]
        earlier = lax.broadcasted_iota(jnp.int32, (tt, tt), 0) < lax.broadcasted_iota(jnp.int32, (tt, tt), 1)
        tri_ref[...] = jnp.where(earlier, 1.0, 0.0).astype(BF16)

    h = _modulate(x_ref[...], g_ref[...], m_ref[3], m_ref[4])
    h_ref[...] = _pack_halves(h)
    h_hi = h.astype(BF16)
    h_lo = (h - h_hi.astype(F32)).astype(BF16)
    logits = _dot_nt(rwhi_ref[...], h_hi) + (_dot_nt(rwhi_ref[...], h_lo) + _dot_nt(rwlo_ref[...], h_hi))
    sc = _sigmoid(logits)
    gr = sc + rb_ref[...]
    gp = EXPERTS_PER_GROUP
    row = lambda a, e: a[e:e + 1, :]
    best_g = None
    for g in range(N_EXPERT_GROUPS):
        vals = [row(gr, g * gp + i) for i in range(gp)]
        gs = None
        for i in range(gp):
            for j in range(i + 1, gp):
                pair = vals[i] + vals[j]
                gs = pair if gs is None else jnp.maximum(gs, pair)
        if best_g is None:
            best_g, best_v = jnp.zeros(gs.shape, jnp.int32), gs
        else:
            better = gs > best_v
            best_g = jnp.where(better, g, best_g)
            best_v = jnp.where(better, gs, best_v)
    sel, raw = [], []
    for i in range(gp):
        s_i, r_i = row(gr, i), row(sc, i)
        for g in range(1, N_EXPERT_GROUPS):
            s_i = jnp.where(best_g == g, row(gr, g * gp + i), s_i)
            r_i = jnp.where(best_g == g, row(sc, g * gp + i), r_i)
        sel.append(s_i)
        raw.append(r_i)

    def argmax_first(vals, raws):
        bi, bv, br = jnp.zeros(vals[0].shape, jnp.int32), vals[0], raws[0]
        for i in range(1, len(vals)):
            better = vals[i] > bv
            bi = jnp.where(better, i, bi)
            bv = jnp.where(better, vals[i], bv)
            br = jnp.where(better, raws[i], br)
        return bi, br

    i1, w1 = argmax_first(sel, raw)
    masked = [jnp.where(i1 == i, -jnp.inf, sel[i]) for i in range(gp)]
    i2, w2 = argmax_first(masked, raw)
    tot = w1 + w2
    e1 = best_g * gp + i1
    e2 = best_g * gp + i2
    idx_ref[0:1, :] = e1
    idx_ref[1:2, :] = e2
    t = e1.shape[1]
    sub = lax.broadcasted_iota(jnp.int32, (8, t), 0)
    w8 = jnp.where(sub == 0, w1 / tot, jnp.where(sub == 1, w2 / tot, 0.0))
    wcol_ref[...] = w8.T
    eio = lax.broadcasted_iota(jnp.int32, (N_EXPERTS, t), 0)
    oh1, oh2 = eio == e1, eio == e2
    oh = jnp.where(oh1, 1.0, jnp.where(oh2, 1.0, 0.0))
    local = _dot(oh.astype(BF16), tri_ref[...])
    rank = local + run_ref[:, 0:1]
    rank_ref[0:1, :] = jnp.sum(jnp.where(oh1, rank, 0.0), axis=0, keepdims=True).astype(jnp.int32)
    rank_ref[1:2, :] = jnp.sum(jnp.where(oh2, rank, 0.0), axis=0, keepdims=True).astype(jnp.int32)
    run_ref[...] = run_ref[...] + jnp.sum(oh, axis=1, keepdims=True)
    cnt_ref[...] = run_ref[...]


def _wide_mod_row(segs, tm):
    ctx, lat = segs
    ctx_tiles = ctx.batch * ctx.seq // tm
    assert ctx_tiles * tm == ctx.batch * ctx.seq and lat.seq % tm == 0
    return lambda i: jnp.where(i < ctx_tiles, ctx.mod0, lat.mod0 + (i - ctx_tiles) // (lat.seq // tm))


def _router(x, mods, layer, segs, g2, rw_hi, rw_lo, rb):
    n, d = x.shape
    tm = WIDE_TILE
    mod_row = _wide_mod_row(segs, tm)
    return pl.pallas_call(
        _router_kernel,
        grid=(n // tm,),
        in_specs=[
            pl.BlockSpec((tm, d), lambda i: (i, 0)),
            pl.BlockSpec((None, None, 6, 1, d), lambda i: (layer, mod_row(i), 0, 0, 0)),
            pl.BlockSpec((1, d), lambda i: (0, 0)),
            pl.BlockSpec(rw_hi.shape, lambda i: (0, 0)),
            pl.BlockSpec(rw_lo.shape, lambda i: (0, 0)),
            pl.BlockSpec(rb.shape, lambda i: (0, 0)),
        ],
        out_specs=[
            pl.BlockSpec((tm, d // 2), lambda i: (i, 0)),
            pl.BlockSpec((2, tm), lambda i: (0, i)),
            pl.BlockSpec((2, tm), lambda i: (0, i)),
            pl.BlockSpec((tm, 8), lambda i: (i, 0)),
            pl.BlockSpec((N_EXPERTS, LANE), lambda i: (0, 0)),
        ],
        out_shape=[
            jax.ShapeDtypeStruct((n, d // 2), jnp.uint32),
            jax.ShapeDtypeStruct((2, n), jnp.int32),
            jax.ShapeDtypeStruct((2, n), jnp.int32),
            jax.ShapeDtypeStruct((n, 8), F32),
            jax.ShapeDtypeStruct((N_EXPERTS, LANE), F32),
        ],
        scratch_shapes=[pltpu.VMEM((N_EXPERTS, LANE), F32), pltpu.VMEM((tm, tm), BF16)],
        compiler_params=_cparams("arbitrary"),
        name="moe_router",
    )(x, mods, g2, rw_hi, rw_lo, rb)


def _expert_kernel(be_ref, nb_ref, xs_ref, wg_ref, wu_ref, wd_ref, y_ref, wg_b, wu_b, wd_b):
    i = pl.program_id(0)
    prev = be_ref[jnp.maximum(i - 1, 0)]

    @pl.when(jnp.logical_or(i == 0, be_ref[i] != prev))
    def _():
        wg_b[...] = wg_ref[...].astype(BF16)
        wu_b[...] = wu_ref[...].astype(BF16)
        wd_b[...] = wd_ref[...].astype(BF16)

    @pl.when(i < nb_ref[0])
    def _():
        lo, hi = _unpack_halves(xs_ref[...])
        lo, hi = lo.astype(BF16), hi.astype(BF16)
        half = lo.shape[1]
        gate = _dot(lo, wg_b[:half, :]) + _dot(hi, wg_b[half:, :])
        up = _dot(lo, wu_b[:half, :]) + _dot(hi, wu_b[half:, :])
        hid = (gate * _sigmoid(gate)) * up
        y_ref[...] = _pack_halves(_dot(hid.astype(BF16), wd_b[...]))

    @pl.when(i >= nb_ref[0])
    def _():
        y_ref[...] = jnp.zeros_like(y_ref)


def _experts(xs, block_e, n_used, w_gate, w_up, w_down, layer):
    rows, half = xs.shape
    d = 2 * half
    de = w_gate.shape[-1]
    n_blocks = rows // MOE_ROWS
    grid_spec = pltpu.PrefetchScalarGridSpec(
        num_scalar_prefetch=2,
        grid=(n_blocks,),
        in_specs=[
            pl.BlockSpec((MOE_ROWS, half), lambda i, be, nb: (i, 0)),
            pl.BlockSpec((None, None, d, de), lambda i, be, nb: (layer, be[i], 0, 0)),
            pl.BlockSpec((None, None, d, de), lambda i, be, nb: (layer, be[i], 0, 0)),
            pl.BlockSpec((None, None, de, d), lambda i, be, nb: (layer, be[i], 0, 0)),
        ],
        out_specs=pl.BlockSpec((MOE_ROWS, half), lambda i, be, nb: (i, 0)),
        scratch_shapes=[pltpu.VMEM((d, de), BF16), pltpu.VMEM((d, de), BF16), pltpu.VMEM((de, d), BF16)],
    )
    return pl.pallas_call(
        _expert_kernel,
        grid_spec=grid_spec,
        out_shape=jax.ShapeDtypeStruct((rows, half), jnp.uint32),
        compiler_params=_cparams("arbitrary"),
        name="moe_experts",
    )(block_e, n_used, xs, w_gate, w_up, w_down)


def _combine_kernel(x_ref, m_ref, y_ref, w_ref, *rest, final):
    if final:
        fg_ref, o_ref = rest
    else:
        (o_ref,) = rest
    w = w_ref[...]
    lo0, hi0 = _unpack_halves(y_ref[0])
    lo1, hi1 = _unpack_halves(y_ref[1])
    w0, w1 = w[:, 0:1], w[:, 1:2]
    y = jnp.concatenate([w0 * lo0 + w1 * lo1, w0 * hi0 + w1 * hi1], axis=-1)
    x = x_ref[...] + m_ref[5] * y
    if final:
        x = _rms(x) * fg_ref[...]
    o_ref[...] = x


def _combine(x, mods, layer, segs, seg, yg, wcol, final_g):
    n, d = x.shape
    final = final_g is not None
    tm = WIDE_TILE
    mod_row = _wide_mod_row(segs, tm)
    t0 = seg.row0 // tm if final else 0
    steps = (seg.batch * seg.seq if final else n) // tm
    x_spec = pl.BlockSpec((tm, d), lambda i: (t0 + i, 0))
    in_specs = [x_spec,
                pl.BlockSpec((None, None, 6, 1, d), lambda i: (layer, mod_row(t0 + i), 0, 0, 0)),
                pl.BlockSpec((2, tm, d // 2), lambda i: (0, t0 + i, 0)),
                pl.BlockSpec((tm, 8), lambda i: (t0 + i, 0))]
    args = [x, mods, yg, wcol]
    if final:
        in_specs.append(pl.BlockSpec((1, d), lambda i: (0, 0)))
        args.append(final_g)
        out_spec = pl.BlockSpec((tm, d), lambda i: (i, 0))
        out_shape = jax.ShapeDtypeStruct((steps * tm, d), F32)
        aliases = {}
    else:
        out_spec, out_shape, aliases = x_spec, jax.ShapeDtypeStruct((n, d), F32), {0: 0}
    return pl.pallas_call(
        functools.partial(_combine_kernel, final=final),
        grid=(steps,),
        in_specs=in_specs,
        out_specs=out_spec,
        out_shape=out_shape,
        input_output_aliases=aliases,
        compiler_params=_cparams("parallel"),
        name="moe_combine_final" if final else "moe_combine",
    )(*args)


def _sc_mesh():
    return plsc.VectorSubcoreMesh(core_axis_name="c", subcore_axis_name="s")


def _sc_worker_split(n):
    workers = SC_CORES * SC_SUBCORES
    per = n // workers
    assert per * workers == n and per % SC_CHUNK == 0
    return workers, per, per // SC_CHUNK


def _sc_dispatch(h, pos, rows):
    n, w = h.shape
    workers, per, chunks = _sc_worker_split(n)

    @functools.partial(
        pl.kernel, out_type=jax.ShapeDtypeStruct((rows, w), h.dtype), mesh=_sc_mesh(),
        scratch_types=[pltpu.VMEM((2, chunks, SC_CHUNK), jnp.int32), pltpu.VMEM((SC_CHUNK, w), h.dtype)],
        name="moe_dispatch_scatter")
    def scatter_rows(h_hbm, pos_hbm, xs_hbm, idx_v, rows_v):
        wid = lax.axis_index("s") * SC_CORES + lax.axis_index("c")
        pltpu.sync_copy(pos_hbm.at[0, wid], idx_v.at[0])
        pltpu.sync_copy(pos_hbm.at[1, wid], idx_v.at[1])

        @pl.loop(0, chunks)
        def _(c):
            pltpu.sync_copy(h_hbm.at[pl.ds(wid * per + c * SC_CHUNK, SC_CHUNK)], rows_v)
            pltpu.sync_copy(rows_v, xs_hbm.at[idx_v.at[0, c]])
            pltpu.sync_copy(rows_v, xs_hbm.at[idx_v.at[1, c]])

    return scatter_rows(h, pos.reshape(2, workers, chunks, SC_CHUNK))


def _sc_gather2(ys, pos):
    _, w = ys.shape
    n = pos.shape[1]
    workers, per, chunks = _sc_worker_split(n)

    @functools.partial(
        pl.kernel, out_type=jax.ShapeDtypeStruct((2, n, w), ys.dtype), mesh=_sc_mesh(),
        scratch_types=[pltpu.VMEM((2, chunks, SC_CHUNK), jnp.int32), pltpu.VMEM((SC_CHUNK, w), ys.dtype),
                       pltpu.SemaphoreType.DMA],
        name="moe_combine_gather")
    def gather_rows(ys_hbm, pos_hbm, out_hbm, idx_v, rows_v, sem):
        wid = lax.axis_index("s") * SC_CORES + lax.axis_index("c")
        pltpu.sync_copy(pos_hbm.at[0, wid], idx_v.at[0])
        pltpu.sync_copy(pos_hbm.at[1, wid], idx_v.at[1])

        @pl.loop(0, chunks)
        def _(c):
            for k in range(2):
                pltpu.async_copy(ys_hbm.at[idx_v.at[k, c]], rows_v, sem).wait()
                pltpu.sync_copy(rows_v, out_hbm.at[k, pl.ds(wid * per + c * SC_CHUNK, SC_CHUNK)])

    return gather_rows(ys, pos.reshape(2, workers, chunks, SC_CHUNK))


def _dispatch_plan(idx, rank, counts):
    n = idx.shape[1]
    padded = (counts + MOE_ROWS - 1) // MOE_ROWS * MOE_ROWS
    pad_end = jnp.cumsum(padded)
    pad_start = pad_end - padded
    experts = jnp.arange(N_EXPERTS, dtype=jnp.int32)
    start_of = jnp.sum(jnp.where(idx[..., None] == experts, pad_start, 0), axis=-1)
    pos = start_of + rank
    n_blocks = 2 * n // MOE_ROWS + N_EXPERTS
    starts = jnp.arange(n_blocks, dtype=jnp.int32) * MOE_ROWS
    block_e = jnp.minimum(jnp.sum(starts[:, None] >= pad_end[None, :], axis=1), N_EXPERTS - 1).astype(jnp.int32)
    n_used = (pad_end[-1] // MOE_ROWS).astype(jnp.int32).reshape(1)
    return pos, block_e, n_used, n_blocks * MOE_ROWS


def _moe(x, mods, layer, segs, g2, rw_hi, rw_lo, rb, w_gate, w_up, w_down, final_g):
    h2p, idx, rank, wcol, cnt = _router(x, mods, layer, segs, g2, rw_hi, rw_lo, rb)
    pos, block_e, n_used, rows = _dispatch_plan(idx, rank, cnt[:, 0].astype(jnp.int32))
    xs = _sc_dispatch(h2p, pos, rows)
    ys = _experts(xs, block_e, n_used, w_gate, w_up, w_down, layer)
    yg = _sc_gather2(ys, pos)
    if final_g is None:
        return _combine(x, mods, layer, segs, None, yg, wcol, None)
    return tuple(_combine(x, mods, layer, segs, seg, yg, wcol, final_g) for seg in segs)


def _mla_rope_tables(t):
    axis_dim = MLA_ROPE // 2
    row = np.repeat(np.arange(t // GRID_W), GRID_W).astype(np.float64)
    col = np.tile(np.arange(GRID_W), t // GRID_W).astype(np.float64)
    inv = ROPE_BASE ** (-np.arange(0, axis_dim, 2, dtype=np.float64) / axis_dim)
    ar, ac = row[:, None] * inv[None, :], col[:, None] * inv[None, :]
    ones = np.ones((t, LANE - MLA_ROPE))
    cos = np.concatenate([np.cos(ar), np.cos(ar), np.cos(ac), np.cos(ac), ones], axis=-1)
    sin = np.concatenate([-np.sin(ar), np.sin(ar), -np.sin(ac), np.sin(ac), 0.0 * ones], axis=-1)
    return jnp.asarray(cos, F32), jnp.asarray(sin, F32)


def _ret_rot_tables(t, dk):
    inv = ROPE_BASE ** (-np.linspace(0.0, 1.0, dk // 2))
    ang = np.arange(t, dtype=np.float64)[:, None] * inv[None, :]
    return jnp.asarray(np.cos(ang), F32), jnp.asarray(np.sin(ang), F32)


def _dft_tables(n):
    k = np.arange(n, dtype=np.int64)
    ang = (np.outer(k, k) % n).astype(np.float64) * (2.0 * math.pi / n)
    return jnp.asarray(np.cos(ang), BF16), jnp.asarray(np.sin(ang), BF16)


def _mla_weights(w_in, q_g, kv_g, w_uq, w_ukv, w_o):
    d = w_in.shape[0]
    hd = MLA_NOPE + MLA_ROPE
    w_in_p = jnp.concatenate([w_in, jnp.zeros((d, LANE - MLA_ROPE), w_in.dtype)], axis=1)
    uq = w_uq.reshape(MLA_Q_LORA, MLA_HEADS, hd)
    uq = jnp.concatenate([uq, jnp.zeros((MLA_Q_LORA, MLA_HEADS, 2 * LANE - hd), uq.dtype)], axis=-1)
    ukv = w_ukv.reshape(MLA_KV_LORA, MLA_HEADS, MLA_NOPE + MLA_V)
    return {
        "w_in": w_in_p.astype(BF16),
        "q_g": q_g.reshape(1, -1),
        "kv_g": kv_g.reshape(1, -1),
        "w_q": uq.reshape(MLA_Q_LORA, MLA_HEADS * 2 * LANE).astype(BF16),
        "w_kn": ukv[..., :MLA_NOPE].reshape(MLA_KV_LORA, MLA_HEADS * MLA_NOPE).astype(BF16),
        "w_v": ukv[..., MLA_NOPE:].reshape(MLA_KV_LORA, MLA_HEADS * MLA_V).astype(BF16),
        "w_o": w_o.astype(BF16),
    }


def kernel(x_prompt, x_sample, cache_mla, state_ret, c, c_ctx, norm1_g, norm2_g, ada_w, ada_b, final_norm_g,
           mla_w_in, mla_q_norm_g, mla_kv_norm_g, mla_w_uq, mla_w_ukv, mla_w_o, ret_w_in, ret_decay_f,
           ret_decay_b, ret_w_o, fnet_w, router_w, router_b, moe_w_gate, moe_w_up, moe_w_down):
    b_ctx, t_ctx, d = x_prompt.shape
    b_lat, t_lat, _ = x_sample.shape
    depth = ada_w.shape[0]
    assert t_ctx == ROW_TILE and t_lat % ROW_TILE == 0 and b_lat + 1 <= 8
    n_ctx = b_ctx * t_ctx
    ctx = _Seg(0, b_ctx, t_ctx, 0, False)
    lat = _Seg(n_ctx, b_lat, t_lat, 1, True)
    segs = (ctx, lat)

    x = jnp.concatenate([x_prompt.reshape(n_ctx, d), x_sample.reshape(b_lat * t_lat, d)], axis=0)
    cond8 = jnp.concatenate([c_ctx[None, :], c, jnp.zeros((8 - 1 - b_lat, d), F32)], axis=0)
    mods = _modulation_all(cond8, ada_w, ada_b).reshape(depth, 8, 6, 1, d)

    rw_t = router_w.T.astype(F32)
    rw_hi = rw_t.astype(BF16)
    rw_lo = (rw_t - rw_hi.astype(F32)).astype(BF16)
    rb = router_b.reshape(N_EXPERTS, 1).astype(F32)
    final_g = final_norm_g.reshape(1, d)
    dk = ret_w_in.shape[2] // (8 * RET_HEADS)
    dv = 2 * dk

    caches, states = [], []
    counters = [0, 0, 0]
    for layer in range(depth):
        kind = layer % 3
        j = counters[kind]
        counters[kind] += 1
        g1 = norm1_g[layer].reshape(1, d)
        g2 = norm2_g[layer].reshape(1, d)
        if kind == 0:
            w = _mla_weights(mla_w_in[j], mla_q_norm_g[j], mla_kv_norm_g[j], mla_w_uq[j], mla_w_ukv[j],
                             mla_w_o[j])
            qc, kc, vc, cache = _mla_proj(x, mods, layer, ctx, g1, w, None)
            ql, kl, vl = _mla_proj(x, mods, layer, lat, g1, w, _mla_rope_tables(t_lat))
            past = cache_mla.shape[2]
            cpad = jnp.pad(cache_mla[:, j].reshape(b_lat * past, -1), ((0, 0), (0, LANE - MLA_ROPE)))
            kp, vp = _cache_kv(cpad, w)
            x = _attention(x, mods, layer, ctx, qc, [(kc, vc, t_ctx)], w["w_o"])
            x = _attention(x, mods, layer, lat, ql, [(kl, vl, t_lat), (kp, vp, past)], w["w_o"])
            caches.append(cache.reshape(b_ctx, t_ctx, -1))
        elif kind == 1:
            w_in = ret_w_in[j]
            qk = RET_HEADS * dk
            k_scale = jnp.concatenate([jnp.ones((qk,), F32), jnp.full((qk,), dk ** -0.5, F32),
                                       jnp.ones((w_in.shape[1] - 2 * qk,), F32)])
            w_in_b = (w_in * k_scale[None, :]).astype(BF16)
            w_o_b = ret_w_o[j].astype(BF16)
            w_qk, w_v, w_g = w_in_b[:, :2 * qk], w_in_b[:, 2 * qk:2 * qk + RET_HEADS * dv], \
                w_in_b[:, 2 * qk + RET_HEADS * dv:]
            rot = _ret_rot_tables(t_lat, dk)
            parts = []
            for seg in segs:
                rotary = seg is lat
                parts.append((
                    _ret_proj(x, mods, layer, seg, g1, w_qk, "rotary" if rotary else "plain", rot, dk),
                    _ret_proj(x, mods, layer, seg, g1, w_v, "plain", None, dk),
                    _ret_proj(x, mods, layer, seg, g1, w_g, "silu", None, dk)))
            yc, s_ctx = _ret_scan(*parts[0], ctx, ret_decay_f[j], ret_decay_b[j], None, True, dk, dv)
            yl, _ = _ret_scan(*parts[1], lat, ret_decay_f[j], ret_decay_b[j], state_ret[:, j], False, dk, dv)
            x = _matmul_residual(x, mods, layer, ctx, yc, w_o_b)
            x = _matmul_residual(x, mods, layer, lat, yl, w_o_b)
            states.append(s_ctx)
        else:
            gd = d // FNET_GROUPS
            cc, sc = _dft_tables(gd)
            cs = jnp.concatenate([cc, sc], axis=1)
            w_b = fnet_w[j].astype(BF16)
            for seg in segs:
                ct, st = _dft_tables(seg.seq)
                ac, as_ = _fnet_a(x, mods, layer, seg, g1, cs)
                x = _fnet_b(x, mods, layer, seg, ac, as_, ct, st, w_b, (seg.seq * gd) ** -0.5)
        last = layer == depth - 1
        x = _moe(x, mods, layer, segs, g2, rw_hi, rw_lo, rb, moe_w_gate, moe_w_up, moe_w_down,
                 final_g if last else None)

    y_prompt, y_sample = x
    new_cache = jnp.stack(caches, axis=1)
    new_state = jnp.stack(states, axis=1)
    return (y_prompt.reshape(b_ctx, t_ctx, d), y_sample.reshape(b_lat, t_lat, d), new_cache, new_state)
```

```python
import functools
import math

import jax
import jax.numpy as jnp
import numpy as np
from jax import lax
from jax.experimental import pallas as pl
from jax.experimental.pallas import tpu as pltpu
from jax.experimental.pallas import tpu_sc as plsc

F32 = jnp.float32
BF16 = jnp.bfloat16

GRID_W = 64
MLA_HEADS = 8
MLA_NOPE = 128
MLA_ROPE = 64
MLA_V = 128
MLA_Q_LORA = 384
MLA_KV_LORA = 256
ROPE_BASE = 10000.0
RET_HEADS = 4
RET_CHUNK = 256
FNET_GROUPS = 4
N_EXPERTS = 16
N_EXPERT_GROUPS = 4
EXPERTS_PER_GROUP = 4
D_EXPERT = 512
NORM_EPS = 1e-6

LANE = 128
ROW_TILE = 256
WIDE_TILE = 512
MOE_ROWS = 512
VMEM_LIMIT = 56 * 1024 * 1024
SC_CORES = 2
SC_SUBCORES = 16
SC_CHUNK = 128


def _cparams(*sem):
    return pltpu.CompilerParams(dimension_semantics=sem, vmem_limit_bytes=VMEM_LIMIT)


def _sigmoid(x):
    return 1.0 / (1.0 + jnp.exp(-x))


def _rms(x):
    return x * lax.rsqrt(jnp.mean(x * x, axis=-1, keepdims=True) + NORM_EPS)


def _modulate(x, g, shift, scale):
    return (_rms(x) * g) * (1.0 + scale) + shift


def _dot(a, b):
    return jnp.dot(a, b, preferred_element_type=F32)


def _dot_nt(a, b):
    return lax.dot_general(a, b, (((1,), (1,)), ((), ())), preferred_element_type=F32)


def _mod_kernel(c_ref, w_ref, b_ref, o_ref):
    c = c_ref[...]
    s = (c * _sigmoid(c)).astype(BF16)
    o_ref[...] = _dot(s, w_ref[...].astype(BF16)) + b_ref[...]


def _modulation_all(cond8, ada_w, ada_b):
    depth, d, d6 = ada_w.shape
    tn = d6 // 4
    return pl.pallas_call(
        _mod_kernel,
        grid=(depth, d6 // tn),
        in_specs=[
            pl.BlockSpec((8, d), lambda l, n: (0, 0)),
            pl.BlockSpec((None, d, tn), lambda l, n: (l, 0, n)),
            pl.BlockSpec((None, 1, tn), lambda l, n: (l, 0, n)),
        ],
        out_specs=pl.BlockSpec((None, 8, tn), lambda l, n: (l, 0, n)),
        out_shape=jax.ShapeDtypeStruct((depth, 8, d6), F32),
        compiler_params=_cparams("parallel", "parallel"),
        name="modulation",
    )(cond8, ada_w, ada_b.reshape(depth, 1, d6))


class _Seg:
    def __init__(self, row0, batch, seq, mod0, per_batch_mod):
        self.row0, self.batch, self.seq = row0, batch, seq
        self.mod0, self.per_batch_mod = mod0, per_batch_mod
        self.tiles = batch * seq // ROW_TILE
        self.tile0 = row0 // ROW_TILE
        self.tiles_per_seq = seq // ROW_TILE

    def mod_row(self, tile):
        if self.per_batch_mod:
            return self.mod0 + tile // self.tiles_per_seq
        return self.mod0


def _mod_spec(layer, seg, d, tile_of=lambda *a: a[0]):
    return pl.BlockSpec((None, None, 6, 1, d), lambda *a: (layer, seg.mod_row(tile_of(*a)), 0, 0, 0))


def _rope_partner(x):
    lane = lax.broadcasted_iota(jnp.int32, x.shape, 1)
    first = (lane % 32) < 16
    return jnp.where(first, pltpu.roll(x, LANE - 16, 1), pltpu.roll(x, 16, 1))


def _mla_proj_kernel(*refs, rope, scale):
    if rope:
        (x_ref, m_ref, g_ref, win_ref, qg_ref, kvg_ref, wq_ref, wkn_ref, wv_ref, cos_ref, sin_ref,
         q_ref, k_ref, v_ref) = refs
    else:
        x_ref, m_ref, g_ref, win_ref, qg_ref, kvg_ref, wq_ref, wkn_ref, wv_ref = refs[:9]
        q_ref, k_ref, v_ref, cache_ref = refs[-4:]
    h = _modulate(x_ref[...], g_ref[...], m_ref[0], m_ref[1]).astype(BF16)
    z = _dot(h, win_ref[...])
    cq = z[:, :MLA_Q_LORA]
    ckv = z[:, MLA_Q_LORA:MLA_Q_LORA + MLA_KV_LORA]
    kpe = z[:, MLA_Q_LORA + MLA_KV_LORA:]
    cqn = (_rms(cq) * qg_ref[...]).astype(BF16)
    ckvn = _rms(ckv) * kvg_ref[...]
    ckvb = ckvn.astype(BF16)
    q = _dot(cqn, wq_ref[...]) * scale
    kn = _dot(ckvb, wkn_ref[...])
    v_ref[...] = _dot(ckvb, wv_ref[...]).astype(BF16)
    if rope:
        cos, sin = cos_ref[...], sin_ref[...]
        kpe = kpe * cos + _rope_partner(kpe) * sin
    else:
        cache_ref[:, :MLA_KV_LORA] = ckvn
        cache_ref[:, MLA_KV_LORA:] = kpe[:, :MLA_ROPE]
    kpe_b = kpe.astype(BF16)
    for hd in range(MLA_HEADS):
        lo = hd * 2 * LANE
        q_ref[:, lo:lo + LANE] = q[:, lo:lo + LANE].astype(BF16)
        qr = q[:, lo + LANE:lo + 2 * LANE]
        if rope:
            qr = qr * cos + _rope_partner(qr) * sin
        q_ref[:, lo + LANE:lo + 2 * LANE] = qr.astype(BF16)
        k_ref[:, lo:lo + LANE] = kn[:, hd * LANE:(hd + 1) * LANE].astype(BF16)
        k_ref[:, lo + LANE:lo + 2 * LANE] = kpe_b


def _mla_proj(x, x_tile0, mods, layer, seg, g1, w, rope_tabs, cache_slot=None):
    n, d = x.shape
    rope = rope_tabs is not None
    rows = seg.batch * seg.seq
    hq = MLA_HEADS * 2 * LANE
    hv = MLA_HEADS * MLA_V
    const = lambda i: (0, 0)
    aliases = {}
    in_specs = [
        pl.BlockSpec((ROW_TILE, d), lambda i: (x_tile0 + i, 0)),
        _mod_spec(layer, seg, d),
        pl.BlockSpec((1, d), const),
        pl.BlockSpec(w["w_in"].shape, const),
        pl.BlockSpec((1, MLA_Q_LORA), const),
        pl.BlockSpec((1, MLA_KV_LORA), const),
        pl.BlockSpec(w["w_q"].shape, const),
        pl.BlockSpec(w["w_kn"].shape, const),
        pl.BlockSpec(w["w_v"].shape, const),
    ]
    args = [x, mods, g1, w["w_in"], w["q_g"], w["kv_g"], w["w_q"], w["w_kn"], w["w_v"]]
    out_specs = [
        pl.BlockSpec((ROW_TILE, hq), lambda i: (i, 0)),
        pl.BlockSpec((ROW_TILE, hq), lambda i: (i, 0)),
        pl.BlockSpec((ROW_TILE, hv), lambda i: (i, 0)),
    ]
    out_shape = [
        jax.ShapeDtypeStruct((rows, hq), BF16),
        jax.ShapeDtypeStruct((rows, hq), BF16),
        jax.ShapeDtypeStruct((rows, hv), BF16),
    ]
    if rope:
        tab = pl.BlockSpec((ROW_TILE, LANE), lambda i: (i % seg.tiles_per_seq, 0))
        in_specs += [tab, tab]
        args += list(rope_tabs)
    else:
        cw = MLA_KV_LORA + MLA_ROPE
        prev, slot, n_slots = cache_slot
        assert seg.seq == ROW_TILE
        out_specs.append(pl.BlockSpec((None, None, ROW_TILE, cw), lambda i: (i, slot, 0, 0)))
        out_shape.append(jax.ShapeDtypeStruct((seg.batch, n_slots, seg.seq, cw), F32))
        if prev is not None:
            in_specs.append(pl.BlockSpec(memory_space=pl.ANY))
            args.append(prev)
            aliases = {len(args) - 1: 3}
    return pl.pallas_call(
        functools.partial(_mla_proj_kernel, rope=rope, scale=(MLA_NOPE + MLA_ROPE) ** -0.5),
        grid=(seg.tiles,),
        in_specs=in_specs,
        out_specs=out_specs,
        out_shape=out_shape,
        input_output_aliases=aliases,
        compiler_params=_cparams("parallel"),
        name="mla_proj_lat" if rope else "mla_proj_ctx",
    )(*args)


def _cache_kv_kernel(c_ref, wkn_ref, wv_ref, k_ref, v_ref):
    c = c_ref[...]
    ckv = c[:, :MLA_KV_LORA].astype(BF16)
    kpe_b = c[:, MLA_KV_LORA:].astype(BF16)
    kn = _dot(ckv, wkn_ref[...])
    v_ref[...] = _dot(ckv, wv_ref[...]).astype(BF16)
    for hd in range(MLA_HEADS):
        lo = hd * 2 * LANE
        k_ref[:, lo:lo + LANE] = kn[:, hd * LANE:(hd + 1) * LANE].astype(BF16)
        k_ref[:, lo + LANE:lo + 2 * LANE] = kpe_b


def _cache_kv(cache_pad, w):
    rows, cw = cache_pad.shape
    hq = MLA_HEADS * 2 * LANE
    hv = MLA_HEADS * MLA_V
    const = lambda i: (0, 0)
    return pl.pallas_call(
        _cache_kv_kernel,
        grid=(rows // ROW_TILE,),
        in_specs=[
            pl.BlockSpec((ROW_TILE, cw), lambda i: (i, 0)),
            pl.BlockSpec(w["w_kn"].shape, const),
            pl.BlockSpec(w["w_v"].shape, const),
        ],
        out_specs=[pl.BlockSpec((ROW_TILE, hq), lambda i: (i, 0)),
                   pl.BlockSpec((ROW_TILE, hv), lambda i: (i, 0))],
        out_shape=[jax.ShapeDtypeStruct((rows, hq), BF16), jax.ShapeDtypeStruct((rows, hv), BF16)],
        compiler_params=_cparams("parallel"),
        name="mla_cache_kv",
    )(cache_pad, w["w_kn"], w["w_v"])


def _attn_kernel(*refs, n_parts):
    q_ref = refs[0]
    kv_refs = refs[1:1 + 2 * n_parts]
    wo_ref, x_ref, m_ref = refs[1 + 2 * n_parts:4 + 2 * n_parts]
    o_ref, acc_ref = refs[-2:]
    for hd in range(MLA_HEADS):
        qh = q_ref[:, hd * 2 * LANE:(hd + 1) * 2 * LANE]
        scores = [_dot_nt(qh, kv_refs[2 * p][:, hd * 2 * LANE:(hd + 1) * 2 * LANE]) for p in range(n_parts)]
        mx = scores[0].max(axis=-1, keepdims=True)
        for s in scores[1:]:
            mx = jnp.maximum(mx, s.max(axis=-1, keepdims=True))
        den = None
        out = None
        for p, s in enumerate(scores):
            e = jnp.exp(s - mx)
            den = e.sum(axis=-1, keepdims=True) if den is None else den + e.sum(axis=-1, keepdims=True)
            pv = _dot(e.astype(BF16), kv_refs[2 * p + 1][:, hd * MLA_V:(hd + 1) * MLA_V])
            out = pv if out is None else out + pv
        acc_ref[:, hd * MLA_V:(hd + 1) * MLA_V] = (out / den).astype(BF16)
    y = _dot(acc_ref[...], wo_ref[...])
    o_ref[...] = x_ref[...] + m_ref[2] * y


def _attention(x, x_tile0, n, dest, mods, layer, seg, q, kv_parts, w_o):
    d = x.shape[1]
    hq = MLA_HEADS * 2 * LANE
    hv = MLA_HEADS * MLA_V
    tps = seg.tiles_per_seq
    in_specs = [pl.BlockSpec((ROW_TILE, hq), lambda b, i: (b * tps + i, 0))]
    args = [q]
    for k, v, rows in kv_parts:
        in_specs += [pl.BlockSpec((rows, hq), lambda b, i: (b, 0)), pl.BlockSpec((rows, hv), lambda b, i: (b, 0))]
        args += [k, v]
    in_specs += [
        pl.BlockSpec(w_o.shape, lambda b, i: (0, 0)),
        pl.BlockSpec((ROW_TILE, d), lambda b, i: (x_tile0 + b * tps + i, 0)),
        _mod_spec(layer, seg, d, tile_of=lambda b, i: b * tps + i),
    ]
    args += [w_o, x, mods]
    if isinstance(dest, str):
        aliases = {len(args) - 2: 0} if dest == "inplace" else {}
    else:
        in_specs.append(pl.BlockSpec(memory_space=pl.ANY))
        args.append(dest)
        aliases = {len(args) - 1: 0}
    return pl.pallas_call(
        functools.partial(_attn_kernel, n_parts=len(kv_parts)),
        grid=(seg.batch, tps),
        in_specs=in_specs,
        out_specs=pl.BlockSpec((ROW_TILE, d), lambda b, i: (seg.tile0 + b * tps + i, 0)),
        out_shape=jax.ShapeDtypeStruct((n, d), F32),
        scratch_shapes=[pltpu.VMEM((ROW_TILE, hv), BF16)],
        input_output_aliases=aliases,
        compiler_params=_cparams("parallel", "arbitrary"),
        name="mla_attention",
    )(*args)


RET_COL = 1024


RET_ROWS = 512


def _ret_proj_kernel(*refs, kinds, dk):
    rotary = "rotary" in kinds
    if rotary:
        x_ref, m_ref, g_ref, w_ref, cos_ref, sin_ref, z_ref = refs
        cos, sin = cos_ref[...], sin_ref[...]
    else:
        x_ref, m_ref, g_ref, w_ref, z_ref = refs
    h = _modulate(x_ref[...], g_ref[...], m_ref[0], m_ref[1]).astype(BF16)
    half = dk // 2
    for j, kind in enumerate(kinds):
        c0 = j * RET_COL
        acc = _dot(h, w_ref[:, c0:c0 + RET_COL])
        if kind == "rotary":
            for hd in range(RET_COL // dk):
                lo = hd * dk
                x1, x2 = acc[:, lo:lo + half], acc[:, lo + half:lo + dk]
                z_ref[:, c0 + lo:c0 + lo + half] = (x1 * cos - x2 * sin).astype(BF16)
                z_ref[:, c0 + lo + half:c0 + lo + dk] = (x1 * sin + x2 * cos).astype(BF16)
        elif kind == "silu":
            z_ref[:, c0:c0 + RET_COL] = (acc * _sigmoid(acc)).astype(BF16)
        else:
            z_ref[:, c0:c0 + RET_COL] = acc.astype(BF16)


def _ret_proj(x, mods, layer, seg, g1, w_in, group, kinds, rot_tabs, dk):
    n, d = x.shape
    rows = seg.batch * seg.seq
    tm = RET_ROWS
    ncol = len(kinds) * RET_COL
    tps = max(seg.seq // tm, 1)
    in_specs = [
        pl.BlockSpec((tm, d), lambda i: (seg.row0 // tm + i, 0)),
        pl.BlockSpec((None, None, 6, 1, d),
                     lambda i: (layer, seg.mod0 + (i * tm // seg.seq if seg.per_batch_mod else 0), 0, 0, 0)),
        pl.BlockSpec((1, d), lambda i: (0, 0)),
        pl.BlockSpec((d, ncol), lambda i: (0, group)),
    ]
    args = [x, mods, g1, w_in]
    if "rotary" in kinds:
        tab = pl.BlockSpec((tm, dk // 2), lambda i: (i % tps, 0))
        in_specs += [tab, tab]
        args += list(rot_tabs)
    return pl.pallas_call(
        functools.partial(_ret_proj_kernel, kinds=kinds, dk=dk),
        grid=(rows // tm,),
        in_specs=in_specs,
        out_specs=pl.BlockSpec((tm, ncol), lambda i: (i, 0)),
        out_shape=jax.ShapeDtypeStruct((rows, ncol), BF16),
        compiler_params=_cparams("parallel"),
        name="ret_proj_" + kinds[0],
    )(*args)


def _log_sigmoid(x):
    return jnp.minimum(x, 0.0) - jnp.log(1.0 + jnp.exp(-jnp.abs(x)))


def _ret_scan_kernel(*refs, has_s0, emit_state, n_chunks):
    refs = list(refs)
    lf_ref, lb_ref, q_ref, k_ref, v_ref, gf_ref, gb_ref = refs[:7]
    pos = 7
    s0_ref = None
    if has_s0:
        s0_ref = refs[pos]
        pos += 1
    y_ref = refs[pos]
    pos += 1
    sout_ref = None
    if emit_state:
        sout_ref = refs[pos]
        pos += 1
    s_ref, yf_ref = refs[pos:]
    c = RET_CHUNK
    ii = lax.broadcasted_iota(jnp.int32, (c, c), 0).astype(F32)
    jj = lax.broadcasted_iota(jnp.int32, (c, c), 1).astype(F32)
    idx = lax.broadcasted_iota(jnp.int32, (c, 1), 0).astype(F32)

    for direction in range(2):
        fwd = direction == 0
        lg = _log_sigmoid((lf_ref if fwd else lb_ref)[...])
        rel = (ii - jj) if fwd else (jj - ii)
        keep = rel >= 0
        decay_in = jnp.where(keep, jnp.exp(jnp.where(keep, rel, 0.0) * lg), 0.0)
        decay_q = jnp.exp(((idx + 1.0) if fwd else (c - idx)) * lg)
        decay_k = jnp.exp(((c - 1.0 - idx) if fwd else idx) * lg)
        decay_c = jnp.exp(c * lg)
        g_ref = gf_ref if fwd else gb_ref

        def chunk(cc, state, fwd=fwd, decay_in=decay_in, decay_q=decay_q, decay_k=decay_k, decay_c=decay_c,
                  g_ref=g_ref):
            r0 = cc * c if isinstance(cc, int) else pl.multiple_of(cc * c, c)
            qc = q_ref[pl.ds(r0, c), :]
            kc = k_ref[pl.ds(r0, c), :]
            vc = v_ref[pl.ds(r0, c), :]
            sc = _dot_nt(qc, kc) * decay_in
            out = _dot(sc.astype(BF16), vc)
            kd_t = (kc.astype(F32) * decay_k).T.astype(BF16)
            new_s = _dot(kd_t, vc)
            if state is not None:
                out = out + _dot((qc.astype(F32) * decay_q).astype(BF16), state.astype(BF16))
                new_s = decay_c * state + new_s
            s_ref[...] = new_s
            o = _rms(out) * g_ref[pl.ds(r0, c), :].astype(F32)
            if fwd:
                yf_ref[pl.ds(r0, c), :] = o
            else:
                y_ref[pl.ds(r0, c), :] = (yf_ref[pl.ds(r0, c), :] + o).astype(BF16)

        chunk(0 if fwd else n_chunks - 1, s0_ref[direction] if has_s0 else None)

        def step(ci, carry, fwd=fwd, chunk=chunk):
            chunk(ci if fwd else n_chunks - 1 - ci, s_ref[...])
            return carry

        lax.fori_loop(1, n_chunks, step, 0)
        if emit_state:
            sout_ref[direction] = s_ref[...]


def _ret_scan(qkv, g, seg, logit_f, logit_b, s0, emit_state, dk, dv):
    rows = seg.batch * seg.seq
    t = seg.seq
    hh = RET_HEADS
    v0 = 2 * hh * dk // dv
    in_specs = [
        pl.BlockSpec((None, 1, 1), lambda b, h: (h, 0, 0)),
        pl.BlockSpec((None, 1, 1), lambda b, h: (h, 0, 0)),
        pl.BlockSpec((t, dk), lambda b, h: (b, h)),
        pl.BlockSpec((t, dk), lambda b, h: (b, hh + h)),
        pl.BlockSpec((t, dv), lambda b, h: (b, v0 + h)),
        pl.BlockSpec((t, dv), lambda b, h: (b, h)),
        pl.BlockSpec((t, dv), lambda b, h: (b, hh + h)),
    ]
    args = [logit_f.reshape(hh, 1, 1), logit_b.reshape(hh, 1, 1), qkv, qkv, qkv, g, g]
    state_spec = pl.BlockSpec((None, 2, None, dk, dv), lambda b, h: (b, 0, h, 0, 0))
    if s0 is not None:
        in_specs.append(state_spec)
        args.append(s0)
    out_specs = [pl.BlockSpec((t, dv), lambda b, h: (b, h))]
    out_shape = [jax.ShapeDtypeStruct((rows, hh * dv), BF16)]
    if emit_state:
        out_specs.append(state_spec)
        out_shape.append(jax.ShapeDtypeStruct((seg.batch, 2, hh, dk, dv), F32))
    res = pl.pallas_call(
        functools.partial(_ret_scan_kernel, has_s0=s0 is not None, emit_state=emit_state,
                          n_chunks=t // RET_CHUNK),
        grid=(seg.batch, hh),
        in_specs=in_specs,
        out_specs=out_specs,
        out_shape=out_shape,
        scratch_shapes=[pltpu.VMEM((dk, dv), F32), pltpu.VMEM((t, dv), F32)],
        compiler_params=_cparams("parallel", "parallel"),
        name="ret_scan",
    )(*args)
    return res if emit_state else (res[0], None)


def _mm_res_kernel(a_ref, w_ref, x_ref, m_ref, o_ref):
    o_ref[...] = x_ref[...] + m_ref[2] * _dot(a_ref[...], w_ref[...])


def _matmul_residual(x, mods, layer, seg, a, w):
    n, d = x.shape
    x_spec = pl.BlockSpec((ROW_TILE, d), lambda i: (seg.tile0 + i, 0))
    return pl.pallas_call(
        _mm_res_kernel,
        grid=(seg.tiles,),
        in_specs=[
            pl.BlockSpec((ROW_TILE, a.shape[1]), lambda i: (i, 0)),
            pl.BlockSpec(w.shape, lambda i: (0, 0)),
            x_spec,
            _mod_spec(layer, seg, d),
        ],
        out_specs=x_spec,
        out_shape=jax.ShapeDtypeStruct((n, d), F32),
        input_output_aliases={2: 0},
        compiler_params=_cparams("parallel"),
        name="matmul_residual",
    )(a, w, x, mods)


def _fnet_a_kernel(x_ref, m_ref, g_ref, cs_ref, ac_ref, as_ref, *, gd):
    h = _modulate(x_ref[...], g_ref[...], m_ref[0], m_ref[1]).astype(BF16)
    cs = cs_ref[...]
    for g in range(FNET_GROUPS):
        a = _dot(h[:, g * gd:(g + 1) * gd], cs)
        ac_ref[:, g * gd:(g + 1) * gd] = a[:, :gd].astype(BF16)
        as_ref[:, g * gd:(g + 1) * gd] = a[:, gd:].astype(BF16)


def _fnet_a(x, mods, layer, seg, g1, cs):
    n, d = x.shape
    rows = seg.batch * seg.seq
    out = pl.BlockSpec((ROW_TILE, d), lambda i: (i, 0))
    return pl.pallas_call(
        functools.partial(_fnet_a_kernel, gd=d // FNET_GROUPS),
        grid=(seg.tiles,),
        in_specs=[
            pl.BlockSpec((ROW_TILE, d), lambda i: (seg.tile0 + i, 0)),
            _mod_spec(layer, seg, d),
            pl.BlockSpec((1, d), lambda i: (0, 0)),
            pl.BlockSpec(cs.shape, lambda i: (0, 0)),
        ],
        out_specs=[out, out],
        out_shape=[jax.ShapeDtypeStruct((rows, d), BF16)] * 2,
        compiler_params=_cparams("parallel"),
        name="fnet_channel_dft",
    )(x, mods, g1, cs)


def _fnet_b_kernel(ct_ref, st_ref, ac_ref, as_ref, w_ref, x_ref, m_ref, o_ref, *, norm):
    f = (_dot(ct_ref[...], ac_ref[...]) - _dot(st_ref[...], as_ref[...])) * norm
    o_ref[...] = x_ref[...] + m_ref[2] * _dot(f.astype(BF16), w_ref[...])


def _fnet_b(x, mods, layer, seg, ac, as_, ct, st, w, norm):
    n, d = x.shape
    t = seg.seq
    tps = seg.tiles_per_seq
    x_spec = pl.BlockSpec((ROW_TILE, d), lambda b, i: (seg.tile0 + b * tps + i, 0))
    tab = pl.BlockSpec((ROW_TILE, t), lambda b, i: (i, 0))
    seq = pl.BlockSpec((t, d), lambda b, i: (b, 0))
    return pl.pallas_call(
        functools.partial(_fnet_b_kernel, norm=norm),
        grid=(seg.batch, tps),
        in_specs=[tab, tab, seq, seq, pl.BlockSpec(w.shape, lambda b, i: (0, 0)), x_spec,
                  _mod_spec(layer, seg, d, tile_of=lambda b, i: b * tps + i)],
        out_specs=x_spec,
        out_shape=jax.ShapeDtypeStruct((n, d), F32),
        input_output_aliases={5: 0},
        compiler_params=_cparams("parallel", "arbitrary"),
        name="fnet_position_dft",
    )(ct, st, ac, as_, w, x, mods)


def _pack_halves(a):
    w = a.shape[1] // 2
    bits = lambda v: lax.bitcast_convert_type(v.astype(BF16).astype(F32), jnp.uint32)
    return (bits(a[:, :w]) >> 16) | (bits(a[:, w:]) & jnp.uint32(0xFFFF0000))


def _unpack_halves(p):
    lo = lax.bitcast_convert_type(p << 16, F32)
    hi = lax.bitcast_convert_type(p & jnp.uint32(0xFFFF0000), F32)
    return lo, hi


def _router_kernel(x_ref, m_ref, g_ref, rwhi_ref, rwlo_ref, rb_ref, h_ref, idx_ref, rank_ref, wcol_ref, cnt_ref,
                   run_ref, tri_ref):
    step = pl.program_id(0)

    @pl.when(step == 0)
    def _():
        run_ref[...] = jnp.zeros_like(run_ref)
        tt = tri_ref.shape[0]
        earlier = lax.broadcasted_iota(jnp.int32, (tt, tt), 0) < lax.broadcasted_iota(jnp.int32, (tt, tt), 1)
        tri_ref[...] = jnp.where(earlier, 1.0, 0.0).astype(BF16)

    h = _modulate(x_ref[...], g_ref[...], m_ref[3], m_ref[4])
    h_ref[...] = _pack_halves(h)
    h_hi = h.astype(BF16)
    h_lo = (h - h_hi.astype(F32)).astype(BF16)
    logits = _dot_nt(rwhi_ref[...], h_hi) + (_dot_nt(rwhi_ref[...], h_lo) + _dot_nt(rwlo_ref[...], h_hi))
    sc = _sigmoid(logits)
    gr = sc + rb_ref[...]
    gp = EXPERTS_PER_GROUP
    row = lambda a, e: a[e:e + 1, :]
    best_g = None
    for g in range(N_EXPERT_GROUPS):
        vals = [row(gr, g * gp + i) for i in range(gp)]
        gs = None
        for i in range(gp):
            for j in range(i + 1, gp):
                pair = vals[i] + vals[j]
                gs = pair if gs is None else jnp.maximum(gs, pair)
        if best_g is None:
            best_g, best_v = jnp.zeros(gs.shape, jnp.int32), gs
        else:
            better = gs > best_v
            best_g = jnp.where(better, g, best_g)
            best_v = jnp.where(better, gs, best_v)
    sel, raw = [], []
    for i in range(gp):
        s_i, r_i = row(gr, i), row(sc, i)
        for g in range(1, N_EXPERT_GROUPS):
            s_i = jnp.where(best_g == g, row(gr, g * gp + i), s_i)
            r_i = jnp.where(best_g == g, row(sc, g * gp + i), r_i)
        sel.append(s_i)
        raw.append(r_i)

    def argmax_first(vals, raws):
        bi, bv, br = jnp.zeros(vals[0].shape, jnp.int32), vals[0], raws[0]
        for i in range(1, len(vals)):
            better = vals[i] > bv
            bi = jnp.where(better, i, bi)
            bv = jnp.where(better, vals[i], bv)
            br = jnp.where(better, raws[i], br)
        return bi, br

    i1, w1 = argmax_first(sel, raw)
    masked = [jnp.where(i1 == i, -jnp.inf, sel[i]) for i in range(gp)]
    i2, w2 = argmax_first(masked, raw)
    tot = w1 + w2
    e1 = best_g * gp + i1
    e2 = best_g * gp + i2
    idx_ref[0:1, :] = e1
    idx_ref[1:2, :] = e2
    t = e1.shape[1]
    sub = lax.broadcasted_iota(jnp.int32, (8, t), 0)
    w8 = jnp.where(sub == 0, w1 / tot, jnp.where(sub == 1, w2 / tot, 0.0))
    wcol_ref[...] = w8.T
    eio = lax.broadcasted_iota(jnp.int32, (N_EXPERTS, t), 0)
    oh1, oh2 = eio == e1, eio == e2
    oh = jnp.where(oh1, 1.0, jnp.where(oh2, 1.0, 0.0))
    local = _dot(oh.astype(BF16), tri_ref[...])
    rank = local + run_ref[:, 0:1]
    rank_ref[0:1, :] = jnp.sum(jnp.where(oh1, rank, 0.0), axis=0, keepdims=True).astype(jnp.int32)
    rank_ref[1:2, :] = jnp.sum(jnp.where(oh2, rank, 0.0), axis=0, keepdims=True).astype(jnp.int32)
    run_ref[...] = run_ref[...] + jnp.sum(oh, axis=1, keepdims=True)
    cnt_ref[...] = run_ref[...]


def _wide_mod_row(segs, tm):
    ctx, lat = segs
    ctx_tiles = ctx.batch * ctx.seq // tm
    assert ctx_tiles * tm == ctx.batch * ctx.seq and lat.seq % tm == 0
    return lambda i: jnp.where(i < ctx_tiles, ctx.mod0, lat.mod0 + (i - ctx_tiles) // (lat.seq // tm))


def _router(x, mods, layer, segs, g2, rw_hi, rw_lo, rb):
    n, d = x.shape
    tm = WIDE_TILE
    mod_row = _wide_mod_row(segs, tm)
    return pl.pallas_call(
        _router_kernel,
        grid=(n // tm,),
        in_specs=[
            pl.BlockSpec((tm, d), lambda i: (i, 0)),
            pl.BlockSpec((None, None, 6, 1, d), lambda i: (layer, mod_row(i), 0, 0, 0)),
            pl.BlockSpec((1, d), lambda i: (0, 0)),
            pl.BlockSpec(rw_hi.shape, lambda i: (0, 0)),
            pl.BlockSpec(rw_lo.shape, lambda i: (0, 0)),
            pl.BlockSpec(rb.shape, lambda i: (0, 0)),
        ],
        out_specs=[
            pl.BlockSpec((tm, d // 2), lambda i: (i, 0)),
            pl.BlockSpec((2, tm), lambda i: (0, i)),
            pl.BlockSpec((2, tm), lambda i: (0, i)),
            pl.BlockSpec((tm, 8), lambda i: (i, 0)),
            pl.BlockSpec((N_EXPERTS, LANE), lambda i: (0, 0)),
        ],
        out_shape=[
            jax.ShapeDtypeStruct((n, d // 2), jnp.uint32),
            jax.ShapeDtypeStruct((2, n), jnp.int32),
            jax.ShapeDtypeStruct((2, n), jnp.int32),
            jax.ShapeDtypeStruct((n, 8), F32),
            jax.ShapeDtypeStruct((N_EXPERTS, LANE), F32),
        ],
        scratch_shapes=[pltpu.VMEM((N_EXPERTS, LANE), F32), pltpu.VMEM((tm, tm), BF16)],
        compiler_params=_cparams("arbitrary"),
        name="moe_router",
    )(x, mods, g2, rw_hi, rw_lo, rb)


def _expert_kernel(be_ref, nb_ref, xs_ref, wg_ref, wu_ref, wd_ref, y_ref, wg_b, wu_b, wd_b):
    i = pl.program_id(0)
    prev = be_ref[jnp.maximum(i - 1, 0)]

    @pl.when(jnp.logical_or(i == 0, be_ref[i] != prev))
    def _():
        wg_b[...] = wg_ref[...].astype(BF16)
        wu_b[...] = wu_ref[...].astype(BF16)
        wd_b[...] = wd_ref[...].astype(BF16)

    @pl.when(i < nb_ref[0])
    def _():
        lo, hi = _unpack_halves(xs_ref[...])
        lo, hi = lo.astype(BF16), hi.astype(BF16)
        half = lo.shape[1]
        gate = _dot(lo, wg_b[:half, :]) + _dot(hi, wg_b[half:, :])
        up = _dot(lo, wu_b[:half, :]) + _dot(hi, wu_b[half:, :])
        hid = (gate * _sigmoid(gate)) * up
        y_ref[...] = _pack_halves(_dot(hid.astype(BF16), wd_b[...]))

    @pl.when(i >= nb_ref[0])
    def _():
        y_ref[...] = jnp.zeros_like(y_ref)


def _experts(xs, block_e, n_used, w_gate, w_up, w_down, layer):
    rows, half = xs.shape
    d = 2 * half
    de = w_gate.shape[-1]
    n_blocks = rows // MOE_ROWS
    grid_spec = pltpu.PrefetchScalarGridSpec(
        num_scalar_prefetch=2,
        grid=(n_blocks,),
        in_specs=[
            pl.BlockSpec((MOE_ROWS, half), lambda i, be, nb: (i, 0)),
            pl.BlockSpec((None, None, d, de), lambda i, be, nb: (layer, be[i], 0, 0)),
            pl.BlockSpec((None, None, d, de), lambda i, be, nb: (layer, be[i], 0, 0)),
            pl.BlockSpec((None, None, de, d), lambda i, be, nb: (layer, be[i], 0, 0)),
        ],
        out_specs=pl.BlockSpec((MOE_ROWS, half), lambda i, be, nb: (i, 0)),
        scratch_shapes=[pltpu.VMEM((d, de), BF16), pltpu.VMEM((d, de), BF16), pltpu.VMEM((de, d), BF16)],
    )
    return pl.pallas_call(
        _expert_kernel,
        grid_spec=grid_spec,
        out_shape=jax.ShapeDtypeStruct((rows, half), jnp.uint32),
        compiler_params=_cparams("arbitrary"),
        name="moe_experts",
    )(block_e, n_used, xs, w_gate, w_up, w_down)


def _combine_kernel(x_ref, m_ref, y_ref, w_ref, *rest, final):
    if final:
        fg_ref, o_ref = rest
    else:
        (o_ref,) = rest
    w = w_ref[...]
    lo0, hi0 = _unpack_halves(y_ref[0])
    lo1, hi1 = _unpack_halves(y_ref[1])
    w0, w1 = w[:, 0:1], w[:, 1:2]
    y = jnp.concatenate([w0 * lo0 + w1 * lo1, w0 * hi0 + w1 * hi1], axis=-1)
    x = x_ref[...] + m_ref[5] * y
    if final:
        x = _rms(x) * fg_ref[...]
    o_ref[...] = x


def _combine(x, mods, layer, segs, seg, yg, wcol, final_g):
    n, d = x.shape
    final = final_g is not None
    tm = WIDE_TILE
    mod_row = _wide_mod_row(segs, tm)
    t0 = seg.row0 // tm if final else 0
    steps = (seg.batch * seg.seq if final else n) // tm
    x_spec = pl.BlockSpec((tm, d), lambda i: (t0 + i, 0))
    in_specs = [x_spec,
                pl.BlockSpec((None, None, 6, 1, d), lambda i: (layer, mod_row(t0 + i), 0, 0, 0)),
                pl.BlockSpec((2, tm, d // 2), lambda i: (0, t0 + i, 0)),
                pl.BlockSpec((tm, 8), lambda i: (t0 + i, 0))]
    args = [x, mods, yg, wcol]
    if final:
        in_specs.append(pl.BlockSpec((1, d), lambda i: (0, 0)))
        args.append(final_g)
        out_spec = pl.BlockSpec((tm, d), lambda i: (i, 0))
        out_shape = jax.ShapeDtypeStruct((steps * tm, d), F32)
        aliases = {}
    else:
        out_spec, out_shape, aliases = x_spec, jax.ShapeDtypeStruct((n, d), F32), {0: 0}
    return pl.pallas_call(
        functools.partial(_combine_kernel, final=final),
        grid=(steps,),
        in_specs=in_specs,
        out_specs=out_spec,
        out_shape=out_shape,
        input_output_aliases=aliases,
        compiler_params=_cparams("parallel"),
        name="moe_combine_final" if final else "moe_combine",
    )(*args)


def _sc_mesh():
    return plsc.VectorSubcoreMesh(core_axis_name="c", subcore_axis_name="s")


def _sc_worker_split(n):
    workers = SC_CORES * SC_SUBCORES
    per = n // workers
    assert per * workers == n and per % SC_CHUNK == 0
    return workers, per, per // SC_CHUNK


def _sc_dispatch(h, pos, rows):
    n, w = h.shape
    workers, per, chunks = _sc_worker_split(n)

    @functools.partial(
        pl.kernel, out_type=jax.ShapeDtypeStruct((rows, w), h.dtype), mesh=_sc_mesh(),
        scratch_types=[pltpu.VMEM((2, chunks, SC_CHUNK), jnp.int32), pltpu.VMEM((SC_CHUNK, w), h.dtype)],
        name="moe_dispatch_scatter")
    def scatter_rows(h_hbm, pos_hbm, xs_hbm, idx_v, rows_v):
        wid = lax.axis_index("s") * SC_CORES + lax.axis_index("c")
        pltpu.sync_copy(pos_hbm.at[0, wid], idx_v.at[0])
        pltpu.sync_copy(pos_hbm.at[1, wid], idx_v.at[1])

        @pl.loop(0, chunks)
        def _(c):
            pltpu.sync_copy(h_hbm.at[pl.ds(wid * per + c * SC_CHUNK, SC_CHUNK)], rows_v)
            pltpu.sync_copy(rows_v, xs_hbm.at[idx_v.at[0, c]])
            pltpu.sync_copy(rows_v, xs_hbm.at[idx_v.at[1, c]])

    return scatter_rows(h, pos.reshape(2, workers, chunks, SC_CHUNK))


def _sc_gather2(ys, pos):
    _, w = ys.shape
    n = pos.shape[1]
    workers, per, chunks = _sc_worker_split(n)

    @functools.partial(
        pl.kernel, out_type=jax.ShapeDtypeStruct((2, n, w), ys.dtype), mesh=_sc_mesh(),
        scratch_types=[pltpu.VMEM((2, chunks, SC_CHUNK), jnp.int32), pltpu.VMEM((SC_CHUNK, w), ys.dtype),
                       pltpu.SemaphoreType.DMA],
        name="moe_combine_gather")
    def gather_rows(ys_hbm, pos_hbm, out_hbm, idx_v, rows_v, sem):
        wid = lax.axis_index("s") * SC_CORES + lax.axis_index("c")
        pltpu.sync_copy(pos_hbm.at[0, wid], idx_v.at[0])
        pltpu.sync_copy(pos_hbm.at[1, wid], idx_v.at[1])

        @pl.loop(0, chunks)
        def _(c):
            for k in range(2):
                pltpu.async_copy(ys_hbm.at[idx_v.at[k, c]], rows_v, sem).wait()
                pltpu.sync_copy(rows_v, out_hbm.at[k, pl.ds(wid * per + c * SC_CHUNK, SC_CHUNK)])

    return gather_rows(ys, pos.reshape(2, workers, chunks, SC_CHUNK))


def _dispatch_plan(idx, rank, counts):
    n = idx.shape[1]
    padded = (counts + MOE_ROWS - 1) // MOE_ROWS * MOE_ROWS
    pad_end = jnp.cumsum(padded)
    pad_start = pad_end - padded
    experts = jnp.arange(N_EXPERTS, dtype=jnp.int32)
    start_of = jnp.sum(jnp.where(idx[..., None] == experts, pad_start, 0), axis=-1)
    pos = start_of + rank
    n_blocks = 2 * n // MOE_ROWS + N_EXPERTS
    starts = jnp.arange(n_blocks, dtype=jnp.int32) * MOE_ROWS
    block_e = jnp.minimum(jnp.sum(starts[:, None] >= pad_end[None, :], axis=1), N_EXPERTS - 1).astype(jnp.int32)
    n_used = (pad_end[-1] // MOE_ROWS).astype(jnp.int32).reshape(1)
    return pos, block_e, n_used, n_blocks * MOE_ROWS


def _moe(x, mods, layer, segs, g2, rw_hi, rw_lo, rb, w_gate, w_up, w_down, final_g):
    h2p, idx, rank, wcol, cnt = _router(x, mods, layer, segs, g2, rw_hi, rw_lo, rb)
    pos, block_e, n_used, rows = _dispatch_plan(idx, rank, cnt[:, 0].astype(jnp.int32))
    xs = _sc_dispatch(h2p, pos, rows)
    ys = _experts(xs, block_e, n_used, w_gate, w_up, w_down, layer)
    yg = _sc_gather2(ys, pos)
    if final_g is None:
        return _combine(x, mods, layer, segs, None, yg, wcol, None)
    return tuple(_combine(x, mods, layer, segs, seg, yg, wcol, final_g) for seg in segs)


def _mla_rope_tables(t):
    axis_dim = MLA_ROPE // 2
    row = np.repeat(np.arange(t // GRID_W), GRID_W).astype(np.float64)
    col = np.tile(np.arange(GRID_W), t // GRID_W).astype(np.float64)
    inv = ROPE_BASE ** (-np.arange(0, axis_dim, 2, dtype=np.float64) / axis_dim)
    ar, ac = row[:, None] * inv[None, :], col[:, None] * inv[None, :]
    ones = np.ones((t, LANE - MLA_ROPE))
    cos = np.concatenate([np.cos(ar), np.cos(ar), np.cos(ac), np.cos(ac), ones], axis=-1)
    sin = np.concatenate([-np.sin(ar), np.sin(ar), -np.sin(ac), np.sin(ac), 0.0 * ones], axis=-1)
    return jnp.asarray(cos, F32), jnp.asarray(sin, F32)


def _ret_rot_tables(t, dk):
    inv = ROPE_BASE ** (-np.linspace(0.0, 1.0, dk // 2))
    ang = np.arange(t, dtype=np.float64)[:, None] * inv[None, :]
    return jnp.asarray(np.cos(ang), F32), jnp.asarray(np.sin(ang), F32)


def _dft_tables(n):
    k = np.arange(n, dtype=np.int64)
    ang = (np.outer(k, k) % n).astype(np.float64) * (2.0 * math.pi / n)
    return jnp.asarray(np.cos(ang), BF16), jnp.asarray(np.sin(ang), BF16)


def _mla_weights(w_in, q_g, kv_g, w_uq, w_ukv, w_o):
    d = w_in.shape[0]
    hd = MLA_NOPE + MLA_ROPE
    w_in_p = jnp.concatenate([w_in, jnp.zeros((d, LANE - MLA_ROPE), w_in.dtype)], axis=1)
    uq = w_uq.reshape(MLA_Q_LORA, MLA_HEADS, hd)
    uq = jnp.concatenate([uq, jnp.zeros((MLA_Q_LORA, MLA_HEADS, 2 * LANE - hd), uq.dtype)], axis=-1)
    ukv = w_ukv.reshape(MLA_KV_LORA, MLA_HEADS, MLA_NOPE + MLA_V)
    return {
        "w_in": w_in_p.astype(BF16),
        "q_g": q_g.reshape(1, -1),
        "kv_g": kv_g.reshape(1, -1),
        "w_q": uq.reshape(MLA_Q_LORA, MLA_HEADS * 2 * LANE).astype(BF16),
        "w_kn": ukv[..., :MLA_NOPE].reshape(MLA_KV_LORA, MLA_HEADS * MLA_NOPE).astype(BF16),
        "w_v": ukv[..., MLA_NOPE:].reshape(MLA_KV_LORA, MLA_HEADS * MLA_V).astype(BF16),
        "w_o": w_o.astype(BF16),
    }


def kernel(x_prompt, x_sample, cache_mla, state_ret, c, c_ctx, norm1_g, norm2_g, ada_w, ada_b, final_norm_g,
           mla_w_in, mla_q_norm_g, mla_kv_norm_g, mla_w_uq, mla_w_ukv, mla_w_o, ret_w_in, ret_decay_f,
           ret_decay_b, ret_w_o, fnet_w, router_w, router_b, moe_w_gate, moe_w_up, moe_w_down):
    b_ctx, t_ctx, d = x_prompt.shape
    b_lat, t_lat, _ = x_sample.shape
    depth = ada_w.shape[0]
    assert t_ctx == ROW_TILE and t_lat % ROW_TILE == 0 and b_lat + 1 <= 8
    n_ctx = b_ctx * t_ctx
    ctx = _Seg(0, b_ctx, t_ctx, 0, False)
    lat = _Seg(n_ctx, b_lat, t_lat, 1, True)
    segs = (ctx, lat)

    n_lat = b_lat * t_lat
    n_mla = mla_w_in.shape[0]
    assert n_mla >= 1
    x, new_cache = None, None
    cond8 = jnp.concatenate([c_ctx[None, :], c, jnp.zeros((8 - 1 - b_lat, d), F32)], axis=0)
    mods = _modulation_all(cond8, ada_w, ada_b).reshape(depth, 8, 6, 1, d)

    rw_t = router_w.T.astype(F32)
    rw_hi = rw_t.astype(BF16)
    rw_lo = (rw_t - rw_hi.astype(F32)).astype(BF16)
    rb = router_b.reshape(N_EXPERTS, 1).astype(F32)
    final_g = final_norm_g.reshape(1, d)
    dk = ret_w_in.shape[2] // (8 * RET_HEADS)
    dv = 2 * dk

    states = []
    counters = [0, 0, 0]
    for layer in range(depth):
        kind = layer % 3
        j = counters[kind]
        counters[kind] += 1
        g1 = norm1_g[layer].reshape(1, d)
        g2 = norm2_g[layer].reshape(1, d)
        if kind == 0:
            w = _mla_weights(mla_w_in[j], mla_q_norm_g[j], mla_kv_norm_g[j], mla_w_uq[j], mla_w_ukv[j],
                             mla_w_o[j])
            if x is None:
                xc, xc0, xl, xl0 = x_prompt.reshape(n_ctx, d), 0, x_sample.reshape(n_lat, d), 0
            else:
                xc, xc0, xl, xl0 = x, ctx.tile0, x, lat.tile0
            qc, kc, vc, new_cache = _mla_proj(xc, xc0, mods, layer, ctx, g1, w, None, (new_cache, j, n_mla))
            ql, kl, vl = _mla_proj(xl, xl0, mods, layer, lat, g1, w, _mla_rope_tables(t_lat))
            past = cache_mla.shape[2]
            cpad = jnp.pad(cache_mla[:, j].reshape(b_lat * past, -1), ((0, 0), (0, LANE - MLA_ROPE)))
            kp, vp = _cache_kv(cpad, w)
            first = x is None
            x = _attention(xc, xc0, n_ctx + n_lat, "fresh" if first else "inplace", mods, layer, ctx, qc,
                           [(kc, vc, t_ctx)], w["w_o"])
            x = _attention(xl if first else x, xl0, n_ctx + n_lat, x if first else "inplace", mods, layer, lat,
                           ql, [(kl, vl, t_lat), (kp, vp, past)], w["w_o"])
        elif kind == 1:
            w_in = ret_w_in[j]
            qk = RET_HEADS * dk
            k_scale = jnp.concatenate([jnp.ones((qk,), F32), jnp.full((qk,), dk ** -0.5, F32),
                                       jnp.ones((w_in.shape[1] - 2 * qk,), F32)])
            w_in_b = (w_in * k_scale[None, :]).astype(BF16)
            w_o_b = ret_w_o[j].astype(BF16)
            rot = _ret_rot_tables(t_lat, dk)
            n_qk, n_v = 2 * qk // RET_COL, RET_HEADS * dv // RET_COL
            assert (n_qk + n_v) * RET_COL * 2 == w_in.shape[1]
            parts = []
            for seg in segs:
                qk_kind = "rotary" if seg is lat else "plain"
                parts.append((
                    _ret_proj(x, mods, layer, seg, g1, w_in_b, 0, (qk_kind,) * n_qk + ("plain",) * n_v, rot, dk),
                    _ret_proj(x, mods, layer, seg, g1, w_in_b, 1, ("silu",) * (n_qk + n_v), None, dk)))
            yc, s_ctx = _ret_scan(*parts[0], ctx, ret_decay_f[j], ret_decay_b[j], None, True, dk, dv)
            yl, _ = _ret_scan(*parts[1], lat, ret_decay_f[j], ret_decay_b[j], state_ret[:, j], False, dk, dv)
            x = _matmul_residual(x, mods, layer, ctx, yc, w_o_b)
            x = _matmul_residual(x, mods, layer, lat, yl, w_o_b)
            states.append(s_ctx)
        else:
            gd = d // FNET_GROUPS
            cc, sc = _dft_tables(gd)
            cs = jnp.concatenate([cc, sc], axis=1)
            w_b = fnet_w[j].astype(BF16)
            for seg in segs:
                ct, st = _dft_tables(seg.seq)
                ac, as_ = _fnet_a(x, mods, layer, seg, g1, cs)
                x = _fnet_b(x, mods, layer, seg, ac, as_, ct, st, w_b, (seg.seq * gd) ** -0.5)
        last = layer == depth - 1
        x = _moe(x, mods, layer, segs, g2, rw_hi, rw_lo, rb, moe_w_gate, moe_w_up, moe_w_down,
                 final_g if last else None)

    y_prompt, y_sample = x
    new_state = jnp.stack(states, axis=1)
    return (y_prompt.reshape(b_ctx, t_ctx, d), y_sample.reshape(b_lat, t_lat, d), new_cache, new_state)
```

```python
import functools
import math

import jax
import jax.numpy as jnp
import numpy as np
from jax import lax
from jax.experimental import pallas as pl
from jax.experimental.pallas import tpu as pltpu
from jax.experimental.pallas import tpu_sc as plsc

F32 = jnp.float32
BF16 = jnp.bfloat16

GRID_W = 64
MLA_HEADS = 8
MLA_NOPE = 128
MLA_ROPE = 64
MLA_V = 128
MLA_Q_LORA = 384
MLA_KV_LORA = 256
ROPE_BASE = 10000.0
RET_HEADS = 4
RET_CHUNK = 256
FNET_GROUPS = 4
N_EXPERTS = 16
N_EXPERT_GROUPS = 4
EXPERTS_PER_GROUP = 4
D_EXPERT = 512
NORM_EPS = 1e-6

LANE = 128
ROW_TILE = 256
WIDE_TILE = 512
MOE_ROWS = 512
VMEM_LIMIT = 56 * 1024 * 1024
SC_CORES = 2
SC_SUBCORES = 16
SC_CHUNK = 128


def _cparams(*sem):
    return pltpu.CompilerParams(dimension_semantics=sem, vmem_limit_bytes=VMEM_LIMIT)


def _sigmoid(x):
    return 1.0 / (1.0 + jnp.exp(-x))


def _rms(x):
    return x * lax.rsqrt(jnp.mean(x * x, axis=-1, keepdims=True) + NORM_EPS)


def _modulate(x, g, shift, scale):
    return (_rms(x) * g) * (1.0 + scale) + shift


def _dot(a, b):
    return jnp.dot(a, b, preferred_element_type=F32)


def _dot_nt(a, b):
    return lax.dot_general(a, b, (((1,), (1,)), ((), ())), preferred_element_type=F32)


def _mod_kernel(c_ref, w_ref, b_ref, o_ref):
    c = c_ref[...]
    s = (c * _sigmoid(c)).astype(BF16)
    o_ref[...] = _dot(s, w_ref[...].astype(BF16)) + b_ref[...]


def _modulation_all(cond8, ada_w, ada_b):
    depth, d, d6 = ada_w.shape
    tn = d6 // 4
    return pl.pallas_call(
        _mod_kernel,
        grid=(depth, d6 // tn),
        in_specs=[
            pl.BlockSpec((8, d), lambda l, n: (0, 0)),
            pl.BlockSpec((None, d, tn), lambda l, n: (l, 0, n)),
            pl.BlockSpec((None, 1, tn), lambda l, n: (l, 0, n)),
        ],
        out_specs=pl.BlockSpec((None, 8, tn), lambda l, n: (l, 0, n)),
        out_shape=jax.ShapeDtypeStruct((depth, 8, d6), F32),
        compiler_params=_cparams("parallel", "parallel"),
        name="modulation",
    )(cond8, ada_w, ada_b.reshape(depth, 1, d6))


class _Seg:
    def __init__(self, row0, batch, seq, mod0, per_batch_mod):
        self.row0, self.batch, self.seq = row0, batch, seq
        self.mod0, self.per_batch_mod = mod0, per_batch_mod
        self.tiles = batch * seq // ROW_TILE
        self.tile0 = row0 // ROW_TILE
        self.tiles_per_seq = seq // ROW_TILE

    def mod_row(self, tile):
        if self.per_batch_mod:
            return self.mod0 + tile // self.tiles_per_seq
        return self.mod0


def _mod_spec(layer, seg, d, tile_of=lambda *a: a[0]):
    return pl.BlockSpec((None, None, 6, 1, d), lambda *a: (layer, seg.mod_row(tile_of(*a)), 0, 0, 0))


def _rope_partner(x):
    lane = lax.broadcasted_iota(jnp.int32, x.shape, 1)
    first = (lane % 32) < 16
    return jnp.where(first, pltpu.roll(x, LANE - 16, 1), pltpu.roll(x, 16, 1))


def _store_values(v_ref, v):
    ones = jnp.ones((v.shape[0], LANE), BF16)
    for hd in range(MLA_HEADS):
        v_ref[:, hd * 2 * LANE:hd * 2 * LANE + LANE] = v[:, hd * MLA_V:(hd + 1) * MLA_V].astype(BF16)
        v_ref[:, hd * 2 * LANE + LANE:(hd + 1) * 2 * LANE] = ones


def _mla_proj_kernel(*refs, rope, scale):
    if rope:
        (x_ref, m_ref, g_ref, win_ref, qg_ref, kvg_ref, wq_ref, wkn_ref, wv_ref, cos_ref, sin_ref,
         q_ref, k_ref, v_ref) = refs
    else:
        x_ref, m_ref, g_ref, win_ref, qg_ref, kvg_ref, wq_ref, wkn_ref, wv_ref = refs[:9]
        q_ref, k_ref, v_ref, cache_ref = refs[-4:]
    h = _modulate(x_ref[...], g_ref[...], m_ref[0], m_ref[1]).astype(BF16)
    z = _dot(h, win_ref[...])
    cq = z[:, :MLA_Q_LORA]
    ckv = z[:, MLA_Q_LORA:MLA_Q_LORA + MLA_KV_LORA]
    kpe = z[:, MLA_Q_LORA + MLA_KV_LORA:]
    cqn = (_rms(cq) * qg_ref[...]).astype(BF16)
    ckvn = _rms(ckv) * kvg_ref[...]
    ckvb = ckvn.astype(BF16)
    q = _dot(cqn, wq_ref[...]) * scale
    kn = _dot(ckvb, wkn_ref[...])
    _store_values(v_ref, _dot(ckvb, wv_ref[...]))
    if rope:
        cos, sin = cos_ref[...], sin_ref[...]
        kpe = kpe * cos + _rope_partner(kpe) * sin
    else:
        cache_ref[:, :MLA_KV_LORA] = ckvn
        cache_ref[:, MLA_KV_LORA:] = kpe[:, :MLA_ROPE]
    kpe_b = kpe.astype(BF16)
    for hd in range(MLA_HEADS):
        lo = hd * 2 * LANE
        q_ref[:, lo:lo + LANE] = q[:, lo:lo + LANE].astype(BF16)
        qr = q[:, lo + LANE:lo + 2 * LANE]
        if rope:
            qr = qr * cos + _rope_partner(qr) * sin
        q_ref[:, lo + LANE:lo + 2 * LANE] = qr.astype(BF16)
        k_ref[:, lo:lo + LANE] = kn[:, hd * LANE:(hd + 1) * LANE].astype(BF16)
        k_ref[:, lo + LANE:lo + 2 * LANE] = kpe_b


def _mla_proj(x, x_tile0, mods, layer, seg, g1, w, rope_tabs, cache_slot=None):
    n, d = x.shape
    rope = rope_tabs is not None
    rows = seg.batch * seg.seq
    hq = MLA_HEADS * 2 * LANE
    hv = MLA_HEADS * MLA_V
    const = lambda i: (0, 0)
    aliases = {}
    in_specs = [
        pl.BlockSpec((ROW_TILE, d), lambda i: (x_tile0 + i, 0)),
        _mod_spec(layer, seg, d),
        pl.BlockSpec((1, d), const),
        pl.BlockSpec(w["w_in"].shape, const),
        pl.BlockSpec((1, MLA_Q_LORA), const),
        pl.BlockSpec((1, MLA_KV_LORA), const),
        pl.BlockSpec(w["w_q"].shape, const),
        pl.BlockSpec(w["w_kn"].shape, const),
        pl.BlockSpec(w["w_v"].shape, const),
    ]
    args = [x, mods, g1, w["w_in"], w["q_g"], w["kv_g"], w["w_q"], w["w_kn"], w["w_v"]]
    out_specs = [
        pl.BlockSpec((ROW_TILE, hq), lambda i: (i, 0)),
        pl.BlockSpec((ROW_TILE, hq), lambda i: (i, 0)),
        pl.BlockSpec((ROW_TILE, hq), lambda i: (i, 0)),
    ]
    out_shape = [jax.ShapeDtypeStruct((rows, hq), BF16)] * 3
    if rope:
        tab = pl.BlockSpec((ROW_TILE, LANE), lambda i: (i % seg.tiles_per_seq, 0))
        in_specs += [tab, tab]
        args += list(rope_tabs)
    else:
        cw = MLA_KV_LORA + MLA_ROPE
        prev, slot, n_slots = cache_slot
        assert seg.seq == ROW_TILE
        out_specs.append(pl.BlockSpec((None, None, ROW_TILE, cw), lambda i: (i, slot, 0, 0)))
        out_shape.append(jax.ShapeDtypeStruct((seg.batch, n_slots, seg.seq, cw), F32))
        in_specs.append(pl.BlockSpec(memory_space=pl.ANY))
        args.append(prev)
        aliases = {len(args) - 1: 3}
    return pl.pallas_call(
        functools.partial(_mla_proj_kernel, rope=rope, scale=(MLA_NOPE + MLA_ROPE) ** -0.5),
        grid=(seg.tiles,),
        in_specs=in_specs,
        out_specs=out_specs,
        out_shape=out_shape,
        input_output_aliases=aliases,
        compiler_params=_cparams("parallel"),
        name="mla_proj_lat" if rope else "mla_proj_ctx",
    )(*args)


def _cache_kv_kernel(c_ref, wkn_ref, wv_ref, k_ref, v_ref):
    c = c_ref[...]
    ckv = c[:, :MLA_KV_LORA].astype(BF16)
    kpe_b = c[:, MLA_KV_LORA:].astype(BF16)
    kn = _dot(ckv, wkn_ref[...])
    _store_values(v_ref, _dot(ckv, wv_ref[...]))
    for hd in range(MLA_HEADS):
        lo = hd * 2 * LANE
        k_ref[:, lo:lo + LANE] = kn[:, hd * LANE:(hd + 1) * LANE].astype(BF16)
        k_ref[:, lo + LANE:lo + 2 * LANE] = kpe_b


def _cache_kv(cache_pad, w):
    rows, cw = cache_pad.shape
    hq = MLA_HEADS * 2 * LANE
    hv = MLA_HEADS * MLA_V
    const = lambda i: (0, 0)
    return pl.pallas_call(
        _cache_kv_kernel,
        grid=(rows // ROW_TILE,),
        in_specs=[
            pl.BlockSpec((ROW_TILE, cw), lambda i: (i, 0)),
            pl.BlockSpec(w["w_kn"].shape, const),
            pl.BlockSpec(w["w_v"].shape, const),
        ],
        out_specs=[pl.BlockSpec((ROW_TILE, hq), lambda i: (i, 0))] * 2,
        out_shape=[jax.ShapeDtypeStruct((rows, hq), BF16)] * 2,
        compiler_params=_cparams("parallel"),
        name="mla_cache_kv",
    )(cache_pad, w["w_kn"], w["w_v"])


def _attn_kernel(*refs, n_parts):
    q_ref = refs[0]
    kv_refs = refs[1:1 + 2 * n_parts]
    wo_ref, x_ref, m_ref = refs[1 + 2 * n_parts:4 + 2 * n_parts]
    o_ref, acc_ref = refs[-2:]
    for hd in range(MLA_HEADS):
        qh = q_ref[:, hd * 2 * LANE:(hd + 1) * 2 * LANE]
        scores = [_dot_nt(qh, kv_refs[2 * p][:, hd * 2 * LANE:(hd + 1) * 2 * LANE]) for p in range(n_parts)]
        mx = scores[0].max(axis=-1, keepdims=True)
        for s in scores[1:]:
            mx = jnp.maximum(mx, s.max(axis=-1, keepdims=True))
        out = None
        for p, s in enumerate(scores):
            e = jnp.exp((s - mx).astype(BF16))
            pv = _dot(e, kv_refs[2 * p + 1][:, hd * 2 * LANE:(hd + 1) * 2 * LANE])
            out = pv if out is None else out + pv
        acc_ref[:, hd * MLA_V:(hd + 1) * MLA_V] = (out[:, :MLA_V] / out[:, LANE:LANE + MLA_V]).astype(BF16)
    y = _dot(acc_ref[...], wo_ref[...])
    o_ref[...] = x_ref[...] + m_ref[2] * y


def _attention(x, x_tile0, n, dest, mods, layer, seg, q, kv_parts, w_o):
    d = x.shape[1]
    hq = MLA_HEADS * 2 * LANE
    hv = MLA_HEADS * MLA_V
    tps = seg.tiles_per_seq
    in_specs = [pl.BlockSpec((ROW_TILE, hq), lambda b, i: (b * tps + i, 0))]
    args = [q]
    for k, v, rows in kv_parts:
        in_specs += [pl.BlockSpec((rows, hq), lambda b, i: (b, 0), pipeline_mode=pl.Buffered(1))] * 2
        args += [k, v]
    in_specs += [
        pl.BlockSpec(w_o.shape, lambda b, i: (0, 0)),
        pl.BlockSpec((ROW_TILE, d), lambda b, i: (x_tile0 + b * tps + i, 0)),
        _mod_spec(layer, seg, d, tile_of=lambda b, i: b * tps + i),
    ]
    args += [w_o, x, mods]
    if isinstance(dest, str):
        assert dest == "inplace"
        aliases = {len(args) - 2: 0}
    else:
        in_specs.append(pl.BlockSpec(memory_space=pl.ANY))
        args.append(dest)
        aliases = {len(args) - 1: 0}
    return pl.pallas_call(
        functools.partial(_attn_kernel, n_parts=len(kv_parts)),
        grid=(seg.batch, tps),
        in_specs=in_specs,
        out_specs=pl.BlockSpec((ROW_TILE, d), lambda b, i: (seg.tile0 + b * tps + i, 0)),
        out_shape=jax.ShapeDtypeStruct((n, d), F32),
        scratch_shapes=[pltpu.VMEM((ROW_TILE, hv), BF16)],
        input_output_aliases=aliases,
        compiler_params=_cparams("parallel", "arbitrary"),
        name="mla_attention",
    )(*args)


RET_COL = 1024


RET_ROWS = 512


def _ret_proj_kernel(*refs, kinds, dk):
    rotary = "rotary" in kinds
    if rotary:
        x_ref, m_ref, g_ref, w_ref, cos_ref, sin_ref, z_ref = refs
        cos, sin = cos_ref[...], sin_ref[...]
    else:
        x_ref, m_ref, g_ref, w_ref, z_ref = refs
    h = _modulate(x_ref[...], g_ref[...], m_ref[0], m_ref[1]).astype(BF16)
    half = dk // 2
    for j, kind in enumerate(kinds):
        c0 = j * RET_COL
        acc = _dot(h, w_ref[:, c0:c0 + RET_COL])
        if kind == "rotary":
            for hd in range(RET_COL // dk):
                lo = hd * dk
                x1, x2 = acc[:, lo:lo + half], acc[:, lo + half:lo + dk]
                z_ref[:, c0 + lo:c0 + lo + half] = (x1 * cos - x2 * sin).astype(BF16)
                z_ref[:, c0 + lo + half:c0 + lo + dk] = (x1 * sin + x2 * cos).astype(BF16)
        elif kind == "silu":
            z_ref[:, c0:c0 + RET_COL] = (acc * _sigmoid(acc)).astype(BF16)
        else:
            z_ref[:, c0:c0 + RET_COL] = acc.astype(BF16)


def _ret_proj(x, mods, layer, seg, g1, w_in, group, kinds, rot_tabs, dk):
    n, d = x.shape
    rows = seg.batch * seg.seq
    tm = RET_ROWS
    ncol = len(kinds) * RET_COL
    tps = max(seg.seq // tm, 1)
    in_specs = [
        pl.BlockSpec((tm, d), lambda i: (seg.row0 // tm + i, 0)),
        pl.BlockSpec((None, None, 6, 1, d),
                     lambda i: (layer, seg.mod0 + (i * tm // seg.seq if seg.per_batch_mod else 0), 0, 0, 0)),
        pl.BlockSpec((1, d), lambda i: (0, 0)),
        pl.BlockSpec((d, ncol), lambda i: (0, group)),
    ]
    args = [x, mods, g1, w_in]
    if "rotary" in kinds:
        tab = pl.BlockSpec((tm, dk // 2), lambda i: (i % tps, 0))
        in_specs += [tab, tab]
        args += list(rot_tabs)
    return pl.pallas_call(
        functools.partial(_ret_proj_kernel, kinds=kinds, dk=dk),
        grid=(rows // tm,),
        in_specs=in_specs,
        out_specs=pl.BlockSpec((tm, ncol), lambda i: (i, 0)),
        out_shape=jax.ShapeDtypeStruct((rows, ncol), BF16),
        compiler_params=_cparams("parallel"),
        name="ret_proj_" + kinds[0],
    )(*args)


def _log_sigmoid(x):
    return jnp.minimum(x, 0.0) - jnp.log(1.0 + jnp.exp(-jnp.abs(x)))


def _ret_scan_kernel(*refs, has_s0, emit_state, n_chunks):
    refs = list(refs)
    lf_ref, lb_ref, q_ref, k_ref, v_ref, gf_ref, gb_ref = refs[:7]
    pos = 7
    s0_ref = None
    if has_s0:
        s0_ref = refs[pos]
        pos += 1
    y_ref = refs[pos]
    pos += 1
    sout_ref = None
    if emit_state:
        sout_ref = refs[pos]
        pos += 1
    s_ref, yf_ref = refs[pos:]
    c = RET_CHUNK
    ii = lax.broadcasted_iota(jnp.int32, (c, c), 0).astype(F32)
    jj = lax.broadcasted_iota(jnp.int32, (c, c), 1).astype(F32)
    idx = lax.broadcasted_iota(jnp.int32, (c, 1), 0).astype(F32)

    for direction in range(2):
        fwd = direction == 0
        lg = _log_sigmoid((lf_ref if fwd else lb_ref)[...])
        rel = (ii - jj) if fwd else (jj - ii)
        keep = rel >= 0
        decay_in = jnp.where(keep, jnp.exp(jnp.where(keep, rel, 0.0) * lg), 0.0)
        decay_q = jnp.exp(((idx + 1.0) if fwd else (c - idx)) * lg)
        decay_k = jnp.exp(((c - 1.0 - idx) if fwd else idx) * lg)
        decay_c = jnp.exp(c * lg)
        g_ref = gf_ref if fwd else gb_ref

        def chunk(cc, state, fwd=fwd, decay_in=decay_in, decay_q=decay_q, decay_k=decay_k, decay_c=decay_c,
                  g_ref=g_ref):
            r0 = cc * c if isinstance(cc, int) else pl.multiple_of(cc * c, c)
            qc = q_ref[pl.ds(r0, c), :]
            kc = k_ref[pl.ds(r0, c), :]
            vc = v_ref[pl.ds(r0, c), :]
            sc = _dot_nt(qc, kc) * decay_in
            out = _dot(sc.astype(BF16), vc)
            kd_t = (kc.astype(F32) * decay_k).T.astype(BF16)
            new_s = _dot(kd_t, vc)
            if state is not None:
                out = out + decay_q * _dot(qc, state.astype(BF16))
                new_s = decay_c * state + new_s
            s_ref[...] = new_s
            o = _rms(out) * g_ref[pl.ds(r0, c), :].astype(F32)
            if fwd:
                yf_ref[pl.ds(r0, c), :] = o
            else:
                y_ref[pl.ds(r0, c), :] = (yf_ref[pl.ds(r0, c), :] + o).astype(BF16)

        chunk(0 if fwd else n_chunks - 1, s0_ref[direction] if has_s0 else None)

        def step(ci, carry, fwd=fwd, chunk=chunk):
            chunk(ci if fwd else n_chunks - 1 - ci, s_ref[...])
            return carry

        lax.fori_loop(1, n_chunks, step, 0)
        if emit_state:
            sout_ref[direction] = s_ref[...]


def _ret_scan(qkv, g, seg, logit_f, logit_b, s0, emit_state, dk, dv):
    rows = seg.batch * seg.seq
    t = seg.seq
    hh = RET_HEADS
    v0 = 2 * hh * dk // dv
    in_specs = [
        pl.BlockSpec((None, 1, 1), lambda b, h: (h, 0, 0)),
        pl.BlockSpec((None, 1, 1), lambda b, h: (h, 0, 0)),
        pl.BlockSpec((t, dk), lambda b, h: (b, h)),
        pl.BlockSpec((t, dk), lambda b, h: (b, hh + h)),
        pl.BlockSpec((t, dv), lambda b, h: (b, v0 + h)),
        pl.BlockSpec((t, dv), lambda b, h: (b, h)),
        pl.BlockSpec((t, dv), lambda b, h: (b, hh + h)),
    ]
    args = [logit_f.reshape(hh, 1, 1), logit_b.reshape(hh, 1, 1), qkv, qkv, qkv, g, g]
    state_spec = pl.BlockSpec((None, 2, None, dk, dv), lambda b, h: (b, 0, h, 0, 0))
    if s0 is not None:
        in_specs.append(state_spec)
        args.append(s0)
    out_specs = [pl.BlockSpec((t, dv), lambda b, h: (b, h))]
    out_shape = [jax.ShapeDtypeStruct((rows, hh * dv), BF16)]
    if emit_state:
        out_specs.append(state_spec)
        out_shape.append(jax.ShapeDtypeStruct((seg.batch, 2, hh, dk, dv), F32))
    res = pl.pallas_call(
        functools.partial(_ret_scan_kernel, has_s0=s0 is not None, emit_state=emit_state,
                          n_chunks=t // RET_CHUNK),
        grid=(seg.batch, hh),
        in_specs=in_specs,
        out_specs=out_specs,
        out_shape=out_shape,
        scratch_shapes=[pltpu.VMEM((dk, dv), F32), pltpu.VMEM((t, dv), F32)],
        compiler_params=_cparams("parallel", "parallel"),
        name="ret_scan",
    )(*args)
    return res if emit_state else (res[0], None)


def _mm_res_kernel(a_ref, w_ref, x_ref, m_ref, o_ref):
    o_ref[...] = x_ref[...] + m_ref[2] * _dot(a_ref[...], w_ref[...])


def _matmul_residual(x, mods, layer, seg, a, w):
    n, d = x.shape
    x_spec = pl.BlockSpec((ROW_TILE, d), lambda i: (seg.tile0 + i, 0))
    return pl.pallas_call(
        _mm_res_kernel,
        grid=(seg.tiles,),
        in_specs=[
            pl.BlockSpec((ROW_TILE, a.shape[1]), lambda i: (i, 0)),
            pl.BlockSpec(w.shape, lambda i: (0, 0)),
            x_spec,
            _mod_spec(layer, seg, d),
        ],
        out_specs=x_spec,
        out_shape=jax.ShapeDtypeStruct((n, d), F32),
        input_output_aliases={2: 0},
        compiler_params=_cparams("parallel"),
        name="matmul_residual",
    )(a, w, x, mods)


def _fnet_a_kernel(x_ref, m_ref, g_ref, cs_ref, ac_ref, as_ref, *, gd):
    h = _modulate(x_ref[...], g_ref[...], m_ref[0], m_ref[1]).astype(BF16)
    cs = cs_ref[...]
    for g in range(FNET_GROUPS):
        a = _dot(h[:, g * gd:(g + 1) * gd], cs)
        ac_ref[:, g * gd:(g + 1) * gd] = a[:, :gd].astype(BF16)
        as_ref[:, g * gd:(g + 1) * gd] = a[:, gd:].astype(BF16)


def _fnet_a(x, mods, layer, seg, g1, cs):
    n, d = x.shape
    rows = seg.batch * seg.seq
    out = pl.BlockSpec((ROW_TILE, d), lambda i: (i, 0))
    return pl.pallas_call(
        functools.partial(_fnet_a_kernel, gd=d // FNET_GROUPS),
        grid=(seg.tiles,),
        in_specs=[
            pl.BlockSpec((ROW_TILE, d), lambda i: (seg.tile0 + i, 0)),
            _mod_spec(layer, seg, d),
            pl.BlockSpec((1, d), lambda i: (0, 0)),
            pl.BlockSpec(cs.shape, lambda i: (0, 0)),
        ],
        out_specs=[out, out],
        out_shape=[jax.ShapeDtypeStruct((rows, d), BF16)] * 2,
        compiler_params=_cparams("parallel"),
        name="fnet_channel_dft",
    )(x, mods, g1, cs)


def _fnet_b_kernel(ct_ref, st_ref, ac_ref, as_ref, w_ref, x_ref, m_ref, o_ref, *, norm):
    f = (_dot(ct_ref[...], ac_ref[...]) - _dot(st_ref[...], as_ref[...])) * norm
    o_ref[...] = x_ref[...] + m_ref[2] * _dot(f.astype(BF16), w_ref[...])


def _fnet_b(x, mods, layer, seg, ac, as_, ct, st, w, norm):
    n, d = x.shape
    t = seg.seq
    tps = seg.tiles_per_seq
    x_spec = pl.BlockSpec((ROW_TILE, d), lambda b, i: (seg.tile0 + b * tps + i, 0))
    tab = pl.BlockSpec((ROW_TILE, t), lambda b, i: (i, 0))
    seq = pl.BlockSpec((t, d), lambda b, i: (b, 0))
    return pl.pallas_call(
        functools.partial(_fnet_b_kernel, norm=norm),
        grid=(seg.batch, tps),
        in_specs=[tab, tab, seq, seq, pl.BlockSpec(w.shape, lambda b, i: (0, 0)), x_spec,
                  _mod_spec(layer, seg, d, tile_of=lambda b, i: b * tps + i)],
        out_specs=x_spec,
        out_shape=jax.ShapeDtypeStruct((n, d), F32),
        input_output_aliases={5: 0},
        compiler_params=_cparams("parallel", "arbitrary"),
        name="fnet_position_dft",
    )(ct, st, ac, as_, w, x, mods)


def _pack_halves(a):
    w = a.shape[1] // 2
    bits = lambda v: lax.bitcast_convert_type(v.astype(BF16).astype(F32), jnp.uint32)
    return (bits(a[:, :w]) >> 16) | (bits(a[:, w:]) & jnp.uint32(0xFFFF0000))


def _unpack_halves(p):
    lo = lax.bitcast_convert_type(p << 16, F32)
    hi = lax.bitcast_convert_type(p & jnp.uint32(0xFFFF0000), F32)
    return lo, hi


def _router_kernel(x_ref, m_ref, g_ref, rwhi_ref, rwlo_ref, rb_ref, h_ref, idx_ref, rank_ref, wcol_ref, cnt_ref,
                   run_ref, tri_ref):
    step = pl.program_id(0)

    @pl.when(step == 0)
    def _():
        run_ref[...] = jnp.zeros_like(run_ref)
        tt = tri_ref.shape[0]
        earlier = lax.broadcasted_iota(jnp.int32, (tt, tt), 0) < lax.broadcasted_iota(jnp.int32, (tt, tt), 1)
        tri_ref[...] = jnp.where(earlier, 1.0, 0.0).astype(BF16)

    h = _modulate(x_ref[...], g_ref[...], m_ref[3], m_ref[4])
    h_ref[...] = _pack_halves(h)
    h_hi = h.astype(BF16)
    h_lo = (h - h_hi.astype(F32)).astype(BF16)
    logits = _dot_nt(rwhi_ref[...], h_hi) + (_dot_nt(rwhi_ref[...], h_lo) + _dot_nt(rwlo_ref[...], h_hi))
    sc = _sigmoid(logits)
    gr = sc + rb_ref[...]
    gp = EXPERTS_PER_GROUP
    row = lambda a, e: a[e:e + 1, :]
    best_g = None
    for g in range(N_EXPERT_GROUPS):
        vals = [row(gr, g * gp + i) for i in range(gp)]
        gs = None
        for i in range(gp):
            for j in range(i + 1, gp):
                pair = vals[i] + vals[j]
                gs = pair if gs is None else jnp.maximum(gs, pair)
        if best_g is None:
            best_g, best_v = jnp.zeros(gs.shape, jnp.int32), gs
        else:
            better = gs > best_v
            best_g = jnp.where(better, g, best_g)
            best_v = jnp.where(better, gs, best_v)
    sel, raw = [], []
    for i in range(gp):
        s_i, r_i = row(gr, i), row(sc, i)
        for g in range(1, N_EXPERT_GROUPS):
            s_i = jnp.where(best_g == g, row(gr, g * gp + i), s_i)
            r_i = jnp.where(best_g == g, row(sc, g * gp + i), r_i)
        sel.append(s_i)
        raw.append(r_i)

    def argmax_first(vals, raws):
        bi, bv, br = jnp.zeros(vals[0].shape, jnp.int32), vals[0], raws[0]
        for i in range(1, len(vals)):
            better = vals[i] > bv
            bi = jnp.where(better, i, bi)
            bv = jnp.where(better, vals[i], bv)
            br = jnp.where(better, raws[i], br)
        return bi, br

    i1, w1 = argmax_first(sel, raw)
    masked = [jnp.where(i1 == i, -jnp.inf, sel[i]) for i in range(gp)]
    i2, w2 = argmax_first(masked, raw)
    tot = w1 + w2
    e1 = best_g * gp + i1
    e2 = best_g * gp + i2
    idx_ref[0:1, :] = e1
    idx_ref[1:2, :] = e2
    t = e1.shape[1]
    sub = lax.broadcasted_iota(jnp.int32, (8, t), 0)
    w8 = jnp.where(sub == 0, w1 / tot, jnp.where(sub == 1, w2 / tot, 0.0))
    wcol_ref[...] = w8.T
    eio = lax.broadcasted_iota(jnp.int32, (N_EXPERTS, t), 0)
    oh1, oh2 = eio == e1, eio == e2
    oh = jnp.where(oh1, 1.0, jnp.where(oh2, 1.0, 0.0))
    local = _dot(oh.astype(BF16), tri_ref[...])
    rank = local + run_ref[:, 0:1]
    rank_ref[0:1, :] = jnp.sum(jnp.where(oh1, rank, 0.0), axis=0, keepdims=True).astype(jnp.int32)
    rank_ref[1:2, :] = jnp.sum(jnp.where(oh2, rank, 0.0), axis=0, keepdims=True).astype(jnp.int32)
    run_ref[...] = run_ref[...] + jnp.sum(oh, axis=1, keepdims=True)
    cnt_ref[...] = run_ref[...]


def _wide_mod_row(segs, tm):
    ctx, lat = segs
    ctx_tiles = ctx.batch * ctx.seq // tm
    assert ctx_tiles * tm == ctx.batch * ctx.seq and lat.seq % tm == 0
    return lambda i: jnp.where(i < ctx_tiles, ctx.mod0, lat.mod0 + (i - ctx_tiles) // (lat.seq // tm))


def _router(x, mods, layer, segs, g2, rw_hi, rw_lo, rb):
    n, d = x.shape
    tm = WIDE_TILE
    mod_row = _wide_mod_row(segs, tm)
    return pl.pallas_call(
        _router_kernel,
        grid=(n // tm,),
        in_specs=[
            pl.BlockSpec((tm, d), lambda i: (i, 0)),
            pl.BlockSpec((None, None, 6, 1, d), lambda i: (layer, mod_row(i), 0, 0, 0)),
            pl.BlockSpec((1, d), lambda i: (0, 0)),
            pl.BlockSpec(rw_hi.shape, lambda i: (0, 0)),
            pl.BlockSpec(rw_lo.shape, lambda i: (0, 0)),
            pl.BlockSpec(rb.shape, lambda i: (0, 0)),
        ],
        out_specs=[
            pl.BlockSpec((tm, d // 2), lambda i: (i, 0)),
            pl.BlockSpec((2, tm), lambda i: (0, i)),
            pl.BlockSpec((2, tm), lambda i: (0, i)),
            pl.BlockSpec((tm, 8), lambda i: (i, 0)),
            pl.BlockSpec((N_EXPERTS, LANE), lambda i: (0, 0)),
        ],
        out_shape=[
            jax.ShapeDtypeStruct((n, d // 2), jnp.uint32),
            jax.ShapeDtypeStruct((2, n), jnp.int32),
            jax.ShapeDtypeStruct((2, n), jnp.int32),
            jax.ShapeDtypeStruct((n, 8), F32),
            jax.ShapeDtypeStruct((N_EXPERTS, LANE), F32),
        ],
        scratch_shapes=[pltpu.VMEM((N_EXPERTS, LANE), F32), pltpu.VMEM((tm, tm), BF16)],
        compiler_params=_cparams("arbitrary"),
        name="moe_router",
    )(x, mods, g2, rw_hi, rw_lo, rb)


def _expert_kernel(be_ref, br_ref, xs_ref, wg_ref, wu_ref, wd_ref, y_ref, wg_b, wu_b, wd_b):
    i = pl.program_id(0)
    prev = be_ref[jnp.maximum(i - 1, 0)]
    valid = br_ref[i]
    half_rows = y_ref.shape[0] // 2

    @pl.when(jnp.logical_or(i == 0, be_ref[i] != prev))
    def _():
        wg_b[...] = wg_ref[...].astype(BF16)
        wu_b[...] = wu_ref[...].astype(BF16)
        wd_b[...] = wd_ref[...].astype(BF16)

    def ffn(r0, nrows):
        lo, hi = _unpack_halves(xs_ref[r0:r0 + nrows, :])
        lo, hi = lo.astype(BF16), hi.astype(BF16)
        half = lo.shape[1]
        gate = _dot(lo, wg_b[:half, :]) + _dot(hi, wg_b[half:, :])
        up = _dot(lo, wu_b[:half, :]) + _dot(hi, wu_b[half:, :])
        hid = (gate * _sigmoid(gate)) * up
        y_ref[r0:r0 + nrows, :] = _pack_halves(_dot(hid.astype(BF16), wd_b[...]))

    @pl.when(valid > half_rows)
    def _():
        ffn(0, 2 * half_rows)

    @pl.when(jnp.logical_and(valid > 0, valid <= half_rows))
    def _():
        ffn(0, half_rows)
        y_ref[half_rows:, :] = jnp.zeros((half_rows, y_ref.shape[1]), y_ref.dtype)

    @pl.when(valid <= 0)
    def _():
        y_ref[...] = jnp.zeros_like(y_ref)


def _experts(xs, block_e, block_rows, w_gate, w_up, w_down, layer):
    rows, half = xs.shape
    d = 2 * half
    de = w_gate.shape[-1]
    n_blocks = rows // MOE_ROWS
    grid_spec = pltpu.PrefetchScalarGridSpec(
        num_scalar_prefetch=2,
        grid=(n_blocks,),
        in_specs=[
            pl.BlockSpec((MOE_ROWS, half), lambda i, be, nb: (i, 0)),
            pl.BlockSpec((None, None, d, de), lambda i, be, nb: (layer, be[i], 0, 0)),
            pl.BlockSpec((None, None, d, de), lambda i, be, nb: (layer, be[i], 0, 0)),
            pl.BlockSpec((None, None, de, d), lambda i, be, nb: (layer, be[i], 0, 0)),
        ],
        out_specs=pl.BlockSpec((MOE_ROWS, half), lambda i, be, nb: (i, 0)),
        scratch_shapes=[pltpu.VMEM((d, de), BF16), pltpu.VMEM((d, de), BF16), pltpu.VMEM((de, d), BF16)],
    )
    return pl.pallas_call(
        _expert_kernel,
        grid_spec=grid_spec,
        out_shape=jax.ShapeDtypeStruct((rows, half), jnp.uint32),
        compiler_params=_cparams("arbitrary"),
        name="moe_experts",
    )(block_e, block_rows, xs, w_gate, w_up, w_down)


def _combine_kernel(x_ref, m_ref, y_ref, w_ref, *rest, final):
    if final:
        fg_ref, o_ref = rest
    else:
        (o_ref,) = rest
    w = w_ref[...]
    lo0, hi0 = _unpack_halves(y_ref[0])
    lo1, hi1 = _unpack_halves(y_ref[1])
    w0, w1 = w[:, 0:1], w[:, 1:2]
    y = jnp.concatenate([w0 * lo0 + w1 * lo1, w0 * hi0 + w1 * hi1], axis=-1)
    x = x_ref[...] + m_ref[5] * y
    if final:
        x = _rms(x) * fg_ref[...]
    o_ref[...] = x


def _combine(x, mods, layer, segs, seg, yg, wcol, final_g):
    n, d = x.shape
    final = final_g is not None
    tm = WIDE_TILE
    mod_row = _wide_mod_row(segs, tm)
    t0 = seg.row0 // tm if final else 0
    steps = (seg.batch * seg.seq if final else n) // tm
    x_spec = pl.BlockSpec((tm, d), lambda i: (t0 + i, 0))
    in_specs = [x_spec,
                pl.BlockSpec((None, None, 6, 1, d), lambda i: (layer, mod_row(t0 + i), 0, 0, 0)),
                pl.BlockSpec((2, tm, d // 2), lambda i: (0, t0 + i, 0)),
                pl.BlockSpec((tm, 8), lambda i: (t0 + i, 0))]
    args = [x, mods, yg, wcol]
    if final:
        in_specs.append(pl.BlockSpec((1, d), lambda i: (0, 0)))
        args.append(final_g)
        out_spec = pl.BlockSpec((tm, d), lambda i: (i, 0))
        out_shape = jax.ShapeDtypeStruct((steps * tm, d), F32)
        aliases = {}
    else:
        out_spec, out_shape, aliases = x_spec, jax.ShapeDtypeStruct((n, d), F32), {0: 0}
    return pl.pallas_call(
        functools.partial(_combine_kernel, final=final),
        grid=(steps,),
        in_specs=in_specs,
        out_specs=out_spec,
        out_shape=out_shape,
        input_output_aliases=aliases,
        compiler_params=_cparams("parallel"),
        name="moe_combine_final" if final else "moe_combine",
    )(*args)


def _sc_mesh():
    return plsc.VectorSubcoreMesh(core_axis_name="c", subcore_axis_name="s")


def _sc_worker_split(n):
    workers = SC_CORES * SC_SUBCORES
    per = n // workers
    assert per * workers == n and per % SC_CHUNK == 0
    return workers, per, per // SC_CHUNK


def _sc_dispatch(h, pos, rows):
    n, w = h.shape
    workers, per, chunks = _sc_worker_split(n)

    @functools.partial(
        pl.kernel, out_type=jax.ShapeDtypeStruct((rows, w), h.dtype), mesh=_sc_mesh(),
        scratch_types=[pltpu.VMEM((2, chunks, SC_CHUNK), jnp.int32), pltpu.VMEM((SC_CHUNK, w), h.dtype)],
        name="moe_dispatch_scatter")
    def scatter_rows(h_hbm, pos_hbm, xs_hbm, idx_v, rows_v):
        wid = lax.axis_index("s") * SC_CORES + lax.axis_index("c")
        pltpu.sync_copy(pos_hbm.at[0, wid], idx_v.at[0])
        pltpu.sync_copy(pos_hbm.at[1, wid], idx_v.at[1])

        @pl.loop(0, chunks)
        def _(c):
            pltpu.sync_copy(h_hbm.at[pl.ds(wid * per + c * SC_CHUNK, SC_CHUNK)], rows_v)
            pltpu.sync_copy(rows_v, xs_hbm.at[idx_v.at[0, c]])
            pltpu.sync_copy(rows_v, xs_hbm.at[idx_v.at[1, c]])

    return scatter_rows(h, pos.reshape(2, workers, chunks, SC_CHUNK))


def _sc_gather2(ys, pos):
    _, w = ys.shape
    n = pos.shape[1]
    workers, per, chunks = _sc_worker_split(n)

    @functools.partial(
        pl.kernel, out_type=jax.ShapeDtypeStruct((2, n, w), ys.dtype), mesh=_sc_mesh(),
        scratch_types=[pltpu.VMEM((2, chunks, SC_CHUNK), jnp.int32), pltpu.VMEM((SC_CHUNK, w), ys.dtype),
                       pltpu.SemaphoreType.DMA],
        name="moe_combine_gather")
    def gather_rows(ys_hbm, pos_hbm, out_hbm, idx_v, rows_v, sem):
        wid = lax.axis_index("s") * SC_CORES + lax.axis_index("c")
        pltpu.sync_copy(pos_hbm.at[0, wid], idx_v.at[0])
        pltpu.sync_copy(pos_hbm.at[1, wid], idx_v.at[1])

        @pl.loop(0, chunks)
        def _(c):
            for k in range(2):
                pltpu.async_copy(ys_hbm.at[idx_v.at[k, c]], rows_v, sem).wait()
                pltpu.sync_copy(rows_v, out_hbm.at[k, pl.ds(wid * per + c * SC_CHUNK, SC_CHUNK)])

    return gather_rows(ys, pos.reshape(2, workers, chunks, SC_CHUNK))


def _dispatch_plan(idx, rank, counts):
    n = idx.shape[1]
    padded = (counts + MOE_ROWS - 1) // MOE_ROWS * MOE_ROWS
    pad_end = jnp.cumsum(padded)
    pad_start = pad_end - padded
    experts = jnp.arange(N_EXPERTS, dtype=jnp.int32)
    start_of = jnp.sum(jnp.where(idx[..., None] == experts, pad_start, 0), axis=-1)
    pos = start_of + rank
    n_blocks = 2 * n // MOE_ROWS + N_EXPERTS
    starts = jnp.arange(n_blocks, dtype=jnp.int32) * MOE_ROWS
    block_e = jnp.minimum(jnp.sum(starts[:, None] >= pad_end[None, :], axis=1), N_EXPERTS - 1).astype(jnp.int32)
    seg_end = jnp.sum(jnp.where(block_e[:, None] == experts, pad_start + counts, 0), axis=-1)
    block_rows = jnp.clip(seg_end - starts, 0, MOE_ROWS).astype(jnp.int32)
    return pos, block_e, block_rows, n_blocks * MOE_ROWS


def _moe(x, mods, layer, segs, g2, rw_hi, rw_lo, rb, w_gate, w_up, w_down, final_g):
    h2p, idx, rank, wcol, cnt = _router(x, mods, layer, segs, g2, rw_hi, rw_lo, rb)
    pos, block_e, block_rows, rows = _dispatch_plan(idx, rank, cnt[:, 0].astype(jnp.int32))
    xs = _sc_dispatch(h2p, pos, rows)
    ys = _experts(xs, block_e, block_rows, w_gate, w_up, w_down, layer)
    yg = _sc_gather2(ys, pos)
    if final_g is None:
        return _combine(x, mods, layer, segs, None, yg, wcol, None)
    return tuple(_combine(x, mods, layer, segs, seg, yg, wcol, final_g) for seg in segs)


def _mla_rope_tables(t):
    axis_dim = MLA_ROPE // 2
    row = np.repeat(np.arange(t // GRID_W), GRID_W).astype(np.float64)
    col = np.tile(np.arange(GRID_W), t // GRID_W).astype(np.float64)
    inv = ROPE_BASE ** (-np.arange(0, axis_dim, 2, dtype=np.float64) / axis_dim)
    ar, ac = row[:, None] * inv[None, :], col[:, None] * inv[None, :]
    ones = np.ones((t, LANE - MLA_ROPE))
    cos = np.concatenate([np.cos(ar), np.cos(ar), np.cos(ac), np.cos(ac), ones], axis=-1)
    sin = np.concatenate([-np.sin(ar), np.sin(ar), -np.sin(ac), np.sin(ac), 0.0 * ones], axis=-1)
    return jnp.asarray(cos, F32), jnp.asarray(sin, F32)


def _ret_rot_tables(t, dk):
    inv = ROPE_BASE ** (-np.linspace(0.0, 1.0, dk // 2))
    ang = np.arange(t, dtype=np.float64)[:, None] * inv[None, :]
    return jnp.asarray(np.cos(ang), F32), jnp.asarray(np.sin(ang), F32)


def _dft_tables(n):
    k = np.arange(n, dtype=np.int64)
    ang = (np.outer(k, k) % n).astype(np.float64) * (2.0 * math.pi / n)
    return jnp.asarray(np.cos(ang), BF16), jnp.asarray(np.sin(ang), BF16)


def _mla_weights(w_in, q_g, kv_g, w_uq, w_ukv, w_o):
    d = w_in.shape[0]
    hd = MLA_NOPE + MLA_ROPE
    w_in_p = jnp.concatenate([w_in, jnp.zeros((d, LANE - MLA_ROPE), w_in.dtype)], axis=1)
    uq = w_uq.reshape(MLA_Q_LORA, MLA_HEADS, hd)
    uq = jnp.concatenate([uq, jnp.zeros((MLA_Q_LORA, MLA_HEADS, 2 * LANE - hd), uq.dtype)], axis=-1)
    ukv = w_ukv.reshape(MLA_KV_LORA, MLA_HEADS, MLA_NOPE + MLA_V)
    return {
        "w_in": w_in_p.astype(BF16),
        "q_g": q_g.reshape(1, -1),
        "kv_g": kv_g.reshape(1, -1),
        "w_q": uq.reshape(MLA_Q_LORA, MLA_HEADS * 2 * LANE).astype(BF16),
        "w_kn": ukv[..., :MLA_NOPE].reshape(MLA_KV_LORA, MLA_HEADS * MLA_NOPE).astype(BF16),
        "w_v": ukv[..., MLA_NOPE:].reshape(MLA_KV_LORA, MLA_HEADS * MLA_V).astype(BF16),
        "w_o": w_o.astype(BF16),
    }


def kernel(x_prompt, x_sample, cache_mla, state_ret, c, c_ctx, norm1_g, norm2_g, ada_w, ada_b, final_norm_g,
           mla_w_in, mla_q_norm_g, mla_kv_norm_g, mla_w_uq, mla_w_ukv, mla_w_o, ret_w_in, ret_decay_f,
           ret_decay_b, ret_w_o, fnet_w, router_w, router_b, moe_w_gate, moe_w_up, moe_w_down):
    b_ctx, t_ctx, d = x_prompt.shape
    b_lat, t_lat, _ = x_sample.shape
    depth = ada_w.shape[0]
    assert t_ctx == ROW_TILE and t_lat % ROW_TILE == 0 and b_lat + 1 <= 8
    n_ctx = b_ctx * t_ctx
    ctx = _Seg(0, b_ctx, t_ctx, 0, False)
    lat = _Seg(n_ctx, b_lat, t_lat, 1, True)
    segs = (ctx, lat)

    n_lat = b_lat * t_lat
    n_mla = mla_w_in.shape[0]
    assert n_mla >= 1
    x = None
    new_cache = jnp.zeros((b_ctx, n_mla, t_ctx, MLA_KV_LORA + MLA_ROPE), F32)
    cond8 = jnp.concatenate([c_ctx[None, :], c, jnp.zeros((8 - 1 - b_lat, d), F32)], axis=0)
    mods = _modulation_all(cond8, ada_w, ada_b).reshape(depth, 8, 6, 1, d)

    rw_t = router_w.T.astype(F32)
    rw_hi = rw_t.astype(BF16)
    rw_lo = (rw_t - rw_hi.astype(F32)).astype(BF16)
    rb = router_b.reshape(N_EXPERTS, 1).astype(F32)
    final_g = final_norm_g.reshape(1, d)
    dk = ret_w_in.shape[2] // (8 * RET_HEADS)
    dv = 2 * dk

    states = []
    counters = [0, 0, 0]
    for layer in range(depth):
        kind = layer % 3
        j = counters[kind]
        counters[kind] += 1
        g1 = norm1_g[layer].reshape(1, d)
        g2 = norm2_g[layer].reshape(1, d)
        if kind == 0:
            w = _mla_weights(mla_w_in[j], mla_q_norm_g[j], mla_kv_norm_g[j], mla_w_uq[j], mla_w_ukv[j],
                             mla_w_o[j])
            if x is None:
                xc, xc0, xl, xl0 = x_prompt.reshape(n_ctx, d), 0, x_sample.reshape(n_lat, d), 0
            else:
                xc, xc0, xl, xl0 = x, ctx.tile0, x, lat.tile0
            qc, kc, vc, new_cache = _mla_proj(xc, xc0, mods, layer, ctx, g1, w, None, (new_cache, j, n_mla))
            ql, kl, vl = _mla_proj(xl, xl0, mods, layer, lat, g1, w, _mla_rope_tables(t_lat))
            past = cache_mla.shape[2]
            cpad = jnp.pad(cache_mla[:, j].reshape(b_lat * past, -1), ((0, 0), (0, LANE - MLA_ROPE)))
            kp, vp = _cache_kv(cpad, w)
            first = x is None
            x = _attention(xc, xc0, n_ctx + n_lat, jnp.zeros((n_ctx + n_lat, d), F32) if first else "inplace",
                           mods, layer, ctx, qc, [(kc, vc, t_ctx)], w["w_o"])
            x = _attention(xl if first else x, xl0, n_ctx + n_lat, x if first else "inplace", mods, layer, lat,
                           ql, [(kl, vl, t_lat), (kp, vp, past)], w["w_o"])
        elif kind == 1:
            w_in = ret_w_in[j]
            qk = RET_HEADS * dk
            k_scale = jnp.concatenate([jnp.ones((qk,), F32), jnp.full((qk,), dk ** -0.5, F32),
                                       jnp.ones((w_in.shape[1] - 2 * qk,), F32)])
            w_in_b = (w_in * k_scale[None, :]).astype(BF16)
            w_o_b = ret_w_o[j].astype(BF16)
            rot = _ret_rot_tables(t_lat, dk)
            n_qk, n_v = 2 * qk // RET_COL, RET_HEADS * dv // RET_COL
            assert (n_qk + n_v) * RET_COL * 2 == w_in.shape[1]
            parts = []
            for seg in segs:
                qk_kind = "rotary" if seg is lat else "plain"
                parts.append((
                    _ret_proj(x, mods, layer, seg, g1, w_in_b, 0, (qk_kind,) * n_qk + ("plain",) * n_v, rot, dk),
                    _ret_proj(x, mods, layer, seg, g1, w_in_b, 1, ("silu",) * (n_qk + n_v), None, dk)))
            yc, s_ctx = _ret_scan(*parts[0], ctx, ret_decay_f[j], ret_decay_b[j], None, True, dk, dv)
            yl, _ = _ret_scan(*parts[1], lat, ret_decay_f[j], ret_decay_b[j], state_ret[:, j], False, dk, dv)
            x = _matmul_residual(x, mods, layer, ctx, yc, w_o_b)
            x = _matmul_residual(x, mods, layer, lat, yl, w_o_b)
            states.append(s_ctx)
        else:
            gd = d // FNET_GROUPS
            cc, sc = _dft_tables(gd)
            cs = jnp.concatenate([cc, sc], axis=1)
            w_b = fnet_w[j].astype(BF16)
            for seg in segs:
                ct, st = _dft_tables(seg.seq)
                ac, as_ = _fnet_a(x, mods, layer, seg, g1, cs)
                x = _fnet_b(x, mods, layer, seg, ac, as_, ct, st, w_b, (seg.seq * gd) ** -0.5)
        last = layer == depth - 1
        x = _moe(x, mods, layer, segs, g2, rw_hi, rw_lo, rb, moe_w_gate, moe_w_up, moe_w_down,
                 final_g if last else None)

    y_prompt, y_sample = x
    new_state = jnp.stack(states, axis=1)
    return (y_prompt.reshape(b_ctx, t_ctx, d), y_sample.reshape(b_lat, t_lat, d), new_cache, new_state)
```

```python
import functools
import math

import jax
import jax.numpy as jnp
import numpy as np
from jax import lax
from jax.experimental import pallas as pl
from jax.experimental.pallas import tpu as pltpu
from jax.experimental.pallas import tpu_sc as plsc

F32 = jnp.float32
BF16 = jnp.bfloat16

GRID_W = 64
MLA_HEADS = 8
MLA_NOPE = 128
MLA_ROPE = 64
MLA_V = 128
MLA_Q_LORA = 384
MLA_KV_LORA = 256
ROPE_BASE = 10000.0
RET_HEADS = 4
RET_CHUNK = 256
FNET_GROUPS = 4
N_EXPERTS = 16
N_EXPERT_GROUPS = 4
EXPERTS_PER_GROUP = 4
D_EXPERT = 512
NORM_EPS = 1e-6

LANE = 128
PROJ_ROWS = 512
ATTN_ROWS = 512
WIDE_TILE = 1024
MOE_ROWS = 512
VMEM_LIMIT = 56 * 1024 * 1024
SC_CORES = 2
SC_SUBCORES = 16
SC_CHUNK = 128


def _cparams(*sem):
    return pltpu.CompilerParams(dimension_semantics=sem, vmem_limit_bytes=VMEM_LIMIT)


def _sigmoid(x):
    return 1.0 / (1.0 + jnp.exp(-x))


def _rms(x):
    return x * lax.rsqrt(jnp.mean(x * x, axis=-1, keepdims=True) + NORM_EPS)


def _modulate(x, g, shift, scale):
    return (_rms(x) * g) * (1.0 + scale) + shift


def _dot(a, b):
    return jnp.dot(a, b, preferred_element_type=F32)


def _dot_nt(a, b):
    return lax.dot_general(a, b, (((1,), (1,)), ((), ())), preferred_element_type=F32)


def _mod_kernel(c_ref, w_ref, b_ref, o_ref):
    c = c_ref[...]
    s = (c * _sigmoid(c)).astype(BF16)
    o_ref[...] = _dot(s, w_ref[...].astype(BF16)) + b_ref[...]


def _modulation_all(cond8, ada_w, ada_b):
    depth, d, d6 = ada_w.shape
    tn = d6 // 4
    return pl.pallas_call(
        _mod_kernel,
        grid=(depth, d6 // tn),
        in_specs=[
            pl.BlockSpec((8, d), lambda l, n: (0, 0)),
            pl.BlockSpec((None, d, tn), lambda l, n: (l, 0, n)),
            pl.BlockSpec((None, 1, tn), lambda l, n: (l, 0, n)),
        ],
        out_specs=pl.BlockSpec((None, 8, tn), lambda l, n: (l, 0, n)),
        out_shape=jax.ShapeDtypeStruct((depth, 8, d6), F32),
        compiler_params=_cparams("parallel", "parallel"),
        name="modulation",
    )(cond8, ada_w, ada_b.reshape(depth, 1, d6))


class _Seg:
    def __init__(self, row0, batch, seq, mod0, per_batch_mod):
        self.row0, self.batch, self.seq = row0, batch, seq
        self.mod0, self.per_batch_mod = mod0, per_batch_mod
        self.rows = batch * seq

    def tile(self, want):
        tm = min(want, self.seq) if self.per_batch_mod else want
        assert self.rows % tm == 0 and self.row0 % tm == 0 and (self.seq % tm == 0 or tm % self.seq == 0)
        return tm

    def seq_tile(self, want):
        tm = min(want, self.seq)
        assert self.seq % tm == 0 and self.row0 % tm == 0
        return tm

    def mod_row(self, tile, tm):
        if self.per_batch_mod:
            return self.mod0 + tile * tm // self.seq
        return self.mod0


def _mod_spec(layer, seg, d, tm, tile_of=lambda *a: a[0]):
    return pl.BlockSpec((None, None, 6, 1, d), lambda *a: (layer, seg.mod_row(tile_of(*a), tm), 0, 0, 0))


def _rope_partner(x):
    lane = lax.broadcasted_iota(jnp.int32, x.shape, 1)
    first = (lane % 32) < 16
    return jnp.where(first, pltpu.roll(x, LANE - 16, 1), pltpu.roll(x, 16, 1))


def _store_values(v_ref, v):
    ones = jnp.ones((v.shape[0], LANE), BF16)
    for hd in range(MLA_HEADS):
        v_ref[:, hd * 2 * LANE:hd * 2 * LANE + LANE] = v[:, hd * MLA_V:(hd + 1) * MLA_V].astype(BF16)
        v_ref[:, hd * 2 * LANE + LANE:(hd + 1) * 2 * LANE] = ones


def _mla_proj_kernel(*refs, rope):
    if rope:
        (x_ref, m_ref, g_ref, win_ref, qg_ref, kvg_ref, wq_ref, wkn_ref, wv_ref, cos_ref, sin_ref,
         q_ref, k_ref, v_ref) = refs
    else:
        x_ref, m_ref, g_ref, win_ref, qg_ref, kvg_ref, wq_ref, wkn_ref, wv_ref = refs[:9]
        q_ref, k_ref, v_ref, cache_ref = refs[-4:]
    h = _modulate(x_ref[...], g_ref[...], m_ref[0], m_ref[1]).astype(BF16)
    z = _dot(h, win_ref[...])
    cq = z[:, :MLA_Q_LORA]
    ckv = z[:, MLA_Q_LORA:MLA_Q_LORA + MLA_KV_LORA]
    kpe = z[:, MLA_Q_LORA + MLA_KV_LORA:]
    cqn = (_rms(cq) * qg_ref[...]).astype(BF16)
    ckvn = _rms(ckv) * kvg_ref[...]
    ckvb = ckvn.astype(BF16)
    q = _dot(cqn, wq_ref[...])
    kn = _dot(ckvb, wkn_ref[...])
    _store_values(v_ref, _dot(ckvb, wv_ref[...]))
    if rope:
        cos, sin = cos_ref[...], sin_ref[...]
        kpe = kpe * cos + _rope_partner(kpe) * sin
    else:
        seq = cache_ref.shape[1]
        for s in range(cache_ref.shape[0]):
            cache_ref[s, :, :MLA_KV_LORA] = ckvn[s * seq:(s + 1) * seq, :]
            cache_ref[s, :, MLA_KV_LORA:] = kpe[s * seq:(s + 1) * seq, :MLA_ROPE]
    kpe_b = kpe.astype(BF16)
    for hd in range(MLA_HEADS):
        lo = hd * 2 * LANE
        q_ref[:, lo:lo + LANE] = q[:, lo:lo + LANE].astype(BF16)
        qr = q[:, lo + LANE:lo + 2 * LANE]
        if rope:
            qr = qr * cos + _rope_partner(qr) * sin
        q_ref[:, lo + LANE:lo + 2 * LANE] = qr.astype(BF16)
        k_ref[:, lo:lo + LANE] = kn[:, hd * LANE:(hd + 1) * LANE].astype(BF16)
        k_ref[:, lo + LANE:lo + 2 * LANE] = kpe_b


def _mla_proj(x, x_row0, mods, layer, seg, g1, w, rope_tabs, cache_slot=None):
    n, d = x.shape
    rope = rope_tabs is not None
    rows = seg.rows
    tm = seg.tile(PROJ_ROWS)
    x_tile0 = x_row0 // tm
    hq = MLA_HEADS * 2 * LANE
    const = lambda i: (0, 0)
    aliases = {}
    in_specs = [
        pl.BlockSpec((tm, d), lambda i: (x_tile0 + i, 0)),
        _mod_spec(layer, seg, d, tm),
        pl.BlockSpec((1, d), const),
        pl.BlockSpec(w["w_in"].shape, const),
        pl.BlockSpec((1, MLA_Q_LORA), const),
        pl.BlockSpec((1, MLA_KV_LORA), const),
        pl.BlockSpec(w["w_q"].shape, const),
        pl.BlockSpec(w["w_kn"].shape, const),
        pl.BlockSpec(w["w_v"].shape, const),
    ]
    args = [x, mods, g1, w["w_in"], w["q_g"], w["kv_g"], w["w_q"], w["w_kn"], w["w_v"]]
    out_specs = [pl.BlockSpec((tm, hq), lambda i: (i, 0))] * 3
    out_shape = [jax.ShapeDtypeStruct((rows, hq), BF16)] * 3
    if rope:
        tab = pl.BlockSpec((tm, LANE), lambda i: (i % (seg.seq // tm), 0))
        in_specs += [tab, tab]
        args += list(rope_tabs)
    else:
        cw = MLA_KV_LORA + MLA_ROPE
        prev, slot, n_slots = cache_slot
        assert tm % seg.seq == 0
        out_specs.append(pl.BlockSpec((tm // seg.seq, None, seg.seq, cw), lambda i: (i, slot, 0, 0)))
        out_shape.append(jax.ShapeDtypeStruct((seg.batch, n_slots, seg.seq, cw), F32))
        in_specs.append(pl.BlockSpec(memory_space=pl.ANY))
        args.append(prev)
        aliases = {len(args) - 1: 3}
    return pl.pallas_call(
        functools.partial(_mla_proj_kernel, rope=rope),
        grid=(rows // tm,),
        in_specs=in_specs,
        out_specs=out_specs,
        out_shape=out_shape,
        input_output_aliases=aliases,
        compiler_params=_cparams("parallel"),
        name="mla_proj_lat" if rope else "mla_proj_ctx",
    )(*args)


def _cache_kv_kernel(c_ref, wkn_ref, wv_ref, k_ref, v_ref):
    c = c_ref[...]
    ckv = c[:, :MLA_KV_LORA].astype(BF16)
    kpe_b = c[:, MLA_KV_LORA:].astype(BF16)
    kn = _dot(ckv, wkn_ref[...])
    _store_values(v_ref, _dot(ckv, wv_ref[...]))
    for hd in range(MLA_HEADS):
        lo = hd * 2 * LANE
        k_ref[:, lo:lo + LANE] = kn[:, hd * LANE:(hd + 1) * LANE].astype(BF16)
        k_ref[:, lo + LANE:lo + 2 * LANE] = kpe_b


def _cache_kv(cache_pad, w):
    rows, cw = cache_pad.shape
    hq = MLA_HEADS * 2 * LANE
    const = lambda i: (0, 0)
    tm = min(PROJ_ROWS, rows)
    assert rows % tm == 0
    return pl.pallas_call(
        _cache_kv_kernel,
        grid=(rows // tm,),
        in_specs=[
            pl.BlockSpec((tm, cw), lambda i: (i, 0)),
            pl.BlockSpec(w["w_kn"].shape, const),
            pl.BlockSpec(w["w_v"].shape, const),
        ],
        out_specs=[pl.BlockSpec((tm, hq), lambda i: (i, 0))] * 2,
        out_shape=[jax.ShapeDtypeStruct((rows, hq), BF16)] * 2,
        compiler_params=_cparams("parallel"),
        name="mla_cache_kv",
    )(cache_pad, w["w_kn"], w["w_v"])


def _attn_kernel(*refs, n_parts):
    q_ref = refs[0]
    kv_refs = refs[1:1 + 2 * n_parts]
    wo_ref, x_ref, m_ref = refs[1 + 2 * n_parts:4 + 2 * n_parts]
    o_ref, acc_ref = refs[-2:]
    for hd in range(MLA_HEADS):
        qh = q_ref[:, hd * 2 * LANE:(hd + 1) * 2 * LANE]
        scores = [_dot_nt(qh, kv_refs[2 * p][:, hd * 2 * LANE:(hd + 1) * 2 * LANE]) for p in range(n_parts)]
        mx = scores[0].max(axis=-1, keepdims=True)
        for s in scores[1:]:
            mx = jnp.maximum(mx, s.max(axis=-1, keepdims=True))
        out = None
        for p, s in enumerate(scores):
            e = jnp.exp((s - mx).astype(BF16))
            pv = _dot(e, kv_refs[2 * p + 1][:, hd * 2 * LANE:(hd + 1) * 2 * LANE])
            out = pv if out is None else out + pv
        acc_ref[:, hd * MLA_V:(hd + 1) * MLA_V] = (out[:, :MLA_V] / out[:, LANE:LANE + MLA_V]).astype(BF16)
    y = _dot(acc_ref[...], wo_ref[...])
    o_ref[...] = x_ref[...] + m_ref[2] * y


def _attention(x, x_row0, n, dest, mods, layer, seg, q, kv_parts, w_o):
    d = x.shape[1]
    hq = MLA_HEADS * 2 * LANE
    hv = MLA_HEADS * MLA_V
    tq = seg.seq_tile(ATTN_ROWS)
    tps = seg.seq // tq
    x_tile0, out_tile0 = x_row0 // tq, seg.row0 // tq
    in_specs = [pl.BlockSpec((tq, hq), lambda b, i: (b * tps + i, 0))]
    args = [q]
    for k, v, rows in kv_parts:
        mode = dict(pipeline_mode=pl.Buffered(1)) if tps > 1 else {}
        in_specs += [pl.BlockSpec((rows, hq), lambda b, i: (b, 0), **mode)] * 2
        args += [k, v]
    in_specs += [
        pl.BlockSpec(w_o.shape, lambda b, i: (0, 0)),
        pl.BlockSpec((tq, d), lambda b, i: (x_tile0 + b * tps + i, 0)),
        _mod_spec(layer, seg, d, tq, tile_of=lambda b, i: b * tps + i),
    ]
    args += [w_o, x, mods]
    if isinstance(dest, str):
        assert dest == "inplace"
        aliases = {len(args) - 2: 0}
    else:
        in_specs.append(pl.BlockSpec(memory_space=pl.ANY))
        args.append(dest)
        aliases = {len(args) - 1: 0}
    return pl.pallas_call(
        functools.partial(_attn_kernel, n_parts=len(kv_parts)),
        grid=(seg.batch, tps),
        in_specs=in_specs,
        out_specs=pl.BlockSpec((tq, d), lambda b, i: (out_tile0 + b * tps + i, 0)),
        out_shape=jax.ShapeDtypeStruct((n, d), F32),
        scratch_shapes=[pltpu.VMEM((tq, hv), BF16)],
        input_output_aliases=aliases,
        compiler_params=_cparams("parallel", "arbitrary"),
        name="mla_attention",
    )(*args)


RET_COL = 1024


RET_ROWS = 512


def _ret_proj_kernel(*refs, kinds, dk):
    rotary = "rotary" in kinds
    if rotary:
        x_ref, m_ref, g_ref, w_ref, cos_ref, sin_ref, z_ref = refs
        cos, sin = cos_ref[...], sin_ref[...]
    else:
        x_ref, m_ref, g_ref, w_ref, z_ref = refs
    h = _modulate(x_ref[...], g_ref[...], m_ref[0], m_ref[1]).astype(BF16)
    half = dk // 2
    for j, kind in enumerate(kinds):
        c0 = j * RET_COL
        acc = _dot(h, w_ref[:, c0:c0 + RET_COL])
        if kind == "rotary":
            for hd in range(RET_COL // dk):
                lo = hd * dk
                x1, x2 = acc[:, lo:lo + half], acc[:, lo + half:lo + dk]
                z_ref[:, c0 + lo:c0 + lo + half] = (x1 * cos - x2 * sin).astype(BF16)
                z_ref[:, c0 + lo + half:c0 + lo + dk] = (x1 * sin + x2 * cos).astype(BF16)
        elif kind == "silu":
            z_ref[:, c0:c0 + RET_COL] = (acc * _sigmoid(acc)).astype(BF16)
        else:
            z_ref[:, c0:c0 + RET_COL] = acc.astype(BF16)


def _ret_proj(x, mods, layer, seg, g1, w_in, group, kinds, rot_tabs, dk):
    n, d = x.shape
    rows = seg.batch * seg.seq
    tm = RET_ROWS
    ncol = len(kinds) * RET_COL
    tps = max(seg.seq // tm, 1)
    in_specs = [
        pl.BlockSpec((tm, d), lambda i: (seg.row0 // tm + i, 0)),
        pl.BlockSpec((None, None, 6, 1, d),
                     lambda i: (layer, seg.mod0 + (i * tm // seg.seq if seg.per_batch_mod else 0), 0, 0, 0)),
        pl.BlockSpec((1, d), lambda i: (0, 0)),
        pl.BlockSpec((d, ncol), lambda i: (0, group)),
    ]
    args = [x, mods, g1, w_in]
    if "rotary" in kinds:
        tab = pl.BlockSpec((tm, dk // 2), lambda i: (i % tps, 0))
        in_specs += [tab, tab]
        args += list(rot_tabs)
    return pl.pallas_call(
        functools.partial(_ret_proj_kernel, kinds=kinds, dk=dk),
        grid=(rows // tm,),
        in_specs=in_specs,
        out_specs=pl.BlockSpec((tm, ncol), lambda i: (i, 0)),
        out_shape=jax.ShapeDtypeStruct((rows, ncol), BF16),
        compiler_params=_cparams("parallel"),
        name="ret_proj_" + kinds[0],
    )(*args)


def _log_sigmoid(x):
    return jnp.minimum(x, 0.0) - jnp.log(1.0 + jnp.exp(-jnp.abs(x)))


def _ret_scan_kernel(*refs, has_s0, emit_state, n_chunks, heads):
    refs = list(refs)
    lf_ref, lb_ref, q_ref, k_ref, v_ref, gf_ref, gb_ref = refs[:7]
    pos = 7
    s0_ref = None
    if has_s0:
        s0_ref = refs[pos]
        pos += 1
    y_ref = refs[pos]
    pos += 1
    sout_ref = None
    if emit_state:
        sout_ref = refs[pos]
        pos += 1
    s_ref, yf_ref = refs[pos:]
    c = RET_CHUNK
    dk, dv = s_ref.shape
    ii = lax.broadcasted_iota(jnp.int32, (c, c), 0).astype(F32)
    jj = lax.broadcasted_iota(jnp.int32, (c, c), 1).astype(F32)
    idx = lax.broadcasted_iota(jnp.int32, (c, 1), 0).astype(F32)

    for hd, direction in [(hd, direction) for hd in range(heads) for direction in range(2)]:
        fwd = direction == 0
        kcol, vcol = slice(hd * dk, (hd + 1) * dk), slice(hd * dv, (hd + 1) * dv)
        lg = _log_sigmoid((lf_ref if fwd else lb_ref)[hd])
        rel = (ii - jj) if fwd else (jj - ii)
        keep = rel >= 0
        decay_in = jnp.where(keep, jnp.exp(jnp.where(keep, rel, 0.0) * lg), 0.0)
        decay_q = jnp.exp(((idx + 1.0) if fwd else (c - idx)) * lg)
        decay_k = jnp.exp(((c - 1.0 - idx) if fwd else idx) * lg)
        decay_c = jnp.exp(c * lg)
        g_ref = gf_ref if fwd else gb_ref

        def chunk(cc, state, fwd=fwd, decay_in=decay_in, decay_q=decay_q, decay_k=decay_k, decay_c=decay_c,
                  g_ref=g_ref, kcol=kcol, vcol=vcol):
            r0 = cc * c if isinstance(cc, int) else pl.multiple_of(cc * c, c)
            qc = q_ref[pl.ds(r0, c), kcol]
            kc = k_ref[pl.ds(r0, c), kcol]
            vc = v_ref[pl.ds(r0, c), vcol]
            sc = _dot_nt(qc, kc) * decay_in
            out = _dot(sc.astype(BF16), vc)
            kd_t = (kc.astype(F32) * decay_k).T.astype(BF16)
            new_s = _dot(kd_t, vc)
            if state is not None:
                out = out + decay_q * _dot(qc, state.astype(BF16))
                new_s = decay_c * state + new_s
            s_ref[...] = new_s
            o = _rms(out) * g_ref[pl.ds(r0, c), vcol].astype(F32)
            if fwd:
                yf_ref[pl.ds(r0, c), :] = o
            else:
                y_ref[pl.ds(r0, c), vcol] = (yf_ref[pl.ds(r0, c), :] + o).astype(BF16)

        chunk(0 if fwd else n_chunks - 1, s0_ref[direction, hd] if has_s0 else None)

        def step(ci, carry, fwd=fwd, chunk=chunk):
            chunk(ci if fwd else n_chunks - 1 - ci, s_ref[...])
            return carry

        lax.fori_loop(1, n_chunks, step, 0)
        if emit_state:
            sout_ref[direction, hd] = s_ref[...]


def _ret_scan(qkv, g, seg, logit_f, logit_b, s0, emit_state, dk, dv, heads):
    rows = seg.batch * seg.seq
    t = seg.seq
    hh = RET_HEADS
    groups = hh // heads
    v0 = 2 * hh * dk // (heads * dv)
    assert groups * heads == hh and v0 * heads * dv == 2 * hh * dk
    in_specs = [
        pl.BlockSpec((heads, 1, 1), lambda b, h: (h, 0, 0)),
        pl.BlockSpec((heads, 1, 1), lambda b, h: (h, 0, 0)),
        pl.BlockSpec((t, heads * dk), lambda b, h: (b, h)),
        pl.BlockSpec((t, heads * dk), lambda b, h: (b, groups + h)),
        pl.BlockSpec((t, heads * dv), lambda b, h: (b, v0 + h)),
        pl.BlockSpec((t, heads * dv), lambda b, h: (b, h)),
        pl.BlockSpec((t, heads * dv), lambda b, h: (b, groups + h)),
    ]
    args = [logit_f.reshape(hh, 1, 1), logit_b.reshape(hh, 1, 1), qkv, qkv, qkv, g, g]
    state_spec = pl.BlockSpec((None, 2, heads, dk, dv), lambda b, h: (b, 0, h, 0, 0))
    if s0 is not None:
        in_specs.append(state_spec)
        args.append(s0)
    out_specs = [pl.BlockSpec((t, heads * dv), lambda b, h: (b, h))]
    out_shape = [jax.ShapeDtypeStruct((rows, hh * dv), BF16)]
    if emit_state:
        out_specs.append(state_spec)
        out_shape.append(jax.ShapeDtypeStruct((seg.batch, 2, hh, dk, dv), F32))
    res = pl.pallas_call(
        functools.partial(_ret_scan_kernel, has_s0=s0 is not None, emit_state=emit_state,
                          n_chunks=t // RET_CHUNK, heads=heads),
        grid=(seg.batch, groups),
        in_specs=in_specs,
        out_specs=out_specs,
        out_shape=out_shape,
        scratch_shapes=[pltpu.VMEM((dk, dv), F32), pltpu.VMEM((t, dv), F32)],
        compiler_params=_cparams("parallel", "parallel"),
        name="ret_scan",
    )(*args)
    return res if emit_state else (res[0], None)


def _mm_res_kernel(a_ref, w_ref, x_ref, m_ref, o_ref):
    o_ref[...] = x_ref[...] + m_ref[2] * _dot(a_ref[...], w_ref[...])


def _matmul_residual(x, mods, layer, seg, a, w):
    n, d = x.shape
    tm = seg.tile(PROJ_ROWS)
    x_spec = pl.BlockSpec((tm, d), lambda i: (seg.row0 // tm + i, 0))
    return pl.pallas_call(
        _mm_res_kernel,
        grid=(seg.rows // tm,),
        in_specs=[
            pl.BlockSpec((tm, a.shape[1]), lambda i: (i, 0)),
            pl.BlockSpec(w.shape, lambda i: (0, 0)),
            x_spec,
            _mod_spec(layer, seg, d, tm),
        ],
        out_specs=x_spec,
        out_shape=jax.ShapeDtypeStruct((n, d), F32),
        input_output_aliases={2: 0},
        compiler_params=_cparams("parallel"),
        name="matmul_residual",
    )(a, w, x, mods)


def _fnet_a_kernel(x_ref, m_ref, g_ref, cs_ref, ac_ref, as_ref, *, gd):
    h = _modulate(x_ref[...], g_ref[...], m_ref[0], m_ref[1]).astype(BF16)
    cs = cs_ref[...]
    for g in range(FNET_GROUPS):
        a = _dot(h[:, g * gd:(g + 1) * gd], cs)
        ac_ref[:, g * gd:(g + 1) * gd] = a[:, :gd].astype(BF16)
        as_ref[:, g * gd:(g + 1) * gd] = a[:, gd:].astype(BF16)


def _fnet_a(x, mods, layer, seg, g1, cs):
    n, d = x.shape
    rows = seg.rows
    tm = seg.tile(PROJ_ROWS)
    out = pl.BlockSpec((tm, d), lambda i: (i, 0))
    return pl.pallas_call(
        functools.partial(_fnet_a_kernel, gd=d // FNET_GROUPS),
        grid=(rows // tm,),
        in_specs=[
            pl.BlockSpec((tm, d), lambda i: (seg.row0 // tm + i, 0)),
            _mod_spec(layer, seg, d, tm),
            pl.BlockSpec((1, d), lambda i: (0, 0)),
            pl.BlockSpec(cs.shape, lambda i: (0, 0)),
        ],
        out_specs=[out, out],
        out_shape=[jax.ShapeDtypeStruct((rows, d), BF16)] * 2,
        compiler_params=_cparams("parallel"),
        name="fnet_channel_dft",
    )(x, mods, g1, cs)


def _fnet_b_kernel(ct_ref, st_ref, ac_ref, as_ref, w_ref, x_ref, m_ref, o_ref, *, norm):
    f = (_dot(ct_ref[...], ac_ref[...]) - _dot(st_ref[...], as_ref[...])) * norm
    o_ref[...] = x_ref[...] + m_ref[2] * _dot(f.astype(BF16), w_ref[...])


def _fnet_b(x, mods, layer, seg, ac, as_, ct, st, w, norm):
    n, d = x.shape
    t = seg.seq
    tq = seg.seq_tile(ATTN_ROWS)
    tps = t // tq
    x_spec = pl.BlockSpec((tq, d), lambda b, i: (seg.row0 // tq + b * tps + i, 0))
    tab = pl.BlockSpec((tq, t), lambda b, i: (i, 0))
    seq = pl.BlockSpec((t, d), lambda b, i: (b, 0))
    return pl.pallas_call(
        functools.partial(_fnet_b_kernel, norm=norm),
        grid=(seg.batch, tps),
        in_specs=[tab, tab, seq, seq, pl.BlockSpec(w.shape, lambda b, i: (0, 0)), x_spec,
                  _mod_spec(layer, seg, d, tq, tile_of=lambda b, i: b * tps + i)],
        out_specs=x_spec,
        out_shape=jax.ShapeDtypeStruct((n, d), F32),
        input_output_aliases={5: 0},
        compiler_params=_cparams("parallel", "arbitrary"),
        name="fnet_position_dft",
    )(ct, st, ac, as_, w, x, mods)


def _pack_halves(a):
    w = a.shape[1] // 2
    bits = lambda v: lax.bitcast_convert_type(v.astype(BF16).astype(F32), jnp.uint32)
    return (bits(a[:, :w]) >> 16) | (bits(a[:, w:]) & jnp.uint32(0xFFFF0000))


def _unpack_halves(p):
    lo = lax.bitcast_convert_type(p << 16, F32)
    hi = lax.bitcast_convert_type(p & jnp.uint32(0xFFFF0000), F32)
    return lo, hi


def _router_kernel(x_ref, m_ref, g_ref, rwhi_ref, rwlo_ref, rb_ref, h_ref, idx_ref, rank_ref, wcol_ref, cnt_ref,
                   run_ref, tri_ref):
    step = pl.program_id(0)

    @pl.when(step == 0)
    def _():
        run_ref[...] = jnp.zeros_like(run_ref)
        tt = tri_ref.shape[0]
        earlier = lax.broadcasted_iota(jnp.int32, (tt, tt), 0) < lax.broadcasted_iota(jnp.int32, (tt, tt), 1)
        tri_ref[...] = jnp.where(earlier, 1.0, 0.0).astype(BF16)

    h = _modulate(x_ref[...], g_ref[...], m_ref[3], m_ref[4])
    h_ref[...] = _pack_halves(h)
    h_hi = h.astype(BF16)
    h_lo = (h - h_hi.astype(F32)).astype(BF16)
    logits = _dot_nt(rwhi_ref[...], h_hi) + (_dot_nt(rwhi_ref[...], h_lo) + _dot_nt(rwlo_ref[...], h_hi))
    sc = _sigmoid(logits)
    gr = sc + rb_ref[...]
    gp = EXPERTS_PER_GROUP
    row = lambda a, e: a[e:e + 1, :]
    best_g = None
    for g in range(N_EXPERT_GROUPS):
        vals = [row(gr, g * gp + i) for i in range(gp)]
        gs = None
        for i in range(gp):
            for j in range(i + 1, gp):
                pair = vals[i] + vals[j]
                gs = pair if gs is None else jnp.maximum(gs, pair)
        if best_g is None:
            best_g, best_v = jnp.zeros(gs.shape, jnp.int32), gs
        else:
            better = gs > best_v
            best_g = jnp.where(better, g, best_g)
            best_v = jnp.where(better, gs, best_v)
    sel, raw = [], []
    for i in range(gp):
        s_i, r_i = row(gr, i), row(sc, i)
        for g in range(1, N_EXPERT_GROUPS):
            s_i = jnp.where(best_g == g, row(gr, g * gp + i), s_i)
            r_i = jnp.where(best_g == g, row(sc, g * gp + i), r_i)
        sel.append(s_i)
        raw.append(r_i)

    def argmax_first(vals, raws):
        bi, bv, br = jnp.zeros(vals[0].shape, jnp.int32), vals[0], raws[0]
        for i in range(1, len(vals)):
            better = vals[i] > bv
            bi = jnp.where(better, i, bi)
            bv = jnp.where(better, vals[i], bv)
            br = jnp.where(better, raws[i], br)
        return bi, br

    i1, w1 = argmax_first(sel, raw)
    masked = [jnp.where(i1 == i, -jnp.inf, sel[i]) for i in range(gp)]
    i2, w2 = argmax_first(masked, raw)
    tot = w1 + w2
    e1 = best_g * gp + i1
    e2 = best_g * gp + i2
    idx_ref[0:1, :] = e1
    idx_ref[1:2, :] = e2
    t = e1.shape[1]
    sub = lax.broadcasted_iota(jnp.int32, (8, t), 0)
    w8 = jnp.where(sub == 0, w1 / tot, jnp.where(sub == 1, w2 / tot, 0.0))
    wcol_ref[...] = w8.T
    eio = lax.broadcasted_iota(jnp.int32, (N_EXPERTS, t), 0)
    oh1, oh2 = eio == e1, eio == e2
    oh = jnp.where(oh1, 1.0, jnp.where(oh2, 1.0, 0.0))
    local = _dot(oh.astype(BF16), tri_ref[...])
    rank = local + run_ref[:, 0:1]
    rank_ref[0:1, :] = jnp.sum(jnp.where(oh1, rank, 0.0), axis=0, keepdims=True).astype(jnp.int32)
    rank_ref[1:2, :] = jnp.sum(jnp.where(oh2, rank, 0.0), axis=0, keepdims=True).astype(jnp.int32)
    run_ref[...] = run_ref[...] + jnp.sum(oh, axis=1, keepdims=True)
    cnt_ref[...] = run_ref[...]


def _wide_mod_row(segs, tm):
    ctx, lat = segs
    ctx_tiles = ctx.batch * ctx.seq // tm
    assert ctx_tiles * tm == ctx.batch * ctx.seq and lat.seq % tm == 0
    return lambda i: jnp.where(i < ctx_tiles, ctx.mod0, lat.mod0 + (i - ctx_tiles) // (lat.seq // tm))


def _router(x, mods, layer, segs, g2, rw_hi, rw_lo, rb):
    n, d = x.shape
    tm = WIDE_TILE
    mod_row = _wide_mod_row(segs, tm)
    return pl.pallas_call(
        _router_kernel,
        grid=(n // tm,),
        in_specs=[
            pl.BlockSpec((tm, d), lambda i: (i, 0)),
            pl.BlockSpec((None, None, 6, 1, d), lambda i: (layer, mod_row(i), 0, 0, 0)),
            pl.BlockSpec((1, d), lambda i: (0, 0)),
            pl.BlockSpec(rw_hi.shape, lambda i: (0, 0)),
            pl.BlockSpec(rw_lo.shape, lambda i: (0, 0)),
            pl.BlockSpec(rb.shape, lambda i: (0, 0)),
        ],
        out_specs=[
            pl.BlockSpec((tm, d // 2), lambda i: (i, 0)),
            pl.BlockSpec((2, tm), lambda i: (0, i)),
            pl.BlockSpec((2, tm), lambda i: (0, i)),
            pl.BlockSpec((tm, 8), lambda i: (i, 0)),
            pl.BlockSpec((N_EXPERTS, LANE), lambda i: (0, 0)),
        ],
        out_shape=[
            jax.ShapeDtypeStruct((n, d // 2), jnp.uint32),
            jax.ShapeDtypeStruct((2, n), jnp.int32),
            jax.ShapeDtypeStruct((2, n), jnp.int32),
            jax.ShapeDtypeStruct((n, 8), F32),
            jax.ShapeDtypeStruct((N_EXPERTS, LANE), F32),
        ],
        scratch_shapes=[pltpu.VMEM((N_EXPERTS, LANE), F32), pltpu.VMEM((tm, tm), BF16)],
        compiler_params=_cparams("arbitrary"),
        name="moe_router",
    )(x, mods, g2, rw_hi, rw_lo, rb)


def _expert_kernel(be_ref, br_ref, xs_ref, wg_ref, wu_ref, wd_ref, y_ref, wg_b, wu_b, wd_b):
    i = pl.program_id(0)
    prev = be_ref[jnp.maximum(i - 1, 0)]
    valid = br_ref[i]
    half_rows = y_ref.shape[0] // 2

    @pl.when(jnp.logical_or(i == 0, be_ref[i] != prev))
    def _():
        wg_b[...] = wg_ref[...].astype(BF16)
        wu_b[...] = wu_ref[...].astype(BF16)
        wd_b[...] = wd_ref[...].astype(BF16)

    def ffn(r0, nrows):
        lo, hi = _unpack_halves(xs_ref[r0:r0 + nrows, :])
        lo, hi = lo.astype(BF16), hi.astype(BF16)
        half = lo.shape[1]
        gate = _dot(lo, wg_b[:half, :]) + _dot(hi, wg_b[half:, :])
        up = _dot(lo, wu_b[:half, :]) + _dot(hi, wu_b[half:, :])
        hid = (gate * _sigmoid(gate)) * up
        y_ref[r0:r0 + nrows, :] = _pack_halves(_dot(hid.astype(BF16), wd_b[...]))

    @pl.when(valid > half_rows)
    def _():
        ffn(0, 2 * half_rows)

    @pl.when(jnp.logical_and(valid > 0, valid <= half_rows))
    def _():
        ffn(0, half_rows)
        y_ref[half_rows:, :] = jnp.zeros((half_rows, y_ref.shape[1]), y_ref.dtype)

    @pl.when(valid <= 0)
    def _():
        y_ref[...] = jnp.zeros_like(y_ref)


def _experts(xs, block_e, block_rows, w_gate, w_up, w_down, layer):
    rows, half = xs.shape
    d = 2 * half
    de = w_gate.shape[-1]
    n_blocks = rows // MOE_ROWS
    grid_spec = pltpu.PrefetchScalarGridSpec(
        num_scalar_prefetch=2,
        grid=(n_blocks,),
        in_specs=[
            pl.BlockSpec((MOE_ROWS, half), lambda i, be, nb: (i, 0)),
            pl.BlockSpec((None, None, d, de), lambda i, be, nb: (layer, be[i], 0, 0)),
            pl.BlockSpec((None, None, d, de), lambda i, be, nb: (layer, be[i], 0, 0)),
            pl.BlockSpec((None, None, de, d), lambda i, be, nb: (layer, be[i], 0, 0)),
        ],
        out_specs=pl.BlockSpec((MOE_ROWS, half), lambda i, be, nb: (i, 0)),
        scratch_shapes=[pltpu.VMEM((d, de), BF16), pltpu.VMEM((d, de), BF16), pltpu.VMEM((de, d), BF16)],
    )
    return pl.pallas_call(
        _expert_kernel,
        grid_spec=grid_spec,
        out_shape=jax.ShapeDtypeStruct((rows, half), jnp.uint32),
        compiler_params=_cparams("arbitrary"),
        name="moe_experts",
    )(block_e, block_rows, xs, w_gate, w_up, w_down)


def _combine_kernel(x_ref, m_ref, y_ref, w_ref, *rest, final):
    if final:
        fg_ref, o_ref = rest
    else:
        (o_ref,) = rest
    w = w_ref[...]
    lo0, hi0 = _unpack_halves(y_ref[0])
    lo1, hi1 = _unpack_halves(y_ref[1])
    w0, w1 = w[:, 0:1], w[:, 1:2]
    y = jnp.concatenate([w0 * lo0 + w1 * lo1, w0 * hi0 + w1 * hi1], axis=-1)
    x = x_ref[...] + m_ref[5] * y
    if final:
        x = _rms(x) * fg_ref[...]
    o_ref[...] = x


def _combine(x, mods, layer, segs, seg, yg, wcol, final_g):
    n, d = x.shape
    final = final_g is not None
    tm = WIDE_TILE
    mod_row = _wide_mod_row(segs, tm)
    t0 = seg.row0 // tm if final else 0
    steps = (seg.batch * seg.seq if final else n) // tm
    x_spec = pl.BlockSpec((tm, d), lambda i: (t0 + i, 0))
    in_specs = [x_spec,
                pl.BlockSpec((None, None, 6, 1, d), lambda i: (layer, mod_row(t0 + i), 0, 0, 0)),
                pl.BlockSpec((2, tm, d // 2), lambda i: (0, t0 + i, 0)),
                pl.BlockSpec((tm, 8), lambda i: (t0 + i, 0))]
    args = [x, mods, yg, wcol]
    if final:
        in_specs.append(pl.BlockSpec((1, d), lambda i: (0, 0)))
        args.append(final_g)
        out_spec = pl.BlockSpec((tm, d), lambda i: (i, 0))
        out_shape = jax.ShapeDtypeStruct((steps * tm, d), F32)
        aliases = {}
    else:
        out_spec, out_shape, aliases = x_spec, jax.ShapeDtypeStruct((n, d), F32), {0: 0}
    return pl.pallas_call(
        functools.partial(_combine_kernel, final=final),
        grid=(steps,),
        in_specs=in_specs,
        out_specs=out_spec,
        out_shape=out_shape,
        input_output_aliases=aliases,
        compiler_params=_cparams("parallel"),
        name="moe_combine_final" if final else "moe_combine",
    )(*args)


def _sc_mesh():
    return plsc.VectorSubcoreMesh(core_axis_name="c", subcore_axis_name="s")


def _sc_worker_split(n):
    workers = SC_CORES * SC_SUBCORES
    per = n // workers
    assert per * workers == n and per % SC_CHUNK == 0
    return workers, per, per // SC_CHUNK


def _sc_dispatch(h, pos, rows):
    n, w = h.shape
    workers, per, chunks = _sc_worker_split(n)

    @functools.partial(
        pl.kernel, out_type=jax.ShapeDtypeStruct((rows, w), h.dtype), mesh=_sc_mesh(),
        scratch_types=[pltpu.VMEM((2, chunks, SC_CHUNK), jnp.int32), pltpu.VMEM((SC_CHUNK, w), h.dtype)],
        name="moe_dispatch_scatter")
    def scatter_rows(h_hbm, pos_hbm, xs_hbm, idx_v, rows_v):
        wid = lax.axis_index("s") * SC_CORES + lax.axis_index("c")
        pltpu.sync_copy(pos_hbm.at[0, wid], idx_v.at[0])
        pltpu.sync_copy(pos_hbm.at[1, wid], idx_v.at[1])

        @pl.loop(0, chunks)
        def _(c):
            pltpu.sync_copy(h_hbm.at[pl.ds(wid * per + c * SC_CHUNK, SC_CHUNK)], rows_v)
            pltpu.sync_copy(rows_v, xs_hbm.at[idx_v.at[0, c]])
            pltpu.sync_copy(rows_v, xs_hbm.at[idx_v.at[1, c]])

    return scatter_rows(h, pos.reshape(2, workers, chunks, SC_CHUNK))


def _sc_gather2(ys, pos):
    _, w = ys.shape
    n = pos.shape[1]
    workers, per, chunks = _sc_worker_split(n)

    @functools.partial(
        pl.kernel, out_type=jax.ShapeDtypeStruct((2, n, w), ys.dtype), mesh=_sc_mesh(),
        scratch_types=[pltpu.VMEM((2, chunks, SC_CHUNK), jnp.int32), pltpu.VMEM((SC_CHUNK, w), ys.dtype),
                       pltpu.SemaphoreType.DMA],
        name="moe_combine_gather")
    def gather_rows(ys_hbm, pos_hbm, out_hbm, idx_v, rows_v, sem):
        wid = lax.axis_index("s") * SC_CORES + lax.axis_index("c")
        pltpu.sync_copy(pos_hbm.at[0, wid], idx_v.at[0])
        pltpu.sync_copy(pos_hbm.at[1, wid], idx_v.at[1])

        @pl.loop(0, chunks)
        def _(c):
            for k in range(2):
                pltpu.async_copy(ys_hbm.at[idx_v.at[k, c]], rows_v, sem).wait()
                pltpu.sync_copy(rows_v, out_hbm.at[k, pl.ds(wid * per + c * SC_CHUNK, SC_CHUNK)])

    return gather_rows(ys, pos.reshape(2, workers, chunks, SC_CHUNK))


def _dispatch_plan(idx, rank, counts):
    n = idx.shape[1]
    padded = (counts + MOE_ROWS - 1) // MOE_ROWS * MOE_ROWS
    pad_end = jnp.cumsum(padded)
    pad_start = pad_end - padded
    experts = jnp.arange(N_EXPERTS, dtype=jnp.int32)
    start_of = jnp.sum(jnp.where(idx[..., None] == experts, pad_start, 0), axis=-1)
    pos = start_of + rank
    n_blocks = 2 * n // MOE_ROWS + N_EXPERTS
    starts = jnp.arange(n_blocks, dtype=jnp.int32) * MOE_ROWS
    block_e = jnp.minimum(jnp.sum(starts[:, None] >= pad_end[None, :], axis=1), N_EXPERTS - 1).astype(jnp.int32)
    seg_end = jnp.sum(jnp.where(block_e[:, None] == experts, pad_start + counts, 0), axis=-1)
    block_rows = jnp.clip(seg_end - starts, 0, MOE_ROWS).astype(jnp.int32)
    return pos, block_e, block_rows, n_blocks * MOE_ROWS


def _moe(x, mods, layer, segs, g2, rw_hi, rw_lo, rb, w_gate, w_up, w_down, final_g):
    h2p, idx, rank, wcol, cnt = _router(x, mods, layer, segs, g2, rw_hi, rw_lo, rb)
    pos, block_e, block_rows, rows = _dispatch_plan(idx, rank, cnt[:, 0].astype(jnp.int32))
    xs = _sc_dispatch(h2p, pos, rows)
    ys = _experts(xs, block_e, block_rows, w_gate, w_up, w_down, layer)
    yg = _sc_gather2(ys, pos)
    if final_g is None:
        return _combine(x, mods, layer, segs, None, yg, wcol, None)
    return tuple(_combine(x, mods, layer, segs, seg, yg, wcol, final_g) for seg in segs)


def _mla_rope_tables(t):
    axis_dim = MLA_ROPE // 2
    row = np.repeat(np.arange(t // GRID_W), GRID_W).astype(np.float64)
    col = np.tile(np.arange(GRID_W), t // GRID_W).astype(np.float64)
    inv = ROPE_BASE ** (-np.arange(0, axis_dim, 2, dtype=np.float64) / axis_dim)
    ar, ac = row[:, None] * inv[None, :], col[:, None] * inv[None, :]
    ones = np.ones((t, LANE - MLA_ROPE))
    cos = np.concatenate([np.cos(ar), np.cos(ar), np.cos(ac), np.cos(ac), ones], axis=-1)
    sin = np.concatenate([-np.sin(ar), np.sin(ar), -np.sin(ac), np.sin(ac), 0.0 * ones], axis=-1)
    return jnp.asarray(cos, F32), jnp.asarray(sin, F32)


def _ret_rot_tables(t, dk):
    inv = ROPE_BASE ** (-np.linspace(0.0, 1.0, dk // 2))
    ang = np.arange(t, dtype=np.float64)[:, None] * inv[None, :]
    return jnp.asarray(np.cos(ang), F32), jnp.asarray(np.sin(ang), F32)


def _dft_tables(n):
    k = np.arange(n, dtype=np.int64)
    ang = (np.outer(k, k) % n).astype(np.float64) * (2.0 * math.pi / n)
    return jnp.asarray(np.cos(ang), BF16), jnp.asarray(np.sin(ang), BF16)


def _mla_weights(w_in, q_g, kv_g, w_uq, w_ukv, w_o):
    d = w_in.shape[0]
    hd = MLA_NOPE + MLA_ROPE
    w_in_p = jnp.concatenate([w_in, jnp.zeros((d, LANE - MLA_ROPE), w_in.dtype)], axis=1)
    uq = w_uq.reshape(MLA_Q_LORA, MLA_HEADS, hd)
    uq = jnp.concatenate([uq, jnp.zeros((MLA_Q_LORA, MLA_HEADS, 2 * LANE - hd), uq.dtype)], axis=-1)
    ukv = w_ukv.reshape(MLA_KV_LORA, MLA_HEADS, MLA_NOPE + MLA_V)
    return {
        "w_in": w_in_p.astype(BF16),
        "q_g": q_g.reshape(1, -1) * (MLA_NOPE + MLA_ROPE) ** -0.5,
        "kv_g": kv_g.reshape(1, -1),
        "w_q": uq.reshape(MLA_Q_LORA, MLA_HEADS * 2 * LANE).astype(BF16),
        "w_kn": ukv[..., :MLA_NOPE].reshape(MLA_KV_LORA, MLA_HEADS * MLA_NOPE).astype(BF16),
        "w_v": ukv[..., MLA_NOPE:].reshape(MLA_KV_LORA, MLA_HEADS * MLA_V).astype(BF16),
        "w_o": w_o.astype(BF16),
    }


def kernel(x_prompt, x_sample, cache_mla, state_ret, c, c_ctx, norm1_g, norm2_g, ada_w, ada_b, final_norm_g,
           mla_w_in, mla_q_norm_g, mla_kv_norm_g, mla_w_uq, mla_w_ukv, mla_w_o, ret_w_in, ret_decay_f,
           ret_decay_b, ret_w_o, fnet_w, router_w, router_b, moe_w_gate, moe_w_up, moe_w_down):
    b_ctx, t_ctx, d = x_prompt.shape
    b_lat, t_lat, _ = x_sample.shape
    depth = ada_w.shape[0]
    assert b_lat + 1 <= 8
    n_ctx = b_ctx * t_ctx
    ctx = _Seg(0, b_ctx, t_ctx, 0, False)
    lat = _Seg(n_ctx, b_lat, t_lat, 1, True)
    segs = (ctx, lat)

    n_lat = b_lat * t_lat
    n_mla = mla_w_in.shape[0]
    assert n_mla >= 1
    x = None
    new_cache = jnp.zeros((b_ctx, n_mla, t_ctx, MLA_KV_LORA + MLA_ROPE), F32)
    cond8 = jnp.concatenate([c_ctx[None, :], c, jnp.zeros((8 - 1 - b_lat, d), F32)], axis=0)
    mods = _modulation_all(cond8, ada_w, ada_b).reshape(depth, 8, 6, 1, d)

    rw_t = router_w.T.astype(F32)
    rw_hi = rw_t.astype(BF16)
    rw_lo = (rw_t - rw_hi.astype(F32)).astype(BF16)
    rb = router_b.reshape(N_EXPERTS, 1).astype(F32)
    final_g = final_norm_g.reshape(1, d)
    dk = ret_w_in.shape[2] // (8 * RET_HEADS)
    dv = 2 * dk

    states = []
    counters = [0, 0, 0]
    for layer in range(depth):
        kind = layer % 3
        j = counters[kind]
        counters[kind] += 1
        g1 = norm1_g[layer].reshape(1, d)
        g2 = norm2_g[layer].reshape(1, d)
        if kind == 0:
            w = _mla_weights(mla_w_in[j], mla_q_norm_g[j], mla_kv_norm_g[j], mla_w_uq[j], mla_w_ukv[j],
                             mla_w_o[j])
            if x is None:
                xc, xc0, xl, xl0 = x_prompt.reshape(n_ctx, d), 0, x_sample.reshape(n_lat, d), 0
            else:
                xc, xc0, xl, xl0 = x, ctx.row0, x, lat.row0
            qc, kc, vc, new_cache = _mla_proj(xc, xc0, mods, layer, ctx, g1, w, None, (new_cache, j, n_mla))
            ql, kl, vl = _mla_proj(xl, xl0, mods, layer, lat, g1, w, _mla_rope_tables(t_lat))
            past = cache_mla.shape[2]
            cpad = jnp.pad(cache_mla[:, j].reshape(b_lat * past, -1), ((0, 0), (0, LANE - MLA_ROPE)))
            kp, vp = _cache_kv(cpad, w)
            first = x is None
            x = _attention(xc, xc0, n_ctx + n_lat, jnp.zeros((n_ctx + n_lat, d), F32) if first else "inplace",
                           mods, layer, ctx, qc, [(kc, vc, t_ctx)], w["w_o"])
            x = _attention(xl if first else x, xl0, n_ctx + n_lat, x if first else "inplace", mods, layer, lat,
                           ql, [(kl, vl, t_lat), (kp, vp, past)], w["w_o"])
        elif kind == 1:
            w_in = ret_w_in[j]
            qk = RET_HEADS * dk
            k_scale = jnp.concatenate([jnp.ones((qk,), F32), jnp.full((qk,), dk ** -0.5, F32),
                                       jnp.ones((w_in.shape[1] - 2 * qk,), F32)])
            w_in_b = (w_in * k_scale[None, :]).astype(BF16)
            w_o_b = ret_w_o[j].astype(BF16)
            rot = _ret_rot_tables(t_lat, dk)
            n_qk, n_v = 2 * qk // RET_COL, RET_HEADS * dv // RET_COL
            assert (n_qk + n_v) * RET_COL * 2 == w_in.shape[1]
            parts = []
            for seg in segs:
                qk_kind = "rotary" if seg is lat else "plain"
                parts.append((
                    _ret_proj(x, mods, layer, seg, g1, w_in_b, 0, (qk_kind,) * n_qk + ("plain",) * n_v, rot, dk),
                    _ret_proj(x, mods, layer, seg, g1, w_in_b, 1, ("silu",) * (n_qk + n_v), None, dk)))
            yc, s_ctx = _ret_scan(*parts[0], ctx, ret_decay_f[j], ret_decay_b[j], None, True, dk, dv, RET_HEADS)
            yl, _ = _ret_scan(*parts[1], lat, ret_decay_f[j], ret_decay_b[j], state_ret[:, j], False, dk, dv, 1)
            x = _matmul_residual(x, mods, layer, ctx, yc, w_o_b)
            x = _matmul_residual(x, mods, layer, lat, yl, w_o_b)
            states.append(s_ctx)
        else:
            gd = d // FNET_GROUPS
            cc, sc = _dft_tables(gd)
            cs = jnp.concatenate([cc, sc], axis=1)
            w_b = fnet_w[j].astype(BF16)
            for seg in segs:
                ct, st = _dft_tables(seg.seq)
                ac, as_ = _fnet_a(x, mods, layer, seg, g1, cs)
                x = _fnet_b(x, mods, layer, seg, ac, as_, ct, st, w_b, (seg.seq * gd) ** -0.5)
        last = layer == depth - 1
        x = _moe(x, mods, layer, segs, g2, rw_hi, rw_lo, rb, moe_w_gate, moe_w_up, moe_w_down,
                 final_g if last else None)

    y_prompt, y_sample = x
    new_state = jnp.stack(states, axis=1)
    return (y_prompt.reshape(b_ctx, t_ctx, d), y_sample.reshape(b_lat, t_lat, d), new_cache, new_state)
```

```python
import functools
import math

import jax
import jax.numpy as jnp
import numpy as np
from jax import lax
from jax.experimental import pallas as pl
from jax.experimental.pallas import tpu as pltpu
from jax.experimental.pallas import tpu_sc as plsc

F32 = jnp.float32
BF16 = jnp.bfloat16

GRID_W = 64
MLA_HEADS = 8
MLA_NOPE = 128
MLA_ROPE = 64
MLA_V = 128
MLA_Q_LORA = 384
MLA_KV_LORA = 256
ROPE_BASE = 10000.0
RET_HEADS = 4
RET_CHUNK = 256
FNET_GROUPS = 4
N_EXPERTS = 16
N_EXPERT_GROUPS = 4
EXPERTS_PER_GROUP = 4
D_EXPERT = 512
NORM_EPS = 1e-6

LANE = 128
PROJ_ROWS = 512
ATTN_ROWS = 512
WIDE_TILE = 1024
MOE_ROWS = 512
VMEM_LIMIT = 56 * 1024 * 1024
SC_CORES = 2
SC_SUBCORES = 16
SC_CHUNK = 128


def _cparams(*sem):
    return pltpu.CompilerParams(dimension_semantics=sem, vmem_limit_bytes=VMEM_LIMIT)


def _sigmoid(x):
    return 1.0 / (1.0 + jnp.exp(-x))


def _rms(x):
    return x * lax.rsqrt(jnp.mean(x * x, axis=-1, keepdims=True) + NORM_EPS)


def _modulate(x, g, shift, scale):
    return (_rms(x) * g) * (1.0 + scale) + shift


def _dot(a, b):
    return jnp.dot(a, b, preferred_element_type=F32)


def _dot_nt(a, b):
    return lax.dot_general(a, b, (((1,), (1,)), ((), ())), preferred_element_type=F32)


def _mod_kernel(c_ref, w_ref, b_ref, o_ref):
    c = c_ref[...]
    s = (c * _sigmoid(c)).astype(BF16)
    o_ref[...] = _dot(s, w_ref[...].astype(BF16)) + b_ref[...]


def _modulation_all(cond8, ada_w, ada_b):
    depth, d, d6 = ada_w.shape
    tn = d6 // 4
    return pl.pallas_call(
        _mod_kernel,
        grid=(depth, d6 // tn),
        in_specs=[
            pl.BlockSpec((8, d), lambda l, n: (0, 0)),
            pl.BlockSpec((None, d, tn), lambda l, n: (l, 0, n)),
            pl.BlockSpec((None, 1, tn), lambda l, n: (l, 0, n)),
        ],
        out_specs=pl.BlockSpec((None, 8, tn), lambda l, n: (l, 0, n)),
        out_shape=jax.ShapeDtypeStruct((depth, 8, d6), F32),
        compiler_params=_cparams("parallel", "parallel"),
        name="modulation",
    )(cond8, ada_w, ada_b.reshape(depth, 1, d6))


class _Seg:
    def __init__(self, row0, batch, seq, mod0, per_batch_mod):
        self.row0, self.batch, self.seq = row0, batch, seq
        self.mod0, self.per_batch_mod = mod0, per_batch_mod
        self.rows = batch * seq

    def tile(self, want):
        tm = min(want, self.seq) if self.per_batch_mod else want
        assert self.rows % tm == 0 and self.row0 % tm == 0 and (self.seq % tm == 0 or tm % self.seq == 0)
        return tm

    def seq_tile(self, want):
        tm = min(want, self.seq)
        assert self.seq % tm == 0 and self.row0 % tm == 0
        return tm

    def mod_row(self, tile, tm):
        if self.per_batch_mod:
            return self.mod0 + tile * tm // self.seq
        return self.mod0


def _mod_spec(layer, seg, d, tm, tile_of=lambda *a: a[0]):
    return pl.BlockSpec((None, None, 6, 1, d), lambda *a: (layer, seg.mod_row(tile_of(*a), tm), 0, 0, 0))


def _unpack_halves(p):
    lo = lax.bitcast_convert_type(p << 16, F32)
    hi = lax.bitcast_convert_type(p & jnp.uint32(0xFFFF0000), F32)
    return lo, hi


def _moe_residual(y_ref, w_ref, gate):
    w = w_ref[...]
    lo0, hi0 = _unpack_halves(y_ref[0])
    lo1, hi1 = _unpack_halves(y_ref[1])
    w0, w1 = w[:, 0:1], w[:, 1:2]
    return gate * jnp.concatenate([w0 * lo0 + w1 * lo1, w0 * hi0 + w1 * hi1], axis=-1)


def _take_pending(refs, pending):
    if not pending:
        return refs, lambda x: x
    y_ref, w_ref, pm_ref, *rest = refs
    xo_ref = rest.pop()

    def resolve(x):
        x = x + _moe_residual(y_ref, w_ref, pm_ref[5])
        xo_ref[...] = x
        return x

    return rest, resolve


def _pending_io(pending, seg, tm, d, n):
    yg, wcol, mods, layer = pending
    t0 = seg.row0 // tm
    specs = [pl.BlockSpec((2, tm, d // 2), lambda i: (0, t0 + i, 0)),
             pl.BlockSpec((tm, 8), lambda i: (t0 + i, 0)),
             _mod_spec(layer, seg, d, tm)]
    return specs, [yg, wcol, mods], pl.BlockSpec((tm, d), lambda i: (t0 + i, 0)), jax.ShapeDtypeStruct((n, d), F32)


def _rope_partner(x):
    lane = lax.broadcasted_iota(jnp.int32, x.shape, 1)
    first = (lane % 32) < 16
    return jnp.where(first, pltpu.roll(x, LANE - 16, 1), pltpu.roll(x, 16, 1))


def _store_values(v_ref, v):
    ones = jnp.ones((v.shape[0], LANE), BF16)
    for hd in range(MLA_HEADS):
        v_ref[:, hd * 2 * LANE:hd * 2 * LANE + LANE] = v[:, hd * MLA_V:(hd + 1) * MLA_V].astype(BF16)
        v_ref[:, hd * 2 * LANE + LANE:(hd + 1) * 2 * LANE] = ones


def _mla_proj_kernel(*refs, rope, pending):
    refs, resolve = _take_pending(refs, pending)
    if rope:
        (x_ref, m_ref, g_ref, win_ref, qg_ref, kvg_ref, wq_ref, wkn_ref, wv_ref, cos_ref, sin_ref,
         q_ref, k_ref, v_ref) = refs
    else:
        x_ref, m_ref, g_ref, win_ref, qg_ref, kvg_ref, wq_ref, wkn_ref, wv_ref = refs[:9]
        q_ref, k_ref, v_ref, cache_ref = refs[-4:]
    h = _modulate(resolve(x_ref[...]), g_ref[...], m_ref[0], m_ref[1]).astype(BF16)
    z = _dot(h, win_ref[...])
    cq = z[:, :MLA_Q_LORA]
    ckv = z[:, MLA_Q_LORA:MLA_Q_LORA + MLA_KV_LORA]
    kpe = z[:, MLA_Q_LORA + MLA_KV_LORA:]
    cqn = (_rms(cq) * qg_ref[...]).astype(BF16)
    ckvn = _rms(ckv) * kvg_ref[...]
    ckvb = ckvn.astype(BF16)
    q = _dot(cqn, wq_ref[...])
    kn = _dot(ckvb, wkn_ref[...])
    _store_values(v_ref, _dot(ckvb, wv_ref[...]))
    if rope:
        cos, sin = cos_ref[...], sin_ref[...]
        kpe = kpe * cos + _rope_partner(kpe) * sin
    else:
        seq = cache_ref.shape[1]
        for s in range(cache_ref.shape[0]):
            cache_ref[s, :, :MLA_KV_LORA] = ckvn[s * seq:(s + 1) * seq, :]
            cache_ref[s, :, MLA_KV_LORA:] = kpe[s * seq:(s + 1) * seq, :MLA_ROPE]
    kpe_b = kpe.astype(BF16)
    for hd in range(MLA_HEADS):
        lo = hd * 2 * LANE
        q_ref[:, lo:lo + LANE] = q[:, lo:lo + LANE].astype(BF16)
        qr = q[:, lo + LANE:lo + 2 * LANE]
        if rope:
            qr = qr * cos + _rope_partner(qr) * sin
        q_ref[:, lo + LANE:lo + 2 * LANE] = qr.astype(BF16)
        k_ref[:, lo:lo + LANE] = kn[:, hd * LANE:(hd + 1) * LANE].astype(BF16)
        k_ref[:, lo + LANE:lo + 2 * LANE] = kpe_b


def _mla_proj(x, x_row0, mods, layer, seg, g1, w, rope_tabs, cache_slot=None, pending=None):
    n, d = x.shape
    rope = rope_tabs is not None
    rows = seg.rows
    tm = seg.tile(PROJ_ROWS)
    x_tile0 = x_row0 // tm
    hq = MLA_HEADS * 2 * LANE
    const = lambda i: (0, 0)
    aliases = {}
    in_specs = [
        pl.BlockSpec((tm, d), lambda i: (x_tile0 + i, 0)),
        _mod_spec(layer, seg, d, tm),
        pl.BlockSpec((1, d), const),
        pl.BlockSpec(w["w_in"].shape, const),
        pl.BlockSpec((1, MLA_Q_LORA), const),
        pl.BlockSpec((1, MLA_KV_LORA), const),
        pl.BlockSpec(w["w_q"].shape, const),
        pl.BlockSpec(w["w_kn"].shape, const),
        pl.BlockSpec(w["w_v"].shape, const),
    ]
    args = [x, mods, g1, w["w_in"], w["q_g"], w["kv_g"], w["w_q"], w["w_kn"], w["w_v"]]
    out_specs = [pl.BlockSpec((tm, hq), lambda i: (i, 0))] * 3
    out_shape = [jax.ShapeDtypeStruct((rows, hq), BF16)] * 3
    if rope:
        tab = pl.BlockSpec((tm, LANE), lambda i: (i % (seg.seq // tm), 0))
        in_specs += [tab, tab]
        args += list(rope_tabs)
    else:
        cw = MLA_KV_LORA + MLA_ROPE
        prev, slot, n_slots = cache_slot
        assert tm % seg.seq == 0
        out_specs.append(pl.BlockSpec((tm // seg.seq, None, seg.seq, cw), lambda i: (i, slot, 0, 0)))
        out_shape.append(jax.ShapeDtypeStruct((seg.batch, n_slots, seg.seq, cw), F32))
        in_specs.append(pl.BlockSpec(memory_space=pl.ANY))
        args.append(prev)
        aliases = {len(args) - 1: 3}
    if pending is not None:
        p_specs, p_args, xo_spec, xo_shape = _pending_io(pending, seg, tm, d, n)
        in_specs, args = p_specs + in_specs, p_args + args
        out_specs.append(xo_spec)
        out_shape.append(xo_shape)
        aliases = {k + len(p_args): v for k, v in aliases.items()}
        aliases[len(p_args)] = len(out_shape) - 1
    return pl.pallas_call(
        functools.partial(_mla_proj_kernel, rope=rope, pending=pending is not None),
        grid=(rows // tm,),
        in_specs=in_specs,
        out_specs=out_specs,
        out_shape=out_shape,
        input_output_aliases=aliases,
        compiler_params=_cparams("parallel"),
        name="mla_proj_lat" if rope else "mla_proj_ctx",
    )(*args)


def _cache_kv_kernel(c_ref, wkn_ref, wv_ref, k_ref, v_ref):
    c = c_ref[...]
    ckv = c[:, :MLA_KV_LORA].astype(BF16)
    kpe_b = c[:, MLA_KV_LORA:].astype(BF16)
    kn = _dot(ckv, wkn_ref[...])
    _store_values(v_ref, _dot(ckv, wv_ref[...]))
    for hd in range(MLA_HEADS):
        lo = hd * 2 * LANE
        k_ref[:, lo:lo + LANE] = kn[:, hd * LANE:(hd + 1) * LANE].astype(BF16)
        k_ref[:, lo + LANE:lo + 2 * LANE] = kpe_b


def _cache_kv(cache_pad, w):
    rows, cw = cache_pad.shape
    hq = MLA_HEADS * 2 * LANE
    const = lambda i: (0, 0)
    tm = min(PROJ_ROWS, rows)
    assert rows % tm == 0
    return pl.pallas_call(
        _cache_kv_kernel,
        grid=(rows // tm,),
        in_specs=[
            pl.BlockSpec((tm, cw), lambda i: (i, 0)),
            pl.BlockSpec(w["w_kn"].shape, const),
            pl.BlockSpec(w["w_v"].shape, const),
        ],
        out_specs=[pl.BlockSpec((tm, hq), lambda i: (i, 0))] * 2,
        out_shape=[jax.ShapeDtypeStruct((rows, hq), BF16)] * 2,
        compiler_params=_cparams("parallel"),
        name="mla_cache_kv",
    )(cache_pad, w["w_kn"], w["w_v"])


def _attn_kernel(*refs, n_parts):
    q_ref = refs[0]
    kv_refs = refs[1:1 + 2 * n_parts]
    wo_ref, x_ref, m_ref = refs[1 + 2 * n_parts:4 + 2 * n_parts]
    o_ref, acc_ref = refs[-2:]
    for hd in range(MLA_HEADS):
        qh = q_ref[:, hd * 2 * LANE:(hd + 1) * 2 * LANE]
        scores = [_dot_nt(qh, kv_refs[2 * p][:, hd * 2 * LANE:(hd + 1) * 2 * LANE]) for p in range(n_parts)]
        mx = scores[0].max(axis=-1, keepdims=True)
        for s in scores[1:]:
            mx = jnp.maximum(mx, s.max(axis=-1, keepdims=True))
        out = None
        for p, s in enumerate(scores):
            e = jnp.exp((s - mx).astype(BF16))
            pv = _dot(e, kv_refs[2 * p + 1][:, hd * 2 * LANE:(hd + 1) * 2 * LANE])
            out = pv if out is None else out + pv
        acc_ref[:, hd * MLA_V:(hd + 1) * MLA_V] = (out[:, :MLA_V] / out[:, LANE:LANE + MLA_V]).astype(BF16)
    y = _dot(acc_ref[...], wo_ref[...])
    o_ref[...] = x_ref[...] + m_ref[2] * y


def _attention(x, x_row0, n, dest, mods, layer, seg, q, kv_parts, w_o):
    d = x.shape[1]
    hq = MLA_HEADS * 2 * LANE
    hv = MLA_HEADS * MLA_V
    tq = seg.seq_tile(ATTN_ROWS)
    tps = seg.seq // tq
    x_tile0, out_tile0 = x_row0 // tq, seg.row0 // tq
    in_specs = [pl.BlockSpec((tq, hq), lambda b, i: (b * tps + i, 0))]
    args = [q]
    for k, v, rows in kv_parts:
        mode = dict(pipeline_mode=pl.Buffered(1)) if tps > 1 else {}
        in_specs += [pl.BlockSpec((rows, hq), lambda b, i: (b, 0), **mode)] * 2
        args += [k, v]
    in_specs += [
        pl.BlockSpec(w_o.shape, lambda b, i: (0, 0)),
        pl.BlockSpec((tq, d), lambda b, i: (x_tile0 + b * tps + i, 0)),
        _mod_spec(layer, seg, d, tq, tile_of=lambda b, i: b * tps + i),
    ]
    args += [w_o, x, mods]
    if isinstance(dest, str):
        assert dest == "inplace"
        aliases = {len(args) - 2: 0}
    else:
        in_specs.append(pl.BlockSpec(memory_space=pl.ANY))
        args.append(dest)
        aliases = {len(args) - 1: 0}
    return pl.pallas_call(
        functools.partial(_attn_kernel, n_parts=len(kv_parts)),
        grid=(seg.batch, tps),
        in_specs=in_specs,
        out_specs=pl.BlockSpec((tq, d), lambda b, i: (out_tile0 + b * tps + i, 0)),
        out_shape=jax.ShapeDtypeStruct((n, d), F32),
        scratch_shapes=[pltpu.VMEM((tq, hv), BF16)],
        input_output_aliases=aliases,
        compiler_params=_cparams("parallel", "arbitrary"),
        name="mla_attention",
    )(*args)


RET_COL = 1024


RET_ROWS = 512


def _ret_proj_kernel(*refs, kinds, dk, pending):
    refs, resolve = _take_pending(refs, pending)
    rotary = "rotary" in kinds
    if rotary:
        x_ref, m_ref, g_ref, w_ref, cos_ref, sin_ref, z_ref = refs
        cos, sin = cos_ref[...], sin_ref[...]
    else:
        x_ref, m_ref, g_ref, w_ref, z_ref = refs
    h = _modulate(resolve(x_ref[...]), g_ref[...], m_ref[0], m_ref[1]).astype(BF16)
    half = dk // 2
    for j, kind in enumerate(kinds):
        c0 = j * RET_COL
        acc = _dot(h, w_ref[:, c0:c0 + RET_COL])
        if kind == "rotary":
            for hd in range(RET_COL // dk):
                lo = hd * dk
                x1, x2 = acc[:, lo:lo + half], acc[:, lo + half:lo + dk]
                z_ref[:, c0 + lo:c0 + lo + half] = (x1 * cos - x2 * sin).astype(BF16)
                z_ref[:, c0 + lo + half:c0 + lo + dk] = (x1 * sin + x2 * cos).astype(BF16)
        elif kind == "silu":
            z_ref[:, c0:c0 + RET_COL] = (acc * _sigmoid(acc)).astype(BF16)
        else:
            z_ref[:, c0:c0 + RET_COL] = acc.astype(BF16)


def _ret_proj(x, mods, layer, seg, g1, w_in, group, kinds, rot_tabs, dk, pending=None):
    n, d = x.shape
    rows = seg.rows
    tm = seg.tile(PROJ_ROWS)
    ncol = len(kinds) * RET_COL
    tps = max(seg.seq // tm, 1)
    in_specs = [
        pl.BlockSpec((tm, d), lambda i: (seg.row0 // tm + i, 0)),
        _mod_spec(layer, seg, d, tm),
        pl.BlockSpec((1, d), lambda i: (0, 0)),
        pl.BlockSpec((d, ncol), lambda i: (0, group)),
    ]
    args = [x, mods, g1, w_in]
    if "rotary" in kinds:
        tab = pl.BlockSpec((tm, dk // 2), lambda i: (i % tps, 0))
        in_specs += [tab, tab]
        args += list(rot_tabs)
    out_specs = [pl.BlockSpec((tm, ncol), lambda i: (i, 0))]
    out_shape = [jax.ShapeDtypeStruct((rows, ncol), BF16)]
    aliases = {}
    if pending is not None:
        p_specs, p_args, xo_spec, xo_shape = _pending_io(pending, seg, tm, d, n)
        in_specs, args = p_specs + in_specs, p_args + args
        out_specs.append(xo_spec)
        out_shape.append(xo_shape)
        aliases = {len(p_args): 1}
    res = pl.pallas_call(
        functools.partial(_ret_proj_kernel, kinds=kinds, dk=dk, pending=pending is not None),
        grid=(rows // tm,),
        in_specs=in_specs,
        out_specs=out_specs,
        out_shape=out_shape,
        input_output_aliases=aliases,
        compiler_params=_cparams("parallel"),
        name="ret_proj_" + kinds[0],
    )(*args)
    return res if pending is not None else res[0]


def _log_sigmoid(x):
    return jnp.minimum(x, 0.0) - jnp.log(1.0 + jnp.exp(-jnp.abs(x)))


def _ret_scan_kernel(*refs, has_s0, emit_state, n_chunks, heads):
    refs = list(refs)
    lf_ref, lb_ref, q_ref, k_ref, v_ref, gf_ref, gb_ref = refs[:7]
    pos = 7
    s0_ref = None
    if has_s0:
        s0_ref = refs[pos]
        pos += 1
    y_ref = refs[pos]
    pos += 1
    sout_ref = None
    if emit_state:
        sout_ref = refs[pos]
        pos += 1
    s_ref, yf_ref = refs[pos:]
    c = RET_CHUNK
    dk, dv = s_ref.shape
    ii = lax.broadcasted_iota(jnp.int32, (c, c), 0).astype(F32)
    jj = lax.broadcasted_iota(jnp.int32, (c, c), 1).astype(F32)
    idx = lax.broadcasted_iota(jnp.int32, (c, 1), 0).astype(F32)

    for hd, direction in [(hd, direction) for hd in range(heads) for direction in range(2)]:
        fwd = direction == 0
        kcol, vcol = slice(hd * dk, (hd + 1) * dk), slice(hd * dv, (hd + 1) * dv)
        lg = _log_sigmoid((lf_ref if fwd else lb_ref)[hd])
        rel = (ii - jj) if fwd else (jj - ii)
        keep = rel >= 0
        decay_in = jnp.where(keep, jnp.exp(jnp.where(keep, rel, 0.0) * lg), 0.0)
        decay_q = jnp.exp(((idx + 1.0) if fwd else (c - idx)) * lg)
        decay_k = jnp.exp(((c - 1.0 - idx) if fwd else idx) * lg)
        decay_c = jnp.exp(c * lg)
        g_ref = gf_ref if fwd else gb_ref

        def chunk(cc, state, fwd=fwd, decay_in=decay_in, decay_q=decay_q, decay_k=decay_k, decay_c=decay_c,
                  g_ref=g_ref, kcol=kcol, vcol=vcol):
            r0 = cc * c if isinstance(cc, int) else pl.multiple_of(cc * c, c)
            qc = q_ref[pl.ds(r0, c), kcol]
            kc = k_ref[pl.ds(r0, c), kcol]
            vc = v_ref[pl.ds(r0, c), vcol]
            sc = _dot_nt(qc, kc) * decay_in
            out = _dot(sc.astype(BF16), vc)
            kd_t = (kc.astype(F32) * decay_k).T.astype(BF16)
            new_s = _dot(kd_t, vc)
            if state is not None:
                out = out + decay_q * _dot(qc, state.astype(BF16))
                new_s = decay_c * state + new_s
            s_ref[...] = new_s
            o = _rms(out) * g_ref[pl.ds(r0, c), vcol].astype(F32)
            if fwd:
                yf_ref[pl.ds(r0, c), :] = o
            else:
                y_ref[pl.ds(r0, c), vcol] = (yf_ref[pl.ds(r0, c), :] + o).astype(BF16)

        chunk(0 if fwd else n_chunks - 1, s0_ref[direction, hd] if has_s0 else None)

        def step(ci, carry, fwd=fwd, chunk=chunk):
            chunk(ci if fwd else n_chunks - 1 - ci, s_ref[...])
            return carry

        lax.fori_loop(1, n_chunks, step, 0, unroll=True)
        if emit_state:
            sout_ref[direction, hd] = s_ref[...]


def _ret_scan(qkv, g, seg, logit_f, logit_b, s0, emit_state, dk, dv, heads):
    rows = seg.batch * seg.seq
    t = seg.seq
    hh = RET_HEADS
    groups = hh // heads
    v0 = 2 * hh * dk // (heads * dv)
    assert groups * heads == hh and v0 * heads * dv == 2 * hh * dk
    in_specs = [
        pl.BlockSpec((heads, 1, 1), lambda b, h: (h, 0, 0)),
        pl.BlockSpec((heads, 1, 1), lambda b, h: (h, 0, 0)),
        pl.BlockSpec((t, heads * dk), lambda b, h: (b, h)),
        pl.BlockSpec((t, heads * dk), lambda b, h: (b, groups + h)),
        pl.BlockSpec((t, heads * dv), lambda b, h: (b, v0 + h)),
        pl.BlockSpec((t, heads * dv), lambda b, h: (b, h)),
        pl.BlockSpec((t, heads * dv), lambda b, h: (b, groups + h)),
    ]
    args = [logit_f.reshape(hh, 1, 1), logit_b.reshape(hh, 1, 1), qkv, qkv, qkv, g, g]
    state_spec = pl.BlockSpec((None, 2, heads, dk, dv), lambda b, h: (b, 0, h, 0, 0))
    if s0 is not None:
        in_specs.append(state_spec)
        args.append(s0)
    out_specs = [pl.BlockSpec((t, heads * dv), lambda b, h: (b, h))]
    out_shape = [jax.ShapeDtypeStruct((rows, hh * dv), BF16)]
    if emit_state:
        out_specs.append(state_spec)
        out_shape.append(jax.ShapeDtypeStruct((seg.batch, 2, hh, dk, dv), F32))
    res = pl.pallas_call(
        functools.partial(_ret_scan_kernel, has_s0=s0 is not None, emit_state=emit_state,
                          n_chunks=t // RET_CHUNK, heads=heads),
        grid=(seg.batch, groups),
        in_specs=in_specs,
        out_specs=out_specs,
        out_shape=out_shape,
        scratch_shapes=[pltpu.VMEM((dk, dv), F32), pltpu.VMEM((t, dv), F32)],
        compiler_params=_cparams("parallel", "parallel"),
        name="ret_scan",
    )(*args)
    return res if emit_state else (res[0], None)


def _mm_res_kernel(a_ref, w_ref, x_ref, m_ref, o_ref):
    o_ref[...] = x_ref[...] + m_ref[2] * _dot(a_ref[...], w_ref[...])


def _matmul_residual(x, mods, layer, seg, a, w):
    n, d = x.shape
    tm = seg.tile(PROJ_ROWS)
    x_spec = pl.BlockSpec((tm, d), lambda i: (seg.row0 // tm + i, 0))
    return pl.pallas_call(
        _mm_res_kernel,
        grid=(seg.rows // tm,),
        in_specs=[
            pl.BlockSpec((tm, a.shape[1]), lambda i: (i, 0)),
            pl.BlockSpec(w.shape, lambda i: (0, 0)),
            x_spec,
            _mod_spec(layer, seg, d, tm),
        ],
        out_specs=x_spec,
        out_shape=jax.ShapeDtypeStruct((n, d), F32),
        input_output_aliases={2: 0},
        compiler_params=_cparams("parallel"),
        name="matmul_residual",
    )(a, w, x, mods)


def _fnet_a_kernel(*refs, gd, pending):
    refs, resolve = _take_pending(refs, pending)
    x_ref, m_ref, g_ref, cs_ref, ac_ref, as_ref = refs
    h = _modulate(resolve(x_ref[...]), g_ref[...], m_ref[0], m_ref[1]).astype(BF16)
    cs = cs_ref[...]
    for g in range(FNET_GROUPS):
        a = _dot(h[:, g * gd:(g + 1) * gd], cs)
        ac_ref[:, g * gd:(g + 1) * gd] = a[:, :gd].astype(BF16)
        as_ref[:, g * gd:(g + 1) * gd] = a[:, gd:].astype(BF16)


def _fnet_a(x, mods, layer, seg, g1, cs, pending=None):
    n, d = x.shape
    rows = seg.rows
    tm = seg.tile(PROJ_ROWS)
    out = pl.BlockSpec((tm, d), lambda i: (i, 0))
    in_specs = [
        pl.BlockSpec((tm, d), lambda i: (seg.row0 // tm + i, 0)),
        _mod_spec(layer, seg, d, tm),
        pl.BlockSpec((1, d), lambda i: (0, 0)),
        pl.BlockSpec(cs.shape, lambda i: (0, 0)),
    ]
    args = [x, mods, g1, cs]
    out_specs, out_shape, aliases = [out, out], [jax.ShapeDtypeStruct((rows, d), BF16)] * 2, {}
    if pending is not None:
        p_specs, p_args, xo_spec, xo_shape = _pending_io(pending, seg, tm, d, n)
        in_specs, args = p_specs + in_specs, p_args + args
        out_specs.append(xo_spec)
        out_shape.append(xo_shape)
        aliases = {len(p_args): 2}
    return pl.pallas_call(
        functools.partial(_fnet_a_kernel, gd=d // FNET_GROUPS, pending=pending is not None),
        grid=(rows // tm,),
        in_specs=in_specs,
        out_specs=out_specs,
        out_shape=out_shape,
        input_output_aliases=aliases,
        compiler_params=_cparams("parallel"),
        name="fnet_channel_dft",
    )(*args)


def _fnet_b_kernel(ct_ref, st_ref, ac_ref, as_ref, w_ref, x_ref, m_ref, o_ref, *, norm):
    f = (_dot(ct_ref[...], ac_ref[...]) - _dot(st_ref[...], as_ref[...])) * norm
    o_ref[...] = x_ref[...] + m_ref[2] * _dot(f.astype(BF16), w_ref[...])


def _fnet_b(x, mods, layer, seg, ac, as_, ct, st, w, norm):
    n, d = x.shape
    t = seg.seq
    tq = seg.seq_tile(ATTN_ROWS)
    tps = t // tq
    x_spec = pl.BlockSpec((tq, d), lambda b, i: (seg.row0 // tq + b * tps + i, 0))
    tab = pl.BlockSpec((tq, t), lambda b, i: (i, 0))
    seq = pl.BlockSpec((t, d), lambda b, i: (b, 0))
    return pl.pallas_call(
        functools.partial(_fnet_b_kernel, norm=norm),
        grid=(seg.batch, tps),
        in_specs=[tab, tab, seq, seq, pl.BlockSpec(w.shape, lambda b, i: (0, 0)), x_spec,
                  _mod_spec(layer, seg, d, tq, tile_of=lambda b, i: b * tps + i)],
        out_specs=x_spec,
        out_shape=jax.ShapeDtypeStruct((n, d), F32),
        input_output_aliases={5: 0},
        compiler_params=_cparams("parallel", "arbitrary"),
        name="fnet_position_dft",
    )(ct, st, ac, as_, w, x, mods)


def _pack_halves(a):
    w = a.shape[1] // 2
    bits = lambda v: lax.bitcast_convert_type(v.astype(BF16).astype(F32), jnp.uint32)
    return (bits(a[:, :w]) >> 16) | (bits(a[:, w:]) & jnp.uint32(0xFFFF0000))


def _router_kernel(x_ref, m_ref, g_ref, rwhi_ref, rwlo_ref, rb_ref, h_ref, idx_ref, rank_ref, wcol_ref, cnt_ref,
                   run_ref, tri_ref):
    step = pl.program_id(0)

    @pl.when(step == 0)
    def _():
        run_ref[...] = jnp.zeros_like(run_ref)
        tt = tri_ref.shape[0]
        earlier = lax.broadcasted_iota(jnp.int32, (tt, tt), 0) < lax.broadcasted_iota(jnp.int32, (tt, tt), 1)
        tri_ref[...] = jnp.where(earlier, 1.0, 0.0).astype(BF16)

    h = _modulate(x_ref[...], g_ref[...], m_ref[3], m_ref[4])
    h_ref[...] = _pack_halves(h)
    h_hi = h.astype(BF16)
    h_lo = (h - h_hi.astype(F32)).astype(BF16)
    logits = _dot_nt(rwhi_ref[...], h_hi) + (_dot_nt(rwhi_ref[...], h_lo) + _dot_nt(rwlo_ref[...], h_hi))
    sc = _sigmoid(logits)
    gr = sc + rb_ref[...]
    gp = EXPERTS_PER_GROUP
    row = lambda a, e: a[e:e + 1, :]
    best_g = None
    for g in range(N_EXPERT_GROUPS):
        vals = [row(gr, g * gp + i) for i in range(gp)]
        gs = None
        for i in range(gp):
            for j in range(i + 1, gp):
                pair = vals[i] + vals[j]
                gs = pair if gs is None else jnp.maximum(gs, pair)
        if best_g is None:
            best_g, best_v = jnp.zeros(gs.shape, jnp.int32), gs
        else:
            better = gs > best_v
            best_g = jnp.where(better, g, best_g)
            best_v = jnp.where(better, gs, best_v)
    sel, raw = [], []
    for i in range(gp):
        s_i, r_i = row(gr, i), row(sc, i)
        for g in range(1, N_EXPERT_GROUPS):
            s_i = jnp.where(best_g == g, row(gr, g * gp + i), s_i)
            r_i = jnp.where(best_g == g, row(sc, g * gp + i), r_i)
        sel.append(s_i)
        raw.append(r_i)

    def argmax_first(vals, raws):
        bi, bv, br = jnp.zeros(vals[0].shape, jnp.int32), vals[0], raws[0]
        for i in range(1, len(vals)):
            better = vals[i] > bv
            bi = jnp.where(better, i, bi)
            bv = jnp.where(better, vals[i], bv)
            br = jnp.where(better, raws[i], br)
        return bi, br

    i1, w1 = argmax_first(sel, raw)
    masked = [jnp.where(i1 == i, -jnp.inf, sel[i]) for i in range(gp)]
    i2, w2 = argmax_first(masked, raw)
    tot = w1 + w2
    e1 = best_g * gp + i1
    e2 = best_g * gp + i2
    idx_ref[0:1, :] = e1
    idx_ref[1:2, :] = e2
    t = e1.shape[1]
    sub = lax.broadcasted_iota(jnp.int32, (8, t), 0)
    w8 = jnp.where(sub == 0, w1 / tot, jnp.where(sub == 1, w2 / tot, 0.0))
    wcol_ref[...] = w8.T
    eio = lax.broadcasted_iota(jnp.int32, (N_EXPERTS, t), 0)
    oh1, oh2 = eio == e1, eio == e2
    oh = jnp.where(oh1, 1.0, jnp.where(oh2, 1.0, 0.0))
    local = _dot(oh.astype(BF16), tri_ref[...])
    rank = local + run_ref[:, 0:1]
    rank_ref[0:1, :] = jnp.sum(jnp.where(oh1, rank, 0.0), axis=0, keepdims=True).astype(jnp.int32)
    rank_ref[1:2, :] = jnp.sum(jnp.where(oh2, rank, 0.0), axis=0, keepdims=True).astype(jnp.int32)
    run_ref[...] = run_ref[...] + jnp.sum(oh, axis=1, keepdims=True)
    cnt_ref[...] = run_ref[...]


def _wide_mod_row(segs, tm):
    ctx, lat = segs
    ctx_tiles = ctx.batch * ctx.seq // tm
    assert ctx_tiles * tm == ctx.batch * ctx.seq and lat.seq % tm == 0
    return lambda i: jnp.where(i < ctx_tiles, ctx.mod0, lat.mod0 + (i - ctx_tiles) // (lat.seq // tm))


def _router(x, mods, layer, segs, g2, rw_hi, rw_lo, rb):
    n, d = x.shape
    tm = WIDE_TILE
    mod_row = _wide_mod_row(segs, tm)
    return pl.pallas_call(
        _router_kernel,
        grid=(n // tm,),
        in_specs=[
            pl.BlockSpec((tm, d), lambda i: (i, 0)),
            pl.BlockSpec((None, None, 6, 1, d), lambda i: (layer, mod_row(i), 0, 0, 0)),
            pl.BlockSpec((1, d), lambda i: (0, 0)),
            pl.BlockSpec(rw_hi.shape, lambda i: (0, 0)),
            pl.BlockSpec(rw_lo.shape, lambda i: (0, 0)),
            pl.BlockSpec(rb.shape, lambda i: (0, 0)),
        ],
        out_specs=[
            pl.BlockSpec((tm, d // 2), lambda i: (i, 0)),
            pl.BlockSpec((2, tm), lambda i: (0, i)),
            pl.BlockSpec((2, tm), lambda i: (0, i)),
            pl.BlockSpec((tm, 8), lambda i: (i, 0)),
            pl.BlockSpec((N_EXPERTS, LANE), lambda i: (0, 0)),
        ],
        out_shape=[
            jax.ShapeDtypeStruct((n, d // 2), jnp.uint32),
            jax.ShapeDtypeStruct((2, n), jnp.int32),
            jax.ShapeDtypeStruct((2, n), jnp.int32),
            jax.ShapeDtypeStruct((n, 8), F32),
            jax.ShapeDtypeStruct((N_EXPERTS, LANE), F32),
        ],
        scratch_shapes=[pltpu.VMEM((N_EXPERTS, LANE), F32), pltpu.VMEM((tm, tm), BF16)],
        compiler_params=_cparams("arbitrary"),
        name="moe_router",
    )(x, mods, g2, rw_hi, rw_lo, rb)


def _expert_kernel(be_ref, br_ref, xs_ref, wg_ref, wu_ref, wd_ref, y_ref, wg_b, wu_b, wd_b):
    i = pl.program_id(0)
    prev = be_ref[jnp.maximum(i - 1, 0)]
    valid = br_ref[i]
    half_rows = y_ref.shape[0] // 2

    @pl.when(jnp.logical_or(i == 0, be_ref[i] != prev))
    def _():
        wg_b[...] = wg_ref[...].astype(BF16)
        wu_b[...] = wu_ref[...].astype(BF16)
        wd_b[...] = wd_ref[...].astype(BF16)

    def ffn(r0, nrows):
        lo, hi = _unpack_halves(xs_ref[r0:r0 + nrows, :])
        xb = jnp.concatenate([lo.astype(BF16), hi.astype(BF16)], axis=1)
        gate = _dot(xb, wg_b[...])
        hid = (gate * _sigmoid(gate)) * _dot(xb, wu_b[...])
        y_ref[r0:r0 + nrows, :] = _pack_halves(_dot(hid.astype(BF16), wd_b[...]))

    @pl.when(valid > half_rows)
    def _():
        ffn(0, 2 * half_rows)

    @pl.when(jnp.logical_and(valid > 0, valid <= half_rows))
    def _():
        ffn(0, half_rows)
        y_ref[half_rows:, :] = jnp.zeros((half_rows, y_ref.shape[1]), y_ref.dtype)

    @pl.when(valid <= 0)
    def _():
        y_ref[...] = jnp.zeros_like(y_ref)


def _experts(xs, block_e, block_rows, w_gate, w_up, w_down, layer):
    rows, half = xs.shape
    d = 2 * half
    de = w_gate.shape[-1]
    n_blocks = rows // MOE_ROWS
    grid_spec = pltpu.PrefetchScalarGridSpec(
        num_scalar_prefetch=2,
        grid=(n_blocks,),
        in_specs=[
            pl.BlockSpec((MOE_ROWS, half), lambda i, be, nb: (i, 0)),
            pl.BlockSpec((None, None, d, de), lambda i, be, nb: (layer, be[i], 0, 0)),
            pl.BlockSpec((None, None, d, de), lambda i, be, nb: (layer, be[i], 0, 0)),
            pl.BlockSpec((None, None, de, d), lambda i, be, nb: (layer, be[i], 0, 0)),
        ],
        out_specs=pl.BlockSpec((MOE_ROWS, half), lambda i, be, nb: (i, 0)),
        scratch_shapes=[pltpu.VMEM((d, de), BF16), pltpu.VMEM((d, de), BF16), pltpu.VMEM((de, d), BF16)],
    )
    return pl.pallas_call(
        _expert_kernel,
        grid_spec=grid_spec,
        out_shape=jax.ShapeDtypeStruct((rows, half), jnp.uint32),
        compiler_params=_cparams("arbitrary"),
        name="moe_experts",
    )(block_e, block_rows, xs, w_gate, w_up, w_down)


def _combine_kernel(x_ref, m_ref, y_ref, w_ref, fg_ref, o_ref):
    x = x_ref[...] + _moe_residual(y_ref, w_ref, m_ref[5])
    o_ref[...] = _rms(x) * fg_ref[...]


def _combine(x, mods, layer, segs, seg, yg, wcol, final_g):
    n, d = x.shape
    tm = WIDE_TILE
    mod_row = _wide_mod_row(segs, tm)
    t0 = seg.row0 // tm
    steps = seg.rows // tm
    in_specs = [pl.BlockSpec((tm, d), lambda i: (t0 + i, 0)),
                pl.BlockSpec((None, None, 6, 1, d), lambda i: (layer, mod_row(t0 + i), 0, 0, 0)),
                pl.BlockSpec((2, tm, d // 2), lambda i: (0, t0 + i, 0)),
                pl.BlockSpec((tm, 8), lambda i: (t0 + i, 0)),
                pl.BlockSpec((1, d), lambda i: (0, 0))]
    return pl.pallas_call(
        _combine_kernel,
        grid=(steps,),
        in_specs=in_specs,
        out_specs=pl.BlockSpec((tm, d), lambda i: (i, 0)),
        out_shape=jax.ShapeDtypeStruct((seg.rows, d), F32),
        compiler_params=_cparams("parallel"),
        name="moe_combine_final",
    )(x, mods, yg, wcol, final_g)


def _sc_mesh():
    return plsc.VectorSubcoreMesh(core_axis_name="c", subcore_axis_name="s")


def _sc_worker_split(n):
    workers = SC_CORES * SC_SUBCORES
    per = n // workers
    assert per * workers == n and per % SC_CHUNK == 0
    return workers, per, per // SC_CHUNK


def _sc_dispatch(h, pos, rows):
    n, w = h.shape
    workers, per, chunks = _sc_worker_split(n)

    @functools.partial(
        pl.kernel, out_type=jax.ShapeDtypeStruct((rows, w), h.dtype), mesh=_sc_mesh(),
        scratch_types=[pltpu.VMEM((2, chunks, SC_CHUNK), jnp.int32), pltpu.VMEM((SC_CHUNK, w), h.dtype)],
        name="moe_dispatch_scatter")
    def scatter_rows(h_hbm, pos_hbm, xs_hbm, idx_v, rows_v):
        wid = lax.axis_index("s") * SC_CORES + lax.axis_index("c")
        pltpu.sync_copy(pos_hbm.at[0, wid], idx_v.at[0])
        pltpu.sync_copy(pos_hbm.at[1, wid], idx_v.at[1])

        @pl.loop(0, chunks)
        def _(c):
            pltpu.sync_copy(h_hbm.at[pl.ds(wid * per + c * SC_CHUNK, SC_CHUNK)], rows_v)
            pltpu.sync_copy(rows_v, xs_hbm.at[idx_v.at[0, c]])
            pltpu.sync_copy(rows_v, xs_hbm.at[idx_v.at[1, c]])

    return scatter_rows(h, pos.reshape(2, workers, chunks, SC_CHUNK))


def _sc_gather2(ys, pos):
    _, w = ys.shape
    n = pos.shape[1]
    workers, per, chunks = _sc_worker_split(n)

    @functools.partial(
        pl.kernel, out_type=jax.ShapeDtypeStruct((2, n, w), ys.dtype), mesh=_sc_mesh(),
        scratch_types=[pltpu.VMEM((2, chunks, SC_CHUNK), jnp.int32), pltpu.VMEM((SC_CHUNK, w), ys.dtype),
                       pltpu.SemaphoreType.DMA],
        name="moe_combine_gather")
    def gather_rows(ys_hbm, pos_hbm, out_hbm, idx_v, rows_v, sem):
        wid = lax.axis_index("s") * SC_CORES + lax.axis_index("c")
        pltpu.sync_copy(pos_hbm.at[0, wid], idx_v.at[0])
        pltpu.sync_copy(pos_hbm.at[1, wid], idx_v.at[1])

        @pl.loop(0, chunks)
        def _(c):
            for k in range(2):
                pltpu.async_copy(ys_hbm.at[idx_v.at[k, c]], rows_v, sem).wait()
                pltpu.sync_copy(rows_v, out_hbm.at[k, pl.ds(wid * per + c * SC_CHUNK, SC_CHUNK)])

    return gather_rows(ys, pos.reshape(2, workers, chunks, SC_CHUNK))


def _dispatch_plan(idx, rank, counts):
    n = idx.shape[1]
    padded = (counts + MOE_ROWS - 1) // MOE_ROWS * MOE_ROWS
    pad_end = jnp.cumsum(padded)
    pad_start = pad_end - padded
    experts = jnp.arange(N_EXPERTS, dtype=jnp.int32)
    start_of = jnp.sum(jnp.where(idx[..., None] == experts, pad_start, 0), axis=-1)
    pos = start_of + rank
    n_blocks = 2 * n // MOE_ROWS + N_EXPERTS
    starts = jnp.arange(n_blocks, dtype=jnp.int32) * MOE_ROWS
    block_e = jnp.minimum(jnp.sum(starts[:, None] >= pad_end[None, :], axis=1), N_EXPERTS - 1).astype(jnp.int32)
    seg_end = jnp.sum(jnp.where(block_e[:, None] == experts, pad_start + counts, 0), axis=-1)
    block_rows = jnp.clip(seg_end - starts, 0, MOE_ROWS).astype(jnp.int32)
    return pos, block_e, block_rows, n_blocks * MOE_ROWS


def _moe(x, mods, layer, segs, g2, rw_hi, rw_lo, rb, w_gate, w_up, w_down, final_g):
    h2p, idx, rank, wcol, cnt = _router(x, mods, layer, segs, g2, rw_hi, rw_lo, rb)
    pos, block_e, block_rows, rows = _dispatch_plan(idx, rank, cnt[:, 0].astype(jnp.int32))
    xs = _sc_dispatch(h2p, pos, rows)
    ys = _experts(xs, block_e, block_rows, w_gate, w_up, w_down, layer)
    yg = _sc_gather2(ys, pos)
    if final_g is None:
        return yg, wcol, mods, layer
    return tuple(_combine(x, mods, layer, segs, seg, yg, wcol, final_g) for seg in segs)


def _mla_rope_tables(t):
    axis_dim = MLA_ROPE // 2
    row = np.repeat(np.arange(t // GRID_W), GRID_W).astype(np.float64)
    col = np.tile(np.arange(GRID_W), t // GRID_W).astype(np.float64)
    inv = ROPE_BASE ** (-np.arange(0, axis_dim, 2, dtype=np.float64) / axis_dim)
    ar, ac = row[:, None] * inv[None, :], col[:, None] * inv[None, :]
    ones = np.ones((t, LANE - MLA_ROPE))
    cos = np.concatenate([np.cos(ar), np.cos(ar), np.cos(ac), np.cos(ac), ones], axis=-1)
    sin = np.concatenate([-np.sin(ar), np.sin(ar), -np.sin(ac), np.sin(ac), 0.0 * ones], axis=-1)
    return jnp.asarray(cos, F32), jnp.asarray(sin, F32)


def _ret_rot_tables(t, dk):
    inv = ROPE_BASE ** (-np.linspace(0.0, 1.0, dk // 2))
    ang = np.arange(t, dtype=np.float64)[:, None] * inv[None, :]
    return jnp.asarray(np.cos(ang), F32), jnp.asarray(np.sin(ang), F32)


def _dft_tables(n):
    k = np.arange(n, dtype=np.int64)
    ang = (np.outer(k, k) % n).astype(np.float64) * (2.0 * math.pi / n)
    return jnp.asarray(np.cos(ang), BF16), jnp.asarray(np.sin(ang), BF16)


def _mla_weights(w_in, q_g, kv_g, w_uq, w_ukv, w_o):
    d = w_in.shape[0]
    hd = MLA_NOPE + MLA_ROPE
    w_in_p = jnp.concatenate([w_in, jnp.zeros((d, LANE - MLA_ROPE), w_in.dtype)], axis=1)
    uq = w_uq.reshape(MLA_Q_LORA, MLA_HEADS, hd)
    uq = jnp.concatenate([uq, jnp.zeros((MLA_Q_LORA, MLA_HEADS, 2 * LANE - hd), uq.dtype)], axis=-1)
    ukv = w_ukv.reshape(MLA_KV_LORA, MLA_HEADS, MLA_NOPE + MLA_V)
    return {
        "w_in": w_in_p.astype(BF16),
        "q_g": q_g.reshape(1, -1) * (MLA_NOPE + MLA_ROPE) ** -0.5,
        "kv_g": kv_g.reshape(1, -1),
        "w_q": uq.reshape(MLA_Q_LORA, MLA_HEADS * 2 * LANE).astype(BF16),
        "w_kn": ukv[..., :MLA_NOPE].reshape(MLA_KV_LORA, MLA_HEADS * MLA_NOPE).astype(BF16),
        "w_v": ukv[..., MLA_NOPE:].reshape(MLA_KV_LORA, MLA_HEADS * MLA_V).astype(BF16),
        "w_o": w_o.astype(BF16),
    }


def kernel(x_prompt, x_sample, cache_mla, state_ret, c, c_ctx, norm1_g, norm2_g, ada_w, ada_b, final_norm_g,
           mla_w_in, mla_q_norm_g, mla_kv_norm_g, mla_w_uq, mla_w_ukv, mla_w_o, ret_w_in, ret_decay_f,
           ret_decay_b, ret_w_o, fnet_w, router_w, router_b, moe_w_gate, moe_w_up, moe_w_down):
    b_ctx, t_ctx, d = x_prompt.shape
    b_lat, t_lat, _ = x_sample.shape
    depth = ada_w.shape[0]
    assert b_lat + 1 <= 8
    n_ctx = b_ctx * t_ctx
    ctx = _Seg(0, b_ctx, t_ctx, 0, False)
    lat = _Seg(n_ctx, b_lat, t_lat, 1, True)
    segs = (ctx, lat)

    n_lat = b_lat * t_lat
    n_mla = mla_w_in.shape[0]
    assert n_mla >= 1
    x = None
    new_cache = jnp.zeros((b_ctx, n_mla, t_ctx, MLA_KV_LORA + MLA_ROPE), F32)
    cond8 = jnp.concatenate([c_ctx[None, :], c, jnp.zeros((8 - 1 - b_lat, d), F32)], axis=0)
    mods = _modulation_all(cond8, ada_w, ada_b).reshape(depth, 8, 6, 1, d)

    rw_t = router_w.T.astype(F32)
    rw_hi = rw_t.astype(BF16)
    rw_lo = (rw_t - rw_hi.astype(F32)).astype(BF16)
    rb = router_b.reshape(N_EXPERTS, 1).astype(F32)
    final_g = final_norm_g.reshape(1, d)
    dk = ret_w_in.shape[2] // (8 * RET_HEADS)
    dv = 2 * dk

    states = []
    pending = None
    counters = [0, 0, 0]
    for layer in range(depth):
        kind = layer % 3
        j = counters[kind]
        counters[kind] += 1
        g1 = norm1_g[layer].reshape(1, d)
        g2 = norm2_g[layer].reshape(1, d)
        if kind == 0:
            w = _mla_weights(mla_w_in[j], mla_q_norm_g[j], mla_kv_norm_g[j], mla_w_uq[j], mla_w_ukv[j],
                             mla_w_o[j])
            if x is None:
                xc, xc0, xl, xl0 = x_prompt.reshape(n_ctx, d), 0, x_sample.reshape(n_lat, d), 0
            else:
                xc, xc0, xl, xl0 = x, ctx.row0, x, lat.row0
            if pending is None:
                qc, kc, vc, new_cache = _mla_proj(xc, xc0, mods, layer, ctx, g1, w, None, (new_cache, j, n_mla))
                ql, kl, vl = _mla_proj(xl, xl0, mods, layer, lat, g1, w, _mla_rope_tables(t_lat))
            else:
                qc, kc, vc, new_cache, x = _mla_proj(x, ctx.row0, mods, layer, ctx, g1, w, None,
                                                     (new_cache, j, n_mla), pending)
                ql, kl, vl, x = _mla_proj(x, lat.row0, mods, layer, lat, g1, w, _mla_rope_tables(t_lat), None,
                                          pending)
                xc = xl = x
            past = cache_mla.shape[2]
            cpad = jnp.pad(cache_mla[:, j].reshape(b_lat * past, -1), ((0, 0), (0, LANE - MLA_ROPE)))
            kp, vp = _cache_kv(cpad, w)
            first = x is None
            x = _attention(xc, xc0, n_ctx + n_lat, jnp.zeros((n_ctx + n_lat, d), F32) if first else "inplace",
                           mods, layer, ctx, qc, [(kc, vc, t_ctx)], w["w_o"])
            x = _attention(xl if first else x, xl0, n_ctx + n_lat, x if first else "inplace", mods, layer, lat,
                           ql, [(kl, vl, t_lat), (kp, vp, past)], w["w_o"])
        elif kind == 1:
            w_in = ret_w_in[j]
            qk = RET_HEADS * dk
            k_scale = jnp.concatenate([jnp.ones((qk,), F32), jnp.full((qk,), dk ** -0.5, F32),
                                       jnp.ones((w_in.shape[1] - 2 * qk,), F32)])
            w_in_b = (w_in * k_scale[None, :]).astype(BF16)
            w_o_b = ret_w_o[j].astype(BF16)
            rot = _ret_rot_tables(t_lat, dk)
            n_qk, n_v = 2 * qk // RET_COL, RET_HEADS * dv // RET_COL
            assert (n_qk + n_v) * RET_COL * 2 == w_in.shape[1]
            parts = []
            for seg in segs:
                kinds = (("rotary" if seg is lat else "plain"),) * n_qk + ("plain",) * n_v
                qkv = _ret_proj(x, mods, layer, seg, g1, w_in_b, 0, kinds, rot, dk, pending)
                if pending is not None:
                    qkv, x = qkv
                parts.append((qkv, _ret_proj(x, mods, layer, seg, g1, w_in_b, 1, ("silu",) * (n_qk + n_v), None, dk)))
            yc, s_ctx = _ret_scan(*parts[0], ctx, ret_decay_f[j], ret_decay_b[j], None, True, dk, dv, RET_HEADS)
            yl, _ = _ret_scan(*parts[1], lat, ret_decay_f[j], ret_decay_b[j], state_ret[:, j], False, dk, dv, 1)
            x = _matmul_residual(x, mods, layer, ctx, yc, w_o_b)
            x = _matmul_residual(x, mods, layer, lat, yl, w_o_b)
            states.append(s_ctx)
        else:
            gd = d // FNET_GROUPS
            cc, sc = _dft_tables(gd)
            cs = jnp.concatenate([cc, sc], axis=1)
            w_b = fnet_w[j].astype(BF16)
            for seg in segs:
                ct, st = _dft_tables(seg.seq)
                if pending is None:
                    ac, as_ = _fnet_a(x, mods, layer, seg, g1, cs)
                else:
                    ac, as_, x = _fnet_a(x, mods, layer, seg, g1, cs, pending)
                x = _fnet_b(x, mods, layer, seg, ac, as_, ct, st, w_b, (seg.seq * gd) ** -0.5)
        if layer < depth - 1:
            pending = _moe(x, mods, layer, segs, g2, rw_hi, rw_lo, rb, moe_w_gate, moe_w_up, moe_w_down, None)
        else:
            y_prompt, y_sample = _moe(x, mods, layer, segs, g2, rw_hi, rw_lo, rb, moe_w_gate, moe_w_up,
                                      moe_w_down, final_g)

    new_state = jnp.stack(states, axis=1)
    return (y_prompt.reshape(b_ctx, t_ctx, d), y_sample.reshape(b_lat, t_lat, d), new_cache, new_state)
```

```python
import functools
import math

import jax
import jax.numpy as jnp
import numpy as np
from jax import lax
from jax.experimental import pallas as pl
from jax.experimental.pallas import tpu as pltpu
from jax.experimental.pallas import tpu_sc as plsc

F32 = jnp.float32
BF16 = jnp.bfloat16

GRID_W = 64
MLA_HEADS = 8
MLA_NOPE = 128
MLA_ROPE = 64
MLA_V = 128
MLA_Q_LORA = 384
MLA_KV_LORA = 256
ROPE_BASE = 10000.0
RET_HEADS = 4
RET_CHUNK = 256
FNET_GROUPS = 4
N_EXPERTS = 16
N_EXPERT_GROUPS = 4
EXPERTS_PER_GROUP = 4
D_EXPERT = 512
NORM_EPS = 1e-6

LANE = 128
PROJ_ROWS = 512
ATTN_ROWS = 512
WIDE_TILE = 1024
MOE_ROWS = 1024
MOE_TAIL_ROWS = 256
VMEM_LIMIT = 56 * 1024 * 1024
SC_CORES = 2
SC_SUBCORES = 16
SC_CHUNK = 128


def _cparams(*sem):
    return pltpu.CompilerParams(dimension_semantics=sem, vmem_limit_bytes=VMEM_LIMIT)


def _sigmoid(x):
    return 1.0 / (1.0 + jnp.exp(-x))


def _rms(x):
    return x * lax.rsqrt(jnp.mean(x * x, axis=-1, keepdims=True) + NORM_EPS)


def _modulate(x, g, shift, scale):
    return _rms(x) * (g * (1.0 + scale)) + shift


def _dot(a, b):
    return jnp.dot(a, b, preferred_element_type=F32)


def _dot_nt(a, b):
    return lax.dot_general(a, b, (((1,), (1,)), ((), ())), preferred_element_type=F32)


def _mod_kernel(c_ref, w_ref, b_ref, o_ref):
    c = c_ref[...]
    s = (c * _sigmoid(c)).astype(BF16)
    o_ref[...] = _dot(s, w_ref[...].astype(BF16)) + b_ref[...]


def _modulation_all(cond8, ada_w, ada_b):
    depth, d, d6 = ada_w.shape
    tn = d6 // 4
    return pl.pallas_call(
        _mod_kernel,
        grid=(depth, d6 // tn),
        in_specs=[
            pl.BlockSpec((8, d), lambda l, n: (0, 0)),
            pl.BlockSpec((None, d, tn), lambda l, n: (l, 0, n)),
            pl.BlockSpec((None, 1, tn), lambda l, n: (l, 0, n)),
        ],
        out_specs=pl.BlockSpec((None, 8, tn), lambda l, n: (l, 0, n)),
        out_shape=jax.ShapeDtypeStruct((depth, 8, d6), F32),
        compiler_params=_cparams("parallel", "parallel"),
        name="modulation",
    )(cond8, ada_w, ada_b.reshape(depth, 1, d6))


class _Seg:
    def __init__(self, row0, batch, seq, mod0, per_batch_mod):
        self.row0, self.batch, self.seq = row0, batch, seq
        self.mod0, self.per_batch_mod = mod0, per_batch_mod
        self.rows = batch * seq

    def tile(self, want):
        tm = min(want, self.seq) if self.per_batch_mod else want
        assert self.rows % tm == 0 and self.row0 % tm == 0 and (self.seq % tm == 0 or tm % self.seq == 0)
        return tm

    def seq_tile(self, want):
        tm = min(want, self.seq)
        assert self.seq % tm == 0 and self.row0 % tm == 0
        return tm

    def mod_row(self, tile, tm):
        if self.per_batch_mod:
            return self.mod0 + tile * tm // self.seq
        return self.mod0


def _mod_spec(layer, seg, d, tm, tile_of=lambda *a: a[0]):
    return pl.BlockSpec((None, None, 6, 1, d), lambda *a: (layer, seg.mod_row(tile_of(*a), tm), 0, 0, 0))


def _unpack_halves(p):
    lo = lax.bitcast_convert_type(p << 16, F32)
    hi = lax.bitcast_convert_type(p & jnp.uint32(0xFFFF0000), F32)
    return lo, hi


def _moe_residual(y_ref, w_ref, gate):
    w = w_ref[...]
    lo0, hi0 = _unpack_halves(y_ref[0])
    lo1, hi1 = _unpack_halves(y_ref[1])
    w0, w1 = w[:, 0:1], w[:, 1:2]
    return gate * jnp.concatenate([w0 * lo0 + w1 * lo1, w0 * hi0 + w1 * hi1], axis=-1)


def _take_pending(refs, pending):
    if not pending:
        return refs, lambda x: x
    y_ref, w_ref, pm_ref, *rest = refs
    xo_ref = rest.pop()

    def resolve(x):
        x = x + _moe_residual(y_ref, w_ref, pm_ref[5])
        xo_ref[...] = x
        return x

    return rest, resolve


def _pending_io(pending, seg, tm, d, n):
    yg, wcol, mods, layer = pending
    t0 = seg.row0 // tm
    specs = [pl.BlockSpec((2, tm, d // 2), lambda i: (0, t0 + i, 0)),
             pl.BlockSpec((tm, 8), lambda i: (t0 + i, 0)),
             _mod_spec(layer, seg, d, tm)]
    return specs, [yg, wcol, mods], pl.BlockSpec((tm, d), lambda i: (t0 + i, 0)), jax.ShapeDtypeStruct((n, d), F32)


def _rope_partner(x):
    lane = lax.broadcasted_iota(jnp.int32, x.shape, 1)
    first = (lane % 32) < 16
    return jnp.where(first, pltpu.roll(x, LANE - 16, 1), pltpu.roll(x, 16, 1))


def _store_values(v_ref, v):
    ones = jnp.ones((v.shape[0], LANE), BF16)
    for hd in range(MLA_HEADS):
        v_ref[:, hd * 2 * LANE:hd * 2 * LANE + LANE] = v[:, hd * MLA_V:(hd + 1) * MLA_V].astype(BF16)
        v_ref[:, hd * 2 * LANE + LANE:(hd + 1) * 2 * LANE] = ones


def _mla_proj_kernel(*refs, rope, pending):
    refs, resolve = _take_pending(refs, pending)
    if rope:
        (x_ref, m_ref, g_ref, win_ref, qg_ref, kvg_ref, wq_ref, wkn_ref, wv_ref, cos_ref, sin_ref,
         q_ref, k_ref, v_ref) = refs
    else:
        x_ref, m_ref, g_ref, win_ref, qg_ref, kvg_ref, wq_ref, wkn_ref, wv_ref = refs[:9]
        q_ref, k_ref, v_ref, cache_ref = refs[-4:]
    h = _modulate(resolve(x_ref[...]), g_ref[...], m_ref[0], m_ref[1]).astype(BF16)
    z = _dot(h, win_ref[...])
    cq = z[:, :MLA_Q_LORA]
    ckv = z[:, MLA_Q_LORA:MLA_Q_LORA + MLA_KV_LORA]
    kpe = z[:, MLA_Q_LORA + MLA_KV_LORA:]
    cqn = (_rms(cq) * qg_ref[...]).astype(BF16)
    ckvn = _rms(ckv) * kvg_ref[...]
    ckvb = ckvn.astype(BF16)
    q = _dot(cqn, wq_ref[...])
    kn = _dot(ckvb, wkn_ref[...])
    _store_values(v_ref, _dot(ckvb, wv_ref[...]))
    if rope:
        cos, sin = cos_ref[...], sin_ref[...]
        kpe = kpe * cos + _rope_partner(kpe) * sin
    else:
        seq = cache_ref.shape[1]
        for s in range(cache_ref.shape[0]):
            cache_ref[s, :, :MLA_KV_LORA] = ckvn[s * seq:(s + 1) * seq, :]
            cache_ref[s, :, MLA_KV_LORA:] = kpe[s * seq:(s + 1) * seq, :MLA_ROPE]
    kpe_b = kpe.astype(BF16)
    for hd in range(MLA_HEADS):
        lo = hd * 2 * LANE
        q_ref[:, lo:lo + LANE] = q[:, lo:lo + LANE].astype(BF16)
        qr = q[:, lo + LANE:lo + 2 * LANE]
        if rope:
            qr = qr * cos + _rope_partner(qr) * sin
        q_ref[:, lo + LANE:lo + 2 * LANE] = qr.astype(BF16)
        k_ref[:, lo:lo + LANE] = kn[:, hd * LANE:(hd + 1) * LANE].astype(BF16)
        k_ref[:, lo + LANE:lo + 2 * LANE] = kpe_b


def _mla_proj(x, x_row0, mods, layer, seg, g1, w, rope_tabs, cache_slot=None, pending=None):
    n, d = x.shape
    rope = rope_tabs is not None
    rows = seg.rows
    tm = seg.tile(PROJ_ROWS)
    x_tile0 = x_row0 // tm
    hq = MLA_HEADS * 2 * LANE
    const = lambda i: (0, 0)
    aliases = {}
    in_specs = [
        pl.BlockSpec((tm, d), lambda i: (x_tile0 + i, 0)),
        _mod_spec(layer, seg, d, tm),
        pl.BlockSpec((1, d), const),
        pl.BlockSpec(w["w_in"].shape, const),
        pl.BlockSpec((1, MLA_Q_LORA), const),
        pl.BlockSpec((1, MLA_KV_LORA), const),
        pl.BlockSpec(w["w_q"].shape, const),
        pl.BlockSpec(w["w_kn"].shape, const),
        pl.BlockSpec(w["w_v"].shape, const),
    ]
    args = [x, mods, g1, w["w_in"], w["q_g"], w["kv_g"], w["w_q"], w["w_kn"], w["w_v"]]
    out_specs = [pl.BlockSpec((tm, hq), lambda i: (i, 0))] * 3
    out_shape = [jax.ShapeDtypeStruct((rows, hq), BF16)] * 3
    if rope:
        tab = pl.BlockSpec((tm, LANE), lambda i: (i % (seg.seq // tm), 0))
        in_specs += [tab, tab]
        args += list(rope_tabs)
    else:
        cw = MLA_KV_LORA + MLA_ROPE
        prev, slot, n_slots = cache_slot
        assert tm % seg.seq == 0
        out_specs.append(pl.BlockSpec((tm // seg.seq, None, seg.seq, cw), lambda i: (i, slot, 0, 0)))
        out_shape.append(jax.ShapeDtypeStruct((seg.batch, n_slots, seg.seq, cw), F32))
        in_specs.append(pl.BlockSpec(memory_space=pl.ANY))
        args.append(prev)
        aliases = {len(args) - 1: 3}
    if pending is not None:
        p_specs, p_args, xo_spec, xo_shape = _pending_io(pending, seg, tm, d, n)
        in_specs, args = p_specs + in_specs, p_args + args
        out_specs.append(xo_spec)
        out_shape.append(xo_shape)
        aliases = {k + len(p_args): v for k, v in aliases.items()}
        aliases[len(p_args)] = len(out_shape) - 1
    return pl.pallas_call(
        functools.partial(_mla_proj_kernel, rope=rope, pending=pending is not None),
        grid=(rows // tm,),
        in_specs=in_specs,
        out_specs=out_specs,
        out_shape=out_shape,
        input_output_aliases=aliases,
        compiler_params=_cparams("parallel"),
        name="mla_proj_lat" if rope else "mla_proj_ctx",
    )(*args)


def _cache_kv_kernel(c_ref, wkn_ref, wv_ref, k_ref, v_ref):
    c = c_ref[...]
    ckv = c[:, :MLA_KV_LORA].astype(BF16)
    kpe_b = c[:, MLA_KV_LORA:].astype(BF16)
    kn = _dot(ckv, wkn_ref[...])
    _store_values(v_ref, _dot(ckv, wv_ref[...]))
    for hd in range(MLA_HEADS):
        lo = hd * 2 * LANE
        k_ref[:, lo:lo + LANE] = kn[:, hd * LANE:(hd + 1) * LANE].astype(BF16)
        k_ref[:, lo + LANE:lo + 2 * LANE] = kpe_b


def _cache_kv(cache_pad, w):
    rows, cw = cache_pad.shape
    hq = MLA_HEADS * 2 * LANE
    const = lambda i: (0, 0)
    tm = min(PROJ_ROWS, rows)
    assert rows % tm == 0
    return pl.pallas_call(
        _cache_kv_kernel,
        grid=(rows // tm,),
        in_specs=[
            pl.BlockSpec((tm, cw), lambda i: (i, 0)),
            pl.BlockSpec(w["w_kn"].shape, const),
            pl.BlockSpec(w["w_v"].shape, const),
        ],
        out_specs=[pl.BlockSpec((tm, hq), lambda i: (i, 0))] * 2,
        out_shape=[jax.ShapeDtypeStruct((rows, hq), BF16)] * 2,
        compiler_params=_cparams("parallel"),
        name="mla_cache_kv",
    )(cache_pad, w["w_kn"], w["w_v"])


def _attn_kernel(*refs, n_parts):
    q_ref = refs[0]
    kv_refs = refs[1:1 + 2 * n_parts]
    wo_ref, x_ref, m_ref = refs[1 + 2 * n_parts:4 + 2 * n_parts]
    o_ref, acc_ref = refs[-2:]
    for hd in range(MLA_HEADS):
        qh = q_ref[:, hd * 2 * LANE:(hd + 1) * 2 * LANE]
        scores = [_dot_nt(qh, kv_refs[2 * p][:, hd * 2 * LANE:(hd + 1) * 2 * LANE]) for p in range(n_parts)]
        mx = scores[0].max(axis=-1, keepdims=True)
        for s in scores[1:]:
            mx = jnp.maximum(mx, s.max(axis=-1, keepdims=True))
        out = None
        for p, s in enumerate(scores):
            e = jnp.exp((s - mx).astype(BF16))
            pv = _dot(e, kv_refs[2 * p + 1][:, hd * 2 * LANE:(hd + 1) * 2 * LANE])
            out = pv if out is None else out + pv
        acc_ref[:, hd * MLA_V:(hd + 1) * MLA_V] = (out[:, :MLA_V] / out[:, LANE:LANE + MLA_V]).astype(BF16)
    y = _dot(acc_ref[...], wo_ref[...])
    o_ref[...] = x_ref[...] + m_ref[2] * y


def _attention(x, x_row0, n, dest, mods, layer, seg, q, kv_parts, w_o):
    d = x.shape[1]
    hq = MLA_HEADS * 2 * LANE
    hv = MLA_HEADS * MLA_V
    tq = seg.seq_tile(ATTN_ROWS)
    tps = seg.seq // tq
    x_tile0, out_tile0 = x_row0 // tq, seg.row0 // tq
    in_specs = [pl.BlockSpec((tq, hq), lambda b, i: (b * tps + i, 0))]
    args = [q]
    for k, v, rows in kv_parts:
        mode = dict(pipeline_mode=pl.Buffered(1)) if tps > 1 else {}
        in_specs += [pl.BlockSpec((rows, hq), lambda b, i: (b, 0), **mode)] * 2
        args += [k, v]
    in_specs += [
        pl.BlockSpec(w_o.shape, lambda b, i: (0, 0)),
        pl.BlockSpec((tq, d), lambda b, i: (x_tile0 + b * tps + i, 0)),
        _mod_spec(layer, seg, d, tq, tile_of=lambda b, i: b * tps + i),
    ]
    args += [w_o, x, mods]
    if isinstance(dest, str):
        assert dest == "inplace"
        aliases = {len(args) - 2: 0}
    else:
        in_specs.append(pl.BlockSpec(memory_space=pl.ANY))
        args.append(dest)
        aliases = {len(args) - 1: 0}
    return pl.pallas_call(
        functools.partial(_attn_kernel, n_parts=len(kv_parts)),
        grid=(seg.batch, tps),
        in_specs=in_specs,
        out_specs=pl.BlockSpec((tq, d), lambda b, i: (out_tile0 + b * tps + i, 0)),
        out_shape=jax.ShapeDtypeStruct((n, d), F32),
        scratch_shapes=[pltpu.VMEM((tq, hv), BF16)],
        input_output_aliases=aliases,
        compiler_params=_cparams("parallel", "arbitrary"),
        name="mla_attention",
    )(*args)


RET_COL = 1024


RET_ROWS = 1024


def _ret_proj_kernel(*refs, kinds, dk, pending):
    refs, resolve = _take_pending(refs, pending)
    rotary = "rotary" in kinds
    if rotary:
        x_ref, m_ref, g_ref, w_ref, cos_ref, sin_ref, z_ref = refs
        cos, sin = cos_ref[...], sin_ref[...]
    else:
        x_ref, m_ref, g_ref, w_ref, z_ref = refs
    h = _modulate(resolve(x_ref[...]), g_ref[...], m_ref[0], m_ref[1]).astype(BF16)
    half = dk // 2
    for j, kind in enumerate(kinds):
        c0 = j * RET_COL
        acc = _dot(h, w_ref[:, c0:c0 + RET_COL])
        if kind == "rotary":
            for hd in range(RET_COL // dk):
                lo = hd * dk
                x1, x2 = acc[:, lo:lo + half], acc[:, lo + half:lo + dk]
                z_ref[:, c0 + lo:c0 + lo + half] = (x1 * cos - x2 * sin).astype(BF16)
                z_ref[:, c0 + lo + half:c0 + lo + dk] = (x1 * sin + x2 * cos).astype(BF16)
        elif kind == "silu":
            z_ref[:, c0:c0 + RET_COL] = (acc * _sigmoid(acc)).astype(BF16)
        else:
            z_ref[:, c0:c0 + RET_COL] = acc.astype(BF16)


def _ret_proj(x, mods, layer, seg, g1, w_in, group, kinds, rot_tabs, dk, pending=None):
    n, d = x.shape
    rows = seg.rows
    tm = seg.tile(RET_ROWS)
    ncol = len(kinds) * RET_COL
    tps = max(seg.seq // tm, 1)
    in_specs = [
        pl.BlockSpec((tm, d), lambda i: (seg.row0 // tm + i, 0)),
        _mod_spec(layer, seg, d, tm),
        pl.BlockSpec((1, d), lambda i: (0, 0)),
        pl.BlockSpec((d, ncol), lambda i: (0, group), pipeline_mode=pl.Buffered(1)),
    ]
    args = [x, mods, g1, w_in]
    if "rotary" in kinds:
        tab = pl.BlockSpec((tm, dk // 2), lambda i: (i % tps, 0))
        in_specs += [tab, tab]
        args += list(rot_tabs)
    out_specs = [pl.BlockSpec((tm, ncol), lambda i: (i, 0))]
    out_shape = [jax.ShapeDtypeStruct((rows, ncol), BF16)]
    aliases = {}
    if pending is not None:
        p_specs, p_args, xo_spec, xo_shape = _pending_io(pending, seg, tm, d, n)
        in_specs, args = p_specs + in_specs, p_args + args
        out_specs.append(xo_spec)
        out_shape.append(xo_shape)
        aliases = {len(p_args): 1}
    res = pl.pallas_call(
        functools.partial(_ret_proj_kernel, kinds=kinds, dk=dk, pending=pending is not None),
        grid=(rows // tm,),
        in_specs=in_specs,
        out_specs=out_specs,
        out_shape=out_shape,
        input_output_aliases=aliases,
        compiler_params=_cparams("parallel"),
        name="ret_proj_" + kinds[0],
    )(*args)
    return res if pending is not None else res[0]


def _log_sigmoid(x):
    return jnp.minimum(x, 0.0) - jnp.log(1.0 + jnp.exp(-jnp.abs(x)))


def _ret_scan_kernel(*refs, has_s0, emit_state, n_chunks, heads):
    refs = list(refs)
    lf_ref, lb_ref, q_ref, k_ref, v_ref, gf_ref, gb_ref = refs[:7]
    pos = 7
    s0_ref = None
    if has_s0:
        s0_ref = refs[pos]
        pos += 1
    y_ref = refs[pos]
    pos += 1
    sout_ref = None
    if emit_state:
        sout_ref = refs[pos]
        pos += 1
    s_ref, yf_ref = refs[pos:]
    c = RET_CHUNK
    dk, dv = s_ref.shape
    ii = lax.broadcasted_iota(jnp.int32, (c, c), 0).astype(F32)
    jj = lax.broadcasted_iota(jnp.int32, (c, c), 1).astype(F32)
    idx = lax.broadcasted_iota(jnp.int32, (c, 1), 0).astype(F32)

    for hd, direction in [(hd, direction) for hd in range(heads) for direction in range(2)]:
        fwd = direction == 0
        kcol, vcol = slice(hd * dk, (hd + 1) * dk), slice(hd * dv, (hd + 1) * dv)
        lg = _log_sigmoid((lf_ref if fwd else lb_ref)[hd])
        rel = (ii - jj) if fwd else (jj - ii)
        keep = rel >= 0
        decay_in = jnp.where(keep, jnp.exp(jnp.where(keep, rel, 0.0) * lg), 0.0)
        decay_q = jnp.exp(((idx + 1.0) if fwd else (c - idx)) * lg)
        decay_k = jnp.exp(((c - 1.0 - idx) if fwd else idx) * lg)
        decay_c = jnp.exp(c * lg)
        g_ref = gf_ref if fwd else gb_ref

        def chunk(cc, state, fwd=fwd, decay_in=decay_in, decay_q=decay_q, decay_k=decay_k, decay_c=decay_c,
                  g_ref=g_ref, kcol=kcol, vcol=vcol):
            r0 = cc * c if isinstance(cc, int) else pl.multiple_of(cc * c, c)
            qc = q_ref[pl.ds(r0, c), kcol]
            kc = k_ref[pl.ds(r0, c), kcol]
            vc = v_ref[pl.ds(r0, c), vcol]
            sc = _dot_nt(qc, kc) * decay_in
            out = _dot(sc.astype(BF16), vc)
            kd_t = (kc.astype(F32) * decay_k).T.astype(BF16)
            new_s = _dot(kd_t, vc)
            if state is not None:
                out = out + decay_q * _dot(qc, state.astype(BF16))
                new_s = decay_c * state + new_s
            s_ref[...] = new_s
            o = _rms(out) * g_ref[pl.ds(r0, c), vcol].astype(F32)
            if fwd:
                yf_ref[pl.ds(r0, c), :] = o
            else:
                y_ref[pl.ds(r0, c), vcol] = (yf_ref[pl.ds(r0, c), :] + o).astype(BF16)

        chunk(0 if fwd else n_chunks - 1, s0_ref[direction, hd] if has_s0 else None)

        def step(ci, carry, fwd=fwd, chunk=chunk):
            chunk(ci if fwd else n_chunks - 1 - ci, s_ref[...])
            return carry

        lax.fori_loop(1, n_chunks, step, 0, unroll=True)
        if emit_state:
            sout_ref[direction, hd] = s_ref[...]


def _ret_scan(qkv, g, seg, logit_f, logit_b, s0, emit_state, dk, dv, heads):
    rows = seg.batch * seg.seq
    t = seg.seq
    hh = RET_HEADS
    groups = hh // heads
    v0 = 2 * hh * dk // (heads * dv)
    assert groups * heads == hh and v0 * heads * dv == 2 * hh * dk
    in_specs = [
        pl.BlockSpec((heads, 1, 1), lambda b, h: (h, 0, 0)),
        pl.BlockSpec((heads, 1, 1), lambda b, h: (h, 0, 0)),
        pl.BlockSpec((t, heads * dk), lambda b, h: (b, h)),
        pl.BlockSpec((t, heads * dk), lambda b, h: (b, groups + h)),
        pl.BlockSpec((t, heads * dv), lambda b, h: (b, v0 + h)),
        pl.BlockSpec((t, heads * dv), lambda b, h: (b, h)),
        pl.BlockSpec((t, heads * dv), lambda b, h: (b, groups + h)),
    ]
    args = [logit_f.reshape(hh, 1, 1), logit_b.reshape(hh, 1, 1), qkv, qkv, qkv, g, g]
    state_spec = pl.BlockSpec((None, 2, heads, dk, dv), lambda b, h: (b, 0, h, 0, 0))
    if s0 is not None:
        in_specs.append(state_spec)
        args.append(s0)
    out_specs = [pl.BlockSpec((t, heads * dv), lambda b, h: (b, h))]
    out_shape = [jax.ShapeDtypeStruct((rows, hh * dv), BF16)]
    if emit_state:
        out_specs.append(state_spec)
        out_shape.append(jax.ShapeDtypeStruct((seg.batch, 2, hh, dk, dv), F32))
    res = pl.pallas_call(
        functools.partial(_ret_scan_kernel, has_s0=s0 is not None, emit_state=emit_state,
                          n_chunks=t // RET_CHUNK, heads=heads),
        grid=(seg.batch, groups),
        in_specs=in_specs,
        out_specs=out_specs,
        out_shape=out_shape,
        scratch_shapes=[pltpu.VMEM((dk, dv), F32), pltpu.VMEM((t, dv), F32)],
        compiler_params=_cparams("parallel", "parallel"),
        name="ret_scan",
    )(*args)
    return res if emit_state else (res[0], None)


def _mm_res_kernel(a_ref, w_ref, x_ref, m_ref, o_ref):
    o_ref[...] = x_ref[...] + m_ref[2] * _dot(a_ref[...], w_ref[...])


def _matmul_residual(x, mods, layer, seg, a, w):
    n, d = x.shape
    tm = seg.tile(RET_ROWS)
    x_spec = pl.BlockSpec((tm, d), lambda i: (seg.row0 // tm + i, 0))
    return pl.pallas_call(
        _mm_res_kernel,
        grid=(seg.rows // tm,),
        in_specs=[
            pl.BlockSpec((tm, a.shape[1]), lambda i: (i, 0)),
            pl.BlockSpec(w.shape, lambda i: (0, 0)),
            x_spec,
            _mod_spec(layer, seg, d, tm),
        ],
        out_specs=x_spec,
        out_shape=jax.ShapeDtypeStruct((n, d), F32),
        input_output_aliases={2: 0},
        compiler_params=_cparams("parallel"),
        name="matmul_residual",
    )(a, w, x, mods)


def _fnet_a_kernel(*refs, gd, pending):
    refs, resolve = _take_pending(refs, pending)
    x_ref, m_ref, g_ref, cs_ref, ac_ref, as_ref = refs
    h = _modulate(resolve(x_ref[...]), g_ref[...], m_ref[0], m_ref[1]).astype(BF16)
    cs = cs_ref[...]
    for g in range(FNET_GROUPS):
        a = _dot(h[:, g * gd:(g + 1) * gd], cs)
        ac_ref[:, g * gd:(g + 1) * gd] = a[:, :gd].astype(BF16)
        as_ref[:, g * gd:(g + 1) * gd] = a[:, gd:].astype(BF16)


def _fnet_a(x, mods, layer, seg, g1, cs, pending=None):
    n, d = x.shape
    rows = seg.rows
    tm = seg.tile(PROJ_ROWS)
    out = pl.BlockSpec((tm, d), lambda i: (i, 0))
    in_specs = [
        pl.BlockSpec((tm, d), lambda i: (seg.row0 // tm + i, 0)),
        _mod_spec(layer, seg, d, tm),
        pl.BlockSpec((1, d), lambda i: (0, 0)),
        pl.BlockSpec(cs.shape, lambda i: (0, 0)),
    ]
    args = [x, mods, g1, cs]
    out_specs, out_shape, aliases = [out, out], [jax.ShapeDtypeStruct((rows, d), BF16)] * 2, {}
    if pending is not None:
        p_specs, p_args, xo_spec, xo_shape = _pending_io(pending, seg, tm, d, n)
        in_specs, args = p_specs + in_specs, p_args + args
        out_specs.append(xo_spec)
        out_shape.append(xo_shape)
        aliases = {len(p_args): 2}
    return pl.pallas_call(
        functools.partial(_fnet_a_kernel, gd=d // FNET_GROUPS, pending=pending is not None),
        grid=(rows // tm,),
        in_specs=in_specs,
        out_specs=out_specs,
        out_shape=out_shape,
        input_output_aliases=aliases,
        compiler_params=_cparams("parallel"),
        name="fnet_channel_dft",
    )(*args)


def _fnet_b_kernel(ct_ref, st_ref, ac_ref, as_ref, w_ref, x_ref, m_ref, o_ref, *, norm):
    f = (_dot(ct_ref[...], ac_ref[...]) - _dot(st_ref[...], as_ref[...])) * norm
    o_ref[...] = x_ref[...] + m_ref[2] * _dot(f.astype(BF16), w_ref[...])


def _fnet_b(x, mods, layer, seg, ac, as_, ct, st, w, norm):
    n, d = x.shape
    t = seg.seq
    tq = seg.seq_tile(ATTN_ROWS)
    tps = t // tq
    x_spec = pl.BlockSpec((tq, d), lambda b, i: (seg.row0 // tq + b * tps + i, 0))
    tab = pl.BlockSpec((tq, t), lambda b, i: (i, 0))
    seq = pl.BlockSpec((t, d), lambda b, i: (b, 0))
    return pl.pallas_call(
        functools.partial(_fnet_b_kernel, norm=norm),
        grid=(seg.batch, tps),
        in_specs=[tab, tab, seq, seq, pl.BlockSpec(w.shape, lambda b, i: (0, 0)), x_spec,
                  _mod_spec(layer, seg, d, tq, tile_of=lambda b, i: b * tps + i)],
        out_specs=x_spec,
        out_shape=jax.ShapeDtypeStruct((n, d), F32),
        input_output_aliases={5: 0},
        compiler_params=_cparams("parallel", "arbitrary"),
        name="fnet_position_dft",
    )(ct, st, ac, as_, w, x, mods)


def _pack_halves(a):
    w = a.shape[1] // 2
    bits = lambda v: lax.bitcast_convert_type(v.astype(BF16).astype(F32), jnp.uint32)
    return (bits(a[:, :w]) >> 16) | (bits(a[:, w:]) & jnp.uint32(0xFFFF0000))


def _router_kernel(x_ref, m_ref, g_ref, rwhi_ref, rwlo_ref, rb_ref, h_ref, idx_ref, rank_ref, wcol_ref, cnt_ref,
                   run_ref, tri_ref):
    step = pl.program_id(0)

    @pl.when(step == 0)
    def _():
        run_ref[...] = jnp.zeros_like(run_ref)
        tt = tri_ref.shape[0]
        earlier = lax.broadcasted_iota(jnp.int32, (tt, tt), 0) < lax.broadcasted_iota(jnp.int32, (tt, tt), 1)
        tri_ref[...] = jnp.where(earlier, 1.0, 0.0).astype(BF16)

    h = _modulate(x_ref[...], g_ref[...], m_ref[3], m_ref[4])
    h_ref[...] = _pack_halves(h)
    h_hi = h.astype(BF16)
    h_lo = (h - h_hi.astype(F32)).astype(BF16)
    logits = _dot_nt(rwhi_ref[...], h_hi) + (_dot_nt(rwhi_ref[...], h_lo) + _dot_nt(rwlo_ref[...], h_hi))
    sc = _sigmoid(logits)
    gr = sc + rb_ref[...]
    gp = EXPERTS_PER_GROUP
    row = lambda a, e: a[e:e + 1, :]
    best_g = None
    for g in range(N_EXPERT_GROUPS):
        vals = [row(gr, g * gp + i) for i in range(gp)]
        gs = None
        for i in range(gp):
            for j in range(i + 1, gp):
                pair = vals[i] + vals[j]
                gs = pair if gs is None else jnp.maximum(gs, pair)
        if best_g is None:
            best_g, best_v = jnp.zeros(gs.shape, jnp.int32), gs
        else:
            better = gs > best_v
            best_g = jnp.where(better, g, best_g)
            best_v = jnp.where(better, gs, best_v)
    sel, raw = [], []
    for i in range(gp):
        s_i, r_i = row(gr, i), row(sc, i)
        for g in range(1, N_EXPERT_GROUPS):
            s_i = jnp.where(best_g == g, row(gr, g * gp + i), s_i)
            r_i = jnp.where(best_g == g, row(sc, g * gp + i), r_i)
        sel.append(s_i)
        raw.append(r_i)

    def argmax_first(vals, raws):
        bi, bv, br = jnp.zeros(vals[0].shape, jnp.int32), vals[0], raws[0]
        for i in range(1, len(vals)):
            better = vals[i] > bv
            bi = jnp.where(better, i, bi)
            bv = jnp.where(better, vals[i], bv)
            br = jnp.where(better, raws[i], br)
        return bi, br

    i1, w1 = argmax_first(sel, raw)
    masked = [jnp.where(i1 == i, -jnp.inf, sel[i]) for i in range(gp)]
    i2, w2 = argmax_first(masked, raw)
    tot = w1 + w2
    e1 = best_g * gp + i1
    e2 = best_g * gp + i2
    idx_ref[0:1, :] = e1
    idx_ref[1:2, :] = e2
    t = e1.shape[1]
    sub = lax.broadcasted_iota(jnp.int32, (8, t), 0)
    w8 = jnp.where(sub == 0, w1 / tot, jnp.where(sub == 1, w2 / tot, 0.0))
    wcol_ref[...] = w8.T
    eio = lax.broadcasted_iota(jnp.int32, (N_EXPERTS, t), 0)
    oh1, oh2 = eio == e1, eio == e2
    oh = jnp.where(oh1, 1.0, jnp.where(oh2, 1.0, 0.0))
    local = _dot(oh.astype(BF16), tri_ref[...])
    rank = local + run_ref[:, 0:1]
    rank_ref[0:1, :] = jnp.sum(jnp.where(oh1, rank, 0.0), axis=0, keepdims=True).astype(jnp.int32)
    rank_ref[1:2, :] = jnp.sum(jnp.where(oh2, rank, 0.0), axis=0, keepdims=True).astype(jnp.int32)
    run_ref[...] = run_ref[...] + jnp.sum(oh, axis=1, keepdims=True)
    cnt_ref[...] = run_ref[...]


def _wide_mod_row(segs, tm):
    ctx, lat = segs
    ctx_tiles = ctx.batch * ctx.seq // tm
    assert ctx_tiles * tm == ctx.batch * ctx.seq and lat.seq % tm == 0
    return lambda i: jnp.where(i < ctx_tiles, ctx.mod0, lat.mod0 + (i - ctx_tiles) // (lat.seq // tm))


def _router(x, mods, layer, segs, g2, rw_hi, rw_lo, rb):
    n, d = x.shape
    tm = WIDE_TILE
    mod_row = _wide_mod_row(segs, tm)
    return pl.pallas_call(
        _router_kernel,
        grid=(n // tm,),
        in_specs=[
            pl.BlockSpec((tm, d), lambda i: (i, 0)),
            pl.BlockSpec((None, None, 6, 1, d), lambda i: (layer, mod_row(i), 0, 0, 0)),
            pl.BlockSpec((1, d), lambda i: (0, 0)),
            pl.BlockSpec(rw_hi.shape, lambda i: (0, 0)),
            pl.BlockSpec(rw_lo.shape, lambda i: (0, 0)),
            pl.BlockSpec(rb.shape, lambda i: (0, 0)),
        ],
        out_specs=[
            pl.BlockSpec((tm, d // 2), lambda i: (i, 0)),
            pl.BlockSpec((2, tm), lambda i: (0, i)),
            pl.BlockSpec((2, tm), lambda i: (0, i)),
            pl.BlockSpec((tm, 8), lambda i: (i, 0)),
            pl.BlockSpec((N_EXPERTS, LANE), lambda i: (0, 0)),
        ],
        out_shape=[
            jax.ShapeDtypeStruct((n, d // 2), jnp.uint32),
            jax.ShapeDtypeStruct((2, n), jnp.int32),
            jax.ShapeDtypeStruct((2, n), jnp.int32),
            jax.ShapeDtypeStruct((n, 8), F32),
            jax.ShapeDtypeStruct((N_EXPERTS, LANE), F32),
        ],
        scratch_shapes=[pltpu.VMEM((N_EXPERTS, LANE), F32), pltpu.VMEM((tm, tm), BF16)],
        compiler_params=_cparams("arbitrary"),
        name="moe_router",
    )(x, mods, g2, rw_hi, rw_lo, rb)


def _expert_kernel(be_ref, br_ref, bs_ref, xs_ref, wg_ref, wu_ref, wd_ref, y_ref, wg_b, wu_b, wd_b):
    i = pl.program_id(0)
    prev = be_ref[jnp.maximum(i - 1, 0)]
    valid = br_ref[i]
    tail_rows = MOE_TAIL_ROWS

    @pl.when(jnp.logical_or(i == 0, be_ref[i] != prev))
    def _():
        wg_b[...] = wg_ref[...].astype(BF16)
        wu_b[...] = wu_ref[...].astype(BF16)
        wd_b[...] = wd_ref[...].astype(BF16)

    def ffn(r0, nrows):
        lo, hi = _unpack_halves(xs_ref[r0:r0 + nrows, :])
        xb = jnp.concatenate([lo.astype(BF16), hi.astype(BF16)], axis=1)
        gate = _dot(xb, wg_b[...])
        hid = (gate * _sigmoid(gate)) * _dot(xb, wu_b[...])
        y_ref[r0:r0 + nrows, :] = _pack_halves(_dot(hid.astype(BF16), wd_b[...]))

    @pl.when(valid > tail_rows)
    def _():
        ffn(0, y_ref.shape[0])

    @pl.when(jnp.logical_and(valid > 0, valid <= tail_rows))
    def _():
        ffn(0, tail_rows)
        y_ref[tail_rows:, :] = jnp.zeros((y_ref.shape[0] - tail_rows, y_ref.shape[1]), y_ref.dtype)


def _experts(xs, block_e, block_rows, block_src, w_gate, w_up, w_down, layer):
    rows, half = xs.shape
    d = 2 * half
    de = w_gate.shape[-1]
    n_blocks = rows // MOE_ROWS
    grid_spec = pltpu.PrefetchScalarGridSpec(
        num_scalar_prefetch=3,
        grid=(n_blocks,),
        in_specs=[
            pl.BlockSpec((MOE_ROWS, half), lambda i, be, br, bs: (bs[i], 0)),
            pl.BlockSpec((None, None, d, de), lambda i, be, br, bs: (layer, be[i], 0, 0)),
            pl.BlockSpec((None, None, d, de), lambda i, be, br, bs: (layer, be[i], 0, 0)),
            pl.BlockSpec((None, None, de, d), lambda i, be, br, bs: (layer, be[i], 0, 0)),
        ],
        out_specs=pl.BlockSpec((MOE_ROWS, half), lambda i, be, br, bs: (bs[i], 0)),
        scratch_shapes=[pltpu.VMEM((d, de), BF16), pltpu.VMEM((d, de), BF16), pltpu.VMEM((de, d), BF16)],
    )
    return pl.pallas_call(
        _expert_kernel,
        grid_spec=grid_spec,
        out_shape=jax.ShapeDtypeStruct((rows, half), jnp.uint32),
        compiler_params=_cparams("arbitrary"),
        name="moe_experts",
    )(block_e, block_rows, block_src, xs, w_gate, w_up, w_down)


def _combine_kernel(x_ref, m_ref, y_ref, w_ref, fg_ref, o_ref):
    x = x_ref[...] + _moe_residual(y_ref, w_ref, m_ref[5])
    o_ref[...] = _rms(x) * fg_ref[...]


def _combine(x, mods, layer, segs, seg, yg, wcol, final_g):
    n, d = x.shape
    tm = WIDE_TILE
    mod_row = _wide_mod_row(segs, tm)
    t0 = seg.row0 // tm
    steps = seg.rows // tm
    in_specs = [pl.BlockSpec((tm, d), lambda i: (t0 + i, 0)),
                pl.BlockSpec((None, None, 6, 1, d), lambda i: (layer, mod_row(t0 + i), 0, 0, 0)),
                pl.BlockSpec((2, tm, d // 2), lambda i: (0, t0 + i, 0)),
                pl.BlockSpec((tm, 8), lambda i: (t0 + i, 0)),
                pl.BlockSpec((1, d), lambda i: (0, 0))]
    return pl.pallas_call(
        _combine_kernel,
        grid=(steps,),
        in_specs=in_specs,
        out_specs=pl.BlockSpec((tm, d), lambda i: (i, 0)),
        out_shape=jax.ShapeDtypeStruct((seg.rows, d), F32),
        compiler_params=_cparams("parallel"),
        name="moe_combine_final",
    )(x, mods, yg, wcol, final_g)


def _sc_mesh():
    return plsc.VectorSubcoreMesh(core_axis_name="c", subcore_axis_name="s")


def _sc_worker_split(n):
    workers = SC_CORES * SC_SUBCORES
    per = n // workers
    assert per * workers == n and per % SC_CHUNK == 0
    return workers, per, per // SC_CHUNK


def _sc_dispatch(h, pos, rows):
    n, w = h.shape
    workers, per, chunks = _sc_worker_split(n)

    @functools.partial(
        pl.kernel, out_type=jax.ShapeDtypeStruct((rows, w), h.dtype), mesh=_sc_mesh(),
        scratch_types=[pltpu.VMEM((2, chunks, SC_CHUNK), jnp.int32), pltpu.VMEM((SC_CHUNK, w), h.dtype)],
        name="moe_dispatch_scatter")
    def scatter_rows(h_hbm, pos_hbm, xs_hbm, idx_v, rows_v):
        wid = lax.axis_index("s") * SC_CORES + lax.axis_index("c")
        pltpu.sync_copy(pos_hbm.at[0, wid], idx_v.at[0])
        pltpu.sync_copy(pos_hbm.at[1, wid], idx_v.at[1])

        @pl.loop(0, chunks)
        def _(c):
            pltpu.sync_copy(h_hbm.at[pl.ds(wid * per + c * SC_CHUNK, SC_CHUNK)], rows_v)
            pltpu.sync_copy(rows_v, xs_hbm.at[idx_v.at[0, c]])
            pltpu.sync_copy(rows_v, xs_hbm.at[idx_v.at[1, c]])

    return scatter_rows(h, pos.reshape(2, workers, chunks, SC_CHUNK))


def _sc_gather2(ys, pos):
    _, w = ys.shape
    n = pos.shape[1]
    workers, per, chunks = _sc_worker_split(n)

    @functools.partial(
        pl.kernel, out_type=jax.ShapeDtypeStruct((2, n, w), ys.dtype), mesh=_sc_mesh(),
        scratch_types=[pltpu.VMEM((2, chunks, SC_CHUNK), jnp.int32), pltpu.VMEM((SC_CHUNK, w), ys.dtype),
                       pltpu.SemaphoreType.DMA],
        name="moe_combine_gather")
    def gather_rows(ys_hbm, pos_hbm, out_hbm, idx_v, rows_v, sem):
        wid = lax.axis_index("s") * SC_CORES + lax.axis_index("c")
        pltpu.sync_copy(pos_hbm.at[0, wid], idx_v.at[0])
        pltpu.sync_copy(pos_hbm.at[1, wid], idx_v.at[1])

        @pl.loop(0, chunks)
        def _(c):
            for k in range(2):
                pltpu.async_copy(ys_hbm.at[idx_v.at[k, c]], rows_v, sem).wait()
                pltpu.sync_copy(rows_v, out_hbm.at[k, pl.ds(wid * per + c * SC_CHUNK, SC_CHUNK)])

    return gather_rows(ys, pos.reshape(2, workers, chunks, SC_CHUNK))


def _dispatch_plan(idx, rank, counts):
    n = idx.shape[1]
    padded = (counts + MOE_ROWS - 1) // MOE_ROWS * MOE_ROWS
    pad_end = jnp.cumsum(padded)
    pad_start = pad_end - padded
    experts = jnp.arange(N_EXPERTS, dtype=jnp.int32)
    start_of = jnp.sum(jnp.where(idx[..., None] == experts, pad_start, 0), axis=-1)
    pos = start_of + rank
    n_blocks = 2 * n // MOE_ROWS + N_EXPERTS
    starts = jnp.arange(n_blocks, dtype=jnp.int32) * MOE_ROWS
    block_e = jnp.minimum(jnp.sum(starts[:, None] >= pad_end[None, :], axis=1), N_EXPERTS - 1).astype(jnp.int32)
    seg_end = jnp.sum(jnp.where(block_e[:, None] == experts, pad_start + counts, 0), axis=-1)
    block_rows = jnp.clip(seg_end - starts, 0, MOE_ROWS).astype(jnp.int32)
    last_used = pad_end[-1] // MOE_ROWS - 1
    block_src = jnp.minimum(jnp.arange(n_blocks, dtype=jnp.int32), last_used).astype(jnp.int32)
    block_e = jnp.sum(jnp.where(block_src[:, None] == jnp.arange(n_blocks)[None, :], block_e[None, :], 0), axis=1)
    return pos, block_e.astype(jnp.int32), block_rows, block_src, n_blocks * MOE_ROWS


def _moe(x, mods, layer, segs, g2, rw_hi, rw_lo, rb, w_gate, w_up, w_down, final_g):
    h2p, idx, rank, wcol, cnt = _router(x, mods, layer, segs, g2, rw_hi, rw_lo, rb)
    pos, block_e, block_rows, block_src, rows = _dispatch_plan(idx, rank, cnt[:, 0].astype(jnp.int32))
    xs = _sc_dispatch(h2p, pos, rows)
    ys = _experts(xs, block_e, block_rows, block_src, w_gate, w_up, w_down, layer)
    yg = _sc_gather2(ys, pos)
    if final_g is None:
        return yg, wcol, mods, layer
    return tuple(_combine(x, mods, layer, segs, seg, yg, wcol, final_g) for seg in segs)


def _mla_rope_tables(t):
    axis_dim = MLA_ROPE // 2
    row = np.repeat(np.arange(t // GRID_W), GRID_W).astype(np.float64)
    col = np.tile(np.arange(GRID_W), t // GRID_W).astype(np.float64)
    inv = ROPE_BASE ** (-np.arange(0, axis_dim, 2, dtype=np.float64) / axis_dim)
    ar, ac = row[:, None] * inv[None, :], col[:, None] * inv[None, :]
    ones = np.ones((t, LANE - MLA_ROPE))
    cos = np.concatenate([np.cos(ar), np.cos(ar), np.cos(ac), np.cos(ac), ones], axis=-1)
    sin = np.concatenate([-np.sin(ar), np.sin(ar), -np.sin(ac), np.sin(ac), 0.0 * ones], axis=-1)
    return jnp.asarray(cos, F32), jnp.asarray(sin, F32)


def _ret_rot_tables(t, dk):
    inv = ROPE_BASE ** (-np.linspace(0.0, 1.0, dk // 2))
    ang = np.arange(t, dtype=np.float64)[:, None] * inv[None, :]
    return jnp.asarray(np.cos(ang), F32), jnp.asarray(np.sin(ang), F32)


def _dft_tables(n):
    k = np.arange(n, dtype=np.int64)
    ang = (np.outer(k, k) % n).astype(np.float64) * (2.0 * math.pi / n)
    return jnp.asarray(np.cos(ang), BF16), jnp.asarray(np.sin(ang), BF16)


def _mla_weights(w_in, q_g, kv_g, w_uq, w_ukv, w_o):
    d = w_in.shape[0]
    hd = MLA_NOPE + MLA_ROPE
    w_in_p = jnp.concatenate([w_in, jnp.zeros((d, LANE - MLA_ROPE), w_in.dtype)], axis=1)
    uq = w_uq.reshape(MLA_Q_LORA, MLA_HEADS, hd)
    uq = jnp.concatenate([uq, jnp.zeros((MLA_Q_LORA, MLA_HEADS, 2 * LANE - hd), uq.dtype)], axis=-1)
    ukv = w_ukv.reshape(MLA_KV_LORA, MLA_HEADS, MLA_NOPE + MLA_V)
    return {
        "w_in": w_in_p.astype(BF16),
        "q_g": q_g.reshape(1, -1) * (MLA_NOPE + MLA_ROPE) ** -0.5,
        "kv_g": kv_g.reshape(1, -1),
        "w_q": uq.reshape(MLA_Q_LORA, MLA_HEADS * 2 * LANE).astype(BF16),
        "w_kn": ukv[..., :MLA_NOPE].reshape(MLA_KV_LORA, MLA_HEADS * MLA_NOPE).astype(BF16),
        "w_v": ukv[..., MLA_NOPE:].reshape(MLA_KV_LORA, MLA_HEADS * MLA_V).astype(BF16),
        "w_o": w_o.astype(BF16),
    }


def kernel(x_prompt, x_sample, cache_mla, state_ret, c, c_ctx, norm1_g, norm2_g, ada_w, ada_b, final_norm_g,
           mla_w_in, mla_q_norm_g, mla_kv_norm_g, mla_w_uq, mla_w_ukv, mla_w_o, ret_w_in, ret_decay_f,
           ret_decay_b, ret_w_o, fnet_w, router_w, router_b, moe_w_gate, moe_w_up, moe_w_down):
    b_ctx, t_ctx, d = x_prompt.shape
    b_lat, t_lat, _ = x_sample.shape
    depth = ada_w.shape[0]
    assert b_lat + 1 <= 8
    n_ctx = b_ctx * t_ctx
    ctx = _Seg(0, b_ctx, t_ctx, 0, False)
    lat = _Seg(n_ctx, b_lat, t_lat, 1, True)
    segs = (ctx, lat)

    n_lat = b_lat * t_lat
    n_mla = mla_w_in.shape[0]
    assert n_mla >= 1
    x = None
    new_cache = jnp.zeros((b_ctx, n_mla, t_ctx, MLA_KV_LORA + MLA_ROPE), F32)
    cond8 = jnp.concatenate([c_ctx[None, :], c, jnp.zeros((8 - 1 - b_lat, d), F32)], axis=0)
    mods = _modulation_all(cond8, ada_w, ada_b).reshape(depth, 8, 6, 1, d)

    rw_t = router_w.T.astype(F32)
    rw_hi = rw_t.astype(BF16)
    rw_lo = (rw_t - rw_hi.astype(F32)).astype(BF16)
    rb = router_b.reshape(N_EXPERTS, 1).astype(F32)
    final_g = final_norm_g.reshape(1, d)
    dk = ret_w_in.shape[2] // (8 * RET_HEADS)
    dv = 2 * dk

    states = []
    pending = None
    counters = [0, 0, 0]
    for layer in range(depth):
        kind = layer % 3
        j = counters[kind]
        counters[kind] += 1
        g1 = norm1_g[layer].reshape(1, d)
        g2 = norm2_g[layer].reshape(1, d)
        if kind == 0:
            w = _mla_weights(mla_w_in[j], mla_q_norm_g[j], mla_kv_norm_g[j], mla_w_uq[j], mla_w_ukv[j],
                             mla_w_o[j])
            if x is None:
                xc, xc0, xl, xl0 = x_prompt.reshape(n_ctx, d), 0, x_sample.reshape(n_lat, d), 0
            else:
                xc, xc0, xl, xl0 = x, ctx.row0, x, lat.row0
            if pending is None:
                qc, kc, vc, new_cache = _mla_proj(xc, xc0, mods, layer, ctx, g1, w, None, (new_cache, j, n_mla))
                ql, kl, vl = _mla_proj(xl, xl0, mods, layer, lat, g1, w, _mla_rope_tables(t_lat))
            else:
                qc, kc, vc, new_cache, x = _mla_proj(x, ctx.row0, mods, layer, ctx, g1, w, None,
                                                     (new_cache, j, n_mla), pending)
                ql, kl, vl, x = _mla_proj(x, lat.row0, mods, layer, lat, g1, w, _mla_rope_tables(t_lat), None,
                                          pending)
                xc = xl = x
            past = cache_mla.shape[2]
            cpad = jnp.pad(cache_mla[:, j].reshape(b_lat * past, -1), ((0, 0), (0, LANE - MLA_ROPE)))
            kp, vp = _cache_kv(cpad, w)
            first = x is None
            x = _attention(xc, xc0, n_ctx + n_lat, jnp.zeros((n_ctx + n_lat, d), F32) if first else "inplace",
                           mods, layer, ctx, qc, [(kc, vc, t_ctx)], w["w_o"])
            x = _attention(xl if first else x, xl0, n_ctx + n_lat, x if first else "inplace", mods, layer, lat,
                           ql, [(kl, vl, t_lat), (kp, vp, past)], w["w_o"])
        elif kind == 1:
            w_in = ret_w_in[j]
            qk = RET_HEADS * dk
            k_scale = jnp.concatenate([jnp.ones((qk,), F32), jnp.full((qk,), dk ** -0.5, F32),
                                       jnp.ones((w_in.shape[1] - 2 * qk,), F32)])
            w_in_b = (w_in * k_scale[None, :]).astype(BF16)
            w_o_b = ret_w_o[j].astype(BF16)
            rot = _ret_rot_tables(t_lat, dk)
            n_qk, n_v = 2 * qk // RET_COL, RET_HEADS * dv // RET_COL
            assert (n_qk + n_v) * RET_COL * 2 == w_in.shape[1]
            parts = []
            for seg in segs:
                kinds = (("rotary" if seg is lat else "plain"),) * n_qk + ("plain",) * n_v
                qkv = _ret_proj(x, mods, layer, seg, g1, w_in_b, 0, kinds, rot, dk, pending)
                if pending is not None:
                    qkv, x = qkv
                parts.append((qkv, _ret_proj(x, mods, layer, seg, g1, w_in_b, 1, ("silu",) * (n_qk + n_v), None, dk)))
            yc, s_ctx = _ret_scan(*parts[0], ctx, ret_decay_f[j], ret_decay_b[j], None, True, dk, dv, RET_HEADS)
            yl, _ = _ret_scan(*parts[1], lat, ret_decay_f[j], ret_decay_b[j], state_ret[:, j], False, dk, dv, 1)
            x = _matmul_residual(x, mods, layer, ctx, yc, w_o_b)
            x = _matmul_residual(x, mods, layer, lat, yl, w_o_b)
            states.append(s_ctx)
        else:
            gd = d // FNET_GROUPS
            cc, sc = _dft_tables(gd)
            cs = jnp.concatenate([cc, sc], axis=1)
            w_b = fnet_w[j].astype(BF16)
            for seg in segs:
                ct, st = _dft_tables(seg.seq)
                if pending is None:
                    ac, as_ = _fnet_a(x, mods, layer, seg, g1, cs)
                else:
                    ac, as_, x = _fnet_a(x, mods, layer, seg, g1, cs, pending)
                x = _fnet_b(x, mods, layer, seg, ac, as_, ct, st, w_b, (seg.seq * gd) ** -0.5)
        if layer < depth - 1:
            pending = _moe(x, mods, layer, segs, g2, rw_hi, rw_lo, rb, moe_w_gate, moe_w_up, moe_w_down, None)
        else:
            y_prompt, y_sample = _moe(x, mods, layer, segs, g2, rw_hi, rw_lo, rb, moe_w_gate, moe_w_up,
                                      moe_w_down, final_g)

    new_state = jnp.stack(states, axis=1)
    return (y_prompt.reshape(b_ctx, t_ctx, d), y_sample.reshape(b_lat, t_lat, d), new_cache, new_state)
```

```python
import functools
import math

import jax
import jax.numpy as jnp
import numpy as np
from jax import lax
from jax.experimental import pallas as pl
from jax.experimental.pallas import tpu as pltpu
from jax.experimental.pallas import tpu_sc as plsc

F32 = jnp.float32
BF16 = jnp.bfloat16

GRID_W = 64
MLA_HEADS = 8
MLA_NOPE = 128
MLA_ROPE = 64
MLA_V = 128
MLA_Q_LORA = 384
MLA_KV_LORA = 256
ROPE_BASE = 10000.0
RET_HEADS = 4
RET_CHUNK = 256
FNET_GROUPS = 4
N_EXPERTS = 16
N_EXPERT_GROUPS = 4
EXPERTS_PER_GROUP = 4
D_EXPERT = 512
NORM_EPS = 1e-6

LANE = 128
PROJ_ROWS = 512
ATTN_ROWS = 512
ATTN_SHORT_SEQS = 4
WIDE_TILE = 1024
ROUTER_SUB_ROWS = 256
MOE_ROWS = 1024
MOE_TAIL_ROWS = 256
VMEM_LIMIT = 56 * 1024 * 1024
SC_CORES = 2
SC_SUBCORES = 16
SC_CHUNK = 128


def _cparams(*sem):
    return pltpu.CompilerParams(dimension_semantics=sem, vmem_limit_bytes=VMEM_LIMIT)


def _sigmoid(x):
    return 1.0 / (1.0 + jnp.exp(-x))


def _rms(x):
    return x * lax.rsqrt(jnp.mean(x * x, axis=-1, keepdims=True) + NORM_EPS)


def _modulate(x, g, shift, scale):
    return _rms(x) * (g * (1.0 + scale)) + shift


def _dot(a, b):
    return jnp.dot(a, b, preferred_element_type=F32)


def _dot_nt(a, b):
    return lax.dot_general(a, b, (((1,), (1,)), ((), ())), preferred_element_type=F32)


def _mod_kernel(c_ref, w_ref, b_ref, o_ref):
    c = c_ref[...]
    s = (c * _sigmoid(c)).astype(BF16)
    o_ref[...] = _dot(s, w_ref[...].astype(BF16)) + b_ref[...]


def _modulation_all(cond8, ada_w, ada_b):
    depth, d, d6 = ada_w.shape
    tn = d6 // 4
    return pl.pallas_call(
        _mod_kernel,
        grid=(depth, d6 // tn),
        in_specs=[
            pl.BlockSpec((8, d), lambda l, n: (0, 0)),
            pl.BlockSpec((None, d, tn), lambda l, n: (l, 0, n)),
            pl.BlockSpec((None, 1, tn), lambda l, n: (l, 0, n)),
        ],
        out_specs=pl.BlockSpec((None, 8, tn), lambda l, n: (l, 0, n)),
        out_shape=jax.ShapeDtypeStruct((depth, 8, d6), F32),
        compiler_params=_cparams("parallel", "parallel"),
        name="modulation",
    )(cond8, ada_w, ada_b.reshape(depth, 1, d6))


class _Seg:
    def __init__(self, row0, batch, seq, mod0, per_batch_mod):
        self.row0, self.batch, self.seq = row0, batch, seq
        self.mod0, self.per_batch_mod = mod0, per_batch_mod
        self.rows = batch * seq

    def tile(self, want):
        tm = min(want, self.seq) if self.per_batch_mod else want
        assert self.rows % tm == 0 and self.row0 % tm == 0 and (self.seq % tm == 0 or tm % self.seq == 0)
        return tm

    def seq_tile(self, want):
        tm = min(want, self.seq)
        assert self.seq % tm == 0 and self.row0 % tm == 0
        return tm

    def mod_row(self, tile, tm):
        if self.per_batch_mod:
            return self.mod0 + tile * tm // self.seq
        return self.mod0


def _mod_spec(layer, seg, d, tm, tile_of=lambda *a: a[0]):
    return pl.BlockSpec((None, None, 6, 1, d), lambda *a: (layer, seg.mod_row(tile_of(*a), tm), 0, 0, 0))


def _unpack_halves(p):
    lo = lax.bitcast_convert_type(p << 16, F32)
    hi = lax.bitcast_convert_type(p & jnp.uint32(0xFFFF0000), F32)
    return lo, hi


def _moe_residual(y_ref, w_ref, gate):
    w = w_ref[...]
    lo0, hi0 = _unpack_halves(y_ref[0])
    lo1, hi1 = _unpack_halves(y_ref[1])
    w0, w1 = w[:, 0:1], w[:, 1:2]
    return gate * jnp.concatenate([w0 * lo0 + w1 * lo1, w0 * hi0 + w1 * hi1], axis=-1)


def _take_pending(refs, pending):
    if not pending:
        return refs, lambda x: x
    y_ref, w_ref, pm_ref, *rest = refs
    xo_ref = rest.pop()

    def resolve(x):
        x = x + _moe_residual(y_ref, w_ref, pm_ref[5])
        xo_ref[...] = x
        return x

    return rest, resolve


def _pending_io(pending, seg, tm, d, n):
    yg, wcol, mods, layer = pending
    t0 = seg.row0 // tm
    specs = [pl.BlockSpec((2, tm, d // 2), lambda i: (0, t0 + i, 0)),
             pl.BlockSpec((tm, 8), lambda i: (t0 + i, 0)),
             _mod_spec(layer, seg, d, tm)]
    return specs, [yg, wcol, mods], pl.BlockSpec((tm, d), lambda i: (t0 + i, 0)), jax.ShapeDtypeStruct((n, d), F32)


def _rope_partner(x):
    lane = lax.broadcasted_iota(jnp.int32, x.shape, 1)
    first = (lane % 32) < 16
    return jnp.where(first, pltpu.roll(x, LANE - 16, 1), pltpu.roll(x, 16, 1))


def _store_values(v_ref, v):
    ones = jnp.ones((v.shape[0], LANE), BF16)
    for hd in range(MLA_HEADS):
        v_ref[:, hd * 2 * LANE:hd * 2 * LANE + LANE] = v[:, hd * MLA_V:(hd + 1) * MLA_V].astype(BF16)
        v_ref[:, hd * 2 * LANE + LANE:(hd + 1) * 2 * LANE] = ones


def _mla_proj_kernel(*refs, rope, pending):
    refs, resolve = _take_pending(refs, pending)
    if rope:
        (x_ref, m_ref, g_ref, win_ref, qg_ref, kvg_ref, wq_ref, wkn_ref, wv_ref, cos_ref, sin_ref,
         q_ref, k_ref, v_ref) = refs
    else:
        x_ref, m_ref, g_ref, win_ref, qg_ref, kvg_ref, wq_ref, wkn_ref, wv_ref = refs[:9]
        q_ref, k_ref, v_ref, cache_ref = refs[-4:]
    h = _modulate(resolve(x_ref[...]), g_ref[...], m_ref[0], m_ref[1]).astype(BF16)
    z = _dot(h, win_ref[...])
    cq = z[:, :MLA_Q_LORA]
    ckv = z[:, MLA_Q_LORA:MLA_Q_LORA + MLA_KV_LORA]
    kpe = z[:, MLA_Q_LORA + MLA_KV_LORA:]
    cqn = (_rms(cq) * qg_ref[...]).astype(BF16)
    ckvn = _rms(ckv) * kvg_ref[...]
    ckvb = ckvn.astype(BF16)
    q = _dot(cqn, wq_ref[...])
    kn = _dot(ckvb, wkn_ref[...])
    _store_values(v_ref, _dot(ckvb, wv_ref[...]))
    if rope:
        cos, sin = cos_ref[...], sin_ref[...]
        kpe = kpe * cos + _rope_partner(kpe) * sin
    else:
        seq = cache_ref.shape[1]
        for s in range(cache_ref.shape[0]):
            cache_ref[s, :, :MLA_KV_LORA] = ckvn[s * seq:(s + 1) * seq, :]
            cache_ref[s, :, MLA_KV_LORA:] = kpe[s * seq:(s + 1) * seq, :MLA_ROPE]
    kpe_b = kpe.astype(BF16)
    for hd in range(MLA_HEADS):
        lo = hd * 2 * LANE
        q_ref[:, lo:lo + LANE] = q[:, lo:lo + LANE].astype(BF16)
        qr = q[:, lo + LANE:lo + 2 * LANE]
        if rope:
            qr = qr * cos + _rope_partner(qr) * sin
        q_ref[:, lo + LANE:lo + 2 * LANE] = qr.astype(BF16)
        k_ref[:, lo:lo + LANE] = kn[:, hd * LANE:(hd + 1) * LANE].astype(BF16)
        k_ref[:, lo + LANE:lo + 2 * LANE] = kpe_b


def _mla_proj(x, x_row0, mods, layer, seg, g1, w, rope_tabs, cache_slot=None, pending=None):
    n, d = x.shape
    rope = rope_tabs is not None
    rows = seg.rows
    tm = seg.tile(PROJ_ROWS)
    x_tile0 = x_row0 // tm
    hq = MLA_HEADS * 2 * LANE
    const = lambda i: (0, 0)
    aliases = {}
    in_specs = [
        pl.BlockSpec((tm, d), lambda i: (x_tile0 + i, 0)),
        _mod_spec(layer, seg, d, tm),
        pl.BlockSpec((1, d), const),
        pl.BlockSpec(w["w_in"].shape, const),
        pl.BlockSpec((1, MLA_Q_LORA), const),
        pl.BlockSpec((1, MLA_KV_LORA), const),
        pl.BlockSpec(w["w_q"].shape, const),
        pl.BlockSpec(w["w_kn"].shape, const),
        pl.BlockSpec(w["w_v"].shape, const),
    ]
    args = [x, mods, g1, w["w_in"], w["q_g"], w["kv_g"], w["w_q"], w["w_kn"], w["w_v"]]
    out_specs = [pl.BlockSpec((tm, hq), lambda i: (i, 0))] * 3
    out_shape = [jax.ShapeDtypeStruct((rows, hq), BF16)] * 3
    if rope:
        tab = pl.BlockSpec((tm, LANE), lambda i: (i % (seg.seq // tm), 0))
        in_specs += [tab, tab]
        args += list(rope_tabs)
    else:
        cw = MLA_KV_LORA + MLA_ROPE
        prev, slot, n_slots = cache_slot
        assert tm % seg.seq == 0
        out_specs.append(pl.BlockSpec((tm // seg.seq, None, seg.seq, cw), lambda i: (i, slot, 0, 0)))
        out_shape.append(jax.ShapeDtypeStruct((seg.batch, n_slots, seg.seq, cw), F32))
        in_specs.append(pl.BlockSpec(memory_space=pl.ANY))
        args.append(prev)
        aliases = {len(args) - 1: 3}
    if pending is not None:
        p_specs, p_args, xo_spec, xo_shape = _pending_io(pending, seg, tm, d, n)
        in_specs, args = p_specs + in_specs, p_args + args
        out_specs.append(xo_spec)
        out_shape.append(xo_shape)
        aliases = {k + len(p_args): v for k, v in aliases.items()}
        aliases[len(p_args)] = len(out_shape) - 1
    return pl.pallas_call(
        functools.partial(_mla_proj_kernel, rope=rope, pending=pending is not None),
        grid=(rows // tm,),
        in_specs=in_specs,
        out_specs=out_specs,
        out_shape=out_shape,
        input_output_aliases=aliases,
        compiler_params=_cparams("parallel"),
        name="mla_proj_lat" if rope else "mla_proj_ctx",
    )(*args)


def _cache_kv_kernel(c_ref, wkn_ref, wv_ref, k_ref, v_ref):
    c = c_ref[...]
    ckv = c[:, :MLA_KV_LORA].astype(BF16)
    kpe_b = c[:, MLA_KV_LORA:].astype(BF16)
    kn = _dot(ckv, wkn_ref[...])
    _store_values(v_ref, _dot(ckv, wv_ref[...]))
    for hd in range(MLA_HEADS):
        lo = hd * 2 * LANE
        k_ref[:, lo:lo + LANE] = kn[:, hd * LANE:(hd + 1) * LANE].astype(BF16)
        k_ref[:, lo + LANE:lo + 2 * LANE] = kpe_b


def _cache_kv(cache_pad, w):
    rows, cw = cache_pad.shape
    hq = MLA_HEADS * 2 * LANE
    const = lambda i: (0, 0)
    tm = min(PROJ_ROWS, rows)
    assert rows % tm == 0
    return pl.pallas_call(
        _cache_kv_kernel,
        grid=(rows // tm,),
        in_specs=[
            pl.BlockSpec((tm, cw), lambda i: (i, 0)),
            pl.BlockSpec(w["w_kn"].shape, const),
            pl.BlockSpec(w["w_v"].shape, const),
        ],
        out_specs=[pl.BlockSpec((tm, hq), lambda i: (i, 0))] * 2,
        out_shape=[jax.ShapeDtypeStruct((rows, hq), BF16)] * 2,
        compiler_params=_cparams("parallel"),
        name="mla_cache_kv",
    )(cache_pad, w["w_kn"], w["w_v"])


def _attn_kernel(*refs, n_parts, n_seq):
    q_ref = refs[0]
    kv_refs = refs[1:1 + 2 * n_parts]
    wo_ref, x_ref, m_ref = refs[1 + 2 * n_parts:4 + 2 * n_parts]
    o_ref, acc_ref = refs[-2:]
    tq = q_ref.shape[0] // n_seq
    for sq, hd in [(sq, hd) for sq in range(n_seq) for hd in range(MLA_HEADS)]:
        rows = slice(sq * tq, (sq + 1) * tq)
        kcol = slice(hd * 2 * LANE, (hd + 1) * 2 * LANE)
        keys = [slice(sq * (r.shape[0] // n_seq), (sq + 1) * (r.shape[0] // n_seq)) for r in kv_refs[::2]]
        scores = [_dot_nt(q_ref[rows, kcol], kv_refs[2 * p][keys[p], kcol]) for p in range(n_parts)]
        mx = scores[0].max(axis=-1, keepdims=True)
        for s in scores[1:]:
            mx = jnp.maximum(mx, s.max(axis=-1, keepdims=True))
        out = None
        for p, s in enumerate(scores):
            e = jnp.exp((s - mx).astype(BF16))
            pv = _dot(e, kv_refs[2 * p + 1][keys[p], kcol])
            out = pv if out is None else out + pv
        acc_ref[rows, hd * MLA_V:(hd + 1) * MLA_V] = (out[:, :MLA_V] / out[:, LANE:LANE + MLA_V]).astype(BF16)
    y = _dot(acc_ref[...], wo_ref[...])
    o_ref[...] = x_ref[...] + m_ref[2] * y


def _attention(x, x_row0, n, dest, mods, layer, seg, q, kv_parts, w_o):
    d = x.shape[1]
    hq = MLA_HEADS * 2 * LANE
    hv = MLA_HEADS * MLA_V
    tq = seg.seq_tile(ATTN_ROWS)
    tps = seg.seq // tq
    n_seq = ATTN_SHORT_SEQS if (tps == 1 and not seg.per_batch_mod and seg.batch % ATTN_SHORT_SEQS == 0) else 1
    tq *= n_seq
    x_tile0, out_tile0 = x_row0 // tq, seg.row0 // tq
    in_specs = [pl.BlockSpec((tq, hq), lambda b, i: (b * tps + i, 0))]
    args = [q]
    for k, v, rows in kv_parts:
        mode = dict(pipeline_mode=pl.Buffered(1)) if tps > 1 else {}
        in_specs += [pl.BlockSpec((n_seq * rows, hq), lambda b, i: (b, 0), **mode)] * 2
        args += [k, v]
    in_specs += [
        pl.BlockSpec(w_o.shape, lambda b, i: (0, 0)),
        pl.BlockSpec((tq, d), lambda b, i: (x_tile0 + b * tps + i, 0)),
        _mod_spec(layer, seg, d, tq // n_seq, tile_of=lambda b, i: (b * tps + i) * n_seq),
    ]
    args += [w_o, x, mods]
    if isinstance(dest, str):
        assert dest == "inplace"
        aliases = {len(args) - 2: 0}
    else:
        in_specs.append(pl.BlockSpec(memory_space=pl.ANY))
        args.append(dest)
        aliases = {len(args) - 1: 0}
    return pl.pallas_call(
        functools.partial(_attn_kernel, n_parts=len(kv_parts), n_seq=n_seq),
        grid=(seg.batch // n_seq, tps),
        in_specs=in_specs,
        out_specs=pl.BlockSpec((tq, d), lambda b, i: (out_tile0 + b * tps + i, 0)),
        out_shape=jax.ShapeDtypeStruct((n, d), F32),
        scratch_shapes=[pltpu.VMEM((tq, hv), BF16)],
        input_output_aliases=aliases,
        compiler_params=_cparams("parallel", "arbitrary"),
        name="mla_attention",
    )(*args)


RET_COL = 1024


RET_ROWS = 1024


def _ret_proj_kernel(*refs, kinds, dk, pending):
    refs, resolve = _take_pending(refs, pending)
    rotary = "rotary" in kinds
    if rotary:
        x_ref, m_ref, g_ref, w_ref, cos_ref, sin_ref, z_ref = refs
        cos, sin = cos_ref[...], sin_ref[...]
    else:
        x_ref, m_ref, g_ref, w_ref, z_ref = refs
    h = _modulate(resolve(x_ref[...]), g_ref[...], m_ref[0], m_ref[1]).astype(BF16)
    half = dk // 2
    for j, kind in enumerate(kinds):
        c0 = j * RET_COL
        acc = _dot(h, w_ref[:, c0:c0 + RET_COL])
        if kind == "rotary":
            for hd in range(RET_COL // dk):
                lo = hd * dk
                x1, x2 = acc[:, lo:lo + half], acc[:, lo + half:lo + dk]
                z_ref[:, c0 + lo:c0 + lo + half] = (x1 * cos - x2 * sin).astype(BF16)
                z_ref[:, c0 + lo + half:c0 + lo + dk] = (x1 * sin + x2 * cos).astype(BF16)
        elif kind == "silu":
            z_ref[:, c0:c0 + RET_COL] = (acc * _sigmoid(acc)).astype(BF16)
        else:
            z_ref[:, c0:c0 + RET_COL] = acc.astype(BF16)


def _ret_proj(x, mods, layer, seg, g1, w_in, group, kinds, rot_tabs, dk, pending=None):
    n, d = x.shape
    rows = seg.rows
    tm = seg.tile(RET_ROWS)
    ncol = len(kinds) * RET_COL
    tps = max(seg.seq // tm, 1)
    in_specs = [
        pl.BlockSpec((tm, d), lambda i: (seg.row0 // tm + i, 0)),
        _mod_spec(layer, seg, d, tm),
        pl.BlockSpec((1, d), lambda i: (0, 0)),
        pl.BlockSpec((d, ncol), lambda i: (0, group), pipeline_mode=pl.Buffered(1)),
    ]
    args = [x, mods, g1, w_in]
    if "rotary" in kinds:
        tab = pl.BlockSpec((tm, dk // 2), lambda i: (i % tps, 0))
        in_specs += [tab, tab]
        args += list(rot_tabs)
    out_specs = [pl.BlockSpec((tm, ncol), lambda i: (i, 0))]
    out_shape = [jax.ShapeDtypeStruct((rows, ncol), BF16)]
    aliases = {}
    if pending is not None:
        p_specs, p_args, xo_spec, xo_shape = _pending_io(pending, seg, tm, d, n)
        in_specs, args = p_specs + in_specs, p_args + args
        out_specs.append(xo_spec)
        out_shape.append(xo_shape)
        aliases = {len(p_args): 1}
    res = pl.pallas_call(
        functools.partial(_ret_proj_kernel, kinds=kinds, dk=dk, pending=pending is not None),
        grid=(rows // tm,),
        in_specs=in_specs,
        out_specs=out_specs,
        out_shape=out_shape,
        input_output_aliases=aliases,
        compiler_params=_cparams("parallel"),
        name="ret_proj_" + kinds[0],
    )(*args)
    return res if pending is not None else res[0]


def _log_sigmoid(x):
    return jnp.minimum(x, 0.0) - jnp.log(1.0 + jnp.exp(-jnp.abs(x)))


def _ret_scan_kernel(*refs, has_s0, emit_state, n_chunks, heads):
    refs = list(refs)
    lf_ref, lb_ref, q_ref, k_ref, v_ref, gf_ref, gb_ref = refs[:7]
    pos = 7
    s0_ref = None
    if has_s0:
        s0_ref = refs[pos]
        pos += 1
    y_ref = refs[pos]
    pos += 1
    sout_ref = None
    if emit_state:
        sout_ref = refs[pos]
        pos += 1
    s_ref, yf_ref = refs[pos:]
    c = RET_CHUNK
    dk, dv = s_ref.shape
    ii = lax.broadcasted_iota(jnp.int32, (c, c), 0).astype(F32)
    jj = lax.broadcasted_iota(jnp.int32, (c, c), 1).astype(F32)
    idx = lax.broadcasted_iota(jnp.int32, (c, 1), 0).astype(F32)

    for hd, direction in [(hd, direction) for hd in range(heads) for direction in range(2)]:
        fwd = direction == 0
        kcol, vcol = slice(hd * dk, (hd + 1) * dk), slice(hd * dv, (hd + 1) * dv)
        lg = _log_sigmoid((lf_ref if fwd else lb_ref)[hd])
        rel = (ii - jj) if fwd else (jj - ii)
        keep = rel >= 0
        decay_in = jnp.where(keep, jnp.exp(jnp.where(keep, rel, 0.0) * lg), 0.0)
        decay_q = jnp.exp(((idx + 1.0) if fwd else (c - idx)) * lg)
        decay_k = jnp.exp(((c - 1.0 - idx) if fwd else idx) * lg)
        decay_c = jnp.exp(c * lg)
        g_ref = gf_ref if fwd else gb_ref

        def chunk(cc, state, fwd=fwd, decay_in=decay_in, decay_q=decay_q, decay_k=decay_k, decay_c=decay_c,
                  g_ref=g_ref, kcol=kcol, vcol=vcol):
            r0 = cc * c if isinstance(cc, int) else pl.multiple_of(cc * c, c)
            qc = q_ref[pl.ds(r0, c), kcol]
            kc = k_ref[pl.ds(r0, c), kcol]
            vc = v_ref[pl.ds(r0, c), vcol]
            sc = _dot_nt(qc, kc) * decay_in
            out = _dot(sc.astype(BF16), vc)
            kd_t = (kc.astype(F32) * decay_k).T.astype(BF16)
            new_s = _dot(kd_t, vc)
            if state is not None:
                out = out + decay_q * _dot(qc, state.astype(BF16))
                new_s = decay_c * state + new_s
            s_ref[...] = new_s
            o = _rms(out) * g_ref[pl.ds(r0, c), vcol].astype(F32)
            if fwd:
                yf_ref[pl.ds(r0, c), :] = o
            else:
                y_ref[pl.ds(r0, c), vcol] = (yf_ref[pl.ds(r0, c), :] + o).astype(BF16)

        chunk(0 if fwd else n_chunks - 1, s0_ref[direction, hd] if has_s0 else None)

        def step(ci, carry, fwd=fwd, chunk=chunk):
            chunk(ci if fwd else n_chunks - 1 - ci, s_ref[...])
            return carry

        lax.fori_loop(1, n_chunks, step, 0, unroll=True)
        if emit_state:
            sout_ref[direction, hd] = s_ref[...]


def _ret_scan(qkv, g, seg, logit_f, logit_b, s0, emit_state, dk, dv, heads):
    rows = seg.batch * seg.seq
    t = seg.seq
    hh = RET_HEADS
    groups = hh // heads
    v0 = 2 * hh * dk // (heads * dv)
    assert groups * heads == hh and v0 * heads * dv == 2 * hh * dk
    in_specs = [
        pl.BlockSpec((heads, 1, 1), lambda b, h: (h, 0, 0)),
        pl.BlockSpec((heads, 1, 1), lambda b, h: (h, 0, 0)),
        pl.BlockSpec((t, heads * dk), lambda b, h: (b, h)),
        pl.BlockSpec((t, heads * dk), lambda b, h: (b, groups + h)),
        pl.BlockSpec((t, heads * dv), lambda b, h: (b, v0 + h)),
        pl.BlockSpec((t, heads * dv), lambda b, h: (b, h)),
        pl.BlockSpec((t, heads * dv), lambda b, h: (b, groups + h)),
    ]
    args = [logit_f.reshape(hh, 1, 1), logit_b.reshape(hh, 1, 1), qkv, qkv, qkv, g, g]
    state_spec = pl.BlockSpec((None, 2, heads, dk, dv), lambda b, h: (b, 0, h, 0, 0))
    if s0 is not None:
        in_specs.append(state_spec)
        args.append(s0)
    out_specs = [pl.BlockSpec((t, heads * dv), lambda b, h: (b, h))]
    out_shape = [jax.ShapeDtypeStruct((rows, hh * dv), BF16)]
    if emit_state:
        out_specs.append(state_spec)
        out_shape.append(jax.ShapeDtypeStruct((seg.batch, 2, hh, dk, dv), F32))
    res = pl.pallas_call(
        functools.partial(_ret_scan_kernel, has_s0=s0 is not None, emit_state=emit_state,
                          n_chunks=t // RET_CHUNK, heads=heads),
        grid=(seg.batch, groups),
        in_specs=in_specs,
        out_specs=out_specs,
        out_shape=out_shape,
        scratch_shapes=[pltpu.VMEM((dk, dv), F32), pltpu.VMEM((t, dv), F32)],
        compiler_params=_cparams("parallel", "parallel"),
        name="ret_scan",
    )(*args)
    return res if emit_state else (res[0], None)


def _mm_res_kernel(a_ref, w_ref, x_ref, m_ref, o_ref):
    o_ref[...] = x_ref[...] + m_ref[2] * _dot(a_ref[...], w_ref[...])


def _matmul_residual(x, mods, layer, seg, a, w):
    n, d = x.shape
    tm = seg.tile(RET_ROWS)
    x_spec = pl.BlockSpec((tm, d), lambda i: (seg.row0 // tm + i, 0))
    return pl.pallas_call(
        _mm_res_kernel,
        grid=(seg.rows // tm,),
        in_specs=[
            pl.BlockSpec((tm, a.shape[1]), lambda i: (i, 0)),
            pl.BlockSpec(w.shape, lambda i: (0, 0)),
            x_spec,
            _mod_spec(layer, seg, d, tm),
        ],
        out_specs=x_spec,
        out_shape=jax.ShapeDtypeStruct((n, d), F32),
        input_output_aliases={2: 0},
        compiler_params=_cparams("parallel"),
        name="matmul_residual",
    )(a, w, x, mods)


def _fnet_a_kernel(*refs, gd, pending):
    refs, resolve = _take_pending(refs, pending)
    x_ref, m_ref, g_ref, cs_ref, ac_ref, as_ref = refs
    h = _modulate(resolve(x_ref[...]), g_ref[...], m_ref[0], m_ref[1]).astype(BF16)
    cs = cs_ref[...]
    for g in range(FNET_GROUPS):
        a = _dot(h[:, g * gd:(g + 1) * gd], cs)
        ac_ref[:, g * gd:(g + 1) * gd] = a[:, :gd].astype(BF16)
        as_ref[:, g * gd:(g + 1) * gd] = a[:, gd:].astype(BF16)


def _fnet_a(x, mods, layer, seg, g1, cs, pending=None):
    n, d = x.shape
    rows = seg.rows
    tm = seg.tile(PROJ_ROWS)
    out = pl.BlockSpec((tm, d), lambda i: (i, 0))
    in_specs = [
        pl.BlockSpec((tm, d), lambda i: (seg.row0 // tm + i, 0)),
        _mod_spec(layer, seg, d, tm),
        pl.BlockSpec((1, d), lambda i: (0, 0)),
        pl.BlockSpec(cs.shape, lambda i: (0, 0)),
    ]
    args = [x, mods, g1, cs]
    out_specs, out_shape, aliases = [out, out], [jax.ShapeDtypeStruct((rows, d), BF16)] * 2, {}
    if pending is not None:
        p_specs, p_args, xo_spec, xo_shape = _pending_io(pending, seg, tm, d, n)
        in_specs, args = p_specs + in_specs, p_args + args
        out_specs.append(xo_spec)
        out_shape.append(xo_shape)
        aliases = {len(p_args): 2}
    return pl.pallas_call(
        functools.partial(_fnet_a_kernel, gd=d // FNET_GROUPS, pending=pending is not None),
        grid=(rows // tm,),
        in_specs=in_specs,
        out_specs=out_specs,
        out_shape=out_shape,
        input_output_aliases=aliases,
        compiler_params=_cparams("parallel"),
        name="fnet_channel_dft",
    )(*args)


def _fnet_b_kernel(ct_ref, st_ref, ac_ref, as_ref, w_ref, x_ref, m_ref, o_ref, *, norm):
    f = (_dot(ct_ref[...], ac_ref[...]) - _dot(st_ref[...], as_ref[...])) * norm
    o_ref[...] = x_ref[...] + m_ref[2] * _dot(f.astype(BF16), w_ref[...])


def _fnet_b(x, mods, layer, seg, ac, as_, ct, st, w, norm):
    n, d = x.shape
    t = seg.seq
    tq = seg.seq_tile(ATTN_ROWS)
    tps = t // tq
    x_spec = pl.BlockSpec((tq, d), lambda b, i: (seg.row0 // tq + b * tps + i, 0))
    tab = pl.BlockSpec((tq, t), lambda b, i: (i, 0))
    seq = pl.BlockSpec((t, d), lambda b, i: (b, 0))
    return pl.pallas_call(
        functools.partial(_fnet_b_kernel, norm=norm),
        grid=(seg.batch, tps),
        in_specs=[tab, tab, seq, seq, pl.BlockSpec(w.shape, lambda b, i: (0, 0)), x_spec,
                  _mod_spec(layer, seg, d, tq, tile_of=lambda b, i: b * tps + i)],
        out_specs=x_spec,
        out_shape=jax.ShapeDtypeStruct((n, d), F32),
        input_output_aliases={5: 0},
        compiler_params=_cparams("parallel", "arbitrary"),
        name="fnet_position_dft",
    )(ct, st, ac, as_, w, x, mods)


def _pack_halves(a):
    w = a.shape[1] // 2
    bits = lambda v: lax.bitcast_convert_type(v.astype(BF16).astype(F32), jnp.uint32)
    return (bits(a[:, :w]) >> 16) | (bits(a[:, w:]) & jnp.uint32(0xFFFF0000))


def _router_kernel(x_ref, m_ref, g_ref, rwhi_ref, rwlo_ref, rb_ref, h_ref, idx_ref, rank_ref, wcol_ref, cnt_ref,
                   run_ref, tri_ref):
    step = pl.program_id(0)

    @pl.when(step == 0)
    def _():
        run_ref[...] = jnp.zeros_like(run_ref)
        tt = tri_ref.shape[0]
        earlier = lax.broadcasted_iota(jnp.int32, (tt, tt), 0) < lax.broadcasted_iota(jnp.int32, (tt, tt), 1)
        tri_ref[...] = jnp.where(earlier, 1.0, 0.0).astype(BF16)

    parts = []
    for r0 in range(0, x_ref.shape[0], ROUTER_SUB_ROWS):
        rows = slice(r0, r0 + ROUTER_SUB_ROWS)
        h = _modulate(x_ref[rows, :], g_ref[...], m_ref[3], m_ref[4])
        h_ref[rows, :] = _pack_halves(h)
        h_hi = h.astype(BF16)
        h_lo = (h - h_hi.astype(F32)).astype(BF16)
        parts.append(_dot_nt(rwhi_ref[...], h_hi) + (_dot_nt(rwhi_ref[...], h_lo) + _dot_nt(rwlo_ref[...], h_hi)))
    logits = jnp.concatenate(parts, axis=1)
    sc = _sigmoid(logits)
    gr = sc + rb_ref[...]
    gp = EXPERTS_PER_GROUP
    row = lambda a, e: a[e:e + 1, :]
    best_g = None
    for g in range(N_EXPERT_GROUPS):
        vals = [row(gr, g * gp + i) for i in range(gp)]
        gs = None
        for i in range(gp):
            for j in range(i + 1, gp):
                pair = vals[i] + vals[j]
                gs = pair if gs is None else jnp.maximum(gs, pair)
        if best_g is None:
            best_g, best_v = jnp.zeros(gs.shape, jnp.int32), gs
        else:
            better = gs > best_v
            best_g = jnp.where(better, g, best_g)
            best_v = jnp.where(better, gs, best_v)
    sel, raw = [], []
    for i in range(gp):
        s_i, r_i = row(gr, i), row(sc, i)
        for g in range(1, N_EXPERT_GROUPS):
            s_i = jnp.where(best_g == g, row(gr, g * gp + i), s_i)
            r_i = jnp.where(best_g == g, row(sc, g * gp + i), r_i)
        sel.append(s_i)
        raw.append(r_i)

    def argmax_first(vals, raws):
        bi, bv, br = jnp.zeros(vals[0].shape, jnp.int32), vals[0], raws[0]
        for i in range(1, len(vals)):
            better = vals[i] > bv
            bi = jnp.where(better, i, bi)
            bv = jnp.where(better, vals[i], bv)
            br = jnp.where(better, raws[i], br)
        return bi, br

    i1, w1 = argmax_first(sel, raw)
    masked = [jnp.where(i1 == i, -jnp.inf, sel[i]) for i in range(gp)]
    i2, w2 = argmax_first(masked, raw)
    tot = w1 + w2
    e1 = best_g * gp + i1
    e2 = best_g * gp + i2
    idx_ref[0:1, :] = e1
    idx_ref[1:2, :] = e2
    t = e1.shape[1]
    sub = lax.broadcasted_iota(jnp.int32, (8, t), 0)
    w8 = jnp.where(sub == 0, w1 / tot, jnp.where(sub == 1, w2 / tot, 0.0))
    wcol_ref[...] = w8.T
    eio = lax.broadcasted_iota(jnp.int32, (N_EXPERTS, t), 0)
    oh1, oh2 = eio == e1, eio == e2
    oh = jnp.where(oh1, 1.0, jnp.where(oh2, 1.0, 0.0))
    local = _dot(oh.astype(BF16), tri_ref[...])
    rank = local + run_ref[:, 0:1]
    rank_ref[0:1, :] = jnp.sum(jnp.where(oh1, rank, 0.0), axis=0, keepdims=True).astype(jnp.int32)
    rank_ref[1:2, :] = jnp.sum(jnp.where(oh2, rank, 0.0), axis=0, keepdims=True).astype(jnp.int32)
    run_ref[...] = run_ref[...] + jnp.sum(oh, axis=1, keepdims=True)
    cnt_ref[...] = run_ref[...]


def _wide_mod_row(segs, tm):
    ctx, lat = segs
    ctx_tiles = ctx.batch * ctx.seq // tm
    assert ctx_tiles * tm == ctx.batch * ctx.seq and lat.seq % tm == 0
    return lambda i: jnp.where(i < ctx_tiles, ctx.mod0, lat.mod0 + (i - ctx_tiles) // (lat.seq // tm))


def _router(x, mods, layer, segs, g2, rw_hi, rw_lo, rb):
    n, d = x.shape
    tm = WIDE_TILE
    mod_row = _wide_mod_row(segs, tm)
    return pl.pallas_call(
        _router_kernel,
        grid=(n // tm,),
        in_specs=[
            pl.BlockSpec((tm, d), lambda i: (i, 0)),
            pl.BlockSpec((None, None, 6, 1, d), lambda i: (layer, mod_row(i), 0, 0, 0)),
            pl.BlockSpec((1, d), lambda i: (0, 0)),
            pl.BlockSpec(rw_hi.shape, lambda i: (0, 0)),
            pl.BlockSpec(rw_lo.shape, lambda i: (0, 0)),
            pl.BlockSpec(rb.shape, lambda i: (0, 0)),
        ],
        out_specs=[
            pl.BlockSpec((tm, d // 2), lambda i: (i, 0)),
            pl.BlockSpec((2, tm), lambda i: (0, i)),
            pl.BlockSpec((2, tm), lambda i: (0, i)),
            pl.BlockSpec((tm, 8), lambda i: (i, 0)),
            pl.BlockSpec((N_EXPERTS, LANE), lambda i: (0, 0)),
        ],
        out_shape=[
            jax.ShapeDtypeStruct((n, d // 2), jnp.uint32),
            jax.ShapeDtypeStruct((2, n), jnp.int32),
            jax.ShapeDtypeStruct((2, n), jnp.int32),
            jax.ShapeDtypeStruct((n, 8), F32),
            jax.ShapeDtypeStruct((N_EXPERTS, LANE), F32),
        ],
        scratch_shapes=[pltpu.VMEM((N_EXPERTS, LANE), F32), pltpu.VMEM((tm, tm), BF16)],
        compiler_params=_cparams("arbitrary"),
        name="moe_router",
    )(x, mods, g2, rw_hi, rw_lo, rb)


def _expert_kernel(be_ref, br_ref, bs_ref, xs_ref, wg_ref, wu_ref, wd_ref, y_ref, wg_b, wu_b, wd_b):
    i = pl.program_id(0)
    prev = be_ref[jnp.maximum(i - 1, 0)]
    valid = br_ref[i]
    tail_rows = MOE_TAIL_ROWS

    @pl.when(jnp.logical_or(i == 0, be_ref[i] != prev))
    def _():
        wg_b[...] = wg_ref[...].astype(BF16)
        wu_b[...] = wu_ref[...].astype(BF16)
        wd_b[...] = wd_ref[...].astype(BF16)

    def ffn(r0, nrows):
        lo, hi = _unpack_halves(xs_ref[r0:r0 + nrows, :])
        xb = jnp.concatenate([lo.astype(BF16), hi.astype(BF16)], axis=1)
        gate = _dot(xb, wg_b[...])
        hid = (gate * _sigmoid(gate)) * _dot(xb, wu_b[...])
        y_ref[r0:r0 + nrows, :] = _pack_halves(_dot(hid.astype(BF16), wd_b[...]))

    @pl.when(valid > tail_rows)
    def _():
        ffn(0, y_ref.shape[0])

    @pl.when(jnp.logical_and(valid > 0, valid <= tail_rows))
    def _():
        ffn(0, tail_rows)
        y_ref[tail_rows:, :] = jnp.zeros((y_ref.shape[0] - tail_rows, y_ref.shape[1]), y_ref.dtype)


def _experts(xs, block_e, block_rows, block_src, w_gate, w_up, w_down, layer):
    rows, half = xs.shape
    d = 2 * half
    de = w_gate.shape[-1]
    n_blocks = rows // MOE_ROWS
    grid_spec = pltpu.PrefetchScalarGridSpec(
        num_scalar_prefetch=3,
        grid=(n_blocks,),
        in_specs=[
            pl.BlockSpec((MOE_ROWS, half), lambda i, be, br, bs: (bs[i], 0)),
            pl.BlockSpec((None, None, d, de), lambda i, be, br, bs: (layer, be[i], 0, 0)),
            pl.BlockSpec((None, None, d, de), lambda i, be, br, bs: (layer, be[i], 0, 0)),
            pl.BlockSpec((None, None, de, d), lambda i, be, br, bs: (layer, be[i], 0, 0)),
        ],
        out_specs=pl.BlockSpec((MOE_ROWS, half), lambda i, be, br, bs: (bs[i], 0)),
        scratch_shapes=[pltpu.VMEM((d, de), BF16), pltpu.VMEM((d, de), BF16), pltpu.VMEM((de, d), BF16)],
    )
    return pl.pallas_call(
        _expert_kernel,
        grid_spec=grid_spec,
        out_shape=jax.ShapeDtypeStruct((rows, half), jnp.uint32),
        compiler_params=_cparams("arbitrary"),
        name="moe_experts",
    )(block_e, block_rows, block_src, xs, w_gate, w_up, w_down)


def _combine_kernel(x_ref, m_ref, y_ref, w_ref, fg_ref, o_ref):
    x = x_ref[...] + _moe_residual(y_ref, w_ref, m_ref[5])
    o_ref[...] = _rms(x) * fg_ref[...]


def _combine(x, mods, layer, segs, seg, yg, wcol, final_g):
    n, d = x.shape
    tm = WIDE_TILE
    mod_row = _wide_mod_row(segs, tm)
    t0 = seg.row0 // tm
    steps = seg.rows // tm
    in_specs = [pl.BlockSpec((tm, d), lambda i: (t0 + i, 0)),
                pl.BlockSpec((None, None, 6, 1, d), lambda i: (layer, mod_row(t0 + i), 0, 0, 0)),
                pl.BlockSpec((2, tm, d // 2), lambda i: (0, t0 + i, 0)),
                pl.BlockSpec((tm, 8), lambda i: (t0 + i, 0)),
                pl.BlockSpec((1, d), lambda i: (0, 0))]
    return pl.pallas_call(
        _combine_kernel,
        grid=(steps,),
        in_specs=in_specs,
        out_specs=pl.BlockSpec((tm, d), lambda i: (i, 0)),
        out_shape=jax.ShapeDtypeStruct((seg.rows, d), F32),
        compiler_params=_cparams("parallel"),
        name="moe_combine_final",
    )(x, mods, yg, wcol, final_g)


def _sc_mesh():
    return plsc.VectorSubcoreMesh(core_axis_name="c", subcore_axis_name="s")


def _sc_worker_split(n):
    workers = SC_CORES * SC_SUBCORES
    per = n // workers
    assert per * workers == n and per % SC_CHUNK == 0
    return workers, per, per // SC_CHUNK


def _sc_dispatch(h, pos, rows):
    n, w = h.shape
    workers, per, chunks = _sc_worker_split(n)

    @functools.partial(
        pl.kernel, out_type=jax.ShapeDtypeStruct((rows, w), h.dtype), mesh=_sc_mesh(),
        scratch_types=[pltpu.VMEM((2, chunks, SC_CHUNK), jnp.int32), pltpu.VMEM((SC_CHUNK, w), h.dtype)],
        name="moe_dispatch_scatter")
    def scatter_rows(h_hbm, pos_hbm, xs_hbm, idx_v, rows_v):
        wid = lax.axis_index("s") * SC_CORES + lax.axis_index("c")
        pltpu.sync_copy(pos_hbm.at[0, wid], idx_v.at[0])
        pltpu.sync_copy(pos_hbm.at[1, wid], idx_v.at[1])

        @pl.loop(0, chunks)
        def _(c):
            pltpu.sync_copy(h_hbm.at[pl.ds(wid * per + c * SC_CHUNK, SC_CHUNK)], rows_v)
            pltpu.sync_copy(rows_v, xs_hbm.at[idx_v.at[0, c]])
            pltpu.sync_copy(rows_v, xs_hbm.at[idx_v.at[1, c]])

    return scatter_rows(h, pos.reshape(2, workers, chunks, SC_CHUNK))


def _sc_gather2(ys, pos):
    _, w = ys.shape
    n = pos.shape[1]
    workers, per, chunks = _sc_worker_split(n)

    @functools.partial(
        pl.kernel, out_type=jax.ShapeDtypeStruct((2, n, w), ys.dtype), mesh=_sc_mesh(),
        scratch_types=[pltpu.VMEM((2, chunks, SC_CHUNK), jnp.int32), pltpu.VMEM((SC_CHUNK, w), ys.dtype),
                       pltpu.SemaphoreType.DMA],
        name="moe_combine_gather")
    def gather_rows(ys_hbm, pos_hbm, out_hbm, idx_v, rows_v, sem):
        wid = lax.axis_index("s") * SC_CORES + lax.axis_index("c")
        pltpu.sync_copy(pos_hbm.at[0, wid], idx_v.at[0])
        pltpu.sync_copy(pos_hbm.at[1, wid], idx_v.at[1])

        @pl.loop(0, chunks)
        def _(c):
            for k in range(2):
                pltpu.async_copy(ys_hbm.at[idx_v.at[k, c]], rows_v, sem).wait()
                pltpu.sync_copy(rows_v, out_hbm.at[k, pl.ds(wid * per + c * SC_CHUNK, SC_CHUNK)])

    return gather_rows(ys, pos.reshape(2, workers, chunks, SC_CHUNK))


def _dispatch_plan(idx, rank, counts):
    n = idx.shape[1]
    padded = (counts + MOE_ROWS - 1) // MOE_ROWS * MOE_ROWS
    pad_end = jnp.cumsum(padded)
    pad_start = pad_end - padded
    experts = jnp.arange(N_EXPERTS, dtype=jnp.int32)
    start_of = jnp.sum(jnp.where(idx[..., None] == experts, pad_start, 0), axis=-1)
    pos = start_of + rank
    n_blocks = 2 * n // MOE_ROWS + N_EXPERTS
    starts = jnp.arange(n_blocks, dtype=jnp.int32) * MOE_ROWS
    block_e = jnp.minimum(jnp.sum(starts[:, None] >= pad_end[None, :], axis=1), N_EXPERTS - 1).astype(jnp.int32)
    seg_end = jnp.sum(jnp.where(block_e[:, None] == experts, pad_start + counts, 0), axis=-1)
    block_rows = jnp.clip(seg_end - starts, 0, MOE_ROWS).astype(jnp.int32)
    last_used = pad_end[-1] // MOE_ROWS - 1
    block_src = jnp.minimum(jnp.arange(n_blocks, dtype=jnp.int32), last_used).astype(jnp.int32)
    block_e = jnp.sum(jnp.where(block_src[:, None] == jnp.arange(n_blocks)[None, :], block_e[None, :], 0), axis=1)
    return pos, block_e.astype(jnp.int32), block_rows, block_src, n_blocks * MOE_ROWS


def _moe(x, mods, layer, segs, g2, rw_hi, rw_lo, rb, w_gate, w_up, w_down, final_g):
    h2p, idx, rank, wcol, cnt = _router(x, mods, layer, segs, g2, rw_hi, rw_lo, rb)
    pos, block_e, block_rows, block_src, rows = _dispatch_plan(idx, rank, cnt[:, 0].astype(jnp.int32))
    xs = _sc_dispatch(h2p, pos, rows)
    ys = _experts(xs, block_e, block_rows, block_src, w_gate, w_up, w_down, layer)
    yg = _sc_gather2(ys, pos)
    if final_g is None:
        return yg, wcol, mods, layer
    return tuple(_combine(x, mods, layer, segs, seg, yg, wcol, final_g) for seg in segs)


def _mla_rope_tables(t):
    axis_dim = MLA_ROPE // 2
    row = np.repeat(np.arange(t // GRID_W), GRID_W).astype(np.float64)
    col = np.tile(np.arange(GRID_W), t // GRID_W).astype(np.float64)
    inv = ROPE_BASE ** (-np.arange(0, axis_dim, 2, dtype=np.float64) / axis_dim)
    ar, ac = row[:, None] * inv[None, :], col[:, None] * inv[None, :]
    ones = np.ones((t, LANE - MLA_ROPE))
    cos = np.concatenate([np.cos(ar), np.cos(ar), np.cos(ac), np.cos(ac), ones], axis=-1)
    sin = np.concatenate([-np.sin(ar), np.sin(ar), -np.sin(ac), np.sin(ac), 0.0 * ones], axis=-1)
    return jnp.asarray(cos, F32), jnp.asarray(sin, F32)


def _ret_rot_tables(t, dk):
    inv = ROPE_BASE ** (-np.linspace(0.0, 1.0, dk // 2))
    ang = np.arange(t, dtype=np.float64)[:, None] * inv[None, :]
    return jnp.asarray(np.cos(ang), F32), jnp.asarray(np.sin(ang), F32)


def _dft_tables(n):
    k = np.arange(n, dtype=np.int64)
    ang = (np.outer(k, k) % n).astype(np.float64) * (2.0 * math.pi / n)
    return jnp.asarray(np.cos(ang), BF16), jnp.asarray(np.sin(ang), BF16)


def _mla_weights(w_in, q_g, kv_g, w_uq, w_ukv, w_o):
    d = w_in.shape[0]
    hd = MLA_NOPE + MLA_ROPE
    w_in_p = jnp.concatenate([w_in, jnp.zeros((d, LANE - MLA_ROPE), w_in.dtype)], axis=1)
    uq = w_uq.reshape(MLA_Q_LORA, MLA_HEADS, hd)
    uq = jnp.concatenate([uq, jnp.zeros((MLA_Q_LORA, MLA_HEADS, 2 * LANE - hd), uq.dtype)], axis=-1)
    ukv = w_ukv.reshape(MLA_KV_LORA, MLA_HEADS, MLA_NOPE + MLA_V)
    return {
        "w_in": w_in_p.astype(BF16),
        "q_g": q_g.reshape(1, -1) * (MLA_NOPE + MLA_ROPE) ** -0.5,
        "kv_g": kv_g.reshape(1, -1),
        "w_q": uq.reshape(MLA_Q_LORA, MLA_HEADS * 2 * LANE).astype(BF16),
        "w_kn": ukv[..., :MLA_NOPE].reshape(MLA_KV_LORA, MLA_HEADS * MLA_NOPE).astype(BF16),
        "w_v": ukv[..., MLA_NOPE:].reshape(MLA_KV_LORA, MLA_HEADS * MLA_V).astype(BF16),
        "w_o": w_o.astype(BF16),
    }


def kernel(x_prompt, x_sample, cache_mla, state_ret, c, c_ctx, norm1_g, norm2_g, ada_w, ada_b, final_norm_g,
           mla_w_in, mla_q_norm_g, mla_kv_norm_g, mla_w_uq, mla_w_ukv, mla_w_o, ret_w_in, ret_decay_f,
           ret_decay_b, ret_w_o, fnet_w, router_w, router_b, moe_w_gate, moe_w_up, moe_w_down):
    b_ctx, t_ctx, d = x_prompt.shape
    b_lat, t_lat, _ = x_sample.shape
    depth = ada_w.shape[0]
    assert b_lat + 1 <= 8
    n_ctx = b_ctx * t_ctx
    ctx = _Seg(0, b_ctx, t_ctx, 0, False)
    lat = _Seg(n_ctx, b_lat, t_lat, 1, True)
    segs = (ctx, lat)

    n_lat = b_lat * t_lat
    n_mla = mla_w_in.shape[0]
    assert n_mla >= 1
    x = None
    new_cache = jnp.zeros((b_ctx, n_mla, t_ctx, MLA_KV_LORA + MLA_ROPE), F32)
    cond8 = jnp.concatenate([c_ctx[None, :], c, jnp.zeros((8 - 1 - b_lat, d), F32)], axis=0)
    mods = _modulation_all(cond8, ada_w, ada_b).reshape(depth, 8, 6, 1, d)

    rw_t = router_w.T.astype(F32)
    rw_hi = rw_t.astype(BF16)
    rw_lo = (rw_t - rw_hi.astype(F32)).astype(BF16)
    rb = router_b.reshape(N_EXPERTS, 1).astype(F32)
    final_g = final_norm_g.reshape(1, d)
    dk = ret_w_in.shape[2] // (8 * RET_HEADS)
    dv = 2 * dk

    states = []
    pending = None
    counters = [0, 0, 0]
    for layer in range(depth):
        kind = layer % 3
        j = counters[kind]
        counters[kind] += 1
        g1 = norm1_g[layer].reshape(1, d)
        g2 = norm2_g[layer].reshape(1, d)
        if kind == 0:
            w = _mla_weights(mla_w_in[j], mla_q_norm_g[j], mla_kv_norm_g[j], mla_w_uq[j], mla_w_ukv[j],
                             mla_w_o[j])
            if x is None:
                xc, xc0, xl, xl0 = x_prompt.reshape(n_ctx, d), 0, x_sample.reshape(n_lat, d), 0
            else:
                xc, xc0, xl, xl0 = x, ctx.row0, x, lat.row0
            if pending is None:
                qc, kc, vc, new_cache = _mla_proj(xc, xc0, mods, layer, ctx, g1, w, None, (new_cache, j, n_mla))
                ql, kl, vl = _mla_proj(xl, xl0, mods, layer, lat, g1, w, _mla_rope_tables(t_lat))
            else:
                qc, kc, vc, new_cache, x = _mla_proj(x, ctx.row0, mods, layer, ctx, g1, w, None,
                                                     (new_cache, j, n_mla), pending)
                ql, kl, vl, x = _mla_proj(x, lat.row0, mods, layer, lat, g1, w, _mla_rope_tables(t_lat), None,
                                          pending)
                xc = xl = x
            past = cache_mla.shape[2]
            cpad = jnp.pad(cache_mla[:, j].reshape(b_lat * past, -1), ((0, 0), (0, LANE - MLA_ROPE)))
            kp, vp = _cache_kv(cpad, w)
            first = x is None
            x = _attention(xc, xc0, n_ctx + n_lat, jnp.zeros((n_ctx + n_lat, d), F32) if first else "inplace",
                           mods, layer, ctx, qc, [(kc, vc, t_ctx)], w["w_o"])
            x = _attention(xl if first else x, xl0, n_ctx + n_lat, x if first else "inplace", mods, layer, lat,
                           ql, [(kl, vl, t_lat), (kp, vp, past)], w["w_o"])
        elif kind == 1:
            w_in = ret_w_in[j]
            qk = RET_HEADS * dk
            k_scale = jnp.concatenate([jnp.ones((qk,), F32), jnp.full((qk,), dk ** -0.5, F32),
                                       jnp.ones((w_in.shape[1] - 2 * qk,), F32)])
            w_in_b = (w_in * k_scale[None, :]).astype(BF16)
            w_o_b = ret_w_o[j].astype(BF16)
            rot = _ret_rot_tables(t_lat, dk)
            n_qk, n_v = 2 * qk // RET_COL, RET_HEADS * dv // RET_COL
            assert (n_qk + n_v) * RET_COL * 2 == w_in.shape[1]
            parts = []
            for seg in segs:
                kinds = (("rotary" if seg is lat else "plain"),) * n_qk + ("plain",) * n_v
                qkv = _ret_proj(x, mods, layer, seg, g1, w_in_b, 0, kinds, rot, dk, pending)
                if pending is not None:
                    qkv, x = qkv
                parts.append((qkv, _ret_proj(x, mods, layer, seg, g1, w_in_b, 1, ("silu",) * (n_qk + n_v), None, dk)))
            yc, s_ctx = _ret_scan(*parts[0], ctx, ret_decay_f[j], ret_decay_b[j], None, True, dk, dv, RET_HEADS)
            yl, _ = _ret_scan(*parts[1], lat, ret_decay_f[j], ret_decay_b[j], state_ret[:, j], False, dk, dv, 1)
            x = _matmul_residual(x, mods, layer, ctx, yc, w_o_b)
            x = _matmul_residual(x, mods, layer, lat, yl, w_o_b)
            states.append(s_ctx)
        else:
            gd = d // FNET_GROUPS
            cc, sc = _dft_tables(gd)
            cs = jnp.concatenate([cc, sc], axis=1)
            w_b = fnet_w[j].astype(BF16)
            for seg in segs:
                ct, st = _dft_tables(seg.seq)
                if pending is None:
                    ac, as_ = _fnet_a(x, mods, layer, seg, g1, cs)
                else:
                    ac, as_, x = _fnet_a(x, mods, layer, seg, g1, cs, pending)
                x = _fnet_b(x, mods, layer, seg, ac, as_, ct, st, w_b, (seg.seq * gd) ** -0.5)
        if layer < depth - 1:
            pending = _moe(x, mods, layer, segs, g2, rw_hi, rw_lo, rb, moe_w_gate, moe_w_up, moe_w_down, None)
        else:
            y_prompt, y_sample = _moe(x, mods, layer, segs, g2, rw_hi, rw_lo, rb, moe_w_gate, moe_w_up,
                                      moe_w_down, final_g)

    new_state = jnp.stack(states, axis=1)
    return (y_prompt.reshape(b_ctx, t_ctx, d), y_sample.reshape(b_lat, t_lat, d), new_cache, new_state)
```

```python
import functools
import math

import jax
import jax.numpy as jnp
import numpy as np
from jax import lax
from jax.experimental import pallas as pl
from jax.experimental.pallas import tpu as pltpu
from jax.experimental.pallas import tpu_sc as plsc

F32 = jnp.float32
BF16 = jnp.bfloat16

GRID_W = 64
MLA_HEADS = 8
MLA_NOPE = 128
MLA_ROPE = 64
MLA_V = 128
MLA_Q_LORA = 384
MLA_KV_LORA = 256
ROPE_BASE = 10000.0
RET_HEADS = 4
RET_CHUNK = 256
FNET_GROUPS = 4
N_EXPERTS = 16
N_EXPERT_GROUPS = 4
EXPERTS_PER_GROUP = 4
D_EXPERT = 512
NORM_EPS = 1e-6

LANE = 128
PROJ_ROWS = 512
ATTN_ROWS = 512
FNET_ROWS = 512
ATTN_SHORT_SEQS = 4
WIDE_TILE = 1024
ROUTER_SUB_ROWS = 256
MOE_ROWS = 1024
MOE_TAIL_ROWS = 256
VMEM_LIMIT = 56 * 1024 * 1024
SC_CORES = 2
SC_SUBCORES = 16
SC_CHUNK = 128


def _cparams(*sem):
    return pltpu.CompilerParams(dimension_semantics=sem, vmem_limit_bytes=VMEM_LIMIT)


def _sigmoid(x):
    return 1.0 / (1.0 + jnp.exp(-x))


def _rms(x):
    return x * lax.rsqrt(jnp.mean(x * x, axis=-1, keepdims=True) + NORM_EPS)


def _modulate(x, g, shift, scale):
    return _rms(x) * (g * (1.0 + scale)) + shift


def _dot(a, b):
    return jnp.dot(a, b, preferred_element_type=F32)


def _dot_nt(a, b):
    return lax.dot_general(a, b, (((1,), (1,)), ((), ())), preferred_element_type=F32)


def _mod_kernel(c_ref, w_ref, b_ref, o_ref):
    c = c_ref[...]
    s = (c * _sigmoid(c)).astype(BF16)
    o_ref[...] = _dot(s, w_ref[...].astype(BF16)) + b_ref[...]


def _modulation_all(cond8, ada_w, ada_b):
    depth, d, d6 = ada_w.shape
    tn = d6 // 4
    return pl.pallas_call(
        _mod_kernel,
        grid=(depth, d6 // tn),
        in_specs=[
            pl.BlockSpec((8, d), lambda l, n: (0, 0)),
            pl.BlockSpec((None, d, tn), lambda l, n: (l, 0, n)),
            pl.BlockSpec((None, 1, tn), lambda l, n: (l, 0, n)),
        ],
        out_specs=pl.BlockSpec((None, 8, tn), lambda l, n: (l, 0, n)),
        out_shape=jax.ShapeDtypeStruct((depth, 8, d6), F32),
        compiler_params=_cparams("parallel", "parallel"),
        name="modulation",
    )(cond8, ada_w, ada_b.reshape(depth, 1, d6))


class _Seg:
    def __init__(self, row0, batch, seq, mod0, per_batch_mod):
        self.row0, self.batch, self.seq = row0, batch, seq
        self.mod0, self.per_batch_mod = mod0, per_batch_mod
        self.rows = batch * seq

    def tile(self, want):
        tm = min(want, self.seq) if self.per_batch_mod else want
        assert self.rows % tm == 0 and self.row0 % tm == 0 and (self.seq % tm == 0 or tm % self.seq == 0)
        return tm

    def seq_tile(self, want):
        tm = min(want, self.seq)
        assert self.seq % tm == 0 and self.row0 % tm == 0
        return tm

    def mod_row(self, tile, tm):
        if self.per_batch_mod:
            return self.mod0 + tile * tm // self.seq
        return self.mod0


def _mod_spec(layer, seg, d, tm, tile_of=lambda *a: a[0]):
    return pl.BlockSpec((None, None, 6, 1, d), lambda *a: (layer, seg.mod_row(tile_of(*a), tm), 0, 0, 0))


def _unpack_halves(p):
    lo = lax.bitcast_convert_type(p << 16, F32)
    hi = lax.bitcast_convert_type(p & jnp.uint32(0xFFFF0000), F32)
    return lo, hi


def _moe_residual(y_ref, w_ref, gate):
    w = w_ref[...]
    lo0, hi0 = _unpack_halves(y_ref[0])
    lo1, hi1 = _unpack_halves(y_ref[1])
    w0, w1 = w[:, 0:1], w[:, 1:2]
    return gate * jnp.concatenate([w0 * lo0 + w1 * lo1, w0 * hi0 + w1 * hi1], axis=-1)


def _take_pending(refs, pending):
    if not pending:
        return refs, lambda x: x
    y_ref, w_ref, pm_ref, *rest = refs
    xo_ref = rest.pop()

    def resolve(x):
        x = x + _moe_residual(y_ref, w_ref, pm_ref[5])
        xo_ref[...] = x
        return x

    return rest, resolve


def _pending_io(pending, seg, tm, d, n):
    yg_by_seg, wcol, mods, layer = pending
    t0 = seg.row0 // tm
    specs = [pl.BlockSpec((2, tm, d // 2), lambda i: (0, i, 0)),
             pl.BlockSpec((tm, 8), lambda i: (t0 + i, 0)),
             _mod_spec(layer, seg, d, tm)]
    return (specs, [yg_by_seg[seg], wcol, mods], pl.BlockSpec((tm, d), lambda i: (t0 + i, 0)),
            jax.ShapeDtypeStruct((n, d), F32))


def _rope_partner(x):
    lane = lax.broadcasted_iota(jnp.int32, x.shape, 1)
    first = (lane % 32) < 16
    return jnp.where(first, pltpu.roll(x, LANE - 16, 1), pltpu.roll(x, 16, 1))


def _store_values(v_ref, v):
    ones = jnp.ones((v.shape[0], LANE), BF16)
    for hd in range(MLA_HEADS):
        v_ref[:, hd * 2 * LANE:hd * 2 * LANE + LANE] = v[:, hd * MLA_V:(hd + 1) * MLA_V].astype(BF16)
        v_ref[:, hd * 2 * LANE + LANE:(hd + 1) * 2 * LANE] = ones


def _mla_proj_kernel(*refs, rope, pending):
    refs, resolve = _take_pending(refs, pending)
    if rope:
        (x_ref, m_ref, g_ref, win_ref, qg_ref, kvg_ref, wq_ref, wkn_ref, wv_ref, cos_ref, sin_ref,
         q_ref, k_ref, v_ref) = refs
    else:
        x_ref, m_ref, g_ref, win_ref, qg_ref, kvg_ref, wq_ref, wkn_ref, wv_ref = refs[:9]
        q_ref, k_ref, v_ref, cache_ref = refs[-4:]
    h = _modulate(resolve(x_ref[...]), g_ref[...], m_ref[0], m_ref[1]).astype(BF16)
    z = _dot(h, win_ref[...])
    cq = z[:, :MLA_Q_LORA]
    ckv = z[:, MLA_Q_LORA:MLA_Q_LORA + MLA_KV_LORA]
    kpe = z[:, MLA_Q_LORA + MLA_KV_LORA:]
    cqn = (_rms(cq) * qg_ref[...]).astype(BF16)
    ckvn = _rms(ckv) * kvg_ref[...]
    ckvb = ckvn.astype(BF16)
    q = _dot(cqn, wq_ref[...])
    kn = _dot(ckvb, wkn_ref[...])
    _store_values(v_ref, _dot(ckvb, wv_ref[...]))
    if rope:
        cos, sin = cos_ref[...], sin_ref[...]
        kpe = kpe * cos + _rope_partner(kpe) * sin
    else:
        seq = cache_ref.shape[1]
        for s in range(cache_ref.shape[0]):
            cache_ref[s, :, :MLA_KV_LORA] = ckvn[s * seq:(s + 1) * seq, :]
            cache_ref[s, :, MLA_KV_LORA:] = kpe[s * seq:(s + 1) * seq, :MLA_ROPE]
    kpe_b = kpe.astype(BF16)
    for hd in range(MLA_HEADS):
        lo = hd * 2 * LANE
        q_ref[:, lo:lo + LANE] = q[:, lo:lo + LANE].astype(BF16)
        qr = q[:, lo + LANE:lo + 2 * LANE]
        if rope:
            qr = qr * cos + _rope_partner(qr) * sin
        q_ref[:, lo + LANE:lo + 2 * LANE] = qr.astype(BF16)
        k_ref[:, lo:lo + LANE] = kn[:, hd * LANE:(hd + 1) * LANE].astype(BF16)
        k_ref[:, lo + LANE:lo + 2 * LANE] = kpe_b


def _mla_proj(x, x_row0, mods, layer, seg, g1, w, rope_tabs, cache_slot=None, pending=None):
    n, d = x.shape
    rope = rope_tabs is not None
    rows = seg.rows
    tm = seg.tile(PROJ_ROWS)
    x_tile0 = x_row0 // tm
    hq = MLA_HEADS * 2 * LANE
    const = lambda i: (0, 0)
    aliases = {}
    in_specs = [
        pl.BlockSpec((tm, d), lambda i: (x_tile0 + i, 0)),
        _mod_spec(layer, seg, d, tm),
        pl.BlockSpec((1, d), const),
        pl.BlockSpec(w["w_in"].shape, const),
        pl.BlockSpec((1, MLA_Q_LORA), const),
        pl.BlockSpec((1, MLA_KV_LORA), const),
        pl.BlockSpec(w["w_q"].shape, const),
        pl.BlockSpec(w["w_kn"].shape, const),
        pl.BlockSpec(w["w_v"].shape, const),
    ]
    args = [x, mods, g1, w["w_in"], w["q_g"], w["kv_g"], w["w_q"], w["w_kn"], w["w_v"]]
    out_specs = [pl.BlockSpec((tm, hq), lambda i: (i, 0))] * 3
    out_shape = [jax.ShapeDtypeStruct((rows, hq), BF16)] * 3
    if rope:
        tab = pl.BlockSpec((tm, LANE), lambda i: (i % (seg.seq // tm), 0))
        in_specs += [tab, tab]
        args += list(rope_tabs)
    else:
        cw = MLA_KV_LORA + MLA_ROPE
        prev, slot, n_slots = cache_slot
        assert tm % seg.seq == 0
        out_specs.append(pl.BlockSpec((tm // seg.seq, None, seg.seq, cw), lambda i: (i, slot, 0, 0)))
        out_shape.append(jax.ShapeDtypeStruct((seg.batch, n_slots, seg.seq, cw), F32))
        in_specs.append(pl.BlockSpec(memory_space=pl.ANY))
        args.append(prev)
        aliases = {len(args) - 1: 3}
    if pending is not None:
        p_specs, p_args, xo_spec, xo_shape = _pending_io(pending, seg, tm, d, n)
        in_specs, args = p_specs + in_specs, p_args + args
        out_specs.append(xo_spec)
        out_shape.append(xo_shape)
        aliases = {k + len(p_args): v for k, v in aliases.items()}
        aliases[len(p_args)] = len(out_shape) - 1
    return pl.pallas_call(
        functools.partial(_mla_proj_kernel, rope=rope, pending=pending is not None),
        grid=(rows // tm,),
        in_specs=in_specs,
        out_specs=out_specs,
        out_shape=out_shape,
        input_output_aliases=aliases,
        compiler_params=_cparams("parallel"),
        name="mla_proj_lat" if rope else "mla_proj_ctx",
    )(*args)


def _cache_kv_kernel(c_ref, wkn_ref, wv_ref, k_ref, v_ref):
    c = c_ref[...]
    ckv = c[:, :MLA_KV_LORA].astype(BF16)
    kpe_b = c[:, MLA_KV_LORA:].astype(BF16)
    kn = _dot(ckv, wkn_ref[...])
    _store_values(v_ref, _dot(ckv, wv_ref[...]))
    for hd in range(MLA_HEADS):
        lo = hd * 2 * LANE
        k_ref[:, lo:lo + LANE] = kn[:, hd * LANE:(hd + 1) * LANE].astype(BF16)
        k_ref[:, lo + LANE:lo + 2 * LANE] = kpe_b


def _cache_kv(cache_pad, w):
    rows, cw = cache_pad.shape
    hq = MLA_HEADS * 2 * LANE
    const = lambda i: (0, 0)
    tm = min(PROJ_ROWS, rows)
    assert rows % tm == 0
    return pl.pallas_call(
        _cache_kv_kernel,
        grid=(rows // tm,),
        in_specs=[
            pl.BlockSpec((tm, cw), lambda i: (i, 0)),
            pl.BlockSpec(w["w_kn"].shape, const),
            pl.BlockSpec(w["w_v"].shape, const),
        ],
        out_specs=[pl.BlockSpec((tm, hq), lambda i: (i, 0))] * 2,
        out_shape=[jax.ShapeDtypeStruct((rows, hq), BF16)] * 2,
        compiler_params=_cparams("parallel"),
        name="mla_cache_kv",
    )(cache_pad, w["w_kn"], w["w_v"])


def _attn_kernel(*refs, n_parts, n_seq):
    q_ref = refs[0]
    kv_refs = refs[1:1 + 2 * n_parts]
    wo_ref, x_ref, m_ref = refs[1 + 2 * n_parts:4 + 2 * n_parts]
    o_ref, acc_ref = refs[-2:]
    tq = q_ref.shape[0] // n_seq
    for sq, hd in [(sq, hd) for sq in range(n_seq) for hd in range(MLA_HEADS)]:
        rows = slice(sq * tq, (sq + 1) * tq)
        kcol = slice(hd * 2 * LANE, (hd + 1) * 2 * LANE)
        keys = [slice(sq * (r.shape[0] // n_seq), (sq + 1) * (r.shape[0] // n_seq)) for r in kv_refs[::2]]
        scores = [_dot_nt(q_ref[rows, kcol], kv_refs[2 * p][keys[p], kcol]) for p in range(n_parts)]
        mx = scores[0].max(axis=-1, keepdims=True)
        for s in scores[1:]:
            mx = jnp.maximum(mx, s.max(axis=-1, keepdims=True))
        out = None
        for p, s in enumerate(scores):
            e = jnp.exp((s - mx).astype(BF16))
            pv = _dot(e, kv_refs[2 * p + 1][keys[p], kcol])
            out = pv if out is None else out + pv
        acc_ref[rows, hd * MLA_V:(hd + 1) * MLA_V] = (out[:, :MLA_V] / out[:, LANE:LANE + MLA_V]).astype(BF16)
    y = _dot(acc_ref[...], wo_ref[...])
    o_ref[...] = x_ref[...] + m_ref[2] * y


def _attention(x, x_row0, n, dest, mods, layer, seg, q, kv_parts, w_o):
    d = x.shape[1]
    hq = MLA_HEADS * 2 * LANE
    hv = MLA_HEADS * MLA_V
    tq = seg.seq_tile(ATTN_ROWS)
    tps = seg.seq // tq
    n_seq = ATTN_SHORT_SEQS if (tps == 1 and not seg.per_batch_mod and seg.batch % ATTN_SHORT_SEQS == 0) else 1
    tq *= n_seq
    x_tile0, out_tile0 = x_row0 // tq, seg.row0 // tq
    in_specs = [pl.BlockSpec((tq, hq), lambda b, i: (b * tps + i, 0))]
    args = [q]
    for k, v, rows in kv_parts:
        mode = dict(pipeline_mode=pl.Buffered(1)) if tps > 1 else {}
        in_specs += [pl.BlockSpec((n_seq * rows, hq), lambda b, i: (b, 0), **mode)] * 2
        args += [k, v]
    in_specs += [
        pl.BlockSpec(w_o.shape, lambda b, i: (0, 0)),
        pl.BlockSpec((tq, d), lambda b, i: (x_tile0 + b * tps + i, 0)),
        _mod_spec(layer, seg, d, tq // n_seq, tile_of=lambda b, i: (b * tps + i) * n_seq),
    ]
    args += [w_o, x, mods]
    if isinstance(dest, str):
        assert dest == "inplace"
        aliases = {len(args) - 2: 0}
    else:
        in_specs.append(pl.BlockSpec(memory_space=pl.ANY))
        args.append(dest)
        aliases = {len(args) - 1: 0}
    return pl.pallas_call(
        functools.partial(_attn_kernel, n_parts=len(kv_parts), n_seq=n_seq),
        grid=(seg.batch // n_seq, tps),
        in_specs=in_specs,
        out_specs=pl.BlockSpec((tq, d), lambda b, i: (out_tile0 + b * tps + i, 0)),
        out_shape=jax.ShapeDtypeStruct((n, d), F32),
        scratch_shapes=[pltpu.VMEM((tq, hv), BF16)],
        input_output_aliases=aliases,
        compiler_params=_cparams("parallel", "arbitrary"),
        name="mla_attention",
    )(*args)


RET_COL = 1024


RET_ROWS = 1024


def _ret_proj_kernel(*refs, kinds, dk, pending):
    refs, resolve = _take_pending(refs, pending)
    rotary = "rotary" in kinds
    if rotary:
        x_ref, m_ref, g_ref, w_ref, cos_ref, sin_ref, z_ref = refs
        cos, sin = cos_ref[...], sin_ref[...]
    else:
        x_ref, m_ref, g_ref, w_ref, z_ref = refs
    h = _modulate(resolve(x_ref[...]), g_ref[...], m_ref[0], m_ref[1]).astype(BF16)
    half = dk // 2
    for j, kind in enumerate(kinds):
        c0 = j * RET_COL
        acc = _dot(h, w_ref[:, c0:c0 + RET_COL])
        if kind == "rotary":
            for hd in range(RET_COL // dk):
                lo = hd * dk
                x1, x2 = acc[:, lo:lo + half], acc[:, lo + half:lo + dk]
                z_ref[:, c0 + lo:c0 + lo + half] = (x1 * cos - x2 * sin).astype(BF16)
                z_ref[:, c0 + lo + half:c0 + lo + dk] = (x1 * sin + x2 * cos).astype(BF16)
        elif kind == "silu":
            z_ref[:, c0:c0 + RET_COL] = (acc * _sigmoid(acc)).astype(BF16)
        else:
            z_ref[:, c0:c0 + RET_COL] = acc.astype(BF16)


def _ret_proj(x, mods, layer, seg, g1, w_in, group, kinds, rot_tabs, dk, pending=None):
    n, d = x.shape
    rows = seg.rows
    tm = seg.tile(RET_ROWS)
    ncol = len(kinds) * RET_COL
    tps = max(seg.seq // tm, 1)
    in_specs = [
        pl.BlockSpec((tm, d), lambda i: (seg.row0 // tm + i, 0)),
        _mod_spec(layer, seg, d, tm),
        pl.BlockSpec((1, d), lambda i: (0, 0)),
        pl.BlockSpec((d, ncol), lambda i: (0, group), pipeline_mode=pl.Buffered(1)),
    ]
    args = [x, mods, g1, w_in]
    if "rotary" in kinds:
        tab = pl.BlockSpec((tm, dk // 2), lambda i: (i % tps, 0))
        in_specs += [tab, tab]
        args += list(rot_tabs)
    out_specs = [pl.BlockSpec((tm, ncol), lambda i: (i, 0))]
    out_shape = [jax.ShapeDtypeStruct((rows, ncol), BF16)]
    aliases = {}
    if pending is not None:
        p_specs, p_args, xo_spec, xo_shape = _pending_io(pending, seg, tm, d, n)
        in_specs, args = p_specs + in_specs, p_args + args
        out_specs.append(xo_spec)
        out_shape.append(xo_shape)
        aliases = {len(p_args): 1}
    res = pl.pallas_call(
        functools.partial(_ret_proj_kernel, kinds=kinds, dk=dk, pending=pending is not None),
        grid=(rows // tm,),
        in_specs=in_specs,
        out_specs=out_specs,
        out_shape=out_shape,
        input_output_aliases=aliases,
        compiler_params=_cparams("parallel"),
        name="ret_proj_" + kinds[0],
    )(*args)
    return res if pending is not None else res[0]


def _log_sigmoid(x):
    return jnp.minimum(x, 0.0) - jnp.log(1.0 + jnp.exp(-jnp.abs(x)))


def _ret_scan_kernel(*refs, has_s0, emit_state, n_chunks, heads):
    refs = list(refs)
    lf_ref, lb_ref, q_ref, k_ref, v_ref, gf_ref, gb_ref = refs[:7]
    pos = 7
    s0_ref = None
    if has_s0:
        s0_ref = refs[pos]
        pos += 1
    y_ref = refs[pos]
    pos += 1
    sout_ref = None
    if emit_state:
        sout_ref = refs[pos]
        pos += 1
    s_ref, yf_ref = refs[pos:]
    c = RET_CHUNK
    dk, dv = s_ref.shape
    ii = lax.broadcasted_iota(jnp.int32, (c, c), 0).astype(F32)
    jj = lax.broadcasted_iota(jnp.int32, (c, c), 1).astype(F32)
    idx = lax.broadcasted_iota(jnp.int32, (c, 1), 0).astype(F32)

    for hd, direction in [(hd, direction) for hd in range(heads) for direction in range(2)]:
        fwd = direction == 0
        kcol, vcol = slice(hd * dk, (hd + 1) * dk), slice(hd * dv, (hd + 1) * dv)
        lg = _log_sigmoid((lf_ref if fwd else lb_ref)[hd])
        rel = (ii - jj) if fwd else (jj - ii)
        keep = rel >= 0
        decay_in = jnp.where(keep, jnp.exp(jnp.where(keep, rel, 0.0) * lg), 0.0)
        decay_q = jnp.exp(((idx + 1.0) if fwd else (c - idx)) * lg)
        decay_k = jnp.exp(((c - 1.0 - idx) if fwd else idx) * lg)
        decay_c = jnp.exp(c * lg)
        g_ref = gf_ref if fwd else gb_ref

        def chunk(cc, state, fwd=fwd, decay_in=decay_in, decay_q=decay_q, decay_k=decay_k, decay_c=decay_c,
                  g_ref=g_ref, kcol=kcol, vcol=vcol):
            r0 = cc * c if isinstance(cc, int) else pl.multiple_of(cc * c, c)
            qc = q_ref[pl.ds(r0, c), kcol]
            kc = k_ref[pl.ds(r0, c), kcol]
            vc = v_ref[pl.ds(r0, c), vcol]
            sc = _dot_nt(qc, kc) * decay_in
            out = _dot(sc.astype(BF16), vc)
            kd_t = (kc.astype(F32) * decay_k).T.astype(BF16)
            new_s = _dot(kd_t, vc)
            if state is not None:
                out = out + decay_q * _dot(qc, state.astype(BF16))
                new_s = decay_c * state + new_s
            s_ref[...] = new_s
            o = _rms(out) * g_ref[pl.ds(r0, c), vcol].astype(F32)
            if fwd:
                yf_ref[pl.ds(r0, c), :] = o
            else:
                y_ref[pl.ds(r0, c), vcol] = (yf_ref[pl.ds(r0, c), :] + o).astype(BF16)

        chunk(0 if fwd else n_chunks - 1, s0_ref[direction, hd] if has_s0 else None)

        def step(ci, carry, fwd=fwd, chunk=chunk):
            chunk(ci if fwd else n_chunks - 1 - ci, s_ref[...])
            return carry

        lax.fori_loop(1, n_chunks, step, 0, unroll=True)
        if emit_state:
            sout_ref[direction, hd] = s_ref[...]


def _ret_scan(qkv, g, seg, logit_f, logit_b, s0, emit_state, dk, dv, heads):
    rows = seg.batch * seg.seq
    t = seg.seq
    hh = RET_HEADS
    groups = hh // heads
    v0 = 2 * hh * dk // (heads * dv)
    assert groups * heads == hh and v0 * heads * dv == 2 * hh * dk
    in_specs = [
        pl.BlockSpec((heads, 1, 1), lambda b, h: (h, 0, 0)),
        pl.BlockSpec((heads, 1, 1), lambda b, h: (h, 0, 0)),
        pl.BlockSpec((t, heads * dk), lambda b, h: (b, h)),
        pl.BlockSpec((t, heads * dk), lambda b, h: (b, groups + h)),
        pl.BlockSpec((t, heads * dv), lambda b, h: (b, v0 + h)),
        pl.BlockSpec((t, heads * dv), lambda b, h: (b, h)),
        pl.BlockSpec((t, heads * dv), lambda b, h: (b, groups + h)),
    ]
    args = [logit_f.reshape(hh, 1, 1), logit_b.reshape(hh, 1, 1), qkv, qkv, qkv, g, g]
    state_spec = pl.BlockSpec((None, 2, heads, dk, dv), lambda b, h: (b, 0, h, 0, 0))
    if s0 is not None:
        in_specs.append(state_spec)
        args.append(s0)
    out_specs = [pl.BlockSpec((t, heads * dv), lambda b, h: (b, h))]
    out_shape = [jax.ShapeDtypeStruct((rows, hh * dv), BF16)]
    if emit_state:
        out_specs.append(state_spec)
        out_shape.append(jax.ShapeDtypeStruct((seg.batch, 2, hh, dk, dv), F32))
    res = pl.pallas_call(
        functools.partial(_ret_scan_kernel, has_s0=s0 is not None, emit_state=emit_state,
                          n_chunks=t // RET_CHUNK, heads=heads),
        grid=(seg.batch, groups),
        in_specs=in_specs,
        out_specs=out_specs,
        out_shape=out_shape,
        scratch_shapes=[pltpu.VMEM((dk, dv), F32), pltpu.VMEM((t, dv), F32)],
        compiler_params=_cparams("parallel", "parallel"),
        name="ret_scan",
    )(*args)
    return res if emit_state else (res[0], None)


def _mm_res_kernel(a_ref, w_ref, x_ref, m_ref, o_ref):
    o_ref[...] = x_ref[...] + m_ref[2] * _dot(a_ref[...], w_ref[...])


def _matmul_residual(x, mods, layer, seg, a, w):
    n, d = x.shape
    tm = seg.tile(RET_ROWS)
    x_spec = pl.BlockSpec((tm, d), lambda i: (seg.row0 // tm + i, 0))
    return pl.pallas_call(
        _mm_res_kernel,
        grid=(seg.rows // tm,),
        in_specs=[
            pl.BlockSpec((tm, a.shape[1]), lambda i: (i, 0)),
            pl.BlockSpec(w.shape, lambda i: (0, 0)),
            x_spec,
            _mod_spec(layer, seg, d, tm),
        ],
        out_specs=x_spec,
        out_shape=jax.ShapeDtypeStruct((n, d), F32),
        input_output_aliases={2: 0},
        compiler_params=_cparams("parallel"),
        name="matmul_residual",
    )(a, w, x, mods)


def _fnet_a_kernel(*refs, gd, pending):
    refs, resolve = _take_pending(refs, pending)
    x_ref, m_ref, g_ref, cs_ref, ac_ref, as_ref = refs
    h = _modulate(resolve(x_ref[...]), g_ref[...], m_ref[0], m_ref[1]).astype(BF16)
    cs = cs_ref[...]
    for g in range(FNET_GROUPS):
        a = _dot(h[:, g * gd:(g + 1) * gd], cs)
        ac_ref[:, g * gd:(g + 1) * gd] = a[:, :gd].astype(BF16)
        as_ref[:, g * gd:(g + 1) * gd] = a[:, gd:].astype(BF16)


def _fnet_a(x, mods, layer, seg, g1, cs, pending=None):
    n, d = x.shape
    rows = seg.rows
    tm = seg.tile(PROJ_ROWS)
    out = pl.BlockSpec((tm, d), lambda i: (i, 0))
    in_specs = [
        pl.BlockSpec((tm, d), lambda i: (seg.row0 // tm + i, 0)),
        _mod_spec(layer, seg, d, tm),
        pl.BlockSpec((1, d), lambda i: (0, 0)),
        pl.BlockSpec(cs.shape, lambda i: (0, 0)),
    ]
    args = [x, mods, g1, cs]
    out_specs, out_shape, aliases = [out, out], [jax.ShapeDtypeStruct((rows, d), BF16)] * 2, {}
    if pending is not None:
        p_specs, p_args, xo_spec, xo_shape = _pending_io(pending, seg, tm, d, n)
        in_specs, args = p_specs + in_specs, p_args + args
        out_specs.append(xo_spec)
        out_shape.append(xo_shape)
        aliases = {len(p_args): 2}
    return pl.pallas_call(
        functools.partial(_fnet_a_kernel, gd=d // FNET_GROUPS, pending=pending is not None),
        grid=(rows // tm,),
        in_specs=in_specs,
        out_specs=out_specs,
        out_shape=out_shape,
        input_output_aliases=aliases,
        compiler_params=_cparams("parallel"),
        name="fnet_channel_dft",
    )(*args)


def _fnet_b_kernel(ct_ref, st_ref, ac_ref, as_ref, w_ref, x_ref, m_ref, o_ref, *, norm):
    f = (_dot(ct_ref[...], ac_ref[...]) - _dot(st_ref[...], as_ref[...])) * norm
    o_ref[...] = x_ref[...] + m_ref[2] * _dot(f.astype(BF16), w_ref[...])


def _fnet_b(x, mods, layer, seg, ac, as_, ct, st, w, norm):
    n, d = x.shape
    t = seg.seq
    tq = seg.seq_tile(FNET_ROWS)
    tps = t // tq
    x_spec = pl.BlockSpec((tq, d), lambda b, i: (seg.row0 // tq + b * tps + i, 0))
    tab = pl.BlockSpec((tq, t), lambda b, i: (i, 0))
    seq = pl.BlockSpec((t, d), lambda b, i: (b, 0))
    return pl.pallas_call(
        functools.partial(_fnet_b_kernel, norm=norm),
        grid=(seg.batch, tps),
        in_specs=[tab, tab, seq, seq, pl.BlockSpec(w.shape, lambda b, i: (0, 0)), x_spec,
                  _mod_spec(layer, seg, d, tq, tile_of=lambda b, i: b * tps + i)],
        out_specs=x_spec,
        out_shape=jax.ShapeDtypeStruct((n, d), F32),
        input_output_aliases={5: 0},
        compiler_params=_cparams("parallel", "arbitrary"),
        name="fnet_position_dft",
    )(ct, st, ac, as_, w, x, mods)


def _pack_halves(a):
    w = a.shape[1] // 2
    bits = lambda v: lax.bitcast_convert_type(v.astype(BF16).astype(F32), jnp.uint32)
    return (bits(a[:, :w]) >> 16) | (bits(a[:, w:]) & jnp.uint32(0xFFFF0000))


def _router_kernel(x_ref, m_ref, g_ref, rwhi_ref, rwlo_ref, rb_ref, h_ref, idx_ref, rank_ref, wcol_ref, cnt_ref,
                   run_ref, tri_ref):
    step = pl.program_id(0)

    @pl.when(step == 0)
    def _():
        run_ref[...] = jnp.zeros_like(run_ref)
        tt = tri_ref.shape[0]
        earlier = lax.broadcasted_iota(jnp.int32, (tt, tt), 0) < lax.broadcasted_iota(jnp.int32, (tt, tt), 1)
        tri_ref[...] = jnp.where(earlier, 1.0, 0.0).astype(BF16)

    parts = []
    for r0 in range(0, x_ref.shape[0], ROUTER_SUB_ROWS):
        rows = slice(r0, r0 + ROUTER_SUB_ROWS)
        h = _modulate(x_ref[rows, :], g_ref[...], m_ref[3], m_ref[4])
        h_ref[rows, :] = _pack_halves(h)
        h_hi = h.astype(BF16)
        h_lo = (h - h_hi.astype(F32)).astype(BF16)
        parts.append(_dot_nt(rwhi_ref[...], h_hi) + (_dot_nt(rwhi_ref[...], h_lo) + _dot_nt(rwlo_ref[...], h_hi)))
    logits = jnp.concatenate(parts, axis=1)
    sc = _sigmoid(logits)
    gr = sc + rb_ref[...]
    gp = EXPERTS_PER_GROUP
    row = lambda a, e: a[e:e + 1, :]
    best_g = None
    for g in range(N_EXPERT_GROUPS):
        vals = [row(gr, g * gp + i) for i in range(gp)]
        gs = None
        for i in range(gp):
            for j in range(i + 1, gp):
                pair = vals[i] + vals[j]
                gs = pair if gs is None else jnp.maximum(gs, pair)
        if best_g is None:
            best_g, best_v = jnp.zeros(gs.shape, jnp.int32), gs
        else:
            better = gs > best_v
            best_g = jnp.where(better, g, best_g)
            best_v = jnp.where(better, gs, best_v)
    sel, raw = [], []
    for i in range(gp):
        s_i, r_i = row(gr, i), row(sc, i)
        for g in range(1, N_EXPERT_GROUPS):
            s_i = jnp.where(best_g == g, row(gr, g * gp + i), s_i)
            r_i = jnp.where(best_g == g, row(sc, g * gp + i), r_i)
        sel.append(s_i)
        raw.append(r_i)

    def argmax_first(vals, raws):
        bi, bv, br = jnp.zeros(vals[0].shape, jnp.int32), vals[0], raws[0]
        for i in range(1, len(vals)):
            better = vals[i] > bv
            bi = jnp.where(better, i, bi)
            bv = jnp.where(better, vals[i], bv)
            br = jnp.where(better, raws[i], br)
        return bi, br

    i1, w1 = argmax_first(sel, raw)
    masked = [jnp.where(i1 == i, -jnp.inf, sel[i]) for i in range(gp)]
    i2, w2 = argmax_first(masked, raw)
    tot = w1 + w2
    e1 = best_g * gp + i1
    e2 = best_g * gp + i2
    idx_ref[0:1, :] = e1
    idx_ref[1:2, :] = e2
    t = e1.shape[1]
    sub = lax.broadcasted_iota(jnp.int32, (8, t), 0)
    w8 = jnp.where(sub == 0, w1 / tot, jnp.where(sub == 1, w2 / tot, 0.0))
    wcol_ref[...] = w8.T
    eio = lax.broadcasted_iota(jnp.int32, (N_EXPERTS, t), 0)
    oh1, oh2 = eio == e1, eio == e2
    oh = jnp.where(oh1, 1.0, jnp.where(oh2, 1.0, 0.0))
    local = _dot(oh.astype(BF16), tri_ref[...])
    rank = local + run_ref[:, 0:1]
    rank_ref[0:1, :] = jnp.sum(jnp.where(oh1, rank, 0.0), axis=0, keepdims=True).astype(jnp.int32)
    rank_ref[1:2, :] = jnp.sum(jnp.where(oh2, rank, 0.0), axis=0, keepdims=True).astype(jnp.int32)
    run_ref[...] = run_ref[...] + jnp.sum(oh, axis=1, keepdims=True)
    cnt_ref[...] = run_ref[...]


def _wide_mod_row(segs, tm):
    ctx, lat = segs
    ctx_tiles = ctx.batch * ctx.seq // tm
    assert ctx_tiles * tm == ctx.batch * ctx.seq and lat.seq % tm == 0
    return lambda i: jnp.where(i < ctx_tiles, ctx.mod0, lat.mod0 + (i - ctx_tiles) // (lat.seq // tm))


def _router(x, mods, layer, segs, g2, rw_hi, rw_lo, rb):
    n, d = x.shape
    tm = WIDE_TILE
    mod_row = _wide_mod_row(segs, tm)
    return pl.pallas_call(
        _router_kernel,
        grid=(n // tm,),
        in_specs=[
            pl.BlockSpec((tm, d), lambda i: (i, 0)),
            pl.BlockSpec((None, None, 6, 1, d), lambda i: (layer, mod_row(i), 0, 0, 0)),
            pl.BlockSpec((1, d), lambda i: (0, 0)),
            pl.BlockSpec(rw_hi.shape, lambda i: (0, 0)),
            pl.BlockSpec(rw_lo.shape, lambda i: (0, 0)),
            pl.BlockSpec(rb.shape, lambda i: (0, 0)),
        ],
        out_specs=[
            pl.BlockSpec((tm, d // 2), lambda i: (i, 0)),
            pl.BlockSpec((2, tm), lambda i: (0, i)),
            pl.BlockSpec((2, tm), lambda i: (0, i)),
            pl.BlockSpec((tm, 8), lambda i: (i, 0)),
            pl.BlockSpec((N_EXPERTS, LANE), lambda i: (0, 0)),
        ],
        out_shape=[
            jax.ShapeDtypeStruct((n, d // 2), jnp.uint32),
            jax.ShapeDtypeStruct((2, n), jnp.int32),
            jax.ShapeDtypeStruct((2, n), jnp.int32),
            jax.ShapeDtypeStruct((n, 8), F32),
            jax.ShapeDtypeStruct((N_EXPERTS, LANE), F32),
        ],
        scratch_shapes=[pltpu.VMEM((N_EXPERTS, LANE), F32), pltpu.VMEM((tm, tm), BF16)],
        compiler_params=_cparams("arbitrary"),
        name="moe_router",
    )(x, mods, g2, rw_hi, rw_lo, rb)


def _expert_kernel(be_ref, br_ref, bs_ref, xs_ref, wg_ref, wu_ref, wd_ref, y_ref, wg_b, wu_b, wd_b):
    i = pl.program_id(0)
    prev = be_ref[jnp.maximum(i - 1, 0)]
    valid = br_ref[i]
    tail_rows = MOE_TAIL_ROWS

    @pl.when(jnp.logical_or(i == 0, be_ref[i] != prev))
    def _():
        wg_b[...] = wg_ref[...].astype(BF16)
        wu_b[...] = wu_ref[...].astype(BF16)
        wd_b[...] = wd_ref[...].astype(BF16)

    def ffn(r0, nrows):
        lo, hi = _unpack_halves(xs_ref[r0:r0 + nrows, :])
        xb = jnp.concatenate([lo.astype(BF16), hi.astype(BF16)], axis=1)
        gate = _dot(xb, wg_b[...])
        hid = (gate * _sigmoid(gate)) * _dot(xb, wu_b[...])
        y_ref[r0:r0 + nrows, :] = _pack_halves(_dot(hid.astype(BF16), wd_b[...]))

    @pl.when(valid > tail_rows)
    def _():
        ffn(0, y_ref.shape[0])

    @pl.when(jnp.logical_and(valid > 0, valid <= tail_rows))
    def _():
        ffn(0, tail_rows)
        y_ref[tail_rows:, :] = jnp.zeros((y_ref.shape[0] - tail_rows, y_ref.shape[1]), y_ref.dtype)


def _experts(xs, block_e, block_rows, block_src, w_gate, w_up, w_down, layer):
    rows, half = xs.shape
    d = 2 * half
    de = w_gate.shape[-1]
    n_blocks = rows // MOE_ROWS
    grid_spec = pltpu.PrefetchScalarGridSpec(
        num_scalar_prefetch=3,
        grid=(n_blocks,),
        in_specs=[
            pl.BlockSpec((MOE_ROWS, half), lambda i, be, br, bs: (bs[i], 0)),
            pl.BlockSpec((None, None, d, de), lambda i, be, br, bs: (layer, be[i], 0, 0)),
            pl.BlockSpec((None, None, d, de), lambda i, be, br, bs: (layer, be[i], 0, 0)),
            pl.BlockSpec((None, None, de, d), lambda i, be, br, bs: (layer, be[i], 0, 0)),
        ],
        out_specs=pl.BlockSpec((MOE_ROWS, half), lambda i, be, br, bs: (bs[i], 0)),
        scratch_shapes=[pltpu.VMEM((d, de), BF16), pltpu.VMEM((d, de), BF16), pltpu.VMEM((de, d), BF16)],
    )
    return pl.pallas_call(
        _expert_kernel,
        grid_spec=grid_spec,
        out_shape=jax.ShapeDtypeStruct((rows, half), jnp.uint32),
        compiler_params=_cparams("arbitrary"),
        name="moe_experts",
    )(block_e, block_rows, block_src, xs, w_gate, w_up, w_down)


def _combine_kernel(x_ref, m_ref, y_ref, w_ref, fg_ref, o_ref):
    x = x_ref[...] + _moe_residual(y_ref, w_ref, m_ref[5])
    o_ref[...] = _rms(x) * fg_ref[...]


def _combine(x, mods, layer, segs, seg, yg, wcol, final_g):
    n, d = x.shape
    tm = WIDE_TILE
    mod_row = _wide_mod_row(segs, tm)
    t0 = seg.row0 // tm
    steps = seg.rows // tm
    in_specs = [pl.BlockSpec((tm, d), lambda i: (t0 + i, 0)),
                pl.BlockSpec((None, None, 6, 1, d), lambda i: (layer, mod_row(t0 + i), 0, 0, 0)),
                pl.BlockSpec((2, tm, d // 2), lambda i: (0, i, 0)),
                pl.BlockSpec((tm, 8), lambda i: (t0 + i, 0)),
                pl.BlockSpec((1, d), lambda i: (0, 0))]
    return pl.pallas_call(
        _combine_kernel,
        grid=(steps,),
        in_specs=in_specs,
        out_specs=pl.BlockSpec((tm, d), lambda i: (i, 0)),
        out_shape=jax.ShapeDtypeStruct((seg.rows, d), F32),
        compiler_params=_cparams("parallel"),
        name="moe_combine_final",
    )(x, mods, yg, wcol, final_g)


def _sc_mesh():
    return plsc.VectorSubcoreMesh(core_axis_name="c", subcore_axis_name="s")


def _sc_worker_split(n):
    workers = SC_CORES * SC_SUBCORES
    per = n // workers
    assert per * workers == n and per % SC_CHUNK == 0
    return workers, per, per // SC_CHUNK


def _sc_dispatch(h, pos, rows):
    n, w = h.shape
    workers, per, chunks = _sc_worker_split(n)

    @functools.partial(
        pl.kernel, out_type=jax.ShapeDtypeStruct((rows, w), h.dtype), mesh=_sc_mesh(),
        scratch_types=[pltpu.VMEM((2, chunks, SC_CHUNK), jnp.int32), pltpu.VMEM((SC_CHUNK, w), h.dtype)],
        name="moe_dispatch_scatter")
    def scatter_rows(h_hbm, pos_hbm, xs_hbm, idx_v, rows_v):
        wid = lax.axis_index("s") * SC_CORES + lax.axis_index("c")
        pltpu.sync_copy(pos_hbm.at[0, wid], idx_v.at[0])
        pltpu.sync_copy(pos_hbm.at[1, wid], idx_v.at[1])

        @pl.loop(0, chunks)
        def _(c):
            pltpu.sync_copy(h_hbm.at[pl.ds(wid * per + c * SC_CHUNK, SC_CHUNK)], rows_v)
            pltpu.sync_copy(rows_v, xs_hbm.at[idx_v.at[0, c]])
            pltpu.sync_copy(rows_v, xs_hbm.at[idx_v.at[1, c]])

    return scatter_rows(h, pos.reshape(2, workers, chunks, SC_CHUNK))


def _sc_gather2(ys, pos):
    _, w = ys.shape
    n = pos.shape[1]
    workers, per, chunks = _sc_worker_split(n)

    @functools.partial(
        pl.kernel, out_type=jax.ShapeDtypeStruct((2, n, w), ys.dtype), mesh=_sc_mesh(),
        scratch_types=[pltpu.VMEM((2, chunks, SC_CHUNK), jnp.int32), pltpu.VMEM((SC_CHUNK, w), ys.dtype),
                       pltpu.SemaphoreType.DMA],
        name="moe_combine_gather")
    def gather_rows(ys_hbm, pos_hbm, out_hbm, idx_v, rows_v, sem):
        wid = lax.axis_index("s") * SC_CORES + lax.axis_index("c")
        pltpu.sync_copy(pos_hbm.at[0, wid], idx_v.at[0])
        pltpu.sync_copy(pos_hbm.at[1, wid], idx_v.at[1])

        @pl.loop(0, chunks)
        def _(c):
            for k in range(2):
                pltpu.async_copy(ys_hbm.at[idx_v.at[k, c]], rows_v, sem).wait()
                pltpu.sync_copy(rows_v, out_hbm.at[k, pl.ds(wid * per + c * SC_CHUNK, SC_CHUNK)])

    return gather_rows(ys, pos.reshape(2, workers, chunks, SC_CHUNK))


def _dispatch_plan(idx, rank, counts):
    n = idx.shape[1]
    padded = (counts + MOE_ROWS - 1) // MOE_ROWS * MOE_ROWS
    pad_end = jnp.cumsum(padded)
    pad_start = pad_end - padded
    experts = jnp.arange(N_EXPERTS, dtype=jnp.int32)
    start_of = jnp.sum(jnp.where(idx[..., None] == experts, pad_start, 0), axis=-1)
    pos = start_of + rank
    n_blocks = 2 * n // MOE_ROWS + N_EXPERTS
    starts = jnp.arange(n_blocks, dtype=jnp.int32) * MOE_ROWS
    block_e = jnp.minimum(jnp.sum(starts[:, None] >= pad_end[None, :], axis=1), N_EXPERTS - 1).astype(jnp.int32)
    seg_end = jnp.sum(jnp.where(block_e[:, None] == experts, pad_start + counts, 0), axis=-1)
    block_rows = jnp.clip(seg_end - starts, 0, MOE_ROWS).astype(jnp.int32)
    last_used = pad_end[-1] // MOE_ROWS - 1
    block_src = jnp.minimum(jnp.arange(n_blocks, dtype=jnp.int32), last_used).astype(jnp.int32)
    block_e = jnp.sum(jnp.where(block_src[:, None] == jnp.arange(n_blocks)[None, :], block_e[None, :], 0), axis=1)
    return pos, block_e.astype(jnp.int32), block_rows, block_src, n_blocks * MOE_ROWS


def _moe(x, mods, layer, segs, g2, rw_hi, rw_lo, rb, w_gate, w_up, w_down, final_g):
    h2p, idx, rank, wcol, cnt = _router(x, mods, layer, segs, g2, rw_hi, rw_lo, rb)
    pos, block_e, block_rows, block_src, rows = _dispatch_plan(idx, rank, cnt[:, 0].astype(jnp.int32))
    xs = _sc_dispatch(h2p, pos, rows)
    ys = _experts(xs, block_e, block_rows, block_src, w_gate, w_up, w_down, layer)
    yg = {seg: _sc_gather2(ys, pos[:, seg.row0:seg.row0 + seg.rows]) for seg in segs}
    if final_g is None:
        return yg, wcol, mods, layer
    return tuple(_combine(x, mods, layer, segs, seg, yg[seg], wcol, final_g) for seg in segs)


def _mla_rope_tables(t):
    axis_dim = MLA_ROPE // 2
    row = np.repeat(np.arange(t // GRID_W), GRID_W).astype(np.float64)
    col = np.tile(np.arange(GRID_W), t // GRID_W).astype(np.float64)
    inv = ROPE_BASE ** (-np.arange(0, axis_dim, 2, dtype=np.float64) / axis_dim)
    ar, ac = row[:, None] * inv[None, :], col[:, None] * inv[None, :]
    ones = np.ones((t, LANE - MLA_ROPE))
    cos = np.concatenate([np.cos(ar), np.cos(ar), np.cos(ac), np.cos(ac), ones], axis=-1)
    sin = np.concatenate([-np.sin(ar), np.sin(ar), -np.sin(ac), np.sin(ac), 0.0 * ones], axis=-1)
    return jnp.asarray(cos, F32), jnp.asarray(sin, F32)


def _ret_rot_tables(t, dk):
    inv = ROPE_BASE ** (-np.linspace(0.0, 1.0, dk // 2))
    ang = np.arange(t, dtype=np.float64)[:, None] * inv[None, :]
    return jnp.asarray(np.cos(ang), F32), jnp.asarray(np.sin(ang), F32)


def _dft_tables(n):
    k = np.arange(n, dtype=np.int64)
    ang = (np.outer(k, k) % n).astype(np.float64) * (2.0 * math.pi / n)
    return jnp.asarray(np.cos(ang), BF16), jnp.asarray(np.sin(ang), BF16)


def _mla_weights(w_in, q_g, kv_g, w_uq, w_ukv, w_o):
    d = w_in.shape[0]
    hd = MLA_NOPE + MLA_ROPE
    w_in_p = jnp.concatenate([w_in, jnp.zeros((d, LANE - MLA_ROPE), w_in.dtype)], axis=1)
    uq = w_uq.reshape(MLA_Q_LORA, MLA_HEADS, hd)
    uq = jnp.concatenate([uq, jnp.zeros((MLA_Q_LORA, MLA_HEADS, 2 * LANE - hd), uq.dtype)], axis=-1)
    ukv = w_ukv.reshape(MLA_KV_LORA, MLA_HEADS, MLA_NOPE + MLA_V)
    return {
        "w_in": w_in_p.astype(BF16),
        "q_g": q_g.reshape(1, -1) * (MLA_NOPE + MLA_ROPE) ** -0.5,
        "kv_g": kv_g.reshape(1, -1),
        "w_q": uq.reshape(MLA_Q_LORA, MLA_HEADS * 2 * LANE).astype(BF16),
        "w_kn": ukv[..., :MLA_NOPE].reshape(MLA_KV_LORA, MLA_HEADS * MLA_NOPE).astype(BF16),
        "w_v": ukv[..., MLA_NOPE:].reshape(MLA_KV_LORA, MLA_HEADS * MLA_V).astype(BF16),
        "w_o": w_o.astype(BF16),
    }


def kernel(x_prompt, x_sample, cache_mla, state_ret, c, c_ctx, norm1_g, norm2_g, ada_w, ada_b, final_norm_g,
           mla_w_in, mla_q_norm_g, mla_kv_norm_g, mla_w_uq, mla_w_ukv, mla_w_o, ret_w_in, ret_decay_f,
           ret_decay_b, ret_w_o, fnet_w, router_w, router_b, moe_w_gate, moe_w_up, moe_w_down):
    b_ctx, t_ctx, d = x_prompt.shape
    b_lat, t_lat, _ = x_sample.shape
    depth = ada_w.shape[0]
    assert b_lat + 1 <= 8
    n_ctx = b_ctx * t_ctx
    ctx = _Seg(0, b_ctx, t_ctx, 0, False)
    lat = _Seg(n_ctx, b_lat, t_lat, 1, True)
    segs = (ctx, lat)

    n_lat = b_lat * t_lat
    n_mla = mla_w_in.shape[0]
    assert n_mla >= 1
    x = None
    new_cache = jnp.zeros((b_ctx, n_mla, t_ctx, MLA_KV_LORA + MLA_ROPE), F32)
    cond8 = jnp.concatenate([c_ctx[None, :], c, jnp.zeros((8 - 1 - b_lat, d), F32)], axis=0)
    mods = _modulation_all(cond8, ada_w, ada_b).reshape(depth, 8, 6, 1, d)

    rw_t = router_w.T.astype(F32)
    rw_hi = rw_t.astype(BF16)
    rw_lo = (rw_t - rw_hi.astype(F32)).astype(BF16)
    rb = router_b.reshape(N_EXPERTS, 1).astype(F32)
    final_g = final_norm_g.reshape(1, d)
    dk = ret_w_in.shape[2] // (8 * RET_HEADS)
    dv = 2 * dk

    states = []
    pending = None
    counters = [0, 0, 0]
    for layer in range(depth):
        kind = layer % 3
        j = counters[kind]
        counters[kind] += 1
        g1 = norm1_g[layer].reshape(1, d)
        g2 = norm2_g[layer].reshape(1, d)
        if kind == 0:
            w = _mla_weights(mla_w_in[j], mla_q_norm_g[j], mla_kv_norm_g[j], mla_w_uq[j], mla_w_ukv[j],
                             mla_w_o[j])
            if x is None:
                xc, xc0, xl, xl0 = x_prompt.reshape(n_ctx, d), 0, x_sample.reshape(n_lat, d), 0
            else:
                xc, xc0, xl, xl0 = x, ctx.row0, x, lat.row0
            if pending is None:
                qc, kc, vc, new_cache = _mla_proj(xc, xc0, mods, layer, ctx, g1, w, None, (new_cache, j, n_mla))
                ql, kl, vl = _mla_proj(xl, xl0, mods, layer, lat, g1, w, _mla_rope_tables(t_lat))
            else:
                qc, kc, vc, new_cache, x = _mla_proj(x, ctx.row0, mods, layer, ctx, g1, w, None,
                                                     (new_cache, j, n_mla), pending)
                ql, kl, vl, x = _mla_proj(x, lat.row0, mods, layer, lat, g1, w, _mla_rope_tables(t_lat), None,
                                          pending)
                xc = xl = x
            past = cache_mla.shape[2]
            cpad = jnp.pad(cache_mla[:, j].reshape(b_lat * past, -1), ((0, 0), (0, LANE - MLA_ROPE)))
            kp, vp = _cache_kv(cpad, w)
            first = x is None
            x = _attention(xc, xc0, n_ctx + n_lat, jnp.zeros((n_ctx + n_lat, d), F32) if first else "inplace",
                           mods, layer, ctx, qc, [(kc, vc, t_ctx)], w["w_o"])
            x = _attention(xl if first else x, xl0, n_ctx + n_lat, x if first else "inplace", mods, layer, lat,
                           ql, [(kl, vl, t_lat), (kp, vp, past)], w["w_o"])
        elif kind == 1:
            w_in = ret_w_in[j]
            qk = RET_HEADS * dk
            k_scale = jnp.concatenate([jnp.ones((qk,), F32), jnp.full((qk,), dk ** -0.5, F32),
                                       jnp.ones((w_in.shape[1] - 2 * qk,), F32)])
            w_in_b = (w_in * k_scale[None, :]).astype(BF16)
            w_o_b = ret_w_o[j].astype(BF16)
            rot = _ret_rot_tables(t_lat, dk)
            n_qk, n_v = 2 * qk // RET_COL, RET_HEADS * dv // RET_COL
            assert (n_qk + n_v) * RET_COL * 2 == w_in.shape[1]
            parts = []
            for seg in segs:
                kinds = (("rotary" if seg is lat else "plain"),) * n_qk + ("plain",) * n_v
                qkv = _ret_proj(x, mods, layer, seg, g1, w_in_b, 0, kinds, rot, dk, pending)
                if pending is not None:
                    qkv, x = qkv
                parts.append((qkv, _ret_proj(x, mods, layer, seg, g1, w_in_b, 1, ("silu",) * (n_qk + n_v), None, dk)))
            yc, s_ctx = _ret_scan(*parts[0], ctx, ret_decay_f[j], ret_decay_b[j], None, True, dk, dv, RET_HEADS)
            yl, _ = _ret_scan(*parts[1], lat, ret_decay_f[j], ret_decay_b[j], state_ret[:, j], False, dk, dv, 1)
            x = _matmul_residual(x, mods, layer, ctx, yc, w_o_b)
            x = _matmul_residual(x, mods, layer, lat, yl, w_o_b)
            states.append(s_ctx)
        else:
            gd = d // FNET_GROUPS
            cc, sc = _dft_tables(gd)
            cs = jnp.concatenate([cc, sc], axis=1)
            w_b = fnet_w[j].astype(BF16)
            for seg in segs:
                ct, st = _dft_tables(seg.seq)
                if pending is None:
                    ac, as_ = _fnet_a(x, mods, layer, seg, g1, cs)
                else:
                    ac, as_, x = _fnet_a(x, mods, layer, seg, g1, cs, pending)
                x = _fnet_b(x, mods, layer, seg, ac, as_, ct, st, w_b, (seg.seq * gd) ** -0.5)
        if layer < depth - 1:
            pending = _moe(x, mods, layer, segs, g2, rw_hi, rw_lo, rb, moe_w_gate, moe_w_up, moe_w_down, None)
        else:
            y_prompt, y_sample = _moe(x, mods, layer, segs, g2, rw_hi, rw_lo, rb, moe_w_gate, moe_w_up,
                                      moe_w_down, final_g)

    new_state = jnp.stack(states, axis=1)
    return (y_prompt.reshape(b_ctx, t_ctx, d), y_sample.reshape(b_lat, t_lat, d), new_cache, new_state)
```

```python
import functools
import math

import jax
import jax.numpy as jnp
import numpy as np
from jax import lax
from jax.experimental import pallas as pl
from jax.experimental.pallas import tpu as pltpu
from jax.experimental.pallas import tpu_sc as plsc

F32 = jnp.float32
BF16 = jnp.bfloat16

GRID_W = 64
MLA_HEADS = 8
MLA_NOPE = 128
MLA_ROPE = 64
MLA_V = 128
MLA_Q_LORA = 384
MLA_KV_LORA = 256
ROPE_BASE = 10000.0
RET_HEADS = 4
RET_CHUNK = 256
FNET_GROUPS = 4
N_EXPERTS = 16
N_EXPERT_GROUPS = 4
EXPERTS_PER_GROUP = 4
D_EXPERT = 512
NORM_EPS = 1e-6

LANE = 128
PROJ_ROWS = 512
ATTN_ROWS = 512
FNET_ROWS = 512
ATTN_SHORT_SEQS = 4
WIDE_TILE = 1024
ROUTER_SUB_ROWS = 256
MOE_ROWS = 1024
MOE_TAIL_ROWS = 256
VMEM_LIMIT = 56 * 1024 * 1024
SC_CORES = 2
SC_SUBCORES = 16
SC_CHUNK = 128


def _cparams(*sem):
    return pltpu.CompilerParams(dimension_semantics=sem, vmem_limit_bytes=VMEM_LIMIT)


def _sigmoid(x):
    return 1.0 / (1.0 + jnp.exp(-x))


def _rms(x):
    return x * lax.rsqrt(jnp.mean(x * x, axis=-1, keepdims=True) + NORM_EPS)


def _modulate(x, g, shift, scale):
    return _rms(x) * (g * (1.0 + scale)) + shift


def _dot(a, b):
    return jnp.dot(a, b, preferred_element_type=F32)


def _dot_nt(a, b):
    return lax.dot_general(a, b, (((1,), (1,)), ((), ())), preferred_element_type=F32)


def _mod_kernel(c_ref, w_ref, b_ref, o_ref):
    c = c_ref[...]
    s = (c * _sigmoid(c)).astype(BF16)
    o_ref[...] = _dot(s, w_ref[...].astype(BF16)) + b_ref[...]


def _modulation_all(cond8, ada_w, ada_b):
    depth, d, d6 = ada_w.shape
    tn = d6 // 4
    return pl.pallas_call(
        _mod_kernel,
        grid=(depth, d6 // tn),
        in_specs=[
            pl.BlockSpec((8, d), lambda l, n: (0, 0)),
            pl.BlockSpec((None, d, tn), lambda l, n: (l, 0, n)),
            pl.BlockSpec((None, 1, tn), lambda l, n: (l, 0, n)),
        ],
        out_specs=pl.BlockSpec((None, 8, tn), lambda l, n: (l, 0, n)),
        out_shape=jax.ShapeDtypeStruct((depth, 8, d6), F32),
        compiler_params=_cparams("parallel", "parallel"),
        name="modulation",
    )(cond8, ada_w, ada_b.reshape(depth, 1, d6))


class _Seg:
    def __init__(self, row0, batch, seq, mod0, per_batch_mod):
        self.row0, self.batch, self.seq = row0, batch, seq
        self.mod0, self.per_batch_mod = mod0, per_batch_mod
        self.rows = batch * seq

    def tile(self, want):
        tm = min(want, self.seq) if self.per_batch_mod else want
        assert self.rows % tm == 0 and self.row0 % tm == 0 and (self.seq % tm == 0 or tm % self.seq == 0)
        return tm

    def seq_tile(self, want):
        tm = min(want, self.seq)
        assert self.seq % tm == 0 and self.row0 % tm == 0
        return tm

    def mod_row(self, tile, tm):
        if self.per_batch_mod:
            return self.mod0 + tile * tm // self.seq
        return self.mod0


def _mod_spec(layer, seg, d, tm, tile_of=lambda *a: a[0]):
    return pl.BlockSpec((None, None, 6, 1, d), lambda *a: (layer, seg.mod_row(tile_of(*a), tm), 0, 0, 0))


def _unpack_halves(p):
    lo = lax.bitcast_convert_type(p << 16, F32)
    hi = lax.bitcast_convert_type(p & jnp.uint32(0xFFFF0000), F32)
    return lo, hi


def _moe_residual(y_ref, w_ref, gate):
    w = w_ref[...]
    lo0, hi0 = _unpack_halves(y_ref[0])
    lo1, hi1 = _unpack_halves(y_ref[1])
    w0, w1 = w[:, 0:1], w[:, 1:2]
    return gate * jnp.concatenate([w0 * lo0 + w1 * lo1, w0 * hi0 + w1 * hi1], axis=-1)


def _take_pending(refs, pending):
    if not pending:
        return refs, lambda x: x
    y_ref, w_ref, pm_ref, *rest = refs
    xo_ref = rest.pop()

    def resolve(x):
        x = x + _moe_residual(y_ref, w_ref, pm_ref[5])
        xo_ref[...] = x
        return x

    return rest, resolve


def _pending_io(pending, seg, tm, d, n):
    yg_by_seg, wcol, mods, layer = pending
    t0 = seg.row0 // tm
    specs = [pl.BlockSpec((2, tm, d // 2), lambda i: (0, i, 0)),
             pl.BlockSpec((tm, 8), lambda i: (t0 + i, 0)),
             _mod_spec(layer, seg, d, tm)]
    return (specs, [yg_by_seg[seg], wcol, mods], pl.BlockSpec((tm, d), lambda i: (t0 + i, 0)),
            jax.ShapeDtypeStruct((n, d), F32))


def _rope_partner(x):
    return pltpu.roll(x, LANE // 2, 1)


def _rope_perm():
    return np.array([l + 16 if l % 32 < 16 else l - 16 for l in range(MLA_ROPE)])


def _store_values(v_ref, v):
    ones = jnp.ones((v.shape[0], LANE), BF16)
    for hd in range(MLA_HEADS):
        v_ref[:, hd * 2 * LANE:hd * 2 * LANE + LANE] = v[:, hd * MLA_V:(hd + 1) * MLA_V].astype(BF16)
        v_ref[:, hd * 2 * LANE + LANE:(hd + 1) * 2 * LANE] = ones


def _mla_proj_kernel(*refs, rope, pending):
    refs, resolve = _take_pending(refs, pending)
    if rope:
        (x_ref, m_ref, g_ref, win_ref, qg_ref, kvg_ref, wq_ref, wkn_ref, wv_ref, cos_ref, sin_ref,
         q_ref, k_ref, v_ref) = refs
    else:
        x_ref, m_ref, g_ref, win_ref, qg_ref, kvg_ref, wq_ref, wkn_ref, wv_ref = refs[:9]
        q_ref, k_ref, v_ref, cache_ref = refs[-4:]
    h = _modulate(resolve(x_ref[...]), g_ref[...], m_ref[0], m_ref[1]).astype(BF16)
    z = _dot(h, win_ref[...])
    cq = z[:, :MLA_Q_LORA]
    ckv = z[:, MLA_Q_LORA:MLA_Q_LORA + MLA_KV_LORA]
    kpe = z[:, MLA_Q_LORA + MLA_KV_LORA:]
    cqn = (_rms(cq) * qg_ref[...]).astype(BF16)
    ckvn = _rms(ckv) * kvg_ref[...]
    ckvb = ckvn.astype(BF16)
    q = _dot(cqn, wq_ref[...])
    kn = _dot(ckvb, wkn_ref[...])
    _store_values(v_ref, _dot(ckvb, wv_ref[...]))
    if rope:
        cos, sin = cos_ref[...], sin_ref[...]
        kpe = kpe * cos + _rope_partner(kpe) * sin
    kpe = jnp.where(lax.broadcasted_iota(jnp.int32, kpe.shape, 1) < MLA_ROPE, kpe, 0.0)
    if not rope:
        seq = cache_ref.shape[1]
        for s in range(cache_ref.shape[0]):
            cache_ref[s, :, :MLA_KV_LORA] = ckvn[s * seq:(s + 1) * seq, :]
            cache_ref[s, :, MLA_KV_LORA:] = kpe[s * seq:(s + 1) * seq, :MLA_ROPE]
    kpe_b = kpe.astype(BF16)
    for hd in range(MLA_HEADS):
        lo = hd * 2 * LANE
        q_ref[:, lo:lo + LANE] = q[:, lo:lo + LANE].astype(BF16)
        qr = q[:, lo + LANE:lo + 2 * LANE]
        if rope:
            qr = qr * cos + _rope_partner(qr) * sin
        q_ref[:, lo + LANE:lo + 2 * LANE] = qr.astype(BF16)
        k_ref[:, lo:lo + LANE] = kn[:, hd * LANE:(hd + 1) * LANE].astype(BF16)
        k_ref[:, lo + LANE:lo + 2 * LANE] = kpe_b


def _mla_proj(x, x_row0, mods, layer, seg, g1, w, rope_tabs, cache_slot=None, pending=None):
    n, d = x.shape
    rope = rope_tabs is not None
    rows = seg.rows
    tm = seg.tile(PROJ_ROWS)
    x_tile0 = x_row0 // tm
    hq = MLA_HEADS * 2 * LANE
    const = lambda i: (0, 0)
    aliases = {}
    in_specs = [
        pl.BlockSpec((tm, d), lambda i: (x_tile0 + i, 0)),
        _mod_spec(layer, seg, d, tm),
        pl.BlockSpec((1, d), const),
        pl.BlockSpec(w["w_in"].shape, const),
        pl.BlockSpec((1, MLA_Q_LORA), const),
        pl.BlockSpec((1, MLA_KV_LORA), const),
        pl.BlockSpec(w["w_q"].shape, const),
        pl.BlockSpec(w["w_kn"].shape, const),
        pl.BlockSpec(w["w_v"].shape, const),
    ]
    args = [x, mods, g1, w["w_in"], w["q_g"], w["kv_g"], w["w_q"], w["w_kn"], w["w_v"]]
    out_specs = [pl.BlockSpec((tm, hq), lambda i: (i, 0))] * 3
    out_shape = [jax.ShapeDtypeStruct((rows, hq), BF16)] * 3
    if rope:
        tab = pl.BlockSpec((tm, LANE), lambda i: (i % (seg.seq // tm), 0))
        in_specs += [tab, tab]
        args += list(rope_tabs)
    else:
        cw = MLA_KV_LORA + MLA_ROPE
        prev, slot, n_slots = cache_slot
        assert tm % seg.seq == 0
        out_specs.append(pl.BlockSpec((tm // seg.seq, None, seg.seq, cw), lambda i: (i, slot, 0, 0)))
        out_shape.append(jax.ShapeDtypeStruct((seg.batch, n_slots, seg.seq, cw), F32))
        in_specs.append(pl.BlockSpec(memory_space=pl.ANY))
        args.append(prev)
        aliases = {len(args) - 1: 3}
    if pending is not None:
        p_specs, p_args, xo_spec, xo_shape = _pending_io(pending, seg, tm, d, n)
        in_specs, args = p_specs + in_specs, p_args + args
        out_specs.append(xo_spec)
        out_shape.append(xo_shape)
        aliases = {k + len(p_args): v for k, v in aliases.items()}
        aliases[len(p_args)] = len(out_shape) - 1
    return pl.pallas_call(
        functools.partial(_mla_proj_kernel, rope=rope, pending=pending is not None),
        grid=(rows // tm,),
        in_specs=in_specs,
        out_specs=out_specs,
        out_shape=out_shape,
        input_output_aliases=aliases,
        compiler_params=_cparams("parallel"),
        name="mla_proj_lat" if rope else "mla_proj_ctx",
    )(*args)


def _cache_kv_kernel(c_ref, wkn_ref, wv_ref, k_ref, v_ref):
    c = c_ref[...]
    ckv = c[:, :MLA_KV_LORA].astype(BF16)
    kpe_b = c[:, MLA_KV_LORA:].astype(BF16)
    kn = _dot(ckv, wkn_ref[...])
    _store_values(v_ref, _dot(ckv, wv_ref[...]))
    for hd in range(MLA_HEADS):
        lo = hd * 2 * LANE
        k_ref[:, lo:lo + LANE] = kn[:, hd * LANE:(hd + 1) * LANE].astype(BF16)
        k_ref[:, lo + LANE:lo + 2 * LANE] = kpe_b


def _cache_kv(cache_pad, w):
    rows, cw = cache_pad.shape
    hq = MLA_HEADS * 2 * LANE
    const = lambda i: (0, 0)
    tm = min(PROJ_ROWS, rows)
    assert rows % tm == 0
    return pl.pallas_call(
        _cache_kv_kernel,
        grid=(rows // tm,),
        in_specs=[
            pl.BlockSpec((tm, cw), lambda i: (i, 0)),
            pl.BlockSpec(w["w_kn"].shape, const),
            pl.BlockSpec(w["w_v"].shape, const),
        ],
        out_specs=[pl.BlockSpec((tm, hq), lambda i: (i, 0))] * 2,
        out_shape=[jax.ShapeDtypeStruct((rows, hq), BF16)] * 2,
        compiler_params=_cparams("parallel"),
        name="mla_cache_kv",
    )(cache_pad, w["w_kn"], w["w_v"])


def _attn_kernel(*refs, n_parts, n_seq):
    q_ref = refs[0]
    kv_refs = refs[1:1 + 2 * n_parts]
    wo_ref, x_ref, m_ref = refs[1 + 2 * n_parts:4 + 2 * n_parts]
    o_ref, acc_ref = refs[-2:]
    tq = q_ref.shape[0] // n_seq
    for sq, hd in [(sq, hd) for sq in range(n_seq) for hd in range(MLA_HEADS)]:
        rows = slice(sq * tq, (sq + 1) * tq)
        kcol = slice(hd * 2 * LANE, (hd + 1) * 2 * LANE)
        keys = [slice(sq * (r.shape[0] // n_seq), (sq + 1) * (r.shape[0] // n_seq)) for r in kv_refs[::2]]
        scores = [_dot_nt(q_ref[rows, kcol], kv_refs[2 * p][keys[p], kcol]) for p in range(n_parts)]
        mx = scores[0].max(axis=-1, keepdims=True)
        for s in scores[1:]:
            mx = jnp.maximum(mx, s.max(axis=-1, keepdims=True))
        out = None
        for p, s in enumerate(scores):
            e = jnp.exp((s - mx).astype(BF16))
            pv = _dot(e, kv_refs[2 * p + 1][keys[p], kcol])
            out = pv if out is None else out + pv
        acc_ref[rows, hd * MLA_V:(hd + 1) * MLA_V] = (out[:, :MLA_V] / out[:, LANE:LANE + MLA_V]).astype(BF16)
    y = _dot(acc_ref[...], wo_ref[...])
    o_ref[...] = x_ref[...] + m_ref[2] * y


def _attention(x, x_row0, n, dest, mods, layer, seg, q, kv_parts, w_o):
    d = x.shape[1]
    hq = MLA_HEADS * 2 * LANE
    hv = MLA_HEADS * MLA_V
    tq = seg.seq_tile(ATTN_ROWS)
    tps = seg.seq // tq
    n_seq = ATTN_SHORT_SEQS if (tps == 1 and not seg.per_batch_mod and seg.batch % ATTN_SHORT_SEQS == 0) else 1
    tq *= n_seq
    x_tile0, out_tile0 = x_row0 // tq, seg.row0 // tq
    in_specs = [pl.BlockSpec((tq, hq), lambda b, i: (b * tps + i, 0))]
    args = [q]
    for k, v, rows in kv_parts:
        mode = dict(pipeline_mode=pl.Buffered(1)) if tps > 1 else {}
        in_specs += [pl.BlockSpec((n_seq * rows, hq), lambda b, i: (b, 0), **mode)] * 2
        args += [k, v]
    in_specs += [
        pl.BlockSpec(w_o.shape, lambda b, i: (0, 0)),
        pl.BlockSpec((tq, d), lambda b, i: (x_tile0 + b * tps + i, 0)),
        _mod_spec(layer, seg, d, tq // n_seq, tile_of=lambda b, i: (b * tps + i) * n_seq),
    ]
    args += [w_o, x, mods]
    if isinstance(dest, str):
        assert dest == "inplace"
        aliases = {len(args) - 2: 0}
    else:
        in_specs.append(pl.BlockSpec(memory_space=pl.ANY))
        args.append(dest)
        aliases = {len(args) - 1: 0}
    return pl.pallas_call(
        functools.partial(_attn_kernel, n_parts=len(kv_parts), n_seq=n_seq),
        grid=(seg.batch // n_seq, tps),
        in_specs=in_specs,
        out_specs=pl.BlockSpec((tq, d), lambda b, i: (out_tile0 + b * tps + i, 0)),
        out_shape=jax.ShapeDtypeStruct((n, d), F32),
        scratch_shapes=[pltpu.VMEM((tq, hv), BF16)],
        input_output_aliases=aliases,
        compiler_params=_cparams("parallel", "arbitrary"),
        name="mla_attention",
    )(*args)


RET_COL = 1024


RET_ROWS = 1024


def _ret_proj_kernel(*refs, kinds, dk, pending):
    refs, resolve = _take_pending(refs, pending)
    rotary = "rotary" in kinds
    if rotary:
        x_ref, m_ref, g_ref, w_ref, cos_ref, sin_ref, z_ref = refs
        cos, sin = cos_ref[...], sin_ref[...]
    else:
        x_ref, m_ref, g_ref, w_ref, z_ref = refs
    h = _modulate(resolve(x_ref[...]), g_ref[...], m_ref[0], m_ref[1]).astype(BF16)
    half = dk // 2
    for j, kind in enumerate(kinds):
        c0 = j * RET_COL
        acc = _dot(h, w_ref[:, c0:c0 + RET_COL])
        if kind == "rotary":
            for hd in range(RET_COL // dk):
                lo = hd * dk
                x1, x2 = acc[:, lo:lo + half], acc[:, lo + half:lo + dk]
                z_ref[:, c0 + lo:c0 + lo + half] = (x1 * cos - x2 * sin).astype(BF16)
                z_ref[:, c0 + lo + half:c0 + lo + dk] = (x1 * sin + x2 * cos).astype(BF16)
        elif kind == "silu":
            z_ref[:, c0:c0 + RET_COL] = (acc * _sigmoid(acc)).astype(BF16)
        else:
            z_ref[:, c0:c0 + RET_COL] = acc.astype(BF16)


def _ret_proj(x, mods, layer, seg, g1, w_in, group, kinds, rot_tabs, dk, pending=None):
    n, d = x.shape
    rows = seg.rows
    tm = seg.tile(RET_ROWS)
    ncol = len(kinds) * RET_COL
    tps = max(seg.seq // tm, 1)
    in_specs = [
        pl.BlockSpec((tm, d), lambda i: (seg.row0 // tm + i, 0)),
        _mod_spec(layer, seg, d, tm),
        pl.BlockSpec((1, d), lambda i: (0, 0)),
        pl.BlockSpec((d, ncol), lambda i: (0, group), pipeline_mode=pl.Buffered(1)),
    ]
    args = [x, mods, g1, w_in]
    if "rotary" in kinds:
        tab = pl.BlockSpec((tm, dk // 2), lambda i: (i % tps, 0))
        in_specs += [tab, tab]
        args += list(rot_tabs)
    out_specs = [pl.BlockSpec((tm, ncol), lambda i: (i, 0))]
    out_shape = [jax.ShapeDtypeStruct((rows, ncol), BF16)]
    aliases = {}
    if pending is not None:
        p_specs, p_args, xo_spec, xo_shape = _pending_io(pending, seg, tm, d, n)
        in_specs, args = p_specs + in_specs, p_args + args
        out_specs.append(xo_spec)
        out_shape.append(xo_shape)
        aliases = {len(p_args): 1}
    res = pl.pallas_call(
        functools.partial(_ret_proj_kernel, kinds=kinds, dk=dk, pending=pending is not None),
        grid=(rows // tm,),
        in_specs=in_specs,
        out_specs=out_specs,
        out_shape=out_shape,
        input_output_aliases=aliases,
        compiler_params=_cparams("parallel"),
        name="ret_proj_" + kinds[0],
    )(*args)
    return res if pending is not None else res[0]


def _log_sigmoid(x):
    return jnp.minimum(x, 0.0) - jnp.log(1.0 + jnp.exp(-jnp.abs(x)))


def _ret_scan_kernel(*refs, has_s0, emit_state, n_chunks, heads):
    refs = list(refs)
    lf_ref, lb_ref, q_ref, k_ref, v_ref, gf_ref, gb_ref = refs[:7]
    pos = 7
    s0_ref = None
    if has_s0:
        s0_ref = refs[pos]
        pos += 1
    y_ref = refs[pos]
    pos += 1
    sout_ref = None
    if emit_state:
        sout_ref = refs[pos]
        pos += 1
    s_ref, yf_ref = refs[pos:]
    c = RET_CHUNK
    dk, dv = s_ref.shape
    ii = lax.broadcasted_iota(jnp.int32, (c, c), 0).astype(F32)
    jj = lax.broadcasted_iota(jnp.int32, (c, c), 1).astype(F32)
    idx = lax.broadcasted_iota(jnp.int32, (c, 1), 0).astype(F32)

    for hd, direction in [(hd, direction) for hd in range(heads) for direction in range(2)]:
        fwd = direction == 0
        kcol, vcol = slice(hd * dk, (hd + 1) * dk), slice(hd * dv, (hd + 1) * dv)
        lg = _log_sigmoid((lf_ref if fwd else lb_ref)[hd])
        rel = (ii - jj) if fwd else (jj - ii)
        keep = rel >= 0
        decay_in = jnp.where(keep, jnp.exp(jnp.where(keep, rel, 0.0) * lg), 0.0)
        decay_q = jnp.exp(((idx + 1.0) if fwd else (c - idx)) * lg)
        decay_k = jnp.exp(((c - 1.0 - idx) if fwd else idx) * lg)
        decay_c = jnp.exp(c * lg)
        g_ref = gf_ref if fwd else gb_ref

        def chunk(cc, state, fwd=fwd, decay_in=decay_in, decay_q=decay_q, decay_k=decay_k, decay_c=decay_c,
                  g_ref=g_ref, kcol=kcol, vcol=vcol):
            r0 = cc * c if isinstance(cc, int) else pl.multiple_of(cc * c, c)
            qc = q_ref[pl.ds(r0, c), kcol]
            kc = k_ref[pl.ds(r0, c), kcol]
            vc = v_ref[pl.ds(r0, c), vcol]
            sc = _dot_nt(qc, kc) * decay_in
            out = _dot(sc.astype(BF16), vc)
            kd_t = (kc.astype(F32) * decay_k).T.astype(BF16)
            new_s = _dot(kd_t, vc)
            if state is not None:
                out = out + decay_q * _dot(qc, state.astype(BF16))
                new_s = decay_c * state + new_s
            s_ref[...] = new_s
            o = _rms(out) * g_ref[pl.ds(r0, c), vcol].astype(F32)
            if fwd:
                yf_ref[pl.ds(r0, c), :] = o
            else:
                y_ref[pl.ds(r0, c), vcol] = (yf_ref[pl.ds(r0, c), :] + o).astype(BF16)

        chunk(0 if fwd else n_chunks - 1, s0_ref[direction, hd] if has_s0 else None)

        def step(ci, carry, fwd=fwd, chunk=chunk):
            chunk(ci if fwd else n_chunks - 1 - ci, s_ref[...])
            return carry

        lax.fori_loop(1, n_chunks, step, 0, unroll=True)
        if emit_state:
            sout_ref[direction, hd] = s_ref[...]


def _ret_scan(qkv, g, seg, logit_f, logit_b, s0, emit_state, dk, dv, heads):
    rows = seg.batch * seg.seq
    t = seg.seq
    hh = RET_HEADS
    groups = hh // heads
    v0 = 2 * hh * dk // (heads * dv)
    assert groups * heads == hh and v0 * heads * dv == 2 * hh * dk
    in_specs = [
        pl.BlockSpec((heads, 1, 1), lambda b, h: (h, 0, 0)),
        pl.BlockSpec((heads, 1, 1), lambda b, h: (h, 0, 0)),
        pl.BlockSpec((t, heads * dk), lambda b, h: (b, h)),
        pl.BlockSpec((t, heads * dk), lambda b, h: (b, groups + h)),
        pl.BlockSpec((t, heads * dv), lambda b, h: (b, v0 + h)),
        pl.BlockSpec((t, heads * dv), lambda b, h: (b, h)),
        pl.BlockSpec((t, heads * dv), lambda b, h: (b, groups + h)),
    ]
    args = [logit_f.reshape(hh, 1, 1), logit_b.reshape(hh, 1, 1), qkv, qkv, qkv, g, g]
    state_spec = pl.BlockSpec((None, 2, heads, dk, dv), lambda b, h: (b, 0, h, 0, 0))
    if s0 is not None:
        in_specs.append(state_spec)
        args.append(s0)
    out_specs = [pl.BlockSpec((t, heads * dv), lambda b, h: (b, h))]
    out_shape = [jax.ShapeDtypeStruct((rows, hh * dv), BF16)]
    if emit_state:
        out_specs.append(state_spec)
        out_shape.append(jax.ShapeDtypeStruct((seg.batch, 2, hh, dk, dv), F32))
    res = pl.pallas_call(
        functools.partial(_ret_scan_kernel, has_s0=s0 is not None, emit_state=emit_state,
                          n_chunks=t // RET_CHUNK, heads=heads),
        grid=(seg.batch, groups),
        in_specs=in_specs,
        out_specs=out_specs,
        out_shape=out_shape,
        scratch_shapes=[pltpu.VMEM((dk, dv), F32), pltpu.VMEM((t, dv), F32)],
        compiler_params=_cparams("parallel", "parallel"),
        name="ret_scan",
    )(*args)
    return res if emit_state else (res[0], None)


def _mm_res_kernel(a_ref, w_ref, x_ref, m_ref, o_ref):
    o_ref[...] = x_ref[...] + m_ref[2] * _dot(a_ref[...], w_ref[...])


def _matmul_residual(x, mods, layer, seg, a, w):
    n, d = x.shape
    tm = seg.tile(RET_ROWS)
    x_spec = pl.BlockSpec((tm, d), lambda i: (seg.row0 // tm + i, 0))
    return pl.pallas_call(
        _mm_res_kernel,
        grid=(seg.rows // tm,),
        in_specs=[
            pl.BlockSpec((tm, a.shape[1]), lambda i: (i, 0)),
            pl.BlockSpec(w.shape, lambda i: (0, 0)),
            x_spec,
            _mod_spec(layer, seg, d, tm),
        ],
        out_specs=x_spec,
        out_shape=jax.ShapeDtypeStruct((n, d), F32),
        input_output_aliases={2: 0},
        compiler_params=_cparams("parallel"),
        name="matmul_residual",
    )(a, w, x, mods)


def _fnet_a_kernel(*refs, gd, pending):
    refs, resolve = _take_pending(refs, pending)
    x_ref, m_ref, g_ref, cs_ref, ac_ref, as_ref = refs
    h = _modulate(resolve(x_ref[...]), g_ref[...], m_ref[0], m_ref[1]).astype(BF16)
    cs = cs_ref[...]
    for g in range(FNET_GROUPS):
        a = _dot(h[:, g * gd:(g + 1) * gd], cs)
        ac_ref[:, g * gd:(g + 1) * gd] = a[:, :gd].astype(BF16)
        as_ref[:, g * gd:(g + 1) * gd] = a[:, gd:].astype(BF16)


def _fnet_a(x, mods, layer, seg, g1, cs, pending=None):
    n, d = x.shape
    rows = seg.rows
    tm = seg.tile(PROJ_ROWS)
    out = pl.BlockSpec((tm, d), lambda i: (i, 0))
    in_specs = [
        pl.BlockSpec((tm, d), lambda i: (seg.row0 // tm + i, 0)),
        _mod_spec(layer, seg, d, tm),
        pl.BlockSpec((1, d), lambda i: (0, 0)),
        pl.BlockSpec(cs.shape, lambda i: (0, 0)),
    ]
    args = [x, mods, g1, cs]
    out_specs, out_shape, aliases = [out, out], [jax.ShapeDtypeStruct((rows, d), BF16)] * 2, {}
    if pending is not None:
        p_specs, p_args, xo_spec, xo_shape = _pending_io(pending, seg, tm, d, n)
        in_specs, args = p_specs + in_specs, p_args + args
        out_specs.append(xo_spec)
        out_shape.append(xo_shape)
        aliases = {len(p_args): 2}
    return pl.pallas_call(
        functools.partial(_fnet_a_kernel, gd=d // FNET_GROUPS, pending=pending is not None),
        grid=(rows // tm,),
        in_specs=in_specs,
        out_specs=out_specs,
        out_shape=out_shape,
        input_output_aliases=aliases,
        compiler_params=_cparams("parallel"),
        name="fnet_channel_dft",
    )(*args)


def _fnet_b_kernel(ct_ref, st_ref, ac_ref, as_ref, w_ref, x_ref, m_ref, o_ref, *, norm, n_seq):
    t = ac_ref.shape[0] // n_seq
    ct, st = ct_ref[...], st_ref[...]
    f = [_dot(ct, ac_ref[s * t:(s + 1) * t, :]) - _dot(st, as_ref[s * t:(s + 1) * t, :]) for s in range(n_seq)]
    f = (f[0] if n_seq == 1 else jnp.concatenate(f, axis=0)) * norm
    o_ref[...] = x_ref[...] + m_ref[2] * _dot(f.astype(BF16), w_ref[...])


def _fnet_b(x, mods, layer, seg, ac, as_, ct, st, w, norm):
    n, d = x.shape
    t = seg.seq
    tq = seg.seq_tile(FNET_ROWS)
    tps = t // tq
    n_seq = ATTN_SHORT_SEQS if (tps == 1 and not seg.per_batch_mod and seg.batch % ATTN_SHORT_SEQS == 0) else 1
    rows = n_seq * tq
    x_spec = pl.BlockSpec((rows, d), lambda b, i: (seg.row0 // rows + b * tps + i, 0))
    tab = pl.BlockSpec((tq, t), lambda b, i: (i, 0))
    seq = pl.BlockSpec((n_seq * t, d), lambda b, i: (b, 0))
    return pl.pallas_call(
        functools.partial(_fnet_b_kernel, norm=norm, n_seq=n_seq),
        grid=(seg.batch // n_seq, tps),
        in_specs=[tab, tab, seq, seq, pl.BlockSpec(w.shape, lambda b, i: (0, 0)), x_spec,
                  _mod_spec(layer, seg, d, tq, tile_of=lambda b, i: (b * tps + i) * n_seq)],
        out_specs=x_spec,
        out_shape=jax.ShapeDtypeStruct((n, d), F32),
        input_output_aliases={5: 0},
        compiler_params=_cparams("parallel", "arbitrary"),
        name="fnet_position_dft",
    )(ct, st, ac, as_, w, x, mods)


def _pack_halves(a):
    w = a.shape[1] // 2
    bits = lambda v: lax.bitcast_convert_type(v.astype(BF16).astype(F32), jnp.uint32)
    return (bits(a[:, :w]) >> 16) | (bits(a[:, w:]) & jnp.uint32(0xFFFF0000))


def _router_kernel(x_ref, m_ref, g_ref, rwhi_ref, rwlo_ref, rb_ref, h_ref, idx_ref, rank_ref, wcol_ref, cnt_ref,
                   run_ref, tri_ref):
    step = pl.program_id(0)

    @pl.when(step == 0)
    def _():
        run_ref[...] = jnp.zeros_like(run_ref)
        tt = tri_ref.shape[0]
        earlier = lax.broadcasted_iota(jnp.int32, (tt, tt), 0) < lax.broadcasted_iota(jnp.int32, (tt, tt), 1)
        tri_ref[...] = jnp.where(earlier, 1.0, 0.0).astype(BF16)

    parts = []
    for r0 in range(0, x_ref.shape[0], ROUTER_SUB_ROWS):
        rows = slice(r0, r0 + ROUTER_SUB_ROWS)
        h = _modulate(x_ref[rows, :], g_ref[...], m_ref[3], m_ref[4])
        h_ref[rows, :] = _pack_halves(h)
        h_hi = h.astype(BF16)
        h_lo = (h - h_hi.astype(F32)).astype(BF16)
        parts.append(_dot_nt(rwhi_ref[...], h_hi) + (_dot_nt(rwhi_ref[...], h_lo) + _dot_nt(rwlo_ref[...], h_hi)))
    logits = jnp.concatenate(parts, axis=1)
    sc = _sigmoid(logits)
    gr = sc + rb_ref[...]
    gp = EXPERTS_PER_GROUP
    row = lambda a, e: a[e:e + 1, :]
    best_g = None
    for g in range(N_EXPERT_GROUPS):
        vals = [row(gr, g * gp + i) for i in range(gp)]
        gs = None
        for i in range(gp):
            for j in range(i + 1, gp):
                pair = vals[i] + vals[j]
                gs = pair if gs is None else jnp.maximum(gs, pair)
        if best_g is None:
            best_g, best_v = jnp.zeros(gs.shape, jnp.int32), gs
        else:
            better = gs > best_v
            best_g = jnp.where(better, g, best_g)
            best_v = jnp.where(better, gs, best_v)
    sel, raw = [], []
    for i in range(gp):
        s_i, r_i = row(gr, i), row(sc, i)
        for g in range(1, N_EXPERT_GROUPS):
            s_i = jnp.where(best_g == g, row(gr, g * gp + i), s_i)
            r_i = jnp.where(best_g == g, row(sc, g * gp + i), r_i)
        sel.append(s_i)
        raw.append(r_i)

    def argmax_first(vals, raws):
        bi, bv, br = jnp.zeros(vals[0].shape, jnp.int32), vals[0], raws[0]
        for i in range(1, len(vals)):
            better = vals[i] > bv
            bi = jnp.where(better, i, bi)
            bv = jnp.where(better, vals[i], bv)
            br = jnp.where(better, raws[i], br)
        return bi, br

    i1, w1 = argmax_first(sel, raw)
    masked = [jnp.where(i1 == i, -jnp.inf, sel[i]) for i in range(gp)]
    i2, w2 = argmax_first(masked, raw)
    tot = w1 + w2
    e1 = best_g * gp + i1
    e2 = best_g * gp + i2
    idx_ref[0:1, :] = e1
    idx_ref[1:2, :] = e2
    t = e1.shape[1]
    sub = lax.broadcasted_iota(jnp.int32, (8, t), 0)
    w8 = jnp.where(sub == 0, w1 / tot, jnp.where(sub == 1, w2 / tot, 0.0))
    wcol_ref[...] = w8.T
    eio = lax.broadcasted_iota(jnp.int32, (N_EXPERTS, t), 0)
    oh1, oh2 = eio == e1, eio == e2
    oh = jnp.where(oh1, 1.0, jnp.where(oh2, 1.0, 0.0))
    local = _dot(oh.astype(BF16), tri_ref[...])
    rank = local + run_ref[:, 0:1]
    rank_ref[0:1, :] = jnp.sum(jnp.where(oh1, rank, 0.0), axis=0, keepdims=True).astype(jnp.int32)
    rank_ref[1:2, :] = jnp.sum(jnp.where(oh2, rank, 0.0), axis=0, keepdims=True).astype(jnp.int32)
    run_ref[...] = run_ref[...] + jnp.sum(oh, axis=1, keepdims=True)
    cnt_ref[...] = run_ref[...]


def _wide_mod_row(segs, tm):
    ctx, lat = segs
    ctx_tiles = ctx.batch * ctx.seq // tm
    assert ctx_tiles * tm == ctx.batch * ctx.seq and lat.seq % tm == 0
    return lambda i: jnp.where(i < ctx_tiles, ctx.mod0, lat.mod0 + (i - ctx_tiles) // (lat.seq // tm))


def _router(x, mods, layer, segs, g2, rw_hi, rw_lo, rb):
    n, d = x.shape
    tm = WIDE_TILE
    mod_row = _wide_mod_row(segs, tm)
    return pl.pallas_call(
        _router_kernel,
        grid=(n // tm,),
        in_specs=[
            pl.BlockSpec((tm, d), lambda i: (i, 0)),
            pl.BlockSpec((None, None, 6, 1, d), lambda i: (layer, mod_row(i), 0, 0, 0)),
            pl.BlockSpec((1, d), lambda i: (0, 0)),
            pl.BlockSpec(rw_hi.shape, lambda i: (0, 0)),
            pl.BlockSpec(rw_lo.shape, lambda i: (0, 0)),
            pl.BlockSpec(rb.shape, lambda i: (0, 0)),
        ],
        out_specs=[
            pl.BlockSpec((tm, d // 2), lambda i: (i, 0)),
            pl.BlockSpec((2, tm), lambda i: (0, i)),
            pl.BlockSpec((2, tm), lambda i: (0, i)),
            pl.BlockSpec((tm, 8), lambda i: (i, 0)),
            pl.BlockSpec((N_EXPERTS, LANE), lambda i: (0, 0)),
        ],
        out_shape=[
            jax.ShapeDtypeStruct((n, d // 2), jnp.uint32),
            jax.ShapeDtypeStruct((2, n), jnp.int32),
            jax.ShapeDtypeStruct((2, n), jnp.int32),
            jax.ShapeDtypeStruct((n, 8), F32),
            jax.ShapeDtypeStruct((N_EXPERTS, LANE), F32),
        ],
        scratch_shapes=[pltpu.VMEM((N_EXPERTS, LANE), F32), pltpu.VMEM((tm, tm), BF16)],
        compiler_params=_cparams("arbitrary"),
        name="moe_router",
    )(x, mods, g2, rw_hi, rw_lo, rb)


def _expert_kernel(be_ref, br_ref, bs_ref, xs_ref, wg_ref, wu_ref, wd_ref, y_ref, wg_b, wu_b, wd_b):
    i = pl.program_id(0)
    prev = be_ref[jnp.maximum(i - 1, 0)]
    valid = br_ref[i]
    tail_rows = MOE_TAIL_ROWS

    @pl.when(jnp.logical_or(i == 0, be_ref[i] != prev))
    def _():
        wg_b[...] = wg_ref[...].astype(BF16)
        wu_b[...] = wu_ref[...].astype(BF16)
        wd_b[...] = wd_ref[...].astype(BF16)

    def ffn(r0, nrows):
        lo, hi = _unpack_halves(xs_ref[r0:r0 + nrows, :])
        xb = jnp.concatenate([lo.astype(BF16), hi.astype(BF16)], axis=1)
        gate = _dot(xb, wg_b[...])
        hid = (gate * _sigmoid(gate)) * _dot(xb, wu_b[...])
        y_ref[r0:r0 + nrows, :] = _pack_halves(_dot(hid.astype(BF16), wd_b[...]))

    @pl.when(valid > tail_rows)
    def _():
        ffn(0, y_ref.shape[0])

    @pl.when(jnp.logical_and(valid > 0, valid <= tail_rows))
    def _():
        ffn(0, tail_rows)
        y_ref[tail_rows:, :] = jnp.zeros((y_ref.shape[0] - tail_rows, y_ref.shape[1]), y_ref.dtype)


def _experts(xs, block_e, block_rows, block_src, w_gate, w_up, w_down, layer):
    rows, half = xs.shape
    d = 2 * half
    de = w_gate.shape[-1]
    n_blocks = rows // MOE_ROWS
    grid_spec = pltpu.PrefetchScalarGridSpec(
        num_scalar_prefetch=3,
        grid=(n_blocks,),
        in_specs=[
            pl.BlockSpec((MOE_ROWS, half), lambda i, be, br, bs: (bs[i], 0)),
            pl.BlockSpec((None, None, d, de), lambda i, be, br, bs: (layer, be[i], 0, 0)),
            pl.BlockSpec((None, None, d, de), lambda i, be, br, bs: (layer, be[i], 0, 0)),
            pl.BlockSpec((None, None, de, d), lambda i, be, br, bs: (layer, be[i], 0, 0)),
        ],
        out_specs=pl.BlockSpec((MOE_ROWS, half), lambda i, be, br, bs: (bs[i], 0)),
        scratch_shapes=[pltpu.VMEM((d, de), BF16), pltpu.VMEM((d, de), BF16), pltpu.VMEM((de, d), BF16)],
    )
    return pl.pallas_call(
        _expert_kernel,
        grid_spec=grid_spec,
        out_shape=jax.ShapeDtypeStruct((rows, half), jnp.uint32),
        compiler_params=_cparams("arbitrary"),
        name="moe_experts",
    )(block_e, block_rows, block_src, xs, w_gate, w_up, w_down)


def _combine_kernel(x_ref, m_ref, y_ref, w_ref, fg_ref, o_ref):
    x = x_ref[...] + _moe_residual(y_ref, w_ref, m_ref[5])
    o_ref[...] = _rms(x) * fg_ref[...]


def _combine(x, mods, layer, segs, seg, yg, wcol, final_g):
    n, d = x.shape
    tm = WIDE_TILE
    mod_row = _wide_mod_row(segs, tm)
    t0 = seg.row0 // tm
    steps = seg.rows // tm
    in_specs = [pl.BlockSpec((tm, d), lambda i: (t0 + i, 0)),
                pl.BlockSpec((None, None, 6, 1, d), lambda i: (layer, mod_row(t0 + i), 0, 0, 0)),
                pl.BlockSpec((2, tm, d // 2), lambda i: (0, i, 0)),
                pl.BlockSpec((tm, 8), lambda i: (t0 + i, 0)),
                pl.BlockSpec((1, d), lambda i: (0, 0))]
    return pl.pallas_call(
        _combine_kernel,
        grid=(steps,),
        in_specs=in_specs,
        out_specs=pl.BlockSpec((tm, d), lambda i: (i, 0)),
        out_shape=jax.ShapeDtypeStruct((seg.rows, d), F32),
        compiler_params=_cparams("parallel"),
        name="moe_combine_final",
    )(x, mods, yg, wcol, final_g)


def _sc_mesh():
    return plsc.VectorSubcoreMesh(core_axis_name="c", subcore_axis_name="s")


def _sc_worker_split(n):
    workers = SC_CORES * SC_SUBCORES
    per = n // workers
    assert per * workers == n and per % SC_CHUNK == 0
    return workers, per, per // SC_CHUNK


def _sc_dispatch(h, pos, rows):
    n, w = h.shape
    workers, per, chunks = _sc_worker_split(n)

    @functools.partial(
        pl.kernel, out_type=jax.ShapeDtypeStruct((rows, w), h.dtype), mesh=_sc_mesh(),
        scratch_types=[pltpu.VMEM((2, chunks, SC_CHUNK), jnp.int32), pltpu.VMEM((SC_CHUNK, w), h.dtype)],
        name="moe_dispatch_scatter")
    def scatter_rows(h_hbm, pos_hbm, xs_hbm, idx_v, rows_v):
        wid = lax.axis_index("s") * SC_CORES + lax.axis_index("c")
        pltpu.sync_copy(pos_hbm.at[0, wid], idx_v.at[0])
        pltpu.sync_copy(pos_hbm.at[1, wid], idx_v.at[1])

        @pl.loop(0, chunks)
        def _(c):
            pltpu.sync_copy(h_hbm.at[pl.ds(wid * per + c * SC_CHUNK, SC_CHUNK)], rows_v)
            pltpu.sync_copy(rows_v, xs_hbm.at[idx_v.at[0, c]])
            pltpu.sync_copy(rows_v, xs_hbm.at[idx_v.at[1, c]])

    return scatter_rows(h, pos.reshape(2, workers, chunks, SC_CHUNK))


def _sc_gather2(ys, pos):
    _, w = ys.shape
    n = pos.shape[1]
    workers, per, chunks = _sc_worker_split(n)

    @functools.partial(
        pl.kernel, out_type=jax.ShapeDtypeStruct((2, n, w), ys.dtype), mesh=_sc_mesh(),
        scratch_types=[pltpu.VMEM((2, chunks, SC_CHUNK), jnp.int32), pltpu.VMEM((SC_CHUNK, w), ys.dtype),
                       pltpu.SemaphoreType.DMA],
        name="moe_combine_gather")
    def gather_rows(ys_hbm, pos_hbm, out_hbm, idx_v, rows_v, sem):
        wid = lax.axis_index("s") * SC_CORES + lax.axis_index("c")
        pltpu.sync_copy(pos_hbm.at[0, wid], idx_v.at[0])
        pltpu.sync_copy(pos_hbm.at[1, wid], idx_v.at[1])

        @pl.loop(0, chunks)
        def _(c):
            for k in range(2):
                pltpu.async_copy(ys_hbm.at[idx_v.at[k, c]], rows_v, sem).wait()
                pltpu.sync_copy(rows_v, out_hbm.at[k, pl.ds(wid * per + c * SC_CHUNK, SC_CHUNK)])

    return gather_rows(ys, pos.reshape(2, workers, chunks, SC_CHUNK))


def _dispatch_plan(idx, rank, counts):
    n = idx.shape[1]
    padded = (counts + MOE_ROWS - 1) // MOE_ROWS * MOE_ROWS
    pad_end = jnp.cumsum(padded)
    pad_start = pad_end - padded
    experts = jnp.arange(N_EXPERTS, dtype=jnp.int32)
    start_of = jnp.sum(jnp.where(idx[..., None] == experts, pad_start, 0), axis=-1)
    pos = start_of + rank
    n_blocks = 2 * n // MOE_ROWS + N_EXPERTS
    starts = jnp.arange(n_blocks, dtype=jnp.int32) * MOE_ROWS
    block_e = jnp.minimum(jnp.sum(starts[:, None] >= pad_end[None, :], axis=1), N_EXPERTS - 1).astype(jnp.int32)
    seg_end = jnp.sum(jnp.where(block_e[:, None] == experts, pad_start + counts, 0), axis=-1)
    block_rows = jnp.clip(seg_end - starts, 0, MOE_ROWS).astype(jnp.int32)
    last_used = pad_end[-1] // MOE_ROWS - 1
    block_src = jnp.minimum(jnp.arange(n_blocks, dtype=jnp.int32), last_used).astype(jnp.int32)
    block_e = jnp.sum(jnp.where(block_src[:, None] == jnp.arange(n_blocks)[None, :], block_e[None, :], 0), axis=1)
    return pos, block_e.astype(jnp.int32), block_rows, block_src, n_blocks * MOE_ROWS


def _moe(x, mods, layer, segs, g2, rw_hi, rw_lo, rb, w_gate, w_up, w_down, final_g):
    h2p, idx, rank, wcol, cnt = _router(x, mods, layer, segs, g2, rw_hi, rw_lo, rb)
    pos, block_e, block_rows, block_src, rows = _dispatch_plan(idx, rank, cnt[:, 0].astype(jnp.int32))
    xs = _sc_dispatch(h2p, pos, rows)
    ys = _experts(xs, block_e, block_rows, block_src, w_gate, w_up, w_down, layer)
    yg = {seg: _sc_gather2(ys, pos[:, seg.row0:seg.row0 + seg.rows]) for seg in segs}
    if final_g is None:
        return yg, wcol, mods, layer
    return tuple(_combine(x, mods, layer, segs, seg, yg[seg], wcol, final_g) for seg in segs)


def _mla_rope_tables(t):
    axis_dim = MLA_ROPE // 2
    row = np.repeat(np.arange(t // GRID_W), GRID_W).astype(np.float64)
    col = np.tile(np.arange(GRID_W), t // GRID_W).astype(np.float64)
    inv = ROPE_BASE ** (-np.arange(0, axis_dim, 2, dtype=np.float64) / axis_dim)
    ar, ac = row[:, None] * inv[None, :], col[:, None] * inv[None, :]
    ones = np.ones((t, LANE - MLA_ROPE))
    cos = np.concatenate([np.cos(ar), np.cos(ar), np.cos(ac), np.cos(ac), ones], axis=-1)
    sin = np.concatenate([-np.sin(ar), np.sin(ar), -np.sin(ac), np.sin(ac), 0.0 * ones], axis=-1)
    return jnp.asarray(cos, F32), jnp.asarray(sin, F32)


def _ret_rot_tables(t, dk):
    inv = ROPE_BASE ** (-np.linspace(0.0, 1.0, dk // 2))
    ang = np.arange(t, dtype=np.float64)[:, None] * inv[None, :]
    return jnp.asarray(np.cos(ang), F32), jnp.asarray(np.sin(ang), F32)


def _dft_tables(n):
    k = np.arange(n, dtype=np.int64)
    ang = (np.outer(k, k) % n).astype(np.float64) * (2.0 * math.pi / n)
    return jnp.asarray(np.cos(ang), BF16), jnp.asarray(np.sin(ang), BF16)


def _mla_weights(w_in, q_g, kv_g, w_uq, w_ukv, w_o):
    d = w_in.shape[0]
    hd = MLA_NOPE + MLA_ROPE
    perm = _rope_perm()
    w_in_p = jnp.concatenate([w_in, w_in[:, MLA_Q_LORA + MLA_KV_LORA + perm]], axis=1)
    uq = w_uq.reshape(MLA_Q_LORA, MLA_HEADS, hd)
    uq = jnp.concatenate([uq, uq[..., MLA_NOPE + perm]], axis=-1)
    ukv = w_ukv.reshape(MLA_KV_LORA, MLA_HEADS, MLA_NOPE + MLA_V)
    return {
        "w_in": w_in_p.astype(BF16),
        "q_g": q_g.reshape(1, -1) * (MLA_NOPE + MLA_ROPE) ** -0.5,
        "kv_g": kv_g.reshape(1, -1),
        "w_q": uq.reshape(MLA_Q_LORA, MLA_HEADS * 2 * LANE).astype(BF16),
        "w_kn": ukv[..., :MLA_NOPE].reshape(MLA_KV_LORA, MLA_HEADS * MLA_NOPE).astype(BF16),
        "w_v": ukv[..., MLA_NOPE:].reshape(MLA_KV_LORA, MLA_HEADS * MLA_V).astype(BF16),
        "w_o": w_o.astype(BF16),
    }


def kernel(x_prompt, x_sample, cache_mla, state_ret, c, c_ctx, norm1_g, norm2_g, ada_w, ada_b, final_norm_g,
           mla_w_in, mla_q_norm_g, mla_kv_norm_g, mla_w_uq, mla_w_ukv, mla_w_o, ret_w_in, ret_decay_f,
           ret_decay_b, ret_w_o, fnet_w, router_w, router_b, moe_w_gate, moe_w_up, moe_w_down):
    b_ctx, t_ctx, d = x_prompt.shape
    b_lat, t_lat, _ = x_sample.shape
    depth = ada_w.shape[0]
    assert b_lat + 1 <= 8
    n_ctx = b_ctx * t_ctx
    ctx = _Seg(0, b_ctx, t_ctx, 0, False)
    lat = _Seg(n_ctx, b_lat, t_lat, 1, True)
    segs = (ctx, lat)

    n_lat = b_lat * t_lat
    n_mla = mla_w_in.shape[0]
    assert n_mla >= 1
    x = None
    new_cache = jnp.zeros((b_ctx, n_mla, t_ctx, MLA_KV_LORA + MLA_ROPE), F32)
    cond8 = jnp.concatenate([c_ctx[None, :], c, jnp.zeros((8 - 1 - b_lat, d), F32)], axis=0)
    mods = _modulation_all(cond8, ada_w, ada_b).reshape(depth, 8, 6, 1, d)

    rw_t = router_w.T.astype(F32)
    rw_hi = rw_t.astype(BF16)
    rw_lo = (rw_t - rw_hi.astype(F32)).astype(BF16)
    rb = router_b.reshape(N_EXPERTS, 1).astype(F32)
    final_g = final_norm_g.reshape(1, d)
    dk = ret_w_in.shape[2] // (8 * RET_HEADS)
    dv = 2 * dk

    states = []
    pending = None
    counters = [0, 0, 0]
    for layer in range(depth):
        kind = layer % 3
        j = counters[kind]
        counters[kind] += 1
        g1 = norm1_g[layer].reshape(1, d)
        g2 = norm2_g[layer].reshape(1, d)
        if kind == 0:
            w = _mla_weights(mla_w_in[j], mla_q_norm_g[j], mla_kv_norm_g[j], mla_w_uq[j], mla_w_ukv[j],
                             mla_w_o[j])
            if x is None:
                xc, xc0, xl, xl0 = x_prompt.reshape(n_ctx, d), 0, x_sample.reshape(n_lat, d), 0
            else:
                xc, xc0, xl, xl0 = x, ctx.row0, x, lat.row0
            if pending is None:
                qc, kc, vc, new_cache = _mla_proj(xc, xc0, mods, layer, ctx, g1, w, None, (new_cache, j, n_mla))
                ql, kl, vl = _mla_proj(xl, xl0, mods, layer, lat, g1, w, _mla_rope_tables(t_lat))
            else:
                qc, kc, vc, new_cache, x = _mla_proj(x, ctx.row0, mods, layer, ctx, g1, w, None,
                                                     (new_cache, j, n_mla), pending)
                ql, kl, vl, x = _mla_proj(x, lat.row0, mods, layer, lat, g1, w, _mla_rope_tables(t_lat), None,
                                          pending)
                xc = xl = x
            past = cache_mla.shape[2]
            cpad = jnp.pad(cache_mla[:, j].reshape(b_lat * past, -1), ((0, 0), (0, LANE - MLA_ROPE)))
            kp, vp = _cache_kv(cpad, w)
            first = x is None
            x = _attention(xc, xc0, n_ctx + n_lat, jnp.zeros((n_ctx + n_lat, d), F32) if first else "inplace",
                           mods, layer, ctx, qc, [(kc, vc, t_ctx)], w["w_o"])
            x = _attention(xl if first else x, xl0, n_ctx + n_lat, x if first else "inplace", mods, layer, lat,
                           ql, [(kl, vl, t_lat), (kp, vp, past)], w["w_o"])
        elif kind == 1:
            w_in = ret_w_in[j]
            qk = RET_HEADS * dk
            k_scale = jnp.concatenate([jnp.ones((qk,), F32), jnp.full((qk,), dk ** -0.5, F32),
                                       jnp.ones((w_in.shape[1] - 2 * qk,), F32)])
            w_in_b = (w_in * k_scale[None, :]).astype(BF16)
            w_o_b = ret_w_o[j].astype(BF16)
            rot = _ret_rot_tables(t_lat, dk)
            n_qk, n_v = 2 * qk // RET_COL, RET_HEADS * dv // RET_COL
            assert (n_qk + n_v) * RET_COL * 2 == w_in.shape[1]
            parts = []
            for seg in segs:
                kinds = (("rotary" if seg is lat else "plain"),) * n_qk + ("plain",) * n_v
                qkv = _ret_proj(x, mods, layer, seg, g1, w_in_b, 0, kinds, rot, dk, pending)
                if pending is not None:
                    qkv, x = qkv
                parts.append((qkv, _ret_proj(x, mods, layer, seg, g1, w_in_b, 1, ("silu",) * (n_qk + n_v), None, dk)))
            yc, s_ctx = _ret_scan(*parts[0], ctx, ret_decay_f[j], ret_decay_b[j], None, True, dk, dv, RET_HEADS)
            yl, _ = _ret_scan(*parts[1], lat, ret_decay_f[j], ret_decay_b[j], state_ret[:, j], False, dk, dv, 1)
            x = _matmul_residual(x, mods, layer, ctx, yc, w_o_b)
            x = _matmul_residual(x, mods, layer, lat, yl, w_o_b)
            states.append(s_ctx)
        else:
            gd = d // FNET_GROUPS
            cc, sc = _dft_tables(gd)
            cs = jnp.concatenate([cc, sc], axis=1)
            w_b = fnet_w[j].astype(BF16)
            for seg in segs:
                ct, st = _dft_tables(seg.seq)
                if pending is None:
                    ac, as_ = _fnet_a(x, mods, layer, seg, g1, cs)
                else:
                    ac, as_, x = _fnet_a(x, mods, layer, seg, g1, cs, pending)
                x = _fnet_b(x, mods, layer, seg, ac, as_, ct, st, w_b, (seg.seq * gd) ** -0.5)
        if layer < depth - 1:
            pending = _moe(x, mods, layer, segs, g2, rw_hi, rw_lo, rb, moe_w_gate, moe_w_up, moe_w_down, None)
        else:
            y_prompt, y_sample = _moe(x, mods, layer, segs, g2, rw_hi, rw_lo, rb, moe_w_gate, moe_w_up,
                                      moe_w_down, final_g)

    new_state = jnp.stack(states, axis=1)
    return (y_prompt.reshape(b_ctx, t_ctx, d), y_sample.reshape(b_lat, t_lat, d), new_cache, new_state)
```

```python
import functools
import math

import jax
import jax.numpy as jnp
import numpy as np
from jax import lax
from jax.experimental import pallas as pl
from jax.experimental.pallas import tpu as pltpu
from jax.experimental.pallas import tpu_sc as plsc

F32 = jnp.float32
BF16 = jnp.bfloat16

GRID_W = 64
MLA_HEADS = 8
MLA_NOPE = 128
MLA_ROPE = 64
MLA_V = 128
MLA_Q_LORA = 384
MLA_KV_LORA = 256
ROPE_BASE = 10000.0
RET_HEADS = 4
RET_CHUNK = 256
FNET_GROUPS = 4
N_EXPERTS = 16
N_EXPERT_GROUPS = 4
EXPERTS_PER_GROUP = 4
D_EXPERT = 512
NORM_EPS = 1e-6

LANE = 128
PROJ_ROWS = 512
ATTN_ROWS = 512
FNET_ROWS = 512
ATTN_SHORT_SEQS = 4
WIDE_TILE = 1024
ROUTER_SUB_ROWS = 256
MOE_ROWS = 1024
MOE_TAIL_ROWS = 256
VMEM_LIMIT = 56 * 1024 * 1024
SC_CORES = 2
SC_SUBCORES = 16
SC_CHUNK = 128


def _cparams(*sem):
    return pltpu.CompilerParams(dimension_semantics=sem, vmem_limit_bytes=VMEM_LIMIT)


def _sigmoid(x):
    return 1.0 / (1.0 + jnp.exp(-x))


def _rms(x):
    return x * lax.rsqrt(jnp.mean(x * x, axis=-1, keepdims=True) + NORM_EPS)


def _modulate(x, g, shift, scale):
    return _rms(x) * (g * (1.0 + scale)) + shift


def _dot(a, b):
    return jnp.dot(a, b, preferred_element_type=F32)


def _dot_nt(a, b):
    return lax.dot_general(a, b, (((1,), (1,)), ((), ())), preferred_element_type=F32)


def _mod_kernel(c_ref, w_ref, b_ref, o_ref):
    c = c_ref[...]
    s = (c * _sigmoid(c)).astype(BF16)
    o_ref[...] = _dot(s, w_ref[...].astype(BF16)) + b_ref[...]


def _modulation_all(cond8, ada_w, ada_b):
    depth, d, d6 = ada_w.shape
    tn = d6 // 4
    return pl.pallas_call(
        _mod_kernel,
        grid=(depth, d6 // tn),
        in_specs=[
            pl.BlockSpec((8, d), lambda l, n: (0, 0)),
            pl.BlockSpec((None, d, tn), lambda l, n: (l, 0, n)),
            pl.BlockSpec((None, 1, tn), lambda l, n: (l, 0, n)),
        ],
        out_specs=pl.BlockSpec((None, 8, tn), lambda l, n: (l, 0, n)),
        out_shape=jax.ShapeDtypeStruct((depth, 8, d6), F32),
        compiler_params=_cparams("parallel", "parallel"),
        name="modulation",
    )(cond8, ada_w, ada_b.reshape(depth, 1, d6))


class _Seg:
    def __init__(self, row0, batch, seq, mod0, per_batch_mod):
        self.row0, self.batch, self.seq = row0, batch, seq
        self.mod0, self.per_batch_mod = mod0, per_batch_mod
        self.rows = batch * seq

    def tile(self, want):
        tm = min(want, self.seq) if self.per_batch_mod else want
        assert self.rows % tm == 0 and self.row0 % tm == 0 and (self.seq % tm == 0 or tm % self.seq == 0)
        return tm

    def seq_tile(self, want):
        tm = min(want, self.seq)
        assert self.seq % tm == 0 and self.row0 % tm == 0
        return tm

    def mod_row(self, tile, tm):
        if self.per_batch_mod:
            return self.mod0 + tile * tm // self.seq
        return self.mod0


def _mod_spec(layer, seg, d, tm, tile_of=lambda *a: a[0]):
    return pl.BlockSpec((None, None, 6, 1, d), lambda *a: (layer, seg.mod_row(tile_of(*a), tm), 0, 0, 0))


def _unpack_halves(p):
    lo = lax.bitcast_convert_type(p << 16, F32)
    hi = lax.bitcast_convert_type(p & jnp.uint32(0xFFFF0000), F32)
    return lo, hi


def _moe_residual(y_ref, w_ref, gate):
    w = w_ref[...]
    lo0, hi0 = _unpack_halves(y_ref[0])
    lo1, hi1 = _unpack_halves(y_ref[1])
    w0, w1 = w[:, 0:1], w[:, 1:2]
    return gate * jnp.concatenate([w0 * lo0 + w1 * lo1, w0 * hi0 + w1 * hi1], axis=-1)


def _take_pending(refs, pending):
    if not pending:
        return refs, lambda x: x
    y_ref, w_ref, pm_ref, *rest = refs
    xo_ref = rest.pop()

    def resolve(x):
        x = x + _moe_residual(y_ref, w_ref, pm_ref[5])
        xo_ref[...] = x
        return x

    return rest, resolve


def _pending_io(pending, seg, tm, d, n):
    yg_by_seg, wcol, mods, layer = pending
    t0 = seg.row0 // tm
    specs = [pl.BlockSpec((2, tm, d // 2), lambda i: (0, i, 0)),
             pl.BlockSpec((tm, 8), lambda i: (t0 + i, 0)),
             _mod_spec(layer, seg, d, tm)]
    return (specs, [yg_by_seg[seg], wcol, mods], pl.BlockSpec((tm, d), lambda i: (t0 + i, 0)),
            jax.ShapeDtypeStruct((n, d), F32))


def _rope_partner(x):
    return pltpu.roll(x, LANE // 2, 1)


def _rope_perm():
    return np.array([l + 16 if l % 32 < 16 else l - 16 for l in range(MLA_ROPE)])


def _store_values(v_ref, v):
    ones = jnp.ones((v.shape[0], LANE), BF16)
    for hd in range(MLA_HEADS):
        v_ref[:, hd * 2 * LANE:hd * 2 * LANE + LANE] = v[:, hd * MLA_V:(hd + 1) * MLA_V].astype(BF16)
        v_ref[:, hd * 2 * LANE + LANE:(hd + 1) * 2 * LANE] = ones


def _mla_proj_kernel(*refs, rope, pending):
    refs, resolve = _take_pending(refs, pending)
    if rope:
        (x_ref, m_ref, g_ref, win_ref, qg_ref, kvg_ref, wq_ref, wkn_ref, wv_ref, cos_ref, sin_ref,
         q_ref, k_ref, v_ref) = refs
    else:
        x_ref, m_ref, g_ref, win_ref, qg_ref, kvg_ref, wq_ref, wkn_ref, wv_ref = refs[:9]
        q_ref, k_ref, v_ref, cache_ref = refs[-4:]
    h = _modulate(resolve(x_ref[...]), g_ref[...], m_ref[0], m_ref[1]).astype(BF16)
    z = _dot(h, win_ref[...])
    cq = z[:, :MLA_Q_LORA]
    ckv = z[:, MLA_Q_LORA:MLA_Q_LORA + MLA_KV_LORA]
    kpe = z[:, MLA_Q_LORA + MLA_KV_LORA:]
    cqn = (_rms(cq) * qg_ref[...]).astype(BF16)
    ckvn = _rms(ckv) * kvg_ref[...]
    ckvb = ckvn.astype(BF16)
    q = _dot(cqn, wq_ref[...])
    kn = _dot(ckvb, wkn_ref[...])
    _store_values(v_ref, _dot(ckvb, wv_ref[...]))
    if rope:
        cos, sin = cos_ref[...], sin_ref[...]
        kpe = kpe * cos + _rope_partner(kpe) * sin
    kpe = jnp.where(lax.broadcasted_iota(jnp.int32, kpe.shape, 1) < MLA_ROPE, kpe, 0.0)
    if not rope:
        seq = cache_ref.shape[1]
        for s in range(cache_ref.shape[0]):
            cache_ref[s, :, :MLA_KV_LORA] = ckvn[s * seq:(s + 1) * seq, :]
            cache_ref[s, :, MLA_KV_LORA:] = kpe[s * seq:(s + 1) * seq, :MLA_ROPE]
    kpe_b = kpe.astype(BF16)
    for hd in range(MLA_HEADS):
        lo = hd * 2 * LANE
        q_ref[:, lo:lo + LANE] = q[:, lo:lo + LANE].astype(BF16)
        qr = q[:, lo + LANE:lo + 2 * LANE]
        if rope:
            qr = qr * cos + _rope_partner(qr) * sin
        q_ref[:, lo + LANE:lo + 2 * LANE] = qr.astype(BF16)
        k_ref[:, lo:lo + LANE] = kn[:, hd * LANE:(hd + 1) * LANE].astype(BF16)
        k_ref[:, lo + LANE:lo + 2 * LANE] = kpe_b


def _mla_proj(x, x_row0, mods, layer, seg, g1, w, rope_tabs, cache_slot=None, pending=None):
    n, d = x.shape
    rope = rope_tabs is not None
    rows = seg.rows
    tm = seg.tile(PROJ_ROWS)
    x_tile0 = x_row0 // tm
    hq = MLA_HEADS * 2 * LANE
    const = lambda i: (0, 0)
    aliases = {}
    in_specs = [
        pl.BlockSpec((tm, d), lambda i: (x_tile0 + i, 0)),
        _mod_spec(layer, seg, d, tm),
        pl.BlockSpec((1, d), const),
        pl.BlockSpec(w["w_in"].shape, const),
        pl.BlockSpec((1, MLA_Q_LORA), const),
        pl.BlockSpec((1, MLA_KV_LORA), const),
        pl.BlockSpec(w["w_q"].shape, const),
        pl.BlockSpec(w["w_kn"].shape, const),
        pl.BlockSpec(w["w_v"].shape, const),
    ]
    args = [x, mods, g1, w["w_in"], w["q_g"], w["kv_g"], w["w_q"], w["w_kn"], w["w_v"]]
    out_specs = [pl.BlockSpec((tm, hq), lambda i: (i, 0))] * 3
    out_shape = [jax.ShapeDtypeStruct((rows, hq), BF16)] * 3
    if rope:
        tab = pl.BlockSpec((tm, LANE), lambda i: (i % (seg.seq // tm), 0))
        in_specs += [tab, tab]
        args += list(rope_tabs)
    else:
        cw = MLA_KV_LORA + MLA_ROPE
        prev, slot, n_slots = cache_slot
        assert tm % seg.seq == 0
        out_specs.append(pl.BlockSpec((tm // seg.seq, None, seg.seq, cw), lambda i: (i, slot, 0, 0)))
        out_shape.append(jax.ShapeDtypeStruct((seg.batch, n_slots, seg.seq, cw), F32))
        in_specs.append(pl.BlockSpec(memory_space=pl.ANY))
        args.append(prev)
        aliases = {len(args) - 1: 3}
    if pending is not None:
        p_specs, p_args, xo_spec, xo_shape = _pending_io(pending, seg, tm, d, n)
        in_specs, args = p_specs + in_specs, p_args + args
        out_specs.append(xo_spec)
        out_shape.append(xo_shape)
        aliases = {k + len(p_args): v for k, v in aliases.items()}
        aliases[len(p_args)] = len(out_shape) - 1
    return pl.pallas_call(
        functools.partial(_mla_proj_kernel, rope=rope, pending=pending is not None),
        grid=(rows // tm,),
        in_specs=in_specs,
        out_specs=out_specs,
        out_shape=out_shape,
        input_output_aliases=aliases,
        compiler_params=_cparams("parallel"),
        name="mla_proj_lat" if rope else "mla_proj_ctx",
    )(*args)


def _cache_kv_kernel(c_ref, wkn_ref, wv_ref, k_ref, v_ref):
    c = c_ref[...]
    ckv = c[:, :MLA_KV_LORA].astype(BF16)
    kpe_b = c[:, MLA_KV_LORA:].astype(BF16)
    kn = _dot(ckv, wkn_ref[...])
    _store_values(v_ref, _dot(ckv, wv_ref[...]))
    for hd in range(MLA_HEADS):
        lo = hd * 2 * LANE
        k_ref[:, lo:lo + LANE] = kn[:, hd * LANE:(hd + 1) * LANE].astype(BF16)
        k_ref[:, lo + LANE:lo + 2 * LANE] = kpe_b


def _cache_kv(cache_pad, w):
    rows, cw = cache_pad.shape
    hq = MLA_HEADS * 2 * LANE
    const = lambda i: (0, 0)
    tm = min(PROJ_ROWS, rows)
    assert rows % tm == 0
    return pl.pallas_call(
        _cache_kv_kernel,
        grid=(rows // tm,),
        in_specs=[
            pl.BlockSpec((tm, cw), lambda i: (i, 0)),
            pl.BlockSpec(w["w_kn"].shape, const),
            pl.BlockSpec(w["w_v"].shape, const),
        ],
        out_specs=[pl.BlockSpec((tm, hq), lambda i: (i, 0))] * 2,
        out_shape=[jax.ShapeDtypeStruct((rows, hq), BF16)] * 2,
        compiler_params=_cparams("parallel"),
        name="mla_cache_kv",
    )(cache_pad, w["w_kn"], w["w_v"])


def _attn_kernel(*refs, n_parts, n_seq):
    q_ref = refs[0]
    kv_refs = refs[1:1 + 2 * n_parts]
    wo_ref, x_ref, m_ref = refs[1 + 2 * n_parts:4 + 2 * n_parts]
    o_ref, acc_ref = refs[-2:]
    tq = q_ref.shape[0] // n_seq
    for sq, hd in [(sq, hd) for sq in range(n_seq) for hd in range(MLA_HEADS)]:
        rows = slice(sq * tq, (sq + 1) * tq)
        kcol = slice(hd * 2 * LANE, (hd + 1) * 2 * LANE)
        keys = [slice(sq * (r.shape[0] // n_seq), (sq + 1) * (r.shape[0] // n_seq)) for r in kv_refs[::2]]
        scores = [_dot_nt(q_ref[rows, kcol], kv_refs[2 * p][keys[p], kcol]) for p in range(n_parts)]
        mx = scores[0].max(axis=-1, keepdims=True)
        for s in scores[1:]:
            mx = jnp.maximum(mx, s.max(axis=-1, keepdims=True))
        out = None
        for p, s in enumerate(scores):
            e = jnp.exp((s - mx).astype(BF16))
            pv = _dot(e, kv_refs[2 * p + 1][keys[p], kcol])
            out = pv if out is None else out + pv
        acc_ref[rows, hd * MLA_V:(hd + 1) * MLA_V] = (out[:, :MLA_V] / out[:, LANE:LANE + MLA_V]).astype(BF16)
    y = _dot(acc_ref[...], wo_ref[...])
    o_ref[...] = x_ref[...] + m_ref[2] * y


def _attention(x, x_row0, n, dest, mods, layer, seg, q, kv_parts, w_o):
    d = x.shape[1]
    hq = MLA_HEADS * 2 * LANE
    hv = MLA_HEADS * MLA_V
    tq = seg.seq_tile(ATTN_ROWS)
    tps = seg.seq // tq
    n_seq = ATTN_SHORT_SEQS if (tps == 1 and not seg.per_batch_mod and seg.batch % ATTN_SHORT_SEQS == 0) else 1
    tq *= n_seq
    x_tile0, out_tile0 = x_row0 // tq, seg.row0 // tq
    in_specs = [pl.BlockSpec((tq, hq), lambda b, i: (b * tps + i, 0))]
    args = [q]
    for k, v, rows in kv_parts:
        mode = dict(pipeline_mode=pl.Buffered(1)) if tps > 1 else {}
        in_specs += [pl.BlockSpec((n_seq * rows, hq), lambda b, i: (b, 0), **mode)] * 2
        args += [k, v]
    in_specs += [
        pl.BlockSpec(w_o.shape, lambda b, i: (0, 0)),
        pl.BlockSpec((tq, d), lambda b, i: (x_tile0 + b * tps + i, 0)),
        _mod_spec(layer, seg, d, tq // n_seq, tile_of=lambda b, i: (b * tps + i) * n_seq),
    ]
    args += [w_o, x, mods]
    if isinstance(dest, str):
        assert dest == "inplace"
        aliases = {len(args) - 2: 0}
    else:
        in_specs.append(pl.BlockSpec(memory_space=pl.ANY))
        args.append(dest)
        aliases = {len(args) - 1: 0}
    return pl.pallas_call(
        functools.partial(_attn_kernel, n_parts=len(kv_parts), n_seq=n_seq),
        grid=(seg.batch // n_seq, tps),
        in_specs=in_specs,
        out_specs=pl.BlockSpec((tq, d), lambda b, i: (out_tile0 + b * tps + i, 0)),
        out_shape=jax.ShapeDtypeStruct((n, d), F32),
        scratch_shapes=[pltpu.VMEM((tq, hv), BF16)],
        input_output_aliases=aliases,
        compiler_params=_cparams("parallel", "arbitrary"),
        name="mla_attention",
    )(*args)


RET_COL = 1024


RET_ROWS = 1024


def _ret_proj_kernel(*refs, kinds, dk, pending):
    refs, resolve = _take_pending(refs, pending)
    rotary = "rotary" in kinds
    if rotary:
        x_ref, m_ref, g_ref, w_ref, cos_ref, sin_ref, z_ref = refs
        cos, sin = cos_ref[...], sin_ref[...]
    else:
        x_ref, m_ref, g_ref, w_ref, z_ref = refs
    h = _modulate(resolve(x_ref[...]), g_ref[...], m_ref[0], m_ref[1]).astype(BF16)
    half = dk // 2
    for j, kind in enumerate(kinds):
        c0 = j * RET_COL
        acc = _dot(h, w_ref[:, c0:c0 + RET_COL])
        if kind == "rotary":
            for hd in range(RET_COL // dk):
                lo = hd * dk
                x1, x2 = acc[:, lo:lo + half], acc[:, lo + half:lo + dk]
                z_ref[:, c0 + lo:c0 + lo + half] = (x1 * cos - x2 * sin).astype(BF16)
                z_ref[:, c0 + lo + half:c0 + lo + dk] = (x1 * sin + x2 * cos).astype(BF16)
        elif kind == "silu":
            z_ref[:, c0:c0 + RET_COL] = (acc * _sigmoid(acc)).astype(BF16)
        else:
            z_ref[:, c0:c0 + RET_COL] = acc.astype(BF16)


def _ret_proj(x, mods, layer, seg, g1, w_in, group, kinds, rot_tabs, dk, pending=None):
    n, d = x.shape
    rows = seg.rows
    tm = seg.tile(RET_ROWS)
    ncol = len(kinds) * RET_COL
    tps = max(seg.seq // tm, 1)
    in_specs = [
        pl.BlockSpec((tm, d), lambda i: (seg.row0 // tm + i, 0)),
        _mod_spec(layer, seg, d, tm),
        pl.BlockSpec((1, d), lambda i: (0, 0)),
        pl.BlockSpec((d, ncol), lambda i: (0, group), pipeline_mode=pl.Buffered(1)),
    ]
    args = [x, mods, g1, w_in]
    if "rotary" in kinds:
        tab = pl.BlockSpec((tm, dk // 2), lambda i: (i % tps, 0))
        in_specs += [tab, tab]
        args += list(rot_tabs)
    out_specs = [pl.BlockSpec((tm, ncol), lambda i: (i, 0))]
    out_shape = [jax.ShapeDtypeStruct((rows, ncol), BF16)]
    aliases = {}
    if pending is not None:
        p_specs, p_args, xo_spec, xo_shape = _pending_io(pending, seg, tm, d, n)
        in_specs, args = p_specs + in_specs, p_args + args
        out_specs.append(xo_spec)
        out_shape.append(xo_shape)
        aliases = {len(p_args): 1}
    res = pl.pallas_call(
        functools.partial(_ret_proj_kernel, kinds=kinds, dk=dk, pending=pending is not None),
        grid=(rows // tm,),
        in_specs=in_specs,
        out_specs=out_specs,
        out_shape=out_shape,
        input_output_aliases=aliases,
        compiler_params=_cparams("parallel"),
        name="ret_proj_" + kinds[0],
    )(*args)
    return res if pending is not None else res[0]


def _log_sigmoid(x):
    return jnp.minimum(x, 0.0) - jnp.log(1.0 + jnp.exp(-jnp.abs(x)))


def _ret_scan_kernel(*refs, has_s0, emit_state, n_chunks, heads, fuse_out):
    refs = list(refs)
    lf_ref, lb_ref, q_ref, k_ref, v_ref, gf_ref, gb_ref = refs[:7]
    pos = 7
    s0_ref = None
    if has_s0:
        s0_ref = refs[pos]
        pos += 1
    if fuse_out:
        wo_ref, x_ref, m_ref, o_ref = refs[pos:pos + 4]
        pos += 4
    else:
        y_ref = refs[pos]
        pos += 1
    sout_ref = None
    if emit_state:
        sout_ref = refs[pos]
        pos += 1
    if fuse_out:
        s_ref, yf_ref, y_ref = refs[pos:]
    else:
        s_ref, yf_ref = refs[pos:]
    c = RET_CHUNK
    dk, dv = s_ref.shape
    ii = lax.broadcasted_iota(jnp.int32, (c, c), 0).astype(F32)
    jj = lax.broadcasted_iota(jnp.int32, (c, c), 1).astype(F32)
    idx = lax.broadcasted_iota(jnp.int32, (c, 1), 0).astype(F32)

    for hd, direction in [(hd, direction) for hd in range(heads) for direction in range(2)]:
        fwd = direction == 0
        kcol, vcol = slice(hd * dk, (hd + 1) * dk), slice(hd * dv, (hd + 1) * dv)
        lg = _log_sigmoid((lf_ref if fwd else lb_ref)[hd])
        rel = (ii - jj) if fwd else (jj - ii)
        keep = rel >= 0
        decay_in = jnp.where(keep, jnp.exp(jnp.where(keep, rel, 0.0) * lg), 0.0)
        decay_q = jnp.exp(((idx + 1.0) if fwd else (c - idx)) * lg)
        decay_k = jnp.exp(((c - 1.0 - idx) if fwd else idx) * lg)
        decay_c = jnp.exp(c * lg)
        g_ref = gf_ref if fwd else gb_ref

        def chunk(cc, state, fwd=fwd, decay_in=decay_in, decay_q=decay_q, decay_k=decay_k, decay_c=decay_c,
                  g_ref=g_ref, kcol=kcol, vcol=vcol):
            r0 = cc * c if isinstance(cc, int) else pl.multiple_of(cc * c, c)
            qc = q_ref[pl.ds(r0, c), kcol]
            kc = k_ref[pl.ds(r0, c), kcol]
            vc = v_ref[pl.ds(r0, c), vcol]
            sc = _dot_nt(qc, kc) * decay_in
            out = _dot(sc.astype(BF16), vc)
            kd_t = (kc.astype(F32) * decay_k).T.astype(BF16)
            new_s = _dot(kd_t, vc)
            if state is not None:
                out = out + decay_q * _dot(qc, state.astype(BF16))
                new_s = decay_c * state + new_s
            s_ref[...] = new_s
            o = _rms(out) * g_ref[pl.ds(r0, c), vcol].astype(F32)
            if fwd:
                yf_ref[pl.ds(r0, c), :] = o
            else:
                y_ref[pl.ds(r0, c), vcol] = (yf_ref[pl.ds(r0, c), :] + o).astype(BF16)

        chunk(0 if fwd else n_chunks - 1, s0_ref[direction, hd] if has_s0 else None)

        def step(ci, carry, fwd=fwd, chunk=chunk):
            chunk(ci if fwd else n_chunks - 1 - ci, s_ref[...])
            return carry

        lax.fori_loop(1, n_chunks, step, 0, unroll=True)
        if emit_state:
            sout_ref[direction, hd] = s_ref[...]
    if fuse_out:
        o_ref[...] = x_ref[...] + m_ref[2] * _dot(y_ref[...], wo_ref[...])


def _ret_scan(qkv, g, seg, logit_f, logit_b, s0, emit_state, dk, dv, heads, out_proj=None):
    rows = seg.batch * seg.seq
    t = seg.seq
    hh = RET_HEADS
    groups = hh // heads
    fuse_out = out_proj is not None
    assert not fuse_out or groups == 1
    v0 = 2 * hh * dk // (heads * dv)
    assert groups * heads == hh and v0 * heads * dv == 2 * hh * dk
    in_specs = [
        pl.BlockSpec((heads, 1, 1), lambda b, h: (h, 0, 0)),
        pl.BlockSpec((heads, 1, 1), lambda b, h: (h, 0, 0)),
        pl.BlockSpec((t, heads * dk), lambda b, h: (b, h)),
        pl.BlockSpec((t, heads * dk), lambda b, h: (b, groups + h)),
        pl.BlockSpec((t, heads * dv), lambda b, h: (b, v0 + h)),
        pl.BlockSpec((t, heads * dv), lambda b, h: (b, h)),
        pl.BlockSpec((t, heads * dv), lambda b, h: (b, groups + h)),
    ]
    args = [logit_f.reshape(hh, 1, 1), logit_b.reshape(hh, 1, 1), qkv, qkv, qkv, g, g]
    state_spec = pl.BlockSpec((None, 2, heads, dk, dv), lambda b, h: (b, 0, h, 0, 0))
    if s0 is not None:
        in_specs.append(state_spec)
        args.append(s0)
    scratch = [pltpu.VMEM((dk, dv), F32), pltpu.VMEM((t, dv), F32)]
    aliases = {}
    if fuse_out:
        x, mods, layer, w_o = out_proj
        n, d = x.shape
        x_spec = pl.BlockSpec((t, d), lambda b, h: (seg.row0 // t + b, 0))
        in_specs += [pl.BlockSpec(w_o.shape, lambda b, h: (0, 0)), x_spec,
                     _mod_spec(layer, seg, d, t, tile_of=lambda b, h: b)]
        args += [w_o, x, mods]
        aliases = {len(args) - 2: 0}
        out_specs, out_shape = [x_spec], [jax.ShapeDtypeStruct((n, d), F32)]
        scratch.append(pltpu.VMEM((t, hh * dv), BF16))
    else:
        out_specs = [pl.BlockSpec((t, heads * dv), lambda b, h: (b, h))]
        out_shape = [jax.ShapeDtypeStruct((rows, hh * dv), BF16)]
    if emit_state:
        out_specs.append(state_spec)
        out_shape.append(jax.ShapeDtypeStruct((seg.batch, 2, hh, dk, dv), F32))
    res = pl.pallas_call(
        functools.partial(_ret_scan_kernel, has_s0=s0 is not None, emit_state=emit_state,
                          n_chunks=t // RET_CHUNK, heads=heads, fuse_out=fuse_out),
        grid=(seg.batch, groups),
        in_specs=in_specs,
        out_specs=out_specs,
        out_shape=out_shape,
        scratch_shapes=scratch,
        input_output_aliases=aliases,
        compiler_params=_cparams("parallel", "parallel"),
        name="ret_scan",
    )(*args)
    return res if emit_state else (res[0], None)


def _mm_res_kernel(a_ref, w_ref, x_ref, m_ref, o_ref):
    o_ref[...] = x_ref[...] + m_ref[2] * _dot(a_ref[...], w_ref[...])


def _matmul_residual(x, mods, layer, seg, a, w):
    n, d = x.shape
    tm = seg.tile(RET_ROWS)
    x_spec = pl.BlockSpec((tm, d), lambda i: (seg.row0 // tm + i, 0))
    return pl.pallas_call(
        _mm_res_kernel,
        grid=(seg.rows // tm,),
        in_specs=[
            pl.BlockSpec((tm, a.shape[1]), lambda i: (i, 0)),
            pl.BlockSpec(w.shape, lambda i: (0, 0)),
            x_spec,
            _mod_spec(layer, seg, d, tm),
        ],
        out_specs=x_spec,
        out_shape=jax.ShapeDtypeStruct((n, d), F32),
        input_output_aliases={2: 0},
        compiler_params=_cparams("parallel"),
        name="matmul_residual",
    )(a, w, x, mods)


def _fnet_a_kernel(*refs, gd, pending):
    refs, resolve = _take_pending(refs, pending)
    x_ref, m_ref, g_ref, cs_ref, ac_ref, as_ref = refs
    h = _modulate(resolve(x_ref[...]), g_ref[...], m_ref[0], m_ref[1]).astype(BF16)
    cs = cs_ref[...]
    for g in range(FNET_GROUPS):
        a = _dot(h[:, g * gd:(g + 1) * gd], cs)
        ac_ref[:, g * gd:(g + 1) * gd] = a[:, :gd].astype(BF16)
        as_ref[:, g * gd:(g + 1) * gd] = a[:, gd:].astype(BF16)


def _fnet_a(x, mods, layer, seg, g1, cs, pending=None):
    n, d = x.shape
    rows = seg.rows
    tm = seg.tile(PROJ_ROWS)
    out = pl.BlockSpec((tm, d), lambda i: (i, 0))
    in_specs = [
        pl.BlockSpec((tm, d), lambda i: (seg.row0 // tm + i, 0)),
        _mod_spec(layer, seg, d, tm),
        pl.BlockSpec((1, d), lambda i: (0, 0)),
        pl.BlockSpec(cs.shape, lambda i: (0, 0)),
    ]
    args = [x, mods, g1, cs]
    out_specs, out_shape, aliases = [out, out], [jax.ShapeDtypeStruct((rows, d), BF16)] * 2, {}
    if pending is not None:
        p_specs, p_args, xo_spec, xo_shape = _pending_io(pending, seg, tm, d, n)
        in_specs, args = p_specs + in_specs, p_args + args
        out_specs.append(xo_spec)
        out_shape.append(xo_shape)
        aliases = {len(p_args): 2}
    return pl.pallas_call(
        functools.partial(_fnet_a_kernel, gd=d // FNET_GROUPS, pending=pending is not None),
        grid=(rows // tm,),
        in_specs=in_specs,
        out_specs=out_specs,
        out_shape=out_shape,
        input_output_aliases=aliases,
        compiler_params=_cparams("parallel"),
        name="fnet_channel_dft",
    )(*args)


def _fnet_b_kernel(ct_ref, st_ref, ac_ref, as_ref, w_ref, x_ref, m_ref, o_ref, *, norm, n_seq):
    t = ac_ref.shape[0] // n_seq
    ct, st = ct_ref[...], st_ref[...]
    f = [_dot(ct, ac_ref[s * t:(s + 1) * t, :]) - _dot(st, as_ref[s * t:(s + 1) * t, :]) for s in range(n_seq)]
    f = (f[0] if n_seq == 1 else jnp.concatenate(f, axis=0)) * norm
    o_ref[...] = x_ref[...] + m_ref[2] * _dot(f.astype(BF16), w_ref[...])


def _fnet_b(x, mods, layer, seg, ac, as_, ct, st, w, norm):
    n, d = x.shape
    t = seg.seq
    tq = seg.seq_tile(FNET_ROWS)
    tps = t // tq
    n_seq = ATTN_SHORT_SEQS if (tps == 1 and not seg.per_batch_mod and seg.batch % ATTN_SHORT_SEQS == 0) else 1
    rows = n_seq * tq
    x_spec = pl.BlockSpec((rows, d), lambda b, i: (seg.row0 // rows + b * tps + i, 0))
    tab = pl.BlockSpec((tq, t), lambda b, i: (i, 0))
    seq = pl.BlockSpec((n_seq * t, d), lambda b, i: (b, 0))
    return pl.pallas_call(
        functools.partial(_fnet_b_kernel, norm=norm, n_seq=n_seq),
        grid=(seg.batch // n_seq, tps),
        in_specs=[tab, tab, seq, seq, pl.BlockSpec(w.shape, lambda b, i: (0, 0)), x_spec,
                  _mod_spec(layer, seg, d, tq, tile_of=lambda b, i: (b * tps + i) * n_seq)],
        out_specs=x_spec,
        out_shape=jax.ShapeDtypeStruct((n, d), F32),
        input_output_aliases={5: 0},
        compiler_params=_cparams("parallel", "arbitrary"),
        name="fnet_position_dft",
    )(ct, st, ac, as_, w, x, mods)


def _pack_halves(a):
    w = a.shape[1] // 2
    bits = lambda v: lax.bitcast_convert_type(v.astype(BF16).astype(F32), jnp.uint32)
    return (bits(a[:, :w]) >> 16) | (bits(a[:, w:]) & jnp.uint32(0xFFFF0000))


def _router_kernel(x_ref, m_ref, g_ref, rwhi_ref, rwlo_ref, rb_ref, h_ref, idx_ref, rank_ref, wcol_ref, cnt_ref,
                   run_ref, tri_ref):
    step = pl.program_id(0)

    @pl.when(step == 0)
    def _():
        run_ref[...] = jnp.zeros_like(run_ref)
        tt = tri_ref.shape[0]
        earlier = lax.broadcasted_iota(jnp.int32, (tt, tt), 0) < lax.broadcasted_iota(jnp.int32, (tt, tt), 1)
        tri_ref[...] = jnp.where(earlier, 1.0, 0.0).astype(BF16)

    parts = []
    for r0 in range(0, x_ref.shape[0], ROUTER_SUB_ROWS):
        rows = slice(r0, r0 + ROUTER_SUB_ROWS)
        h = _modulate(x_ref[rows, :], g_ref[...], m_ref[3], m_ref[4])
        h_ref[rows, :] = _pack_halves(h)
        h_hi = h.astype(BF16)
        h_lo = (h - h_hi.astype(F32)).astype(BF16)
        parts.append(_dot_nt(rwhi_ref[...], h_hi) + (_dot_nt(rwhi_ref[...], h_lo) + _dot_nt(rwlo_ref[...], h_hi)))
    logits = jnp.concatenate(parts, axis=1)
    sc = _sigmoid(logits)
    gr = sc + rb_ref[...]
    gp = EXPERTS_PER_GROUP
    row = lambda a, e: a[e:e + 1, :]
    best_g = None
    for g in range(N_EXPERT_GROUPS):
        vals = [row(gr, g * gp + i) for i in range(gp)]
        gs = None
        for i in range(gp):
            for j in range(i + 1, gp):
                pair = vals[i] + vals[j]
                gs = pair if gs is None else jnp.maximum(gs, pair)
        if best_g is None:
            best_g, best_v = jnp.zeros(gs.shape, jnp.int32), gs
        else:
            better = gs > best_v
            best_g = jnp.where(better, g, best_g)
            best_v = jnp.where(better, gs, best_v)
    sel, raw = [], []
    for i in range(gp):
        s_i, r_i = row(gr, i), row(sc, i)
        for g in range(1, N_EXPERT_GROUPS):
            s_i = jnp.where(best_g == g, row(gr, g * gp + i), s_i)
            r_i = jnp.where(best_g == g, row(sc, g * gp + i), r_i)
        sel.append(s_i)
        raw.append(r_i)

    def argmax_first(vals, raws):
        bi, bv, br = jnp.zeros(vals[0].shape, jnp.int32), vals[0], raws[0]
        for i in range(1, len(vals)):
            better = vals[i] > bv
            bi = jnp.where(better, i, bi)
            bv = jnp.where(better, vals[i], bv)
            br = jnp.where(better, raws[i], br)
        return bi, br

    i1, w1 = argmax_first(sel, raw)
    masked = [jnp.where(i1 == i, -jnp.inf, sel[i]) for i in range(gp)]
    i2, w2 = argmax_first(masked, raw)
    tot = w1 + w2
    e1 = best_g * gp + i1
    e2 = best_g * gp + i2
    idx_ref[0:1, :] = e1
    idx_ref[1:2, :] = e2
    t = e1.shape[1]
    sub = lax.broadcasted_iota(jnp.int32, (8, t), 0)
    w8 = jnp.where(sub == 0, w1 / tot, jnp.where(sub == 1, w2 / tot, 0.0))
    wcol_ref[...] = w8.T
    eio = lax.broadcasted_iota(jnp.int32, (N_EXPERTS, t), 0)
    oh1, oh2 = eio == e1, eio == e2
    oh = jnp.where(oh1, 1.0, jnp.where(oh2, 1.0, 0.0))
    local = _dot(oh.astype(BF16), tri_ref[...])
    rank = local + run_ref[:, 0:1]
    rank_ref[0:1, :] = jnp.sum(jnp.where(oh1, rank, 0.0), axis=0, keepdims=True).astype(jnp.int32)
    rank_ref[1:2, :] = jnp.sum(jnp.where(oh2, rank, 0.0), axis=0, keepdims=True).astype(jnp.int32)
    run_ref[...] = run_ref[...] + jnp.sum(oh, axis=1, keepdims=True)
    cnt_ref[...] = run_ref[...]


def _wide_mod_row(segs, tm):
    ctx, lat = segs
    ctx_tiles = ctx.batch * ctx.seq // tm
    assert ctx_tiles * tm == ctx.batch * ctx.seq and lat.seq % tm == 0
    return lambda i: jnp.where(i < ctx_tiles, ctx.mod0, lat.mod0 + (i - ctx_tiles) // (lat.seq // tm))


def _router(x, mods, layer, segs, g2, rw_hi, rw_lo, rb):
    n, d = x.shape
    tm = WIDE_TILE
    mod_row = _wide_mod_row(segs, tm)
    return pl.pallas_call(
        _router_kernel,
        grid=(n // tm,),
        in_specs=[
            pl.BlockSpec((tm, d), lambda i: (i, 0)),
            pl.BlockSpec((None, None, 6, 1, d), lambda i: (layer, mod_row(i), 0, 0, 0)),
            pl.BlockSpec((1, d), lambda i: (0, 0)),
            pl.BlockSpec(rw_hi.shape, lambda i: (0, 0)),
            pl.BlockSpec(rw_lo.shape, lambda i: (0, 0)),
            pl.BlockSpec(rb.shape, lambda i: (0, 0)),
        ],
        out_specs=[
            pl.BlockSpec((tm, d // 2), lambda i: (i, 0)),
            pl.BlockSpec((2, tm), lambda i: (0, i)),
            pl.BlockSpec((2, tm), lambda i: (0, i)),
            pl.BlockSpec((tm, 8), lambda i: (i, 0)),
            pl.BlockSpec((N_EXPERTS, LANE), lambda i: (0, 0)),
        ],
        out_shape=[
            jax.ShapeDtypeStruct((n, d // 2), jnp.uint32),
            jax.ShapeDtypeStruct((2, n), jnp.int32),
            jax.ShapeDtypeStruct((2, n), jnp.int32),
            jax.ShapeDtypeStruct((n, 8), F32),
            jax.ShapeDtypeStruct((N_EXPERTS, LANE), F32),
        ],
        scratch_shapes=[pltpu.VMEM((N_EXPERTS, LANE), F32), pltpu.VMEM((tm, tm), BF16)],
        compiler_params=_cparams("arbitrary"),
        name="moe_router",
    )(x, mods, g2, rw_hi, rw_lo, rb)


def _expert_kernel(be_ref, br_ref, bs_ref, xs_ref, wg_ref, wu_ref, wd_ref, y_ref, wg_b, wu_b, wd_b):
    i = pl.program_id(0)
    prev = be_ref[jnp.maximum(i - 1, 0)]
    valid = br_ref[i]
    tail_rows = MOE_TAIL_ROWS

    @pl.when(jnp.logical_or(i == 0, be_ref[i] != prev))
    def _():
        wg_b[...] = wg_ref[...].astype(BF16)
        wu_b[...] = wu_ref[...].astype(BF16)
        wd_b[...] = wd_ref[...].astype(BF16)

    def ffn(r0, nrows):
        lo, hi = _unpack_halves(xs_ref[r0:r0 + nrows, :])
        xb = jnp.concatenate([lo.astype(BF16), hi.astype(BF16)], axis=1)
        gate = _dot(xb, wg_b[...])
        hid = (gate * _sigmoid(gate)) * _dot(xb, wu_b[...])
        y_ref[r0:r0 + nrows, :] = _pack_halves(_dot(hid.astype(BF16), wd_b[...]))

    @pl.when(valid > tail_rows)
    def _():
        ffn(0, y_ref.shape[0])

    @pl.when(jnp.logical_and(valid > 0, valid <= tail_rows))
    def _():
        ffn(0, tail_rows)
        y_ref[tail_rows:, :] = jnp.zeros((y_ref.shape[0] - tail_rows, y_ref.shape[1]), y_ref.dtype)


def _experts(xs, block_e, block_rows, block_src, w_gate, w_up, w_down, layer):
    rows, half = xs.shape
    d = 2 * half
    de = w_gate.shape[-1]
    n_blocks = rows // MOE_ROWS
    grid_spec = pltpu.PrefetchScalarGridSpec(
        num_scalar_prefetch=3,
        grid=(n_blocks,),
        in_specs=[
            pl.BlockSpec((MOE_ROWS, half), lambda i, be, br, bs: (bs[i], 0)),
            pl.BlockSpec((None, None, d, de), lambda i, be, br, bs: (layer, be[i], 0, 0)),
            pl.BlockSpec((None, None, d, de), lambda i, be, br, bs: (layer, be[i], 0, 0)),
            pl.BlockSpec((None, None, de, d), lambda i, be, br, bs: (layer, be[i], 0, 0)),
        ],
        out_specs=pl.BlockSpec((MOE_ROWS, half), lambda i, be, br, bs: (bs[i], 0)),
        scratch_shapes=[pltpu.VMEM((d, de), BF16), pltpu.VMEM((d, de), BF16), pltpu.VMEM((de, d), BF16)],
    )
    return pl.pallas_call(
        _expert_kernel,
        grid_spec=grid_spec,
        out_shape=jax.ShapeDtypeStruct((rows, half), jnp.uint32),
        compiler_params=_cparams("arbitrary"),
        name="moe_experts",
    )(block_e, block_rows, block_src, xs, w_gate, w_up, w_down)


def _combine_kernel(x_ref, m_ref, y_ref, w_ref, fg_ref, o_ref):
    x = x_ref[...] + _moe_residual(y_ref, w_ref, m_ref[5])
    o_ref[...] = _rms(x) * fg_ref[...]


def _combine(x, mods, layer, segs, seg, yg, wcol, final_g):
    n, d = x.shape
    tm = WIDE_TILE
    mod_row = _wide_mod_row(segs, tm)
    t0 = seg.row0 // tm
    steps = seg.rows // tm
    in_specs = [pl.BlockSpec((tm, d), lambda i: (t0 + i, 0)),
                pl.BlockSpec((None, None, 6, 1, d), lambda i: (layer, mod_row(t0 + i), 0, 0, 0)),
                pl.BlockSpec((2, tm, d // 2), lambda i: (0, i, 0)),
                pl.BlockSpec((tm, 8), lambda i: (t0 + i, 0)),
                pl.BlockSpec((1, d), lambda i: (0, 0))]
    return pl.pallas_call(
        _combine_kernel,
        grid=(steps,),
        in_specs=in_specs,
        out_specs=pl.BlockSpec((tm, d), lambda i: (i, 0)),
        out_shape=jax.ShapeDtypeStruct((seg.rows, d), F32),
        compiler_params=_cparams("parallel"),
        name="moe_combine_final",
    )(x, mods, yg, wcol, final_g)


def _sc_mesh():
    return plsc.VectorSubcoreMesh(core_axis_name="c", subcore_axis_name="s")


def _sc_worker_split(n):
    workers = SC_CORES * SC_SUBCORES
    per = n // workers
    assert per * workers == n and per % SC_CHUNK == 0
    return workers, per, per // SC_CHUNK


def _sc_dispatch(h, pos, rows):
    n, w = h.shape
    workers, per, chunks = _sc_worker_split(n)

    @functools.partial(
        pl.kernel, out_type=jax.ShapeDtypeStruct((rows, w), h.dtype), mesh=_sc_mesh(),
        scratch_types=[pltpu.VMEM((2, chunks, SC_CHUNK), jnp.int32), pltpu.VMEM((SC_CHUNK, w), h.dtype)],
        name="moe_dispatch_scatter")
    def scatter_rows(h_hbm, pos_hbm, xs_hbm, idx_v, rows_v):
        wid = lax.axis_index("s") * SC_CORES + lax.axis_index("c")
        pltpu.sync_copy(pos_hbm.at[0, wid], idx_v.at[0])
        pltpu.sync_copy(pos_hbm.at[1, wid], idx_v.at[1])

        @pl.loop(0, chunks)
        def _(c):
            pltpu.sync_copy(h_hbm.at[pl.ds(wid * per + c * SC_CHUNK, SC_CHUNK)], rows_v)
            pltpu.sync_copy(rows_v, xs_hbm.at[idx_v.at[0, c]])
            pltpu.sync_copy(rows_v, xs_hbm.at[idx_v.at[1, c]])

    return scatter_rows(h, pos.reshape(2, workers, chunks, SC_CHUNK))


def _sc_gather2(ys, pos):
    _, w = ys.shape
    n = pos.shape[1]
    workers, per, chunks = _sc_worker_split(n)

    @functools.partial(
        pl.kernel, out_type=jax.ShapeDtypeStruct((2, n, w), ys.dtype), mesh=_sc_mesh(),
        scratch_types=[pltpu.VMEM((2, chunks, SC_CHUNK), jnp.int32), pltpu.VMEM((SC_CHUNK, w), ys.dtype),
                       pltpu.SemaphoreType.DMA],
        name="moe_combine_gather")
    def gather_rows(ys_hbm, pos_hbm, out_hbm, idx_v, rows_v, sem):
        wid = lax.axis_index("s") * SC_CORES + lax.axis_index("c")
        pltpu.sync_copy(pos_hbm.at[0, wid], idx_v.at[0])
        pltpu.sync_copy(pos_hbm.at[1, wid], idx_v.at[1])

        @pl.loop(0, chunks)
        def _(c):
            for k in range(2):
                pltpu.async_copy(ys_hbm.at[idx_v.at[k, c]], rows_v, sem).wait()
                pltpu.sync_copy(rows_v, out_hbm.at[k, pl.ds(wid * per + c * SC_CHUNK, SC_CHUNK)])

    return gather_rows(ys, pos.reshape(2, workers, chunks, SC_CHUNK))


def _dispatch_plan(idx, rank, counts):
    n = idx.shape[1]
    padded = (counts + MOE_ROWS - 1) // MOE_ROWS * MOE_ROWS
    pad_end = jnp.cumsum(padded)
    pad_start = pad_end - padded
    experts = jnp.arange(N_EXPERTS, dtype=jnp.int32)
    start_of = jnp.sum(jnp.where(idx[..., None] == experts, pad_start, 0), axis=-1)
    pos = start_of + rank
    n_blocks = 2 * n // MOE_ROWS + N_EXPERTS
    starts = jnp.arange(n_blocks, dtype=jnp.int32) * MOE_ROWS
    block_e = jnp.minimum(jnp.sum(starts[:, None] >= pad_end[None, :], axis=1), N_EXPERTS - 1).astype(jnp.int32)
    seg_end = jnp.sum(jnp.where(block_e[:, None] == experts, pad_start + counts, 0), axis=-1)
    block_rows = jnp.clip(seg_end - starts, 0, MOE_ROWS).astype(jnp.int32)
    last_used = pad_end[-1] // MOE_ROWS - 1
    block_src = jnp.minimum(jnp.arange(n_blocks, dtype=jnp.int32), last_used).astype(jnp.int32)
    block_e = jnp.sum(jnp.where(block_src[:, None] == jnp.arange(n_blocks)[None, :], block_e[None, :], 0), axis=1)
    return pos, block_e.astype(jnp.int32), block_rows, block_src, n_blocks * MOE_ROWS


def _moe(x, mods, layer, segs, g2, rw_hi, rw_lo, rb, w_gate, w_up, w_down, final_g):
    h2p, idx, rank, wcol, cnt = _router(x, mods, layer, segs, g2, rw_hi, rw_lo, rb)
    pos, block_e, block_rows, block_src, rows = _dispatch_plan(idx, rank, cnt[:, 0].astype(jnp.int32))
    xs = _sc_dispatch(h2p, pos, rows)
    ys = _experts(xs, block_e, block_rows, block_src, w_gate, w_up, w_down, layer)
    yg = {seg: _sc_gather2(ys, pos[:, seg.row0:seg.row0 + seg.rows]) for seg in segs}
    if final_g is None:
        return yg, wcol, mods, layer
    return tuple(_combine(x, mods, layer, segs, seg, yg[seg], wcol, final_g) for seg in segs)


def _mla_rope_tables(t):
    axis_dim = MLA_ROPE // 2
    row = np.repeat(np.arange(t // GRID_W), GRID_W).astype(np.float64)
    col = np.tile(np.arange(GRID_W), t // GRID_W).astype(np.float64)
    inv = ROPE_BASE ** (-np.arange(0, axis_dim, 2, dtype=np.float64) / axis_dim)
    ar, ac = row[:, None] * inv[None, :], col[:, None] * inv[None, :]
    ones = np.ones((t, LANE - MLA_ROPE))
    cos = np.concatenate([np.cos(ar), np.cos(ar), np.cos(ac), np.cos(ac), ones], axis=-1)
    sin = np.concatenate([-np.sin(ar), np.sin(ar), -np.sin(ac), np.sin(ac), 0.0 * ones], axis=-1)
    return jnp.asarray(cos, F32), jnp.asarray(sin, F32)


def _ret_rot_tables(t, dk):
    inv = ROPE_BASE ** (-np.linspace(0.0, 1.0, dk // 2))
    ang = np.arange(t, dtype=np.float64)[:, None] * inv[None, :]
    return jnp.asarray(np.cos(ang), F32), jnp.asarray(np.sin(ang), F32)


def _dft_tables(n):
    k = np.arange(n, dtype=np.int64)
    ang = (np.outer(k, k) % n).astype(np.float64) * (2.0 * math.pi / n)
    return jnp.asarray(np.cos(ang), BF16), jnp.asarray(np.sin(ang), BF16)


def _mla_weights(w_in, q_g, kv_g, w_uq, w_ukv, w_o):
    d = w_in.shape[0]
    hd = MLA_NOPE + MLA_ROPE
    perm = _rope_perm()
    w_in_p = jnp.concatenate([w_in, w_in[:, MLA_Q_LORA + MLA_KV_LORA + perm]], axis=1)
    uq = w_uq.reshape(MLA_Q_LORA, MLA_HEADS, hd)
    uq = jnp.concatenate([uq, uq[..., MLA_NOPE + perm]], axis=-1)
    ukv = w_ukv.reshape(MLA_KV_LORA, MLA_HEADS, MLA_NOPE + MLA_V)
    return {
        "w_in": w_in_p.astype(BF16),
        "q_g": q_g.reshape(1, -1) * (MLA_NOPE + MLA_ROPE) ** -0.5,
        "kv_g": kv_g.reshape(1, -1),
        "w_q": uq.reshape(MLA_Q_LORA, MLA_HEADS * 2 * LANE).astype(BF16),
        "w_kn": ukv[..., :MLA_NOPE].reshape(MLA_KV_LORA, MLA_HEADS * MLA_NOPE).astype(BF16),
        "w_v": ukv[..., MLA_NOPE:].reshape(MLA_KV_LORA, MLA_HEADS * MLA_V).astype(BF16),
        "w_o": w_o.astype(BF16),
    }


def kernel(x_prompt, x_sample, cache_mla, state_ret, c, c_ctx, norm1_g, norm2_g, ada_w, ada_b, final_norm_g,
           mla_w_in, mla_q_norm_g, mla_kv_norm_g, mla_w_uq, mla_w_ukv, mla_w_o, ret_w_in, ret_decay_f,
           ret_decay_b, ret_w_o, fnet_w, router_w, router_b, moe_w_gate, moe_w_up, moe_w_down):
    b_ctx, t_ctx, d = x_prompt.shape
    b_lat, t_lat, _ = x_sample.shape
    depth = ada_w.shape[0]
    assert b_lat + 1 <= 8
    n_ctx = b_ctx * t_ctx
    ctx = _Seg(0, b_ctx, t_ctx, 0, False)
    lat = _Seg(n_ctx, b_lat, t_lat, 1, True)
    segs = (ctx, lat)

    n_lat = b_lat * t_lat
    n_mla = mla_w_in.shape[0]
    assert n_mla >= 1
    x = None
    new_cache = jnp.zeros((b_ctx, n_mla, t_ctx, MLA_KV_LORA + MLA_ROPE), F32)
    cond8 = jnp.concatenate([c_ctx[None, :], c, jnp.zeros((8 - 1 - b_lat, d), F32)], axis=0)
    mods = _modulation_all(cond8, ada_w, ada_b).reshape(depth, 8, 6, 1, d)

    rw_t = router_w.T.astype(F32)
    rw_hi = rw_t.astype(BF16)
    rw_lo = (rw_t - rw_hi.astype(F32)).astype(BF16)
    rb = router_b.reshape(N_EXPERTS, 1).astype(F32)
    final_g = final_norm_g.reshape(1, d)
    dk = ret_w_in.shape[2] // (8 * RET_HEADS)
    dv = 2 * dk

    states = []
    pending = None
    counters = [0, 0, 0]
    for layer in range(depth):
        kind = layer % 3
        j = counters[kind]
        counters[kind] += 1
        g1 = norm1_g[layer].reshape(1, d)
        g2 = norm2_g[layer].reshape(1, d)
        if kind == 0:
            w = _mla_weights(mla_w_in[j], mla_q_norm_g[j], mla_kv_norm_g[j], mla_w_uq[j], mla_w_ukv[j],
                             mla_w_o[j])
            if x is None:
                xc, xc0, xl, xl0 = x_prompt.reshape(n_ctx, d), 0, x_sample.reshape(n_lat, d), 0
            else:
                xc, xc0, xl, xl0 = x, ctx.row0, x, lat.row0
            if pending is None:
                qc, kc, vc, new_cache = _mla_proj(xc, xc0, mods, layer, ctx, g1, w, None, (new_cache, j, n_mla))
                ql, kl, vl = _mla_proj(xl, xl0, mods, layer, lat, g1, w, _mla_rope_tables(t_lat))
            else:
                qc, kc, vc, new_cache, x = _mla_proj(x, ctx.row0, mods, layer, ctx, g1, w, None,
                                                     (new_cache, j, n_mla), pending)
                ql, kl, vl, x = _mla_proj(x, lat.row0, mods, layer, lat, g1, w, _mla_rope_tables(t_lat), None,
                                          pending)
                xc = xl = x
            past = cache_mla.shape[2]
            cpad = jnp.pad(cache_mla[:, j].reshape(b_lat * past, -1), ((0, 0), (0, LANE - MLA_ROPE)))
            kp, vp = _cache_kv(cpad, w)
            first = x is None
            x = _attention(xc, xc0, n_ctx + n_lat, jnp.zeros((n_ctx + n_lat, d), F32) if first else "inplace",
                           mods, layer, ctx, qc, [(kc, vc, t_ctx)], w["w_o"])
            x = _attention(xl if first else x, xl0, n_ctx + n_lat, x if first else "inplace", mods, layer, lat,
                           ql, [(kl, vl, t_lat), (kp, vp, past)], w["w_o"])
        elif kind == 1:
            w_in = ret_w_in[j]
            qk = RET_HEADS * dk
            k_scale = jnp.concatenate([jnp.ones((qk,), F32), jnp.full((qk,), dk ** -0.5, F32),
                                       jnp.ones((w_in.shape[1] - 2 * qk,), F32)])
            w_in_b = (w_in * k_scale[None, :]).astype(BF16)
            w_o_b = ret_w_o[j].astype(BF16)
            rot = _ret_rot_tables(t_lat, dk)
            n_qk, n_v = 2 * qk // RET_COL, RET_HEADS * dv // RET_COL
            assert (n_qk + n_v) * RET_COL * 2 == w_in.shape[1]
            parts = []
            for seg in segs:
                kinds = (("rotary" if seg is lat else "plain"),) * n_qk + ("plain",) * n_v
                qkv = _ret_proj(x, mods, layer, seg, g1, w_in_b, 0, kinds, rot, dk, pending)
                if pending is not None:
                    qkv, x = qkv
                parts.append((qkv, _ret_proj(x, mods, layer, seg, g1, w_in_b, 1, ("silu",) * (n_qk + n_v), None, dk)))
            x, s_ctx = _ret_scan(*parts[0], ctx, ret_decay_f[j], ret_decay_b[j], None, True, dk, dv, RET_HEADS,
                                 out_proj=(x, mods, layer, w_o_b))
            yl, _ = _ret_scan(*parts[1], lat, ret_decay_f[j], ret_decay_b[j], state_ret[:, j], False, dk, dv, 1)
            x = _matmul_residual(x, mods, layer, lat, yl, w_o_b)
            states.append(s_ctx)
        else:
            gd = d // FNET_GROUPS
            cc, sc = _dft_tables(gd)
            cs = jnp.concatenate([cc, sc], axis=1)
            w_b = fnet_w[j].astype(BF16)
            for seg in segs:
                ct, st = _dft_tables(seg.seq)
                if pending is None:
                    ac, as_ = _fnet_a(x, mods, layer, seg, g1, cs)
                else:
                    ac, as_, x = _fnet_a(x, mods, layer, seg, g1, cs, pending)
                x = _fnet_b(x, mods, layer, seg, ac, as_, ct, st, w_b, (seg.seq * gd) ** -0.5)
        if layer < depth - 1:
            pending = _moe(x, mods, layer, segs, g2, rw_hi, rw_lo, rb, moe_w_gate, moe_w_up, moe_w_down, None)
        else:
            y_prompt, y_sample = _moe(x, mods, layer, segs, g2, rw_hi, rw_lo, rb, moe_w_gate, moe_w_up,
                                      moe_w_down, final_g)

    new_state = jnp.stack(states, axis=1)
    return (y_prompt.reshape(b_ctx, t_ctx, d), y_sample.reshape(b_lat, t_lat, d), new_cache, new_state)
```

```python
import functools
import math

import jax
import jax.numpy as jnp
import numpy as np
from jax import lax
from jax.experimental import pallas as pl
from jax.experimental.pallas import tpu as pltpu
from jax.experimental.pallas import tpu_sc as plsc

F32 = jnp.float32
BF16 = jnp.bfloat16

GRID_W = 64
MLA_HEADS = 8
MLA_NOPE = 128
MLA_ROPE = 64
MLA_V = 128
MLA_Q_LORA = 384
MLA_KV_LORA = 256
ROPE_BASE = 10000.0
RET_HEADS = 4
RET_CHUNK = 256
FNET_GROUPS = 4
N_EXPERTS = 16
N_EXPERT_GROUPS = 4
EXPERTS_PER_GROUP = 4
D_EXPERT = 512
NORM_EPS = 1e-6

LANE = 128
PROJ_ROWS = 512
ATTN_ROWS = 512
FNET_ROWS = 512
ATTN_SHORT_SEQS = 4
WIDE_TILE = 1024
ROUTER_SUB_ROWS = 256
MOE_ROWS = 1024
MOE_TAIL_ROWS = 256
VMEM_LIMIT = 56 * 1024 * 1024
SC_CORES = 2
SC_SUBCORES = 16
SC_CHUNK = 128


def _cparams(*sem):
    return pltpu.CompilerParams(dimension_semantics=sem, vmem_limit_bytes=VMEM_LIMIT)


def _sigmoid(x):
    return 1.0 / (1.0 + jnp.exp(-x))


def _rms(x):
    return x * lax.rsqrt(jnp.mean(x * x, axis=-1, keepdims=True) + NORM_EPS)


def _modulate(x, g, shift, scale):
    return _rms(x) * (g * (1.0 + scale)) + shift


def _dot(a, b):
    return jnp.dot(a, b, preferred_element_type=F32)


def _dot_nt(a, b):
    return lax.dot_general(a, b, (((1,), (1,)), ((), ())), preferred_element_type=F32)


def _mod_kernel(c_ref, w_ref, b_ref, o_ref):
    c = c_ref[...]
    s = (c * _sigmoid(c)).astype(BF16)
    o_ref[...] = _dot(s, w_ref[...].astype(BF16)) + b_ref[...]


def _modulation_all(cond8, ada_w, ada_b):
    depth, d, d6 = ada_w.shape
    tn = d6 // 4
    return pl.pallas_call(
        _mod_kernel,
        grid=(depth, d6 // tn),
        in_specs=[
            pl.BlockSpec((8, d), lambda l, n: (0, 0)),
            pl.BlockSpec((None, d, tn), lambda l, n: (l, 0, n)),
            pl.BlockSpec((None, 1, tn), lambda l, n: (l, 0, n)),
        ],
        out_specs=pl.BlockSpec((None, 8, tn), lambda l, n: (l, 0, n)),
        out_shape=jax.ShapeDtypeStruct((depth, 8, d6), F32),
        compiler_params=_cparams("parallel", "parallel"),
        name="modulation",
    )(cond8, ada_w, ada_b.reshape(depth, 1, d6))


class _Seg:
    def __init__(self, row0, batch, seq, mod0, per_batch_mod):
        self.row0, self.batch, self.seq = row0, batch, seq
        self.mod0, self.per_batch_mod = mod0, per_batch_mod
        self.rows = batch * seq

    def tile(self, want):
        tm = min(want, self.seq) if self.per_batch_mod else want
        assert self.rows % tm == 0 and self.row0 % tm == 0 and (self.seq % tm == 0 or tm % self.seq == 0)
        return tm

    def seq_tile(self, want):
        tm = min(want, self.seq)
        assert self.seq % tm == 0 and self.row0 % tm == 0
        return tm

    def mod_row(self, tile, tm):
        if self.per_batch_mod:
            return self.mod0 + tile * tm // self.seq
        return self.mod0


def _mod_spec(layer, seg, d, tm, tile_of=lambda *a: a[0]):
    return pl.BlockSpec((None, None, 6, 1, d), lambda *a: (layer, seg.mod_row(tile_of(*a), tm), 0, 0, 0))


def _unpack_halves(p):
    lo = lax.bitcast_convert_type(p << 16, F32)
    hi = lax.bitcast_convert_type(p & jnp.uint32(0xFFFF0000), F32)
    return lo, hi


def _moe_residual(y_ref, w_ref, gate):
    w = w_ref[...]
    lo0, hi0 = _unpack_halves(y_ref[0])
    lo1, hi1 = _unpack_halves(y_ref[1])
    w0, w1 = w[:, 0:1], w[:, 1:2]
    return gate * jnp.concatenate([w0 * lo0 + w1 * lo1, w0 * hi0 + w1 * hi1], axis=-1)


def _take_pending(refs, pending):
    if not pending:
        return refs, lambda x: x
    y_ref, w_ref, pm_ref, *rest = refs
    xo_ref = rest.pop()

    def resolve(x):
        x = x + _moe_residual(y_ref, w_ref, pm_ref[5])
        xo_ref[...] = x
        return x

    return rest, resolve


def _pending_io(pending, seg, tm, d, n):
    yg_by_seg, wcol, mods, layer = pending
    t0 = seg.row0 // tm
    specs = [pl.BlockSpec((2, tm, d // 2), lambda i: (0, i, 0)),
             pl.BlockSpec((tm, 8), lambda i: (t0 + i, 0)),
             _mod_spec(layer, seg, d, tm)]
    return (specs, [yg_by_seg[seg], wcol, mods], pl.BlockSpec((tm, d), lambda i: (t0 + i, 0)),
            jax.ShapeDtypeStruct((n, d), F32))


def _rope_partner(x):
    return pltpu.roll(x, LANE // 2, 1)


def _rope_perm():
    return np.array([l + 16 if l % 32 < 16 else l - 16 for l in range(MLA_ROPE)])


def _store_values(v_ref, v):
    ones = jnp.ones((v.shape[0], LANE), BF16)
    for hd in range(MLA_HEADS):
        v_ref[:, hd * 2 * LANE:hd * 2 * LANE + LANE] = v[:, hd * MLA_V:(hd + 1) * MLA_V].astype(BF16)
        v_ref[:, hd * 2 * LANE + LANE:(hd + 1) * 2 * LANE] = ones


def _mla_proj_kernel(*refs, rope, pending):
    refs, resolve = _take_pending(refs, pending)
    if rope:
        (x_ref, m_ref, g_ref, win_ref, qg_ref, kvg_ref, wq_ref, wkn_ref, wv_ref, cos_ref, sin_ref,
         q_ref, k_ref, v_ref) = refs
    else:
        x_ref, m_ref, g_ref, win_ref, qg_ref, kvg_ref, wq_ref, wkn_ref, wv_ref = refs[:9]
        q_ref, k_ref, v_ref, cache_ref = refs[-4:]
    h = _modulate(resolve(x_ref[...]), g_ref[...], m_ref[0], m_ref[1]).astype(BF16)
    z = _dot(h, win_ref[...])
    cq = z[:, :MLA_Q_LORA]
    ckv = z[:, MLA_Q_LORA:MLA_Q_LORA + MLA_KV_LORA]
    kpe = z[:, MLA_Q_LORA + MLA_KV_LORA:]
    cqn = (_rms(cq) * qg_ref[...]).astype(BF16)
    ckvn = _rms(ckv) * kvg_ref[...]
    ckvb = ckvn.astype(BF16)
    q = _dot(cqn, wq_ref[...])
    kn = _dot(ckvb, wkn_ref[...])
    _store_values(v_ref, _dot(ckvb, wv_ref[...]))
    if rope:
        cos, sin = cos_ref[...], sin_ref[...]
        kpe = kpe * cos + _rope_partner(kpe) * sin
    kpe = jnp.where(lax.broadcasted_iota(jnp.int32, kpe.shape, 1) < MLA_ROPE, kpe, 0.0)
    if not rope:
        seq = cache_ref.shape[1]
        for s in range(cache_ref.shape[0]):
            cache_ref[s, :, :MLA_KV_LORA] = ckvn[s * seq:(s + 1) * seq, :]
            cache_ref[s, :, MLA_KV_LORA:] = kpe[s * seq:(s + 1) * seq, :MLA_ROPE]
    kpe_b = kpe.astype(BF16)
    for hd in range(MLA_HEADS):
        lo = hd * 2 * LANE
        q_ref[:, lo:lo + LANE] = q[:, lo:lo + LANE].astype(BF16)
        qr = q[:, lo + LANE:lo + 2 * LANE]
        if rope:
            qr = qr * cos + _rope_partner(qr) * sin
        q_ref[:, lo + LANE:lo + 2 * LANE] = qr.astype(BF16)
        k_ref[:, lo:lo + LANE] = kn[:, hd * LANE:(hd + 1) * LANE].astype(BF16)
        k_ref[:, lo + LANE:lo + 2 * LANE] = kpe_b


def _mla_proj(x, x_row0, mods, layer, seg, g1, w, rope_tabs, cache_slot=None, pending=None):
    n, d = x.shape
    rope = rope_tabs is not None
    rows = seg.rows
    tm = seg.tile(PROJ_ROWS)
    x_tile0 = x_row0 // tm
    hq = MLA_HEADS * 2 * LANE
    const = lambda i: (0, 0)
    aliases = {}
    in_specs = [
        pl.BlockSpec((tm, d), lambda i: (x_tile0 + i, 0)),
        _mod_spec(layer, seg, d, tm),
        pl.BlockSpec((1, d), const),
        pl.BlockSpec(w["w_in"].shape, const),
        pl.BlockSpec((1, MLA_Q_LORA), const),
        pl.BlockSpec((1, MLA_KV_LORA), const),
        pl.BlockSpec(w["w_q"].shape, const),
        pl.BlockSpec(w["w_kn"].shape, const),
        pl.BlockSpec(w["w_v"].shape, const),
    ]
    args = [x, mods, g1, w["w_in"], w["q_g"], w["kv_g"], w["w_q"], w["w_kn"], w["w_v"]]
    out_specs = [pl.BlockSpec((tm, hq), lambda i: (i, 0))] * 3
    out_shape = [jax.ShapeDtypeStruct((rows, hq), BF16)] * 3
    if rope:
        tab = pl.BlockSpec((tm, LANE), lambda i: (i % (seg.seq // tm), 0))
        in_specs += [tab, tab]
        args += list(rope_tabs)
    else:
        cw = MLA_KV_LORA + MLA_ROPE
        prev, slot, n_slots = cache_slot
        assert tm % seg.seq == 0
        out_specs.append(pl.BlockSpec((tm // seg.seq, None, seg.seq, cw), lambda i: (i, slot, 0, 0)))
        out_shape.append(jax.ShapeDtypeStruct((seg.batch, n_slots, seg.seq, cw), F32))
        in_specs.append(pl.BlockSpec(memory_space=pl.ANY))
        args.append(prev)
        aliases = {len(args) - 1: 3}
    if pending is not None:
        p_specs, p_args, xo_spec, xo_shape = _pending_io(pending, seg, tm, d, n)
        in_specs, args = p_specs + in_specs, p_args + args
        out_specs.append(xo_spec)
        out_shape.append(xo_shape)
        aliases = {k + len(p_args): v for k, v in aliases.items()}
        aliases[len(p_args)] = len(out_shape) - 1
    return pl.pallas_call(
        functools.partial(_mla_proj_kernel, rope=rope, pending=pending is not None),
        grid=(rows // tm,),
        in_specs=in_specs,
        out_specs=out_specs,
        out_shape=out_shape,
        input_output_aliases=aliases,
        compiler_params=_cparams("parallel"),
        name="mla_proj_lat" if rope else "mla_proj_ctx",
    )(*args)


def _cache_kv_kernel(c_ref, wkn_ref, wv_ref, k_ref, v_ref):
    c = c_ref[...]
    ckv = c[:, :MLA_KV_LORA].astype(BF16)
    kpe_b = c[:, MLA_KV_LORA:].astype(BF16)
    kn = _dot(ckv, wkn_ref[...])
    _store_values(v_ref, _dot(ckv, wv_ref[...]))
    for hd in range(MLA_HEADS):
        lo = hd * 2 * LANE
        k_ref[:, lo:lo + LANE] = kn[:, hd * LANE:(hd + 1) * LANE].astype(BF16)
        k_ref[:, lo + LANE:lo + 2 * LANE] = kpe_b


def _cache_kv(cache_pad, w):
    rows, cw = cache_pad.shape
    hq = MLA_HEADS * 2 * LANE
    const = lambda i: (0, 0)
    tm = min(PROJ_ROWS, rows)
    assert rows % tm == 0
    return pl.pallas_call(
        _cache_kv_kernel,
        grid=(rows // tm,),
        in_specs=[
            pl.BlockSpec((tm, cw), lambda i: (i, 0)),
            pl.BlockSpec(w["w_kn"].shape, const),
            pl.BlockSpec(w["w_v"].shape, const),
        ],
        out_specs=[pl.BlockSpec((tm, hq), lambda i: (i, 0))] * 2,
        out_shape=[jax.ShapeDtypeStruct((rows, hq), BF16)] * 2,
        compiler_params=_cparams("parallel"),
        name="mla_cache_kv",
    )(cache_pad, w["w_kn"], w["w_v"])


def _attn_kernel(*refs, n_parts, n_seq):
    q_ref = refs[0]
    kv_refs = refs[1:1 + 2 * n_parts]
    wo_ref, x_ref, m_ref = refs[1 + 2 * n_parts:4 + 2 * n_parts]
    o_ref, acc_ref = refs[-2:]
    tq = q_ref.shape[0] // n_seq
    for sq, hd in [(sq, hd) for sq in range(n_seq) for hd in range(MLA_HEADS)]:
        rows = slice(sq * tq, (sq + 1) * tq)
        kcol = slice(hd * 2 * LANE, (hd + 1) * 2 * LANE)
        keys = [slice(sq * (r.shape[0] // n_seq), (sq + 1) * (r.shape[0] // n_seq)) for r in kv_refs[::2]]
        scores = [_dot_nt(q_ref[rows, kcol], kv_refs[2 * p][keys[p], kcol]) for p in range(n_parts)]
        mx = scores[0].max(axis=-1, keepdims=True)
        for s in scores[1:]:
            mx = jnp.maximum(mx, s.max(axis=-1, keepdims=True))
        out = None
        for p, s in enumerate(scores):
            e = jnp.exp((s - mx).astype(BF16))
            pv = _dot(e, kv_refs[2 * p + 1][keys[p], kcol])
            out = pv if out is None else out + pv
        acc_ref[rows, hd * MLA_V:(hd + 1) * MLA_V] = (out[:, :MLA_V] / out[:, LANE:LANE + MLA_V]).astype(BF16)
    y = _dot(acc_ref[...], wo_ref[...])
    o_ref[...] = x_ref[...] + m_ref[2] * y


def _attention(x, x_row0, n, dest, mods, layer, seg, q, kv_parts, w_o):
    d = x.shape[1]
    hq = MLA_HEADS * 2 * LANE
    hv = MLA_HEADS * MLA_V
    tq = seg.seq_tile(ATTN_ROWS)
    tps = seg.seq // tq
    n_seq = ATTN_SHORT_SEQS if (tps == 1 and not seg.per_batch_mod and seg.batch % ATTN_SHORT_SEQS == 0) else 1
    tq *= n_seq
    x_tile0, out_tile0 = x_row0 // tq, seg.row0 // tq
    in_specs = [pl.BlockSpec((tq, hq), lambda b, i: (b * tps + i, 0))]
    args = [q]
    for k, v, rows in kv_parts:
        mode = dict(pipeline_mode=pl.Buffered(1)) if tps > 1 else {}
        in_specs += [pl.BlockSpec((n_seq * rows, hq), lambda b, i: (b, 0), **mode)] * 2
        args += [k, v]
    in_specs += [
        pl.BlockSpec(w_o.shape, lambda b, i: (0, 0)),
        pl.BlockSpec((tq, d), lambda b, i: (x_tile0 + b * tps + i, 0)),
        _mod_spec(layer, seg, d, tq // n_seq, tile_of=lambda b, i: (b * tps + i) * n_seq),
    ]
    args += [w_o, x, mods]
    if isinstance(dest, str):
        assert dest == "inplace"
        aliases = {len(args) - 2: 0}
    else:
        in_specs.append(pl.BlockSpec(memory_space=pl.ANY))
        args.append(dest)
        aliases = {len(args) - 1: 0}
    return pl.pallas_call(
        functools.partial(_attn_kernel, n_parts=len(kv_parts), n_seq=n_seq),
        grid=(seg.batch // n_seq, tps),
        in_specs=in_specs,
        out_specs=pl.BlockSpec((tq, d), lambda b, i: (out_tile0 + b * tps + i, 0)),
        out_shape=jax.ShapeDtypeStruct((n, d), F32),
        scratch_shapes=[pltpu.VMEM((tq, hv), BF16)],
        input_output_aliases=aliases,
        compiler_params=_cparams("parallel", "arbitrary"),
        name="mla_attention",
    )(*args)


RET_COL = 1024


RET_ROWS = 1024


def _ret_proj_kernel(*refs, kinds, dk, pending):
    refs, resolve = _take_pending(refs, pending)
    rotary = "rotary" in kinds
    if rotary:
        x_ref, m_ref, g_ref, w_ref, cos_ref, sin_ref, z_ref = refs
        cos, sin = cos_ref[...], sin_ref[...]
    else:
        x_ref, m_ref, g_ref, w_ref, z_ref = refs
    h = _modulate(resolve(x_ref[...]), g_ref[...], m_ref[0], m_ref[1]).astype(BF16)
    half = dk // 2
    for j, kind in enumerate(kinds):
        c0 = j * RET_COL
        acc = _dot(h, w_ref[:, c0:c0 + RET_COL])
        if kind == "rotary":
            for hd in range(RET_COL // dk):
                lo = hd * dk
                x1, x2 = acc[:, lo:lo + half], acc[:, lo + half:lo + dk]
                z_ref[:, c0 + lo:c0 + lo + half] = (x1 * cos - x2 * sin).astype(BF16)
                z_ref[:, c0 + lo + half:c0 + lo + dk] = (x1 * sin + x2 * cos).astype(BF16)
        elif kind == "silu":
            z_ref[:, c0:c0 + RET_COL] = (acc * _sigmoid(acc)).astype(BF16)
        else:
            z_ref[:, c0:c0 + RET_COL] = acc.astype(BF16)


def _ret_proj(x, mods, layer, seg, g1, w_in, group, kinds, rot_tabs, dk, pending=None):
    n, d = x.shape
    rows = seg.rows
    tm = seg.tile(RET_ROWS)
    ncol = len(kinds) * RET_COL
    tps = max(seg.seq // tm, 1)
    in_specs = [
        pl.BlockSpec((tm, d), lambda i: (seg.row0 // tm + i, 0)),
        _mod_spec(layer, seg, d, tm),
        pl.BlockSpec((1, d), lambda i: (0, 0)),
        pl.BlockSpec((d, ncol), lambda i: (0, group), pipeline_mode=pl.Buffered(1)),
    ]
    args = [x, mods, g1, w_in]
    if "rotary" in kinds:
        tab = pl.BlockSpec((tm, dk // 2), lambda i: (i % tps, 0))
        in_specs += [tab, tab]
        args += list(rot_tabs)
    out_specs = [pl.BlockSpec((tm, ncol), lambda i: (i, 0))]
    out_shape = [jax.ShapeDtypeStruct((rows, ncol), BF16)]
    aliases = {}
    if pending is not None:
        p_specs, p_args, xo_spec, xo_shape = _pending_io(pending, seg, tm, d, n)
        in_specs, args = p_specs + in_specs, p_args + args
        out_specs.append(xo_spec)
        out_shape.append(xo_shape)
        aliases = {len(p_args): 1}
    res = pl.pallas_call(
        functools.partial(_ret_proj_kernel, kinds=kinds, dk=dk, pending=pending is not None),
        grid=(rows // tm,),
        in_specs=in_specs,
        out_specs=out_specs,
        out_shape=out_shape,
        input_output_aliases=aliases,
        compiler_params=_cparams("parallel"),
        name="ret_proj_" + kinds[0],
    )(*args)
    return res if pending is not None else res[0]


def _log_sigmoid(x):
    return jnp.minimum(x, 0.0) - jnp.log(1.0 + jnp.exp(-jnp.abs(x)))


def _ret_scan_kernel(*refs, has_s0, emit_state, n_chunks, heads, fuse_out):
    refs = list(refs)
    lf_ref, lb_ref, q_ref, k_ref, v_ref, gf_ref, gb_ref = refs[:7]
    pos = 7
    s0_ref = None
    if has_s0:
        s0_ref = refs[pos]
        pos += 1
    if fuse_out:
        wo_ref, x_ref, m_ref, o_ref = refs[pos:pos + 4]
        pos += 4
    else:
        y_ref = refs[pos]
        pos += 1
    sout_ref = None
    if emit_state:
        sout_ref = refs[pos]
        pos += 1
    if fuse_out:
        s_ref, yf_ref, y_ref = refs[pos:]
    else:
        s_ref, yf_ref = refs[pos:]
    c = RET_CHUNK
    dk, dv = s_ref.shape
    ii = lax.broadcasted_iota(jnp.int32, (c, c), 0).astype(F32)
    jj = lax.broadcasted_iota(jnp.int32, (c, c), 1).astype(F32)
    idx = lax.broadcasted_iota(jnp.int32, (c, 1), 0).astype(F32)

    for hd, direction in [(hd, direction) for hd in range(heads) for direction in range(2)]:
        fwd = direction == 0
        kcol, vcol = slice(hd * dk, (hd + 1) * dk), slice(hd * dv, (hd + 1) * dv)
        lg = _log_sigmoid((lf_ref if fwd else lb_ref)[hd])
        rel = (ii - jj) if fwd else (jj - ii)
        keep = rel >= 0
        decay_in = jnp.where(keep, jnp.exp(jnp.where(keep, rel, 0.0) * lg), 0.0)
        decay_q = jnp.exp(((idx + 1.0) if fwd else (c - idx)) * lg)
        decay_k = jnp.exp(((c - 1.0 - idx) if fwd else idx) * lg)
        decay_c = jnp.exp(c * lg)
        g_ref = gf_ref if fwd else gb_ref

        def chunk(cc, state, fwd=fwd, decay_in=decay_in, decay_q=decay_q, decay_k=decay_k, decay_c=decay_c,
                  g_ref=g_ref, kcol=kcol, vcol=vcol):
            r0 = cc * c if isinstance(cc, int) else pl.multiple_of(cc * c, c)
            qc = q_ref[pl.ds(r0, c), kcol]
            kc = k_ref[pl.ds(r0, c), kcol]
            vc = v_ref[pl.ds(r0, c), vcol]
            sc = _dot_nt(qc, kc) * decay_in
            out = _dot(sc.astype(BF16), vc)
            kd_t = (kc.astype(F32) * decay_k).T.astype(BF16)
            new_s = _dot(kd_t, vc)
            if state is not None:
                out = out + decay_q * _dot(qc, state.astype(BF16))
                new_s = decay_c * state + new_s
            s_ref[...] = new_s
            o = _rms(out) * g_ref[pl.ds(r0, c), vcol].astype(F32)
            if fwd:
                yf_ref[pl.ds(r0, c), :] = o
            else:
                y_ref[pl.ds(r0, c), vcol] = (yf_ref[pl.ds(r0, c), :] + o).astype(BF16)

        chunk(0 if fwd else n_chunks - 1, s0_ref[direction, hd] if has_s0 else None)

        def step(ci, carry, fwd=fwd, chunk=chunk):
            chunk(ci if fwd else n_chunks - 1 - ci, s_ref[...])
            return carry

        lax.fori_loop(1, n_chunks, step, 0, unroll=True)
        if emit_state:
            sout_ref[direction, hd] = s_ref[...]
    if fuse_out:
        o_ref[...] = x_ref[...] + m_ref[2] * _dot(y_ref[...], wo_ref[...])


def _ret_scan(qkv, g, seg, logit_f, logit_b, s0, emit_state, dk, dv, heads, out_proj=None):
    rows = seg.batch * seg.seq
    t = seg.seq
    hh = RET_HEADS
    groups = hh // heads
    fuse_out = out_proj is not None
    assert not fuse_out or groups == 1
    v0 = 2 * hh * dk // (heads * dv)
    assert groups * heads == hh and v0 * heads * dv == 2 * hh * dk
    in_specs = [
        pl.BlockSpec((heads, 1, 1), lambda b, h: (h, 0, 0)),
        pl.BlockSpec((heads, 1, 1), lambda b, h: (h, 0, 0)),
        pl.BlockSpec((t, heads * dk), lambda b, h: (b, h)),
        pl.BlockSpec((t, heads * dk), lambda b, h: (b, groups + h)),
        pl.BlockSpec((t, heads * dv), lambda b, h: (b, v0 + h)),
        pl.BlockSpec((t, heads * dv), lambda b, h: (b, h)),
        pl.BlockSpec((t, heads * dv), lambda b, h: (b, groups + h)),
    ]
    args = [logit_f.reshape(hh, 1, 1), logit_b.reshape(hh, 1, 1), qkv, qkv, qkv, g, g]
    state_spec = pl.BlockSpec((None, 2, heads, dk, dv), lambda b, h: (b, 0, h, 0, 0))
    if s0 is not None:
        in_specs.append(state_spec)
        args.append(s0)
    scratch = [pltpu.VMEM((dk, dv), F32), pltpu.VMEM((t, dv), F32)]
    aliases = {}
    if fuse_out:
        x, mods, layer, w_o = out_proj
        n, d = x.shape
        x_spec = pl.BlockSpec((t, d), lambda b, h: (seg.row0 // t + b, 0))
        in_specs += [pl.BlockSpec(w_o.shape, lambda b, h: (0, 0)), x_spec,
                     _mod_spec(layer, seg, d, t, tile_of=lambda b, h: b)]
        args += [w_o, x, mods]
        aliases = {len(args) - 2: 0}
        out_specs, out_shape = [x_spec], [jax.ShapeDtypeStruct((n, d), F32)]
        scratch.append(pltpu.VMEM((t, hh * dv), BF16))
    else:
        out_specs = [pl.BlockSpec((t, heads * dv), lambda b, h: (b, h))]
        out_shape = [jax.ShapeDtypeStruct((rows, hh * dv), BF16)]
    if emit_state:
        out_specs.append(state_spec)
        out_shape.append(jax.ShapeDtypeStruct((seg.batch, 2, hh, dk, dv), F32))
    res = pl.pallas_call(
        functools.partial(_ret_scan_kernel, has_s0=s0 is not None, emit_state=emit_state,
                          n_chunks=t // RET_CHUNK, heads=heads, fuse_out=fuse_out),
        grid=(seg.batch, groups),
        in_specs=in_specs,
        out_specs=out_specs,
        out_shape=out_shape,
        scratch_shapes=scratch,
        input_output_aliases=aliases,
        compiler_params=_cparams("parallel", "parallel"),
        name="ret_scan",
    )(*args)
    return res if emit_state else (res[0], None)


def _mm_res_kernel(a_ref, w_ref, x_ref, m_ref, o_ref):
    o_ref[...] = x_ref[...] + m_ref[2] * _dot(a_ref[...], w_ref[...])


def _matmul_residual(x, mods, layer, seg, a, w):
    n, d = x.shape
    tm = seg.tile(RET_ROWS)
    x_spec = pl.BlockSpec((tm, d), lambda i: (seg.row0 // tm + i, 0))
    return pl.pallas_call(
        _mm_res_kernel,
        grid=(seg.rows // tm,),
        in_specs=[
            pl.BlockSpec((tm, a.shape[1]), lambda i: (i, 0)),
            pl.BlockSpec(w.shape, lambda i: (0, 0)),
            x_spec,
            _mod_spec(layer, seg, d, tm),
        ],
        out_specs=x_spec,
        out_shape=jax.ShapeDtypeStruct((n, d), F32),
        input_output_aliases={2: 0},
        compiler_params=_cparams("parallel"),
        name="matmul_residual",
    )(a, w, x, mods)


def _fnet_a_kernel(*refs, gd, pending):
    refs, resolve = _take_pending(refs, pending)
    x_ref, m_ref, g_ref, cs_ref, ac_ref, as_ref = refs
    h = _modulate(resolve(x_ref[...]), g_ref[...], m_ref[0], m_ref[1]).astype(BF16)
    cs = cs_ref[...]
    for g in range(FNET_GROUPS):
        a = _dot(h[:, g * gd:(g + 1) * gd], cs)
        ac_ref[:, g * gd:(g + 1) * gd] = a[:, :gd].astype(BF16)
        as_ref[:, g * gd:(g + 1) * gd] = a[:, gd:].astype(BF16)


def _fnet_a(x, mods, layer, seg, g1, cs, pending=None):
    n, d = x.shape
    rows = seg.rows
    tm = seg.tile(PROJ_ROWS)
    out = pl.BlockSpec((tm, d), lambda i: (i, 0))
    in_specs = [
        pl.BlockSpec((tm, d), lambda i: (seg.row0 // tm + i, 0)),
        _mod_spec(layer, seg, d, tm),
        pl.BlockSpec((1, d), lambda i: (0, 0)),
        pl.BlockSpec(cs.shape, lambda i: (0, 0)),
    ]
    args = [x, mods, g1, cs]
    out_specs, out_shape, aliases = [out, out], [jax.ShapeDtypeStruct((rows, d), BF16)] * 2, {}
    if pending is not None:
        p_specs, p_args, xo_spec, xo_shape = _pending_io(pending, seg, tm, d, n)
        in_specs, args = p_specs + in_specs, p_args + args
        out_specs.append(xo_spec)
        out_shape.append(xo_shape)
        aliases = {len(p_args): 2}
    return pl.pallas_call(
        functools.partial(_fnet_a_kernel, gd=d // FNET_GROUPS, pending=pending is not None),
        grid=(rows // tm,),
        in_specs=in_specs,
        out_specs=out_specs,
        out_shape=out_shape,
        input_output_aliases=aliases,
        compiler_params=_cparams("parallel"),
        name="fnet_channel_dft",
    )(*args)


def _fnet_b_kernel(ct_ref, st_ref, ac_ref, as_ref, w_ref, x_ref, m_ref, o_ref, *, norm, n_seq):
    t = ac_ref.shape[0] // n_seq
    ct, st = ct_ref[...], st_ref[...]
    f = [_dot(ct, ac_ref[s * t:(s + 1) * t, :]) - _dot(st, as_ref[s * t:(s + 1) * t, :]) for s in range(n_seq)]
    f = (f[0] if n_seq == 1 else jnp.concatenate(f, axis=0)) * norm
    o_ref[...] = x_ref[...] + m_ref[2] * _dot(f.astype(BF16), w_ref[...])


def _fnet_b(x, mods, layer, seg, ac, as_, ct, st, w, norm):
    n, d = x.shape
    t = seg.seq
    tq = seg.seq_tile(FNET_ROWS)
    tps = t // tq
    n_seq = ATTN_SHORT_SEQS if (tps == 1 and not seg.per_batch_mod and seg.batch % ATTN_SHORT_SEQS == 0) else 1
    rows = n_seq * tq
    x_spec = pl.BlockSpec((rows, d), lambda b, i: (seg.row0 // rows + b * tps + i, 0))
    tab = pl.BlockSpec((tq, t), lambda b, i: (i, 0))
    seq = pl.BlockSpec((n_seq * t, d), lambda b, i: (b, 0))
    return pl.pallas_call(
        functools.partial(_fnet_b_kernel, norm=norm, n_seq=n_seq),
        grid=(seg.batch // n_seq, tps),
        in_specs=[tab, tab, seq, seq, pl.BlockSpec(w.shape, lambda b, i: (0, 0)), x_spec,
                  _mod_spec(layer, seg, d, tq, tile_of=lambda b, i: (b * tps + i) * n_seq)],
        out_specs=x_spec,
        out_shape=jax.ShapeDtypeStruct((n, d), F32),
        input_output_aliases={5: 0},
        compiler_params=_cparams("parallel", "arbitrary"),
        name="fnet_position_dft",
    )(ct, st, ac, as_, w, x, mods)


def _pack_halves(a):
    w = a.shape[1] // 2
    bits = lambda v: lax.bitcast_convert_type(v.astype(BF16).astype(F32), jnp.uint32)
    return (bits(a[:, :w]) >> 16) | (bits(a[:, w:]) & jnp.uint32(0xFFFF0000))


def _router_kernel(x_ref, m_ref, g_ref, rwhi_ref, rwlo_ref, rb_ref, h_ref, idx_ref, rank_ref, wcol_ref, cnt_ref,
                   run_ref, tri_ref):
    step = pl.program_id(0)

    @pl.when(step == 0)
    def _():
        run_ref[...] = jnp.zeros_like(run_ref)
        tt = tri_ref.shape[0]
        earlier = lax.broadcasted_iota(jnp.int32, (tt, tt), 0) < lax.broadcasted_iota(jnp.int32, (tt, tt), 1)
        tri_ref[...] = jnp.where(earlier, 1.0, 0.0).astype(BF16)

    parts = []
    for r0 in range(0, x_ref.shape[0], ROUTER_SUB_ROWS):
        rows = slice(r0, r0 + ROUTER_SUB_ROWS)
        h = _modulate(x_ref[rows, :], g_ref[...], m_ref[3], m_ref[4])
        h_ref[rows, :] = _pack_halves(h)
        h_hi = h.astype(BF16)
        h_lo = (h - h_hi.astype(F32)).astype(BF16)
        parts.append(_dot_nt(rwhi_ref[...], h_hi) + (_dot_nt(rwhi_ref[...], h_lo) + _dot_nt(rwlo_ref[...], h_hi)))
    logits = jnp.concatenate(parts, axis=1)
    sc = _sigmoid(logits)
    gr = sc + rb_ref[...]
    gp = EXPERTS_PER_GROUP
    row = lambda a, e: a[e:e + 1, :]
    best_g = None
    for g in range(N_EXPERT_GROUPS):
        vals = [row(gr, g * gp + i) for i in range(gp)]
        gs = None
        for i in range(gp):
            for j in range(i + 1, gp):
                pair = vals[i] + vals[j]
                gs = pair if gs is None else jnp.maximum(gs, pair)
        if best_g is None:
            best_g, best_v = jnp.zeros(gs.shape, jnp.int32), gs
        else:
            better = gs > best_v
            best_g = jnp.where(better, g, best_g)
            best_v = jnp.where(better, gs, best_v)
    sel, raw = [], []
    for i in range(gp):
        s_i, r_i = row(gr, i), row(sc, i)
        for g in range(1, N_EXPERT_GROUPS):
            s_i = jnp.where(best_g == g, row(gr, g * gp + i), s_i)
            r_i = jnp.where(best_g == g, row(sc, g * gp + i), r_i)
        sel.append(s_i)
        raw.append(r_i)

    def argmax_first(vals, raws):
        bi, bv, br = jnp.zeros(vals[0].shape, jnp.int32), vals[0], raws[0]
        for i in range(1, len(vals)):
            better = vals[i] > bv
            bi = jnp.where(better, i, bi)
            bv = jnp.where(better, vals[i], bv)
            br = jnp.where(better, raws[i], br)
        return bi, br

    i1, w1 = argmax_first(sel, raw)
    masked = [jnp.where(i1 == i, -jnp.inf, sel[i]) for i in range(gp)]
    i2, w2 = argmax_first(masked, raw)
    tot = w1 + w2
    e1 = best_g * gp + i1
    e2 = best_g * gp + i2
    idx_ref[0:1, :] = e1
    idx_ref[1:2, :] = e2
    t = e1.shape[1]
    sub = lax.broadcasted_iota(jnp.int32, (8, t), 0)
    w8 = jnp.where(sub == 0, w1 / tot, jnp.where(sub == 1, w2 / tot, 0.0))
    wcol_ref[...] = w8.T
    eio = lax.broadcasted_iota(jnp.int32, (N_EXPERTS, t), 0)
    oh1, oh2 = eio == e1, eio == e2
    oh = jnp.where(oh1, 1.0, jnp.where(oh2, 1.0, 0.0))
    local = _dot(oh.astype(BF16), tri_ref[...])
    rank = local + run_ref[:, 0:1]
    rank_ref[0:1, :] = jnp.sum(jnp.where(oh1, rank, 0.0), axis=0, keepdims=True).astype(jnp.int32)
    rank_ref[1:2, :] = jnp.sum(jnp.where(oh2, rank, 0.0), axis=0, keepdims=True).astype(jnp.int32)
    run_ref[...] = run_ref[...] + jnp.sum(oh, axis=1, keepdims=True)
    cnt_ref[...] = run_ref[...]


def _wide_mod_row(segs, tm):
    ctx, lat = segs
    ctx_tiles = ctx.batch * ctx.seq // tm
    assert ctx_tiles * tm == ctx.batch * ctx.seq and lat.seq % tm == 0
    return lambda i: jnp.where(i < ctx_tiles, ctx.mod0, lat.mod0 + (i - ctx_tiles) // (lat.seq // tm))


def _router(x, mods, layer, segs, g2, rw_hi, rw_lo, rb):
    n, d = x.shape
    tm = WIDE_TILE
    mod_row = _wide_mod_row(segs, tm)
    return pl.pallas_call(
        _router_kernel,
        grid=(n // tm,),
        in_specs=[
            pl.BlockSpec((tm, d), lambda i: (i, 0)),
            pl.BlockSpec((None, None, 6, 1, d), lambda i: (layer, mod_row(i), 0, 0, 0)),
            pl.BlockSpec((1, d), lambda i: (0, 0)),
            pl.BlockSpec(rw_hi.shape, lambda i: (0, 0)),
            pl.BlockSpec(rw_lo.shape, lambda i: (0, 0)),
            pl.BlockSpec(rb.shape, lambda i: (0, 0)),
        ],
        out_specs=[
            pl.BlockSpec((tm, d // 2), lambda i: (i, 0)),
            pl.BlockSpec((2, tm), lambda i: (0, i)),
            pl.BlockSpec((2, tm), lambda i: (0, i)),
            pl.BlockSpec((tm, 8), lambda i: (i, 0)),
            pl.BlockSpec((N_EXPERTS, LANE), lambda i: (0, 0)),
        ],
        out_shape=[
            jax.ShapeDtypeStruct((n, d // 2), jnp.uint32),
            jax.ShapeDtypeStruct((2, n), jnp.int32),
            jax.ShapeDtypeStruct((2, n), jnp.int32),
            jax.ShapeDtypeStruct((n, 8), F32),
            jax.ShapeDtypeStruct((N_EXPERTS, LANE), F32),
        ],
        scratch_shapes=[pltpu.VMEM((N_EXPERTS, LANE), F32), pltpu.VMEM((tm, tm), BF16)],
        compiler_params=_cparams("arbitrary"),
        name="moe_router",
    )(x, mods, g2, rw_hi, rw_lo, rb)


def _expert_kernel(be_ref, br_ref, bs_ref, xs_ref, wg_ref, wu_ref, wd_ref, y_ref, wg_b, wu_b, wd_b):
    i = pl.program_id(0)
    prev = be_ref[jnp.maximum(i - 1, 0)]
    valid = br_ref[i]
    tail_rows = MOE_TAIL_ROWS

    @pl.when(jnp.logical_or(i == 0, be_ref[i] != prev))
    def _():
        wg_b[...] = wg_ref[...].astype(BF16)
        wu_b[...] = wu_ref[...].astype(BF16)
        wd_b[...] = wd_ref[...].astype(BF16)

    def ffn(r0, nrows):
        lo, hi = _unpack_halves(xs_ref[r0:r0 + nrows, :])
        xb = jnp.concatenate([lo.astype(BF16), hi.astype(BF16)], axis=1)
        gate = _dot(xb, wg_b[...])
        hid = (gate * _sigmoid(gate)) * _dot(xb, wu_b[...])
        y_ref[r0:r0 + nrows, :] = _pack_halves(_dot(hid.astype(BF16), wd_b[...]))

    @pl.when(valid > tail_rows)
    def _():
        ffn(0, y_ref.shape[0])

    @pl.when(jnp.logical_and(valid > 0, valid <= tail_rows))
    def _():
        ffn(0, tail_rows)
        y_ref[tail_rows:, :] = jnp.zeros((y_ref.shape[0] - tail_rows, y_ref.shape[1]), y_ref.dtype)


def _experts(xs, block_e, block_rows, block_src, w_gate, w_up, w_down, layer):
    rows, half = xs.shape
    d = 2 * half
    de = w_gate.shape[-1]
    n_blocks = rows // MOE_ROWS
    grid_spec = pltpu.PrefetchScalarGridSpec(
        num_scalar_prefetch=3,
        grid=(n_blocks,),
        in_specs=[
            pl.BlockSpec((MOE_ROWS, half), lambda i, be, br, bs: (bs[i], 0)),
            pl.BlockSpec((None, None, d, de), lambda i, be, br, bs: (layer, be[i], 0, 0)),
            pl.BlockSpec((None, None, d, de), lambda i, be, br, bs: (layer, be[i], 0, 0)),
            pl.BlockSpec((None, None, de, d), lambda i, be, br, bs: (layer, be[i], 0, 0)),
        ],
        out_specs=pl.BlockSpec((MOE_ROWS, half), lambda i, be, br, bs: (bs[i], 0)),
        scratch_shapes=[pltpu.VMEM((d, de), BF16), pltpu.VMEM((d, de), BF16), pltpu.VMEM((de, d), BF16)],
    )
    return pl.pallas_call(
        _expert_kernel,
        grid_spec=grid_spec,
        out_shape=jax.ShapeDtypeStruct((rows, half), jnp.uint32),
        compiler_params=_cparams("arbitrary"),
        name="moe_experts",
    )(block_e, block_rows, block_src, xs, w_gate, w_up, w_down)


def _combine_kernel(x_ref, m_ref, y_ref, w_ref, fg_ref, o_ref):
    x = x_ref[...] + _moe_residual(y_ref, w_ref, m_ref[5])
    o_ref[...] = _rms(x) * fg_ref[...]


def _combine(x, mods, layer, segs, seg, yg, wcol, final_g):
    n, d = x.shape
    tm = WIDE_TILE
    mod_row = _wide_mod_row(segs, tm)
    t0 = seg.row0 // tm
    steps = seg.rows // tm
    in_specs = [pl.BlockSpec((tm, d), lambda i: (t0 + i, 0)),
                pl.BlockSpec((None, None, 6, 1, d), lambda i: (layer, mod_row(t0 + i), 0, 0, 0)),
                pl.BlockSpec((2, tm, d // 2), lambda i: (0, i, 0)),
                pl.BlockSpec((tm, 8), lambda i: (t0 + i, 0)),
                pl.BlockSpec((1, d), lambda i: (0, 0))]
    return pl.pallas_call(
        _combine_kernel,
        grid=(steps,),
        in_specs=in_specs,
        out_specs=pl.BlockSpec((tm, d), lambda i: (i, 0)),
        out_shape=jax.ShapeDtypeStruct((seg.rows, d), F32),
        compiler_params=_cparams("parallel"),
        name="moe_combine_final",
    )(x, mods, yg, wcol, final_g)


def _sc_mesh():
    return plsc.VectorSubcoreMesh(core_axis_name="c", subcore_axis_name="s")


def _sc_worker_split(n):
    workers = SC_CORES * SC_SUBCORES
    per = n // workers
    assert per * workers == n and per % SC_CHUNK == 0
    return workers, per, per // SC_CHUNK


def _sc_dispatch(h, pos, rows):
    n, w = h.shape
    workers, per, chunks = _sc_worker_split(n)

    @functools.partial(
        pl.kernel, out_type=jax.ShapeDtypeStruct((rows, w), h.dtype), mesh=_sc_mesh(),
        scratch_types=[pltpu.VMEM((2, chunks, SC_CHUNK), jnp.int32), pltpu.VMEM((SC_CHUNK, w), h.dtype)],
        name="moe_dispatch_scatter")
    def scatter_rows(h_hbm, pos_hbm, xs_hbm, idx_v, rows_v):
        wid = lax.axis_index("s") * SC_CORES + lax.axis_index("c")
        pltpu.sync_copy(pos_hbm.at[0, wid], idx_v.at[0])
        pltpu.sync_copy(pos_hbm.at[1, wid], idx_v.at[1])

        @pl.loop(0, chunks)
        def _(c):
            pltpu.sync_copy(h_hbm.at[pl.ds(wid * per + c * SC_CHUNK, SC_CHUNK)], rows_v)
            pltpu.sync_copy(rows_v, xs_hbm.at[idx_v.at[0, c]])
            pltpu.sync_copy(rows_v, xs_hbm.at[idx_v.at[1, c]])

    return scatter_rows(h, pos.reshape(2, workers, chunks, SC_CHUNK))


def _sc_gather2(ys, pos):
    _, w = ys.shape
    n = pos.shape[1]
    workers, per, chunks = _sc_worker_split(n)

    @functools.partial(
        pl.kernel, out_type=jax.ShapeDtypeStruct((2, n, w), ys.dtype), mesh=_sc_mesh(),
        scratch_types=[pltpu.VMEM((2, chunks, SC_CHUNK), jnp.int32), pltpu.VMEM((SC_CHUNK, w), ys.dtype),
                       pltpu.SemaphoreType.DMA],
        name="moe_combine_gather")
    def gather_rows(ys_hbm, pos_hbm, out_hbm, idx_v, rows_v, sem):
        wid = lax.axis_index("s") * SC_CORES + lax.axis_index("c")
        pltpu.sync_copy(pos_hbm.at[0, wid], idx_v.at[0])
        pltpu.sync_copy(pos_hbm.at[1, wid], idx_v.at[1])

        @pl.loop(0, chunks)
        def _(c):
            for k in range(2):
                pltpu.async_copy(ys_hbm.at[idx_v.at[k, c]], rows_v, sem).wait()
                pltpu.sync_copy(rows_v, out_hbm.at[k, pl.ds(wid * per + c * SC_CHUNK, SC_CHUNK)])

    return gather_rows(ys, pos.reshape(2, workers, chunks, SC_CHUNK))


def _dispatch_plan(idx, rank, counts):
    n = idx.shape[1]
    padded = (counts + MOE_ROWS - 1) // MOE_ROWS * MOE_ROWS
    pad_end = jnp.cumsum(padded)
    pad_start = pad_end - padded
    experts = jnp.arange(N_EXPERTS, dtype=jnp.int32)
    start_of = jnp.sum(jnp.where(idx[..., None] == experts, pad_start, 0), axis=-1)
    pos = start_of + rank
    n_blocks = 2 * n // MOE_ROWS + N_EXPERTS
    steps = jnp.arange(n_blocks, dtype=jnp.int32)
    last_used = pad_end[-1] // MOE_ROWS - 1
    step = jnp.minimum(steps, last_used)
    block_e = jnp.minimum(jnp.sum(step[:, None] * MOE_ROWS >= pad_end[None, :], axis=1), N_EXPERTS - 1)
    pick = lambda per_expert: jnp.sum(jnp.where(block_e[:, None] == experts, per_expert, 0), axis=-1)
    first, count = pick(pad_start // MOE_ROWS), jnp.maximum(pick(padded // MOE_ROWS), 1)
    block_src = first + (step - first - 1) % count
    block_rows = jnp.clip(pick(pad_start + counts) - block_src * MOE_ROWS, 0, MOE_ROWS)
    block_rows = jnp.where(steps <= last_used, block_rows, 0)
    return (pos, block_e.astype(jnp.int32), block_rows.astype(jnp.int32), block_src.astype(jnp.int32),
            n_blocks * MOE_ROWS)


def _moe(x, mods, layer, segs, g2, rw_hi, rw_lo, rb, w_gate, w_up, w_down, final_g):
    h2p, idx, rank, wcol, cnt = _router(x, mods, layer, segs, g2, rw_hi, rw_lo, rb)
    pos, block_e, block_rows, block_src, rows = _dispatch_plan(idx, rank, cnt[:, 0].astype(jnp.int32))
    xs = _sc_dispatch(h2p, pos, rows)
    ys = _experts(xs, block_e, block_rows, block_src, w_gate, w_up, w_down, layer)
    yg = {seg: _sc_gather2(ys, pos[:, seg.row0:seg.row0 + seg.rows]) for seg in segs}
    if final_g is None:
        return yg, wcol, mods, layer
    return tuple(_combine(x, mods, layer, segs, seg, yg[seg], wcol, final_g) for seg in segs)


def _mla_rope_tables(t):
    axis_dim = MLA_ROPE // 2
    row = np.repeat(np.arange(t // GRID_W), GRID_W).astype(np.float64)
    col = np.tile(np.arange(GRID_W), t // GRID_W).astype(np.float64)
    inv = ROPE_BASE ** (-np.arange(0, axis_dim, 2, dtype=np.float64) / axis_dim)
    ar, ac = row[:, None] * inv[None, :], col[:, None] * inv[None, :]
    ones = np.ones((t, LANE - MLA_ROPE))
    cos = np.concatenate([np.cos(ar), np.cos(ar), np.cos(ac), np.cos(ac), ones], axis=-1)
    sin = np.concatenate([-np.sin(ar), np.sin(ar), -np.sin(ac), np.sin(ac), 0.0 * ones], axis=-1)
    return jnp.asarray(cos, F32), jnp.asarray(sin, F32)


def _ret_rot_tables(t, dk):
    inv = ROPE_BASE ** (-np.linspace(0.0, 1.0, dk // 2))
    ang = np.arange(t, dtype=np.float64)[:, None] * inv[None, :]
    return jnp.asarray(np.cos(ang), F32), jnp.asarray(np.sin(ang), F32)


def _dft_tables(n):
    k = np.arange(n, dtype=np.int64)
    ang = (np.outer(k, k) % n).astype(np.float64) * (2.0 * math.pi / n)
    return jnp.asarray(np.cos(ang), BF16), jnp.asarray(np.sin(ang), BF16)


def _mla_weights(w_in, q_g, kv_g, w_uq, w_ukv, w_o):
    d = w_in.shape[0]
    hd = MLA_NOPE + MLA_ROPE
    perm = _rope_perm()
    w_in_p = jnp.concatenate([w_in, w_in[:, MLA_Q_LORA + MLA_KV_LORA + perm]], axis=1)
    uq = w_uq.reshape(MLA_Q_LORA, MLA_HEADS, hd)
    uq = jnp.concatenate([uq, uq[..., MLA_NOPE + perm]], axis=-1)
    ukv = w_ukv.reshape(MLA_KV_LORA, MLA_HEADS, MLA_NOPE + MLA_V)
    return {
        "w_in": w_in_p.astype(BF16),
        "q_g": q_g.reshape(1, -1) * (MLA_NOPE + MLA_ROPE) ** -0.5,
        "kv_g": kv_g.reshape(1, -1),
        "w_q": uq.reshape(MLA_Q_LORA, MLA_HEADS * 2 * LANE).astype(BF16),
        "w_kn": ukv[..., :MLA_NOPE].reshape(MLA_KV_LORA, MLA_HEADS * MLA_NOPE).astype(BF16),
        "w_v": ukv[..., MLA_NOPE:].reshape(MLA_KV_LORA, MLA_HEADS * MLA_V).astype(BF16),
        "w_o": w_o.astype(BF16),
    }


def kernel(x_prompt, x_sample, cache_mla, state_ret, c, c_ctx, norm1_g, norm2_g, ada_w, ada_b, final_norm_g,
           mla_w_in, mla_q_norm_g, mla_kv_norm_g, mla_w_uq, mla_w_ukv, mla_w_o, ret_w_in, ret_decay_f,
           ret_decay_b, ret_w_o, fnet_w, router_w, router_b, moe_w_gate, moe_w_up, moe_w_down):
    b_ctx, t_ctx, d = x_prompt.shape
    b_lat, t_lat, _ = x_sample.shape
    depth = ada_w.shape[0]
    assert b_lat + 1 <= 8
    n_ctx = b_ctx * t_ctx
    ctx = _Seg(0, b_ctx, t_ctx, 0, False)
    lat = _Seg(n_ctx, b_lat, t_lat, 1, True)
    segs = (ctx, lat)

    n_lat = b_lat * t_lat
    n_mla = mla_w_in.shape[0]
    assert n_mla >= 1
    x = None
    new_cache = jnp.zeros((b_ctx, n_mla, t_ctx, MLA_KV_LORA + MLA_ROPE), F32)
    cond8 = jnp.concatenate([c_ctx[None, :], c, jnp.zeros((8 - 1 - b_lat, d), F32)], axis=0)
    mods = _modulation_all(cond8, ada_w, ada_b).reshape(depth, 8, 6, 1, d)

    rw_t = router_w.T.astype(F32)
    rw_hi = rw_t.astype(BF16)
    rw_lo = (rw_t - rw_hi.astype(F32)).astype(BF16)
    rb = router_b.reshape(N_EXPERTS, 1).astype(F32)
    final_g = final_norm_g.reshape(1, d)
    dk = ret_w_in.shape[2] // (8 * RET_HEADS)
    dv = 2 * dk

    states = []
    pending = None
    counters = [0, 0, 0]
    for layer in range(depth):
        kind = layer % 3
        j = counters[kind]
        counters[kind] += 1
        g1 = norm1_g[layer].reshape(1, d)
        g2 = norm2_g[layer].reshape(1, d)
        if kind == 0:
            w = _mla_weights(mla_w_in[j], mla_q_norm_g[j], mla_kv_norm_g[j], mla_w_uq[j], mla_w_ukv[j],
                             mla_w_o[j])
            if x is None:
                xc, xc0, xl, xl0 = x_prompt.reshape(n_ctx, d), 0, x_sample.reshape(n_lat, d), 0
            else:
                xc, xc0, xl, xl0 = x, ctx.row0, x, lat.row0
            if pending is None:
                qc, kc, vc, new_cache = _mla_proj(xc, xc0, mods, layer, ctx, g1, w, None, (new_cache, j, n_mla))
                ql, kl, vl = _mla_proj(xl, xl0, mods, layer, lat, g1, w, _mla_rope_tables(t_lat))
            else:
                qc, kc, vc, new_cache, x = _mla_proj(x, ctx.row0, mods, layer, ctx, g1, w, None,
                                                     (new_cache, j, n_mla), pending)
                ql, kl, vl, x = _mla_proj(x, lat.row0, mods, layer, lat, g1, w, _mla_rope_tables(t_lat), None,
                                          pending)
                xc = xl = x
            past = cache_mla.shape[2]
            cpad = jnp.pad(cache_mla[:, j].reshape(b_lat * past, -1), ((0, 0), (0, LANE - MLA_ROPE)))
            kp, vp = _cache_kv(cpad, w)
            first = x is None
            x = _attention(xc, xc0, n_ctx + n_lat, jnp.zeros((n_ctx + n_lat, d), F32) if first else "inplace",
                           mods, layer, ctx, qc, [(kc, vc, t_ctx)], w["w_o"])
            x = _attention(xl if first else x, xl0, n_ctx + n_lat, x if first else "inplace", mods, layer, lat,
                           ql, [(kl, vl, t_lat), (kp, vp, past)], w["w_o"])
        elif kind == 1:
            w_in = ret_w_in[j]
            qk = RET_HEADS * dk
            k_scale = jnp.concatenate([jnp.ones((qk,), F32), jnp.full((qk,), dk ** -0.5, F32),
                                       jnp.ones((w_in.shape[1] - 2 * qk,), F32)])
            w_in_b = (w_in * k_scale[None, :]).astype(BF16)
            w_o_b = ret_w_o[j].astype(BF16)
            rot = _ret_rot_tables(t_lat, dk)
            n_qk, n_v = 2 * qk // RET_COL, RET_HEADS * dv // RET_COL
            assert (n_qk + n_v) * RET_COL * 2 == w_in.shape[1]
            parts = []
            for seg in segs:
                kinds = (("rotary" if seg is lat else "plain"),) * n_qk + ("plain",) * n_v
                qkv = _ret_proj(x, mods, layer, seg, g1, w_in_b, 0, kinds, rot, dk, pending)
                if pending is not None:
                    qkv, x = qkv
                parts.append((qkv, _ret_proj(x, mods, layer, seg, g1, w_in_b, 1, ("silu",) * (n_qk + n_v), None, dk)))
            x, s_ctx = _ret_scan(*parts[0], ctx, ret_decay_f[j], ret_decay_b[j], None, True, dk, dv, RET_HEADS,
                                 out_proj=(x, mods, layer, w_o_b))
            yl, _ = _ret_scan(*parts[1], lat, ret_decay_f[j], ret_decay_b[j], state_ret[:, j], False, dk, dv, 1)
            x = _matmul_residual(x, mods, layer, lat, yl, w_o_b)
            states.append(s_ctx)
        else:
            gd = d // FNET_GROUPS
            cc, sc = _dft_tables(gd)
            cs = jnp.concatenate([cc, sc], axis=1)
            w_b = fnet_w[j].astype(BF16)
            for seg in segs:
                ct, st = _dft_tables(seg.seq)
                if pending is None:
                    ac, as_ = _fnet_a(x, mods, layer, seg, g1, cs)
                else:
                    ac, as_, x = _fnet_a(x, mods, layer, seg, g1, cs, pending)
                x = _fnet_b(x, mods, layer, seg, ac, as_, ct, st, w_b, (seg.seq * gd) ** -0.5)
        if layer < depth - 1:
            pending = _moe(x, mods, layer, segs, g2, rw_hi, rw_lo, rb, moe_w_gate, moe_w_up, moe_w_down, None)
        else:
            y_prompt, y_sample = _moe(x, mods, layer, segs, g2, rw_hi, rw_lo, rb, moe_w_gate, moe_w_up,
                                      moe_w_down, final_g)

    new_state = jnp.stack(states, axis=1)
    return (y_prompt.reshape(b_ctx, t_ctx, d), y_sample.reshape(b_lat, t_lat, d), new_cache, new_state)
```

```python
import functools
import math

import jax
import jax.numpy as jnp
import numpy as np
from jax import lax
from jax.experimental import pallas as pl
from jax.experimental.pallas import tpu as pltpu
from jax.experimental.pallas import tpu_sc as plsc

F32 = jnp.float32
BF16 = jnp.bfloat16

GRID_W = 64
MLA_HEADS = 8
MLA_NOPE = 128
MLA_ROPE = 64
MLA_V = 128
MLA_Q_LORA = 384
MLA_KV_LORA = 256
ROPE_BASE = 10000.0
RET_HEADS = 4
RET_CHUNK = 256
FNET_GROUPS = 4
N_EXPERTS = 16
N_EXPERT_GROUPS = 4
EXPERTS_PER_GROUP = 4
D_EXPERT = 512
NORM_EPS = 1e-6

LANE = 128
PROJ_ROWS = 512
ATTN_ROWS = 512
FNET_ROWS = 512
ATTN_SHORT_SEQS = 4
WIDE_TILE = 1024
ROUTER_SUB_ROWS = 256
MOE_ROWS = 1024
MOE_TAIL_ROWS = 256
VMEM_LIMIT = 56 * 1024 * 1024
SC_CORES = 2
SC_SUBCORES = 16
SC_CHUNK = 128


def _cparams(*sem):
    return pltpu.CompilerParams(dimension_semantics=sem, vmem_limit_bytes=VMEM_LIMIT)


def _sigmoid(x):
    return 1.0 / (1.0 + jnp.exp(-x))


def _rms(x):
    return x * lax.rsqrt(jnp.mean(x * x, axis=-1, keepdims=True) + NORM_EPS)


def _modulate(x, g, shift, scale):
    return _rms(x) * (g * (1.0 + scale)) + shift


def _dot(a, b):
    return jnp.dot(a, b, preferred_element_type=F32)


def _dot_nt(a, b):
    return lax.dot_general(a, b, (((1,), (1,)), ((), ())), preferred_element_type=F32)


def _mod_kernel(c_ref, w_ref, b_ref, o_ref):
    c = c_ref[...]
    s = (c * _sigmoid(c)).astype(BF16)
    o_ref[...] = _dot(s, w_ref[...].astype(BF16)) + b_ref[...]


def _modulation_all(cond8, ada_w, ada_b):
    depth, d, d6 = ada_w.shape
    tn = d6 // 4
    return pl.pallas_call(
        _mod_kernel,
        grid=(depth, d6 // tn),
        in_specs=[
            pl.BlockSpec((8, d), lambda l, n: (0, 0)),
            pl.BlockSpec((None, d, tn), lambda l, n: (l, 0, n)),
            pl.BlockSpec((None, 1, tn), lambda l, n: (l, 0, n)),
        ],
        out_specs=pl.BlockSpec((None, 8, tn), lambda l, n: (l, 0, n)),
        out_shape=jax.ShapeDtypeStruct((depth, 8, d6), F32),
        compiler_params=_cparams("parallel", "parallel"),
        name="modulation",
    )(cond8, ada_w, ada_b.reshape(depth, 1, d6))


class _Seg:
    def __init__(self, row0, batch, seq, mod0, per_batch_mod):
        self.row0, self.batch, self.seq = row0, batch, seq
        self.mod0, self.per_batch_mod = mod0, per_batch_mod
        self.rows = batch * seq

    def tile(self, want):
        tm = min(want, self.seq) if self.per_batch_mod else want
        assert self.rows % tm == 0 and self.row0 % tm == 0 and (self.seq % tm == 0 or tm % self.seq == 0)
        return tm

    def seq_tile(self, want):
        tm = min(want, self.seq)
        assert self.seq % tm == 0 and self.row0 % tm == 0
        return tm

    def mod_row(self, tile, tm):
        if self.per_batch_mod:
            return self.mod0 + tile * tm // self.seq
        return self.mod0


def _mod_spec(layer, seg, d, tm, tile_of=lambda *a: a[0]):
    return pl.BlockSpec((None, None, 6, 1, d), lambda *a: (layer, seg.mod_row(tile_of(*a), tm), 0, 0, 0))


def _unpack_halves(p):
    lo = lax.bitcast_convert_type(p << 16, F32)
    hi = lax.bitcast_convert_type(p & jnp.uint32(0xFFFF0000), F32)
    return lo, hi


def _moe_residual(y_ref, w_ref, gate):
    w = w_ref[...]
    lo0, hi0 = _unpack_halves(y_ref[0])
    lo1, hi1 = _unpack_halves(y_ref[1])
    w0, w1 = w[:, 0:1], w[:, 1:2]
    return gate * jnp.concatenate([w0 * lo0 + w1 * lo1, w0 * hi0 + w1 * hi1], axis=-1)


def _take_pending(refs, pending):
    if not pending:
        return refs, lambda x: x
    y_ref, w_ref, pm_ref, *rest = refs
    xo_ref = rest.pop()

    def resolve(x):
        x = x + _moe_residual(y_ref, w_ref, pm_ref[5])
        xo_ref[...] = x
        return x

    return rest, resolve


def _pending_io(pending, seg, tm, d, n):
    yg_by_seg, wcol, mods, layer = pending
    t0 = seg.row0 // tm
    specs = [pl.BlockSpec((2, tm, d // 2), lambda i: (0, i, 0)),
             pl.BlockSpec((tm, 8), lambda i: (t0 + i, 0)),
             _mod_spec(layer, seg, d, tm)]
    return (specs, [yg_by_seg[seg], wcol, mods], pl.BlockSpec((tm, d), lambda i: (t0 + i, 0)),
            jax.ShapeDtypeStruct((n, d), F32))


def _rope_partner(x):
    return pltpu.roll(x, LANE // 2, 1)


def _rope_perm():
    return np.array([l + 16 if l % 32 < 16 else l - 16 for l in range(MLA_ROPE)])


def _store_values(v_ref, v):
    ones = jnp.ones((v.shape[0], LANE), BF16)
    for hd in range(MLA_HEADS):
        v_ref[:, hd * 2 * LANE:hd * 2 * LANE + LANE] = v[:, hd * MLA_V:(hd + 1) * MLA_V].astype(BF16)
        v_ref[:, hd * 2 * LANE + LANE:(hd + 1) * 2 * LANE] = ones


def _mla_proj_kernel(*refs, rope, pending):
    refs, resolve = _take_pending(refs, pending)
    if rope:
        (x_ref, m_ref, g_ref, win_ref, qg_ref, kvg_ref, wq_ref, wkn_ref, wv_ref, cos_ref, sin_ref,
         q_ref, k_ref, v_ref) = refs
    else:
        x_ref, m_ref, g_ref, win_ref, qg_ref, kvg_ref, wq_ref, wkn_ref, wv_ref = refs[:9]
        q_ref, k_ref, v_ref, cache_ref = refs[-4:]
    h = _modulate(resolve(x_ref[...]), g_ref[...], m_ref[0], m_ref[1]).astype(BF16)
    z = _dot(h, win_ref[...])
    cq = z[:, :MLA_Q_LORA]
    ckv = z[:, MLA_Q_LORA:MLA_Q_LORA + MLA_KV_LORA]
    kpe = z[:, MLA_Q_LORA + MLA_KV_LORA:]
    cqn = (_rms(cq) * qg_ref[...]).astype(BF16)
    ckvn = _rms(ckv) * kvg_ref[...]
    ckvb = ckvn.astype(BF16)
    q = _dot(cqn, wq_ref[...])
    kn = _dot(ckvb, wkn_ref[...])
    _store_values(v_ref, _dot(ckvb, wv_ref[...]))
    if rope:
        cos, sin = cos_ref[...], sin_ref[...]
        kpe = kpe * cos + _rope_partner(kpe) * sin
    kpe = jnp.where(lax.broadcasted_iota(jnp.int32, kpe.shape, 1) < MLA_ROPE, kpe, 0.0)
    if not rope:
        seq = cache_ref.shape[1]
        for s in range(cache_ref.shape[0]):
            cache_ref[s, :, :MLA_KV_LORA] = ckvn[s * seq:(s + 1) * seq, :]
            cache_ref[s, :, MLA_KV_LORA:] = kpe[s * seq:(s + 1) * seq, :MLA_ROPE]
    kpe_b = kpe.astype(BF16)
    for hd in range(MLA_HEADS):
        lo = hd * 2 * LANE
        q_ref[:, lo:lo + LANE] = q[:, lo:lo + LANE].astype(BF16)
        qr = q[:, lo + LANE:lo + 2 * LANE]
        if rope:
            qr = qr * cos + _rope_partner(qr) * sin
        q_ref[:, lo + LANE:lo + 2 * LANE] = qr.astype(BF16)
        k_ref[:, lo:lo + LANE] = kn[:, hd * LANE:(hd + 1) * LANE].astype(BF16)
        k_ref[:, lo + LANE:lo + 2 * LANE] = kpe_b


def _mla_proj(x, x_row0, mods, layer, seg, g1, w, rope_tabs, cache_slot=None, pending=None):
    n, d = x.shape
    rope = rope_tabs is not None
    rows = seg.rows
    tm = seg.tile(PROJ_ROWS)
    x_tile0 = x_row0 // tm
    hq = MLA_HEADS * 2 * LANE
    const = lambda i: (0, 0)
    aliases = {}
    in_specs = [
        pl.BlockSpec((tm, d), lambda i: (x_tile0 + i, 0)),
        _mod_spec(layer, seg, d, tm),
        pl.BlockSpec((1, d), const),
        pl.BlockSpec(w["w_in"].shape, const),
        pl.BlockSpec((1, MLA_Q_LORA), const),
        pl.BlockSpec((1, MLA_KV_LORA), const),
        pl.BlockSpec(w["w_q"].shape, const),
        pl.BlockSpec(w["w_kn"].shape, const),
        pl.BlockSpec(w["w_v"].shape, const),
    ]
    args = [x, mods, g1, w["w_in"], w["q_g"], w["kv_g"], w["w_q"], w["w_kn"], w["w_v"]]
    out_specs = [pl.BlockSpec((tm, hq), lambda i: (i, 0))] * 3
    out_shape = [jax.ShapeDtypeStruct((rows, hq), BF16)] * 3
    if rope:
        tab = pl.BlockSpec((tm, LANE), lambda i: (i % (seg.seq // tm), 0))
        in_specs += [tab, tab]
        args += list(rope_tabs)
    else:
        cw = MLA_KV_LORA + MLA_ROPE
        prev, slot, n_slots = cache_slot
        assert tm % seg.seq == 0
        out_specs.append(pl.BlockSpec((tm // seg.seq, None, seg.seq, cw), lambda i: (i, slot, 0, 0)))
        out_shape.append(jax.ShapeDtypeStruct((seg.batch, n_slots, seg.seq, cw), F32))
        in_specs.append(pl.BlockSpec(memory_space=pl.ANY))
        args.append(prev)
        aliases = {len(args) - 1: 3}
    if pending is not None:
        p_specs, p_args, xo_spec, xo_shape = _pending_io(pending, seg, tm, d, n)
        in_specs, args = p_specs + in_specs, p_args + args
        out_specs.append(xo_spec)
        out_shape.append(xo_shape)
        aliases = {k + len(p_args): v for k, v in aliases.items()}
        aliases[len(p_args)] = len(out_shape) - 1
    return pl.pallas_call(
        functools.partial(_mla_proj_kernel, rope=rope, pending=pending is not None),
        grid=(rows // tm,),
        in_specs=in_specs,
        out_specs=out_specs,
        out_shape=out_shape,
        input_output_aliases=aliases,
        compiler_params=_cparams("parallel"),
        name="mla_proj_lat" if rope else "mla_proj_ctx",
    )(*args)


def _cache_kv_kernel(c_ref, wkn_ref, wv_ref, k_ref, v_ref):
    c = c_ref[...]
    ckv = c[:, :MLA_KV_LORA].astype(BF16)
    kpe_b = c[:, MLA_KV_LORA:].astype(BF16)
    kn = _dot(ckv, wkn_ref[...])
    _store_values(v_ref, _dot(ckv, wv_ref[...]))
    for hd in range(MLA_HEADS):
        lo = hd * 2 * LANE
        k_ref[:, lo:lo + LANE] = kn[:, hd * LANE:(hd + 1) * LANE].astype(BF16)
        k_ref[:, lo + LANE:lo + 2 * LANE] = kpe_b


def _cache_kv(cache_pad, w):
    rows, cw = cache_pad.shape
    hq = MLA_HEADS * 2 * LANE
    const = lambda i: (0, 0)
    tm = min(PROJ_ROWS, rows)
    assert rows % tm == 0
    return pl.pallas_call(
        _cache_kv_kernel,
        grid=(rows // tm,),
        in_specs=[
            pl.BlockSpec((tm, cw), lambda i: (i, 0)),
            pl.BlockSpec(w["w_kn"].shape, const),
            pl.BlockSpec(w["w_v"].shape, const),
        ],
        out_specs=[pl.BlockSpec((tm, hq), lambda i: (i, 0))] * 2,
        out_shape=[jax.ShapeDtypeStruct((rows, hq), BF16)] * 2,
        compiler_params=_cparams("parallel"),
        name="mla_cache_kv",
    )(cache_pad, w["w_kn"], w["w_v"])


def _attn_kernel(*refs, n_parts, n_seq):
    q_ref = refs[0]
    kv_refs = refs[1:1 + 2 * n_parts]
    wo_ref, x_ref, m_ref = refs[1 + 2 * n_parts:4 + 2 * n_parts]
    o_ref, acc_ref = refs[-2:]
    tq = q_ref.shape[0] // n_seq
    for sq, hd in [(sq, hd) for sq in range(n_seq) for hd in range(MLA_HEADS)]:
        rows = slice(sq * tq, (sq + 1) * tq)
        kcol = slice(hd * 2 * LANE, (hd + 1) * 2 * LANE)
        keys = [slice(sq * (r.shape[0] // n_seq), (sq + 1) * (r.shape[0] // n_seq)) for r in kv_refs[::2]]
        scores = [_dot_nt(q_ref[rows, kcol], kv_refs[2 * p][keys[p], kcol]) for p in range(n_parts)]
        mx = scores[0].max(axis=-1, keepdims=True)
        for s in scores[1:]:
            mx = jnp.maximum(mx, s.max(axis=-1, keepdims=True))
        out = None
        for p, s in enumerate(scores):
            e = jnp.exp((s - mx).astype(BF16))
            pv = _dot(e, kv_refs[2 * p + 1][keys[p], kcol])
            out = pv if out is None else out + pv
        acc_ref[rows, hd * MLA_V:(hd + 1) * MLA_V] = (out[:, :MLA_V] / out[:, LANE:LANE + MLA_V]).astype(BF16)
    y = _dot(acc_ref[...], wo_ref[...])
    o_ref[...] = x_ref[...] + m_ref[2] * y


def _attention(x, x_row0, n, dest, mods, layer, seg, q, kv_parts, w_o):
    d = x.shape[1]
    hq = MLA_HEADS * 2 * LANE
    hv = MLA_HEADS * MLA_V
    tq = seg.seq_tile(ATTN_ROWS)
    tps = seg.seq // tq
    n_seq = ATTN_SHORT_SEQS if (tps == 1 and not seg.per_batch_mod and seg.batch % ATTN_SHORT_SEQS == 0) else 1
    tq *= n_seq
    x_tile0, out_tile0 = x_row0 // tq, seg.row0 // tq
    in_specs = [pl.BlockSpec((tq, hq), lambda b, i: (b * tps + i, 0))]
    args = [q]
    for k, v, rows in kv_parts:
        mode = dict(pipeline_mode=pl.Buffered(1)) if tps > 1 else {}
        in_specs += [pl.BlockSpec((n_seq * rows, hq), lambda b, i: (b, 0)),
                     pl.BlockSpec((n_seq * rows, hq), lambda b, i: (b, 0), **mode)]
        args += [k, v]
    in_specs += [
        pl.BlockSpec(w_o.shape, lambda b, i: (0, 0)),
        pl.BlockSpec((tq, d), lambda b, i: (x_tile0 + b * tps + i, 0)),
        _mod_spec(layer, seg, d, tq // n_seq, tile_of=lambda b, i: (b * tps + i) * n_seq),
    ]
    args += [w_o, x, mods]
    if isinstance(dest, str):
        assert dest == "inplace"
        aliases = {len(args) - 2: 0}
    else:
        in_specs.append(pl.BlockSpec(memory_space=pl.ANY))
        args.append(dest)
        aliases = {len(args) - 1: 0}
    return pl.pallas_call(
        functools.partial(_attn_kernel, n_parts=len(kv_parts), n_seq=n_seq),
        grid=(seg.batch // n_seq, tps),
        in_specs=in_specs,
        out_specs=pl.BlockSpec((tq, d), lambda b, i: (out_tile0 + b * tps + i, 0)),
        out_shape=jax.ShapeDtypeStruct((n, d), F32),
        scratch_shapes=[pltpu.VMEM((tq, hv), BF16)],
        input_output_aliases=aliases,
        compiler_params=_cparams("parallel", "arbitrary"),
        name="mla_attention",
    )(*args)


RET_COL = 1024


RET_ROWS = 1024


def _ret_proj_kernel(*refs, kinds, dk, pending):
    refs, resolve = _take_pending(refs, pending)
    rotary = "rotary" in kinds
    if rotary:
        x_ref, m_ref, g_ref, w_ref, cos_ref, sin_ref, z_ref = refs
        cos, sin = cos_ref[...], sin_ref[...]
    else:
        x_ref, m_ref, g_ref, w_ref, z_ref = refs
    h = _modulate(resolve(x_ref[...]), g_ref[...], m_ref[0], m_ref[1]).astype(BF16)
    half = dk // 2
    for j, kind in enumerate(kinds):
        c0 = j * RET_COL
        acc = _dot(h, w_ref[:, c0:c0 + RET_COL])
        if kind == "rotary":
            for hd in range(RET_COL // dk):
                lo = hd * dk
                x1, x2 = acc[:, lo:lo + half], acc[:, lo + half:lo + dk]
                z_ref[:, c0 + lo:c0 + lo + half] = (x1 * cos - x2 * sin).astype(BF16)
                z_ref[:, c0 + lo + half:c0 + lo + dk] = (x1 * sin + x2 * cos).astype(BF16)
        elif kind == "silu":
            z_ref[:, c0:c0 + RET_COL] = (acc * _sigmoid(acc)).astype(BF16)
        else:
            z_ref[:, c0:c0 + RET_COL] = acc.astype(BF16)


def _ret_proj(x, mods, layer, seg, g1, w_in, group, kinds, rot_tabs, dk, pending=None):
    n, d = x.shape
    rows = seg.rows
    tm = seg.tile(RET_ROWS)
    ncol = len(kinds) * RET_COL
    tps = max(seg.seq // tm, 1)
    in_specs = [
        pl.BlockSpec((tm, d), lambda i: (seg.row0 // tm + i, 0)),
        _mod_spec(layer, seg, d, tm),
        pl.BlockSpec((1, d), lambda i: (0, 0)),
        pl.BlockSpec((d, ncol), lambda i: (0, group), pipeline_mode=pl.Buffered(1)),
    ]
    args = [x, mods, g1, w_in]
    if "rotary" in kinds:
        tab = pl.BlockSpec((tm, dk // 2), lambda i: (i % tps, 0))
        in_specs += [tab, tab]
        args += list(rot_tabs)
    out_specs = [pl.BlockSpec((tm, ncol), lambda i: (i, 0))]
    out_shape = [jax.ShapeDtypeStruct((rows, ncol), BF16)]
    aliases = {}
    if pending is not None:
        p_specs, p_args, xo_spec, xo_shape = _pending_io(pending, seg, tm, d, n)
        in_specs, args = p_specs + in_specs, p_args + args
        out_specs.append(xo_spec)
        out_shape.append(xo_shape)
        aliases = {len(p_args): 1}
    res = pl.pallas_call(
        functools.partial(_ret_proj_kernel, kinds=kinds, dk=dk, pending=pending is not None),
        grid=(rows // tm,),
        in_specs=in_specs,
        out_specs=out_specs,
        out_shape=out_shape,
        input_output_aliases=aliases,
        compiler_params=_cparams("parallel"),
        name="ret_proj_" + kinds[0],
    )(*args)
    return res if pending is not None else res[0]


def _log_sigmoid(x):
    return jnp.minimum(x, 0.0) - jnp.log(1.0 + jnp.exp(-jnp.abs(x)))


def _ret_scan_kernel(*refs, has_s0, emit_state, n_chunks, heads, fuse_out):
    refs = list(refs)
    lf_ref, lb_ref, q_ref, k_ref, v_ref, gf_ref, gb_ref = refs[:7]
    pos = 7
    s0_ref = None
    if has_s0:
        s0_ref = refs[pos]
        pos += 1
    if fuse_out:
        wo_ref, x_ref, m_ref, o_ref = refs[pos:pos + 4]
        pos += 4
    else:
        y_ref = refs[pos]
        pos += 1
    sout_ref = None
    if emit_state:
        sout_ref = refs[pos]
        pos += 1
    if fuse_out:
        s_ref, yf_ref, y_ref = refs[pos:]
    else:
        s_ref, yf_ref = refs[pos:]
    c = RET_CHUNK
    dk, dv = s_ref.shape
    ii = lax.broadcasted_iota(jnp.int32, (c, c), 0).astype(F32)
    jj = lax.broadcasted_iota(jnp.int32, (c, c), 1).astype(F32)
    idx = lax.broadcasted_iota(jnp.int32, (c, 1), 0).astype(F32)

    for hd, direction in [(hd, direction) for hd in range(heads) for direction in range(2)]:
        fwd = direction == 0
        kcol, vcol = slice(hd * dk, (hd + 1) * dk), slice(hd * dv, (hd + 1) * dv)
        lg = _log_sigmoid((lf_ref if fwd else lb_ref)[hd])
        rel = (ii - jj) if fwd else (jj - ii)
        keep = rel >= 0
        decay_in = jnp.where(keep, jnp.exp(jnp.where(keep, rel, 0.0) * lg), 0.0)
        decay_q = jnp.exp(((idx + 1.0) if fwd else (c - idx)) * lg)
        decay_k = jnp.exp(((c - 1.0 - idx) if fwd else idx) * lg)
        decay_c = jnp.exp(c * lg)
        g_ref = gf_ref if fwd else gb_ref

        def chunk(cc, state, fwd=fwd, decay_in=decay_in, decay_q=decay_q, decay_k=decay_k, decay_c=decay_c,
                  g_ref=g_ref, kcol=kcol, vcol=vcol):
            r0 = cc * c if isinstance(cc, int) else pl.multiple_of(cc * c, c)
            qc = q_ref[pl.ds(r0, c), kcol]
            kc = k_ref[pl.ds(r0, c), kcol]
            vc = v_ref[pl.ds(r0, c), vcol]
            sc = _dot_nt(qc, kc) * decay_in
            out = _dot(sc.astype(BF16), vc)
            kd_t = (kc.astype(F32) * decay_k).T.astype(BF16)
            new_s = _dot(kd_t, vc)
            if state is not None:
                out = out + decay_q * _dot(qc, state.astype(BF16))
                new_s = decay_c * state + new_s
            s_ref[...] = new_s
            o = _rms(out) * g_ref[pl.ds(r0, c), vcol].astype(F32)
            if fwd:
                yf_ref[pl.ds(r0, c), :] = o
            else:
                y_ref[pl.ds(r0, c), vcol] = (yf_ref[pl.ds(r0, c), :] + o).astype(BF16)

        chunk(0 if fwd else n_chunks - 1, s0_ref[direction, hd] if has_s0 else None)

        def step(ci, carry, fwd=fwd, chunk=chunk):
            chunk(ci if fwd else n_chunks - 1 - ci, s_ref[...])
            return carry

        lax.fori_loop(1, n_chunks, step, 0, unroll=True)
        if emit_state:
            sout_ref[direction, hd] = s_ref[...]
    if fuse_out:
        o_ref[...] = x_ref[...] + m_ref[2] * _dot(y_ref[...], wo_ref[...])


def _ret_scan(qkv, g, seg, logit_f, logit_b, s0, emit_state, dk, dv, heads, out_proj=None):
    rows = seg.batch * seg.seq
    t = seg.seq
    hh = RET_HEADS
    groups = hh // heads
    fuse_out = out_proj is not None
    assert not fuse_out or groups == 1
    v0 = 2 * hh * dk // (heads * dv)
    assert groups * heads == hh and v0 * heads * dv == 2 * hh * dk
    in_specs = [
        pl.BlockSpec((heads, 1, 1), lambda b, h: (h, 0, 0)),
        pl.BlockSpec((heads, 1, 1), lambda b, h: (h, 0, 0)),
        pl.BlockSpec((t, heads * dk), lambda b, h: (b, h)),
        pl.BlockSpec((t, heads * dk), lambda b, h: (b, groups + h)),
        pl.BlockSpec((t, heads * dv), lambda b, h: (b, v0 + h)),
        pl.BlockSpec((t, heads * dv), lambda b, h: (b, h)),
        pl.BlockSpec((t, heads * dv), lambda b, h: (b, groups + h)),
    ]
    args = [logit_f.reshape(hh, 1, 1), logit_b.reshape(hh, 1, 1), qkv, qkv, qkv, g, g]
    state_spec = pl.BlockSpec((None, 2, heads, dk, dv), lambda b, h: (b, 0, h, 0, 0))
    if s0 is not None:
        in_specs.append(state_spec)
        args.append(s0)
    scratch = [pltpu.VMEM((dk, dv), F32), pltpu.VMEM((t, dv), F32)]
    aliases = {}
    if fuse_out:
        x, mods, layer, w_o = out_proj
        n, d = x.shape
        x_spec = pl.BlockSpec((t, d), lambda b, h: (seg.row0 // t + b, 0))
        in_specs += [pl.BlockSpec(w_o.shape, lambda b, h: (0, 0)), x_spec,
                     _mod_spec(layer, seg, d, t, tile_of=lambda b, h: b)]
        args += [w_o, x, mods]
        aliases = {len(args) - 2: 0}
        out_specs, out_shape = [x_spec], [jax.ShapeDtypeStruct((n, d), F32)]
        scratch.append(pltpu.VMEM((t, hh * dv), BF16))
    else:
        out_specs = [pl.BlockSpec((t, heads * dv), lambda b, h: (b, h))]
        out_shape = [jax.ShapeDtypeStruct((rows, hh * dv), BF16)]
    if emit_state:
        out_specs.append(state_spec)
        out_shape.append(jax.ShapeDtypeStruct((seg.batch, 2, hh, dk, dv), F32))
    res = pl.pallas_call(
        functools.partial(_ret_scan_kernel, has_s0=s0 is not None, emit_state=emit_state,
                          n_chunks=t // RET_CHUNK, heads=heads, fuse_out=fuse_out),
        grid=(seg.batch, groups),
        in_specs=in_specs,
        out_specs=out_specs,
        out_shape=out_shape,
        scratch_shapes=scratch,
        input_output_aliases=aliases,
        compiler_params=_cparams("parallel", "parallel"),
        name="ret_scan",
    )(*args)
    return res if emit_state else (res[0], None)


def _mm_res_kernel(a_ref, w_ref, x_ref, m_ref, o_ref):
    o_ref[...] = x_ref[...] + m_ref[2] * _dot(a_ref[...], w_ref[...])


def _matmul_residual(x, mods, layer, seg, a, w):
    n, d = x.shape
    tm = seg.tile(RET_ROWS)
    x_spec = pl.BlockSpec((tm, d), lambda i: (seg.row0 // tm + i, 0))
    return pl.pallas_call(
        _mm_res_kernel,
        grid=(seg.rows // tm,),
        in_specs=[
            pl.BlockSpec((tm, a.shape[1]), lambda i: (i, 0)),
            pl.BlockSpec(w.shape, lambda i: (0, 0)),
            x_spec,
            _mod_spec(layer, seg, d, tm),
        ],
        out_specs=x_spec,
        out_shape=jax.ShapeDtypeStruct((n, d), F32),
        input_output_aliases={2: 0},
        compiler_params=_cparams("parallel"),
        name="matmul_residual",
    )(a, w, x, mods)


def _fnet_a_kernel(*refs, gd, pending):
    refs, resolve = _take_pending(refs, pending)
    x_ref, m_ref, g_ref, cs_ref, ac_ref, as_ref = refs
    h = _modulate(resolve(x_ref[...]), g_ref[...], m_ref[0], m_ref[1]).astype(BF16)
    cs = cs_ref[...]
    for g in range(FNET_GROUPS):
        a = _dot(h[:, g * gd:(g + 1) * gd], cs)
        ac_ref[:, g * gd:(g + 1) * gd] = a[:, :gd].astype(BF16)
        as_ref[:, g * gd:(g + 1) * gd] = a[:, gd:].astype(BF16)


def _fnet_a(x, mods, layer, seg, g1, cs, pending=None):
    n, d = x.shape
    rows = seg.rows
    tm = seg.tile(PROJ_ROWS)
    out = pl.BlockSpec((tm, d), lambda i: (i, 0))
    in_specs = [
        pl.BlockSpec((tm, d), lambda i: (seg.row0 // tm + i, 0)),
        _mod_spec(layer, seg, d, tm),
        pl.BlockSpec((1, d), lambda i: (0, 0)),
        pl.BlockSpec(cs.shape, lambda i: (0, 0)),
    ]
    args = [x, mods, g1, cs]
    out_specs, out_shape, aliases = [out, out], [jax.ShapeDtypeStruct((rows, d), BF16)] * 2, {}
    if pending is not None:
        p_specs, p_args, xo_spec, xo_shape = _pending_io(pending, seg, tm, d, n)
        in_specs, args = p_specs + in_specs, p_args + args
        out_specs.append(xo_spec)
        out_shape.append(xo_shape)
        aliases = {len(p_args): 2}
    return pl.pallas_call(
        functools.partial(_fnet_a_kernel, gd=d // FNET_GROUPS, pending=pending is not None),
        grid=(rows // tm,),
        in_specs=in_specs,
        out_specs=out_specs,
        out_shape=out_shape,
        input_output_aliases=aliases,
        compiler_params=_cparams("parallel"),
        name="fnet_channel_dft",
    )(*args)


def _fnet_b_kernel(ct_ref, st_ref, ac_ref, as_ref, w_ref, x_ref, m_ref, o_ref, *, norm, n_seq):
    t = ac_ref.shape[0] // n_seq
    ct, st = ct_ref[...], st_ref[...]
    f = [_dot(ct, ac_ref[s * t:(s + 1) * t, :]) - _dot(st, as_ref[s * t:(s + 1) * t, :]) for s in range(n_seq)]
    f = (f[0] if n_seq == 1 else jnp.concatenate(f, axis=0)) * norm
    o_ref[...] = x_ref[...] + m_ref[2] * _dot(f.astype(BF16), w_ref[...])


def _fnet_b(x, mods, layer, seg, ac, as_, ct, st, w, norm):
    n, d = x.shape
    t = seg.seq
    tq = seg.seq_tile(FNET_ROWS)
    tps = t // tq
    n_seq = ATTN_SHORT_SEQS if (tps == 1 and not seg.per_batch_mod and seg.batch % ATTN_SHORT_SEQS == 0) else 1
    rows = n_seq * tq
    x_spec = pl.BlockSpec((rows, d), lambda b, i: (seg.row0 // rows + b * tps + i, 0))
    tab = pl.BlockSpec((tq, t), lambda b, i: (i, 0))
    seq = pl.BlockSpec((n_seq * t, d), lambda b, i: (b, 0))
    return pl.pallas_call(
        functools.partial(_fnet_b_kernel, norm=norm, n_seq=n_seq),
        grid=(seg.batch // n_seq, tps),
        in_specs=[tab, tab, seq, seq, pl.BlockSpec(w.shape, lambda b, i: (0, 0)), x_spec,
                  _mod_spec(layer, seg, d, tq, tile_of=lambda b, i: (b * tps + i) * n_seq)],
        out_specs=x_spec,
        out_shape=jax.ShapeDtypeStruct((n, d), F32),
        input_output_aliases={5: 0},
        compiler_params=_cparams("parallel", "arbitrary"),
        name="fnet_position_dft",
    )(ct, st, ac, as_, w, x, mods)


def _pack_halves(a):
    w = a.shape[1] // 2
    bits = lambda v: lax.bitcast_convert_type(v.astype(BF16).astype(F32), jnp.uint32)
    return (bits(a[:, :w]) >> 16) | (bits(a[:, w:]) & jnp.uint32(0xFFFF0000))


def _router_kernel(x_ref, m_ref, g_ref, rwhi_ref, rwlo_ref, rb_ref, h_ref, idx_ref, rank_ref, wcol_ref, cnt_ref,
                   run_ref, tri_ref):
    step = pl.program_id(0)

    @pl.when(step == 0)
    def _():
        run_ref[...] = jnp.zeros_like(run_ref)
        tt = tri_ref.shape[0]
        earlier = lax.broadcasted_iota(jnp.int32, (tt, tt), 0) < lax.broadcasted_iota(jnp.int32, (tt, tt), 1)
        tri_ref[...] = jnp.where(earlier, 1.0, 0.0).astype(BF16)

    parts = []
    for r0 in range(0, x_ref.shape[0], ROUTER_SUB_ROWS):
        rows = slice(r0, r0 + ROUTER_SUB_ROWS)
        h = _modulate(x_ref[rows, :], g_ref[...], m_ref[3], m_ref[4])
        h_ref[rows, :] = _pack_halves(h)
        h_hi = h.astype(BF16)
        h_lo = (h - h_hi.astype(F32)).astype(BF16)
        parts.append(_dot_nt(rwhi_ref[...], h_hi) + (_dot_nt(rwhi_ref[...], h_lo) + _dot_nt(rwlo_ref[...], h_hi)))
    logits = jnp.concatenate(parts, axis=1)
    sc = _sigmoid(logits)
    gr = sc + rb_ref[...]
    gp = EXPERTS_PER_GROUP
    row = lambda a, e: a[e:e + 1, :]
    best_g = None
    for g in range(N_EXPERT_GROUPS):
        vals = [row(gr, g * gp + i) for i in range(gp)]
        gs = None
        for i in range(gp):
            for j in range(i + 1, gp):
                pair = vals[i] + vals[j]
                gs = pair if gs is None else jnp.maximum(gs, pair)
        if best_g is None:
            best_g, best_v = jnp.zeros(gs.shape, jnp.int32), gs
        else:
            better = gs > best_v
            best_g = jnp.where(better, g, best_g)
            best_v = jnp.where(better, gs, best_v)
    sel, raw = [], []
    for i in range(gp):
        s_i, r_i = row(gr, i), row(sc, i)
        for g in range(1, N_EXPERT_GROUPS):
            s_i = jnp.where(best_g == g, row(gr, g * gp + i), s_i)
            r_i = jnp.where(best_g == g, row(sc, g * gp + i), r_i)
        sel.append(s_i)
        raw.append(r_i)

    def argmax_first(vals, raws):
        bi, bv, br = jnp.zeros(vals[0].shape, jnp.int32), vals[0], raws[0]
        for i in range(1, len(vals)):
            better = vals[i] > bv
            bi = jnp.where(better, i, bi)
            bv = jnp.where(better, vals[i], bv)
            br = jnp.where(better, raws[i], br)
        return bi, br

    i1, w1 = argmax_first(sel, raw)
    masked = [jnp.where(i1 == i, -jnp.inf, sel[i]) for i in range(gp)]
    i2, w2 = argmax_first(masked, raw)
    tot = w1 + w2
    e1 = best_g * gp + i1
    e2 = best_g * gp + i2
    idx_ref[0:1, :] = e1
    idx_ref[1:2, :] = e2
    t = e1.shape[1]
    sub = lax.broadcasted_iota(jnp.int32, (8, t), 0)
    w8 = jnp.where(sub == 0, w1 / tot, jnp.where(sub == 1, w2 / tot, 0.0))
    wcol_ref[...] = w8.T
    eio = lax.broadcasted_iota(jnp.int32, (N_EXPERTS, t), 0)
    oh1, oh2 = eio == e1, eio == e2
    oh = jnp.where(oh1, 1.0, jnp.where(oh2, 1.0, 0.0))
    local = _dot(oh.astype(BF16), tri_ref[...])
    rank = local + run_ref[:, 0:1]
    rank_ref[0:1, :] = jnp.sum(jnp.where(oh1, rank, 0.0), axis=0, keepdims=True).astype(jnp.int32)
    rank_ref[1:2, :] = jnp.sum(jnp.where(oh2, rank, 0.0), axis=0, keepdims=True).astype(jnp.int32)
    run_ref[...] = run_ref[...] + jnp.sum(oh, axis=1, keepdims=True)
    cnt_ref[...] = run_ref[...]


def _wide_mod_row(segs, tm):
    ctx, lat = segs
    ctx_tiles = ctx.batch * ctx.seq // tm
    assert ctx_tiles * tm == ctx.batch * ctx.seq and lat.seq % tm == 0
    return lambda i: jnp.where(i < ctx_tiles, ctx.mod0, lat.mod0 + (i - ctx_tiles) // (lat.seq // tm))


def _router(x, mods, layer, segs, g2, rw_hi, rw_lo, rb):
    n, d = x.shape
    tm = WIDE_TILE
    mod_row = _wide_mod_row(segs, tm)
    return pl.pallas_call(
        _router_kernel,
        grid=(n // tm,),
        in_specs=[
            pl.BlockSpec((tm, d), lambda i: (i, 0)),
            pl.BlockSpec((None, None, 6, 1, d), lambda i: (layer, mod_row(i), 0, 0, 0)),
            pl.BlockSpec((1, d), lambda i: (0, 0)),
            pl.BlockSpec(rw_hi.shape, lambda i: (0, 0)),
            pl.BlockSpec(rw_lo.shape, lambda i: (0, 0)),
            pl.BlockSpec(rb.shape, lambda i: (0, 0)),
        ],
        out_specs=[
            pl.BlockSpec((tm, d // 2), lambda i: (i, 0)),
            pl.BlockSpec((2, tm), lambda i: (0, i)),
            pl.BlockSpec((2, tm), lambda i: (0, i)),
            pl.BlockSpec((tm, 8), lambda i: (i, 0)),
            pl.BlockSpec((N_EXPERTS, LANE), lambda i: (0, 0)),
        ],
        out_shape=[
            jax.ShapeDtypeStruct((n, d // 2), jnp.uint32),
            jax.ShapeDtypeStruct((2, n), jnp.int32),
            jax.ShapeDtypeStruct((2, n), jnp.int32),
            jax.ShapeDtypeStruct((n, 8), F32),
            jax.ShapeDtypeStruct((N_EXPERTS, LANE), F32),
        ],
        scratch_shapes=[pltpu.VMEM((N_EXPERTS, LANE), F32), pltpu.VMEM((tm, tm), BF16)],
        compiler_params=_cparams("arbitrary"),
        name="moe_router",
    )(x, mods, g2, rw_hi, rw_lo, rb)


def _expert_kernel(be_ref, br_ref, bs_ref, xs_ref, wg_ref, wu_ref, wd_ref, y_ref, wg_b, wu_b, wd_b):
    i = pl.program_id(0)
    prev = be_ref[jnp.maximum(i - 1, 0)]
    valid = br_ref[i]
    tail_rows = MOE_TAIL_ROWS

    @pl.when(jnp.logical_or(i == 0, be_ref[i] != prev))
    def _():
        wg_b[...] = wg_ref[...].astype(BF16)
        wu_b[...] = wu_ref[...].astype(BF16)
        wd_b[...] = wd_ref[...].astype(BF16)

    def ffn(r0, nrows):
        lo, hi = _unpack_halves(xs_ref[r0:r0 + nrows, :])
        xb = jnp.concatenate([lo.astype(BF16), hi.astype(BF16)], axis=1)
        gate = _dot(xb, wg_b[...])
        hid = (gate * _sigmoid(gate)) * _dot(xb, wu_b[...])
        y_ref[r0:r0 + nrows, :] = _pack_halves(_dot(hid.astype(BF16), wd_b[...]))

    @pl.when(valid > tail_rows)
    def _():
        ffn(0, y_ref.shape[0])

    @pl.when(jnp.logical_and(valid > 0, valid <= tail_rows))
    def _():
        ffn(0, tail_rows)
        y_ref[tail_rows:, :] = jnp.zeros((y_ref.shape[0] - tail_rows, y_ref.shape[1]), y_ref.dtype)


def _experts(xs, block_e, block_rows, block_src, w_gate, w_up, w_down, layer):
    rows, half = xs.shape
    d = 2 * half
    de = w_gate.shape[-1]
    n_blocks = rows // MOE_ROWS
    grid_spec = pltpu.PrefetchScalarGridSpec(
        num_scalar_prefetch=3,
        grid=(n_blocks,),
        in_specs=[
            pl.BlockSpec((MOE_ROWS, half), lambda i, be, br, bs: (bs[i], 0)),
            pl.BlockSpec((None, None, d, de), lambda i, be, br, bs: (layer, be[i], 0, 0)),
            pl.BlockSpec((None, None, d, de), lambda i, be, br, bs: (layer, be[i], 0, 0)),
            pl.BlockSpec((None, None, de, d), lambda i, be, br, bs: (layer, be[i], 0, 0)),
        ],
        out_specs=pl.BlockSpec((MOE_ROWS, half), lambda i, be, br, bs: (bs[i], 0)),
        scratch_shapes=[pltpu.VMEM((d, de), BF16), pltpu.VMEM((d, de), BF16), pltpu.VMEM((de, d), BF16)],
    )
    return pl.pallas_call(
        _expert_kernel,
        grid_spec=grid_spec,
        out_shape=jax.ShapeDtypeStruct((rows, half), jnp.uint32),
        compiler_params=_cparams("arbitrary"),
        name="moe_experts",
    )(block_e, block_rows, block_src, xs, w_gate, w_up, w_down)


def _combine_kernel(x_ref, m_ref, y_ref, w_ref, fg_ref, o_ref):
    x = x_ref[...] + _moe_residual(y_ref, w_ref, m_ref[5])
    o_ref[...] = _rms(x) * fg_ref[...]


def _combine(x, mods, layer, segs, seg, yg, wcol, final_g):
    n, d = x.shape
    tm = WIDE_TILE
    mod_row = _wide_mod_row(segs, tm)
    t0 = seg.row0 // tm
    steps = seg.rows // tm
    in_specs = [pl.BlockSpec((tm, d), lambda i: (t0 + i, 0)),
                pl.BlockSpec((None, None, 6, 1, d), lambda i: (layer, mod_row(t0 + i), 0, 0, 0)),
                pl.BlockSpec((2, tm, d // 2), lambda i: (0, i, 0)),
                pl.BlockSpec((tm, 8), lambda i: (t0 + i, 0)),
                pl.BlockSpec((1, d), lambda i: (0, 0))]
    return pl.pallas_call(
        _combine_kernel,
        grid=(steps,),
        in_specs=in_specs,
        out_specs=pl.BlockSpec((tm, d), lambda i: (i, 0)),
        out_shape=jax.ShapeDtypeStruct((seg.rows, d), F32),
        compiler_params=_cparams("parallel"),
        name="moe_combine_final",
    )(x, mods, yg, wcol, final_g)


def _sc_mesh():
    return plsc.VectorSubcoreMesh(core_axis_name="c", subcore_axis_name="s")


def _sc_worker_split(n):
    workers = SC_CORES * SC_SUBCORES
    per = n // workers
    assert per * workers == n and per % SC_CHUNK == 0
    return workers, per, per // SC_CHUNK


def _sc_dispatch(h, pos, rows):
    n, w = h.shape
    workers, per, chunks = _sc_worker_split(n)

    @functools.partial(
        pl.kernel, out_type=jax.ShapeDtypeStruct((rows, w), h.dtype), mesh=_sc_mesh(),
        scratch_types=[pltpu.VMEM((2, chunks, SC_CHUNK), jnp.int32), pltpu.VMEM((SC_CHUNK, w), h.dtype)],
        name="moe_dispatch_scatter")
    def scatter_rows(h_hbm, pos_hbm, xs_hbm, idx_v, rows_v):
        wid = lax.axis_index("s") * SC_CORES + lax.axis_index("c")
        pltpu.sync_copy(pos_hbm.at[0, wid], idx_v.at[0])
        pltpu.sync_copy(pos_hbm.at[1, wid], idx_v.at[1])

        @pl.loop(0, chunks)
        def _(c):
            pltpu.sync_copy(h_hbm.at[pl.ds(wid * per + c * SC_CHUNK, SC_CHUNK)], rows_v)
            pltpu.sync_copy(rows_v, xs_hbm.at[idx_v.at[0, c]])
            pltpu.sync_copy(rows_v, xs_hbm.at[idx_v.at[1, c]])

    return scatter_rows(h, pos.reshape(2, workers, chunks, SC_CHUNK))


def _sc_gather2(ys, pos):
    _, w = ys.shape
    n = pos.shape[1]
    workers, per, chunks = _sc_worker_split(n)

    @functools.partial(
        pl.kernel, out_type=jax.ShapeDtypeStruct((2, n, w), ys.dtype), mesh=_sc_mesh(),
        scratch_types=[pltpu.VMEM((2, chunks, SC_CHUNK), jnp.int32), pltpu.VMEM((SC_CHUNK, w), ys.dtype),
                       pltpu.SemaphoreType.DMA],
        name="moe_combine_gather")
    def gather_rows(ys_hbm, pos_hbm, out_hbm, idx_v, rows_v, sem):
        wid = lax.axis_index("s") * SC_CORES + lax.axis_index("c")
        pltpu.sync_copy(pos_hbm.at[0, wid], idx_v.at[0])
        pltpu.sync_copy(pos_hbm.at[1, wid], idx_v.at[1])

        @pl.loop(0, chunks)
        def _(c):
            for k in range(2):
                pltpu.async_copy(ys_hbm.at[idx_v.at[k, c]], rows_v, sem).wait()
                pltpu.sync_copy(rows_v, out_hbm.at[k, pl.ds(wid * per + c * SC_CHUNK, SC_CHUNK)])

    return gather_rows(ys, pos.reshape(2, workers, chunks, SC_CHUNK))


def _dispatch_plan(idx, rank, counts):
    n = idx.shape[1]
    padded = (counts + MOE_ROWS - 1) // MOE_ROWS * MOE_ROWS
    pad_end = jnp.cumsum(padded)
    pad_start = pad_end - padded
    experts = jnp.arange(N_EXPERTS, dtype=jnp.int32)
    start_of = jnp.sum(jnp.where(idx[..., None] == experts, pad_start, 0), axis=-1)
    pos = start_of + rank
    n_blocks = 2 * n // MOE_ROWS + N_EXPERTS
    steps = jnp.arange(n_blocks, dtype=jnp.int32)
    last_used = pad_end[-1] // MOE_ROWS - 1
    step = jnp.minimum(steps, last_used)
    block_e = jnp.minimum(jnp.sum(step[:, None] * MOE_ROWS >= pad_end[None, :], axis=1), N_EXPERTS - 1)
    pick = lambda per_expert: jnp.sum(jnp.where(block_e[:, None] == experts, per_expert, 0), axis=-1)
    first, count = pick(pad_start // MOE_ROWS), jnp.maximum(pick(padded // MOE_ROWS), 1)
    block_src = first + (step - first - 1) % count
    block_rows = jnp.clip(pick(pad_start + counts) - block_src * MOE_ROWS, 0, MOE_ROWS)
    block_rows = jnp.where(steps <= last_used, block_rows, 0)
    return (pos, block_e.astype(jnp.int32), block_rows.astype(jnp.int32), block_src.astype(jnp.int32),
            n_blocks * MOE_ROWS)


def _moe(x, mods, layer, segs, g2, rw_hi, rw_lo, rb, w_gate, w_up, w_down, final_g):
    h2p, idx, rank, wcol, cnt = _router(x, mods, layer, segs, g2, rw_hi, rw_lo, rb)
    pos, block_e, block_rows, block_src, rows = _dispatch_plan(idx, rank, cnt[:, 0].astype(jnp.int32))
    xs = _sc_dispatch(h2p, pos, rows)
    ys = _experts(xs, block_e, block_rows, block_src, w_gate, w_up, w_down, layer)
    yg = {seg: _sc_gather2(ys, pos[:, seg.row0:seg.row0 + seg.rows]) for seg in segs}
    if final_g is None:
        return yg, wcol, mods, layer
    return tuple(_combine(x, mods, layer, segs, seg, yg[seg], wcol, final_g) for seg in segs)


def _mla_rope_tables(t):
    axis_dim = MLA_ROPE // 2
    row = np.repeat(np.arange(t // GRID_W), GRID_W).astype(np.float64)
    col = np.tile(np.arange(GRID_W), t // GRID_W).astype(np.float64)
    inv = ROPE_BASE ** (-np.arange(0, axis_dim, 2, dtype=np.float64) / axis_dim)
    ar, ac = row[:, None] * inv[None, :], col[:, None] * inv[None, :]
    ones = np.ones((t, LANE - MLA_ROPE))
    cos = np.concatenate([np.cos(ar), np.cos(ar), np.cos(ac), np.cos(ac), ones], axis=-1)
    sin = np.concatenate([-np.sin(ar), np.sin(ar), -np.sin(ac), np.sin(ac), 0.0 * ones], axis=-1)
    return jnp.asarray(cos, F32), jnp.asarray(sin, F32)


def _ret_rot_tables(t, dk):
    inv = ROPE_BASE ** (-np.linspace(0.0, 1.0, dk // 2))
    ang = np.arange(t, dtype=np.float64)[:, None] * inv[None, :]
    return jnp.asarray(np.cos(ang), F32), jnp.asarray(np.sin(ang), F32)


def _dft_tables(n):
    k = np.arange(n, dtype=np.int64)
    ang = (np.outer(k, k) % n).astype(np.float64) * (2.0 * math.pi / n)
    return jnp.asarray(np.cos(ang), BF16), jnp.asarray(np.sin(ang), BF16)


def _mla_weights(w_in, q_g, kv_g, w_uq, w_ukv, w_o):
    d = w_in.shape[0]
    hd = MLA_NOPE + MLA_ROPE
    perm = _rope_perm()
    w_in_p = jnp.concatenate([w_in, w_in[:, MLA_Q_LORA + MLA_KV_LORA + perm]], axis=1)
    uq = w_uq.reshape(MLA_Q_LORA, MLA_HEADS, hd)
    uq = jnp.concatenate([uq, uq[..., MLA_NOPE + perm]], axis=-1)
    ukv = w_ukv.reshape(MLA_KV_LORA, MLA_HEADS, MLA_NOPE + MLA_V)
    return {
        "w_in": w_in_p.astype(BF16),
        "q_g": q_g.reshape(1, -1) * (MLA_NOPE + MLA_ROPE) ** -0.5,
        "kv_g": kv_g.reshape(1, -1),
        "w_q": uq.reshape(MLA_Q_LORA, MLA_HEADS * 2 * LANE).astype(BF16),
        "w_kn": ukv[..., :MLA_NOPE].reshape(MLA_KV_LORA, MLA_HEADS * MLA_NOPE).astype(BF16),
        "w_v": ukv[..., MLA_NOPE:].reshape(MLA_KV_LORA, MLA_HEADS * MLA_V).astype(BF16),
        "w_o": w_o.astype(BF16),
    }


def kernel(x_prompt, x_sample, cache_mla, state_ret, c, c_ctx, norm1_g, norm2_g, ada_w, ada_b, final_norm_g,
           mla_w_in, mla_q_norm_g, mla_kv_norm_g, mla_w_uq, mla_w_ukv, mla_w_o, ret_w_in, ret_decay_f,
           ret_decay_b, ret_w_o, fnet_w, router_w, router_b, moe_w_gate, moe_w_up, moe_w_down):
    b_ctx, t_ctx, d = x_prompt.shape
    b_lat, t_lat, _ = x_sample.shape
    depth = ada_w.shape[0]
    assert b_lat + 1 <= 8
    n_ctx = b_ctx * t_ctx
    ctx = _Seg(0, b_ctx, t_ctx, 0, False)
    lat = _Seg(n_ctx, b_lat, t_lat, 1, True)
    segs = (ctx, lat)

    n_lat = b_lat * t_lat
    n_mla = mla_w_in.shape[0]
    assert n_mla >= 1
    x = None
    new_cache = jnp.zeros((b_ctx, n_mla, t_ctx, MLA_KV_LORA + MLA_ROPE), F32)
    cond8 = jnp.concatenate([c_ctx[None, :], c, jnp.zeros((8 - 1 - b_lat, d), F32)], axis=0)
    mods = _modulation_all(cond8, ada_w, ada_b).reshape(depth, 8, 6, 1, d)

    rw_t = router_w.T.astype(F32)
    rw_hi = rw_t.astype(BF16)
    rw_lo = (rw_t - rw_hi.astype(F32)).astype(BF16)
    rb = router_b.reshape(N_EXPERTS, 1).astype(F32)
    final_g = final_norm_g.reshape(1, d)
    dk = ret_w_in.shape[2] // (8 * RET_HEADS)
    dv = 2 * dk

    states = []
    pending = None
    counters = [0, 0, 0]
    for layer in range(depth):
        kind = layer % 3
        j = counters[kind]
        counters[kind] += 1
        g1 = norm1_g[layer].reshape(1, d)
        g2 = norm2_g[layer].reshape(1, d)
        if kind == 0:
            w = _mla_weights(mla_w_in[j], mla_q_norm_g[j], mla_kv_norm_g[j], mla_w_uq[j], mla_w_ukv[j],
                             mla_w_o[j])
            if x is None:
                xc, xc0, xl, xl0 = x_prompt.reshape(n_ctx, d), 0, x_sample.reshape(n_lat, d), 0
            else:
                xc, xc0, xl, xl0 = x, ctx.row0, x, lat.row0
            if pending is None:
                qc, kc, vc, new_cache = _mla_proj(xc, xc0, mods, layer, ctx, g1, w, None, (new_cache, j, n_mla))
                ql, kl, vl = _mla_proj(xl, xl0, mods, layer, lat, g1, w, _mla_rope_tables(t_lat))
            else:
                qc, kc, vc, new_cache, x = _mla_proj(x, ctx.row0, mods, layer, ctx, g1, w, None,
                                                     (new_cache, j, n_mla), pending)
                ql, kl, vl, x = _mla_proj(x, lat.row0, mods, layer, lat, g1, w, _mla_rope_tables(t_lat), None,
                                          pending)
                xc = xl = x
            past = cache_mla.shape[2]
            cpad = jnp.pad(cache_mla[:, j].reshape(b_lat * past, -1), ((0, 0), (0, LANE - MLA_ROPE)))
            kp, vp = _cache_kv(cpad, w)
            first = x is None
            x = _attention(xc, xc0, n_ctx + n_lat, jnp.zeros((n_ctx + n_lat, d), F32) if first else "inplace",
                           mods, layer, ctx, qc, [(kc, vc, t_ctx)], w["w_o"])
            x = _attention(xl if first else x, xl0, n_ctx + n_lat, x if first else "inplace", mods, layer, lat,
                           ql, [(kl, vl, t_lat), (kp, vp, past)], w["w_o"])
        elif kind == 1:
            w_in = ret_w_in[j]
            qk = RET_HEADS * dk
            k_scale = jnp.concatenate([jnp.ones((qk,), F32), jnp.full((qk,), dk ** -0.5, F32),
                                       jnp.ones((w_in.shape[1] - 2 * qk,), F32)])
            w_in_b = (w_in * k_scale[None, :]).astype(BF16)
            w_o_b = ret_w_o[j].astype(BF16)
            rot = _ret_rot_tables(t_lat, dk)
            n_qk, n_v = 2 * qk // RET_COL, RET_HEADS * dv // RET_COL
            assert (n_qk + n_v) * RET_COL * 2 == w_in.shape[1]
            parts = []
            for seg in segs:
                kinds = (("rotary" if seg is lat else "plain"),) * n_qk + ("plain",) * n_v
                qkv = _ret_proj(x, mods, layer, seg, g1, w_in_b, 0, kinds, rot, dk, pending)
                if pending is not None:
                    qkv, x = qkv
                parts.append((qkv, _ret_proj(x, mods, layer, seg, g1, w_in_b, 1, ("silu",) * (n_qk + n_v), None, dk)))
            x, s_ctx = _ret_scan(*parts[0], ctx, ret_decay_f[j], ret_decay_b[j], None, True, dk, dv, RET_HEADS,
                                 out_proj=(x, mods, layer, w_o_b))
            yl, _ = _ret_scan(*parts[1], lat, ret_decay_f[j], ret_decay_b[j], state_ret[:, j], False, dk, dv, 1)
            x = _matmul_residual(x, mods, layer, lat, yl, w_o_b)
            states.append(s_ctx)
        else:
            gd = d // FNET_GROUPS
            cc, sc = _dft_tables(gd)
            cs = jnp.concatenate([cc, sc], axis=1)
            w_b = fnet_w[j].astype(BF16)
            for seg in segs:
                ct, st = _dft_tables(seg.seq)
                if pending is None:
                    ac, as_ = _fnet_a(x, mods, layer, seg, g1, cs)
                else:
                    ac, as_, x = _fnet_a(x, mods, layer, seg, g1, cs, pending)
                x = _fnet_b(x, mods, layer, seg, ac, as_, ct, st, w_b, (seg.seq * gd) ** -0.5)
        if layer < depth - 1:
            pending = _moe(x, mods, layer, segs, g2, rw_hi, rw_lo, rb, moe_w_gate, moe_w_up, moe_w_down, None)
        else:
            y_prompt, y_sample = _moe(x, mods, layer, segs, g2, rw_hi, rw_lo, rb, moe_w_gate, moe_w_up,
                                      moe_w_down, final_g)

    new_state = jnp.stack(states, axis=1)
    return (y_prompt.reshape(b_ctx, t_ctx, d), y_sample.reshape(b_lat, t_lat, d), new_cache, new_state)
```

```python
import functools
import math

import jax
import jax.numpy as jnp
import numpy as np
from jax import lax
from jax.experimental import pallas as pl
from jax.experimental.pallas import tpu as pltpu
from jax.experimental.pallas import tpu_sc as plsc

F32 = jnp.float32
BF16 = jnp.bfloat16

GRID_W = 64
MLA_HEADS = 8
MLA_NOPE = 128
MLA_ROPE = 64
MLA_V = 128
MLA_Q_LORA = 384
MLA_KV_LORA = 256
ROPE_BASE = 10000.0
RET_HEADS = 4
RET_CHUNK = 256
FNET_GROUPS = 4
N_EXPERTS = 16
N_EXPERT_GROUPS = 4
EXPERTS_PER_GROUP = 4
D_EXPERT = 512
NORM_EPS = 1e-6

LANE = 128
PROJ_ROWS = 512
ATTN_ROWS = 512
FNET_ROWS = 512
ATTN_SHORT_SEQS = 4
WIDE_TILE = 1024
ROUTER_SUB_ROWS = 256
MOE_ROWS = 1024
MOE_TAIL_ROWS = 256
VMEM_LIMIT = 56 * 1024 * 1024
SC_CORES = 2
SC_SUBCORES = 16
SC_CHUNK = 128


def _cparams(*sem):
    return pltpu.CompilerParams(dimension_semantics=sem, vmem_limit_bytes=VMEM_LIMIT)


def _sigmoid(x):
    return 1.0 / (1.0 + jnp.exp(-x))


def _rms(x):
    return x * lax.rsqrt(jnp.mean(x * x, axis=-1, keepdims=True) + NORM_EPS)


def _modulate(x, g, shift, scale):
    return _rms(x) * (g * (1.0 + scale)) + shift


def _dot(a, b):
    return jnp.dot(a, b, preferred_element_type=F32)


def _dot_nt(a, b):
    return lax.dot_general(a, b, (((1,), (1,)), ((), ())), preferred_element_type=F32)


def _mod_kernel(c_ref, w_ref, b_ref, o_ref):
    c = c_ref[...]
    s = (c * _sigmoid(c)).astype(BF16)
    o_ref[...] = _dot(s, w_ref[...].astype(BF16)) + b_ref[...]


def _modulation_all(cond8, ada_w, ada_b):
    depth, d, d6 = ada_w.shape
    tn = d6 // 4
    return pl.pallas_call(
        _mod_kernel,
        grid=(depth, d6 // tn),
        in_specs=[
            pl.BlockSpec((8, d), lambda l, n: (0, 0)),
            pl.BlockSpec((None, d, tn), lambda l, n: (l, 0, n)),
            pl.BlockSpec((None, 1, tn), lambda l, n: (l, 0, n)),
        ],
        out_specs=pl.BlockSpec((None, 8, tn), lambda l, n: (l, 0, n)),
        out_shape=jax.ShapeDtypeStruct((depth, 8, d6), F32),
        compiler_params=_cparams("parallel", "parallel"),
        name="modulation",
    )(cond8, ada_w, ada_b.reshape(depth, 1, d6))


class _Seg:
    def __init__(self, row0, batch, seq, mod0, per_batch_mod):
        self.row0, self.batch, self.seq = row0, batch, seq
        self.mod0, self.per_batch_mod = mod0, per_batch_mod
        self.rows = batch * seq

    def tile(self, want):
        tm = min(want, self.seq) if self.per_batch_mod else want
        assert self.rows % tm == 0 and self.row0 % tm == 0 and (self.seq % tm == 0 or tm % self.seq == 0)
        return tm

    def seq_tile(self, want):
        tm = min(want, self.seq)
        assert self.seq % tm == 0 and self.row0 % tm == 0
        return tm

    def mod_row(self, tile, tm):
        if self.per_batch_mod:
            return self.mod0 + tile * tm // self.seq
        return self.mod0


def _mod_spec(layer, seg, d, tm, tile_of=lambda *a: a[0]):
    return pl.BlockSpec((None, None, 6, 1, d), lambda *a: (layer, seg.mod_row(tile_of(*a), tm), 0, 0, 0))


def _unpack_halves(p):
    lo = lax.bitcast_convert_type(p << 16, F32)
    hi = lax.bitcast_convert_type(p & jnp.uint32(0xFFFF0000), F32)
    return lo, hi


def _moe_residual(y_ref, w_ref, gate):
    w = w_ref[...]
    lo0, hi0 = _unpack_halves(y_ref[0])
    lo1, hi1 = _unpack_halves(y_ref[1])
    w0, w1 = w[:, 0:1], w[:, 1:2]
    return gate * jnp.concatenate([w0 * lo0 + w1 * lo1, w0 * hi0 + w1 * hi1], axis=-1)


def _take_pending(refs, pending):
    if not pending:
        return refs, lambda x: x
    y_ref, w_ref, pm_ref, *rest = refs
    xo_ref = rest.pop()

    def resolve(x):
        x = x + _moe_residual(y_ref, w_ref, pm_ref[5])
        xo_ref[...] = x
        return x

    return rest, resolve


def _pending_io(pending, seg, tm, d, n):
    yg_by_seg, wcol, mods, layer = pending
    t0 = seg.row0 // tm
    specs = [pl.BlockSpec((2, tm, d // 2), lambda i: (0, i, 0)),
             pl.BlockSpec((tm, 8), lambda i: (t0 + i, 0)),
             _mod_spec(layer, seg, d, tm)]
    return (specs, [yg_by_seg[seg], wcol, mods], pl.BlockSpec((tm, d), lambda i: (t0 + i, 0)),
            jax.ShapeDtypeStruct((n, d), F32))


def _rope_partner(x):
    return pltpu.roll(x, LANE // 2, 1)


def _rope_perm():
    return np.array([l + 16 if l % 32 < 16 else l - 16 for l in range(MLA_ROPE)])


def _store_values(v_ref, v):
    ones = jnp.ones((v.shape[0], LANE), BF16)
    for hd in range(MLA_HEADS):
        v_ref[:, hd * 2 * LANE:hd * 2 * LANE + LANE] = v[:, hd * MLA_V:(hd + 1) * MLA_V].astype(BF16)
        v_ref[:, hd * 2 * LANE + LANE:(hd + 1) * 2 * LANE] = ones


def _mla_proj_kernel(*refs, rope, pending):
    refs, resolve = _take_pending(refs, pending)
    if rope:
        (x_ref, m_ref, g_ref, win_ref, qg_ref, kvg_ref, wq_ref, wkn_ref, wv_ref, cos_ref, sin_ref,
         q_ref, k_ref, v_ref) = refs
    else:
        x_ref, m_ref, g_ref, win_ref, qg_ref, kvg_ref, wq_ref, wkn_ref, wv_ref = refs[:9]
        q_ref, k_ref, v_ref, cache_ref = refs[-4:]
    h = _modulate(resolve(x_ref[...]), g_ref[...], m_ref[0], m_ref[1]).astype(BF16)
    z = _dot(h, win_ref[...])
    cq = z[:, :MLA_Q_LORA]
    ckv = z[:, MLA_Q_LORA:MLA_Q_LORA + MLA_KV_LORA]
    kpe = z[:, MLA_Q_LORA + MLA_KV_LORA:]
    cqn = (_rms(cq) * qg_ref[...]).astype(BF16)
    ckvn = _rms(ckv) * kvg_ref[...]
    ckvb = ckvn.astype(BF16)
    q = _dot(cqn, wq_ref[...])
    kn = _dot(ckvb, wkn_ref[...])
    _store_values(v_ref, _dot(ckvb, wv_ref[...]))
    if rope:
        cos, sin = cos_ref[...], sin_ref[...]
        kpe = kpe * cos + _rope_partner(kpe) * sin
    kpe = jnp.where(lax.broadcasted_iota(jnp.int32, kpe.shape, 1) < MLA_ROPE, kpe, 0.0)
    if not rope:
        seq = cache_ref.shape[1]
        for s in range(cache_ref.shape[0]):
            cache_ref[s, :, :MLA_KV_LORA] = ckvn[s * seq:(s + 1) * seq, :]
            cache_ref[s, :, MLA_KV_LORA:] = kpe[s * seq:(s + 1) * seq, :MLA_ROPE]
    kpe_b = kpe.astype(BF16)
    for hd in range(MLA_HEADS):
        lo = hd * 2 * LANE
        q_ref[:, lo:lo + LANE] = q[:, lo:lo + LANE].astype(BF16)
        qr = q[:, lo + LANE:lo + 2 * LANE]
        if rope:
            qr = qr * cos + _rope_partner(qr) * sin
        q_ref[:, lo + LANE:lo + 2 * LANE] = qr.astype(BF16)
        k_ref[:, lo:lo + LANE] = kn[:, hd * LANE:(hd + 1) * LANE].astype(BF16)
        k_ref[:, lo + LANE:lo + 2 * LANE] = kpe_b


def _mla_proj(x, x_row0, mods, layer, seg, g1, w, rope_tabs, cache_slot=None, pending=None):
    n, d = x.shape
    rope = rope_tabs is not None
    rows = seg.rows
    tm = seg.tile(PROJ_ROWS)
    x_tile0 = x_row0 // tm
    hq = MLA_HEADS * 2 * LANE
    const = lambda i: (0, 0)
    aliases = {}
    in_specs = [
        pl.BlockSpec((tm, d), lambda i: (x_tile0 + i, 0)),
        _mod_spec(layer, seg, d, tm),
        pl.BlockSpec((1, d), const),
        pl.BlockSpec(w["w_in"].shape, const),
        pl.BlockSpec((1, MLA_Q_LORA), const),
        pl.BlockSpec((1, MLA_KV_LORA), const),
        pl.BlockSpec(w["w_q"].shape, const),
        pl.BlockSpec(w["w_kn"].shape, const),
        pl.BlockSpec(w["w_v"].shape, const),
    ]
    args = [x, mods, g1, w["w_in"], w["q_g"], w["kv_g"], w["w_q"], w["w_kn"], w["w_v"]]
    out_specs = [pl.BlockSpec((tm, hq), lambda i: (i, 0))] * 3
    out_shape = [jax.ShapeDtypeStruct((rows, hq), BF16)] * 3
    if rope:
        tab = pl.BlockSpec((tm, LANE), lambda i: (i % (seg.seq // tm), 0))
        in_specs += [tab, tab]
        args += list(rope_tabs)
    else:
        cw = MLA_KV_LORA + MLA_ROPE
        prev, slot, n_slots = cache_slot
        assert tm % seg.seq == 0
        out_specs.append(pl.BlockSpec((tm // seg.seq, None, seg.seq, cw), lambda i: (i, slot, 0, 0)))
        out_shape.append(jax.ShapeDtypeStruct((seg.batch, n_slots, seg.seq, cw), F32))
        in_specs.append(pl.BlockSpec(memory_space=pl.ANY))
        args.append(prev)
        aliases = {len(args) - 1: 3}
    if pending is not None:
        p_specs, p_args, xo_spec, xo_shape = _pending_io(pending, seg, tm, d, n)
        in_specs, args = p_specs + in_specs, p_args + args
        out_specs.append(xo_spec)
        out_shape.append(xo_shape)
        aliases = {k + len(p_args): v for k, v in aliases.items()}
        aliases[len(p_args)] = len(out_shape) - 1
    return pl.pallas_call(
        functools.partial(_mla_proj_kernel, rope=rope, pending=pending is not None),
        grid=(rows // tm,),
        in_specs=in_specs,
        out_specs=out_specs,
        out_shape=out_shape,
        input_output_aliases=aliases,
        compiler_params=_cparams("parallel"),
        name="mla_proj_lat" if rope else "mla_proj_ctx",
    )(*args)


def _cache_kv_kernel(c_ref, wkn_ref, wv_ref, k_ref, v_ref):
    c = c_ref[...]
    ckv = c[:, :MLA_KV_LORA].astype(BF16)
    kpe_b = c[:, MLA_KV_LORA:].astype(BF16)
    kn = _dot(ckv, wkn_ref[...])
    _store_values(v_ref, _dot(ckv, wv_ref[...]))
    for hd in range(MLA_HEADS):
        lo = hd * 2 * LANE
        k_ref[:, lo:lo + LANE] = kn[:, hd * LANE:(hd + 1) * LANE].astype(BF16)
        k_ref[:, lo + LANE:lo + 2 * LANE] = kpe_b


def _cache_kv(cache_pad, w):
    rows, cw = cache_pad.shape
    hq = MLA_HEADS * 2 * LANE
    const = lambda i: (0, 0)
    tm = min(PROJ_ROWS, rows)
    assert rows % tm == 0
    return pl.pallas_call(
        _cache_kv_kernel,
        grid=(rows // tm,),
        in_specs=[
            pl.BlockSpec((tm, cw), lambda i: (i, 0)),
            pl.BlockSpec(w["w_kn"].shape, const),
            pl.BlockSpec(w["w_v"].shape, const),
        ],
        out_specs=[pl.BlockSpec((tm, hq), lambda i: (i, 0))] * 2,
        out_shape=[jax.ShapeDtypeStruct((rows, hq), BF16)] * 2,
        compiler_params=_cparams("parallel"),
        name="mla_cache_kv",
    )(cache_pad, w["w_kn"], w["w_v"])


def _attn_kernel(*refs, n_parts, n_seq):
    q_ref = refs[0]
    kv_refs = refs[1:1 + 2 * n_parts]
    wo_ref, x_ref, m_ref = refs[1 + 2 * n_parts:4 + 2 * n_parts]
    o_ref, acc_ref = refs[-2:]
    tq = q_ref.shape[0] // n_seq
    for sq, hd in [(sq, hd) for sq in range(n_seq) for hd in range(MLA_HEADS)]:
        rows = slice(sq * tq, (sq + 1) * tq)
        kcol = slice(hd * 2 * LANE, (hd + 1) * 2 * LANE)
        keys = [slice(sq * (r.shape[0] // n_seq), (sq + 1) * (r.shape[0] // n_seq)) for r in kv_refs[::2]]
        scores = [_dot_nt(q_ref[rows, kcol], kv_refs[2 * p][keys[p], kcol]) for p in range(n_parts)]
        mx = scores[0].max(axis=-1, keepdims=True)
        for s in scores[1:]:
            mx = jnp.maximum(mx, s.max(axis=-1, keepdims=True))
        out = None
        for p, s in enumerate(scores):
            e = jnp.exp((s - mx).astype(BF16))
            pv = _dot(e, kv_refs[2 * p + 1][keys[p], kcol])
            out = pv if out is None else out + pv
        acc_ref[rows, hd * MLA_V:(hd + 1) * MLA_V] = (out[:, :MLA_V] / out[:, LANE:LANE + MLA_V]).astype(BF16)
    y = _dot(acc_ref[...], wo_ref[...])
    o_ref[...] = x_ref[...] + m_ref[2] * y


def _attention(x, x_row0, n, dest, mods, layer, seg, q, kv_parts, w_o):
    d = x.shape[1]
    hq = MLA_HEADS * 2 * LANE
    hv = MLA_HEADS * MLA_V
    tq = seg.seq_tile(ATTN_ROWS)
    tps = seg.seq // tq
    n_seq = ATTN_SHORT_SEQS if (tps == 1 and not seg.per_batch_mod and seg.batch % ATTN_SHORT_SEQS == 0) else 1
    tq *= n_seq
    x_tile0, out_tile0 = x_row0 // tq, seg.row0 // tq
    in_specs = [pl.BlockSpec((tq, hq), lambda b, i: (b * tps + i, 0))]
    args = [q]
    for k, v, rows in kv_parts:
        mode = dict(pipeline_mode=pl.Buffered(1)) if tps > 1 else {}
        in_specs += [pl.BlockSpec((n_seq * rows, hq), lambda b, i: (b, 0)),
                     pl.BlockSpec((n_seq * rows, hq), lambda b, i: (b, 0), **mode)]
        args += [k, v]
    in_specs += [
        pl.BlockSpec(w_o.shape, lambda b, i: (0, 0)),
        pl.BlockSpec((tq, d), lambda b, i: (x_tile0 + b * tps + i, 0)),
        _mod_spec(layer, seg, d, tq // n_seq, tile_of=lambda b, i: (b * tps + i) * n_seq),
    ]
    args += [w_o, x, mods]
    if isinstance(dest, str):
        assert dest == "inplace"
        aliases = {len(args) - 2: 0}
    else:
        in_specs.append(pl.BlockSpec(memory_space=pl.ANY))
        args.append(dest)
        aliases = {len(args) - 1: 0}
    return pl.pallas_call(
        functools.partial(_attn_kernel, n_parts=len(kv_parts), n_seq=n_seq),
        grid=(seg.batch // n_seq, tps),
        in_specs=in_specs,
        out_specs=pl.BlockSpec((tq, d), lambda b, i: (out_tile0 + b * tps + i, 0)),
        out_shape=jax.ShapeDtypeStruct((n, d), F32),
        scratch_shapes=[pltpu.VMEM((tq, hv), BF16)],
        input_output_aliases=aliases,
        compiler_params=_cparams("parallel", "arbitrary"),
        name="mla_attention",
    )(*args)


RET_COL = 1024


RET_ROWS = 1024


def _ret_proj_kernel(*refs, kinds, dk, pending):
    refs, resolve = _take_pending(refs, pending)
    rotary = "rotary" in kinds
    if rotary:
        x_ref, m_ref, g_ref, w_ref, cos_ref, sin_ref, z_ref = refs
        cos, sin = cos_ref[...], sin_ref[...]
    else:
        x_ref, m_ref, g_ref, w_ref, z_ref = refs
    h = _modulate(resolve(x_ref[...]), g_ref[...], m_ref[0], m_ref[1]).astype(BF16)
    half = dk // 2
    for j, kind in enumerate(kinds):
        c0 = j * RET_COL
        acc = _dot(h, w_ref[:, c0:c0 + RET_COL])
        if kind == "rotary":
            for hd in range(RET_COL // dk):
                lo = hd * dk
                x1, x2 = acc[:, lo:lo + half], acc[:, lo + half:lo + dk]
                z_ref[:, c0 + lo:c0 + lo + half] = (x1 * cos - x2 * sin).astype(BF16)
                z_ref[:, c0 + lo + half:c0 + lo + dk] = (x1 * sin + x2 * cos).astype(BF16)
        elif kind == "silu":
            z_ref[:, c0:c0 + RET_COL] = (acc * _sigmoid(acc)).astype(BF16)
        else:
            z_ref[:, c0:c0 + RET_COL] = acc.astype(BF16)


def _ret_proj(x, mods, layer, seg, g1, w_in, group, kinds, rot_tabs, dk, pending=None):
    n, d = x.shape
    rows = seg.rows
    tm = seg.tile(RET_ROWS)
    ncol = len(kinds) * RET_COL
    tps = max(seg.seq // tm, 1)
    in_specs = [
        pl.BlockSpec((tm, d), lambda i: (seg.row0 // tm + i, 0)),
        _mod_spec(layer, seg, d, tm),
        pl.BlockSpec((1, d), lambda i: (0, 0)),
        pl.BlockSpec((d, ncol), lambda i: (0, group), pipeline_mode=pl.Buffered(1)),
    ]
    args = [x, mods, g1, w_in]
    if "rotary" in kinds:
        tab = pl.BlockSpec((tm, dk // 2), lambda i: (i % tps, 0))
        in_specs += [tab, tab]
        args += list(rot_tabs)
    out_specs = [pl.BlockSpec((tm, ncol), lambda i: (i, 0))]
    out_shape = [jax.ShapeDtypeStruct((rows, ncol), BF16)]
    aliases = {}
    if pending is not None:
        p_specs, p_args, xo_spec, xo_shape = _pending_io(pending, seg, tm, d, n)
        in_specs, args = p_specs + in_specs, p_args + args
        out_specs.append(xo_spec)
        out_shape.append(xo_shape)
        aliases = {len(p_args): 1}
    res = pl.pallas_call(
        functools.partial(_ret_proj_kernel, kinds=kinds, dk=dk, pending=pending is not None),
        grid=(rows // tm,),
        in_specs=in_specs,
        out_specs=out_specs,
        out_shape=out_shape,
        input_output_aliases=aliases,
        compiler_params=_cparams("parallel"),
        name="ret_proj_" + kinds[0],
    )(*args)
    return res if pending is not None else res[0]


def _log_sigmoid(x):
    return jnp.minimum(x, 0.0) - jnp.log(1.0 + jnp.exp(-jnp.abs(x)))


def _ret_scan_kernel(*refs, has_s0, emit_state, n_chunks, heads, fuse_out):
    refs = list(refs)
    lf_ref, lb_ref, q_ref, k_ref, v_ref, gf_ref, gb_ref = refs[:7]
    pos = 7
    s0_ref = None
    if has_s0:
        s0_ref = refs[pos]
        pos += 1
    if fuse_out:
        wo_ref, x_ref, m_ref, o_ref = refs[pos:pos + 4]
        pos += 4
    else:
        y_ref = refs[pos]
        pos += 1
    sout_ref = None
    if emit_state:
        sout_ref = refs[pos]
        pos += 1
    if fuse_out:
        s_ref, yf_ref, y_ref = refs[pos:]
    else:
        s_ref, yf_ref = refs[pos:]
    c = RET_CHUNK
    dk, dv = s_ref.shape
    ii = lax.broadcasted_iota(jnp.int32, (c, c), 0).astype(F32)
    jj = lax.broadcasted_iota(jnp.int32, (c, c), 1).astype(F32)
    idx = lax.broadcasted_iota(jnp.int32, (c, 1), 0).astype(F32)

    for hd, direction in [(hd, direction) for hd in range(heads) for direction in range(2)]:
        fwd = direction == 0
        kcol, vcol = slice(hd * dk, (hd + 1) * dk), slice(hd * dv, (hd + 1) * dv)
        lg = _log_sigmoid((lf_ref if fwd else lb_ref)[hd])
        rel = (ii - jj) if fwd else (jj - ii)
        keep = rel >= 0
        decay_in = jnp.where(keep, jnp.exp(jnp.where(keep, rel, 0.0) * lg), 0.0)
        decay_q = jnp.exp(((idx + 1.0) if fwd else (c - idx)) * lg)
        decay_k = jnp.exp(((c - 1.0 - idx) if fwd else idx) * lg)
        decay_c = jnp.exp(c * lg)
        g_ref = gf_ref if fwd else gb_ref

        def chunk(cc, state, fwd=fwd, decay_in=decay_in, decay_q=decay_q, decay_k=decay_k, decay_c=decay_c,
                  g_ref=g_ref, kcol=kcol, vcol=vcol):
            r0 = cc * c if isinstance(cc, int) else pl.multiple_of(cc * c, c)
            qc = q_ref[pl.ds(r0, c), kcol]
            kc = k_ref[pl.ds(r0, c), kcol]
            vc = v_ref[pl.ds(r0, c), vcol]
            sc = _dot_nt(qc, kc) * decay_in
            out = _dot(sc.astype(BF16), vc)
            kd_t = (kc.astype(F32) * decay_k).T.astype(BF16)
            new_s = _dot(kd_t, vc)
            if state is not None:
                out = out + decay_q * _dot(qc, state.astype(BF16))
                new_s = decay_c * state + new_s
            s_ref[...] = new_s
            o = _rms(out) * g_ref[pl.ds(r0, c), vcol].astype(F32)
            if fwd:
                yf_ref[pl.ds(r0, c), :] = o
            else:
                y_ref[pl.ds(r0, c), vcol] = (yf_ref[pl.ds(r0, c), :] + o).astype(BF16)

        chunk(0 if fwd else n_chunks - 1, s0_ref[direction, hd] if has_s0 else None)

        def step(ci, carry, fwd=fwd, chunk=chunk):
            chunk(ci if fwd else n_chunks - 1 - ci, s_ref[...])
            return carry

        lax.fori_loop(1, n_chunks, step, 0, unroll=True)
        if emit_state:
            sout_ref[direction, hd] = s_ref[...]
    if fuse_out:
        o_ref[...] = x_ref[...] + m_ref[2] * _dot(y_ref[...], wo_ref[...])


def _ret_scan(qkv, g, seg, logit_f, logit_b, s0, emit_state, dk, dv, heads, out_proj=None):
    rows = seg.batch * seg.seq
    t = seg.seq
    hh = RET_HEADS
    groups = hh // heads
    fuse_out = out_proj is not None
    assert not fuse_out or groups == 1
    v0 = 2 * hh * dk // (heads * dv)
    assert groups * heads == hh and v0 * heads * dv == 2 * hh * dk
    in_specs = [
        pl.BlockSpec((heads, 1, 1), lambda b, h: (h, 0, 0)),
        pl.BlockSpec((heads, 1, 1), lambda b, h: (h, 0, 0)),
        pl.BlockSpec((t, heads * dk), lambda b, h: (b, h)),
        pl.BlockSpec((t, heads * dk), lambda b, h: (b, groups + h)),
        pl.BlockSpec((t, heads * dv), lambda b, h: (b, v0 + h)),
        pl.BlockSpec((t, heads * dv), lambda b, h: (b, h)),
        pl.BlockSpec((t, heads * dv), lambda b, h: (b, groups + h)),
    ]
    args = [logit_f.reshape(hh, 1, 1), logit_b.reshape(hh, 1, 1), qkv, qkv, qkv, g, g]
    state_spec = pl.BlockSpec((None, 2, heads, dk, dv), lambda b, h: (b, 0, h, 0, 0))
    if s0 is not None:
        in_specs.append(state_spec)
        args.append(s0)
    scratch = [pltpu.VMEM((dk, dv), F32), pltpu.VMEM((t, dv), F32)]
    aliases = {}
    if fuse_out:
        x, mods, layer, w_o = out_proj
        n, d = x.shape
        x_spec = pl.BlockSpec((t, d), lambda b, h: (seg.row0 // t + b, 0))
        in_specs += [pl.BlockSpec(w_o.shape, lambda b, h: (0, 0)), x_spec,
                     _mod_spec(layer, seg, d, t, tile_of=lambda b, h: b)]
        args += [w_o, x, mods]
        aliases = {len(args) - 2: 0}
        out_specs, out_shape = [x_spec], [jax.ShapeDtypeStruct((n, d), F32)]
        scratch.append(pltpu.VMEM((t, hh * dv), BF16))
    else:
        out_specs = [pl.BlockSpec((t, heads * dv), lambda b, h: (b, h))]
        out_shape = [jax.ShapeDtypeStruct((rows, hh * dv), BF16)]
    if emit_state:
        out_specs.append(state_spec)
        out_shape.append(jax.ShapeDtypeStruct((seg.batch, 2, hh, dk, dv), F32))
    res = pl.pallas_call(
        functools.partial(_ret_scan_kernel, has_s0=s0 is not None, emit_state=emit_state,
                          n_chunks=t // RET_CHUNK, heads=heads, fuse_out=fuse_out),
        grid=(seg.batch, groups),
        in_specs=in_specs,
        out_specs=out_specs,
        out_shape=out_shape,
        scratch_shapes=scratch,
        input_output_aliases=aliases,
        compiler_params=_cparams("parallel", "parallel"),
        name="ret_scan",
    )(*args)
    return res if emit_state else (res[0], None)


def _mm_res_kernel(a_ref, w_ref, x_ref, m_ref, o_ref):
    o_ref[...] = x_ref[...] + m_ref[2] * _dot(a_ref[...], w_ref[...])


def _matmul_residual(x, mods, layer, seg, a, w):
    n, d = x.shape
    tm = seg.tile(RET_ROWS)
    x_spec = pl.BlockSpec((tm, d), lambda i: (seg.row0 // tm + i, 0))
    return pl.pallas_call(
        _mm_res_kernel,
        grid=(seg.rows // tm,),
        in_specs=[
            pl.BlockSpec((tm, a.shape[1]), lambda i: (i, 0)),
            pl.BlockSpec(w.shape, lambda i: (0, 0)),
            x_spec,
            _mod_spec(layer, seg, d, tm),
        ],
        out_specs=x_spec,
        out_shape=jax.ShapeDtypeStruct((n, d), F32),
        input_output_aliases={2: 0},
        compiler_params=_cparams("parallel"),
        name="matmul_residual",
    )(a, w, x, mods)


def _fnet_a_kernel(*refs, gd, pending):
    refs, resolve = _take_pending(refs, pending)
    x_ref, m_ref, g_ref, cs_ref, ac_ref, as_ref = refs
    h = _modulate(resolve(x_ref[...]), g_ref[...], m_ref[0], m_ref[1]).astype(BF16)
    cs = cs_ref[...]
    for g in range(FNET_GROUPS):
        a = _dot(h[:, g * gd:(g + 1) * gd], cs)
        ac_ref[:, g * gd:(g + 1) * gd] = a[:, :gd].astype(BF16)
        as_ref[:, g * gd:(g + 1) * gd] = a[:, gd:].astype(BF16)


def _fnet_a(x, mods, layer, seg, g1, cs, pending=None):
    n, d = x.shape
    rows = seg.rows
    tm = seg.tile(PROJ_ROWS)
    out = pl.BlockSpec((tm, d), lambda i: (i, 0))
    in_specs = [
        pl.BlockSpec((tm, d), lambda i: (seg.row0 // tm + i, 0)),
        _mod_spec(layer, seg, d, tm),
        pl.BlockSpec((1, d), lambda i: (0, 0)),
        pl.BlockSpec(cs.shape, lambda i: (0, 0)),
    ]
    args = [x, mods, g1, cs]
    out_specs, out_shape, aliases = [out, out], [jax.ShapeDtypeStruct((rows, d), BF16)] * 2, {}
    if pending is not None:
        p_specs, p_args, xo_spec, xo_shape = _pending_io(pending, seg, tm, d, n)
        in_specs, args = p_specs + in_specs, p_args + args
        out_specs.append(xo_spec)
        out_shape.append(xo_shape)
        aliases = {len(p_args): 2}
    return pl.pallas_call(
        functools.partial(_fnet_a_kernel, gd=d // FNET_GROUPS, pending=pending is not None),
        grid=(rows // tm,),
        in_specs=in_specs,
        out_specs=out_specs,
        out_shape=out_shape,
        input_output_aliases=aliases,
        compiler_params=_cparams("parallel"),
        name="fnet_channel_dft",
    )(*args)


def _fnet_b_kernel(ct_ref, st_ref, ac_ref, as_ref, w_ref, x_ref, m_ref, o_ref, *, norm):
    f = (_dot(ct_ref[...], ac_ref[...]) - _dot(st_ref[...], as_ref[...])) * norm
    o_ref[...] = x_ref[...] + m_ref[2] * _dot(f.astype(BF16), w_ref[...])


def _fnet_b(x, mods, layer, seg, ac, as_, ct, st, w, norm):
    n, d = x.shape
    t = seg.seq
    tq = seg.seq_tile(FNET_ROWS)
    tps = t // tq
    x_spec = pl.BlockSpec((tq, d), lambda b, i: (seg.row0 // tq + b * tps + i, 0))
    tab = pl.BlockSpec((tq, t), lambda b, i: (i, 0))
    seq = pl.BlockSpec((t, d), lambda b, i: (b, 0))
    return pl.pallas_call(
        functools.partial(_fnet_b_kernel, norm=norm),
        grid=(seg.batch, tps),
        in_specs=[tab, tab, seq, seq, pl.BlockSpec(w.shape, lambda b, i: (0, 0)), x_spec,
                  _mod_spec(layer, seg, d, tq, tile_of=lambda b, i: b * tps + i)],
        out_specs=x_spec,
        out_shape=jax.ShapeDtypeStruct((n, d), F32),
        input_output_aliases={5: 0},
        compiler_params=_cparams("parallel", "arbitrary"),
        name="fnet_position_dft",
    )(ct, st, ac, as_, w, x, mods)


def _fnet_short_kernel(*refs, gd, norm, n_seq, pending):
    if pending:
        y_ref, wr_ref, pm_ref, *refs = refs
    x_ref, m_ref, g_ref, cs_ref, ct_ref, st_ref, w_ref, o_ref, ac_ref, as_ref = refs
    x = x_ref[...]
    if pending:
        x = x + _moe_residual(y_ref, wr_ref, pm_ref[5])
    h = _modulate(x, g_ref[...], m_ref[0], m_ref[1]).astype(BF16)
    cs = cs_ref[...]
    for g in range(FNET_GROUPS):
        a = _dot(h[:, g * gd:(g + 1) * gd], cs)
        ac_ref[:, g * gd:(g + 1) * gd] = a[:, :gd].astype(BF16)
        as_ref[:, g * gd:(g + 1) * gd] = a[:, gd:].astype(BF16)
    t = x.shape[0] // n_seq
    ct, st = ct_ref[...], st_ref[...]
    f = [_dot(ct, ac_ref[s * t:(s + 1) * t, :]) - _dot(st, as_ref[s * t:(s + 1) * t, :]) for s in range(n_seq)]
    f = (f[0] if n_seq == 1 else jnp.concatenate(f, axis=0)) * norm
    o_ref[...] = x + m_ref[2] * _dot(f.astype(BF16), w_ref[...])


def _fnet_short(x, mods, layer, seg, g1, cs, ct, st, w, norm, pending=None):
    n, d = x.shape
    t = seg.seq
    n_seq = ATTN_SHORT_SEQS if (not seg.per_batch_mod and seg.batch % ATTN_SHORT_SEQS == 0) else 1
    rows = n_seq * t
    t0 = seg.row0 // rows
    const = lambda i: (0, 0)
    x_spec = pl.BlockSpec((rows, d), lambda i: (t0 + i, 0))
    in_specs = [x_spec, _mod_spec(layer, seg, d, t, tile_of=lambda i: i * n_seq), pl.BlockSpec((1, d), const),
                pl.BlockSpec(cs.shape, const), pl.BlockSpec((t, t), const), pl.BlockSpec((t, t), const),
                pl.BlockSpec(w.shape, const)]
    args = [x, mods, g1, cs, ct, st, w]
    if pending is not None:
        yg_by_seg, wcol, pmods, player = pending
        in_specs = [pl.BlockSpec((2, rows, d // 2), lambda i: (0, i, 0)),
                    pl.BlockSpec((rows, 8), lambda i: (t0 + i, 0)),
                    _mod_spec(player, seg, d, t, tile_of=lambda i: i * n_seq)] + in_specs
        args = [yg_by_seg[seg], wcol, pmods] + args
    return pl.pallas_call(
        functools.partial(_fnet_short_kernel, gd=d // FNET_GROUPS, norm=norm, n_seq=n_seq,
                          pending=pending is not None),
        grid=(seg.rows // rows,),
        in_specs=in_specs,
        out_specs=x_spec,
        out_shape=jax.ShapeDtypeStruct((n, d), F32),
        scratch_shapes=[pltpu.VMEM((rows, d), BF16), pltpu.VMEM((rows, d), BF16)],
        input_output_aliases={len(args) - 7: 0},
        compiler_params=_cparams("parallel"),
        name="fnet_short",
    )(*args)


def _pack_halves(a):
    w = a.shape[1] // 2
    bits = lambda v: lax.bitcast_convert_type(v.astype(BF16).astype(F32), jnp.uint32)
    return (bits(a[:, :w]) >> 16) | (bits(a[:, w:]) & jnp.uint32(0xFFFF0000))


def _router_kernel(x_ref, m_ref, g_ref, rwhi_ref, rwlo_ref, rb_ref, h_ref, idx_ref, rank_ref, wcol_ref, cnt_ref,
                   run_ref, tri_ref):
    step = pl.program_id(0)

    @pl.when(step == 0)
    def _():
        run_ref[...] = jnp.zeros_like(run_ref)
        tt = tri_ref.shape[0]
        earlier = lax.broadcasted_iota(jnp.int32, (tt, tt), 0) < lax.broadcasted_iota(jnp.int32, (tt, tt), 1)
        tri_ref[...] = jnp.where(earlier, 1.0, 0.0).astype(BF16)

    parts = []
    for r0 in range(0, x_ref.shape[0], ROUTER_SUB_ROWS):
        rows = slice(r0, r0 + ROUTER_SUB_ROWS)
        h = _modulate(x_ref[rows, :], g_ref[...], m_ref[3], m_ref[4])
        h_ref[rows, :] = _pack_halves(h)
        h_hi = h.astype(BF16)
        h_lo = (h - h_hi.astype(F32)).astype(BF16)
        parts.append(_dot_nt(rwhi_ref[...], h_hi) + (_dot_nt(rwhi_ref[...], h_lo) + _dot_nt(rwlo_ref[...], h_hi)))
    logits = jnp.concatenate(parts, axis=1)
    sc = _sigmoid(logits)
    gr = sc + rb_ref[...]
    gp = EXPERTS_PER_GROUP
    row = lambda a, e: a[e:e + 1, :]
    best_g = None
    for g in range(N_EXPERT_GROUPS):
        vals = [row(gr, g * gp + i) for i in range(gp)]
        gs = None
        for i in range(gp):
            for j in range(i + 1, gp):
                pair = vals[i] + vals[j]
                gs = pair if gs is None else jnp.maximum(gs, pair)
        if best_g is None:
            best_g, best_v = jnp.zeros(gs.shape, jnp.int32), gs
        else:
            better = gs > best_v
            best_g = jnp.where(better, g, best_g)
            best_v = jnp.where(better, gs, best_v)
    sel, raw = [], []
    for i in range(gp):
        s_i, r_i = row(gr, i), row(sc, i)
        for g in range(1, N_EXPERT_GROUPS):
            s_i = jnp.where(best_g == g, row(gr, g * gp + i), s_i)
            r_i = jnp.where(best_g == g, row(sc, g * gp + i), r_i)
        sel.append(s_i)
        raw.append(r_i)

    def argmax_first(vals, raws):
        bi, bv, br = jnp.zeros(vals[0].shape, jnp.int32), vals[0], raws[0]
        for i in range(1, len(vals)):
            better = vals[i] > bv
            bi = jnp.where(better, i, bi)
            bv = jnp.where(better, vals[i], bv)
            br = jnp.where(better, raws[i], br)
        return bi, br

    i1, w1 = argmax_first(sel, raw)
    masked = [jnp.where(i1 == i, -jnp.inf, sel[i]) for i in range(gp)]
    i2, w2 = argmax_first(masked, raw)
    tot = w1 + w2
    e1 = best_g * gp + i1
    e2 = best_g * gp + i2
    idx_ref[0:1, :] = e1
    idx_ref[1:2, :] = e2
    t = e1.shape[1]
    sub = lax.broadcasted_iota(jnp.int32, (8, t), 0)
    w8 = jnp.where(sub == 0, w1 / tot, jnp.where(sub == 1, w2 / tot, 0.0))
    wcol_ref[...] = w8.T
    eio = lax.broadcasted_iota(jnp.int32, (N_EXPERTS, t), 0)
    oh1, oh2 = eio == e1, eio == e2
    oh = jnp.where(oh1, 1.0, jnp.where(oh2, 1.0, 0.0))
    local = _dot(oh.astype(BF16), tri_ref[...])
    rank = local + run_ref[:, 0:1]
    rank_ref[0:1, :] = jnp.sum(jnp.where(oh1, rank, 0.0), axis=0, keepdims=True).astype(jnp.int32)
    rank_ref[1:2, :] = jnp.sum(jnp.where(oh2, rank, 0.0), axis=0, keepdims=True).astype(jnp.int32)
    run_ref[...] = run_ref[...] + jnp.sum(oh, axis=1, keepdims=True)
    cnt_ref[...] = run_ref[...]


def _wide_mod_row(segs, tm):
    ctx, lat = segs
    ctx_tiles = ctx.batch * ctx.seq // tm
    assert ctx_tiles * tm == ctx.batch * ctx.seq and lat.seq % tm == 0
    return lambda i: jnp.where(i < ctx_tiles, ctx.mod0, lat.mod0 + (i - ctx_tiles) // (lat.seq // tm))


def _router(x, mods, layer, segs, g2, rw_hi, rw_lo, rb):
    n, d = x.shape
    tm = WIDE_TILE
    mod_row = _wide_mod_row(segs, tm)
    return pl.pallas_call(
        _router_kernel,
        grid=(n // tm,),
        in_specs=[
            pl.BlockSpec((tm, d), lambda i: (i, 0)),
            pl.BlockSpec((None, None, 6, 1, d), lambda i: (layer, mod_row(i), 0, 0, 0)),
            pl.BlockSpec((1, d), lambda i: (0, 0)),
            pl.BlockSpec(rw_hi.shape, lambda i: (0, 0)),
            pl.BlockSpec(rw_lo.shape, lambda i: (0, 0)),
            pl.BlockSpec(rb.shape, lambda i: (0, 0)),
        ],
        out_specs=[
            pl.BlockSpec((tm, d // 2), lambda i: (i, 0)),
            pl.BlockSpec((2, tm), lambda i: (0, i)),
            pl.BlockSpec((2, tm), lambda i: (0, i)),
            pl.BlockSpec((tm, 8), lambda i: (i, 0)),
            pl.BlockSpec((N_EXPERTS, LANE), lambda i: (0, 0)),
        ],
        out_shape=[
            jax.ShapeDtypeStruct((n, d // 2), jnp.uint32),
            jax.ShapeDtypeStruct((2, n), jnp.int32),
            jax.ShapeDtypeStruct((2, n), jnp.int32),
            jax.ShapeDtypeStruct((n, 8), F32),
            jax.ShapeDtypeStruct((N_EXPERTS, LANE), F32),
        ],
        scratch_shapes=[pltpu.VMEM((N_EXPERTS, LANE), F32), pltpu.VMEM((tm, tm), BF16)],
        compiler_params=_cparams("arbitrary"),
        name="moe_router",
    )(x, mods, g2, rw_hi, rw_lo, rb)


def _expert_kernel(be_ref, br_ref, bs_ref, xs_ref, wg_ref, wu_ref, wd_ref, y_ref, wg_b, wu_b, wd_b):
    i = pl.program_id(0)
    prev = be_ref[jnp.maximum(i - 1, 0)]
    valid = br_ref[i]
    tail_rows = MOE_TAIL_ROWS

    @pl.when(jnp.logical_or(i == 0, be_ref[i] != prev))
    def _():
        wg_b[...] = wg_ref[...].astype(BF16)
        wu_b[...] = wu_ref[...].astype(BF16)
        wd_b[...] = wd_ref[...].astype(BF16)

    def ffn(r0, nrows):
        lo, hi = _unpack_halves(xs_ref[r0:r0 + nrows, :])
        xb = jnp.concatenate([lo.astype(BF16), hi.astype(BF16)], axis=1)
        gate = _dot(xb, wg_b[...])
        hid = (gate * _sigmoid(gate)) * _dot(xb, wu_b[...])
        y_ref[r0:r0 + nrows, :] = _pack_halves(_dot(hid.astype(BF16), wd_b[...]))

    @pl.when(valid > tail_rows)
    def _():
        ffn(0, y_ref.shape[0])

    @pl.when(jnp.logical_and(valid > 0, valid <= tail_rows))
    def _():
        ffn(0, tail_rows)
        y_ref[tail_rows:, :] = jnp.zeros((y_ref.shape[0] - tail_rows, y_ref.shape[1]), y_ref.dtype)


def _experts(xs, block_e, block_rows, block_src, w_gate, w_up, w_down, layer):
    rows, half = xs.shape
    d = 2 * half
    de = w_gate.shape[-1]
    n_blocks = rows // MOE_ROWS
    grid_spec = pltpu.PrefetchScalarGridSpec(
        num_scalar_prefetch=3,
        grid=(n_blocks,),
        in_specs=[
            pl.BlockSpec((MOE_ROWS, half), lambda i, be, br, bs: (bs[i], 0)),
            pl.BlockSpec((None, None, d, de), lambda i, be, br, bs: (layer, be[i], 0, 0)),
            pl.BlockSpec((None, None, d, de), lambda i, be, br, bs: (layer, be[i], 0, 0)),
            pl.BlockSpec((None, None, de, d), lambda i, be, br, bs: (layer, be[i], 0, 0)),
        ],
        out_specs=pl.BlockSpec((MOE_ROWS, half), lambda i, be, br, bs: (bs[i], 0)),
        scratch_shapes=[pltpu.VMEM((d, de), BF16), pltpu.VMEM((d, de), BF16), pltpu.VMEM((de, d), BF16)],
    )
    return pl.pallas_call(
        _expert_kernel,
        grid_spec=grid_spec,
        out_shape=jax.ShapeDtypeStruct((rows, half), jnp.uint32),
        compiler_params=_cparams("arbitrary"),
        name="moe_experts",
    )(block_e, block_rows, block_src, xs, w_gate, w_up, w_down)


def _combine_kernel(x_ref, m_ref, y_ref, w_ref, fg_ref, o_ref):
    x = x_ref[...] + _moe_residual(y_ref, w_ref, m_ref[5])
    o_ref[...] = _rms(x) * fg_ref[...]


def _combine(x, mods, layer, segs, seg, yg, wcol, final_g):
    n, d = x.shape
    tm = WIDE_TILE
    mod_row = _wide_mod_row(segs, tm)
    t0 = seg.row0 // tm
    steps = seg.rows // tm
    in_specs = [pl.BlockSpec((tm, d), lambda i: (t0 + i, 0)),
                pl.BlockSpec((None, None, 6, 1, d), lambda i: (layer, mod_row(t0 + i), 0, 0, 0)),
                pl.BlockSpec((2, tm, d // 2), lambda i: (0, i, 0)),
                pl.BlockSpec((tm, 8), lambda i: (t0 + i, 0)),
                pl.BlockSpec((1, d), lambda i: (0, 0))]
    return pl.pallas_call(
        _combine_kernel,
        grid=(steps,),
        in_specs=in_specs,
        out_specs=pl.BlockSpec((tm, d), lambda i: (i, 0)),
        out_shape=jax.ShapeDtypeStruct((seg.rows, d), F32),
        compiler_params=_cparams("parallel"),
        name="moe_combine_final",
    )(x, mods, yg, wcol, final_g)


def _sc_mesh():
    return plsc.VectorSubcoreMesh(core_axis_name="c", subcore_axis_name="s")


def _sc_worker_split(n):
    workers = SC_CORES * SC_SUBCORES
    per = n // workers
    assert per * workers == n and per % SC_CHUNK == 0
    return workers, per, per // SC_CHUNK


def _sc_dispatch(h, pos, rows):
    n, w = h.shape
    workers, per, chunks = _sc_worker_split(n)

    @functools.partial(
        pl.kernel, out_type=jax.ShapeDtypeStruct((rows, w), h.dtype), mesh=_sc_mesh(),
        scratch_types=[pltpu.VMEM((2, chunks, SC_CHUNK), jnp.int32), pltpu.VMEM((SC_CHUNK, w), h.dtype)],
        name="moe_dispatch_scatter")
    def scatter_rows(h_hbm, pos_hbm, xs_hbm, idx_v, rows_v):
        wid = lax.axis_index("s") * SC_CORES + lax.axis_index("c")
        pltpu.sync_copy(pos_hbm.at[0, wid], idx_v.at[0])
        pltpu.sync_copy(pos_hbm.at[1, wid], idx_v.at[1])

        @pl.loop(0, chunks)
        def _(c):
            pltpu.sync_copy(h_hbm.at[pl.ds(wid * per + c * SC_CHUNK, SC_CHUNK)], rows_v)
            pltpu.sync_copy(rows_v, xs_hbm.at[idx_v.at[0, c]])
            pltpu.sync_copy(rows_v, xs_hbm.at[idx_v.at[1, c]])

    return scatter_rows(h, pos.reshape(2, workers, chunks, SC_CHUNK))


def _sc_gather2(ys, pos):
    _, w = ys.shape
    n = pos.shape[1]
    workers, per, chunks = _sc_worker_split(n)

    @functools.partial(
        pl.kernel, out_type=jax.ShapeDtypeStruct((2, n, w), ys.dtype), mesh=_sc_mesh(),
        scratch_types=[pltpu.VMEM((2, chunks, SC_CHUNK), jnp.int32), pltpu.VMEM((SC_CHUNK, w), ys.dtype),
                       pltpu.SemaphoreType.DMA],
        name="moe_combine_gather")
    def gather_rows(ys_hbm, pos_hbm, out_hbm, idx_v, rows_v, sem):
        wid = lax.axis_index("s") * SC_CORES + lax.axis_index("c")
        pltpu.sync_copy(pos_hbm.at[0, wid], idx_v.at[0])
        pltpu.sync_copy(pos_hbm.at[1, wid], idx_v.at[1])

        @pl.loop(0, chunks)
        def _(c):
            for k in range(2):
                pltpu.async_copy(ys_hbm.at[idx_v.at[k, c]], rows_v, sem).wait()
                pltpu.sync_copy(rows_v, out_hbm.at[k, pl.ds(wid * per + c * SC_CHUNK, SC_CHUNK)])

    return gather_rows(ys, pos.reshape(2, workers, chunks, SC_CHUNK))


def _dispatch_plan(idx, rank, counts):
    n = idx.shape[1]
    padded = (counts + MOE_ROWS - 1) // MOE_ROWS * MOE_ROWS
    pad_end = jnp.cumsum(padded)
    pad_start = pad_end - padded
    experts = jnp.arange(N_EXPERTS, dtype=jnp.int32)
    start_of = jnp.sum(jnp.where(idx[..., None] == experts, pad_start, 0), axis=-1)
    pos = start_of + rank
    n_blocks = 2 * n // MOE_ROWS + N_EXPERTS
    steps = jnp.arange(n_blocks, dtype=jnp.int32)
    last_used = pad_end[-1] // MOE_ROWS - 1
    step = jnp.minimum(steps, last_used)
    block_e = jnp.minimum(jnp.sum(step[:, None] * MOE_ROWS >= pad_end[None, :], axis=1), N_EXPERTS - 1)
    pick = lambda per_expert: jnp.sum(jnp.where(block_e[:, None] == experts, per_expert, 0), axis=-1)
    first, count = pick(pad_start // MOE_ROWS), jnp.maximum(pick(padded // MOE_ROWS), 1)
    block_src = first + (step - first - 1) % count
    block_rows = jnp.clip(pick(pad_start + counts) - block_src * MOE_ROWS, 0, MOE_ROWS)
    block_rows = jnp.where(steps <= last_used, block_rows, 0)
    return (pos, block_e.astype(jnp.int32), block_rows.astype(jnp.int32), block_src.astype(jnp.int32),
            n_blocks * MOE_ROWS)


def _moe(x, mods, layer, segs, g2, rw_hi, rw_lo, rb, w_gate, w_up, w_down, final_g):
    h2p, idx, rank, wcol, cnt = _router(x, mods, layer, segs, g2, rw_hi, rw_lo, rb)
    pos, block_e, block_rows, block_src, rows = _dispatch_plan(idx, rank, cnt[:, 0].astype(jnp.int32))
    xs = _sc_dispatch(h2p, pos, rows)
    ys = _experts(xs, block_e, block_rows, block_src, w_gate, w_up, w_down, layer)
    yg = {seg: _sc_gather2(ys, pos[:, seg.row0:seg.row0 + seg.rows]) for seg in segs}
    if final_g is None:
        return yg, wcol, mods, layer
    return tuple(_combine(x, mods, layer, segs, seg, yg[seg], wcol, final_g) for seg in segs)


def _mla_rope_tables(t):
    axis_dim = MLA_ROPE // 2
    row = np.repeat(np.arange(t // GRID_W), GRID_W).astype(np.float64)
    col = np.tile(np.arange(GRID_W), t // GRID_W).astype(np.float64)
    inv = ROPE_BASE ** (-np.arange(0, axis_dim, 2, dtype=np.float64) / axis_dim)
    ar, ac = row[:, None] * inv[None, :], col[:, None] * inv[None, :]
    ones = np.ones((t, LANE - MLA_ROPE))
    cos = np.concatenate([np.cos(ar), np.cos(ar), np.cos(ac), np.cos(ac), ones], axis=-1)
    sin = np.concatenate([-np.sin(ar), np.sin(ar), -np.sin(ac), np.sin(ac), 0.0 * ones], axis=-1)
    return jnp.asarray(cos, F32), jnp.asarray(sin, F32)


def _ret_rot_tables(t, dk):
    inv = ROPE_BASE ** (-np.linspace(0.0, 1.0, dk // 2))
    ang = np.arange(t, dtype=np.float64)[:, None] * inv[None, :]
    return jnp.asarray(np.cos(ang), F32), jnp.asarray(np.sin(ang), F32)


def _dft_tables(n):
    k = np.arange(n, dtype=np.int64)
    ang = (np.outer(k, k) % n).astype(np.float64) * (2.0 * math.pi / n)
    return jnp.asarray(np.cos(ang), BF16), jnp.asarray(np.sin(ang), BF16)


def _mla_weights(w_in, q_g, kv_g, w_uq, w_ukv, w_o):
    d = w_in.shape[0]
    hd = MLA_NOPE + MLA_ROPE
    perm = _rope_perm()
    w_in_p = jnp.concatenate([w_in, w_in[:, MLA_Q_LORA + MLA_KV_LORA + perm]], axis=1)
    uq = w_uq.reshape(MLA_Q_LORA, MLA_HEADS, hd)
    uq = jnp.concatenate([uq, uq[..., MLA_NOPE + perm]], axis=-1)
    ukv = w_ukv.reshape(MLA_KV_LORA, MLA_HEADS, MLA_NOPE + MLA_V)
    return {
        "w_in": w_in_p.astype(BF16),
        "q_g": q_g.reshape(1, -1) * (MLA_NOPE + MLA_ROPE) ** -0.5,
        "kv_g": kv_g.reshape(1, -1),
        "w_q": uq.reshape(MLA_Q_LORA, MLA_HEADS * 2 * LANE).astype(BF16),
        "w_kn": ukv[..., :MLA_NOPE].reshape(MLA_KV_LORA, MLA_HEADS * MLA_NOPE).astype(BF16),
        "w_v": ukv[..., MLA_NOPE:].reshape(MLA_KV_LORA, MLA_HEADS * MLA_V).astype(BF16),
        "w_o": w_o.astype(BF16),
    }


def kernel(x_prompt, x_sample, cache_mla, state_ret, c, c_ctx, norm1_g, norm2_g, ada_w, ada_b, final_norm_g,
           mla_w_in, mla_q_norm_g, mla_kv_norm_g, mla_w_uq, mla_w_ukv, mla_w_o, ret_w_in, ret_decay_f,
           ret_decay_b, ret_w_o, fnet_w, router_w, router_b, moe_w_gate, moe_w_up, moe_w_down):
    b_ctx, t_ctx, d = x_prompt.shape
    b_lat, t_lat, _ = x_sample.shape
    depth = ada_w.shape[0]
    assert b_lat + 1 <= 8
    n_ctx = b_ctx * t_ctx
    ctx = _Seg(0, b_ctx, t_ctx, 0, False)
    lat = _Seg(n_ctx, b_lat, t_lat, 1, True)
    segs = (ctx, lat)

    n_lat = b_lat * t_lat
    n_mla = mla_w_in.shape[0]
    assert n_mla >= 1
    x = None
    new_cache = jnp.zeros((b_ctx, n_mla, t_ctx, MLA_KV_LORA + MLA_ROPE), F32)
    cond8 = jnp.concatenate([c_ctx[None, :], c, jnp.zeros((8 - 1 - b_lat, d), F32)], axis=0)
    mods = _modulation_all(cond8, ada_w, ada_b).reshape(depth, 8, 6, 1, d)

    rw_t = router_w.T.astype(F32)
    rw_hi = rw_t.astype(BF16)
    rw_lo = (rw_t - rw_hi.astype(F32)).astype(BF16)
    rb = router_b.reshape(N_EXPERTS, 1).astype(F32)
    final_g = final_norm_g.reshape(1, d)
    dk = ret_w_in.shape[2] // (8 * RET_HEADS)
    dv = 2 * dk

    states = []
    pending = None
    counters = [0, 0, 0]
    for layer in range(depth):
        kind = layer % 3
        j = counters[kind]
        counters[kind] += 1
        g1 = norm1_g[layer].reshape(1, d)
        g2 = norm2_g[layer].reshape(1, d)
        if kind == 0:
            w = _mla_weights(mla_w_in[j], mla_q_norm_g[j], mla_kv_norm_g[j], mla_w_uq[j], mla_w_ukv[j],
                             mla_w_o[j])
            if x is None:
                xc, xc0, xl, xl0 = x_prompt.reshape(n_ctx, d), 0, x_sample.reshape(n_lat, d), 0
            else:
                xc, xc0, xl, xl0 = x, ctx.row0, x, lat.row0
            past = cache_mla.shape[2]
            cpad = jnp.pad(cache_mla[:, j].reshape(b_lat * past, -1), ((0, 0), (0, LANE - MLA_ROPE)))
            kp, vp = _cache_kv(cpad, w)
            if pending is None:
                qc, kc, vc, new_cache = _mla_proj(xc, xc0, mods, layer, ctx, g1, w, None, (new_cache, j, n_mla))
                ql, kl, vl = _mla_proj(xl, xl0, mods, layer, lat, g1, w, _mla_rope_tables(t_lat))
            else:
                qc, kc, vc, new_cache, x = _mla_proj(x, ctx.row0, mods, layer, ctx, g1, w, None,
                                                     (new_cache, j, n_mla), pending)
                ql, kl, vl, x = _mla_proj(x, lat.row0, mods, layer, lat, g1, w, _mla_rope_tables(t_lat), None,
                                          pending)
                xc = xl = x
            first = x is None
            x = _attention(xc, xc0, n_ctx + n_lat, jnp.zeros((n_ctx + n_lat, d), F32) if first else "inplace",
                           mods, layer, ctx, qc, [(kc, vc, t_ctx)], w["w_o"])
            x = _attention(xl if first else x, xl0, n_ctx + n_lat, x if first else "inplace", mods, layer, lat,
                           ql, [(kl, vl, t_lat), (kp, vp, past)], w["w_o"])
        elif kind == 1:
            w_in = ret_w_in[j]
            qk = RET_HEADS * dk
            k_scale = jnp.concatenate([jnp.ones((qk,), F32), jnp.full((qk,), dk ** -0.5, F32),
                                       jnp.ones((w_in.shape[1] - 2 * qk,), F32)])
            w_in_b = (w_in * k_scale[None, :]).astype(BF16)
            w_o_b = ret_w_o[j].astype(BF16)
            rot = _ret_rot_tables(t_lat, dk)
            n_qk, n_v = 2 * qk // RET_COL, RET_HEADS * dv // RET_COL
            assert (n_qk + n_v) * RET_COL * 2 == w_in.shape[1]
            parts = []
            for seg in segs:
                kinds = (("rotary" if seg is lat else "plain"),) * n_qk + ("plain",) * n_v
                qkv = _ret_proj(x, mods, layer, seg, g1, w_in_b, 0, kinds, rot, dk, pending)
                if pending is not None:
                    qkv, x = qkv
                parts.append((qkv, _ret_proj(x, mods, layer, seg, g1, w_in_b, 1, ("silu",) * (n_qk + n_v), None, dk)))
            x, s_ctx = _ret_scan(*parts[0], ctx, ret_decay_f[j], ret_decay_b[j], None, True, dk, dv, RET_HEADS,
                                 out_proj=(x, mods, layer, w_o_b))
            yl, _ = _ret_scan(*parts[1], lat, ret_decay_f[j], ret_decay_b[j], state_ret[:, j], False, dk, dv, 1)
            x = _matmul_residual(x, mods, layer, lat, yl, w_o_b)
            states.append(s_ctx)
        else:
            gd = d // FNET_GROUPS
            cc, sc = _dft_tables(gd)
            cs = jnp.concatenate([cc, sc], axis=1)
            w_b = fnet_w[j].astype(BF16)
            for seg in segs:
                ct, st = _dft_tables(seg.seq)
                if seg.seq <= FNET_ROWS:
                    x = _fnet_short(x, mods, layer, seg, g1, cs, ct, st, w_b, (seg.seq * gd) ** -0.5, pending)
                    continue
                if pending is None:
                    ac, as_ = _fnet_a(x, mods, layer, seg, g1, cs)
                else:
                    ac, as_, x = _fnet_a(x, mods, layer, seg, g1, cs, pending)
                x = _fnet_b(x, mods, layer, seg, ac, as_, ct, st, w_b, (seg.seq * gd) ** -0.5)
        if layer < depth - 1:
            pending = _moe(x, mods, layer, segs, g2, rw_hi, rw_lo, rb, moe_w_gate, moe_w_up, moe_w_down, None)
        else:
            y_prompt, y_sample = _moe(x, mods, layer, segs, g2, rw_hi, rw_lo, rb, moe_w_gate, moe_w_up,
                                      moe_w_down, final_g)

    new_state = jnp.stack(states, axis=1)
    return (y_prompt.reshape(b_ctx, t_ctx, d), y_sample.reshape(b_lat, t_lat, d), new_cache, new_state)
```

```python
import functools
import math

import jax
import jax.numpy as jnp
import numpy as np
from jax import lax
from jax.experimental import pallas as pl
from jax.experimental.pallas import tpu as pltpu
from jax.experimental.pallas import tpu_sc as plsc

F32 = jnp.float32
BF16 = jnp.bfloat16

GRID_W = 64
MLA_HEADS = 8
MLA_NOPE = 128
MLA_ROPE = 64
MLA_V = 128
MLA_Q_LORA = 384
MLA_KV_LORA = 256
ROPE_BASE = 10000.0
RET_HEADS = 4
RET_CHUNK = 256
FNET_GROUPS = 4
N_EXPERTS = 16
N_EXPERT_GROUPS = 4
EXPERTS_PER_GROUP = 4
D_EXPERT = 512
NORM_EPS = 1e-6

LANE = 128
PROJ_ROWS = 512
ATTN_ROWS = 512
FNET_ROWS = 512
ATTN_SHORT_SEQS = 4
WIDE_TILE = 1024
ROUTER_SUB_ROWS = 256
MOE_ROWS = 1024
MOE_TAIL_ROWS = 256
VMEM_LIMIT = 56 * 1024 * 1024
SC_CORES = 2
SC_SUBCORES = 16
SC_CHUNK = 128


def _cparams(*sem):
    return pltpu.CompilerParams(dimension_semantics=sem, vmem_limit_bytes=VMEM_LIMIT)


def _sigmoid(x):
    return 1.0 / (1.0 + jnp.exp(-x))


def _rms(x):
    return x * lax.rsqrt(jnp.mean(x * x, axis=-1, keepdims=True) + NORM_EPS)


def _modulate(x, g, shift, scale):
    return _rms(x) * (g * (1.0 + scale)) + shift


def _dot(a, b):
    return jnp.dot(a, b, preferred_element_type=F32)


def _dot_nt(a, b):
    return lax.dot_general(a, b, (((1,), (1,)), ((), ())), preferred_element_type=F32)


def _mod_kernel(c_ref, w_ref, b_ref, o_ref):
    c = c_ref[...]
    s = (c * _sigmoid(c)).astype(BF16)
    o_ref[...] = _dot(s, w_ref[...].astype(BF16)) + b_ref[...]


def _modulation_all(cond8, ada_w, ada_b):
    depth, d, d6 = ada_w.shape
    tn = d6 // 4
    return pl.pallas_call(
        _mod_kernel,
        grid=(depth, d6 // tn),
        in_specs=[
            pl.BlockSpec((8, d), lambda l, n: (0, 0)),
            pl.BlockSpec((None, d, tn), lambda l, n: (l, 0, n)),
            pl.BlockSpec((None, 1, tn), lambda l, n: (l, 0, n)),
        ],
        out_specs=pl.BlockSpec((None, 8, tn), lambda l, n: (l, 0, n)),
        out_shape=jax.ShapeDtypeStruct((depth, 8, d6), F32),
        compiler_params=_cparams("parallel", "parallel"),
        name="modulation",
    )(cond8, ada_w, ada_b.reshape(depth, 1, d6))


class _Seg:
    def __init__(self, row0, batch, seq, mod0, per_batch_mod):
        self.row0, self.batch, self.seq = row0, batch, seq
        self.mod0, self.per_batch_mod = mod0, per_batch_mod
        self.rows = batch * seq

    def tile(self, want):
        tm = min(want, self.seq) if self.per_batch_mod else want
        assert self.rows % tm == 0 and self.row0 % tm == 0 and (self.seq % tm == 0 or tm % self.seq == 0)
        return tm

    def seq_tile(self, want):
        tm = min(want, self.seq)
        assert self.seq % tm == 0 and self.row0 % tm == 0
        return tm

    def mod_row(self, tile, tm):
        if self.per_batch_mod:
            return self.mod0 + tile * tm // self.seq
        return self.mod0


def _mod_spec(layer, seg, d, tm, tile_of=lambda *a: a[0]):
    return pl.BlockSpec((None, None, 6, 1, d), lambda *a: (layer, seg.mod_row(tile_of(*a), tm), 0, 0, 0))


def _unpack_halves(p):
    lo = lax.bitcast_convert_type(p << 16, F32)
    hi = lax.bitcast_convert_type(p & jnp.uint32(0xFFFF0000), F32)
    return lo, hi


def _moe_residual(y_ref, w_ref, gate):
    w = w_ref[...]
    lo0, hi0 = _unpack_halves(y_ref[0])
    lo1, hi1 = _unpack_halves(y_ref[1])
    w0, w1 = w[:, 0:1], w[:, 1:2]
    return gate * jnp.concatenate([w0 * lo0 + w1 * lo1, w0 * hi0 + w1 * hi1], axis=-1)


def _take_pending(refs, pending):
    if not pending:
        return refs, lambda x: x
    y_ref, w_ref, pm_ref, *rest = refs
    xo_ref = rest.pop()

    def resolve(x):
        x = x + _moe_residual(y_ref, w_ref, pm_ref[5])
        xo_ref[...] = x
        return x

    return rest, resolve


def _pending_io(pending, seg, tm, d, n):
    yg_by_seg, wcol, mods, layer = pending
    t0 = seg.row0 // tm
    specs = [pl.BlockSpec((2, tm, d // 2), lambda i: (0, i, 0)),
             pl.BlockSpec((tm, 8), lambda i: (t0 + i, 0)),
             _mod_spec(layer, seg, d, tm)]
    return (specs, [yg_by_seg[seg], wcol, mods], pl.BlockSpec((tm, d), lambda i: (t0 + i, 0)),
            jax.ShapeDtypeStruct((n, d), F32))


def _rope_partner(x):
    return pltpu.roll(x, LANE // 2, 1)


def _rope_perm():
    return np.array([l + 16 if l % 32 < 16 else l - 16 for l in range(MLA_ROPE)])


def _store_values(v_ref, v):
    ones = jnp.ones((v.shape[0], LANE), BF16)
    for hd in range(MLA_HEADS):
        v_ref[:, hd * 2 * LANE:hd * 2 * LANE + LANE] = v[:, hd * MLA_V:(hd + 1) * MLA_V].astype(BF16)
        v_ref[:, hd * 2 * LANE + LANE:(hd + 1) * 2 * LANE] = ones


def _mla_proj_kernel(*refs, rope, pending):
    refs, resolve = _take_pending(refs, pending)
    if rope:
        (x_ref, m_ref, g_ref, win_ref, qg_ref, kvg_ref, wq_ref, wkn_ref, wv_ref, cos_ref, sin_ref,
         q_ref, k_ref, v_ref) = refs
    else:
        x_ref, m_ref, g_ref, win_ref, qg_ref, kvg_ref, wq_ref, wkn_ref, wv_ref = refs[:9]
        q_ref, k_ref, v_ref, cache_ref = refs[-4:]
    h = _modulate(resolve(x_ref[...]), g_ref[...], m_ref[0], m_ref[1]).astype(BF16)
    z = _dot(h, win_ref[...])
    cq = z[:, :MLA_Q_LORA]
    ckv = z[:, MLA_Q_LORA:MLA_Q_LORA + MLA_KV_LORA]
    kpe = z[:, MLA_Q_LORA + MLA_KV_LORA:]
    cqn = (_rms(cq) * qg_ref[...]).astype(BF16)
    ckvn = _rms(ckv) * kvg_ref[...]
    ckvb = ckvn.astype(BF16)
    q = _dot(cqn, wq_ref[...])
    kn = _dot(ckvb, wkn_ref[...])
    _store_values(v_ref, _dot(ckvb, wv_ref[...]))
    if rope:
        cos, sin = cos_ref[...], sin_ref[...]
        kpe = kpe * cos + _rope_partner(kpe) * sin
    kpe = jnp.where(lax.broadcasted_iota(jnp.int32, kpe.shape, 1) < MLA_ROPE, kpe, 0.0)
    if not rope:
        seq = cache_ref.shape[1]
        for s in range(cache_ref.shape[0]):
            cache_ref[s, :, :MLA_KV_LORA] = ckvn[s * seq:(s + 1) * seq, :]
            cache_ref[s, :, MLA_KV_LORA:] = kpe[s * seq:(s + 1) * seq, :MLA_ROPE]
    kpe_b = kpe.astype(BF16)
    for hd in range(MLA_HEADS):
        lo = hd * 2 * LANE
        q_ref[:, lo:lo + LANE] = q[:, lo:lo + LANE].astype(BF16)
        qr = q[:, lo + LANE:lo + 2 * LANE]
        if rope:
            qr = qr * cos + _rope_partner(qr) * sin
        q_ref[:, lo + LANE:lo + 2 * LANE] = qr.astype(BF16)
        k_ref[:, lo:lo + LANE] = kn[:, hd * LANE:(hd + 1) * LANE].astype(BF16)
        k_ref[:, lo + LANE:lo + 2 * LANE] = kpe_b


def _mla_proj(x, x_row0, mods, layer, seg, g1, w, rope_tabs, cache_slot=None, pending=None):
    n, d = x.shape
    rope = rope_tabs is not None
    rows = seg.rows
    tm = seg.tile(PROJ_ROWS)
    x_tile0 = x_row0 // tm
    hq = MLA_HEADS * 2 * LANE
    const = lambda i: (0, 0)
    aliases = {}
    in_specs = [
        pl.BlockSpec((tm, d), lambda i: (x_tile0 + i, 0)),
        _mod_spec(layer, seg, d, tm),
        pl.BlockSpec((1, d), const),
        pl.BlockSpec(w["w_in"].shape, const),
        pl.BlockSpec((1, MLA_Q_LORA), const),
        pl.BlockSpec((1, MLA_KV_LORA), const),
        pl.BlockSpec(w["w_q"].shape, const),
        pl.BlockSpec(w["w_kn"].shape, const),
        pl.BlockSpec(w["w_v"].shape, const),
    ]
    args = [x, mods, g1, w["w_in"], w["q_g"], w["kv_g"], w["w_q"], w["w_kn"], w["w_v"]]
    out_specs = [pl.BlockSpec((tm, hq), lambda i: (i, 0))] * 3
    out_shape = [jax.ShapeDtypeStruct((rows, hq), BF16)] * 3
    if rope:
        tab = pl.BlockSpec((tm, LANE), lambda i: (i % (seg.seq // tm), 0))
        in_specs += [tab, tab]
        args += list(rope_tabs)
    else:
        cw = MLA_KV_LORA + MLA_ROPE
        prev, slot, n_slots = cache_slot
        assert tm % seg.seq == 0
        out_specs.append(pl.BlockSpec((tm // seg.seq, None, seg.seq, cw), lambda i: (i, slot, 0, 0)))
        out_shape.append(jax.ShapeDtypeStruct((seg.batch, n_slots, seg.seq, cw), F32))
        in_specs.append(pl.BlockSpec(memory_space=pl.ANY))
        args.append(prev)
        aliases = {len(args) - 1: 3}
    if pending is not None:
        p_specs, p_args, xo_spec, xo_shape = _pending_io(pending, seg, tm, d, n)
        in_specs, args = p_specs + in_specs, p_args + args
        out_specs.append(xo_spec)
        out_shape.append(xo_shape)
        aliases = {k + len(p_args): v for k, v in aliases.items()}
        aliases[len(p_args)] = len(out_shape) - 1
    return pl.pallas_call(
        functools.partial(_mla_proj_kernel, rope=rope, pending=pending is not None),
        grid=(rows // tm,),
        in_specs=in_specs,
        out_specs=out_specs,
        out_shape=out_shape,
        input_output_aliases=aliases,
        compiler_params=_cparams("parallel"),
        name="mla_proj_lat" if rope else "mla_proj_ctx",
    )(*args)


def _cache_kv_kernel(c_ref, wkn_ref, wv_ref, k_ref, v_ref):
    c = c_ref[...]
    ckv = c[:, :MLA_KV_LORA].astype(BF16)
    kpe_b = c[:, MLA_KV_LORA:].astype(BF16)
    kn = _dot(ckv, wkn_ref[...])
    _store_values(v_ref, _dot(ckv, wv_ref[...]))
    for hd in range(MLA_HEADS):
        lo = hd * 2 * LANE
        k_ref[:, lo:lo + LANE] = kn[:, hd * LANE:(hd + 1) * LANE].astype(BF16)
        k_ref[:, lo + LANE:lo + 2 * LANE] = kpe_b


def _cache_kv(cache_pad, w):
    rows, cw = cache_pad.shape
    hq = MLA_HEADS * 2 * LANE
    const = lambda i: (0, 0)
    tm = min(PROJ_ROWS, rows)
    assert rows % tm == 0
    return pl.pallas_call(
        _cache_kv_kernel,
        grid=(rows // tm,),
        in_specs=[
            pl.BlockSpec((tm, cw), lambda i: (i, 0)),
            pl.BlockSpec(w["w_kn"].shape, const),
            pl.BlockSpec(w["w_v"].shape, const),
        ],
        out_specs=[pl.BlockSpec((tm, hq), lambda i: (i, 0))] * 2,
        out_shape=[jax.ShapeDtypeStruct((rows, hq), BF16)] * 2,
        compiler_params=_cparams("parallel"),
        name="mla_cache_kv",
    )(cache_pad, w["w_kn"], w["w_v"])


def _attn_kernel(*refs, n_parts, n_seq):
    q_ref = refs[0]
    kv_refs = refs[1:1 + 2 * n_parts]
    wo_ref, x_ref, m_ref = refs[1 + 2 * n_parts:4 + 2 * n_parts]
    o_ref, acc_ref = refs[-2:]
    tq = q_ref.shape[0] // n_seq
    for sq, hd in [(sq, hd) for sq in range(n_seq) for hd in range(MLA_HEADS)]:
        rows = slice(sq * tq, (sq + 1) * tq)
        kcol = slice(hd * 2 * LANE, (hd + 1) * 2 * LANE)
        keys = [slice(sq * (r.shape[0] // n_seq), (sq + 1) * (r.shape[0] // n_seq)) for r in kv_refs[::2]]
        scores = [_dot_nt(q_ref[rows, kcol], kv_refs[2 * p][keys[p], kcol]) for p in range(n_parts)]
        mx = scores[0].max(axis=-1, keepdims=True)
        for s in scores[1:]:
            mx = jnp.maximum(mx, s.max(axis=-1, keepdims=True))
        out = None
        for p, s in enumerate(scores):
            e = jnp.exp((s - mx).astype(BF16))
            pv = _dot(e, kv_refs[2 * p + 1][keys[p], kcol])
            out = pv if out is None else out + pv
        acc_ref[rows, hd * MLA_V:(hd + 1) * MLA_V] = (out[:, :MLA_V] / out[:, LANE:LANE + MLA_V]).astype(BF16)
    y = _dot(acc_ref[...], wo_ref[...])
    o_ref[...] = x_ref[...] + m_ref[2] * y


def _attention(x, x_row0, n, dest, mods, layer, seg, q, kv_parts, w_o):
    d = x.shape[1]
    hq = MLA_HEADS * 2 * LANE
    hv = MLA_HEADS * MLA_V
    tq = seg.seq_tile(ATTN_ROWS)
    tps = seg.seq // tq
    n_seq = ATTN_SHORT_SEQS if (tps == 1 and not seg.per_batch_mod and seg.batch % ATTN_SHORT_SEQS == 0) else 1
    tq *= n_seq
    x_tile0, out_tile0 = x_row0 // tq, seg.row0 // tq
    in_specs = [pl.BlockSpec((tq, hq), lambda b, i: (b * tps + i, 0))]
    args = [q]
    for k, v, rows in kv_parts:
        mode = dict(pipeline_mode=pl.Buffered(1)) if tps > 1 else {}
        in_specs += [pl.BlockSpec((n_seq * rows, hq), lambda b, i: (b, 0)),
                     pl.BlockSpec((n_seq * rows, hq), lambda b, i: (b, 0), **mode)]
        args += [k, v]
    in_specs += [
        pl.BlockSpec(w_o.shape, lambda b, i: (0, 0)),
        pl.BlockSpec((tq, d), lambda b, i: (x_tile0 + b * tps + i, 0)),
        _mod_spec(layer, seg, d, tq // n_seq, tile_of=lambda b, i: (b * tps + i) * n_seq),
    ]
    args += [w_o, x, mods]
    if isinstance(dest, str):
        assert dest == "inplace"
        aliases = {len(args) - 2: 0}
    else:
        in_specs.append(pl.BlockSpec(memory_space=pl.ANY))
        args.append(dest)
        aliases = {len(args) - 1: 0}
    return pl.pallas_call(
        functools.partial(_attn_kernel, n_parts=len(kv_parts), n_seq=n_seq),
        grid=(seg.batch // n_seq, tps),
        in_specs=in_specs,
        out_specs=pl.BlockSpec((tq, d), lambda b, i: (out_tile0 + b * tps + i, 0)),
        out_shape=jax.ShapeDtypeStruct((n, d), F32),
        scratch_shapes=[pltpu.VMEM((tq, hv), BF16)],
        input_output_aliases=aliases,
        compiler_params=_cparams("parallel", "arbitrary"),
        name="mla_attention",
    )(*args)


RET_COL = 1024


RET_ROWS = 1024


def _ret_proj_kernel(*refs, kinds, dk, pending):
    refs, resolve = _take_pending(refs, pending)
    rotary = "rotary" in kinds
    if rotary:
        x_ref, m_ref, g_ref, w_ref, cos_ref, sin_ref, z_ref = refs
        cos, sin = cos_ref[...], sin_ref[...]
    else:
        x_ref, m_ref, g_ref, w_ref, z_ref = refs
    h = _modulate(resolve(x_ref[...]), g_ref[...], m_ref[0], m_ref[1]).astype(BF16)
    half = dk // 2
    for j, kind in enumerate(kinds):
        c0 = j * RET_COL
        acc = _dot(h, w_ref[:, c0:c0 + RET_COL])
        if kind == "rotary":
            for hd in range(RET_COL // dk):
                lo = hd * dk
                x1, x2 = acc[:, lo:lo + half], acc[:, lo + half:lo + dk]
                z_ref[:, c0 + lo:c0 + lo + half] = (x1 * cos - x2 * sin).astype(BF16)
                z_ref[:, c0 + lo + half:c0 + lo + dk] = (x1 * sin + x2 * cos).astype(BF16)
        elif kind == "silu":
            z_ref[:, c0:c0 + RET_COL] = (acc * _sigmoid(acc)).astype(BF16)
        else:
            z_ref[:, c0:c0 + RET_COL] = acc.astype(BF16)


def _ret_proj(x, mods, layer, seg, g1, w_in, group, kinds, rot_tabs, dk, pending=None):
    n, d = x.shape
    rows = seg.rows
    tm = seg.tile(RET_ROWS)
    ncol = len(kinds) * RET_COL
    tps = max(seg.seq // tm, 1)
    in_specs = [
        pl.BlockSpec((tm, d), lambda i: (seg.row0 // tm + i, 0)),
        _mod_spec(layer, seg, d, tm),
        pl.BlockSpec((1, d), lambda i: (0, 0)),
        pl.BlockSpec((d, ncol), lambda i: (0, group), pipeline_mode=pl.Buffered(1)),
    ]
    args = [x, mods, g1, w_in]
    if "rotary" in kinds:
        tab = pl.BlockSpec((tm, dk // 2), lambda i: (i % tps, 0))
        in_specs += [tab, tab]
        args += list(rot_tabs)
    out_specs = [pl.BlockSpec((tm, ncol), lambda i: (i, 0))]
    out_shape = [jax.ShapeDtypeStruct((rows, ncol), BF16)]
    aliases = {}
    if pending is not None:
        p_specs, p_args, xo_spec, xo_shape = _pending_io(pending, seg, tm, d, n)
        in_specs, args = p_specs + in_specs, p_args + args
        out_specs.append(xo_spec)
        out_shape.append(xo_shape)
        aliases = {len(p_args): 1}
    res = pl.pallas_call(
        functools.partial(_ret_proj_kernel, kinds=kinds, dk=dk, pending=pending is not None),
        grid=(rows // tm,),
        in_specs=in_specs,
        out_specs=out_specs,
        out_shape=out_shape,
        input_output_aliases=aliases,
        compiler_params=_cparams("parallel"),
        name="ret_proj_" + kinds[0],
    )(*args)
    return res if pending is not None else res[0]


def _log_sigmoid(x):
    return jnp.minimum(x, 0.0) - jnp.log(1.0 + jnp.exp(-jnp.abs(x)))


def _ret_scan_kernel(*refs, has_s0, emit_state, n_chunks, heads, fuse_out):
    refs = list(refs)
    lf_ref, lb_ref, q_ref, k_ref, v_ref, gf_ref, gb_ref = refs[:7]
    pos = 7
    s0_ref = None
    if has_s0:
        s0_ref = refs[pos]
        pos += 1
    if fuse_out:
        wo_ref, x_ref, m_ref, o_ref = refs[pos:pos + 4]
        pos += 4
    else:
        y_ref = refs[pos]
        pos += 1
    sout_ref = None
    if emit_state:
        sout_ref = refs[pos]
        pos += 1
    if fuse_out:
        s_ref, yf_ref, y_ref = refs[pos:]
    else:
        s_ref, yf_ref = refs[pos:]
    c = RET_CHUNK
    dk, dv = s_ref.shape
    ii = lax.broadcasted_iota(jnp.int32, (c, c), 0).astype(F32)
    jj = lax.broadcasted_iota(jnp.int32, (c, c), 1).astype(F32)
    idx = lax.broadcasted_iota(jnp.int32, (c, 1), 0).astype(F32)

    for hd, direction in [(hd, direction) for hd in range(heads) for direction in range(2)]:
        fwd = direction == 0
        kcol, vcol = slice(hd * dk, (hd + 1) * dk), slice(hd * dv, (hd + 1) * dv)
        lg = _log_sigmoid((lf_ref if fwd else lb_ref)[hd])
        rel = (ii - jj) if fwd else (jj - ii)
        keep = rel >= 0
        decay_in = jnp.where(keep, jnp.exp(jnp.where(keep, rel, 0.0) * lg), 0.0) * dk ** -0.5
        decay_q = jnp.exp(((idx + 1.0) if fwd else (c - idx)) * lg)
        decay_k = jnp.exp(((c - 1.0 - idx) if fwd else idx) * lg) * dk ** -0.5
        decay_c = jnp.exp(c * lg)
        g_ref = gf_ref if fwd else gb_ref

        def chunk(cc, state, fwd=fwd, decay_in=decay_in, decay_q=decay_q, decay_k=decay_k, decay_c=decay_c,
                  g_ref=g_ref, kcol=kcol, vcol=vcol):
            r0 = cc * c if isinstance(cc, int) else pl.multiple_of(cc * c, c)
            qc = q_ref[pl.ds(r0, c), kcol]
            kc = k_ref[pl.ds(r0, c), kcol]
            vc = v_ref[pl.ds(r0, c), vcol]
            sc = _dot_nt(qc, kc) * decay_in
            out = _dot(sc.astype(BF16), vc)
            kd_t = (kc.astype(F32) * decay_k).T.astype(BF16)
            new_s = _dot(kd_t, vc)
            if state is not None:
                out = out + decay_q * _dot(qc, state.astype(BF16))
                new_s = decay_c * state + new_s
            s_ref[...] = new_s
            o = _rms(out) * g_ref[pl.ds(r0, c), vcol].astype(F32)
            if fwd:
                yf_ref[pl.ds(r0, c), :] = o
            else:
                y_ref[pl.ds(r0, c), vcol] = (yf_ref[pl.ds(r0, c), :] + o).astype(BF16)

        chunk(0 if fwd else n_chunks - 1, s0_ref[direction, hd] if has_s0 else None)

        def step(ci, carry, fwd=fwd, chunk=chunk):
            chunk(ci if fwd else n_chunks - 1 - ci, s_ref[...])
            return carry

        lax.fori_loop(1, n_chunks, step, 0, unroll=True)
        if emit_state:
            sout_ref[direction, hd] = s_ref[...]
    if fuse_out:
        o_ref[...] = x_ref[...] + m_ref[2] * _dot(y_ref[...], wo_ref[...])


def _ret_scan(qkv, g, seg, logit_f, logit_b, s0, emit_state, dk, dv, heads, out_proj=None):
    rows = seg.batch * seg.seq
    t = seg.seq
    hh = RET_HEADS
    groups = hh // heads
    fuse_out = out_proj is not None
    assert not fuse_out or groups == 1
    v0 = 2 * hh * dk // (heads * dv)
    assert groups * heads == hh and v0 * heads * dv == 2 * hh * dk
    in_specs = [
        pl.BlockSpec((heads, 1, 1), lambda b, h: (h, 0, 0)),
        pl.BlockSpec((heads, 1, 1), lambda b, h: (h, 0, 0)),
        pl.BlockSpec((t, heads * dk), lambda b, h: (b, h)),
        pl.BlockSpec((t, heads * dk), lambda b, h: (b, groups + h)),
        pl.BlockSpec((t, heads * dv), lambda b, h: (b, v0 + h)),
        pl.BlockSpec((t, heads * dv), lambda b, h: (b, h)),
        pl.BlockSpec((t, heads * dv), lambda b, h: (b, groups + h)),
    ]
    args = [logit_f.reshape(hh, 1, 1), logit_b.reshape(hh, 1, 1), qkv, qkv, qkv, g, g]
    state_spec = pl.BlockSpec((None, 2, heads, dk, dv), lambda b, h: (b, 0, h, 0, 0))
    if s0 is not None:
        in_specs.append(state_spec)
        args.append(s0)
    scratch = [pltpu.VMEM((dk, dv), F32), pltpu.VMEM((t, dv), F32)]
    aliases = {}
    if fuse_out:
        x, mods, layer, w_o = out_proj
        n, d = x.shape
        x_spec = pl.BlockSpec((t, d), lambda b, h: (seg.row0 // t + b, 0))
        in_specs += [pl.BlockSpec(w_o.shape, lambda b, h: (0, 0)), x_spec,
                     _mod_spec(layer, seg, d, t, tile_of=lambda b, h: b)]
        args += [w_o, x, mods]
        aliases = {len(args) - 2: 0}
        out_specs, out_shape = [x_spec], [jax.ShapeDtypeStruct((n, d), F32)]
        scratch.append(pltpu.VMEM((t, hh * dv), BF16))
    else:
        out_specs = [pl.BlockSpec((t, heads * dv), lambda b, h: (b, h))]
        out_shape = [jax.ShapeDtypeStruct((rows, hh * dv), BF16)]
    if emit_state:
        out_specs.append(state_spec)
        out_shape.append(jax.ShapeDtypeStruct((seg.batch, 2, hh, dk, dv), F32))
    res = pl.pallas_call(
        functools.partial(_ret_scan_kernel, has_s0=s0 is not None, emit_state=emit_state,
                          n_chunks=t // RET_CHUNK, heads=heads, fuse_out=fuse_out),
        grid=(seg.batch, groups),
        in_specs=in_specs,
        out_specs=out_specs,
        out_shape=out_shape,
        scratch_shapes=scratch,
        input_output_aliases=aliases,
        compiler_params=_cparams("parallel", "parallel"),
        name="ret_scan",
    )(*args)
    return res if emit_state else (res[0], None)


def _mm_res_kernel(a_ref, w_ref, x_ref, m_ref, o_ref):
    o_ref[...] = x_ref[...] + m_ref[2] * _dot(a_ref[...], w_ref[...])


def _matmul_residual(x, mods, layer, seg, a, w):
    n, d = x.shape
    tm = seg.tile(RET_ROWS)
    x_spec = pl.BlockSpec((tm, d), lambda i: (seg.row0 // tm + i, 0))
    return pl.pallas_call(
        _mm_res_kernel,
        grid=(seg.rows // tm,),
        in_specs=[
            pl.BlockSpec((tm, a.shape[1]), lambda i: (i, 0)),
            pl.BlockSpec(w.shape, lambda i: (0, 0)),
            x_spec,
            _mod_spec(layer, seg, d, tm),
        ],
        out_specs=x_spec,
        out_shape=jax.ShapeDtypeStruct((n, d), F32),
        input_output_aliases={2: 0},
        compiler_params=_cparams("parallel"),
        name="matmul_residual",
    )(a, w, x, mods)


def _fnet_a_kernel(*refs, gd, pending):
    refs, resolve = _take_pending(refs, pending)
    x_ref, m_ref, g_ref, cs_ref, ac_ref, as_ref = refs
    h = _modulate(resolve(x_ref[...]), g_ref[...], m_ref[0], m_ref[1]).astype(BF16)
    cs = cs_ref[...]
    for g in range(FNET_GROUPS):
        a = _dot(h[:, g * gd:(g + 1) * gd], cs)
        ac_ref[:, g * gd:(g + 1) * gd] = a[:, :gd].astype(BF16)
        as_ref[:, g * gd:(g + 1) * gd] = a[:, gd:].astype(BF16)


def _fnet_a(x, mods, layer, seg, g1, cs, pending=None):
    n, d = x.shape
    rows = seg.rows
    tm = seg.tile(PROJ_ROWS)
    out = pl.BlockSpec((tm, d), lambda i: (i, 0))
    in_specs = [
        pl.BlockSpec((tm, d), lambda i: (seg.row0 // tm + i, 0)),
        _mod_spec(layer, seg, d, tm),
        pl.BlockSpec((1, d), lambda i: (0, 0)),
        pl.BlockSpec(cs.shape, lambda i: (0, 0)),
    ]
    args = [x, mods, g1, cs]
    out_specs, out_shape, aliases = [out, out], [jax.ShapeDtypeStruct((rows, d), BF16)] * 2, {}
    if pending is not None:
        p_specs, p_args, xo_spec, xo_shape = _pending_io(pending, seg, tm, d, n)
        in_specs, args = p_specs + in_specs, p_args + args
        out_specs.append(xo_spec)
        out_shape.append(xo_shape)
        aliases = {len(p_args): 2}
    return pl.pallas_call(
        functools.partial(_fnet_a_kernel, gd=d // FNET_GROUPS, pending=pending is not None),
        grid=(rows // tm,),
        in_specs=in_specs,
        out_specs=out_specs,
        out_shape=out_shape,
        input_output_aliases=aliases,
        compiler_params=_cparams("parallel"),
        name="fnet_channel_dft",
    )(*args)


def _fnet_b_kernel(ct_ref, st_ref, ac_ref, as_ref, w_ref, x_ref, m_ref, o_ref, *, norm):
    f = (_dot(ct_ref[...], ac_ref[...]) - _dot(st_ref[...], as_ref[...])) * norm
    o_ref[...] = x_ref[...] + m_ref[2] * _dot(f.astype(BF16), w_ref[...])


def _fnet_b(x, mods, layer, seg, ac, as_, ct, st, w, norm):
    n, d = x.shape
    t = seg.seq
    tq = seg.seq_tile(FNET_ROWS)
    tps = t // tq
    x_spec = pl.BlockSpec((tq, d), lambda b, i: (seg.row0 // tq + b * tps + i, 0))
    tab = pl.BlockSpec((tq, t), lambda b, i: (i, 0))
    seq = pl.BlockSpec((t, d), lambda b, i: (b, 0))
    return pl.pallas_call(
        functools.partial(_fnet_b_kernel, norm=norm),
        grid=(seg.batch, tps),
        in_specs=[tab, tab, seq, seq, pl.BlockSpec(w.shape, lambda b, i: (0, 0)), x_spec,
                  _mod_spec(layer, seg, d, tq, tile_of=lambda b, i: b * tps + i)],
        out_specs=x_spec,
        out_shape=jax.ShapeDtypeStruct((n, d), F32),
        input_output_aliases={5: 0},
        compiler_params=_cparams("parallel", "arbitrary"),
        name="fnet_position_dft",
    )(ct, st, ac, as_, w, x, mods)


def _fnet_short_kernel(*refs, gd, norm, n_seq, pending):
    if pending:
        y_ref, wr_ref, pm_ref, *refs = refs
    x_ref, m_ref, g_ref, cs_ref, ct_ref, st_ref, w_ref, o_ref, ac_ref, as_ref = refs
    x = x_ref[...]
    if pending:
        x = x + _moe_residual(y_ref, wr_ref, pm_ref[5])
    h = _modulate(x, g_ref[...], m_ref[0], m_ref[1]).astype(BF16)
    cs = cs_ref[...]
    for g in range(FNET_GROUPS):
        a = _dot(h[:, g * gd:(g + 1) * gd], cs)
        ac_ref[:, g * gd:(g + 1) * gd] = a[:, :gd].astype(BF16)
        as_ref[:, g * gd:(g + 1) * gd] = a[:, gd:].astype(BF16)
    t = x.shape[0] // n_seq
    ct, st = ct_ref[...], st_ref[...]
    f = [_dot(ct, ac_ref[s * t:(s + 1) * t, :]) - _dot(st, as_ref[s * t:(s + 1) * t, :]) for s in range(n_seq)]
    f = (f[0] if n_seq == 1 else jnp.concatenate(f, axis=0)) * norm
    o_ref[...] = x + m_ref[2] * _dot(f.astype(BF16), w_ref[...])


def _fnet_short(x, mods, layer, seg, g1, cs, ct, st, w, norm, pending=None):
    n, d = x.shape
    t = seg.seq
    n_seq = ATTN_SHORT_SEQS if (not seg.per_batch_mod and seg.batch % ATTN_SHORT_SEQS == 0) else 1
    rows = n_seq * t
    t0 = seg.row0 // rows
    const = lambda i: (0, 0)
    x_spec = pl.BlockSpec((rows, d), lambda i: (t0 + i, 0))
    in_specs = [x_spec, _mod_spec(layer, seg, d, t, tile_of=lambda i: i * n_seq), pl.BlockSpec((1, d), const),
                pl.BlockSpec(cs.shape, const), pl.BlockSpec((t, t), const), pl.BlockSpec((t, t), const),
                pl.BlockSpec(w.shape, const)]
    args = [x, mods, g1, cs, ct, st, w]
    if pending is not None:
        yg_by_seg, wcol, pmods, player = pending
        in_specs = [pl.BlockSpec((2, rows, d // 2), lambda i: (0, i, 0)),
                    pl.BlockSpec((rows, 8), lambda i: (t0 + i, 0)),
                    _mod_spec(player, seg, d, t, tile_of=lambda i: i * n_seq)] + in_specs
        args = [yg_by_seg[seg], wcol, pmods] + args
    return pl.pallas_call(
        functools.partial(_fnet_short_kernel, gd=d // FNET_GROUPS, norm=norm, n_seq=n_seq,
                          pending=pending is not None),
        grid=(seg.rows // rows,),
        in_specs=in_specs,
        out_specs=x_spec,
        out_shape=jax.ShapeDtypeStruct((n, d), F32),
        scratch_shapes=[pltpu.VMEM((rows, d), BF16), pltpu.VMEM((rows, d), BF16)],
        input_output_aliases={len(args) - 7: 0},
        compiler_params=_cparams("parallel"),
        name="fnet_short",
    )(*args)


def _pack_halves(a):
    w = a.shape[1] // 2
    bits = lambda v: lax.bitcast_convert_type(v.astype(BF16).astype(F32), jnp.uint32)
    return (bits(a[:, :w]) >> 16) | (bits(a[:, w:]) & jnp.uint32(0xFFFF0000))


def _router_kernel(x_ref, m_ref, g_ref, rwhi_ref, rwlo_ref, rb_ref, h_ref, idx_ref, rank_ref, wcol_ref, cnt_ref,
                   run_ref, tri_ref):
    step = pl.program_id(0)

    @pl.when(step == 0)
    def _():
        run_ref[...] = jnp.zeros_like(run_ref)
        tt = tri_ref.shape[0]
        earlier = lax.broadcasted_iota(jnp.int32, (tt, tt), 0) < lax.broadcasted_iota(jnp.int32, (tt, tt), 1)
        tri_ref[...] = jnp.where(earlier, 1.0, 0.0).astype(BF16)

    parts = []
    for r0 in range(0, x_ref.shape[0], ROUTER_SUB_ROWS):
        rows = slice(r0, r0 + ROUTER_SUB_ROWS)
        h = _modulate(x_ref[rows, :], g_ref[...], m_ref[3], m_ref[4])
        h_ref[rows, :] = _pack_halves(h)
        h_hi = h.astype(BF16)
        h_lo = (h - h_hi.astype(F32)).astype(BF16)
        parts.append(_dot_nt(rwhi_ref[...], h_hi) + (_dot_nt(rwhi_ref[...], h_lo) + _dot_nt(rwlo_ref[...], h_hi)))
    logits = jnp.concatenate(parts, axis=1)
    sc = _sigmoid(logits)
    gr = sc + rb_ref[...]
    gp = EXPERTS_PER_GROUP
    row = lambda a, e: a[e:e + 1, :]
    best_g = None
    for g in range(N_EXPERT_GROUPS):
        vals = [row(gr, g * gp + i) for i in range(gp)]
        gs = None
        for i in range(gp):
            for j in range(i + 1, gp):
                pair = vals[i] + vals[j]
                gs = pair if gs is None else jnp.maximum(gs, pair)
        if best_g is None:
            best_g, best_v = jnp.zeros(gs.shape, jnp.int32), gs
        else:
            better = gs > best_v
            best_g = jnp.where(better, g, best_g)
            best_v = jnp.where(better, gs, best_v)
    sel, raw = [], []
    for i in range(gp):
        s_i, r_i = row(gr, i), row(sc, i)
        for g in range(1, N_EXPERT_GROUPS):
            s_i = jnp.where(best_g == g, row(gr, g * gp + i), s_i)
            r_i = jnp.where(best_g == g, row(sc, g * gp + i), r_i)
        sel.append(s_i)
        raw.append(r_i)

    def argmax_first(vals, raws):
        bi, bv, br = jnp.zeros(vals[0].shape, jnp.int32), vals[0], raws[0]
        for i in range(1, len(vals)):
            better = vals[i] > bv
            bi = jnp.where(better, i, bi)
            bv = jnp.where(better, vals[i], bv)
            br = jnp.where(better, raws[i], br)
        return bi, br

    i1, w1 = argmax_first(sel, raw)
    masked = [jnp.where(i1 == i, -jnp.inf, sel[i]) for i in range(gp)]
    i2, w2 = argmax_first(masked, raw)
    tot = w1 + w2
    e1 = best_g * gp + i1
    e2 = best_g * gp + i2
    idx_ref[0:1, :] = e1
    idx_ref[1:2, :] = e2
    t = e1.shape[1]
    sub = lax.broadcasted_iota(jnp.int32, (8, t), 0)
    w8 = jnp.where(sub == 0, w1 / tot, jnp.where(sub == 1, w2 / tot, 0.0))
    wcol_ref[...] = w8.T
    eio = lax.broadcasted_iota(jnp.int32, (N_EXPERTS, t), 0)
    oh1, oh2 = eio == e1, eio == e2
    oh = jnp.where(oh1, 1.0, jnp.where(oh2, 1.0, 0.0))
    local = _dot(oh.astype(BF16), tri_ref[...])
    rank = local + run_ref[:, 0:1]
    rank_ref[0:1, :] = jnp.sum(jnp.where(oh1, rank, 0.0), axis=0, keepdims=True).astype(jnp.int32)
    rank_ref[1:2, :] = jnp.sum(jnp.where(oh2, rank, 0.0), axis=0, keepdims=True).astype(jnp.int32)
    run_ref[...] = run_ref[...] + jnp.sum(oh, axis=1, keepdims=True)
    cnt_ref[...] = run_ref[...]


def _wide_mod_row(segs, tm):
    ctx, lat = segs
    ctx_tiles = ctx.batch * ctx.seq // tm
    assert ctx_tiles * tm == ctx.batch * ctx.seq and lat.seq % tm == 0
    return lambda i: jnp.where(i < ctx_tiles, ctx.mod0, lat.mod0 + (i - ctx_tiles) // (lat.seq // tm))


def _router(x, mods, layer, segs, g2, rw_hi, rw_lo, rb):
    n, d = x.shape
    tm = WIDE_TILE
    mod_row = _wide_mod_row(segs, tm)
    return pl.pallas_call(
        _router_kernel,
        grid=(n // tm,),
        in_specs=[
            pl.BlockSpec((tm, d), lambda i: (i, 0)),
            pl.BlockSpec((None, None, 6, 1, d), lambda i: (layer, mod_row(i), 0, 0, 0)),
            pl.BlockSpec((1, d), lambda i: (0, 0)),
            pl.BlockSpec(rw_hi.shape, lambda i: (0, 0)),
            pl.BlockSpec(rw_lo.shape, lambda i: (0, 0)),
            pl.BlockSpec(rb.shape, lambda i: (0, 0)),
        ],
        out_specs=[
            pl.BlockSpec((tm, d // 2), lambda i: (i, 0)),
            pl.BlockSpec((2, tm), lambda i: (0, i)),
            pl.BlockSpec((2, tm), lambda i: (0, i)),
            pl.BlockSpec((tm, 8), lambda i: (i, 0)),
            pl.BlockSpec((N_EXPERTS, LANE), lambda i: (0, 0)),
        ],
        out_shape=[
            jax.ShapeDtypeStruct((n, d // 2), jnp.uint32),
            jax.ShapeDtypeStruct((2, n), jnp.int32),
            jax.ShapeDtypeStruct((2, n), jnp.int32),
            jax.ShapeDtypeStruct((n, 8), F32),
            jax.ShapeDtypeStruct((N_EXPERTS, LANE), F32),
        ],
        scratch_shapes=[pltpu.VMEM((N_EXPERTS, LANE), F32), pltpu.VMEM((tm, tm), BF16)],
        compiler_params=_cparams("arbitrary"),
        name="moe_router",
    )(x, mods, g2, rw_hi, rw_lo, rb)


def _expert_kernel(be_ref, br_ref, bs_ref, xs_ref, wg_ref, wu_ref, wd_ref, y_ref, wg_b, wu_b, wd_b):
    i = pl.program_id(0)
    prev = be_ref[jnp.maximum(i - 1, 0)]
    valid = br_ref[i]
    tail_rows = MOE_TAIL_ROWS

    @pl.when(jnp.logical_or(i == 0, be_ref[i] != prev))
    def _():
        wg_b[...] = wg_ref[...].astype(BF16)
        wu_b[...] = wu_ref[...].astype(BF16)
        wd_b[...] = wd_ref[...].astype(BF16)

    def ffn(r0, nrows):
        lo, hi = _unpack_halves(xs_ref[r0:r0 + nrows, :])
        xb = jnp.concatenate([lo.astype(BF16), hi.astype(BF16)], axis=1)
        gate = _dot(xb, wg_b[...])
        hid = (gate * _sigmoid(gate)) * _dot(xb, wu_b[...])
        y_ref[r0:r0 + nrows, :] = _pack_halves(_dot(hid.astype(BF16), wd_b[...]))

    @pl.when(valid > tail_rows)
    def _():
        ffn(0, y_ref.shape[0])

    @pl.when(jnp.logical_and(valid > 0, valid <= tail_rows))
    def _():
        ffn(0, tail_rows)
        y_ref[tail_rows:, :] = jnp.zeros((y_ref.shape[0] - tail_rows, y_ref.shape[1]), y_ref.dtype)


def _experts(xs, block_e, block_rows, block_src, w_gate, w_up, w_down, layer):
    rows, half = xs.shape
    d = 2 * half
    de = w_gate.shape[-1]
    n_blocks = rows // MOE_ROWS
    grid_spec = pltpu.PrefetchScalarGridSpec(
        num_scalar_prefetch=3,
        grid=(n_blocks,),
        in_specs=[
            pl.BlockSpec((MOE_ROWS, half), lambda i, be, br, bs: (bs[i], 0)),
            pl.BlockSpec((None, None, d, de), lambda i, be, br, bs: (layer, be[i], 0, 0)),
            pl.BlockSpec((None, None, d, de), lambda i, be, br, bs: (layer, be[i], 0, 0)),
            pl.BlockSpec((None, None, de, d), lambda i, be, br, bs: (layer, be[i], 0, 0)),
        ],
        out_specs=pl.BlockSpec((MOE_ROWS, half), lambda i, be, br, bs: (bs[i], 0)),
        scratch_shapes=[pltpu.VMEM((d, de), BF16), pltpu.VMEM((d, de), BF16), pltpu.VMEM((de, d), BF16)],
    )
    return pl.pallas_call(
        _expert_kernel,
        grid_spec=grid_spec,
        out_shape=jax.ShapeDtypeStruct((rows, half), jnp.uint32),
        compiler_params=_cparams("arbitrary"),
        name="moe_experts",
    )(block_e, block_rows, block_src, xs, w_gate, w_up, w_down)


def _combine_kernel(x_ref, m_ref, y_ref, w_ref, fg_ref, o_ref):
    x = x_ref[...] + _moe_residual(y_ref, w_ref, m_ref[5])
    o_ref[...] = _rms(x) * fg_ref[...]


def _combine(x, mods, layer, segs, seg, yg, wcol, final_g):
    n, d = x.shape
    tm = WIDE_TILE
    mod_row = _wide_mod_row(segs, tm)
    t0 = seg.row0 // tm
    steps = seg.rows // tm
    in_specs = [pl.BlockSpec((tm, d), lambda i: (t0 + i, 0)),
                pl.BlockSpec((None, None, 6, 1, d), lambda i: (layer, mod_row(t0 + i), 0, 0, 0)),
                pl.BlockSpec((2, tm, d // 2), lambda i: (0, i, 0)),
                pl.BlockSpec((tm, 8), lambda i: (t0 + i, 0)),
                pl.BlockSpec((1, d), lambda i: (0, 0))]
    return pl.pallas_call(
        _combine_kernel,
        grid=(steps,),
        in_specs=in_specs,
        out_specs=pl.BlockSpec((tm, d), lambda i: (i, 0)),
        out_shape=jax.ShapeDtypeStruct((seg.rows, d), F32),
        compiler_params=_cparams("parallel"),
        name="moe_combine_final",
    )(x, mods, yg, wcol, final_g)


def _sc_mesh():
    return plsc.VectorSubcoreMesh(core_axis_name="c", subcore_axis_name="s")


def _sc_worker_split(n):
    workers = SC_CORES * SC_SUBCORES
    per = n // workers
    assert per * workers == n and per % SC_CHUNK == 0
    return workers, per, per // SC_CHUNK


def _sc_dispatch(h, pos, rows):
    n, w = h.shape
    workers, per, chunks = _sc_worker_split(n)

    @functools.partial(
        pl.kernel, out_type=jax.ShapeDtypeStruct((rows, w), h.dtype), mesh=_sc_mesh(),
        scratch_types=[pltpu.VMEM((2, chunks, SC_CHUNK), jnp.int32), pltpu.VMEM((SC_CHUNK, w), h.dtype)],
        name="moe_dispatch_scatter")
    def scatter_rows(h_hbm, pos_hbm, xs_hbm, idx_v, rows_v):
        wid = lax.axis_index("s") * SC_CORES + lax.axis_index("c")
        pltpu.sync_copy(pos_hbm.at[0, wid], idx_v.at[0])
        pltpu.sync_copy(pos_hbm.at[1, wid], idx_v.at[1])

        @pl.loop(0, chunks)
        def _(c):
            pltpu.sync_copy(h_hbm.at[pl.ds(wid * per + c * SC_CHUNK, SC_CHUNK)], rows_v)
            pltpu.sync_copy(rows_v, xs_hbm.at[idx_v.at[0, c]])
            pltpu.sync_copy(rows_v, xs_hbm.at[idx_v.at[1, c]])

    return scatter_rows(h, pos.reshape(2, workers, chunks, SC_CHUNK))


def _sc_gather2(ys, pos):
    _, w = ys.shape
    n = pos.shape[1]
    workers, per, chunks = _sc_worker_split(n)

    @functools.partial(
        pl.kernel, out_type=jax.ShapeDtypeStruct((2, n, w), ys.dtype), mesh=_sc_mesh(),
        scratch_types=[pltpu.VMEM((2, chunks, SC_CHUNK), jnp.int32), pltpu.VMEM((SC_CHUNK, w), ys.dtype),
                       pltpu.SemaphoreType.DMA],
        name="moe_combine_gather")
    def gather_rows(ys_hbm, pos_hbm, out_hbm, idx_v, rows_v, sem):
        wid = lax.axis_index("s") * SC_CORES + lax.axis_index("c")
        pltpu.sync_copy(pos_hbm.at[0, wid], idx_v.at[0])
        pltpu.sync_copy(pos_hbm.at[1, wid], idx_v.at[1])

        @pl.loop(0, chunks)
        def _(c):
            for k in range(2):
                pltpu.async_copy(ys_hbm.at[idx_v.at[k, c]], rows_v, sem).wait()
                pltpu.sync_copy(rows_v, out_hbm.at[k, pl.ds(wid * per + c * SC_CHUNK, SC_CHUNK)])

    return gather_rows(ys, pos.reshape(2, workers, chunks, SC_CHUNK))


def _dispatch_plan(idx, rank, counts):
    n = idx.shape[1]
    padded = (counts + MOE_ROWS - 1) // MOE_ROWS * MOE_ROWS
    pad_end = jnp.cumsum(padded)
    pad_start = pad_end - padded
    experts = jnp.arange(N_EXPERTS, dtype=jnp.int32)
    start_of = jnp.sum(jnp.where(idx[..., None] == experts, pad_start, 0), axis=-1)
    pos = start_of + rank
    n_blocks = 2 * n // MOE_ROWS + N_EXPERTS
    steps = jnp.arange(n_blocks, dtype=jnp.int32)
    last_used = pad_end[-1] // MOE_ROWS - 1
    step = jnp.minimum(steps, last_used)
    block_e = jnp.minimum(jnp.sum(step[:, None] * MOE_ROWS >= pad_end[None, :], axis=1), N_EXPERTS - 1)
    pick = lambda per_expert: jnp.sum(jnp.where(block_e[:, None] == experts, per_expert, 0), axis=-1)
    first, count = pick(pad_start // MOE_ROWS), jnp.maximum(pick(padded // MOE_ROWS), 1)
    block_src = first + (step - first - 1) % count
    block_rows = jnp.clip(pick(pad_start + counts) - block_src * MOE_ROWS, 0, MOE_ROWS)
    block_rows = jnp.where(steps <= last_used, block_rows, 0)
    return (pos, block_e.astype(jnp.int32), block_rows.astype(jnp.int32), block_src.astype(jnp.int32),
            n_blocks * MOE_ROWS)


def _moe(x, mods, layer, segs, g2, rw_hi, rw_lo, rb, w_gate, w_up, w_down, final_g):
    h2p, idx, rank, wcol, cnt = _router(x, mods, layer, segs, g2, rw_hi, rw_lo, rb)
    pos, block_e, block_rows, block_src, rows = _dispatch_plan(idx, rank, cnt[:, 0].astype(jnp.int32))
    xs = _sc_dispatch(h2p, pos, rows)
    ys = _experts(xs, block_e, block_rows, block_src, w_gate, w_up, w_down, layer)
    yg = {seg: _sc_gather2(ys, pos[:, seg.row0:seg.row0 + seg.rows]) for seg in segs}
    if final_g is None:
        return yg, wcol, mods, layer
    return tuple(_combine(x, mods, layer, segs, seg, yg[seg], wcol, final_g) for seg in segs)


def _mla_rope_tables(t):
    axis_dim = MLA_ROPE // 2
    row = np.repeat(np.arange(t // GRID_W), GRID_W).astype(np.float64)
    col = np.tile(np.arange(GRID_W), t // GRID_W).astype(np.float64)
    inv = ROPE_BASE ** (-np.arange(0, axis_dim, 2, dtype=np.float64) / axis_dim)
    ar, ac = row[:, None] * inv[None, :], col[:, None] * inv[None, :]
    ones = np.ones((t, LANE - MLA_ROPE))
    cos = np.concatenate([np.cos(ar), np.cos(ar), np.cos(ac), np.cos(ac), ones], axis=-1)
    sin = np.concatenate([-np.sin(ar), np.sin(ar), -np.sin(ac), np.sin(ac), 0.0 * ones], axis=-1)
    return jnp.asarray(cos, F32), jnp.asarray(sin, F32)


def _ret_rot_tables(t, dk):
    inv = ROPE_BASE ** (-np.linspace(0.0, 1.0, dk // 2))
    ang = np.arange(t, dtype=np.float64)[:, None] * inv[None, :]
    return jnp.asarray(np.cos(ang), F32), jnp.asarray(np.sin(ang), F32)


def _dft_tables(n):
    k = np.arange(n, dtype=np.int64)
    ang = (np.outer(k, k) % n).astype(np.float64) * (2.0 * math.pi / n)
    return jnp.asarray(np.cos(ang), BF16), jnp.asarray(np.sin(ang), BF16)


def _mla_weights(w_in, q_g, kv_g, w_uq, w_ukv, w_o):
    d = w_in.shape[0]
    hd = MLA_NOPE + MLA_ROPE
    perm = _rope_perm()
    w_in_p = jnp.concatenate([w_in, w_in[:, MLA_Q_LORA + MLA_KV_LORA + perm]], axis=1)
    uq = w_uq.reshape(MLA_Q_LORA, MLA_HEADS, hd)
    uq = jnp.concatenate([uq, uq[..., MLA_NOPE + perm]], axis=-1)
    ukv = w_ukv.reshape(MLA_KV_LORA, MLA_HEADS, MLA_NOPE + MLA_V)
    return {
        "w_in": w_in_p.astype(BF16),
        "q_g": q_g.reshape(1, -1) * (MLA_NOPE + MLA_ROPE) ** -0.5,
        "kv_g": kv_g.reshape(1, -1),
        "w_q": uq.reshape(MLA_Q_LORA, MLA_HEADS * 2 * LANE).astype(BF16),
        "w_kn": ukv[..., :MLA_NOPE].reshape(MLA_KV_LORA, MLA_HEADS * MLA_NOPE).astype(BF16),
        "w_v": ukv[..., MLA_NOPE:].reshape(MLA_KV_LORA, MLA_HEADS * MLA_V).astype(BF16),
        "w_o": w_o.astype(BF16),
    }


def kernel(x_prompt, x_sample, cache_mla, state_ret, c, c_ctx, norm1_g, norm2_g, ada_w, ada_b, final_norm_g,
           mla_w_in, mla_q_norm_g, mla_kv_norm_g, mla_w_uq, mla_w_ukv, mla_w_o, ret_w_in, ret_decay_f,
           ret_decay_b, ret_w_o, fnet_w, router_w, router_b, moe_w_gate, moe_w_up, moe_w_down):
    b_ctx, t_ctx, d = x_prompt.shape
    b_lat, t_lat, _ = x_sample.shape
    depth = ada_w.shape[0]
    assert b_lat + 1 <= 8
    n_ctx = b_ctx * t_ctx
    ctx = _Seg(0, b_ctx, t_ctx, 0, False)
    lat = _Seg(n_ctx, b_lat, t_lat, 1, True)
    segs = (ctx, lat)

    n_lat = b_lat * t_lat
    n_mla = mla_w_in.shape[0]
    assert n_mla >= 1
    x = None
    new_cache = jnp.zeros((b_ctx, n_mla, t_ctx, MLA_KV_LORA + MLA_ROPE), F32)
    cond8 = jnp.concatenate([c_ctx[None, :], c, jnp.zeros((8 - 1 - b_lat, d), F32)], axis=0)
    mods = _modulation_all(cond8, ada_w, ada_b).reshape(depth, 8, 6, 1, d)

    rw_t = router_w.T.astype(F32)
    rw_hi = rw_t.astype(BF16)
    rw_lo = (rw_t - rw_hi.astype(F32)).astype(BF16)
    rb = router_b.reshape(N_EXPERTS, 1).astype(F32)
    final_g = final_norm_g.reshape(1, d)
    dk = ret_w_in.shape[2] // (8 * RET_HEADS)
    dv = 2 * dk

    states = []
    pending = None
    counters = [0, 0, 0]
    for layer in range(depth):
        kind = layer % 3
        j = counters[kind]
        counters[kind] += 1
        g1 = norm1_g[layer].reshape(1, d)
        g2 = norm2_g[layer].reshape(1, d)
        if kind == 0:
            w = _mla_weights(mla_w_in[j], mla_q_norm_g[j], mla_kv_norm_g[j], mla_w_uq[j], mla_w_ukv[j],
                             mla_w_o[j])
            if x is None:
                xc, xc0, xl, xl0 = x_prompt.reshape(n_ctx, d), 0, x_sample.reshape(n_lat, d), 0
            else:
                xc, xc0, xl, xl0 = x, ctx.row0, x, lat.row0
            past = cache_mla.shape[2]
            cpad = jnp.pad(cache_mla[:, j].reshape(b_lat * past, -1), ((0, 0), (0, LANE - MLA_ROPE)))
            kp, vp = _cache_kv(cpad, w)
            if pending is None:
                qc, kc, vc, new_cache = _mla_proj(xc, xc0, mods, layer, ctx, g1, w, None, (new_cache, j, n_mla))
                ql, kl, vl = _mla_proj(xl, xl0, mods, layer, lat, g1, w, _mla_rope_tables(t_lat))
            else:
                qc, kc, vc, new_cache, x = _mla_proj(x, ctx.row0, mods, layer, ctx, g1, w, None,
                                                     (new_cache, j, n_mla), pending)
                ql, kl, vl, x = _mla_proj(x, lat.row0, mods, layer, lat, g1, w, _mla_rope_tables(t_lat), None,
                                          pending)
                xc = xl = x
            first = x is None
            x = _attention(xc, xc0, n_ctx + n_lat, jnp.zeros((n_ctx + n_lat, d), F32) if first else "inplace",
                           mods, layer, ctx, qc, [(kc, vc, t_ctx)], w["w_o"])
            x = _attention(xl if first else x, xl0, n_ctx + n_lat, x if first else "inplace", mods, layer, lat,
                           ql, [(kl, vl, t_lat), (kp, vp, past)], w["w_o"])
        elif kind == 1:
            w_in = ret_w_in[j]
            qk = RET_HEADS * dk
            w_in_b = w_in.astype(BF16)
            w_o_b = ret_w_o[j].astype(BF16)
            rot = _ret_rot_tables(t_lat, dk)
            n_qk, n_v = 2 * qk // RET_COL, RET_HEADS * dv // RET_COL
            assert (n_qk + n_v) * RET_COL * 2 == w_in.shape[1]
            parts = []
            for seg in segs:
                kinds = (("rotary" if seg is lat else "plain"),) * n_qk + ("plain",) * n_v
                qkv = _ret_proj(x, mods, layer, seg, g1, w_in_b, 0, kinds, rot, dk, pending)
                if pending is not None:
                    qkv, x = qkv
                parts.append((qkv, _ret_proj(x, mods, layer, seg, g1, w_in_b, 1, ("silu",) * (n_qk + n_v), None, dk)))
            x, s_ctx = _ret_scan(*parts[0], ctx, ret_decay_f[j], ret_decay_b[j], None, True, dk, dv, RET_HEADS,
                                 out_proj=(x, mods, layer, w_o_b))
            yl, _ = _ret_scan(*parts[1], lat, ret_decay_f[j], ret_decay_b[j], state_ret[:, j], False, dk, dv, 1)
            x = _matmul_residual(x, mods, layer, lat, yl, w_o_b)
            states.append(s_ctx)
        else:
            gd = d // FNET_GROUPS
            cc, sc = _dft_tables(gd)
            cs = jnp.concatenate([cc, sc], axis=1)
            w_b = fnet_w[j].astype(BF16)
            for seg in segs:
                ct, st = _dft_tables(seg.seq)
                if seg.seq <= FNET_ROWS:
                    x = _fnet_short(x, mods, layer, seg, g1, cs, ct, st, w_b, (seg.seq * gd) ** -0.5, pending)
                    continue
                if pending is None:
                    ac, as_ = _fnet_a(x, mods, layer, seg, g1, cs)
                else:
                    ac, as_, x = _fnet_a(x, mods, layer, seg, g1, cs, pending)
                x = _fnet_b(x, mods, layer, seg, ac, as_, ct, st, w_b, (seg.seq * gd) ** -0.5)
        if layer < depth - 1:
            pending = _moe(x, mods, layer, segs, g2, rw_hi, rw_lo, rb, moe_w_gate, moe_w_up, moe_w_down, None)
        else:
            y_prompt, y_sample = _moe(x, mods, layer, segs, g2, rw_hi, rw_lo, rb, moe_w_gate, moe_w_up,
                                      moe_w_down, final_g)

    new_state = jnp.stack(states, axis=1)
    return (y_prompt.reshape(b_ctx, t_ctx, d), y_sample.reshape(b_lat, t_lat, d), new_cache, new_state)
```

```python
import functools
import math

import jax
import jax.numpy as jnp
import numpy as np
from jax import lax
from jax.experimental import pallas as pl
from jax.experimental.pallas import tpu as pltpu
from jax.experimental.pallas import tpu_sc as plsc

F32 = jnp.float32
BF16 = jnp.bfloat16

GRID_W = 64
MLA_HEADS = 8
MLA_NOPE = 128
MLA_ROPE = 64
MLA_V = 128
MLA_Q_LORA = 384
MLA_KV_LORA = 256
ROPE_BASE = 10000.0
RET_HEADS = 4
RET_CHUNK = 256
FNET_GROUPS = 4
N_EXPERTS = 16
N_EXPERT_GROUPS = 4
EXPERTS_PER_GROUP = 4
D_EXPERT = 512
NORM_EPS = 1e-6

LANE = 128
PROJ_ROWS = 512
ATTN_ROWS = 512
FNET_ROWS = 512
ATTN_SHORT_SEQS = 4
WIDE_TILE = 1024
ROUTER_SUB_ROWS = 256
MOE_ROWS = 1024
MOE_TAIL_ROWS = 256
VMEM_LIMIT = 56 * 1024 * 1024
SC_CORES = 2
SC_SUBCORES = 16
SC_CHUNK = 128


def _cparams(*sem):
    return pltpu.CompilerParams(dimension_semantics=sem, vmem_limit_bytes=VMEM_LIMIT)


def _sigmoid(x):
    return 1.0 / (1.0 + jnp.exp(-x))


def _rms(x):
    return x * lax.rsqrt(jnp.mean(x * x, axis=-1, keepdims=True) + NORM_EPS)


def _modulate(x, g, shift, scale):
    return _rms(x) * (g * (1.0 + scale)) + shift


def _dot(a, b):
    return jnp.dot(a, b, preferred_element_type=F32)


def _dot_nt(a, b):
    return lax.dot_general(a, b, (((1,), (1,)), ((), ())), preferred_element_type=F32)


def _cast_kernel(w_ref, o_ref):
    o_ref[...] = w_ref[...].astype(BF16)


def _to_bf16(w):
    rows, cols = w.shape
    tr = 128
    assert rows % tr == 0
    return pl.pallas_call(
        _cast_kernel,
        grid=(rows // tr,),
        in_specs=[pl.BlockSpec((tr, cols), lambda i: (i, 0))],
        out_specs=pl.BlockSpec((tr, cols), lambda i: (i, 0)),
        out_shape=jax.ShapeDtypeStruct((rows, cols), BF16),
        compiler_params=_cparams("parallel"),
        name="weight_cast",
    )(w)


def _mod_kernel(c_ref, w_ref, b_ref, o_ref):
    c = c_ref[...]
    s = (c * _sigmoid(c)).astype(BF16)
    o_ref[...] = _dot(s, w_ref[...].astype(BF16)) + b_ref[...]


def _modulation_all(cond8, ada_w, ada_b):
    depth, d, d6 = ada_w.shape
    tn = d6 // 4
    return pl.pallas_call(
        _mod_kernel,
        grid=(depth, d6 // tn),
        in_specs=[
            pl.BlockSpec((8, d), lambda l, n: (0, 0)),
            pl.BlockSpec((None, d, tn), lambda l, n: (l, 0, n)),
            pl.BlockSpec((None, 1, tn), lambda l, n: (l, 0, n)),
        ],
        out_specs=pl.BlockSpec((None, 8, tn), lambda l, n: (l, 0, n)),
        out_shape=jax.ShapeDtypeStruct((depth, 8, d6), F32),
        compiler_params=_cparams("parallel", "parallel"),
        name="modulation",
    )(cond8, ada_w, ada_b.reshape(depth, 1, d6))


class _Seg:
    def __init__(self, row0, batch, seq, mod0, per_batch_mod):
        self.row0, self.batch, self.seq = row0, batch, seq
        self.mod0, self.per_batch_mod = mod0, per_batch_mod
        self.rows = batch * seq

    def tile(self, want):
        tm = min(want, self.seq) if self.per_batch_mod else want
        assert self.rows % tm == 0 and self.row0 % tm == 0 and (self.seq % tm == 0 or tm % self.seq == 0)
        return tm

    def seq_tile(self, want):
        tm = min(want, self.seq)
        assert self.seq % tm == 0 and self.row0 % tm == 0
        return tm

    def mod_row(self, tile, tm):
        if self.per_batch_mod:
            return self.mod0 + tile * tm // self.seq
        return self.mod0


def _mod_spec(layer, seg, d, tm, tile_of=lambda *a: a[0]):
    return pl.BlockSpec((None, None, 6, 1, d), lambda *a: (layer, seg.mod_row(tile_of(*a), tm), 0, 0, 0))


def _unpack_halves(p):
    lo = lax.bitcast_convert_type(p << 16, F32)
    hi = lax.bitcast_convert_type(p & jnp.uint32(0xFFFF0000), F32)
    return lo, hi


def _moe_residual(y_ref, w_ref, gate):
    w = w_ref[...]
    lo0, hi0 = _unpack_halves(y_ref[0])
    lo1, hi1 = _unpack_halves(y_ref[1])
    w0, w1 = w[:, 0:1], w[:, 1:2]
    return gate * jnp.concatenate([w0 * lo0 + w1 * lo1, w0 * hi0 + w1 * hi1], axis=-1)


def _take_pending(refs, pending):
    if not pending:
        return refs, lambda x: x
    y_ref, w_ref, pm_ref, *rest = refs
    xo_ref = rest.pop()

    def resolve(x):
        x = x + _moe_residual(y_ref, w_ref, pm_ref[5])
        xo_ref[...] = x
        return x

    return rest, resolve


def _pending_io(pending, seg, tm, d, n):
    yg_by_seg, wcol, mods, layer = pending
    t0 = seg.row0 // tm
    specs = [pl.BlockSpec((2, tm, d // 2), lambda i: (0, i, 0)),
             pl.BlockSpec((tm, 8), lambda i: (t0 + i, 0)),
             _mod_spec(layer, seg, d, tm)]
    return (specs, [yg_by_seg[seg], wcol, mods], pl.BlockSpec((tm, d), lambda i: (t0 + i, 0)),
            jax.ShapeDtypeStruct((n, d), F32))


def _rope_partner(x):
    return pltpu.roll(x, LANE // 2, 1)


def _rope_perm():
    return np.array([l + 16 if l % 32 < 16 else l - 16 for l in range(MLA_ROPE)])


def _store_values(v_ref, v):
    ones = jnp.ones((v.shape[0], LANE), BF16)
    for hd in range(MLA_HEADS):
        v_ref[:, hd * 2 * LANE:hd * 2 * LANE + LANE] = v[:, hd * MLA_V:(hd + 1) * MLA_V].astype(BF16)
        v_ref[:, hd * 2 * LANE + LANE:(hd + 1) * 2 * LANE] = ones


def _mla_proj_kernel(*refs, rope, pending):
    refs, resolve = _take_pending(refs, pending)
    if rope:
        (x_ref, m_ref, g_ref, win_ref, qg_ref, kvg_ref, wq_ref, wkn_ref, wv_ref, cos_ref, sin_ref,
         q_ref, k_ref, v_ref) = refs
    else:
        x_ref, m_ref, g_ref, win_ref, qg_ref, kvg_ref, wq_ref, wkn_ref, wv_ref = refs[:9]
        q_ref, k_ref, v_ref, cache_ref = refs[-4:]
    h = _modulate(resolve(x_ref[...]), g_ref[...], m_ref[0], m_ref[1]).astype(BF16)
    z = _dot(h, win_ref[...])
    cq = z[:, :MLA_Q_LORA]
    ckv = z[:, MLA_Q_LORA:MLA_Q_LORA + MLA_KV_LORA]
    kpe = z[:, MLA_Q_LORA + MLA_KV_LORA:]
    cqn = (_rms(cq) * qg_ref[...]).astype(BF16)
    ckvn = _rms(ckv) * kvg_ref[...]
    ckvb = ckvn.astype(BF16)
    q = _dot(cqn, wq_ref[...])
    kn = _dot(ckvb, wkn_ref[...])
    _store_values(v_ref, _dot(ckvb, wv_ref[...]))
    if rope:
        cos, sin = cos_ref[...], sin_ref[...]
        kpe = kpe * cos + _rope_partner(kpe) * sin
    kpe = jnp.where(lax.broadcasted_iota(jnp.int32, kpe.shape, 1) < MLA_ROPE, kpe, 0.0)
    if not rope:
        seq = cache_ref.shape[1]
        for s in range(cache_ref.shape[0]):
            cache_ref[s, :, :MLA_KV_LORA] = ckvn[s * seq:(s + 1) * seq, :]
            cache_ref[s, :, MLA_KV_LORA:] = kpe[s * seq:(s + 1) * seq, :MLA_ROPE]
    kpe_b = kpe.astype(BF16)
    for hd in range(MLA_HEADS):
        lo = hd * 2 * LANE
        q_ref[:, lo:lo + LANE] = q[:, lo:lo + LANE].astype(BF16)
        qr = q[:, lo + LANE:lo + 2 * LANE]
        if rope:
            qr = qr * cos + _rope_partner(qr) * sin
        q_ref[:, lo + LANE:lo + 2 * LANE] = qr.astype(BF16)
        k_ref[:, lo:lo + LANE] = kn[:, hd * LANE:(hd + 1) * LANE].astype(BF16)
        k_ref[:, lo + LANE:lo + 2 * LANE] = kpe_b


def _mla_proj(x, x_row0, mods, layer, seg, g1, w, rope_tabs, cache_slot=None, pending=None):
    n, d = x.shape
    rope = rope_tabs is not None
    rows = seg.rows
    tm = seg.tile(PROJ_ROWS)
    x_tile0 = x_row0 // tm
    hq = MLA_HEADS * 2 * LANE
    const = lambda i: (0, 0)
    aliases = {}
    in_specs = [
        pl.BlockSpec((tm, d), lambda i: (x_tile0 + i, 0)),
        _mod_spec(layer, seg, d, tm),
        pl.BlockSpec((1, d), const),
        pl.BlockSpec(w["w_in"].shape, const),
        pl.BlockSpec((1, MLA_Q_LORA), const),
        pl.BlockSpec((1, MLA_KV_LORA), const),
        pl.BlockSpec(w["w_q"].shape, const),
        pl.BlockSpec(w["w_kn"].shape, const),
        pl.BlockSpec(w["w_v"].shape, const),
    ]
    args = [x, mods, g1, w["w_in"], w["q_g"], w["kv_g"], w["w_q"], w["w_kn"], w["w_v"]]
    out_specs = [pl.BlockSpec((tm, hq), lambda i: (i, 0))] * 3
    out_shape = [jax.ShapeDtypeStruct((rows, hq), BF16)] * 3
    if rope:
        tab = pl.BlockSpec((tm, LANE), lambda i: (i % (seg.seq // tm), 0))
        in_specs += [tab, tab]
        args += list(rope_tabs)
    else:
        cw = MLA_KV_LORA + MLA_ROPE
        prev, slot, n_slots = cache_slot
        assert tm % seg.seq == 0
        out_specs.append(pl.BlockSpec((tm // seg.seq, None, seg.seq, cw), lambda i: (i, slot, 0, 0)))
        out_shape.append(jax.ShapeDtypeStruct((seg.batch, n_slots, seg.seq, cw), F32))
        in_specs.append(pl.BlockSpec(memory_space=pl.ANY))
        args.append(prev)
        aliases = {len(args) - 1: 3}
    if pending is not None:
        p_specs, p_args, xo_spec, xo_shape = _pending_io(pending, seg, tm, d, n)
        in_specs, args = p_specs + in_specs, p_args + args
        out_specs.append(xo_spec)
        out_shape.append(xo_shape)
        aliases = {k + len(p_args): v for k, v in aliases.items()}
        aliases[len(p_args)] = len(out_shape) - 1
    return pl.pallas_call(
        functools.partial(_mla_proj_kernel, rope=rope, pending=pending is not None),
        grid=(rows // tm,),
        in_specs=in_specs,
        out_specs=out_specs,
        out_shape=out_shape,
        input_output_aliases=aliases,
        compiler_params=_cparams("parallel"),
        name="mla_proj_lat" if rope else "mla_proj_ctx",
    )(*args)


def _cache_kv_kernel(c_ref, wkn_ref, wv_ref, k_ref, v_ref):
    c = c_ref[...]
    ckv = c[:, :MLA_KV_LORA].astype(BF16)
    kpe_b = c[:, MLA_KV_LORA:].astype(BF16)
    kn = _dot(ckv, wkn_ref[...])
    _store_values(v_ref, _dot(ckv, wv_ref[...]))
    for hd in range(MLA_HEADS):
        lo = hd * 2 * LANE
        k_ref[:, lo:lo + LANE] = kn[:, hd * LANE:(hd + 1) * LANE].astype(BF16)
        k_ref[:, lo + LANE:lo + 2 * LANE] = kpe_b


def _cache_kv(cache_pad, w):
    rows, cw = cache_pad.shape
    hq = MLA_HEADS * 2 * LANE
    const = lambda i: (0, 0)
    tm = min(PROJ_ROWS, rows)
    assert rows % tm == 0
    return pl.pallas_call(
        _cache_kv_kernel,
        grid=(rows // tm,),
        in_specs=[
            pl.BlockSpec((tm, cw), lambda i: (i, 0)),
            pl.BlockSpec(w["w_kn"].shape, const),
            pl.BlockSpec(w["w_v"].shape, const),
        ],
        out_specs=[pl.BlockSpec((tm, hq), lambda i: (i, 0))] * 2,
        out_shape=[jax.ShapeDtypeStruct((rows, hq), BF16)] * 2,
        compiler_params=_cparams("parallel"),
        name="mla_cache_kv",
    )(cache_pad, w["w_kn"], w["w_v"])


def _attn_kernel(*refs, n_parts, n_seq):
    q_ref = refs[0]
    kv_refs = refs[1:1 + 2 * n_parts]
    wo_ref, x_ref, m_ref = refs[1 + 2 * n_parts:4 + 2 * n_parts]
    o_ref, acc_ref = refs[-2:]
    tq = q_ref.shape[0] // n_seq
    for sq, hd in [(sq, hd) for sq in range(n_seq) for hd in range(MLA_HEADS)]:
        rows = slice(sq * tq, (sq + 1) * tq)
        kcol = slice(hd * 2 * LANE, (hd + 1) * 2 * LANE)
        keys = [slice(sq * (r.shape[0] // n_seq), (sq + 1) * (r.shape[0] // n_seq)) for r in kv_refs[::2]]
        scores = [_dot_nt(q_ref[rows, kcol], kv_refs[2 * p][keys[p], kcol]) for p in range(n_parts)]
        mx = scores[0].max(axis=-1, keepdims=True)
        for s in scores[1:]:
            mx = jnp.maximum(mx, s.max(axis=-1, keepdims=True))
        out = None
        for p, s in enumerate(scores):
            e = jnp.exp((s - mx).astype(BF16))
            pv = _dot(e, kv_refs[2 * p + 1][keys[p], kcol])
            out = pv if out is None else out + pv
        acc_ref[rows, hd * MLA_V:(hd + 1) * MLA_V] = (out[:, :MLA_V] / out[:, LANE:LANE + MLA_V]).astype(BF16)
    y = _dot(acc_ref[...], wo_ref[...])
    o_ref[...] = x_ref[...] + m_ref[2] * y


def _attention(x, x_row0, n, dest, mods, layer, seg, q, kv_parts, w_o):
    d = x.shape[1]
    hq = MLA_HEADS * 2 * LANE
    hv = MLA_HEADS * MLA_V
    tq = seg.seq_tile(ATTN_ROWS)
    tps = seg.seq // tq
    n_seq = ATTN_SHORT_SEQS if (tps == 1 and not seg.per_batch_mod and seg.batch % ATTN_SHORT_SEQS == 0) else 1
    tq *= n_seq
    x_tile0, out_tile0 = x_row0 // tq, seg.row0 // tq
    in_specs = [pl.BlockSpec((tq, hq), lambda b, i: (b * tps + i, 0))]
    args = [q]
    for k, v, rows in kv_parts:
        mode = dict(pipeline_mode=pl.Buffered(1)) if tps > 1 else {}
        in_specs += [pl.BlockSpec((n_seq * rows, hq), lambda b, i: (b, 0)),
                     pl.BlockSpec((n_seq * rows, hq), lambda b, i: (b, 0), **mode)]
        args += [k, v]
    in_specs += [
        pl.BlockSpec(w_o.shape, lambda b, i: (0, 0)),
        pl.BlockSpec((tq, d), lambda b, i: (x_tile0 + b * tps + i, 0)),
        _mod_spec(layer, seg, d, tq // n_seq, tile_of=lambda b, i: (b * tps + i) * n_seq),
    ]
    args += [w_o, x, mods]
    if isinstance(dest, str):
        assert dest == "inplace"
        aliases = {len(args) - 2: 0}
    else:
        in_specs.append(pl.BlockSpec(memory_space=pl.ANY))
        args.append(dest)
        aliases = {len(args) - 1: 0}
    return pl.pallas_call(
        functools.partial(_attn_kernel, n_parts=len(kv_parts), n_seq=n_seq),
        grid=(seg.batch // n_seq, tps),
        in_specs=in_specs,
        out_specs=pl.BlockSpec((tq, d), lambda b, i: (out_tile0 + b * tps + i, 0)),
        out_shape=jax.ShapeDtypeStruct((n, d), F32),
        scratch_shapes=[pltpu.VMEM((tq, hv), BF16)],
        input_output_aliases=aliases,
        compiler_params=_cparams("parallel", "arbitrary"),
        name="mla_attention",
    )(*args)


RET_COL = 1024


RET_ROWS = 1024


def _ret_proj_kernel(*refs, kinds, dk, pending):
    refs, resolve = _take_pending(refs, pending)
    rotary = "rotary" in kinds
    if rotary:
        x_ref, m_ref, g_ref, w_ref, cos_ref, sin_ref, z_ref = refs
        cos, sin = cos_ref[...], sin_ref[...]
    else:
        x_ref, m_ref, g_ref, w_ref, z_ref = refs
    h = _modulate(resolve(x_ref[...]), g_ref[...], m_ref[0], m_ref[1]).astype(BF16)
    half = dk // 2
    for j, kind in enumerate(kinds):
        c0 = j * RET_COL
        acc = _dot(h, w_ref[:, c0:c0 + RET_COL])
        if kind == "rotary":
            for hd in range(RET_COL // dk):
                lo = hd * dk
                x1, x2 = acc[:, lo:lo + half], acc[:, lo + half:lo + dk]
                z_ref[:, c0 + lo:c0 + lo + half] = (x1 * cos - x2 * sin).astype(BF16)
                z_ref[:, c0 + lo + half:c0 + lo + dk] = (x1 * sin + x2 * cos).astype(BF16)
        elif kind == "silu":
            z_ref[:, c0:c0 + RET_COL] = (acc * _sigmoid(acc)).astype(BF16)
        else:
            z_ref[:, c0:c0 + RET_COL] = acc.astype(BF16)


def _ret_proj(x, mods, layer, seg, g1, w_in, group, kinds, rot_tabs, dk, pending=None):
    n, d = x.shape
    rows = seg.rows
    tm = seg.tile(RET_ROWS)
    ncol = len(kinds) * RET_COL
    tps = max(seg.seq // tm, 1)
    in_specs = [
        pl.BlockSpec((tm, d), lambda i: (seg.row0 // tm + i, 0)),
        _mod_spec(layer, seg, d, tm),
        pl.BlockSpec((1, d), lambda i: (0, 0)),
        pl.BlockSpec((d, ncol), lambda i: (0, group), pipeline_mode=pl.Buffered(1)),
    ]
    args = [x, mods, g1, w_in]
    if "rotary" in kinds:
        tab = pl.BlockSpec((tm, dk // 2), lambda i: (i % tps, 0))
        in_specs += [tab, tab]
        args += list(rot_tabs)
    out_specs = [pl.BlockSpec((tm, ncol), lambda i: (i, 0))]
    out_shape = [jax.ShapeDtypeStruct((rows, ncol), BF16)]
    aliases = {}
    if pending is not None:
        p_specs, p_args, xo_spec, xo_shape = _pending_io(pending, seg, tm, d, n)
        in_specs, args = p_specs + in_specs, p_args + args
        out_specs.append(xo_spec)
        out_shape.append(xo_shape)
        aliases = {len(p_args): 1}
    res = pl.pallas_call(
        functools.partial(_ret_proj_kernel, kinds=kinds, dk=dk, pending=pending is not None),
        grid=(rows // tm,),
        in_specs=in_specs,
        out_specs=out_specs,
        out_shape=out_shape,
        input_output_aliases=aliases,
        compiler_params=_cparams("parallel"),
        name="ret_proj_" + kinds[0],
    )(*args)
    return res if pending is not None else res[0]


def _log_sigmoid(x):
    return jnp.minimum(x, 0.0) - jnp.log(1.0 + jnp.exp(-jnp.abs(x)))


def _ret_scan_kernel(*refs, has_s0, emit_state, n_chunks, heads, fuse_out):
    refs = list(refs)
    lf_ref, lb_ref, q_ref, k_ref, v_ref, gf_ref, gb_ref = refs[:7]
    pos = 7
    s0_ref = None
    if has_s0:
        s0_ref = refs[pos]
        pos += 1
    if fuse_out:
        wo_ref, x_ref, m_ref, o_ref = refs[pos:pos + 4]
        pos += 4
    else:
        y_ref = refs[pos]
        pos += 1
    sout_ref = None
    if emit_state:
        sout_ref = refs[pos]
        pos += 1
    if fuse_out:
        s_ref, yf_ref, y_ref = refs[pos:]
    else:
        s_ref, yf_ref = refs[pos:]
    c = RET_CHUNK
    dk, dv = s_ref.shape
    ii = lax.broadcasted_iota(jnp.int32, (c, c), 0).astype(F32)
    jj = lax.broadcasted_iota(jnp.int32, (c, c), 1).astype(F32)
    idx = lax.broadcasted_iota(jnp.int32, (c, 1), 0).astype(F32)

    for hd, direction in [(hd, direction) for hd in range(heads) for direction in range(2)]:
        fwd = direction == 0
        kcol, vcol = slice(hd * dk, (hd + 1) * dk), slice(hd * dv, (hd + 1) * dv)
        lg = _log_sigmoid((lf_ref if fwd else lb_ref)[hd])
        rel = (ii - jj) if fwd else (jj - ii)
        keep = rel >= 0
        decay_in = jnp.where(keep, jnp.exp(jnp.where(keep, rel, 0.0) * lg), 0.0) * dk ** -0.5
        decay_q = jnp.exp(((idx + 1.0) if fwd else (c - idx)) * lg)
        decay_k = jnp.exp(((c - 1.0 - idx) if fwd else idx) * lg) * dk ** -0.5
        decay_c = jnp.exp(c * lg)
        g_ref = gf_ref if fwd else gb_ref

        def chunk(cc, state, fwd=fwd, decay_in=decay_in, decay_q=decay_q, decay_k=decay_k, decay_c=decay_c,
                  g_ref=g_ref, kcol=kcol, vcol=vcol):
            r0 = cc * c if isinstance(cc, int) else pl.multiple_of(cc * c, c)
            qc = q_ref[pl.ds(r0, c), kcol]
            kc = k_ref[pl.ds(r0, c), kcol]
            vc = v_ref[pl.ds(r0, c), vcol]
            sc = _dot_nt(qc, kc) * decay_in
            out = _dot(sc.astype(BF16), vc)
            kd_t = (kc.astype(F32) * decay_k).T.astype(BF16)
            new_s = _dot(kd_t, vc)
            if state is not None:
                out = out + decay_q * _dot(qc, state.astype(BF16))
                new_s = decay_c * state + new_s
            s_ref[...] = new_s
            o = _rms(out) * g_ref[pl.ds(r0, c), vcol].astype(F32)
            if fwd:
                yf_ref[pl.ds(r0, c), :] = o
            else:
                y_ref[pl.ds(r0, c), vcol] = (yf_ref[pl.ds(r0, c), :] + o).astype(BF16)

        chunk(0 if fwd else n_chunks - 1, s0_ref[direction, hd] if has_s0 else None)

        def step(ci, carry, fwd=fwd, chunk=chunk):
            chunk(ci if fwd else n_chunks - 1 - ci, s_ref[...])
            return carry

        lax.fori_loop(1, n_chunks, step, 0, unroll=True)
        if emit_state:
            sout_ref[direction, hd] = s_ref[...]
    if fuse_out:
        o_ref[...] = x_ref[...] + m_ref[2] * _dot(y_ref[...], wo_ref[...])


def _ret_scan(qkv, g, seg, logit_f, logit_b, s0, emit_state, dk, dv, heads, out_proj=None):
    rows = seg.batch * seg.seq
    t = seg.seq
    hh = RET_HEADS
    groups = hh // heads
    fuse_out = out_proj is not None
    assert not fuse_out or groups == 1
    v0 = 2 * hh * dk // (heads * dv)
    assert groups * heads == hh and v0 * heads * dv == 2 * hh * dk
    in_specs = [
        pl.BlockSpec((heads, 1, 1), lambda b, h: (h, 0, 0)),
        pl.BlockSpec((heads, 1, 1), lambda b, h: (h, 0, 0)),
        pl.BlockSpec((t, heads * dk), lambda b, h: (b, h)),
        pl.BlockSpec((t, heads * dk), lambda b, h: (b, groups + h)),
        pl.BlockSpec((t, heads * dv), lambda b, h: (b, v0 + h)),
        pl.BlockSpec((t, heads * dv), lambda b, h: (b, h)),
        pl.BlockSpec((t, heads * dv), lambda b, h: (b, groups + h)),
    ]
    args = [logit_f.reshape(hh, 1, 1), logit_b.reshape(hh, 1, 1), qkv, qkv, qkv, g, g]
    state_spec = pl.BlockSpec((None, 2, heads, dk, dv), lambda b, h: (b, 0, h, 0, 0))
    if s0 is not None:
        in_specs.append(state_spec)
        args.append(s0)
    scratch = [pltpu.VMEM((dk, dv), F32), pltpu.VMEM((t, dv), F32)]
    aliases = {}
    if fuse_out:
        x, mods, layer, w_o = out_proj
        n, d = x.shape
        x_spec = pl.BlockSpec((t, d), lambda b, h: (seg.row0 // t + b, 0))
        in_specs += [pl.BlockSpec(w_o.shape, lambda b, h: (0, 0)), x_spec,
                     _mod_spec(layer, seg, d, t, tile_of=lambda b, h: b)]
        args += [w_o, x, mods]
        aliases = {len(args) - 2: 0}
        out_specs, out_shape = [x_spec], [jax.ShapeDtypeStruct((n, d), F32)]
        scratch.append(pltpu.VMEM((t, hh * dv), BF16))
    else:
        out_specs = [pl.BlockSpec((t, heads * dv), lambda b, h: (b, h))]
        out_shape = [jax.ShapeDtypeStruct((rows, hh * dv), BF16)]
    if emit_state:
        out_specs.append(state_spec)
        out_shape.append(jax.ShapeDtypeStruct((seg.batch, 2, hh, dk, dv), F32))
    res = pl.pallas_call(
        functools.partial(_ret_scan_kernel, has_s0=s0 is not None, emit_state=emit_state,
                          n_chunks=t // RET_CHUNK, heads=heads, fuse_out=fuse_out),
        grid=(seg.batch, groups),
        in_specs=in_specs,
        out_specs=out_specs,
        out_shape=out_shape,
        scratch_shapes=scratch,
        input_output_aliases=aliases,
        compiler_params=_cparams("parallel", "parallel"),
        name="ret_scan",
    )(*args)
    return res if emit_state else (res[0], None)


def _mm_res_kernel(a_ref, w_ref, x_ref, m_ref, o_ref):
    o_ref[...] = x_ref[...] + m_ref[2] * _dot(a_ref[...], w_ref[...])


def _matmul_residual(x, mods, layer, seg, a, w):
    n, d = x.shape
    tm = seg.tile(RET_ROWS)
    x_spec = pl.BlockSpec((tm, d), lambda i: (seg.row0 // tm + i, 0))
    return pl.pallas_call(
        _mm_res_kernel,
        grid=(seg.rows // tm,),
        in_specs=[
            pl.BlockSpec((tm, a.shape[1]), lambda i: (i, 0)),
            pl.BlockSpec(w.shape, lambda i: (0, 0)),
            x_spec,
            _mod_spec(layer, seg, d, tm),
        ],
        out_specs=x_spec,
        out_shape=jax.ShapeDtypeStruct((n, d), F32),
        input_output_aliases={2: 0},
        compiler_params=_cparams("parallel"),
        name="matmul_residual",
    )(a, w, x, mods)


def _fnet_a_kernel(*refs, gd, pending):
    refs, resolve = _take_pending(refs, pending)
    x_ref, m_ref, g_ref, cs_ref, ac_ref, as_ref = refs
    h = _modulate(resolve(x_ref[...]), g_ref[...], m_ref[0], m_ref[1]).astype(BF16)
    cs = cs_ref[...]
    for g in range(FNET_GROUPS):
        a = _dot(h[:, g * gd:(g + 1) * gd], cs)
        ac_ref[:, g * gd:(g + 1) * gd] = a[:, :gd].astype(BF16)
        as_ref[:, g * gd:(g + 1) * gd] = a[:, gd:].astype(BF16)


def _fnet_a(x, mods, layer, seg, g1, cs, pending=None):
    n, d = x.shape
    rows = seg.rows
    tm = seg.tile(PROJ_ROWS)
    out = pl.BlockSpec((tm, d), lambda i: (i, 0))
    in_specs = [
        pl.BlockSpec((tm, d), lambda i: (seg.row0 // tm + i, 0)),
        _mod_spec(layer, seg, d, tm),
        pl.BlockSpec((1, d), lambda i: (0, 0)),
        pl.BlockSpec(cs.shape, lambda i: (0, 0)),
    ]
    args = [x, mods, g1, cs]
    out_specs, out_shape, aliases = [out, out], [jax.ShapeDtypeStruct((rows, d), BF16)] * 2, {}
    if pending is not None:
        p_specs, p_args, xo_spec, xo_shape = _pending_io(pending, seg, tm, d, n)
        in_specs, args = p_specs + in_specs, p_args + args
        out_specs.append(xo_spec)
        out_shape.append(xo_shape)
        aliases = {len(p_args): 2}
    return pl.pallas_call(
        functools.partial(_fnet_a_kernel, gd=d // FNET_GROUPS, pending=pending is not None),
        grid=(rows // tm,),
        in_specs=in_specs,
        out_specs=out_specs,
        out_shape=out_shape,
        input_output_aliases=aliases,
        compiler_params=_cparams("parallel"),
        name="fnet_channel_dft",
    )(*args)


def _fnet_b_kernel(ct_ref, st_ref, ac_ref, as_ref, w_ref, x_ref, m_ref, o_ref, *, norm):
    f = (_dot(ct_ref[...], ac_ref[...]) - _dot(st_ref[...], as_ref[...])) * norm
    o_ref[...] = x_ref[...] + m_ref[2] * _dot(f.astype(BF16), w_ref[...])


def _fnet_b(x, mods, layer, seg, ac, as_, ct, st, w, norm):
    n, d = x.shape
    t = seg.seq
    tq = seg.seq_tile(FNET_ROWS)
    tps = t // tq
    x_spec = pl.BlockSpec((tq, d), lambda b, i: (seg.row0 // tq + b * tps + i, 0))
    tab = pl.BlockSpec((tq, t), lambda b, i: (i, 0))
    seq = pl.BlockSpec((t, d), lambda b, i: (b, 0))
    return pl.pallas_call(
        functools.partial(_fnet_b_kernel, norm=norm),
        grid=(seg.batch, tps),
        in_specs=[tab, tab, seq, seq, pl.BlockSpec(w.shape, lambda b, i: (0, 0)), x_spec,
                  _mod_spec(layer, seg, d, tq, tile_of=lambda b, i: b * tps + i)],
        out_specs=x_spec,
        out_shape=jax.ShapeDtypeStruct((n, d), F32),
        input_output_aliases={5: 0},
        compiler_params=_cparams("parallel", "arbitrary"),
        name="fnet_position_dft",
    )(ct, st, ac, as_, w, x, mods)


def _fnet_short_kernel(*refs, gd, norm, n_seq, pending):
    if pending:
        y_ref, wr_ref, pm_ref, *refs = refs
    x_ref, m_ref, g_ref, cs_ref, ct_ref, st_ref, w_ref, o_ref, ac_ref, as_ref = refs
    x = x_ref[...]
    if pending:
        x = x + _moe_residual(y_ref, wr_ref, pm_ref[5])
    h = _modulate(x, g_ref[...], m_ref[0], m_ref[1]).astype(BF16)
    cs = cs_ref[...]
    for g in range(FNET_GROUPS):
        a = _dot(h[:, g * gd:(g + 1) * gd], cs)
        ac_ref[:, g * gd:(g + 1) * gd] = a[:, :gd].astype(BF16)
        as_ref[:, g * gd:(g + 1) * gd] = a[:, gd:].astype(BF16)
    t = x.shape[0] // n_seq
    ct, st = ct_ref[...], st_ref[...]
    f = [_dot(ct, ac_ref[s * t:(s + 1) * t, :]) - _dot(st, as_ref[s * t:(s + 1) * t, :]) for s in range(n_seq)]
    f = (f[0] if n_seq == 1 else jnp.concatenate(f, axis=0)) * norm
    o_ref[...] = x + m_ref[2] * _dot(f.astype(BF16), w_ref[...])


def _fnet_short(x, mods, layer, seg, g1, cs, ct, st, w, norm, pending=None):
    n, d = x.shape
    t = seg.seq
    n_seq = ATTN_SHORT_SEQS if (not seg.per_batch_mod and seg.batch % ATTN_SHORT_SEQS == 0) else 1
    rows = n_seq * t
    t0 = seg.row0 // rows
    const = lambda i: (0, 0)
    x_spec = pl.BlockSpec((rows, d), lambda i: (t0 + i, 0))
    in_specs = [x_spec, _mod_spec(layer, seg, d, t, tile_of=lambda i: i * n_seq), pl.BlockSpec((1, d), const),
                pl.BlockSpec(cs.shape, const), pl.BlockSpec((t, t), const), pl.BlockSpec((t, t), const),
                pl.BlockSpec(w.shape, const)]
    args = [x, mods, g1, cs, ct, st, w]
    if pending is not None:
        yg_by_seg, wcol, pmods, player = pending
        in_specs = [pl.BlockSpec((2, rows, d // 2), lambda i: (0, i, 0)),
                    pl.BlockSpec((rows, 8), lambda i: (t0 + i, 0)),
                    _mod_spec(player, seg, d, t, tile_of=lambda i: i * n_seq)] + in_specs
        args = [yg_by_seg[seg], wcol, pmods] + args
    return pl.pallas_call(
        functools.partial(_fnet_short_kernel, gd=d // FNET_GROUPS, norm=norm, n_seq=n_seq,
                          pending=pending is not None),
        grid=(seg.rows // rows,),
        in_specs=in_specs,
        out_specs=x_spec,
        out_shape=jax.ShapeDtypeStruct((n, d), F32),
        scratch_shapes=[pltpu.VMEM((rows, d), BF16), pltpu.VMEM((rows, d), BF16)],
        input_output_aliases={len(args) - 7: 0},
        compiler_params=_cparams("parallel"),
        name="fnet_short",
    )(*args)


def _pack_halves(a):
    w = a.shape[1] // 2
    bits = lambda v: lax.bitcast_convert_type(v.astype(BF16).astype(F32), jnp.uint32)
    return (bits(a[:, :w]) >> 16) | (bits(a[:, w:]) & jnp.uint32(0xFFFF0000))


def _router_kernel(x_ref, m_ref, g_ref, rwhi_ref, rwlo_ref, rb_ref, h_ref, idx_ref, rank_ref, wcol_ref, cnt_ref,
                   run_ref, tri_ref):
    step = pl.program_id(0)

    @pl.when(step == 0)
    def _():
        run_ref[...] = jnp.zeros_like(run_ref)
        tt = tri_ref.shape[0]
        earlier = lax.broadcasted_iota(jnp.int32, (tt, tt), 0) < lax.broadcasted_iota(jnp.int32, (tt, tt), 1)
        tri_ref[...] = jnp.where(earlier, 1.0, 0.0).astype(BF16)

    parts = []
    for r0 in range(0, x_ref.shape[0], ROUTER_SUB_ROWS):
        rows = slice(r0, r0 + ROUTER_SUB_ROWS)
        h = _modulate(x_ref[rows, :], g_ref[...], m_ref[3], m_ref[4])
        h_ref[rows, :] = _pack_halves(h)
        h_hi = h.astype(BF16)
        h_lo = (h - h_hi.astype(F32)).astype(BF16)
        parts.append(_dot_nt(rwhi_ref[...], h_hi) + (_dot_nt(rwhi_ref[...], h_lo) + _dot_nt(rwlo_ref[...], h_hi)))
    logits = jnp.concatenate(parts, axis=1)
    sc = _sigmoid(logits)
    gr = sc + rb_ref[...]
    gp = EXPERTS_PER_GROUP
    row = lambda a, e: a[e:e + 1, :]
    best_g = None
    for g in range(N_EXPERT_GROUPS):
        vals = [row(gr, g * gp + i) for i in range(gp)]
        gs = None
        for i in range(gp):
            for j in range(i + 1, gp):
                pair = vals[i] + vals[j]
                gs = pair if gs is None else jnp.maximum(gs, pair)
        if best_g is None:
            best_g, best_v = jnp.zeros(gs.shape, jnp.int32), gs
        else:
            better = gs > best_v
            best_g = jnp.where(better, g, best_g)
            best_v = jnp.where(better, gs, best_v)
    sel, raw = [], []
    for i in range(gp):
        s_i, r_i = row(gr, i), row(sc, i)
        for g in range(1, N_EXPERT_GROUPS):
            s_i = jnp.where(best_g == g, row(gr, g * gp + i), s_i)
            r_i = jnp.where(best_g == g, row(sc, g * gp + i), r_i)
        sel.append(s_i)
        raw.append(r_i)

    def argmax_first(vals, raws):
        bi, bv, br = jnp.zeros(vals[0].shape, jnp.int32), vals[0], raws[0]
        for i in range(1, len(vals)):
            better = vals[i] > bv
            bi = jnp.where(better, i, bi)
            bv = jnp.where(better, vals[i], bv)
            br = jnp.where(better, raws[i], br)
        return bi, br

    i1, w1 = argmax_first(sel, raw)
    masked = [jnp.where(i1 == i, -jnp.inf, sel[i]) for i in range(gp)]
    i2, w2 = argmax_first(masked, raw)
    tot = w1 + w2
    e1 = best_g * gp + i1
    e2 = best_g * gp + i2
    idx_ref[0:1, :] = e1
    idx_ref[1:2, :] = e2
    t = e1.shape[1]
    sub = lax.broadcasted_iota(jnp.int32, (8, t), 0)
    w8 = jnp.where(sub == 0, w1 / tot, jnp.where(sub == 1, w2 / tot, 0.0))
    wcol_ref[...] = w8.T
    eio = lax.broadcasted_iota(jnp.int32, (N_EXPERTS, t), 0)
    oh1, oh2 = eio == e1, eio == e2
    oh = jnp.where(oh1, 1.0, jnp.where(oh2, 1.0, 0.0))
    local = _dot(oh.astype(BF16), tri_ref[...])
    rank = local + run_ref[:, 0:1]
    rank_ref[0:1, :] = jnp.sum(jnp.where(oh1, rank, 0.0), axis=0, keepdims=True).astype(jnp.int32)
    rank_ref[1:2, :] = jnp.sum(jnp.where(oh2, rank, 0.0), axis=0, keepdims=True).astype(jnp.int32)
    run_ref[...] = run_ref[...] + jnp.sum(oh, axis=1, keepdims=True)
    cnt_ref[...] = run_ref[...]


def _wide_mod_row(segs, tm):
    ctx, lat = segs
    ctx_tiles = ctx.batch * ctx.seq // tm
    assert ctx_tiles * tm == ctx.batch * ctx.seq and lat.seq % tm == 0
    return lambda i: jnp.where(i < ctx_tiles, ctx.mod0, lat.mod0 + (i - ctx_tiles) // (lat.seq // tm))


def _router(x, mods, layer, segs, g2, rw_hi, rw_lo, rb):
    n, d = x.shape
    tm = WIDE_TILE
    mod_row = _wide_mod_row(segs, tm)
    return pl.pallas_call(
        _router_kernel,
        grid=(n // tm,),
        in_specs=[
            pl.BlockSpec((tm, d), lambda i: (i, 0)),
            pl.BlockSpec((None, None, 6, 1, d), lambda i: (layer, mod_row(i), 0, 0, 0)),
            pl.BlockSpec((1, d), lambda i: (0, 0)),
            pl.BlockSpec(rw_hi.shape, lambda i: (0, 0)),
            pl.BlockSpec(rw_lo.shape, lambda i: (0, 0)),
            pl.BlockSpec(rb.shape, lambda i: (0, 0)),
        ],
        out_specs=[
            pl.BlockSpec((tm, d // 2), lambda i: (i, 0)),
            pl.BlockSpec((2, tm), lambda i: (0, i)),
            pl.BlockSpec((2, tm), lambda i: (0, i)),
            pl.BlockSpec((tm, 8), lambda i: (i, 0)),
            pl.BlockSpec((N_EXPERTS, LANE), lambda i: (0, 0)),
        ],
        out_shape=[
            jax.ShapeDtypeStruct((n, d // 2), jnp.uint32),
            jax.ShapeDtypeStruct((2, n), jnp.int32),
            jax.ShapeDtypeStruct((2, n), jnp.int32),
            jax.ShapeDtypeStruct((n, 8), F32),
            jax.ShapeDtypeStruct((N_EXPERTS, LANE), F32),
        ],
        scratch_shapes=[pltpu.VMEM((N_EXPERTS, LANE), F32), pltpu.VMEM((tm, tm), BF16)],
        compiler_params=_cparams("arbitrary"),
        name="moe_router",
    )(x, mods, g2, rw_hi, rw_lo, rb)


def _expert_kernel(be_ref, br_ref, bs_ref, xs_ref, wg_ref, wu_ref, wd_ref, y_ref, wg_b, wu_b, wd_b):
    i = pl.program_id(0)
    prev = be_ref[jnp.maximum(i - 1, 0)]
    valid = br_ref[i]
    tail_rows = MOE_TAIL_ROWS

    @pl.when(jnp.logical_or(i == 0, be_ref[i] != prev))
    def _():
        wg_b[...] = wg_ref[...].astype(BF16)
        wu_b[...] = wu_ref[...].astype(BF16)
        wd_b[...] = wd_ref[...].astype(BF16)

    def ffn(r0, nrows):
        lo, hi = _unpack_halves(xs_ref[r0:r0 + nrows, :])
        xb = jnp.concatenate([lo.astype(BF16), hi.astype(BF16)], axis=1)
        gate = _dot(xb, wg_b[...])
        hid = (gate * _sigmoid(gate)) * _dot(xb, wu_b[...])
        y_ref[r0:r0 + nrows, :] = _pack_halves(_dot(hid.astype(BF16), wd_b[...]))

    @pl.when(valid > tail_rows)
    def _():
        ffn(0, y_ref.shape[0])

    @pl.when(jnp.logical_and(valid > 0, valid <= tail_rows))
    def _():
        ffn(0, tail_rows)
        y_ref[tail_rows:, :] = jnp.zeros((y_ref.shape[0] - tail_rows, y_ref.shape[1]), y_ref.dtype)


def _experts(xs, block_e, block_rows, block_src, w_gate, w_up, w_down, layer):
    rows, half = xs.shape
    d = 2 * half
    de = w_gate.shape[-1]
    n_blocks = rows // MOE_ROWS
    grid_spec = pltpu.PrefetchScalarGridSpec(
        num_scalar_prefetch=3,
        grid=(n_blocks,),
        in_specs=[
            pl.BlockSpec((MOE_ROWS, half), lambda i, be, br, bs: (bs[i], 0)),
            pl.BlockSpec((None, None, d, de), lambda i, be, br, bs: (layer, be[i], 0, 0)),
            pl.BlockSpec((None, None, d, de), lambda i, be, br, bs: (layer, be[i], 0, 0)),
            pl.BlockSpec((None, None, de, d), lambda i, be, br, bs: (layer, be[i], 0, 0)),
        ],
        out_specs=pl.BlockSpec((MOE_ROWS, half), lambda i, be, br, bs: (bs[i], 0)),
        scratch_shapes=[pltpu.VMEM((d, de), BF16), pltpu.VMEM((d, de), BF16), pltpu.VMEM((de, d), BF16)],
    )
    return pl.pallas_call(
        _expert_kernel,
        grid_spec=grid_spec,
        out_shape=jax.ShapeDtypeStruct((rows, half), jnp.uint32),
        compiler_params=_cparams("arbitrary"),
        name="moe_experts",
    )(block_e, block_rows, block_src, xs, w_gate, w_up, w_down)


def _combine_kernel(x_ref, m_ref, y_ref, w_ref, fg_ref, o_ref):
    x = x_ref[...] + _moe_residual(y_ref, w_ref, m_ref[5])
    o_ref[...] = _rms(x) * fg_ref[...]


def _combine(x, mods, layer, segs, seg, yg, wcol, final_g):
    n, d = x.shape
    tm = WIDE_TILE
    mod_row = _wide_mod_row(segs, tm)
    t0 = seg.row0 // tm
    steps = seg.rows // tm
    in_specs = [pl.BlockSpec((tm, d), lambda i: (t0 + i, 0)),
                pl.BlockSpec((None, None, 6, 1, d), lambda i: (layer, mod_row(t0 + i), 0, 0, 0)),
                pl.BlockSpec((2, tm, d // 2), lambda i: (0, i, 0)),
                pl.BlockSpec((tm, 8), lambda i: (t0 + i, 0)),
                pl.BlockSpec((1, d), lambda i: (0, 0))]
    return pl.pallas_call(
        _combine_kernel,
        grid=(steps,),
        in_specs=in_specs,
        out_specs=pl.BlockSpec((tm, d), lambda i: (i, 0)),
        out_shape=jax.ShapeDtypeStruct((seg.rows, d), F32),
        compiler_params=_cparams("parallel"),
        name="moe_combine_final",
    )(x, mods, yg, wcol, final_g)


def _sc_mesh():
    return plsc.VectorSubcoreMesh(core_axis_name="c", subcore_axis_name="s")


def _sc_worker_split(n):
    workers = SC_CORES * SC_SUBCORES
    per = n // workers
    assert per * workers == n and per % SC_CHUNK == 0
    return workers, per, per // SC_CHUNK


def _sc_dispatch(h, pos, rows):
    n, w = h.shape
    workers, per, chunks = _sc_worker_split(n)

    @functools.partial(
        pl.kernel, out_type=jax.ShapeDtypeStruct((rows, w), h.dtype), mesh=_sc_mesh(),
        scratch_types=[pltpu.VMEM((2, chunks, SC_CHUNK), jnp.int32), pltpu.VMEM((SC_CHUNK, w), h.dtype)],
        name="moe_dispatch_scatter")
    def scatter_rows(h_hbm, pos_hbm, xs_hbm, idx_v, rows_v):
        wid = lax.axis_index("s") * SC_CORES + lax.axis_index("c")
        pltpu.sync_copy(pos_hbm.at[0, wid], idx_v.at[0])
        pltpu.sync_copy(pos_hbm.at[1, wid], idx_v.at[1])

        @pl.loop(0, chunks)
        def _(c):
            pltpu.sync_copy(h_hbm.at[pl.ds(wid * per + c * SC_CHUNK, SC_CHUNK)], rows_v)
            pltpu.sync_copy(rows_v, xs_hbm.at[idx_v.at[0, c]])
            pltpu.sync_copy(rows_v, xs_hbm.at[idx_v.at[1, c]])

    return scatter_rows(h, pos.reshape(2, workers, chunks, SC_CHUNK))


def _sc_gather2(ys, pos):
    _, w = ys.shape
    n = pos.shape[1]
    workers, per, chunks = _sc_worker_split(n)

    @functools.partial(
        pl.kernel, out_type=jax.ShapeDtypeStruct((2, n, w), ys.dtype), mesh=_sc_mesh(),
        scratch_types=[pltpu.VMEM((2, chunks, SC_CHUNK), jnp.int32), pltpu.VMEM((SC_CHUNK, w), ys.dtype),
                       pltpu.SemaphoreType.DMA],
        name="moe_combine_gather")
    def gather_rows(ys_hbm, pos_hbm, out_hbm, idx_v, rows_v, sem):
        wid = lax.axis_index("s") * SC_CORES + lax.axis_index("c")
        pltpu.sync_copy(pos_hbm.at[0, wid], idx_v.at[0])
        pltpu.sync_copy(pos_hbm.at[1, wid], idx_v.at[1])

        @pl.loop(0, chunks)
        def _(c):
            for k in range(2):
                pltpu.async_copy(ys_hbm.at[idx_v.at[k, c]], rows_v, sem).wait()
                pltpu.sync_copy(rows_v, out_hbm.at[k, pl.ds(wid * per + c * SC_CHUNK, SC_CHUNK)])

    return gather_rows(ys, pos.reshape(2, workers, chunks, SC_CHUNK))


def _dispatch_plan(idx, rank, counts):
    n = idx.shape[1]
    padded = (counts + MOE_ROWS - 1) // MOE_ROWS * MOE_ROWS
    pad_end = jnp.cumsum(padded)
    pad_start = pad_end - padded
    experts = jnp.arange(N_EXPERTS, dtype=jnp.int32)
    start_of = jnp.sum(jnp.where(idx[..., None] == experts, pad_start, 0), axis=-1)
    pos = start_of + rank
    n_blocks = 2 * n // MOE_ROWS + N_EXPERTS
    steps = jnp.arange(n_blocks, dtype=jnp.int32)
    last_used = pad_end[-1] // MOE_ROWS - 1
    step = jnp.minimum(steps, last_used)
    block_e = jnp.minimum(jnp.sum(step[:, None] * MOE_ROWS >= pad_end[None, :], axis=1), N_EXPERTS - 1)
    pick = lambda per_expert: jnp.sum(jnp.where(block_e[:, None] == experts, per_expert, 0), axis=-1)
    first, count = pick(pad_start // MOE_ROWS), jnp.maximum(pick(padded // MOE_ROWS), 1)
    block_src = first + (step - first - 1) % count
    block_rows = jnp.clip(pick(pad_start + counts) - block_src * MOE_ROWS, 0, MOE_ROWS)
    block_rows = jnp.where(steps <= last_used, block_rows, 0)
    return (pos, block_e.astype(jnp.int32), block_rows.astype(jnp.int32), block_src.astype(jnp.int32),
            n_blocks * MOE_ROWS)


def _moe(x, mods, layer, segs, g2, rw_hi, rw_lo, rb, w_gate, w_up, w_down, final_g):
    h2p, idx, rank, wcol, cnt = _router(x, mods, layer, segs, g2, rw_hi, rw_lo, rb)
    pos, block_e, block_rows, block_src, rows = _dispatch_plan(idx, rank, cnt[:, 0].astype(jnp.int32))
    xs = _sc_dispatch(h2p, pos, rows)
    ys = _experts(xs, block_e, block_rows, block_src, w_gate, w_up, w_down, layer)
    yg = {seg: _sc_gather2(ys, pos[:, seg.row0:seg.row0 + seg.rows]) for seg in segs}
    if final_g is None:
        return yg, wcol, mods, layer
    return tuple(_combine(x, mods, layer, segs, seg, yg[seg], wcol, final_g) for seg in segs)


def _mla_rope_tables(t):
    axis_dim = MLA_ROPE // 2
    row = np.repeat(np.arange(t // GRID_W), GRID_W).astype(np.float64)
    col = np.tile(np.arange(GRID_W), t // GRID_W).astype(np.float64)
    inv = ROPE_BASE ** (-np.arange(0, axis_dim, 2, dtype=np.float64) / axis_dim)
    ar, ac = row[:, None] * inv[None, :], col[:, None] * inv[None, :]
    ones = np.ones((t, LANE - MLA_ROPE))
    cos = np.concatenate([np.cos(ar), np.cos(ar), np.cos(ac), np.cos(ac), ones], axis=-1)
    sin = np.concatenate([-np.sin(ar), np.sin(ar), -np.sin(ac), np.sin(ac), 0.0 * ones], axis=-1)
    return jnp.asarray(cos, F32), jnp.asarray(sin, F32)


def _ret_rot_tables(t, dk):
    inv = ROPE_BASE ** (-np.linspace(0.0, 1.0, dk // 2))
    ang = np.arange(t, dtype=np.float64)[:, None] * inv[None, :]
    return jnp.asarray(np.cos(ang), F32), jnp.asarray(np.sin(ang), F32)


def _dft_tables(n):
    k = np.arange(n, dtype=np.int64)
    ang = (np.outer(k, k) % n).astype(np.float64) * (2.0 * math.pi / n)
    return jnp.asarray(np.cos(ang), BF16), jnp.asarray(np.sin(ang), BF16)


def _mla_weights(w_in, q_g, kv_g, w_uq, w_ukv, w_o):
    d = w_in.shape[0]
    hd = MLA_NOPE + MLA_ROPE
    perm = _rope_perm()
    w_in_p = jnp.concatenate([w_in, w_in[:, MLA_Q_LORA + MLA_KV_LORA + perm]], axis=1)
    uq = w_uq.reshape(MLA_Q_LORA, MLA_HEADS, hd)
    uq = jnp.concatenate([uq, uq[..., MLA_NOPE + perm]], axis=-1)
    ukv = w_ukv.reshape(MLA_KV_LORA, MLA_HEADS, MLA_NOPE + MLA_V)
    return {
        "w_in": w_in_p.astype(BF16),
        "q_g": q_g.reshape(1, -1) * (MLA_NOPE + MLA_ROPE) ** -0.5,
        "kv_g": kv_g.reshape(1, -1),
        "w_q": uq.reshape(MLA_Q_LORA, MLA_HEADS * 2 * LANE).astype(BF16),
        "w_kn": ukv[..., :MLA_NOPE].reshape(MLA_KV_LORA, MLA_HEADS * MLA_NOPE).astype(BF16),
        "w_v": ukv[..., MLA_NOPE:].reshape(MLA_KV_LORA, MLA_HEADS * MLA_V).astype(BF16),
        "w_o": w_o.astype(BF16),
    }


def kernel(x_prompt, x_sample, cache_mla, state_ret, c, c_ctx, norm1_g, norm2_g, ada_w, ada_b, final_norm_g,
           mla_w_in, mla_q_norm_g, mla_kv_norm_g, mla_w_uq, mla_w_ukv, mla_w_o, ret_w_in, ret_decay_f,
           ret_decay_b, ret_w_o, fnet_w, router_w, router_b, moe_w_gate, moe_w_up, moe_w_down):
    b_ctx, t_ctx, d = x_prompt.shape
    b_lat, t_lat, _ = x_sample.shape
    depth = ada_w.shape[0]
    assert b_lat + 1 <= 8
    n_ctx = b_ctx * t_ctx
    ctx = _Seg(0, b_ctx, t_ctx, 0, False)
    lat = _Seg(n_ctx, b_lat, t_lat, 1, True)
    segs = (ctx, lat)

    n_lat = b_lat * t_lat
    n_mla = mla_w_in.shape[0]
    assert n_mla >= 1
    x = None
    new_cache = jnp.zeros((b_ctx, n_mla, t_ctx, MLA_KV_LORA + MLA_ROPE), F32)
    cond8 = jnp.concatenate([c_ctx[None, :], c, jnp.zeros((8 - 1 - b_lat, d), F32)], axis=0)
    mods = _modulation_all(cond8, ada_w, ada_b).reshape(depth, 8, 6, 1, d)

    rw_t = router_w.T.astype(F32)
    rw_hi = rw_t.astype(BF16)
    rw_lo = (rw_t - rw_hi.astype(F32)).astype(BF16)
    rb = router_b.reshape(N_EXPERTS, 1).astype(F32)
    final_g = final_norm_g.reshape(1, d)
    dk = ret_w_in.shape[2] // (8 * RET_HEADS)
    dv = 2 * dk

    states = []
    pending = None
    counters = [0, 0, 0]
    for layer in range(depth):
        kind = layer % 3
        j = counters[kind]
        counters[kind] += 1
        g1 = norm1_g[layer].reshape(1, d)
        g2 = norm2_g[layer].reshape(1, d)
        if kind == 0:
            w = _mla_weights(mla_w_in[j], mla_q_norm_g[j], mla_kv_norm_g[j], mla_w_uq[j], mla_w_ukv[j],
                             mla_w_o[j])
            if x is None:
                xc, xc0, xl, xl0 = x_prompt.reshape(n_ctx, d), 0, x_sample.reshape(n_lat, d), 0
            else:
                xc, xc0, xl, xl0 = x, ctx.row0, x, lat.row0
            past = cache_mla.shape[2]
            cpad = jnp.pad(cache_mla[:, j].reshape(b_lat * past, -1), ((0, 0), (0, LANE - MLA_ROPE)))
            kp, vp = _cache_kv(cpad, w)
            if pending is None:
                qc, kc, vc, new_cache = _mla_proj(xc, xc0, mods, layer, ctx, g1, w, None, (new_cache, j, n_mla))
                ql, kl, vl = _mla_proj(xl, xl0, mods, layer, lat, g1, w, _mla_rope_tables(t_lat))
            else:
                qc, kc, vc, new_cache, x = _mla_proj(x, ctx.row0, mods, layer, ctx, g1, w, None,
                                                     (new_cache, j, n_mla), pending)
                ql, kl, vl, x = _mla_proj(x, lat.row0, mods, layer, lat, g1, w, _mla_rope_tables(t_lat), None,
                                          pending)
                xc = xl = x
            first = x is None
            x = _attention(xc, xc0, n_ctx + n_lat, jnp.zeros((n_ctx + n_lat, d), F32) if first else "inplace",
                           mods, layer, ctx, qc, [(kc, vc, t_ctx)], w["w_o"])
            x = _attention(xl if first else x, xl0, n_ctx + n_lat, x if first else "inplace", mods, layer, lat,
                           ql, [(kl, vl, t_lat), (kp, vp, past)], w["w_o"])
        elif kind == 1:
            w_in = ret_w_in[j]
            qk = RET_HEADS * dk
            w_in_b = _to_bf16(w_in)
            w_o_b = _to_bf16(ret_w_o[j])
            rot = _ret_rot_tables(t_lat, dk)
            n_qk, n_v = 2 * qk // RET_COL, RET_HEADS * dv // RET_COL
            assert (n_qk + n_v) * RET_COL * 2 == w_in.shape[1]
            parts = []
            for seg in segs:
                kinds = (("rotary" if seg is lat else "plain"),) * n_qk + ("plain",) * n_v
                qkv = _ret_proj(x, mods, layer, seg, g1, w_in_b, 0, kinds, rot, dk, pending)
                if pending is not None:
                    qkv, x = qkv
                parts.append((qkv, _ret_proj(x, mods, layer, seg, g1, w_in_b, 1, ("silu",) * (n_qk + n_v), None, dk)))
            x, s_ctx = _ret_scan(*parts[0], ctx, ret_decay_f[j], ret_decay_b[j], None, True, dk, dv, RET_HEADS,
                                 out_proj=(x, mods, layer, w_o_b))
            yl, _ = _ret_scan(*parts[1], lat, ret_decay_f[j], ret_decay_b[j], state_ret[:, j], False, dk, dv, 1)
            x = _matmul_residual(x, mods, layer, lat, yl, w_o_b)
            states.append(s_ctx)
        else:
            gd = d // FNET_GROUPS
            cc, sc = _dft_tables(gd)
            cs = jnp.concatenate([cc, sc], axis=1)
            w_b = fnet_w[j].astype(BF16)
            for seg in segs:
                ct, st = _dft_tables(seg.seq)
                if seg.seq <= FNET_ROWS:
                    x = _fnet_short(x, mods, layer, seg, g1, cs, ct, st, w_b, (seg.seq * gd) ** -0.5, pending)
                    continue
                if pending is None:
                    ac, as_ = _fnet_a(x, mods, layer, seg, g1, cs)
                else:
                    ac, as_, x = _fnet_a(x, mods, layer, seg, g1, cs, pending)
                x = _fnet_b(x, mods, layer, seg, ac, as_, ct, st, w_b, (seg.seq * gd) ** -0.5)
        if layer < depth - 1:
            pending = _moe(x, mods, layer, segs, g2, rw_hi, rw_lo, rb, moe_w_gate, moe_w_up, moe_w_down, None)
        else:
            y_prompt, y_sample = _moe(x, mods, layer, segs, g2, rw_hi, rw_lo, rb, moe_w_gate, moe_w_up,
                                      moe_w_down, final_g)

    new_state = jnp.stack(states, axis=1)
    return (y_prompt.reshape(b_ctx, t_ctx, d), y_sample.reshape(b_lat, t_lat, d), new_cache, new_state)
```

```python
import functools
import math

import jax
import jax.numpy as jnp
import numpy as np
from jax import lax
from jax.experimental import pallas as pl
from jax.experimental.pallas import tpu as pltpu
from jax.experimental.pallas import tpu_sc as plsc

F32 = jnp.float32
BF16 = jnp.bfloat16

GRID_W = 64
MLA_HEADS = 8
MLA_NOPE = 128
MLA_ROPE = 64
MLA_V = 128
MLA_Q_LORA = 384
MLA_KV_LORA = 256
ROPE_BASE = 10000.0
RET_HEADS = 4
RET_CHUNK = 256
FNET_GROUPS = 4
N_EXPERTS = 16
N_EXPERT_GROUPS = 4
EXPERTS_PER_GROUP = 4
D_EXPERT = 512
NORM_EPS = 1e-6

LANE = 128
PROJ_ROWS = 512
ATTN_ROWS = 512
FNET_ROWS = 512
ATTN_SHORT_SEQS = 4
WIDE_TILE = 1024
ROUTER_SUB_ROWS = 256
MOE_ROWS = 2048
MOE_TAIL_ROWS = 256
VMEM_LIMIT = 56 * 1024 * 1024
SC_CORES = 2
SC_SUBCORES = 16
SC_CHUNK = 128


def _cparams(*sem):
    return pltpu.CompilerParams(dimension_semantics=sem, vmem_limit_bytes=VMEM_LIMIT)


def _sigmoid(x):
    return 1.0 / (1.0 + jnp.exp(-x))


def _rms(x):
    return x * lax.rsqrt(jnp.mean(x * x, axis=-1, keepdims=True) + NORM_EPS)


def _modulate(x, g, shift, scale):
    return _rms(x) * (g * (1.0 + scale)) + shift


def _dot(a, b):
    return jnp.dot(a, b, preferred_element_type=F32)


def _dot_nt(a, b):
    return lax.dot_general(a, b, (((1,), (1,)), ((), ())), preferred_element_type=F32)


def _mod_kernel(c_ref, w_ref, b_ref, o_ref):
    c = c_ref[...]
    s = (c * _sigmoid(c)).astype(BF16)
    o_ref[...] = _dot(s, w_ref[...].astype(BF16)) + b_ref[...]


def _modulation_all(cond8, ada_w, ada_b):
    depth, d, d6 = ada_w.shape
    tn = d6 // 4
    return pl.pallas_call(
        _mod_kernel,
        grid=(depth, d6 // tn),
        in_specs=[
            pl.BlockSpec((8, d), lambda l, n: (0, 0)),
            pl.BlockSpec((None, d, tn), lambda l, n: (l, 0, n)),
            pl.BlockSpec((None, 1, tn), lambda l, n: (l, 0, n)),
        ],
        out_specs=pl.BlockSpec((None, 8, tn), lambda l, n: (l, 0, n)),
        out_shape=jax.ShapeDtypeStruct((depth, 8, d6), F32),
        compiler_params=_cparams("parallel", "parallel"),
        name="modulation",
    )(cond8, ada_w, ada_b.reshape(depth, 1, d6))


class _Seg:
    def __init__(self, row0, batch, seq, mod0, per_batch_mod):
        self.row0, self.batch, self.seq = row0, batch, seq
        self.mod0, self.per_batch_mod = mod0, per_batch_mod
        self.rows = batch * seq

    def tile(self, want):
        tm = min(want, self.seq) if self.per_batch_mod else want
        assert self.rows % tm == 0 and self.row0 % tm == 0 and (self.seq % tm == 0 or tm % self.seq == 0)
        return tm

    def seq_tile(self, want):
        tm = min(want, self.seq)
        assert self.seq % tm == 0 and self.row0 % tm == 0
        return tm

    def mod_row(self, tile, tm):
        if self.per_batch_mod:
            return self.mod0 + tile * tm // self.seq
        return self.mod0


def _mod_spec(layer, seg, d, tm, tile_of=lambda *a: a[0]):
    return pl.BlockSpec((None, None, 6, 1, d), lambda *a: (layer, seg.mod_row(tile_of(*a), tm), 0, 0, 0))


def _unpack_halves(p):
    lo = lax.bitcast_convert_type(p << 16, F32)
    hi = lax.bitcast_convert_type(p & jnp.uint32(0xFFFF0000), F32)
    return lo, hi


def _moe_residual(y_ref, w_ref, gate):
    w = w_ref[...]
    lo0, hi0 = _unpack_halves(y_ref[0])
    lo1, hi1 = _unpack_halves(y_ref[1])
    w0, w1 = w[:, 0:1], w[:, 1:2]
    return gate * jnp.concatenate([w0 * lo0 + w1 * lo1, w0 * hi0 + w1 * hi1], axis=-1)


def _take_pending(refs, pending):
    if not pending:
        return refs, lambda x: x
    y_ref, w_ref, pm_ref, *rest = refs
    xo_ref = rest.pop()

    def resolve(x):
        x = x + _moe_residual(y_ref, w_ref, pm_ref[5])
        xo_ref[...] = x
        return x

    return rest, resolve


def _pending_io(pending, seg, tm, d, n):
    yg_by_seg, wcol, mods, layer = pending
    t0 = seg.row0 // tm
    specs = [pl.BlockSpec((2, tm, d // 2), lambda i: (0, i, 0)),
             pl.BlockSpec((tm, 8), lambda i: (t0 + i, 0)),
             _mod_spec(layer, seg, d, tm)]
    return (specs, [yg_by_seg[seg], wcol, mods], pl.BlockSpec((tm, d), lambda i: (t0 + i, 0)),
            jax.ShapeDtypeStruct((n, d), F32))


def _rope_partner(x):
    return pltpu.roll(x, LANE // 2, 1)


def _rope_perm():
    return np.array([l + 16 if l % 32 < 16 else l - 16 for l in range(MLA_ROPE)])


def _store_values(v_ref, v):
    ones = jnp.ones((v.shape[0], LANE), BF16)
    for hd in range(MLA_HEADS):
        v_ref[:, hd * 2 * LANE:hd * 2 * LANE + LANE] = v[:, hd * MLA_V:(hd + 1) * MLA_V].astype(BF16)
        v_ref[:, hd * 2 * LANE + LANE:(hd + 1) * 2 * LANE] = ones


def _mla_proj_kernel(*refs, rope, pending):
    refs, resolve = _take_pending(refs, pending)
    if rope:
        (x_ref, m_ref, g_ref, win_ref, qg_ref, kvg_ref, wq_ref, wkn_ref, wv_ref, cos_ref, sin_ref,
         q_ref, k_ref, v_ref) = refs
    else:
        x_ref, m_ref, g_ref, win_ref, qg_ref, kvg_ref, wq_ref, wkn_ref, wv_ref = refs[:9]
        q_ref, k_ref, v_ref, cache_ref = refs[-4:]
    h = _modulate(resolve(x_ref[...]), g_ref[...], m_ref[0], m_ref[1]).astype(BF16)
    z = _dot(h, win_ref[...])
    cq = z[:, :MLA_Q_LORA]
    ckv = z[:, MLA_Q_LORA:MLA_Q_LORA + MLA_KV_LORA]
    kpe = z[:, MLA_Q_LORA + MLA_KV_LORA:]
    cqn = (_rms(cq) * qg_ref[...]).astype(BF16)
    ckvn = _rms(ckv) * kvg_ref[...]
    ckvb = ckvn.astype(BF16)
    q = _dot(cqn, wq_ref[...])
    kn = _dot(ckvb, wkn_ref[...])
    _store_values(v_ref, _dot(ckvb, wv_ref[...]))
    if rope:
        cos, sin = cos_ref[...], sin_ref[...]
        kpe = kpe * cos + _rope_partner(kpe) * sin
    kpe = jnp.where(lax.broadcasted_iota(jnp.int32, kpe.shape, 1) < MLA_ROPE, kpe, 0.0)
    if not rope:
        seq = cache_ref.shape[1]
        for s in range(cache_ref.shape[0]):
            cache_ref[s, :, :MLA_KV_LORA] = ckvn[s * seq:(s + 1) * seq, :]
            cache_ref[s, :, MLA_KV_LORA:] = kpe[s * seq:(s + 1) * seq, :MLA_ROPE]
    kpe_b = kpe.astype(BF16)
    for hd in range(MLA_HEADS):
        lo = hd * 2 * LANE
        q_ref[:, lo:lo + LANE] = q[:, lo:lo + LANE].astype(BF16)
        qr = q[:, lo + LANE:lo + 2 * LANE]
        if rope:
            qr = qr * cos + _rope_partner(qr) * sin
        q_ref[:, lo + LANE:lo + 2 * LANE] = qr.astype(BF16)
        k_ref[:, lo:lo + LANE] = kn[:, hd * LANE:(hd + 1) * LANE].astype(BF16)
        k_ref[:, lo + LANE:lo + 2 * LANE] = kpe_b


def _mla_proj(x, x_row0, mods, layer, seg, g1, w, rope_tabs, cache_slot=None, pending=None):
    n, d = x.shape
    rope = rope_tabs is not None
    rows = seg.rows
    tm = seg.tile(PROJ_ROWS)
    x_tile0 = x_row0 // tm
    hq = MLA_HEADS * 2 * LANE
    const = lambda i: (0, 0)
    aliases = {}
    in_specs = [
        pl.BlockSpec((tm, d), lambda i: (x_tile0 + i, 0)),
        _mod_spec(layer, seg, d, tm),
        pl.BlockSpec((1, d), const),
        pl.BlockSpec(w["w_in"].shape, const),
        pl.BlockSpec((1, MLA_Q_LORA), const),
        pl.BlockSpec((1, MLA_KV_LORA), const),
        pl.BlockSpec(w["w_q"].shape, const),
        pl.BlockSpec(w["w_kn"].shape, const),
        pl.BlockSpec(w["w_v"].shape, const),
    ]
    args = [x, mods, g1, w["w_in"], w["q_g"], w["kv_g"], w["w_q"], w["w_kn"], w["w_v"]]
    out_specs = [pl.BlockSpec((tm, hq), lambda i: (i, 0))] * 3
    out_shape = [jax.ShapeDtypeStruct((rows, hq), BF16)] * 3
    if rope:
        tab = pl.BlockSpec((tm, LANE), lambda i: (i % (seg.seq // tm), 0))
        in_specs += [tab, tab]
        args += list(rope_tabs)
    else:
        cw = MLA_KV_LORA + MLA_ROPE
        prev, slot, n_slots = cache_slot
        assert tm % seg.seq == 0
        out_specs.append(pl.BlockSpec((tm // seg.seq, None, seg.seq, cw), lambda i: (i, slot, 0, 0)))
        out_shape.append(jax.ShapeDtypeStruct((seg.batch, n_slots, seg.seq, cw), F32))
        in_specs.append(pl.BlockSpec(memory_space=pl.ANY))
        args.append(prev)
        aliases = {len(args) - 1: 3}
    if pending is not None:
        p_specs, p_args, xo_spec, xo_shape = _pending_io(pending, seg, tm, d, n)
        in_specs, args = p_specs + in_specs, p_args + args
        out_specs.append(xo_spec)
        out_shape.append(xo_shape)
        aliases = {k + len(p_args): v for k, v in aliases.items()}
        aliases[len(p_args)] = len(out_shape) - 1
    return pl.pallas_call(
        functools.partial(_mla_proj_kernel, rope=rope, pending=pending is not None),
        grid=(rows // tm,),
        in_specs=in_specs,
        out_specs=out_specs,
        out_shape=out_shape,
        input_output_aliases=aliases,
        compiler_params=_cparams("parallel"),
        name="mla_proj_lat" if rope else "mla_proj_ctx",
    )(*args)


def _cache_kv_kernel(c_ref, wkn_ref, wv_ref, k_ref, v_ref):
    c = c_ref[...]
    ckv = c[:, :MLA_KV_LORA].astype(BF16)
    kpe_b = c[:, MLA_KV_LORA:].astype(BF16)
    kn = _dot(ckv, wkn_ref[...])
    _store_values(v_ref, _dot(ckv, wv_ref[...]))
    for hd in range(MLA_HEADS):
        lo = hd * 2 * LANE
        k_ref[:, lo:lo + LANE] = kn[:, hd * LANE:(hd + 1) * LANE].astype(BF16)
        k_ref[:, lo + LANE:lo + 2 * LANE] = kpe_b


def _cache_kv(cache_pad, w):
    rows, cw = cache_pad.shape
    hq = MLA_HEADS * 2 * LANE
    const = lambda i: (0, 0)
    tm = min(PROJ_ROWS, rows)
    assert rows % tm == 0
    return pl.pallas_call(
        _cache_kv_kernel,
        grid=(rows // tm,),
        in_specs=[
            pl.BlockSpec((tm, cw), lambda i: (i, 0)),
            pl.BlockSpec(w["w_kn"].shape, const),
            pl.BlockSpec(w["w_v"].shape, const),
        ],
        out_specs=[pl.BlockSpec((tm, hq), lambda i: (i, 0))] * 2,
        out_shape=[jax.ShapeDtypeStruct((rows, hq), BF16)] * 2,
        compiler_params=_cparams("parallel"),
        name="mla_cache_kv",
    )(cache_pad, w["w_kn"], w["w_v"])


def _attn_kernel(*refs, n_parts, n_seq):
    q_ref = refs[0]
    kv_refs = refs[1:1 + 2 * n_parts]
    wo_ref, x_ref, m_ref = refs[1 + 2 * n_parts:4 + 2 * n_parts]
    o_ref, acc_ref = refs[-2:]
    tq = q_ref.shape[0] // n_seq
    for sq, hd in [(sq, hd) for sq in range(n_seq) for hd in range(MLA_HEADS)]:
        rows = slice(sq * tq, (sq + 1) * tq)
        kcol = slice(hd * 2 * LANE, (hd + 1) * 2 * LANE)
        keys = [slice(sq * (r.shape[0] // n_seq), (sq + 1) * (r.shape[0] // n_seq)) for r in kv_refs[::2]]
        scores = [_dot_nt(q_ref[rows, kcol], kv_refs[2 * p][keys[p], kcol]) for p in range(n_parts)]
        mx = scores[0].max(axis=-1, keepdims=True)
        for s in scores[1:]:
            mx = jnp.maximum(mx, s.max(axis=-1, keepdims=True))
        out = None
        for p, s in enumerate(scores):
            e = jnp.exp((s - mx).astype(BF16))
            pv = _dot(e, kv_refs[2 * p + 1][keys[p], kcol])
            out = pv if out is None else out + pv
        acc_ref[rows, hd * MLA_V:(hd + 1) * MLA_V] = (out[:, :MLA_V] / out[:, LANE:LANE + MLA_V]).astype(BF16)
    y = _dot(acc_ref[...], wo_ref[...])
    o_ref[...] = x_ref[...] + m_ref[2] * y


def _attention(x, x_row0, n, dest, mods, layer, seg, q, kv_parts, w_o):
    d = x.shape[1]
    hq = MLA_HEADS * 2 * LANE
    hv = MLA_HEADS * MLA_V
    tq = seg.seq_tile(ATTN_ROWS)
    tps = seg.seq // tq
    n_seq = ATTN_SHORT_SEQS if (tps == 1 and not seg.per_batch_mod and seg.batch % ATTN_SHORT_SEQS == 0) else 1
    tq *= n_seq
    x_tile0, out_tile0 = x_row0 // tq, seg.row0 // tq
    in_specs = [pl.BlockSpec((tq, hq), lambda b, i: (b * tps + i, 0))]
    args = [q]
    for k, v, rows in kv_parts:
        mode = dict(pipeline_mode=pl.Buffered(1)) if tps > 1 else {}
        in_specs += [pl.BlockSpec((n_seq * rows, hq), lambda b, i: (b, 0)),
                     pl.BlockSpec((n_seq * rows, hq), lambda b, i: (b, 0), **mode)]
        args += [k, v]
    in_specs += [
        pl.BlockSpec(w_o.shape, lambda b, i: (0, 0)),
        pl.BlockSpec((tq, d), lambda b, i: (x_tile0 + b * tps + i, 0)),
        _mod_spec(layer, seg, d, tq // n_seq, tile_of=lambda b, i: (b * tps + i) * n_seq),
    ]
    args += [w_o, x, mods]
    if isinstance(dest, str):
        assert dest == "inplace"
        aliases = {len(args) - 2: 0}
    else:
        in_specs.append(pl.BlockSpec(memory_space=pl.ANY))
        args.append(dest)
        aliases = {len(args) - 1: 0}
    return pl.pallas_call(
        functools.partial(_attn_kernel, n_parts=len(kv_parts), n_seq=n_seq),
        grid=(seg.batch // n_seq, tps),
        in_specs=in_specs,
        out_specs=pl.BlockSpec((tq, d), lambda b, i: (out_tile0 + b * tps + i, 0)),
        out_shape=jax.ShapeDtypeStruct((n, d), F32),
        scratch_shapes=[pltpu.VMEM((tq, hv), BF16)],
        input_output_aliases=aliases,
        compiler_params=_cparams("parallel", "arbitrary"),
        name="mla_attention",
    )(*args)


RET_COL = 1024


RET_ROWS = 1024


def _ret_proj_kernel(*refs, kinds, dk, pending):
    refs, resolve = _take_pending(refs, pending)
    rotary = "rotary" in kinds
    if rotary:
        x_ref, m_ref, g_ref, w_ref, cos_ref, sin_ref, z_ref = refs
        cos, sin = cos_ref[...], sin_ref[...]
    else:
        x_ref, m_ref, g_ref, w_ref, z_ref = refs
    h = _modulate(resolve(x_ref[...]), g_ref[...], m_ref[0], m_ref[1]).astype(BF16)
    half = dk // 2
    for j, kind in enumerate(kinds):
        c0 = j * RET_COL
        acc = _dot(h, w_ref[:, c0:c0 + RET_COL])
        if kind == "rotary":
            for hd in range(RET_COL // dk):
                lo = hd * dk
                x1, x2 = acc[:, lo:lo + half], acc[:, lo + half:lo + dk]
                z_ref[:, c0 + lo:c0 + lo + half] = (x1 * cos - x2 * sin).astype(BF16)
                z_ref[:, c0 + lo + half:c0 + lo + dk] = (x1 * sin + x2 * cos).astype(BF16)
        elif kind == "silu":
            z_ref[:, c0:c0 + RET_COL] = (acc * _sigmoid(acc)).astype(BF16)
        else:
            z_ref[:, c0:c0 + RET_COL] = acc.astype(BF16)


def _ret_proj(x, mods, layer, seg, g1, w_in, group, kinds, rot_tabs, dk, pending=None):
    n, d = x.shape
    rows = seg.rows
    tm = seg.tile(RET_ROWS)
    ncol = len(kinds) * RET_COL
    tps = max(seg.seq // tm, 1)
    in_specs = [
        pl.BlockSpec((tm, d), lambda i: (seg.row0 // tm + i, 0)),
        _mod_spec(layer, seg, d, tm),
        pl.BlockSpec((1, d), lambda i: (0, 0)),
        pl.BlockSpec((d, ncol), lambda i: (0, group), pipeline_mode=pl.Buffered(1)),
    ]
    args = [x, mods, g1, w_in]
    if "rotary" in kinds:
        tab = pl.BlockSpec((tm, dk // 2), lambda i: (i % tps, 0))
        in_specs += [tab, tab]
        args += list(rot_tabs)
    out_specs = [pl.BlockSpec((tm, ncol), lambda i: (i, 0))]
    out_shape = [jax.ShapeDtypeStruct((rows, ncol), BF16)]
    aliases = {}
    if pending is not None:
        p_specs, p_args, xo_spec, xo_shape = _pending_io(pending, seg, tm, d, n)
        in_specs, args = p_specs + in_specs, p_args + args
        out_specs.append(xo_spec)
        out_shape.append(xo_shape)
        aliases = {len(p_args): 1}
    res = pl.pallas_call(
        functools.partial(_ret_proj_kernel, kinds=kinds, dk=dk, pending=pending is not None),
        grid=(rows // tm,),
        in_specs=in_specs,
        out_specs=out_specs,
        out_shape=out_shape,
        input_output_aliases=aliases,
        compiler_params=_cparams("parallel"),
        name="ret_proj_" + kinds[0],
    )(*args)
    return res if pending is not None else res[0]


def _log_sigmoid(x):
    return jnp.minimum(x, 0.0) - jnp.log(1.0 + jnp.exp(-jnp.abs(x)))


def _ret_scan_kernel(*refs, has_s0, emit_state, n_chunks, heads, fuse_out):
    refs = list(refs)
    lf_ref, lb_ref, q_ref, k_ref, v_ref, gf_ref, gb_ref = refs[:7]
    pos = 7
    s0_ref = None
    if has_s0:
        s0_ref = refs[pos]
        pos += 1
    if fuse_out:
        wo_ref, x_ref, m_ref, o_ref = refs[pos:pos + 4]
        pos += 4
    else:
        y_ref = refs[pos]
        pos += 1
    sout_ref = None
    if emit_state:
        sout_ref = refs[pos]
        pos += 1
    if fuse_out:
        s_ref, yf_ref, y_ref = refs[pos:]
    else:
        s_ref, yf_ref = refs[pos:]
    c = RET_CHUNK
    dk, dv = s_ref.shape
    ii = lax.broadcasted_iota(jnp.int32, (c, c), 0).astype(F32)
    jj = lax.broadcasted_iota(jnp.int32, (c, c), 1).astype(F32)
    idx = lax.broadcasted_iota(jnp.int32, (c, 1), 0).astype(F32)

    for hd, direction in [(hd, direction) for hd in range(heads) for direction in range(2)]:
        fwd = direction == 0
        kcol, vcol = slice(hd * dk, (hd + 1) * dk), slice(hd * dv, (hd + 1) * dv)
        lg = _log_sigmoid((lf_ref if fwd else lb_ref)[hd])
        rel = (ii - jj) if fwd else (jj - ii)
        keep = rel >= 0
        decay_in = jnp.where(keep, jnp.exp(jnp.where(keep, rel, 0.0) * lg), 0.0) * dk ** -0.5
        decay_q = jnp.exp(((idx + 1.0) if fwd else (c - idx)) * lg)
        decay_k = jnp.exp(((c - 1.0 - idx) if fwd else idx) * lg) * dk ** -0.5
        decay_c = jnp.exp(c * lg)
        g_ref = gf_ref if fwd else gb_ref

        def chunk(cc, state, fwd=fwd, decay_in=decay_in, decay_q=decay_q, decay_k=decay_k, decay_c=decay_c,
                  g_ref=g_ref, kcol=kcol, vcol=vcol):
            r0 = cc * c if isinstance(cc, int) else pl.multiple_of(cc * c, c)
            qc = q_ref[pl.ds(r0, c), kcol]
            kc = k_ref[pl.ds(r0, c), kcol]
            vc = v_ref[pl.ds(r0, c), vcol]
            sc = _dot_nt(qc, kc) * decay_in
            out = _dot(sc.astype(BF16), vc)
            kd_t = (kc.astype(F32) * decay_k).T.astype(BF16)
            new_s = _dot(kd_t, vc)
            if state is not None:
                out = out + decay_q * _dot(qc, state.astype(BF16))
                new_s = decay_c * state + new_s
            s_ref[...] = new_s
            o = _rms(out) * g_ref[pl.ds(r0, c), vcol].astype(F32)
            if fwd:
                yf_ref[pl.ds(r0, c), :] = o
            else:
                y_ref[pl.ds(r0, c), vcol] = (yf_ref[pl.ds(r0, c), :] + o).astype(BF16)

        chunk(0 if fwd else n_chunks - 1, s0_ref[direction, hd] if has_s0 else None)

        def step(ci, carry, fwd=fwd, chunk=chunk):
            chunk(ci if fwd else n_chunks - 1 - ci, s_ref[...])
            return carry

        lax.fori_loop(1, n_chunks, step, 0, unroll=True)
        if emit_state:
            sout_ref[direction, hd] = s_ref[...]
    if fuse_out:
        o_ref[...] = x_ref[...] + m_ref[2] * _dot(y_ref[...], wo_ref[...])


def _ret_scan(qkv, g, seg, logit_f, logit_b, s0, emit_state, dk, dv, heads, out_proj=None):
    rows = seg.batch * seg.seq
    t = seg.seq
    hh = RET_HEADS
    groups = hh // heads
    fuse_out = out_proj is not None
    assert not fuse_out or groups == 1
    v0 = 2 * hh * dk // (heads * dv)
    assert groups * heads == hh and v0 * heads * dv == 2 * hh * dk
    in_specs = [
        pl.BlockSpec((heads, 1, 1), lambda b, h: (h, 0, 0)),
        pl.BlockSpec((heads, 1, 1), lambda b, h: (h, 0, 0)),
        pl.BlockSpec((t, heads * dk), lambda b, h: (b, h)),
        pl.BlockSpec((t, heads * dk), lambda b, h: (b, groups + h)),
        pl.BlockSpec((t, heads * dv), lambda b, h: (b, v0 + h)),
        pl.BlockSpec((t, heads * dv), lambda b, h: (b, h)),
        pl.BlockSpec((t, heads * dv), lambda b, h: (b, groups + h)),
    ]
    args = [logit_f.reshape(hh, 1, 1), logit_b.reshape(hh, 1, 1), qkv, qkv, qkv, g, g]
    state_spec = pl.BlockSpec((None, 2, heads, dk, dv), lambda b, h: (b, 0, h, 0, 0))
    if s0 is not None:
        in_specs.append(state_spec)
        args.append(s0)
    scratch = [pltpu.VMEM((dk, dv), F32), pltpu.VMEM((t, dv), F32)]
    aliases = {}
    if fuse_out:
        x, mods, layer, w_o = out_proj
        n, d = x.shape
        x_spec = pl.BlockSpec((t, d), lambda b, h: (seg.row0 // t + b, 0))
        in_specs += [pl.BlockSpec(w_o.shape, lambda b, h: (0, 0)), x_spec,
                     _mod_spec(layer, seg, d, t, tile_of=lambda b, h: b)]
        args += [w_o, x, mods]
        aliases = {len(args) - 2: 0}
        out_specs, out_shape = [x_spec], [jax.ShapeDtypeStruct((n, d), F32)]
        scratch.append(pltpu.VMEM((t, hh * dv), BF16))
    else:
        out_specs = [pl.BlockSpec((t, heads * dv), lambda b, h: (b, h))]
        out_shape = [jax.ShapeDtypeStruct((rows, hh * dv), BF16)]
    if emit_state:
        out_specs.append(state_spec)
        out_shape.append(jax.ShapeDtypeStruct((seg.batch, 2, hh, dk, dv), F32))
    res = pl.pallas_call(
        functools.partial(_ret_scan_kernel, has_s0=s0 is not None, emit_state=emit_state,
                          n_chunks=t // RET_CHUNK, heads=heads, fuse_out=fuse_out),
        grid=(seg.batch, groups),
        in_specs=in_specs,
        out_specs=out_specs,
        out_shape=out_shape,
        scratch_shapes=scratch,
        input_output_aliases=aliases,
        compiler_params=_cparams("parallel", "parallel"),
        name="ret_scan",
    )(*args)
    return res if emit_state else (res[0], None)


def _mm_res_kernel(a_ref, w_ref, x_ref, m_ref, o_ref):
    o_ref[...] = x_ref[...] + m_ref[2] * _dot(a_ref[...], w_ref[...])


def _matmul_residual(x, mods, layer, seg, a, w):
    n, d = x.shape
    tm = seg.tile(RET_ROWS)
    x_spec = pl.BlockSpec((tm, d), lambda i: (seg.row0 // tm + i, 0))
    return pl.pallas_call(
        _mm_res_kernel,
        grid=(seg.rows // tm,),
        in_specs=[
            pl.BlockSpec((tm, a.shape[1]), lambda i: (i, 0)),
            pl.BlockSpec(w.shape, lambda i: (0, 0)),
            x_spec,
            _mod_spec(layer, seg, d, tm),
        ],
        out_specs=x_spec,
        out_shape=jax.ShapeDtypeStruct((n, d), F32),
        input_output_aliases={2: 0},
        compiler_params=_cparams("parallel"),
        name="matmul_residual",
    )(a, w, x, mods)


def _fnet_a_kernel(*refs, gd, pending):
    refs, resolve = _take_pending(refs, pending)
    x_ref, m_ref, g_ref, cs_ref, ac_ref, as_ref = refs
    h = _modulate(resolve(x_ref[...]), g_ref[...], m_ref[0], m_ref[1]).astype(BF16)
    cs = cs_ref[...]
    for g in range(FNET_GROUPS):
        a = _dot(h[:, g * gd:(g + 1) * gd], cs)
        ac_ref[:, g * gd:(g + 1) * gd] = a[:, :gd].astype(BF16)
        as_ref[:, g * gd:(g + 1) * gd] = a[:, gd:].astype(BF16)


def _fnet_a(x, mods, layer, seg, g1, cs, pending=None):
    n, d = x.shape
    rows = seg.rows
    tm = seg.tile(PROJ_ROWS)
    out = pl.BlockSpec((tm, d), lambda i: (i, 0))
    in_specs = [
        pl.BlockSpec((tm, d), lambda i: (seg.row0 // tm + i, 0)),
        _mod_spec(layer, seg, d, tm),
        pl.BlockSpec((1, d), lambda i: (0, 0)),
        pl.BlockSpec(cs.shape, lambda i: (0, 0)),
    ]
    args = [x, mods, g1, cs]
    out_specs, out_shape, aliases = [out, out], [jax.ShapeDtypeStruct((rows, d), BF16)] * 2, {}
    if pending is not None:
        p_specs, p_args, xo_spec, xo_shape = _pending_io(pending, seg, tm, d, n)
        in_specs, args = p_specs + in_specs, p_args + args
        out_specs.append(xo_spec)
        out_shape.append(xo_shape)
        aliases = {len(p_args): 2}
    return pl.pallas_call(
        functools.partial(_fnet_a_kernel, gd=d // FNET_GROUPS, pending=pending is not None),
        grid=(rows // tm,),
        in_specs=in_specs,
        out_specs=out_specs,
        out_shape=out_shape,
        input_output_aliases=aliases,
        compiler_params=_cparams("parallel"),
        name="fnet_channel_dft",
    )(*args)


def _fnet_b_kernel(ct_ref, st_ref, ac_ref, as_ref, w_ref, x_ref, m_ref, o_ref, *, norm):
    f = (_dot(ct_ref[...], ac_ref[...]) - _dot(st_ref[...], as_ref[...])) * norm
    o_ref[...] = x_ref[...] + m_ref[2] * _dot(f.astype(BF16), w_ref[...])


def _fnet_b(x, mods, layer, seg, ac, as_, ct, st, w, norm):
    n, d = x.shape
    t = seg.seq
    tq = seg.seq_tile(FNET_ROWS)
    tps = t // tq
    x_spec = pl.BlockSpec((tq, d), lambda b, i: (seg.row0 // tq + b * tps + i, 0))
    tab = pl.BlockSpec((tq, t), lambda b, i: (i, 0))
    seq = pl.BlockSpec((t, d), lambda b, i: (b, 0))
    return pl.pallas_call(
        functools.partial(_fnet_b_kernel, norm=norm),
        grid=(seg.batch, tps),
        in_specs=[tab, tab, seq, seq, pl.BlockSpec(w.shape, lambda b, i: (0, 0)), x_spec,
                  _mod_spec(layer, seg, d, tq, tile_of=lambda b, i: b * tps + i)],
        out_specs=x_spec,
        out_shape=jax.ShapeDtypeStruct((n, d), F32),
        input_output_aliases={5: 0},
        compiler_params=_cparams("parallel", "arbitrary"),
        name="fnet_position_dft",
    )(ct, st, ac, as_, w, x, mods)


def _fnet_short_kernel(*refs, gd, norm, n_seq, pending):
    if pending:
        y_ref, wr_ref, pm_ref, *refs = refs
    x_ref, m_ref, g_ref, cs_ref, ct_ref, st_ref, w_ref, o_ref, ac_ref, as_ref = refs
    x = x_ref[...]
    if pending:
        x = x + _moe_residual(y_ref, wr_ref, pm_ref[5])
    h = _modulate(x, g_ref[...], m_ref[0], m_ref[1]).astype(BF16)
    cs = cs_ref[...]
    for g in range(FNET_GROUPS):
        a = _dot(h[:, g * gd:(g + 1) * gd], cs)
        ac_ref[:, g * gd:(g + 1) * gd] = a[:, :gd].astype(BF16)
        as_ref[:, g * gd:(g + 1) * gd] = a[:, gd:].astype(BF16)
    t = x.shape[0] // n_seq
    ct, st = ct_ref[...], st_ref[...]
    f = [_dot(ct, ac_ref[s * t:(s + 1) * t, :]) - _dot(st, as_ref[s * t:(s + 1) * t, :]) for s in range(n_seq)]
    f = (f[0] if n_seq == 1 else jnp.concatenate(f, axis=0)) * norm
    o_ref[...] = x + m_ref[2] * _dot(f.astype(BF16), w_ref[...])


def _fnet_short(x, mods, layer, seg, g1, cs, ct, st, w, norm, pending=None):
    n, d = x.shape
    t = seg.seq
    n_seq = ATTN_SHORT_SEQS if (not seg.per_batch_mod and seg.batch % ATTN_SHORT_SEQS == 0) else 1
    rows = n_seq * t
    t0 = seg.row0 // rows
    const = lambda i: (0, 0)
    x_spec = pl.BlockSpec((rows, d), lambda i: (t0 + i, 0))
    in_specs = [x_spec, _mod_spec(layer, seg, d, t, tile_of=lambda i: i * n_seq), pl.BlockSpec((1, d), const),
                pl.BlockSpec(cs.shape, const), pl.BlockSpec((t, t), const), pl.BlockSpec((t, t), const),
                pl.BlockSpec(w.shape, const)]
    args = [x, mods, g1, cs, ct, st, w]
    if pending is not None:
        yg_by_seg, wcol, pmods, player = pending
        in_specs = [pl.BlockSpec((2, rows, d // 2), lambda i: (0, i, 0)),
                    pl.BlockSpec((rows, 8), lambda i: (t0 + i, 0)),
                    _mod_spec(player, seg, d, t, tile_of=lambda i: i * n_seq)] + in_specs
        args = [yg_by_seg[seg], wcol, pmods] + args
    return pl.pallas_call(
        functools.partial(_fnet_short_kernel, gd=d // FNET_GROUPS, norm=norm, n_seq=n_seq,
                          pending=pending is not None),
        grid=(seg.rows // rows,),
        in_specs=in_specs,
        out_specs=x_spec,
        out_shape=jax.ShapeDtypeStruct((n, d), F32),
        scratch_shapes=[pltpu.VMEM((rows, d), BF16), pltpu.VMEM((rows, d), BF16)],
        input_output_aliases={len(args) - 7: 0},
        compiler_params=_cparams("parallel"),
        name="fnet_short",
    )(*args)


def _pack_halves(a):
    w = a.shape[1] // 2
    bits = lambda v: lax.bitcast_convert_type(v.astype(BF16).astype(F32), jnp.uint32)
    return (bits(a[:, :w]) >> 16) | (bits(a[:, w:]) & jnp.uint32(0xFFFF0000))


def _router_kernel(x_ref, m_ref, g_ref, rwhi_ref, rwlo_ref, rb_ref, h_ref, idx_ref, rank_ref, wcol_ref, cnt_ref,
                   run_ref, tri_ref):
    step = pl.program_id(0)

    @pl.when(step == 0)
    def _():
        run_ref[...] = jnp.zeros_like(run_ref)
        tt = tri_ref.shape[0]
        earlier = lax.broadcasted_iota(jnp.int32, (tt, tt), 0) < lax.broadcasted_iota(jnp.int32, (tt, tt), 1)
        tri_ref[...] = jnp.where(earlier, 1.0, 0.0).astype(BF16)

    parts = []
    for r0 in range(0, x_ref.shape[0], ROUTER_SUB_ROWS):
        rows = slice(r0, r0 + ROUTER_SUB_ROWS)
        h = _modulate(x_ref[rows, :], g_ref[...], m_ref[3], m_ref[4])
        h_ref[rows, :] = _pack_halves(h)
        h_hi = h.astype(BF16)
        h_lo = (h - h_hi.astype(F32)).astype(BF16)
        parts.append(_dot_nt(rwhi_ref[...], h_hi) + (_dot_nt(rwhi_ref[...], h_lo) + _dot_nt(rwlo_ref[...], h_hi)))
    logits = jnp.concatenate(parts, axis=1)
    sc = _sigmoid(logits)
    gr = sc + rb_ref[...]
    gp = EXPERTS_PER_GROUP
    row = lambda a, e: a[e:e + 1, :]
    best_g = None
    for g in range(N_EXPERT_GROUPS):
        vals = [row(gr, g * gp + i) for i in range(gp)]
        gs = None
        for i in range(gp):
            for j in range(i + 1, gp):
                pair = vals[i] + vals[j]
                gs = pair if gs is None else jnp.maximum(gs, pair)
        if best_g is None:
            best_g, best_v = jnp.zeros(gs.shape, jnp.int32), gs
        else:
            better = gs > best_v
            best_g = jnp.where(better, g, best_g)
            best_v = jnp.where(better, gs, best_v)
    sel, raw = [], []
    for i in range(gp):
        s_i, r_i = row(gr, i), row(sc, i)
        for g in range(1, N_EXPERT_GROUPS):
            s_i = jnp.where(best_g == g, row(gr, g * gp + i), s_i)
            r_i = jnp.where(best_g == g, row(sc, g * gp + i), r_i)
        sel.append(s_i)
        raw.append(r_i)

    def argmax_first(vals, raws):
        bi, bv, br = jnp.zeros(vals[0].shape, jnp.int32), vals[0], raws[0]
        for i in range(1, len(vals)):
            better = vals[i] > bv
            bi = jnp.where(better, i, bi)
            bv = jnp.where(better, vals[i], bv)
            br = jnp.where(better, raws[i], br)
        return bi, br

    i1, w1 = argmax_first(sel, raw)
    masked = [jnp.where(i1 == i, -jnp.inf, sel[i]) for i in range(gp)]
    i2, w2 = argmax_first(masked, raw)
    tot = w1 + w2
    e1 = best_g * gp + i1
    e2 = best_g * gp + i2
    idx_ref[0:1, :] = e1
    idx_ref[1:2, :] = e2
    t = e1.shape[1]
    sub = lax.broadcasted_iota(jnp.int32, (8, t), 0)
    w8 = jnp.where(sub == 0, w1 / tot, jnp.where(sub == 1, w2 / tot, 0.0))
    wcol_ref[...] = w8.T
    eio = lax.broadcasted_iota(jnp.int32, (N_EXPERTS, t), 0)
    oh1, oh2 = eio == e1, eio == e2
    oh = jnp.where(oh1, 1.0, jnp.where(oh2, 1.0, 0.0))
    local = _dot(oh.astype(BF16), tri_ref[...])
    rank = local + run_ref[:, 0:1]
    rank_ref[0:1, :] = jnp.sum(jnp.where(oh1, rank, 0.0), axis=0, keepdims=True).astype(jnp.int32)
    rank_ref[1:2, :] = jnp.sum(jnp.where(oh2, rank, 0.0), axis=0, keepdims=True).astype(jnp.int32)
    run_ref[...] = run_ref[...] + jnp.sum(oh, axis=1, keepdims=True)
    cnt_ref[...] = run_ref[...]


def _wide_mod_row(segs, tm):
    ctx, lat = segs
    ctx_tiles = ctx.batch * ctx.seq // tm
    assert ctx_tiles * tm == ctx.batch * ctx.seq and lat.seq % tm == 0
    return lambda i: jnp.where(i < ctx_tiles, ctx.mod0, lat.mod0 + (i - ctx_tiles) // (lat.seq // tm))


def _router(x, mods, layer, segs, g2, rw_hi, rw_lo, rb):
    n, d = x.shape
    tm = WIDE_TILE
    mod_row = _wide_mod_row(segs, tm)
    return pl.pallas_call(
        _router_kernel,
        grid=(n // tm,),
        in_specs=[
            pl.BlockSpec((tm, d), lambda i: (i, 0)),
            pl.BlockSpec((None, None, 6, 1, d), lambda i: (layer, mod_row(i), 0, 0, 0)),
            pl.BlockSpec((1, d), lambda i: (0, 0)),
            pl.BlockSpec(rw_hi.shape, lambda i: (0, 0)),
            pl.BlockSpec(rw_lo.shape, lambda i: (0, 0)),
            pl.BlockSpec(rb.shape, lambda i: (0, 0)),
        ],
        out_specs=[
            pl.BlockSpec((tm, d // 2), lambda i: (i, 0)),
            pl.BlockSpec((2, tm), lambda i: (0, i)),
            pl.BlockSpec((2, tm), lambda i: (0, i)),
            pl.BlockSpec((tm, 8), lambda i: (i, 0)),
            pl.BlockSpec((N_EXPERTS, LANE), lambda i: (0, 0)),
        ],
        out_shape=[
            jax.ShapeDtypeStruct((n, d // 2), jnp.uint32),
            jax.ShapeDtypeStruct((2, n), jnp.int32),
            jax.ShapeDtypeStruct((2, n), jnp.int32),
            jax.ShapeDtypeStruct((n, 8), F32),
            jax.ShapeDtypeStruct((N_EXPERTS, LANE), F32),
        ],
        scratch_shapes=[pltpu.VMEM((N_EXPERTS, LANE), F32), pltpu.VMEM((tm, tm), BF16)],
        compiler_params=_cparams("arbitrary"),
        name="moe_router",
    )(x, mods, g2, rw_hi, rw_lo, rb)


def _expert_kernel(be_ref, br_ref, bs_ref, xs_ref, wg_ref, wu_ref, wd_ref, y_ref, wg_b, wu_b, wd_b):
    i = pl.program_id(0)
    prev = be_ref[jnp.maximum(i - 1, 0)]
    valid = br_ref[i]
    tail_rows = MOE_TAIL_ROWS

    @pl.when(jnp.logical_or(i == 0, be_ref[i] != prev))
    def _():
        wg_b[...] = wg_ref[...].astype(BF16)
        wu_b[...] = wu_ref[...].astype(BF16)
        wd_b[...] = wd_ref[...].astype(BF16)

    def ffn(r0, nrows):
        lo, hi = _unpack_halves(xs_ref[r0:r0 + nrows, :])
        xb = jnp.concatenate([lo.astype(BF16), hi.astype(BF16)], axis=1)
        gate = _dot(xb, wg_b[...])
        hid = (gate * _sigmoid(gate)) * _dot(xb, wu_b[...])
        y_ref[r0:r0 + nrows, :] = _pack_halves(_dot(hid.astype(BF16), wd_b[...]))

    @pl.when(valid > tail_rows)
    def _():
        ffn(0, y_ref.shape[0])

    @pl.when(jnp.logical_and(valid > 0, valid <= tail_rows))
    def _():
        ffn(0, tail_rows)
        y_ref[tail_rows:, :] = jnp.zeros((y_ref.shape[0] - tail_rows, y_ref.shape[1]), y_ref.dtype)


def _experts(xs, block_e, block_rows, block_src, w_gate, w_up, w_down, layer):
    rows, half = xs.shape
    d = 2 * half
    de = w_gate.shape[-1]
    n_blocks = rows // MOE_ROWS
    grid_spec = pltpu.PrefetchScalarGridSpec(
        num_scalar_prefetch=3,
        grid=(n_blocks,),
        in_specs=[
            pl.BlockSpec((MOE_ROWS, half), lambda i, be, br, bs: (bs[i], 0)),
            pl.BlockSpec((None, None, d, de), lambda i, be, br, bs: (layer, be[i], 0, 0)),
            pl.BlockSpec((None, None, d, de), lambda i, be, br, bs: (layer, be[i], 0, 0)),
            pl.BlockSpec((None, None, de, d), lambda i, be, br, bs: (layer, be[i], 0, 0)),
        ],
        out_specs=pl.BlockSpec((MOE_ROWS, half), lambda i, be, br, bs: (bs[i], 0)),
        scratch_shapes=[pltpu.VMEM((d, de), BF16), pltpu.VMEM((d, de), BF16), pltpu.VMEM((de, d), BF16)],
    )
    return pl.pallas_call(
        _expert_kernel,
        grid_spec=grid_spec,
        out_shape=jax.ShapeDtypeStruct((rows, half), jnp.uint32),
        compiler_params=_cparams("arbitrary"),
        name="moe_experts",
    )(block_e, block_rows, block_src, xs, w_gate, w_up, w_down)


def _combine_kernel(x_ref, m_ref, y_ref, w_ref, fg_ref, o_ref):
    x = x_ref[...] + _moe_residual(y_ref, w_ref, m_ref[5])
    o_ref[...] = _rms(x) * fg_ref[...]


def _combine(x, mods, layer, segs, seg, yg, wcol, final_g):
    n, d = x.shape
    tm = WIDE_TILE
    mod_row = _wide_mod_row(segs, tm)
    t0 = seg.row0 // tm
    steps = seg.rows // tm
    in_specs = [pl.BlockSpec((tm, d), lambda i: (t0 + i, 0)),
                pl.BlockSpec((None, None, 6, 1, d), lambda i: (layer, mod_row(t0 + i), 0, 0, 0)),
                pl.BlockSpec((2, tm, d // 2), lambda i: (0, i, 0)),
                pl.BlockSpec((tm, 8), lambda i: (t0 + i, 0)),
                pl.BlockSpec((1, d), lambda i: (0, 0))]
    return pl.pallas_call(
        _combine_kernel,
        grid=(steps,),
        in_specs=in_specs,
        out_specs=pl.BlockSpec((tm, d), lambda i: (i, 0)),
        out_shape=jax.ShapeDtypeStruct((seg.rows, d), F32),
        compiler_params=_cparams("parallel"),
        name="moe_combine_final",
    )(x, mods, yg, wcol, final_g)


def _sc_mesh():
    return plsc.VectorSubcoreMesh(core_axis_name="c", subcore_axis_name="s")


def _sc_worker_split(n):
    workers = SC_CORES * SC_SUBCORES
    per = n // workers
    assert per * workers == n and per % SC_CHUNK == 0
    return workers, per, per // SC_CHUNK


def _sc_dispatch(h, pos, rows):
    n, w = h.shape
    workers, per, chunks = _sc_worker_split(n)

    @functools.partial(
        pl.kernel, out_type=jax.ShapeDtypeStruct((rows, w), h.dtype), mesh=_sc_mesh(),
        scratch_types=[pltpu.VMEM((2, chunks, SC_CHUNK), jnp.int32), pltpu.VMEM((SC_CHUNK, w), h.dtype)],
        name="moe_dispatch_scatter")
    def scatter_rows(h_hbm, pos_hbm, xs_hbm, idx_v, rows_v):
        wid = lax.axis_index("s") * SC_CORES + lax.axis_index("c")
        pltpu.sync_copy(pos_hbm.at[0, wid], idx_v.at[0])
        pltpu.sync_copy(pos_hbm.at[1, wid], idx_v.at[1])

        @pl.loop(0, chunks)
        def _(c):
            pltpu.sync_copy(h_hbm.at[pl.ds(wid * per + c * SC_CHUNK, SC_CHUNK)], rows_v)
            pltpu.sync_copy(rows_v, xs_hbm.at[idx_v.at[0, c]])
            pltpu.sync_copy(rows_v, xs_hbm.at[idx_v.at[1, c]])

    return scatter_rows(h, pos.reshape(2, workers, chunks, SC_CHUNK))


def _sc_gather2(ys, pos):
    _, w = ys.shape
    n = pos.shape[1]
    workers, per, chunks = _sc_worker_split(n)

    @functools.partial(
        pl.kernel, out_type=jax.ShapeDtypeStruct((2, n, w), ys.dtype), mesh=_sc_mesh(),
        scratch_types=[pltpu.VMEM((2, chunks, SC_CHUNK), jnp.int32), pltpu.VMEM((SC_CHUNK, w), ys.dtype),
                       pltpu.SemaphoreType.DMA],
        name="moe_combine_gather")
    def gather_rows(ys_hbm, pos_hbm, out_hbm, idx_v, rows_v, sem):
        wid = lax.axis_index("s") * SC_CORES + lax.axis_index("c")
        pltpu.sync_copy(pos_hbm.at[0, wid], idx_v.at[0])
        pltpu.sync_copy(pos_hbm.at[1, wid], idx_v.at[1])

        @pl.loop(0, chunks)
        def _(c):
            for k in range(2):
                pltpu.async_copy(ys_hbm.at[idx_v.at[k, c]], rows_v, sem).wait()
                pltpu.sync_copy(rows_v, out_hbm.at[k, pl.ds(wid * per + c * SC_CHUNK, SC_CHUNK)])

    return gather_rows(ys, pos.reshape(2, workers, chunks, SC_CHUNK))


def _dispatch_plan(idx, rank, counts):
    n = idx.shape[1]
    padded = (counts + MOE_ROWS - 1) // MOE_ROWS * MOE_ROWS
    pad_end = jnp.cumsum(padded)
    pad_start = pad_end - padded
    experts = jnp.arange(N_EXPERTS, dtype=jnp.int32)
    start_of = jnp.sum(jnp.where(idx[..., None] == experts, pad_start, 0), axis=-1)
    pos = start_of + rank
    n_blocks = 2 * n // MOE_ROWS + N_EXPERTS
    steps = jnp.arange(n_blocks, dtype=jnp.int32)
    last_used = pad_end[-1] // MOE_ROWS - 1
    step = jnp.minimum(steps, last_used)
    block_e = jnp.minimum(jnp.sum(step[:, None] * MOE_ROWS >= pad_end[None, :], axis=1), N_EXPERTS - 1)
    pick = lambda per_expert: jnp.sum(jnp.where(block_e[:, None] == experts, per_expert, 0), axis=-1)
    first, count = pick(pad_start // MOE_ROWS), jnp.maximum(pick(padded // MOE_ROWS), 1)
    block_src = first + (step - first - 1) % count
    block_rows = jnp.clip(pick(pad_start + counts) - block_src * MOE_ROWS, 0, MOE_ROWS)
    block_rows = jnp.where(steps <= last_used, block_rows, 0)
    return (pos, block_e.astype(jnp.int32), block_rows.astype(jnp.int32), block_src.astype(jnp.int32),
            n_blocks * MOE_ROWS)


def _moe(x, mods, layer, segs, g2, rw_hi, rw_lo, rb, w_gate, w_up, w_down, final_g):
    h2p, idx, rank, wcol, cnt = _router(x, mods, layer, segs, g2, rw_hi, rw_lo, rb)
    pos, block_e, block_rows, block_src, rows = _dispatch_plan(idx, rank, cnt[:, 0].astype(jnp.int32))
    xs = _sc_dispatch(h2p, pos, rows)
    ys = _experts(xs, block_e, block_rows, block_src, w_gate, w_up, w_down, layer)
    yg = {seg: _sc_gather2(ys, pos[:, seg.row0:seg.row0 + seg.rows]) for seg in segs}
    if final_g is None:
        return yg, wcol, mods, layer
    return tuple(_combine(x, mods, layer, segs, seg, yg[seg], wcol, final_g) for seg in segs)


def _mla_rope_tables(t):
    axis_dim = MLA_ROPE // 2
    row = np.repeat(np.arange(t // GRID_W), GRID_W).astype(np.float64)
    col = np.tile(np.arange(GRID_W), t // GRID_W).astype(np.float64)
    inv = ROPE_BASE ** (-np.arange(0, axis_dim, 2, dtype=np.float64) / axis_dim)
    ar, ac = row[:, None] * inv[None, :], col[:, None] * inv[None, :]
    ones = np.ones((t, LANE - MLA_ROPE))
    cos = np.concatenate([np.cos(ar), np.cos(ar), np.cos(ac), np.cos(ac), ones], axis=-1)
    sin = np.concatenate([-np.sin(ar), np.sin(ar), -np.sin(ac), np.sin(ac), 0.0 * ones], axis=-1)
    return jnp.asarray(cos, F32), jnp.asarray(sin, F32)


def _ret_rot_tables(t, dk):
    inv = ROPE_BASE ** (-np.linspace(0.0, 1.0, dk // 2))
    ang = np.arange(t, dtype=np.float64)[:, None] * inv[None, :]
    return jnp.asarray(np.cos(ang), F32), jnp.asarray(np.sin(ang), F32)


def _dft_tables(n):
    k = np.arange(n, dtype=np.int64)
    ang = (np.outer(k, k) % n).astype(np.float64) * (2.0 * math.pi / n)
    return jnp.asarray(np.cos(ang), BF16), jnp.asarray(np.sin(ang), BF16)


def _mla_weights(w_in, q_g, kv_g, w_uq, w_ukv, w_o):
    d = w_in.shape[0]
    hd = MLA_NOPE + MLA_ROPE
    perm = _rope_perm()
    w_in_p = jnp.concatenate([w_in, w_in[:, MLA_Q_LORA + MLA_KV_LORA + perm]], axis=1)
    uq = w_uq.reshape(MLA_Q_LORA, MLA_HEADS, hd)
    uq = jnp.concatenate([uq, uq[..., MLA_NOPE + perm]], axis=-1)
    ukv = w_ukv.reshape(MLA_KV_LORA, MLA_HEADS, MLA_NOPE + MLA_V)
    return {
        "w_in": w_in_p.astype(BF16),
        "q_g": q_g.reshape(1, -1) * (MLA_NOPE + MLA_ROPE) ** -0.5,
        "kv_g": kv_g.reshape(1, -1),
        "w_q": uq.reshape(MLA_Q_LORA, MLA_HEADS * 2 * LANE).astype(BF16),
        "w_kn": ukv[..., :MLA_NOPE].reshape(MLA_KV_LORA, MLA_HEADS * MLA_NOPE).astype(BF16),
        "w_v": ukv[..., MLA_NOPE:].reshape(MLA_KV_LORA, MLA_HEADS * MLA_V).astype(BF16),
        "w_o": w_o.astype(BF16),
    }


def kernel(x_prompt, x_sample, cache_mla, state_ret, c, c_ctx, norm1_g, norm2_g, ada_w, ada_b, final_norm_g,
           mla_w_in, mla_q_norm_g, mla_kv_norm_g, mla_w_uq, mla_w_ukv, mla_w_o, ret_w_in, ret_decay_f,
           ret_decay_b, ret_w_o, fnet_w, router_w, router_b, moe_w_gate, moe_w_up, moe_w_down):
    b_ctx, t_ctx, d = x_prompt.shape
    b_lat, t_lat, _ = x_sample.shape
    depth = ada_w.shape[0]
    assert b_lat + 1 <= 8
    n_ctx = b_ctx * t_ctx
    ctx = _Seg(0, b_ctx, t_ctx, 0, False)
    lat = _Seg(n_ctx, b_lat, t_lat, 1, True)
    segs = (ctx, lat)

    n_lat = b_lat * t_lat
    n_mla = mla_w_in.shape[0]
    assert n_mla >= 1
    x = None
    new_cache = jnp.zeros((b_ctx, n_mla, t_ctx, MLA_KV_LORA + MLA_ROPE), F32)
    cond8 = jnp.concatenate([c_ctx[None, :], c, jnp.zeros((8 - 1 - b_lat, d), F32)], axis=0)
    mods = _modulation_all(cond8, ada_w, ada_b).reshape(depth, 8, 6, 1, d)

    rw_t = router_w.T.astype(F32)
    rw_hi = rw_t.astype(BF16)
    rw_lo = (rw_t - rw_hi.astype(F32)).astype(BF16)
    rb = router_b.reshape(N_EXPERTS, 1).astype(F32)
    final_g = final_norm_g.reshape(1, d)
    dk = ret_w_in.shape[2] // (8 * RET_HEADS)
    dv = 2 * dk

    states = []
    pending = None
    counters = [0, 0, 0]
    for layer in range(depth):
        kind = layer % 3
        j = counters[kind]
        counters[kind] += 1
        g1 = norm1_g[layer].reshape(1, d)
        g2 = norm2_g[layer].reshape(1, d)
        if kind == 0:
            w = _mla_weights(mla_w_in[j], mla_q_norm_g[j], mla_kv_norm_g[j], mla_w_uq[j], mla_w_ukv[j],
                             mla_w_o[j])
            if x is None:
                xc, xc0, xl, xl0 = x_prompt.reshape(n_ctx, d), 0, x_sample.reshape(n_lat, d), 0
            else:
                xc, xc0, xl, xl0 = x, ctx.row0, x, lat.row0
            past = cache_mla.shape[2]
            cpad = jnp.pad(cache_mla[:, j].reshape(b_lat * past, -1), ((0, 0), (0, LANE - MLA_ROPE)))
            kp, vp = _cache_kv(cpad, w)
            if pending is None:
                qc, kc, vc, new_cache = _mla_proj(xc, xc0, mods, layer, ctx, g1, w, None, (new_cache, j, n_mla))
                ql, kl, vl = _mla_proj(xl, xl0, mods, layer, lat, g1, w, _mla_rope_tables(t_lat))
            else:
                qc, kc, vc, new_cache, x = _mla_proj(x, ctx.row0, mods, layer, ctx, g1, w, None,
                                                     (new_cache, j, n_mla), pending)
                ql, kl, vl, x = _mla_proj(x, lat.row0, mods, layer, lat, g1, w, _mla_rope_tables(t_lat), None,
                                          pending)
                xc = xl = x
            first = x is None
            x = _attention(xc, xc0, n_ctx + n_lat, jnp.zeros((n_ctx + n_lat, d), F32) if first else "inplace",
                           mods, layer, ctx, qc, [(kc, vc, t_ctx)], w["w_o"])
            x = _attention(xl if first else x, xl0, n_ctx + n_lat, x if first else "inplace", mods, layer, lat,
                           ql, [(kl, vl, t_lat), (kp, vp, past)], w["w_o"])
        elif kind == 1:
            w_in = ret_w_in[j]
            qk = RET_HEADS * dk
            w_in_b = w_in.astype(BF16)
            w_o_b = ret_w_o[j].astype(BF16)
            rot = _ret_rot_tables(t_lat, dk)
            n_qk, n_v = 2 * qk // RET_COL, RET_HEADS * dv // RET_COL
            assert (n_qk + n_v) * RET_COL * 2 == w_in.shape[1]
            parts = []
            for seg in segs:
                kinds = (("rotary" if seg is lat else "plain"),) * n_qk + ("plain",) * n_v
                qkv = _ret_proj(x, mods, layer, seg, g1, w_in_b, 0, kinds, rot, dk, pending)
                if pending is not None:
                    qkv, x = qkv
                parts.append((qkv, _ret_proj(x, mods, layer, seg, g1, w_in_b, 1, ("silu",) * (n_qk + n_v), None, dk)))
            x, s_ctx = _ret_scan(*parts[0], ctx, ret_decay_f[j], ret_decay_b[j], None, True, dk, dv, RET_HEADS,
                                 out_proj=(x, mods, layer, w_o_b))
            yl, _ = _ret_scan(*parts[1], lat, ret_decay_f[j], ret_decay_b[j], state_ret[:, j], False, dk, dv, 1)
            x = _matmul_residual(x, mods, layer, lat, yl, w_o_b)
            states.append(s_ctx)
        else:
            gd = d // FNET_GROUPS
            cc, sc = _dft_tables(gd)
            cs = jnp.concatenate([cc, sc], axis=1)
            w_b = fnet_w[j].astype(BF16)
            for seg in segs:
                ct, st = _dft_tables(seg.seq)
                if seg.seq <= FNET_ROWS:
                    x = _fnet_short(x, mods, layer, seg, g1, cs, ct, st, w_b, (seg.seq * gd) ** -0.5, pending)
                    continue
                if pending is None:
                    ac, as_ = _fnet_a(x, mods, layer, seg, g1, cs)
                else:
                    ac, as_, x = _fnet_a(x, mods, layer, seg, g1, cs, pending)
                x = _fnet_b(x, mods, layer, seg, ac, as_, ct, st, w_b, (seg.seq * gd) ** -0.5)
        if layer < depth - 1:
            pending = _moe(x, mods, layer, segs, g2, rw_hi, rw_lo, rb, moe_w_gate, moe_w_up, moe_w_down, None)
        else:
            y_prompt, y_sample = _moe(x, mods, layer, segs, g2, rw_hi, rw_lo, rb, moe_w_gate, moe_w_up,
                                      moe_w_down, final_g)

    new_state = jnp.stack(states, axis=1)
    return (y_prompt.reshape(b_ctx, t_ctx, d), y_sample.reshape(b_lat, t_lat, d), new_cache, new_state)
```

```python
import functools
import math

import jax
import jax.numpy as jnp
import numpy as np
from jax import lax
from jax.experimental import pallas as pl
from jax.experimental.pallas import tpu as pltpu
from jax.experimental.pallas import tpu_sc as plsc

F32 = jnp.float32
BF16 = jnp.bfloat16

GRID_W = 64
MLA_HEADS = 8
MLA_NOPE = 128
MLA_ROPE = 64
MLA_V = 128
MLA_Q_LORA = 384
MLA_KV_LORA = 256
ROPE_BASE = 10000.0
RET_HEADS = 4
RET_CHUNK = 256
FNET_GROUPS = 4
N_EXPERTS = 16
N_EXPERT_GROUPS = 4
EXPERTS_PER_GROUP = 4
D_EXPERT = 512
NORM_EPS = 1e-6

LANE = 128
PROJ_ROWS = 512
ATTN_ROWS = 512
FNET_ROWS = 512
ATTN_SHORT_SEQS = 4
WIDE_TILE = 1024
ROUTER_SUB_ROWS = 256
MOE_ROWS = 1024
MOE_TAIL_ROWS = 256
VMEM_LIMIT = 56 * 1024 * 1024
SC_CORES = 2
SC_SUBCORES = 16
SC_CHUNK = 128


def _cparams(*sem):
    return pltpu.CompilerParams(dimension_semantics=sem, vmem_limit_bytes=VMEM_LIMIT)


def _sigmoid(x):
    return 1.0 / (1.0 + jnp.exp(-x))


def _rms(x):
    return x * lax.rsqrt(jnp.mean(x * x, axis=-1, keepdims=True) + NORM_EPS)


def _modulate(x, g, shift, scale):
    return _rms(x) * (g * (1.0 + scale)) + shift


def _dot(a, b):
    return jnp.dot(a, b, preferred_element_type=F32)


def _dot_nt(a, b):
    return lax.dot_general(a, b, (((1,), (1,)), ((), ())), preferred_element_type=F32)


def _mod_kernel(c_ref, w_ref, b_ref, o_ref):
    c = c_ref[...]
    s = (c * _sigmoid(c)).astype(BF16)
    o_ref[...] = _dot(s, w_ref[...].astype(BF16)) + b_ref[...]


def _modulation_all(cond8, ada_w, ada_b):
    depth, d, d6 = ada_w.shape
    tn = d6 // 4
    return pl.pallas_call(
        _mod_kernel,
        grid=(depth, d6 // tn),
        in_specs=[
            pl.BlockSpec((8, d), lambda l, n: (0, 0)),
            pl.BlockSpec((None, d, tn), lambda l, n: (l, 0, n)),
            pl.BlockSpec((None, 1, tn), lambda l, n: (l, 0, n)),
        ],
        out_specs=pl.BlockSpec((None, 8, tn), lambda l, n: (l, 0, n)),
        out_shape=jax.ShapeDtypeStruct((depth, 8, d6), F32),
        compiler_params=_cparams("parallel", "parallel"),
        name="modulation",
    )(cond8, ada_w, ada_b.reshape(depth, 1, d6))


class _Seg:
    def __init__(self, row0, batch, seq, mod0, per_batch_mod):
        self.row0, self.batch, self.seq = row0, batch, seq
        self.mod0, self.per_batch_mod = mod0, per_batch_mod
        self.rows = batch * seq

    def tile(self, want):
        tm = min(want, self.seq) if self.per_batch_mod else want
        assert self.rows % tm == 0 and self.row0 % tm == 0 and (self.seq % tm == 0 or tm % self.seq == 0)
        return tm

    def seq_tile(self, want):
        tm = min(want, self.seq)
        assert self.seq % tm == 0 and self.row0 % tm == 0
        return tm

    def mod_row(self, tile, tm):
        if self.per_batch_mod:
            return self.mod0 + tile * tm // self.seq
        return self.mod0


def _mod_spec(layer, seg, d, tm, tile_of=lambda *a: a[0]):
    return pl.BlockSpec((None, None, 6, 1, d), lambda *a: (layer, seg.mod_row(tile_of(*a), tm), 0, 0, 0))


def _unpack_halves(p):
    lo = lax.bitcast_convert_type(p << 16, F32)
    hi = lax.bitcast_convert_type(p & jnp.uint32(0xFFFF0000), F32)
    return lo, hi


def _moe_residual(y_ref, w_ref, gate):
    w = w_ref[...]
    lo0, hi0 = _unpack_halves(y_ref[0])
    lo1, hi1 = _unpack_halves(y_ref[1])
    w0, w1 = w[:, 0:1], w[:, 1:2]
    return gate * jnp.concatenate([w0 * lo0 + w1 * lo1, w0 * hi0 + w1 * hi1], axis=-1)


def _take_pending(refs, pending):
    if not pending:
        return refs, lambda x: x
    y_ref, w_ref, pm_ref, *rest = refs
    xo_ref = rest.pop()

    def resolve(x):
        x = x + _moe_residual(y_ref, w_ref, pm_ref[5])
        xo_ref[...] = x
        return x

    return rest, resolve


def _pending_io(pending, seg, tm, d, n):
    yg_by_seg, wcol, mods, layer = pending
    t0 = seg.row0 // tm
    specs = [pl.BlockSpec((2, tm, d // 2), lambda i: (0, i, 0)),
             pl.BlockSpec((tm, 8), lambda i: (t0 + i, 0)),
             _mod_spec(layer, seg, d, tm)]
    return (specs, [yg_by_seg[seg], wcol, mods], pl.BlockSpec((tm, d), lambda i: (t0 + i, 0)),
            jax.ShapeDtypeStruct((n, d), F32))


def _rope_partner(x):
    return pltpu.roll(x, LANE // 2, 1)


def _rope_perm():
    return np.array([l + 16 if l % 32 < 16 else l - 16 for l in range(MLA_ROPE)])


def _store_values(v_ref, v):
    ones = jnp.ones((v.shape[0], LANE), BF16)
    for hd in range(MLA_HEADS):
        v_ref[:, hd * 2 * LANE:hd * 2 * LANE + LANE] = v[:, hd * MLA_V:(hd + 1) * MLA_V].astype(BF16)
        v_ref[:, hd * 2 * LANE + LANE:(hd + 1) * 2 * LANE] = ones


def _mla_proj_kernel(*refs, rope, pending):
    refs, resolve = _take_pending(refs, pending)
    if rope:
        (x_ref, m_ref, g_ref, win_ref, qg_ref, kvg_ref, wq_ref, wkn_ref, wv_ref, cos_ref, sin_ref,
         q_ref, k_ref, v_ref) = refs
    else:
        x_ref, m_ref, g_ref, win_ref, qg_ref, kvg_ref, wq_ref, wkn_ref, wv_ref = refs[:9]
        q_ref, k_ref, v_ref, cache_ref = refs[-4:]
    h = _modulate(resolve(x_ref[...]), g_ref[...], m_ref[0], m_ref[1]).astype(BF16)
    z = _dot(h, win_ref[...])
    cq = z[:, :MLA_Q_LORA]
    ckv = z[:, MLA_Q_LORA:MLA_Q_LORA + MLA_KV_LORA]
    kpe = z[:, MLA_Q_LORA + MLA_KV_LORA:]
    cqn = (_rms(cq) * qg_ref[...]).astype(BF16)
    ckvn = _rms(ckv) * kvg_ref[...]
    ckvb = ckvn.astype(BF16)
    q = _dot(cqn, wq_ref[...])
    kn = _dot(ckvb, wkn_ref[...])
    _store_values(v_ref, _dot(ckvb, wv_ref[...]))
    if rope:
        cos, sin = cos_ref[...], sin_ref[...]
        kpe = kpe * cos + _rope_partner(kpe) * sin
    kpe = jnp.where(lax.broadcasted_iota(jnp.int32, kpe.shape, 1) < MLA_ROPE, kpe, 0.0)
    if not rope:
        seq = cache_ref.shape[1]
        for s in range(cache_ref.shape[0]):
            cache_ref[s, :, :MLA_KV_LORA] = ckvn[s * seq:(s + 1) * seq, :]
            cache_ref[s, :, MLA_KV_LORA:] = kpe[s * seq:(s + 1) * seq, :MLA_ROPE]
    kpe_b = kpe.astype(BF16)
    for hd in range(MLA_HEADS):
        lo = hd * 2 * LANE
        q_ref[:, lo:lo + LANE] = q[:, lo:lo + LANE].astype(BF16)
        qr = q[:, lo + LANE:lo + 2 * LANE]
        if rope:
            qr = qr * cos + _rope_partner(qr) * sin
        q_ref[:, lo + LANE:lo + 2 * LANE] = qr.astype(BF16)
        k_ref[:, lo:lo + LANE] = kn[:, hd * LANE:(hd + 1) * LANE].astype(BF16)
        k_ref[:, lo + LANE:lo + 2 * LANE] = kpe_b


def _mla_proj(x, x_row0, mods, layer, seg, g1, w, rope_tabs, cache_slot=None, pending=None):
    n, d = x.shape
    rope = rope_tabs is not None
    rows = seg.rows
    tm = seg.tile(PROJ_ROWS)
    x_tile0 = x_row0 // tm
    hq = MLA_HEADS * 2 * LANE
    const = lambda i: (0, 0)
    aliases = {}
    in_specs = [
        pl.BlockSpec((tm, d), lambda i: (x_tile0 + i, 0)),
        _mod_spec(layer, seg, d, tm),
        pl.BlockSpec((1, d), const),
        pl.BlockSpec(w["w_in"].shape, const),
        pl.BlockSpec((1, MLA_Q_LORA), const),
        pl.BlockSpec((1, MLA_KV_LORA), const),
        pl.BlockSpec(w["w_q"].shape, const),
        pl.BlockSpec(w["w_kn"].shape, const),
        pl.BlockSpec(w["w_v"].shape, const),
    ]
    args = [x, mods, g1, w["w_in"], w["q_g"], w["kv_g"], w["w_q"], w["w_kn"], w["w_v"]]
    out_specs = [pl.BlockSpec((tm, hq), lambda i: (i, 0))] * 3
    out_shape = [jax.ShapeDtypeStruct((rows, hq), BF16)] * 3
    if rope:
        tab = pl.BlockSpec((tm, LANE), lambda i: (i % (seg.seq // tm), 0))
        in_specs += [tab, tab]
        args += list(rope_tabs)
    else:
        cw = MLA_KV_LORA + MLA_ROPE
        prev, slot, n_slots = cache_slot
        assert tm % seg.seq == 0
        out_specs.append(pl.BlockSpec((tm // seg.seq, None, seg.seq, cw), lambda i: (i, slot, 0, 0)))
        out_shape.append(jax.ShapeDtypeStruct((seg.batch, n_slots, seg.seq, cw), F32))
        in_specs.append(pl.BlockSpec(memory_space=pl.ANY))
        args.append(prev)
        aliases = {len(args) - 1: 3}
    if pending is not None:
        p_specs, p_args, xo_spec, xo_shape = _pending_io(pending, seg, tm, d, n)
        in_specs, args = p_specs + in_specs, p_args + args
        out_specs.append(xo_spec)
        out_shape.append(xo_shape)
        aliases = {k + len(p_args): v for k, v in aliases.items()}
        aliases[len(p_args)] = len(out_shape) - 1
    return pl.pallas_call(
        functools.partial(_mla_proj_kernel, rope=rope, pending=pending is not None),
        grid=(rows // tm,),
        in_specs=in_specs,
        out_specs=out_specs,
        out_shape=out_shape,
        input_output_aliases=aliases,
        compiler_params=_cparams("parallel"),
        name="mla_proj_lat" if rope else "mla_proj_ctx",
    )(*args)


def _cache_kv_kernel(c_ref, wkn_ref, wv_ref, k_ref, v_ref):
    c = c_ref[...]
    ckv = c[:, :MLA_KV_LORA].astype(BF16)
    kpe_b = c[:, MLA_KV_LORA:].astype(BF16)
    kn = _dot(ckv, wkn_ref[...])
    _store_values(v_ref, _dot(ckv, wv_ref[...]))
    for hd in range(MLA_HEADS):
        lo = hd * 2 * LANE
        k_ref[:, lo:lo + LANE] = kn[:, hd * LANE:(hd + 1) * LANE].astype(BF16)
        k_ref[:, lo + LANE:lo + 2 * LANE] = kpe_b


def _cache_kv(cache_pad, w):
    rows, cw = cache_pad.shape
    hq = MLA_HEADS * 2 * LANE
    const = lambda i: (0, 0)
    tm = min(PROJ_ROWS, rows)
    assert rows % tm == 0
    return pl.pallas_call(
        _cache_kv_kernel,
        grid=(rows // tm,),
        in_specs=[
            pl.BlockSpec((tm, cw), lambda i: (i, 0)),
            pl.BlockSpec(w["w_kn"].shape, const),
            pl.BlockSpec(w["w_v"].shape, const),
        ],
        out_specs=[pl.BlockSpec((tm, hq), lambda i: (i, 0))] * 2,
        out_shape=[jax.ShapeDtypeStruct((rows, hq), BF16)] * 2,
        compiler_params=_cparams("parallel"),
        name="mla_cache_kv",
    )(cache_pad, w["w_kn"], w["w_v"])


def _attn_kernel(*refs, n_parts, n_seq):
    q_ref = refs[0]
    kv_refs = refs[1:1 + 2 * n_parts]
    wo_ref, x_ref, m_ref = refs[1 + 2 * n_parts:4 + 2 * n_parts]
    o_ref, acc_ref = refs[-2:]
    tq = q_ref.shape[0] // n_seq
    for sq, hd in [(sq, hd) for sq in range(n_seq) for hd in range(MLA_HEADS)]:
        rows = slice(sq * tq, (sq + 1) * tq)
        kcol = slice(hd * 2 * LANE, (hd + 1) * 2 * LANE)
        keys = [slice(sq * (r.shape[0] // n_seq), (sq + 1) * (r.shape[0] // n_seq)) for r in kv_refs[::2]]
        scores = [_dot_nt(q_ref[rows, kcol], kv_refs[2 * p][keys[p], kcol]) for p in range(n_parts)]
        mx = scores[0].max(axis=-1, keepdims=True)
        for s in scores[1:]:
            mx = jnp.maximum(mx, s.max(axis=-1, keepdims=True))
        out = None
        for p, s in enumerate(scores):
            e = jnp.exp((s - mx).astype(BF16))
            pv = _dot(e, kv_refs[2 * p + 1][keys[p], kcol])
            out = pv if out is None else out + pv
        acc_ref[rows, hd * MLA_V:(hd + 1) * MLA_V] = (out[:, :MLA_V] / out[:, LANE:LANE + MLA_V]).astype(BF16)
    y = _dot(acc_ref[...], wo_ref[...])
    o_ref[...] = x_ref[...] + m_ref[2] * y


def _attention(x, x_row0, n, dest, mods, layer, seg, q, kv_parts, w_o):
    d = x.shape[1]
    hq = MLA_HEADS * 2 * LANE
    hv = MLA_HEADS * MLA_V
    tq = seg.seq_tile(ATTN_ROWS)
    tps = seg.seq // tq
    n_seq = ATTN_SHORT_SEQS if (tps == 1 and not seg.per_batch_mod and seg.batch % ATTN_SHORT_SEQS == 0) else 1
    tq *= n_seq
    x_tile0, out_tile0 = x_row0 // tq, seg.row0 // tq
    in_specs = [pl.BlockSpec((tq, hq), lambda b, i: (b * tps + i, 0))]
    args = [q]
    for k, v, rows in kv_parts:
        mode = dict(pipeline_mode=pl.Buffered(1)) if tps > 1 else {}
        in_specs += [pl.BlockSpec((n_seq * rows, hq), lambda b, i: (b, 0)),
                     pl.BlockSpec((n_seq * rows, hq), lambda b, i: (b, 0), **mode)]
        args += [k, v]
    in_specs += [
        pl.BlockSpec(w_o.shape, lambda b, i: (0, 0)),
        pl.BlockSpec((tq, d), lambda b, i: (x_tile0 + b * tps + i, 0)),
        _mod_spec(layer, seg, d, tq // n_seq, tile_of=lambda b, i: (b * tps + i) * n_seq),
    ]
    args += [w_o, x, mods]
    if isinstance(dest, str):
        assert dest == "inplace"
        aliases = {len(args) - 2: 0}
    else:
        in_specs.append(pl.BlockSpec(memory_space=pl.ANY))
        args.append(dest)
        aliases = {len(args) - 1: 0}
    return pl.pallas_call(
        functools.partial(_attn_kernel, n_parts=len(kv_parts), n_seq=n_seq),
        grid=(seg.batch // n_seq, tps),
        in_specs=in_specs,
        out_specs=pl.BlockSpec((tq, d), lambda b, i: (out_tile0 + b * tps + i, 0)),
        out_shape=jax.ShapeDtypeStruct((n, d), F32),
        scratch_shapes=[pltpu.VMEM((tq, hv), BF16)],
        input_output_aliases=aliases,
        compiler_params=_cparams("parallel", "arbitrary"),
        name="mla_attention",
    )(*args)


RET_COL = 1024


RET_ROWS = 1024


def _ret_proj_kernel(*refs, kinds, dk, pending):
    refs, resolve = _take_pending(refs, pending)
    rotary = "rotary" in kinds
    if rotary:
        x_ref, m_ref, g_ref, w_ref, cos_ref, sin_ref, z_ref = refs
        cos, sin = cos_ref[...], sin_ref[...]
    else:
        x_ref, m_ref, g_ref, w_ref, z_ref = refs
    h = _modulate(resolve(x_ref[...]), g_ref[...], m_ref[0], m_ref[1]).astype(BF16)
    half = dk // 2
    for j, kind in enumerate(kinds):
        c0 = j * RET_COL
        acc = _dot(h, w_ref[:, c0:c0 + RET_COL])
        if kind == "rotary":
            for hd in range(RET_COL // dk):
                lo = hd * dk
                x1, x2 = acc[:, lo:lo + half], acc[:, lo + half:lo + dk]
                z_ref[:, c0 + lo:c0 + lo + half] = (x1 * cos - x2 * sin).astype(BF16)
                z_ref[:, c0 + lo + half:c0 + lo + dk] = (x1 * sin + x2 * cos).astype(BF16)
        elif kind == "silu":
            z_ref[:, c0:c0 + RET_COL] = (acc * _sigmoid(acc)).astype(BF16)
        else:
            z_ref[:, c0:c0 + RET_COL] = acc.astype(BF16)


def _ret_proj(x, mods, layer, seg, g1, w_in, group, kinds, rot_tabs, dk, pending=None):
    n, d = x.shape
    rows = seg.rows
    tm = seg.tile(RET_ROWS)
    ncol = len(kinds) * RET_COL
    tps = max(seg.seq // tm, 1)
    in_specs = [
        pl.BlockSpec((tm, d), lambda i: (seg.row0 // tm + i, 0)),
        _mod_spec(layer, seg, d, tm),
        pl.BlockSpec((1, d), lambda i: (0, 0)),
        pl.BlockSpec((d, ncol), lambda i: (0, group), pipeline_mode=pl.Buffered(1)),
    ]
    args = [x, mods, g1, w_in]
    if "rotary" in kinds:
        tab = pl.BlockSpec((tm, dk // 2), lambda i: (i % tps, 0))
        in_specs += [tab, tab]
        args += list(rot_tabs)
    out_specs = [pl.BlockSpec((tm, ncol), lambda i: (i, 0))]
    out_shape = [jax.ShapeDtypeStruct((rows, ncol), BF16)]
    aliases = {}
    if pending is not None:
        p_specs, p_args, xo_spec, xo_shape = _pending_io(pending, seg, tm, d, n)
        in_specs, args = p_specs + in_specs, p_args + args
        out_specs.append(xo_spec)
        out_shape.append(xo_shape)
        aliases = {len(p_args): 1}
    res = pl.pallas_call(
        functools.partial(_ret_proj_kernel, kinds=kinds, dk=dk, pending=pending is not None),
        grid=(rows // tm,),
        in_specs=in_specs,
        out_specs=out_specs,
        out_shape=out_shape,
        input_output_aliases=aliases,
        compiler_params=_cparams("parallel"),
        name="ret_proj_" + kinds[0],
    )(*args)
    return res if pending is not None else res[0]


def _log_sigmoid(x):
    return jnp.minimum(x, 0.0) - jnp.log(1.0 + jnp.exp(-jnp.abs(x)))


def _ret_scan_kernel(*refs, has_s0, emit_state, n_chunks, heads, fuse_out):
    refs = list(refs)
    lf_ref, lb_ref, q_ref, k_ref, v_ref, gf_ref, gb_ref = refs[:7]
    pos = 7
    s0_ref = None
    if has_s0:
        s0_ref = refs[pos]
        pos += 1
    if fuse_out:
        wo_ref, x_ref, m_ref, o_ref = refs[pos:pos + 4]
        pos += 4
    else:
        y_ref = refs[pos]
        pos += 1
    sout_ref = None
    if emit_state:
        sout_ref = refs[pos]
        pos += 1
    if fuse_out:
        s_ref, yf_ref, y_ref = refs[pos:]
    else:
        s_ref, yf_ref = refs[pos:]
    c = RET_CHUNK
    dk, dv = s_ref.shape
    ii = lax.broadcasted_iota(jnp.int32, (c, c), 0).astype(F32)
    jj = lax.broadcasted_iota(jnp.int32, (c, c), 1).astype(F32)
    idx = lax.broadcasted_iota(jnp.int32, (c, 1), 0).astype(F32)

    for hd, direction in [(hd, direction) for hd in range(heads) for direction in range(2)]:
        fwd = direction == 0
        kcol, vcol = slice(hd * dk, (hd + 1) * dk), slice(hd * dv, (hd + 1) * dv)
        lg = _log_sigmoid((lf_ref if fwd else lb_ref)[hd])
        rel = (ii - jj) if fwd else (jj - ii)
        keep = rel >= 0
        decay_in = jnp.where(keep, jnp.exp(jnp.where(keep, rel, 0.0) * lg), 0.0)
        decay_q = jnp.exp(((idx + 1.0) if fwd else (c - idx)) * lg)
        decay_k = jnp.exp(((c - 1.0 - idx) if fwd else idx) * lg)
        decay_c = jnp.exp(c * lg)
        g_ref = gf_ref if fwd else gb_ref

        def chunk(cc, state, fwd=fwd, decay_in=decay_in, decay_q=decay_q, decay_k=decay_k, decay_c=decay_c,
                  g_ref=g_ref, kcol=kcol, vcol=vcol):
            r0 = cc * c if isinstance(cc, int) else pl.multiple_of(cc * c, c)
            qc = q_ref[pl.ds(r0, c), kcol]
            kc = k_ref[pl.ds(r0, c), kcol]
            vc = v_ref[pl.ds(r0, c), vcol]
            sc = _dot_nt(qc, kc) * decay_in
            out = _dot(sc.astype(BF16), vc)
            kd_t = (kc.astype(F32) * decay_k).T.astype(BF16)
            new_s = _dot(kd_t, vc)
            if state is not None:
                out = out + decay_q * _dot(qc, state.astype(BF16))
                new_s = decay_c * state + new_s
            s_ref[...] = new_s
            o = _rms(out) * g_ref[pl.ds(r0, c), vcol].astype(F32)
            if fwd:
                yf_ref[pl.ds(r0, c), :] = o
            else:
                y_ref[pl.ds(r0, c), vcol] = (yf_ref[pl.ds(r0, c), :] + o).astype(BF16)

        chunk(0 if fwd else n_chunks - 1, s0_ref[direction, hd] if has_s0 else None)

        def step(ci, carry, fwd=fwd, chunk=chunk):
            chunk(ci if fwd else n_chunks - 1 - ci, s_ref[...])
            return carry

        lax.fori_loop(1, n_chunks, step, 0, unroll=True)
        if emit_state:
            sout_ref[direction, hd] = s_ref[...]
    if fuse_out:
        o_ref[...] = x_ref[...] + m_ref[2] * _dot(y_ref[...], wo_ref[...])


def _ret_scan(qkv, g, seg, logit_f, logit_b, s0, emit_state, dk, dv, heads, out_proj=None):
    rows = seg.batch * seg.seq
    t = seg.seq
    hh = RET_HEADS
    groups = hh // heads
    fuse_out = out_proj is not None
    assert not fuse_out or groups == 1
    v0 = 2 * hh * dk // (heads * dv)
    assert groups * heads == hh and v0 * heads * dv == 2 * hh * dk
    in_specs = [
        pl.BlockSpec((heads, 1, 1), lambda b, h: (h, 0, 0)),
        pl.BlockSpec((heads, 1, 1), lambda b, h: (h, 0, 0)),
        pl.BlockSpec((t, heads * dk), lambda b, h: (b, h)),
        pl.BlockSpec((t, heads * dk), lambda b, h: (b, groups + h)),
        pl.BlockSpec((t, heads * dv), lambda b, h: (b, v0 + h)),
        pl.BlockSpec((t, heads * dv), lambda b, h: (b, h)),
        pl.BlockSpec((t, heads * dv), lambda b, h: (b, groups + h)),
    ]
    args = [logit_f.reshape(hh, 1, 1), logit_b.reshape(hh, 1, 1), qkv, qkv, qkv, g, g]
    state_spec = pl.BlockSpec((None, 2, heads, dk, dv), lambda b, h: (b, 0, h, 0, 0))
    if s0 is not None:
        in_specs.append(state_spec)
        args.append(s0)
    scratch = [pltpu.VMEM((dk, dv), F32), pltpu.VMEM((t, dv), F32)]
    aliases = {}
    if fuse_out:
        x, mods, layer, w_o = out_proj
        n, d = x.shape
        x_spec = pl.BlockSpec((t, d), lambda b, h: (seg.row0 // t + b, 0))
        in_specs += [pl.BlockSpec(w_o.shape, lambda b, h: (0, 0)), x_spec,
                     _mod_spec(layer, seg, d, t, tile_of=lambda b, h: b)]
        args += [w_o, x, mods]
        aliases = {len(args) - 2: 0}
        out_specs, out_shape = [x_spec], [jax.ShapeDtypeStruct((n, d), F32)]
        scratch.append(pltpu.VMEM((t, hh * dv), BF16))
    else:
        out_specs = [pl.BlockSpec((t, heads * dv), lambda b, h: (b, h))]
        out_shape = [jax.ShapeDtypeStruct((rows, hh * dv), BF16)]
    if emit_state:
        out_specs.append(state_spec)
        out_shape.append(jax.ShapeDtypeStruct((seg.batch, 2, hh, dk, dv), F32))
    res = pl.pallas_call(
        functools.partial(_ret_scan_kernel, has_s0=s0 is not None, emit_state=emit_state,
                          n_chunks=t // RET_CHUNK, heads=heads, fuse_out=fuse_out),
        grid=(seg.batch, groups),
        in_specs=in_specs,
        out_specs=out_specs,
        out_shape=out_shape,
        scratch_shapes=scratch,
        input_output_aliases=aliases,
        compiler_params=_cparams("parallel", "parallel"),
        name="ret_scan",
    )(*args)
    return res if emit_state else (res[0], None)


def _mm_res_kernel(a_ref, w_ref, x_ref, m_ref, o_ref):
    o_ref[...] = x_ref[...] + m_ref[2] * _dot(a_ref[...], w_ref[...])


def _matmul_residual(x, mods, layer, seg, a, w):
    n, d = x.shape
    tm = seg.tile(RET_ROWS)
    x_spec = pl.BlockSpec((tm, d), lambda i: (seg.row0 // tm + i, 0))
    return pl.pallas_call(
        _mm_res_kernel,
        grid=(seg.rows // tm,),
        in_specs=[
            pl.BlockSpec((tm, a.shape[1]), lambda i: (i, 0)),
            pl.BlockSpec(w.shape, lambda i: (0, 0)),
            x_spec,
            _mod_spec(layer, seg, d, tm),
        ],
        out_specs=x_spec,
        out_shape=jax.ShapeDtypeStruct((n, d), F32),
        input_output_aliases={2: 0},
        compiler_params=_cparams("parallel"),
        name="matmul_residual",
    )(a, w, x, mods)


def _fnet_a_kernel(*refs, gd, pending):
    refs, resolve = _take_pending(refs, pending)
    x_ref, m_ref, g_ref, cs_ref, ac_ref, as_ref = refs
    h = _modulate(resolve(x_ref[...]), g_ref[...], m_ref[0], m_ref[1]).astype(BF16)
    cs = cs_ref[...]
    for g in range(FNET_GROUPS):
        a = _dot(h[:, g * gd:(g + 1) * gd], cs)
        ac_ref[:, g * gd:(g + 1) * gd] = a[:, :gd].astype(BF16)
        as_ref[:, g * gd:(g + 1) * gd] = a[:, gd:].astype(BF16)


def _fnet_a(x, mods, layer, seg, g1, cs, pending=None):
    n, d = x.shape
    rows = seg.rows
    tm = seg.tile(PROJ_ROWS)
    out = pl.BlockSpec((tm, d), lambda i: (i, 0))
    in_specs = [
        pl.BlockSpec((tm, d), lambda i: (seg.row0 // tm + i, 0)),
        _mod_spec(layer, seg, d, tm),
        pl.BlockSpec((1, d), lambda i: (0, 0)),
        pl.BlockSpec(cs.shape, lambda i: (0, 0)),
    ]
    args = [x, mods, g1, cs]
    out_specs, out_shape, aliases = [out, out], [jax.ShapeDtypeStruct((rows, d), BF16)] * 2, {}
    if pending is not None:
        p_specs, p_args, xo_spec, xo_shape = _pending_io(pending, seg, tm, d, n)
        in_specs, args = p_specs + in_specs, p_args + args
        out_specs.append(xo_spec)
        out_shape.append(xo_shape)
        aliases = {len(p_args): 2}
    return pl.pallas_call(
        functools.partial(_fnet_a_kernel, gd=d // FNET_GROUPS, pending=pending is not None),
        grid=(rows // tm,),
        in_specs=in_specs,
        out_specs=out_specs,
        out_shape=out_shape,
        input_output_aliases=aliases,
        compiler_params=_cparams("parallel"),
        name="fnet_channel_dft",
    )(*args)


def _fnet_b_kernel(ct_ref, st_ref, ac_ref, as_ref, w_ref, x_ref, m_ref, o_ref, *, norm):
    f = (_dot(ct_ref[...], ac_ref[...]) - _dot(st_ref[...], as_ref[...])) * norm
    o_ref[...] = x_ref[...] + m_ref[2] * _dot(f.astype(BF16), w_ref[...])


def _fnet_b(x, mods, layer, seg, ac, as_, ct, st, w, norm):
    n, d = x.shape
    t = seg.seq
    tq = seg.seq_tile(FNET_ROWS)
    tps = t // tq
    x_spec = pl.BlockSpec((tq, d), lambda b, i: (seg.row0 // tq + b * tps + i, 0))
    tab = pl.BlockSpec((tq, t), lambda b, i: (i, 0))
    seq = pl.BlockSpec((t, d), lambda b, i: (b, 0))
    return pl.pallas_call(
        functools.partial(_fnet_b_kernel, norm=norm),
        grid=(seg.batch, tps),
        in_specs=[tab, tab, seq, seq, pl.BlockSpec(w.shape, lambda b, i: (0, 0)), x_spec,
                  _mod_spec(layer, seg, d, tq, tile_of=lambda b, i: b * tps + i)],
        out_specs=x_spec,
        out_shape=jax.ShapeDtypeStruct((n, d), F32),
        input_output_aliases={5: 0},
        compiler_params=_cparams("parallel", "arbitrary"),
        name="fnet_position_dft",
    )(ct, st, ac, as_, w, x, mods)


def _fnet_short_kernel(*refs, gd, norm, n_seq, pending):
    if pending:
        y_ref, wr_ref, pm_ref, *refs = refs
    x_ref, m_ref, g_ref, cs_ref, ct_ref, st_ref, w_ref, o_ref, ac_ref, as_ref = refs
    x = x_ref[...]
    if pending:
        x = x + _moe_residual(y_ref, wr_ref, pm_ref[5])
    h = _modulate(x, g_ref[...], m_ref[0], m_ref[1]).astype(BF16)
    cs = cs_ref[...]
    for g in range(FNET_GROUPS):
        a = _dot(h[:, g * gd:(g + 1) * gd], cs)
        ac_ref[:, g * gd:(g + 1) * gd] = a[:, :gd].astype(BF16)
        as_ref[:, g * gd:(g + 1) * gd] = a[:, gd:].astype(BF16)
    t = x.shape[0] // n_seq
    ct, st = ct_ref[...], st_ref[...]
    f = [_dot(ct, ac_ref[s * t:(s + 1) * t, :]) - _dot(st, as_ref[s * t:(s + 1) * t, :]) for s in range(n_seq)]
    f = (f[0] if n_seq == 1 else jnp.concatenate(f, axis=0)) * norm
    o_ref[...] = x + m_ref[2] * _dot(f.astype(BF16), w_ref[...])


def _fnet_short(x, mods, layer, seg, g1, cs, ct, st, w, norm, pending=None):
    n, d = x.shape
    t = seg.seq
    n_seq = ATTN_SHORT_SEQS if (not seg.per_batch_mod and seg.batch % ATTN_SHORT_SEQS == 0) else 1
    rows = n_seq * t
    t0 = seg.row0 // rows
    const = lambda i: (0, 0)
    x_spec = pl.BlockSpec((rows, d), lambda i: (t0 + i, 0))
    in_specs = [x_spec, _mod_spec(layer, seg, d, t, tile_of=lambda i: i * n_seq), pl.BlockSpec((1, d), const),
                pl.BlockSpec(cs.shape, const), pl.BlockSpec((t, t), const), pl.BlockSpec((t, t), const),
                pl.BlockSpec(w.shape, const)]
    args = [x, mods, g1, cs, ct, st, w]
    if pending is not None:
        yg_by_seg, wcol, pmods, player = pending
        in_specs = [pl.BlockSpec((2, rows, d // 2), lambda i: (0, i, 0)),
                    pl.BlockSpec((rows, 8), lambda i: (t0 + i, 0)),
                    _mod_spec(player, seg, d, t, tile_of=lambda i: i * n_seq)] + in_specs
        args = [yg_by_seg[seg], wcol, pmods] + args
    return pl.pallas_call(
        functools.partial(_fnet_short_kernel, gd=d // FNET_GROUPS, norm=norm, n_seq=n_seq,
                          pending=pending is not None),
        grid=(seg.rows // rows,),
        in_specs=in_specs,
        out_specs=x_spec,
        out_shape=jax.ShapeDtypeStruct((n, d), F32),
        scratch_shapes=[pltpu.VMEM((rows, d), BF16), pltpu.VMEM((rows, d), BF16)],
        input_output_aliases={len(args) - 7: 0},
        compiler_params=_cparams("parallel"),
        name="fnet_short",
    )(*args)


def _pack_halves(a):
    w = a.shape[1] // 2
    bits = lambda v: lax.bitcast_convert_type(v.astype(BF16).astype(F32), jnp.uint32)
    return (bits(a[:, :w]) >> 16) | (bits(a[:, w:]) & jnp.uint32(0xFFFF0000))


def _router_kernel(x_ref, m_ref, g_ref, rwhi_ref, rwlo_ref, rb_ref, h_ref, idx_ref, rank_ref, wcol_ref, cnt_ref,
                   run_ref, tri_ref):
    step = pl.program_id(0)

    @pl.when(step == 0)
    def _():
        run_ref[...] = jnp.zeros_like(run_ref)
        tt = tri_ref.shape[0]
        earlier = lax.broadcasted_iota(jnp.int32, (tt, tt), 0) < lax.broadcasted_iota(jnp.int32, (tt, tt), 1)
        tri_ref[...] = jnp.where(earlier, 1.0, 0.0).astype(BF16)

    parts = []
    for r0 in range(0, x_ref.shape[0], ROUTER_SUB_ROWS):
        rows = slice(r0, r0 + ROUTER_SUB_ROWS)
        h = _modulate(x_ref[rows, :], g_ref[...], m_ref[3], m_ref[4])
        h_ref[rows, :] = _pack_halves(h)
        h_hi = h.astype(BF16)
        h_lo = (h - h_hi.astype(F32)).astype(BF16)
        parts.append(_dot_nt(rwhi_ref[...], h_hi) + (_dot_nt(rwhi_ref[...], h_lo) + _dot_nt(rwlo_ref[...], h_hi)))
    logits = jnp.concatenate(parts, axis=1)
    sc = _sigmoid(logits)
    gr = sc + rb_ref[...]
    gp = EXPERTS_PER_GROUP
    row = lambda a, e: a[e:e + 1, :]
    best_g = None
    for g in range(N_EXPERT_GROUPS):
        vals = [row(gr, g * gp + i) for i in range(gp)]
        gs = None
        for i in range(gp):
            for j in range(i + 1, gp):
                pair = vals[i] + vals[j]
                gs = pair if gs is None else jnp.maximum(gs, pair)
        if best_g is None:
            best_g, best_v = jnp.zeros(gs.shape, jnp.int32), gs
        else:
            better = gs > best_v
            best_g = jnp.where(better, g, best_g)
            best_v = jnp.where(better, gs, best_v)
    sel, raw = [], []
    for i in range(gp):
        s_i, r_i = row(gr, i), row(sc, i)
        for g in range(1, N_EXPERT_GROUPS):
            s_i = jnp.where(best_g == g, row(gr, g * gp + i), s_i)
            r_i = jnp.where(best_g == g, row(sc, g * gp + i), r_i)
        sel.append(s_i)
        raw.append(r_i)

    def argmax_first(vals, raws):
        bi, bv, br = jnp.zeros(vals[0].shape, jnp.int32), vals[0], raws[0]
        for i in range(1, len(vals)):
            better = vals[i] > bv
            bi = jnp.where(better, i, bi)
            bv = jnp.where(better, vals[i], bv)
            br = jnp.where(better, raws[i], br)
        return bi, br

    i1, w1 = argmax_first(sel, raw)
    masked = [jnp.where(i1 == i, -jnp.inf, sel[i]) for i in range(gp)]
    i2, w2 = argmax_first(masked, raw)
    tot = w1 + w2
    e1 = best_g * gp + i1
    e2 = best_g * gp + i2
    idx_ref[0:1, :] = e1
    idx_ref[1:2, :] = e2
    t = e1.shape[1]
    sub = lax.broadcasted_iota(jnp.int32, (8, t), 0)
    w8 = jnp.where(sub == 0, w1 / tot, jnp.where(sub == 1, w2 / tot, 0.0))
    wcol_ref[...] = w8.T
    eio = lax.broadcasted_iota(jnp.int32, (N_EXPERTS, t), 0)
    oh1, oh2 = eio == e1, eio == e2
    oh = jnp.where(oh1, 1.0, jnp.where(oh2, 1.0, 0.0))
    local = _dot(oh.astype(BF16), tri_ref[...])
    rank = local + run_ref[:, 0:1]
    rank_ref[0:1, :] = jnp.sum(jnp.where(oh1, rank, 0.0), axis=0, keepdims=True).astype(jnp.int32)
    rank_ref[1:2, :] = jnp.sum(jnp.where(oh2, rank, 0.0), axis=0, keepdims=True).astype(jnp.int32)
    run_ref[...] = run_ref[...] + jnp.sum(oh, axis=1, keepdims=True)
    cnt_ref[...] = run_ref[...]


def _wide_mod_row(segs, tm):
    ctx, lat = segs
    ctx_tiles = ctx.batch * ctx.seq // tm
    assert ctx_tiles * tm == ctx.batch * ctx.seq and lat.seq % tm == 0
    return lambda i: jnp.where(i < ctx_tiles, ctx.mod0, lat.mod0 + (i - ctx_tiles) // (lat.seq // tm))


def _router(x, mods, layer, segs, g2, rw_hi, rw_lo, rb):
    n, d = x.shape
    tm = WIDE_TILE
    mod_row = _wide_mod_row(segs, tm)
    return pl.pallas_call(
        _router_kernel,
        grid=(n // tm,),
        in_specs=[
            pl.BlockSpec((tm, d), lambda i: (i, 0)),
            pl.BlockSpec((None, None, 6, 1, d), lambda i: (layer, mod_row(i), 0, 0, 0)),
            pl.BlockSpec((1, d), lambda i: (0, 0)),
            pl.BlockSpec(rw_hi.shape, lambda i: (0, 0)),
            pl.BlockSpec(rw_lo.shape, lambda i: (0, 0)),
            pl.BlockSpec(rb.shape, lambda i: (0, 0)),
        ],
        out_specs=[
            pl.BlockSpec((tm, d // 2), lambda i: (i, 0)),
            pl.BlockSpec((2, tm), lambda i: (0, i)),
            pl.BlockSpec((2, tm), lambda i: (0, i)),
            pl.BlockSpec((tm, 8), lambda i: (i, 0)),
            pl.BlockSpec((N_EXPERTS, LANE), lambda i: (0, 0)),
        ],
        out_shape=[
            jax.ShapeDtypeStruct((n, d // 2), jnp.uint32),
            jax.ShapeDtypeStruct((2, n), jnp.int32),
            jax.ShapeDtypeStruct((2, n), jnp.int32),
            jax.ShapeDtypeStruct((n, 8), F32),
            jax.ShapeDtypeStruct((N_EXPERTS, LANE), F32),
        ],
        scratch_shapes=[pltpu.VMEM((N_EXPERTS, LANE), F32), pltpu.VMEM((tm, tm), BF16)],
        compiler_params=_cparams("arbitrary"),
        name="moe_router",
    )(x, mods, g2, rw_hi, rw_lo, rb)


def _expert_kernel(be_ref, br_ref, bs_ref, xs_ref, wg_ref, wu_ref, wd_ref, y_ref, wg_b, wu_b, wd_b):
    i = pl.program_id(0)
    prev = be_ref[jnp.maximum(i - 1, 0)]
    valid = br_ref[i]
    tail_rows = MOE_TAIL_ROWS

    @pl.when(jnp.logical_or(i == 0, be_ref[i] != prev))
    def _():
        wg_b[...] = wg_ref[...].astype(BF16)
        wu_b[...] = wu_ref[...].astype(BF16)
        wd_b[...] = wd_ref[...].astype(BF16)

    def ffn(r0, nrows):
        lo, hi = _unpack_halves(xs_ref[r0:r0 + nrows, :])
        xb = jnp.concatenate([lo.astype(BF16), hi.astype(BF16)], axis=1)
        gate = _dot(xb, wg_b[...])
        hid = (gate * _sigmoid(gate)) * _dot(xb, wu_b[...])
        y_ref[r0:r0 + nrows, :] = _pack_halves(_dot(hid.astype(BF16), wd_b[...]))

    @pl.when(valid > tail_rows)
    def _():
        ffn(0, y_ref.shape[0])

    @pl.when(jnp.logical_and(valid > 0, valid <= tail_rows))
    def _():
        ffn(0, tail_rows)
        y_ref[tail_rows:, :] = jnp.zeros((y_ref.shape[0] - tail_rows, y_ref.shape[1]), y_ref.dtype)


def _experts(xs, block_e, block_rows, block_src, w_gate, w_up, w_down, layer):
    rows, half = xs.shape
    d = 2 * half
    de = w_gate.shape[-1]
    n_blocks = rows // MOE_ROWS
    grid_spec = pltpu.PrefetchScalarGridSpec(
        num_scalar_prefetch=3,
        grid=(n_blocks,),
        in_specs=[
            pl.BlockSpec((MOE_ROWS, half), lambda i, be, br, bs: (bs[i], 0)),
            pl.BlockSpec((None, None, d, de), lambda i, be, br, bs: (layer, be[i], 0, 0)),
            pl.BlockSpec((None, None, d, de), lambda i, be, br, bs: (layer, be[i], 0, 0)),
            pl.BlockSpec((None, None, de, d), lambda i, be, br, bs: (layer, be[i], 0, 0)),
        ],
        out_specs=pl.BlockSpec((MOE_ROWS, half), lambda i, be, br, bs: (bs[i], 0)),
        scratch_shapes=[pltpu.VMEM((d, de), BF16), pltpu.VMEM((d, de), BF16), pltpu.VMEM((de, d), BF16)],
    )
    return pl.pallas_call(
        _expert_kernel,
        grid_spec=grid_spec,
        out_shape=jax.ShapeDtypeStruct((rows, half), jnp.uint32),
        compiler_params=_cparams("arbitrary"),
        name="moe_experts",
    )(block_e, block_rows, block_src, xs, w_gate, w_up, w_down)


def _combine_kernel(x_ref, m_ref, y_ref, w_ref, fg_ref, o_ref):
    x = x_ref[...] + _moe_residual(y_ref, w_ref, m_ref[5])
    o_ref[...] = _rms(x) * fg_ref[...]


def _combine(x, mods, layer, segs, seg, yg, wcol, final_g):
    n, d = x.shape
    tm = WIDE_TILE
    mod_row = _wide_mod_row(segs, tm)
    t0 = seg.row0 // tm
    steps = seg.rows // tm
    in_specs = [pl.BlockSpec((tm, d), lambda i: (t0 + i, 0)),
                pl.BlockSpec((None, None, 6, 1, d), lambda i: (layer, mod_row(t0 + i), 0, 0, 0)),
                pl.BlockSpec((2, tm, d // 2), lambda i: (0, i, 0)),
                pl.BlockSpec((tm, 8), lambda i: (t0 + i, 0)),
                pl.BlockSpec((1, d), lambda i: (0, 0))]
    return pl.pallas_call(
        _combine_kernel,
        grid=(steps,),
        in_specs=in_specs,
        out_specs=pl.BlockSpec((tm, d), lambda i: (i, 0)),
        out_shape=jax.ShapeDtypeStruct((seg.rows, d), F32),
        compiler_params=_cparams("parallel"),
        name="moe_combine_final",
    )(x, mods, yg, wcol, final_g)


def _sc_mesh():
    return plsc.VectorSubcoreMesh(core_axis_name="c", subcore_axis_name="s")


def _sc_worker_split(n):
    workers = SC_CORES * SC_SUBCORES
    per = n // workers
    assert per * workers == n and per % SC_CHUNK == 0
    return workers, per, per // SC_CHUNK


def _sc_dispatch(h, pos, rows):
    n, w = h.shape
    workers, per, chunks = _sc_worker_split(n)

    @functools.partial(
        pl.kernel, out_type=jax.ShapeDtypeStruct((rows, w), h.dtype), mesh=_sc_mesh(),
        scratch_types=[pltpu.VMEM((2, chunks, SC_CHUNK), jnp.int32), pltpu.VMEM((SC_CHUNK, w), h.dtype)],
        name="moe_dispatch_scatter")
    def scatter_rows(h_hbm, pos_hbm, xs_hbm, idx_v, rows_v):
        wid = lax.axis_index("s") * SC_CORES + lax.axis_index("c")
        pltpu.sync_copy(pos_hbm.at[0, wid], idx_v.at[0])
        pltpu.sync_copy(pos_hbm.at[1, wid], idx_v.at[1])

        @pl.loop(0, chunks)
        def _(c):
            pltpu.sync_copy(h_hbm.at[pl.ds(wid * per + c * SC_CHUNK, SC_CHUNK)], rows_v)
            pltpu.sync_copy(rows_v, xs_hbm.at[idx_v.at[0, c]])
            pltpu.sync_copy(rows_v, xs_hbm.at[idx_v.at[1, c]])

    return scatter_rows(h, pos.reshape(2, workers, chunks, SC_CHUNK))


def _sc_gather2(ys, pos):
    _, w = ys.shape
    n = pos.shape[1]
    workers, per, chunks = _sc_worker_split(n)

    @functools.partial(
        pl.kernel, out_type=jax.ShapeDtypeStruct((2, n, w), ys.dtype), mesh=_sc_mesh(),
        scratch_types=[pltpu.VMEM((2, chunks, SC_CHUNK), jnp.int32), pltpu.VMEM((SC_CHUNK, w), ys.dtype),
                       pltpu.SemaphoreType.DMA],
        name="moe_combine_gather")
    def gather_rows(ys_hbm, pos_hbm, out_hbm, idx_v, rows_v, sem):
        wid = lax.axis_index("s") * SC_CORES + lax.axis_index("c")
        pltpu.sync_copy(pos_hbm.at[0, wid], idx_v.at[0])
        pltpu.sync_copy(pos_hbm.at[1, wid], idx_v.at[1])

        @pl.loop(0, chunks)
        def _(c):
            for k in range(2):
                pltpu.async_copy(ys_hbm.at[idx_v.at[k, c]], rows_v, sem).wait()
                pltpu.sync_copy(rows_v, out_hbm.at[k, pl.ds(wid * per + c * SC_CHUNK, SC_CHUNK)])

    return gather_rows(ys, pos.reshape(2, workers, chunks, SC_CHUNK))


def _dispatch_plan(idx, rank, counts):
    n = idx.shape[1]
    padded = (counts + MOE_ROWS - 1) // MOE_ROWS * MOE_ROWS
    pad_end = jnp.cumsum(padded)
    pad_start = pad_end - padded
    experts = jnp.arange(N_EXPERTS, dtype=jnp.int32)
    start_of = jnp.sum(jnp.where(idx[..., None] == experts, pad_start, 0), axis=-1)
    pos = start_of + rank
    n_blocks = 2 * n // MOE_ROWS + N_EXPERTS
    steps = jnp.arange(n_blocks, dtype=jnp.int32)
    last_used = pad_end[-1] // MOE_ROWS - 1
    step = jnp.minimum(steps, last_used)
    block_e = jnp.minimum(jnp.sum(step[:, None] * MOE_ROWS >= pad_end[None, :], axis=1), N_EXPERTS - 1)
    pick = lambda per_expert: jnp.sum(jnp.where(block_e[:, None] == experts, per_expert, 0), axis=-1)
    first, count = pick(pad_start // MOE_ROWS), jnp.maximum(pick(padded // MOE_ROWS), 1)
    block_src = first + (step - first - 1) % count
    block_rows = jnp.clip(pick(pad_start + counts) - block_src * MOE_ROWS, 0, MOE_ROWS)
    block_rows = jnp.where(steps <= last_used, block_rows, 0)
    return (pos, block_e.astype(jnp.int32), block_rows.astype(jnp.int32), block_src.astype(jnp.int32),
            n_blocks * MOE_ROWS)


def _moe(x, mods, layer, segs, g2, rw_hi, rw_lo, rb, w_gate, w_up, w_down, final_g, tie=None):
    h2p, idx, rank, wcol, cnt = _router(x, mods, layer, segs, g2, rw_hi, rw_lo, rb)
    pos, block_e, block_rows, block_src, rows = _dispatch_plan(idx, rank, cnt[:, 0].astype(jnp.int32))
    if tie is not None:
        pos, tie = lax.optimization_barrier((pos, tie))
    xs = _sc_dispatch(h2p, pos, rows)
    ys = _experts(xs, block_e, block_rows, block_src, w_gate, w_up, w_down, layer)
    yg = {seg: _sc_gather2(ys, pos[:, seg.row0:seg.row0 + seg.rows]) for seg in segs}
    if final_g is None:
        return (yg, wcol, mods, layer), tie
    return tuple(_combine(x, mods, layer, segs, seg, yg[seg], wcol, final_g) for seg in segs)


def _mla_rope_tables(t):
    axis_dim = MLA_ROPE // 2
    row = np.repeat(np.arange(t // GRID_W), GRID_W).astype(np.float64)
    col = np.tile(np.arange(GRID_W), t // GRID_W).astype(np.float64)
    inv = ROPE_BASE ** (-np.arange(0, axis_dim, 2, dtype=np.float64) / axis_dim)
    ar, ac = row[:, None] * inv[None, :], col[:, None] * inv[None, :]
    ones = np.ones((t, LANE - MLA_ROPE))
    cos = np.concatenate([np.cos(ar), np.cos(ar), np.cos(ac), np.cos(ac), ones], axis=-1)
    sin = np.concatenate([-np.sin(ar), np.sin(ar), -np.sin(ac), np.sin(ac), 0.0 * ones], axis=-1)
    return jnp.asarray(cos, F32), jnp.asarray(sin, F32)


def _ret_rot_tables(t, dk):
    inv = ROPE_BASE ** (-np.linspace(0.0, 1.0, dk // 2))
    ang = np.arange(t, dtype=np.float64)[:, None] * inv[None, :]
    return jnp.asarray(np.cos(ang), F32), jnp.asarray(np.sin(ang), F32)


def _dft_tables(n):
    k = np.arange(n, dtype=np.int64)
    ang = (np.outer(k, k) % n).astype(np.float64) * (2.0 * math.pi / n)
    return jnp.asarray(np.cos(ang), BF16), jnp.asarray(np.sin(ang), BF16)


def _mla_weights(w_in, q_g, kv_g, w_uq, w_ukv, w_o):
    d = w_in.shape[0]
    hd = MLA_NOPE + MLA_ROPE
    perm = _rope_perm()
    w_in_p = jnp.concatenate([w_in, w_in[:, MLA_Q_LORA + MLA_KV_LORA + perm]], axis=1)
    uq = w_uq.reshape(MLA_Q_LORA, MLA_HEADS, hd)
    uq = jnp.concatenate([uq, uq[..., MLA_NOPE + perm]], axis=-1)
    ukv = w_ukv.reshape(MLA_KV_LORA, MLA_HEADS, MLA_NOPE + MLA_V)
    return {
        "w_in": w_in_p.astype(BF16),
        "q_g": q_g.reshape(1, -1) * (MLA_NOPE + MLA_ROPE) ** -0.5,
        "kv_g": kv_g.reshape(1, -1),
        "w_q": uq.reshape(MLA_Q_LORA, MLA_HEADS * 2 * LANE).astype(BF16),
        "w_kn": ukv[..., :MLA_NOPE].reshape(MLA_KV_LORA, MLA_HEADS * MLA_NOPE).astype(BF16),
        "w_v": ukv[..., MLA_NOPE:].reshape(MLA_KV_LORA, MLA_HEADS * MLA_V).astype(BF16),
        "w_o": w_o.astype(BF16),
    }


def kernel(x_prompt, x_sample, cache_mla, state_ret, c, c_ctx, norm1_g, norm2_g, ada_w, ada_b, final_norm_g,
           mla_w_in, mla_q_norm_g, mla_kv_norm_g, mla_w_uq, mla_w_ukv, mla_w_o, ret_w_in, ret_decay_f,
           ret_decay_b, ret_w_o, fnet_w, router_w, router_b, moe_w_gate, moe_w_up, moe_w_down):
    b_ctx, t_ctx, d = x_prompt.shape
    b_lat, t_lat, _ = x_sample.shape
    depth = ada_w.shape[0]
    assert b_lat + 1 <= 8
    n_ctx = b_ctx * t_ctx
    ctx = _Seg(0, b_ctx, t_ctx, 0, False)
    lat = _Seg(n_ctx, b_lat, t_lat, 1, True)
    segs = (ctx, lat)

    n_lat = b_lat * t_lat
    n_mla = mla_w_in.shape[0]
    assert n_mla >= 1
    x = None
    new_cache = jnp.zeros((b_ctx, n_mla, t_ctx, MLA_KV_LORA + MLA_ROPE), F32)
    cond8 = jnp.concatenate([c_ctx[None, :], c, jnp.zeros((8 - 1 - b_lat, d), F32)], axis=0)
    mods = _modulation_all(cond8, ada_w, ada_b).reshape(depth, 8, 6, 1, d)

    rw_t = router_w.T.astype(F32)
    rw_hi = rw_t.astype(BF16)
    rw_lo = (rw_t - rw_hi.astype(F32)).astype(BF16)
    rb = router_b.reshape(N_EXPERTS, 1).astype(F32)
    final_g = final_norm_g.reshape(1, d)
    dk = ret_w_in.shape[2] // (8 * RET_HEADS)
    dv = 2 * dk

    states = []
    pending = None
    def mixer_params(layer):
        kind, j = layer % 3, layer // 3
        if kind == 0:
            return dict(w_in=mla_w_in[j], q_g=mla_q_norm_g[j], kv_g=mla_kv_norm_g[j], w_uq=mla_w_uq[j],
                        w_ukv=mla_w_ukv[j], w_o=mla_w_o[j], cache=cache_mla[:, j])
        if kind == 1:
            return dict(w_in=ret_w_in[j], w_o=ret_w_o[j])
        return dict(w=fnet_w[j])

    p = mixer_params(0)
    for layer in range(depth):
        kind, j = layer % 3, layer // 3
        g1 = norm1_g[layer].reshape(1, d)
        g2 = norm2_g[layer].reshape(1, d)
        if kind == 0:
            w = _mla_weights(p["w_in"], p["q_g"], p["kv_g"], p["w_uq"], p["w_ukv"], p["w_o"])
            if x is None:
                xc, xc0, xl, xl0 = x_prompt.reshape(n_ctx, d), 0, x_sample.reshape(n_lat, d), 0
            else:
                xc, xc0, xl, xl0 = x, ctx.row0, x, lat.row0
            past = cache_mla.shape[2]
            cpad = jnp.pad(p["cache"].reshape(b_lat * past, -1), ((0, 0), (0, LANE - MLA_ROPE)))
            kp, vp = _cache_kv(cpad, w)
            if pending is None:
                qc, kc, vc, new_cache = _mla_proj(xc, xc0, mods, layer, ctx, g1, w, None, (new_cache, j, n_mla))
                ql, kl, vl = _mla_proj(xl, xl0, mods, layer, lat, g1, w, _mla_rope_tables(t_lat))
            else:
                qc, kc, vc, new_cache, x = _mla_proj(x, ctx.row0, mods, layer, ctx, g1, w, None,
                                                     (new_cache, j, n_mla), pending)
                ql, kl, vl, x = _mla_proj(x, lat.row0, mods, layer, lat, g1, w, _mla_rope_tables(t_lat), None,
                                          pending)
                xc = xl = x
            first = x is None
            x = _attention(xc, xc0, n_ctx + n_lat, jnp.zeros((n_ctx + n_lat, d), F32) if first else "inplace",
                           mods, layer, ctx, qc, [(kc, vc, t_ctx)], w["w_o"])
            x = _attention(xl if first else x, xl0, n_ctx + n_lat, x if first else "inplace", mods, layer, lat,
                           ql, [(kl, vl, t_lat), (kp, vp, past)], w["w_o"])
        elif kind == 1:
            w_in = p["w_in"]
            qk = RET_HEADS * dk
            k_scale = jnp.concatenate([jnp.ones((qk,), F32), jnp.full((qk,), dk ** -0.5, F32),
                                       jnp.ones((w_in.shape[1] - 2 * qk,), F32)])
            w_in_b = (w_in * k_scale[None, :]).astype(BF16)
            w_o_b = p["w_o"].astype(BF16)
            rot = _ret_rot_tables(t_lat, dk)
            n_qk, n_v = 2 * qk // RET_COL, RET_HEADS * dv // RET_COL
            assert (n_qk + n_v) * RET_COL * 2 == w_in.shape[1]
            parts = []
            for seg in segs:
                kinds = (("rotary" if seg is lat else "plain"),) * n_qk + ("plain",) * n_v
                qkv = _ret_proj(x, mods, layer, seg, g1, w_in_b, 0, kinds, rot, dk, pending)
                if pending is not None:
                    qkv, x = qkv
                parts.append((qkv, _ret_proj(x, mods, layer, seg, g1, w_in_b, 1, ("silu",) * (n_qk + n_v), None, dk)))
            x, s_ctx = _ret_scan(*parts[0], ctx, ret_decay_f[j], ret_decay_b[j], None, True, dk, dv, RET_HEADS,
                                 out_proj=(x, mods, layer, w_o_b))
            yl, _ = _ret_scan(*parts[1], lat, ret_decay_f[j], ret_decay_b[j], state_ret[:, j], False, dk, dv, 1)
            x = _matmul_residual(x, mods, layer, lat, yl, w_o_b)
            states.append(s_ctx)
        else:
            gd = d // FNET_GROUPS
            cc, sc = _dft_tables(gd)
            cs = jnp.concatenate([cc, sc], axis=1)
            w_b = p["w"].astype(BF16)
            for seg in segs:
                ct, st = _dft_tables(seg.seq)
                if seg.seq <= FNET_ROWS:
                    x = _fnet_short(x, mods, layer, seg, g1, cs, ct, st, w_b, (seg.seq * gd) ** -0.5, pending)
                    continue
                if pending is None:
                    ac, as_ = _fnet_a(x, mods, layer, seg, g1, cs)
                else:
                    ac, as_, x = _fnet_a(x, mods, layer, seg, g1, cs, pending)
                x = _fnet_b(x, mods, layer, seg, ac, as_, ct, st, w_b, (seg.seq * gd) ** -0.5)
        if layer < depth - 1:
            pending, p = _moe(x, mods, layer, segs, g2, rw_hi, rw_lo, rb, moe_w_gate, moe_w_up, moe_w_down, None,
                              tie=mixer_params(layer + 1))
        else:
            y_prompt, y_sample = _moe(x, mods, layer, segs, g2, rw_hi, rw_lo, rb, moe_w_gate, moe_w_up,
                                      moe_w_down, final_g)

    new_state = jnp.stack(states, axis=1)
    return (y_prompt.reshape(b_ctx, t_ctx, d), y_sample.reshape(b_lat, t_lat, d), new_cache, new_state)
```
